```python
import jax, jax.numpy as jnp
from jax import lax
import numpy as np

D_MODEL = 1024
BATCH = 8
SEQ = 8192
DEPTH = 1

CHUNK = 64
N_LEFT_CHUNKS = 8
BAND = (N_LEFT_CHUNKS + 1) * CHUNK
MIX_WIDTH = D_MODEL
CONV_WIDTH = MIX_WIDTH // 2
ATTN_WIDTH = MIX_WIDTH - CONV_WIDTH
HEAD_DIM = 64
N_HEADS = ATTN_WIDTH // HEAD_DIM
CONV_KERNEL = 31
MAX_REL = 128
D_FF = 2816
FFN_CONV_KERNEL = 3
IN_COLS = 2 * CONV_WIDTH + 3 * ATTN_WIDTH
EPS = 1e-6
NEG_INF = -1e30

kernel_name = "chunk_causal_conformer_hybrid_block"


def rms_norm(x, g):
    xf = x.astype(jnp.float32)
    y = xf * lax.rsqrt(jnp.mean(xf * xf, axis=-1, keepdims=True) + EPS)
    return (y * g.astype(jnp.float32)).astype(x.dtype)


def layer_norm(x, g, b):
    xf = x.astype(jnp.float32)
    mu = jnp.mean(xf, axis=-1, keepdims=True)
    xc = xf - mu
    var = jnp.mean(xc * xc, axis=-1, keepdims=True)
    y = xc * lax.rsqrt(var + EPS) * g.astype(jnp.float32) + b.astype(jnp.float32)
    return y.astype(x.dtype)


def causal_depthwise_conv(x, w, b):
    k = w.shape[0]
    c = x.shape[-1]
    y = lax.conv_general_dilated(
        x, w[:, None, :].astype(x.dtype), window_strides=(1,), padding=[(k - 1, 0)],
        dimension_numbers=("NWC", "WIO", "NWC"), feature_group_count=c)
    return y + b.astype(x.dtype)


def conformer_conv_group(a_val, a_gate, dw_w, dw_b, ln_g, ln_b):
    h = a_val * jax.nn.sigmoid(a_gate)
    h = causal_depthwise_conv(h, dw_w, dw_b)
    h = layer_norm(h, ln_g, ln_b)
    return jax.nn.silu(h)


def rel_bias_band(rel_table):
    i = np.arange(CHUNK)[:, None]
    j = np.arange(BAND)[None, :]
    rel = N_LEFT_CHUNKS * CHUNK + i - j
    idx = np.clip(rel, -MAX_REL, MAX_REL) + MAX_REL
    return rel_table[:, idx]


def chunked_band_attention(q, k, v, rel_table):
    b, t, _ = q.shape
    nc = t // CHUNK
    pad = N_LEFT_CHUNKS * CHUNK
    q = q.reshape(b, nc, CHUNK, N_HEADS, HEAD_DIM)

    def band(z):
        z = z.reshape(b, t, N_HEADS, HEAD_DIM)
        z = jnp.pad(z, ((0, 0), (pad, 0), (0, 0), (0, 0)))
        zc = z.reshape(b, nc + N_LEFT_CHUNKS, CHUNK, N_HEADS, HEAD_DIM)
        return jnp.concatenate([zc[:, s:s + nc] for s in range(N_LEFT_CHUNKS + 1)], axis=2)

    kb = band(k)
    vb = band(v)
    scale = HEAD_DIM ** -0.5
    scores = jnp.einsum("bnqhd,bnkhd->bhnqk", q, kb,
                        preferred_element_type=jnp.float32) * scale
    scores = scores + rel_bias_band(rel_table).astype(jnp.float32)[None, :, None]
    key_pos = (jnp.arange(nc)[:, None] - N_LEFT_CHUNKS) * CHUNK + jnp.arange(BAND)[None, :]
    valid = (key_pos >= 0)[None, None, :, None, :]
    scores = jnp.where(valid, scores, NEG_INF)
    p = jax.nn.softmax(scores, axis=-1)
    out = jnp.einsum("bhnqk,bnkhd->bnqhd", p.astype(vb.dtype), vb)
    return out.reshape(b, t, ATTN_WIDTH)


def conv_gated_ffn(u, w_up, dw_w, dw_b, w_down):
    h = u @ w_up
    h = causal_depthwise_conv(h, dw_w, dw_b)
    gate, val = jnp.split(h, 2, axis=-1)
    return (jax.nn.gelu(gate) * val) @ w_down


def _fwd_setup_inputs(seed: int = 0) -> dict:
    key = jax.random.key(seed)
    ks = jax.random.split(key, 20)
    L = DEPTH
    nrm = jax.random.normal
    f32 = jnp.float32
    return {
        "x": nrm(ks[0], (BATCH, SEQ, D_MODEL), f32),
        "norm_mix_pre": 1.0 + 0.05 * nrm(ks[1], (L, D_MODEL), f32),
        "w_in": nrm(ks[2], (L, D_MODEL, IN_COLS), f32) * D_MODEL ** -0.5,
        "conv_dw_w": nrm(ks[3], (L, CONV_KERNEL, CONV_WIDTH), f32) * CONV_KERNEL ** -0.5,
        "conv_dw_b": 0.01 * nrm(ks[4], (L, CONV_WIDTH), f32),
        "conv_ln_g": 1.0 + 0.05 * nrm(ks[5], (L, CONV_WIDTH), f32),
        "conv_ln_b": 0.01 * nrm(ks[6], (L, CONV_WIDTH), f32),
        "rel_bias": 0.5 * nrm(ks[7], (L, N_HEADS, 2 * MAX_REL + 1), f32),
        "w_out": nrm(ks[8], (L, MIX_WIDTH, D_MODEL), f32) * MIX_WIDTH ** -0.5,
        "norm_mix_post": 1.0 + 0.05 * nrm(ks[9], (L, D_MODEL), f32),
        "norm_ffn_pre": 1.0 + 0.05 * nrm(ks[10], (L, D_MODEL), f32),
        "w_up": nrm(ks[11], (L, D_MODEL, 2 * D_FF), f32) * D_MODEL ** -0.5,
        "ffn_dw_w": nrm(ks[12], (L, FFN_CONV_KERNEL, 2 * D_FF), f32) * FFN_CONV_KERNEL ** -0.5,
        "ffn_dw_b": 0.01 * nrm(ks[13], (L, 2 * D_FF), f32),
        "w_down": nrm(ks[14], (L, D_FF, D_MODEL), f32) * D_FF ** -0.5,
        "norm_ffn_post": 1.0 + 0.05 * nrm(ks[15], (L, D_MODEL), f32),
    }


def _fwd_reference(x, norm_mix_pre, w_in, conv_dw_w, conv_dw_b, conv_ln_g, conv_ln_b,
              rel_bias, w_out, norm_mix_post, norm_ffn_pre, w_up, ffn_dw_w, ffn_dw_b,
              w_down, norm_ffn_post):
    h = x
    for l in range(DEPTH):
        u = rms_norm(h, norm_mix_pre[l])
        proj = u @ w_in[l]
        a_val, a_gate, q, k, v = jnp.split(
            proj, np.cumsum([CONV_WIDTH, CONV_WIDTH, ATTN_WIDTH, ATTN_WIDTH]).tolist(), axis=-1)
        conv_out = conformer_conv_group(a_val, a_gate, conv_dw_w[l], conv_dw_b[l],
                                        conv_ln_g[l], conv_ln_b[l])
        attn_out = chunked_band_attention(q, k, v, rel_bias[l])
        mixed = jnp.concatenate([conv_out, attn_out], axis=-1) @ w_out[l]
        h = h + rms_norm(mixed, norm_mix_post[l])
        u = rms_norm(h, norm_ffn_pre[l])
        f = conv_gated_ffn(u, w_up[l], ffn_dw_w[l], ffn_dw_b[l], w_down[l])
        h = h + rms_norm(f, norm_ffn_post[l])
    return h


import jax as _jax
import jax.numpy as _jnp

TWIN_FORMAT = 'train_step'
FWD_PARAMS = ['x', 'norm_mix_pre', 'w_in', 'conv_dw_w', 'conv_dw_b', 'conv_ln_g', 'conv_ln_b', 'rel_bias', 'w_out', 'norm_mix_post', 'norm_ffn_pre', 'w_up', 'ffn_dw_w', 'ffn_dw_b', 'w_down', 'norm_ffn_post']
TWIN_WEIGHTS = ['norm_mix_pre', 'w_in', 'conv_dw_w', 'conv_dw_b', 'conv_ln_g', 'conv_ln_b', 'rel_bias', 'w_out', 'norm_mix_post', 'norm_ffn_pre', 'w_up', 'ffn_dw_w', 'ffn_dw_b', 'w_down', 'norm_ffn_post']
TWIN_DIFF_INPUT = 'x'
TWIN_INPUTS = ['x', 'norm_mix_pre', 'w_in', 'conv_dw_w', 'conv_dw_b', 'conv_ln_g', 'conv_ln_b', 'rel_bias', 'w_out', 'norm_mix_post', 'norm_ffn_pre', 'w_up', 'ffn_dw_w', 'ffn_dw_b', 'w_down', 'norm_ffn_post', 'loss_target', 'm_norm_mix_pre', 'm_w_in', 'm_conv_dw_w', 'm_conv_dw_b', 'm_conv_ln_g', 'm_conv_ln_b', 'm_rel_bias', 'm_w_out', 'm_norm_mix_post', 'm_norm_ffn_pre', 'm_w_up', 'm_ffn_dw_w', 'm_ffn_dw_b', 'm_w_down', 'm_norm_ffn_post', 'v_norm_mix_pre', 'v_w_in', 'v_conv_dw_w', 'v_conv_dw_b', 'v_conv_ln_g', 'v_conv_ln_b', 'v_rel_bias', 'v_w_out', 'v_norm_mix_post', 'v_norm_ffn_pre', 'v_w_up', 'v_ffn_dw_w', 'v_ffn_dw_b', 'v_w_down', 'v_norm_ffn_post']
TWIN_OUTPUTS = ['loss', 'grad_x', 'grad_norm_mix_pre', 'grad_w_in', 'grad_conv_dw_w', 'grad_conv_dw_b', 'grad_conv_ln_g', 'grad_conv_ln_b', 'grad_rel_bias', 'grad_w_out', 'grad_norm_mix_post', 'grad_norm_ffn_pre', 'grad_w_up', 'grad_ffn_dw_w', 'grad_ffn_dw_b', 'grad_w_down', 'grad_norm_ffn_post', 'delta_norm_mix_pre', 'delta_w_in', 'delta_conv_dw_w', 'delta_conv_dw_b', 'delta_conv_ln_g', 'delta_conv_ln_b', 'delta_rel_bias', 'delta_w_out', 'delta_norm_mix_post', 'delta_norm_ffn_pre', 'delta_w_up', 'delta_ffn_dw_w', 'delta_ffn_dw_b', 'delta_w_down', 'delta_norm_ffn_post', 'new_m_norm_mix_pre', 'new_m_w_in', 'new_m_conv_dw_w', 'new_m_conv_dw_b', 'new_m_conv_ln_g', 'new_m_conv_ln_b', 'new_m_rel_bias', 'new_m_w_out', 'new_m_norm_mix_post', 'new_m_norm_ffn_pre', 'new_m_w_up', 'new_m_ffn_dw_w', 'new_m_ffn_dw_b', 'new_m_w_down', 'new_m_norm_ffn_post', 'new_v_norm_mix_pre', 'new_v_w_in', 'new_v_conv_dw_w', 'new_v_conv_dw_b', 'new_v_conv_ln_g', 'new_v_conv_ln_b', 'new_v_rel_bias', 'new_v_w_out', 'new_v_norm_mix_post', 'new_v_norm_ffn_pre', 'new_v_w_up', 'new_v_ffn_dw_w', 'new_v_ffn_dw_b', 'new_v_w_down', 'new_v_norm_ffn_post']
TWIN_LEAF_KINDS = {'loss': 'loss', 'grad_x': 'grad_x', 'grad_norm_mix_pre': 'grad_w', 'grad_w_in': 'grad_w', 'grad_conv_dw_w': 'grad_w', 'grad_conv_dw_b': 'grad_w', 'grad_conv_ln_g': 'grad_w', 'grad_conv_ln_b': 'grad_w', 'grad_rel_bias': 'grad_w', 'grad_w_out': 'grad_w', 'grad_norm_mix_post': 'grad_w', 'grad_norm_ffn_pre': 'grad_w', 'grad_w_up': 'grad_w', 'grad_ffn_dw_w': 'grad_w', 'grad_ffn_dw_b': 'grad_w', 'grad_w_down': 'grad_w', 'grad_norm_ffn_post': 'grad_w', 'delta_norm_mix_pre': 'delta_w', 'delta_w_in': 'delta_w', 'delta_conv_dw_w': 'delta_w', 'delta_conv_dw_b': 'delta_w', 'delta_conv_ln_g': 'delta_w', 'delta_conv_ln_b': 'delta_w', 'delta_rel_bias': 'delta_w', 'delta_w_out': 'delta_w', 'delta_norm_mix_post': 'delta_w', 'delta_norm_ffn_pre': 'delta_w', 'delta_w_up': 'delta_w', 'delta_ffn_dw_w': 'delta_w', 'delta_ffn_dw_b': 'delta_w', 'delta_w_down': 'delta_w', 'delta_norm_ffn_post': 'delta_w', 'new_m_norm_mix_pre': 'new_m', 'new_m_w_in': 'new_m', 'new_m_conv_dw_w': 'new_m', 'new_m_conv_dw_b': 'new_m', 'new_m_conv_ln_g': 'new_m', 'new_m_conv_ln_b': 'new_m', 'new_m_rel_bias': 'new_m', 'new_m_w_out': 'new_m', 'new_m_norm_mix_post': 'new_m', 'new_m_norm_ffn_pre': 'new_m', 'new_m_w_up': 'new_m', 'new_m_ffn_dw_w': 'new_m', 'new_m_ffn_dw_b': 'new_m', 'new_m_w_down': 'new_m', 'new_m_norm_ffn_post': 'new_m', 'new_v_norm_mix_pre': 'new_v', 'new_v_w_in': 'new_v', 'new_v_conv_dw_w': 'new_v', 'new_v_conv_dw_b': 'new_v', 'new_v_conv_ln_g': 'new_v', 'new_v_conv_ln_b': 'new_v', 'new_v_rel_bias': 'new_v', 'new_v_w_out': 'new_v', 'new_v_norm_mix_post': 'new_v', 'new_v_norm_ffn_pre': 'new_v', 'new_v_w_up': 'new_v', 'new_v_ffn_dw_w': 'new_v', 'new_v_ffn_dw_b': 'new_v', 'new_v_w_down': 'new_v', 'new_v_norm_ffn_post': 'new_v'}


def _forward(args):
    return _fwd_reference(*[args[k] for k in FWD_PARAMS])


def _output_shape():
    def fwd():
        inp = _fwd_setup_inputs(0)
        return _fwd_reference(*[inp[k] for k in FWD_PARAMS])
    out = _jax.eval_shape(fwd)
    return out.shape, out.dtype

N_MICROBATCH = 1
ADAM_LR = 0.001
ADAM_B1 = 0.9
ADAM_B2 = 0.999
ADAM_EPS = 1e-08
ADAM_WD = 0.01
ADAM_STEP = 10
PER_EXAMPLE_BATCH_AXIS = {'x': 0, 'loss_target': 0}
SHARED_INPUTS = []
_WEIGHT_DTYPES = {'norm_mix_pre': _jnp.float32, 'w_in': _jnp.float32, 'conv_dw_w': _jnp.float32, 'conv_dw_b': _jnp.float32, 'conv_ln_g': _jnp.float32, 'conv_ln_b': _jnp.float32, 'rel_bias': _jnp.float32, 'w_out': _jnp.float32, 'norm_mix_post': _jnp.float32, 'norm_ffn_pre': _jnp.float32, 'w_up': _jnp.float32, 'ffn_dw_w': _jnp.float32, 'ffn_dw_b': _jnp.float32, 'w_down': _jnp.float32, 'norm_ffn_post': _jnp.float32}
MOMENT_SCALE = {'norm_mix_pre': 7.626015e-01, 'w_in': 4.623750e-01, 'conv_dw_w': 1.396436e+00, 'conv_dw_b': 2.250283e+01, 'conv_ln_g': 8.912149e+00, 'conv_ln_b': 1.312897e+01, 'rel_bias': 1.047122e-01, 'w_out': 3.519784e+00, 'norm_mix_post': 6.508826e+01, 'norm_ffn_pre': 2.489028e+00, 'w_up': 1.056895e+00, 'ffn_dw_w': 1.323456e+00, 'ffn_dw_b': 5.063653e+00, 'w_down': 2.433597e+00, 'norm_ffn_post': 6.420749e+01}


def _to_microbatches(a, axis):
    t = _jnp.moveaxis(a, axis, 0)
    t = t.reshape((N_MICROBATCH, t.shape[0] // N_MICROBATCH) + t.shape[1:])
    return _jnp.moveaxis(t, 1, axis + 1)


def setup_inputs(seed: int = 0) -> dict:
    inp = _fwd_setup_inputs(seed)
    key = _jax.random.fold_in(_jax.random.key(seed), 7919)
    shape, _ = _output_shape()
    out = dict(inp)
    out["loss_target"] = _jax.random.normal(_jax.random.fold_in(key, 0), shape, _jnp.float32)
    for i, name in enumerate(TWIN_WEIGHTS):
        w = inp[name].astype(_jnp.float32)
        if MOMENT_SCALE is None:
            s = _jnp.sqrt(_jnp.mean(_jnp.square(w)) + 1e-30)
        else:
            s = MOMENT_SCALE[name]
        km, kv = _jax.random.split(_jax.random.fold_in(key, i + 1))
        out[name] = w
        out["m_" + name] = s * _jax.random.normal(km, w.shape, _jnp.float32)
        out["v_" + name] = (s * s) * _jax.random.uniform(kv, w.shape, _jnp.float32, 0.5, 1.5)
    if N_MICROBATCH > 1:
        for name, axis in PER_EXAMPLE_BATCH_AXIS.items():
            out[name] = _to_microbatches(out[name], axis)
    return {'x': out['x'], 'norm_mix_pre': out['norm_mix_pre'], 'w_in': out['w_in'], 'conv_dw_w': out['conv_dw_w'], 'conv_dw_b': out['conv_dw_b'], 'conv_ln_g': out['conv_ln_g'], 'conv_ln_b': out['conv_ln_b'], 'rel_bias': out['rel_bias'], 'w_out': out['w_out'], 'norm_mix_post': out['norm_mix_post'], 'norm_ffn_pre': out['norm_ffn_pre'], 'w_up': out['w_up'], 'ffn_dw_w': out['ffn_dw_w'], 'ffn_dw_b': out['ffn_dw_b'], 'w_down': out['w_down'], 'norm_ffn_post': out['norm_ffn_post'], 'loss_target': out['loss_target'], 'm_norm_mix_pre': out['m_norm_mix_pre'], 'm_w_in': out['m_w_in'], 'm_conv_dw_w': out['m_conv_dw_w'], 'm_conv_dw_b': out['m_conv_dw_b'], 'm_conv_ln_g': out['m_conv_ln_g'], 'm_conv_ln_b': out['m_conv_ln_b'], 'm_rel_bias': out['m_rel_bias'], 'm_w_out': out['m_w_out'], 'm_norm_mix_post': out['m_norm_mix_post'], 'm_norm_ffn_pre': out['m_norm_ffn_pre'], 'm_w_up': out['m_w_up'], 'm_ffn_dw_w': out['m_ffn_dw_w'], 'm_ffn_dw_b': out['m_ffn_dw_b'], 'm_w_down': out['m_w_down'], 'm_norm_ffn_post': out['m_norm_ffn_post'], 'v_norm_mix_pre': out['v_norm_mix_pre'], 'v_w_in': out['v_w_in'], 'v_conv_dw_w': out['v_conv_dw_w'], 'v_conv_dw_b': out['v_conv_dw_b'], 'v_conv_ln_g': out['v_conv_ln_g'], 'v_conv_ln_b': out['v_conv_ln_b'], 'v_rel_bias': out['v_rel_bias'], 'v_w_out': out['v_w_out'], 'v_norm_mix_post': out['v_norm_mix_post'], 'v_norm_ffn_pre': out['v_norm_ffn_pre'], 'v_w_up': out['v_w_up'], 'v_ffn_dw_w': out['v_ffn_dw_w'], 'v_ffn_dw_b': out['v_ffn_dw_b'], 'v_w_down': out['v_w_down'], 'v_norm_ffn_post': out['v_norm_ffn_post']}


def _loss(weights, diff, rest, loss_target):
    with _jax.named_scope("forward"):
        args = {**rest, TWIN_DIFF_INPUT: diff, **{k: w.astype(_WEIGHT_DTYPES[k]) for k, w in weights.items()}}
        y = _forward(args)
    with _jax.named_scope("loss_head"):
        err = _jnp.square(y.astype(_jnp.float32) - loss_target)
        return 0.5 * _jnp.sum(_jnp.mean(err, axis=-1)) if err.ndim else 0.5 * err


def _adamw(w, g, m, v):
    m = ADAM_B1 * m + (1.0 - ADAM_B1) * g
    v = ADAM_B2 * v + (1.0 - ADAM_B2) * _jnp.square(g)
    m_hat = m / (1.0 - ADAM_B1 ** ADAM_STEP)
    v_hat = v / (1.0 - ADAM_B2 ** ADAM_STEP)
    delta = -ADAM_LR * (m_hat / (_jnp.sqrt(v_hat) + ADAM_EPS) + ADAM_WD * w)
    return delta, m, v


def reference(x, norm_mix_pre, w_in, conv_dw_w, conv_dw_b, conv_ln_g, conv_ln_b, rel_bias, w_out, norm_mix_post, norm_ffn_pre, w_up, ffn_dw_w, ffn_dw_b, w_down, norm_ffn_post, loss_target, m_norm_mix_pre, m_w_in, m_conv_dw_w, m_conv_dw_b, m_conv_ln_g, m_conv_ln_b, m_rel_bias, m_w_out, m_norm_mix_post, m_norm_ffn_pre, m_w_up, m_ffn_dw_w, m_ffn_dw_b, m_w_down, m_norm_ffn_post, v_norm_mix_pre, v_w_in, v_conv_dw_w, v_conv_dw_b, v_conv_ln_g, v_conv_ln_b, v_rel_bias, v_w_out, v_norm_mix_post, v_norm_ffn_pre, v_w_up, v_ffn_dw_w, v_ffn_dw_b, v_w_down, v_norm_ffn_post):
    given = dict(x=x, norm_mix_pre=norm_mix_pre, w_in=w_in, conv_dw_w=conv_dw_w, conv_dw_b=conv_dw_b, conv_ln_g=conv_ln_g, conv_ln_b=conv_ln_b, rel_bias=rel_bias, w_out=w_out, norm_mix_post=norm_mix_post, norm_ffn_pre=norm_ffn_pre, w_up=w_up, ffn_dw_w=ffn_dw_w, ffn_dw_b=ffn_dw_b, w_down=w_down, norm_ffn_post=norm_ffn_post, loss_target=loss_target, m_norm_mix_pre=m_norm_mix_pre, m_w_in=m_w_in, m_conv_dw_w=m_conv_dw_w, m_conv_dw_b=m_conv_dw_b, m_conv_ln_g=m_conv_ln_g, m_conv_ln_b=m_conv_ln_b, m_rel_bias=m_rel_bias, m_w_out=m_w_out, m_norm_mix_post=m_norm_mix_post, m_norm_ffn_pre=m_norm_ffn_pre, m_w_up=m_w_up, m_ffn_dw_w=m_ffn_dw_w, m_ffn_dw_b=m_ffn_dw_b, m_w_down=m_w_down, m_norm_ffn_post=m_norm_ffn_post, v_norm_mix_pre=v_norm_mix_pre, v_w_in=v_w_in, v_conv_dw_w=v_conv_dw_w, v_conv_dw_b=v_conv_dw_b, v_conv_ln_g=v_conv_ln_g, v_conv_ln_b=v_conv_ln_b, v_rel_bias=v_rel_bias, v_w_out=v_w_out, v_norm_mix_post=v_norm_mix_post, v_norm_ffn_pre=v_norm_ffn_pre, v_w_up=v_w_up, v_ffn_dw_w=v_ffn_dw_w, v_ffn_dw_b=v_ffn_dw_b, v_w_down=v_w_down, v_norm_ffn_post=v_norm_ffn_post)
    weights = {n: given[n] for n in TWIN_WEIGHTS}
    shared = {n: given[n] for n in SHARED_INPUTS}
    per_example = {n: given[n] for n in ['x']}
    grad_fn = _jax.value_and_grad(_loss, argnums=(0, 1))

    def one_microbatch(ex, loss_target):
        ex = dict(ex)
        diff = ex.pop(TWIN_DIFF_INPUT)
        return grad_fn(weights, diff, {**shared, **ex}, loss_target)

    if N_MICROBATCH == 1:
        loss, (grad_w, grad_x) = one_microbatch(per_example, given["loss_target"])
    else:
        def body(carry, xs):
            loss_sum, grad_sum = carry
            l_k, (gw_k, gx_k) = one_microbatch(xs[0], xs[1])
            with _jax.named_scope("update"):
                return (loss_sum + l_k, _jax.tree.map(_jnp.add, grad_sum, gw_k)), gx_k

        init = (_jnp.zeros((), _jnp.float32), _jax.tree.map(_jnp.zeros_like, weights))
        (loss, grad_w), grad_x = _jax.lax.scan(body, init, (per_example, given["loss_target"]))
    with _jax.named_scope("update"):
        delta_w, new_m, new_v = {}, {}, {}
        for n in TWIN_WEIGHTS:
            delta_w[n], new_m[n], new_v[n] = _adamw(weights[n], grad_w[n], given["m_" + n], given["v_" + n])
    return (loss, grad_x, *[grad_w[n] for n in TWIN_WEIGHTS], *[delta_w[n] for n in TWIN_WEIGHTS],
            *[new_m[n] for n in TWIN_WEIGHTS], *[new_v[n] for n in TWIN_WEIGHTS])
```

```python
import functools
import math

import numpy as np
import jax
import jax.numpy as jnp
from jax import lax
from jax.experimental import pallas as pl
from jax.experimental.pallas import tpu as pltpu

F32 = jnp.float32
BF16 = jnp.bfloat16

D_MODEL = 1024
CONV_W = 512
ATTN_W = 512
N_HEADS = 8
HEAD_DIM = 64
CHUNK = 64
N_LEFT = 8
MAX_REL = 128
CONV_K = 31
CONV_HALO = 32
D_FF = 2816
FF_SHARD = 1408
IN_COLS = 2560
IN_SHARD = 640
EPS = 1e-6
NEG_INF = -1e30
ATT_BLK = 256
N_ATT_TILES = 3
LANES = 128
N_CHIPS = 4

ADAM_LR = 0.001
ADAM_B1 = 0.9
ADAM_B2 = 0.999
ADAM_EPS = 1e-08
ADAM_WD = 0.01
ADAM_STEP = 10

MESH = pl.DeviceIdType.MESH
ANY = pl.BlockSpec(memory_space=pl.ANY)
VMEM_FULL = pl.BlockSpec(memory_space=pltpu.VMEM)


def _params(vmem_mb, n_grid=0):
    sem = ("arbitrary",) * n_grid if n_grid else None
    return pltpu.CompilerParams(dimension_semantics=sem, vmem_limit_bytes=vmem_mb << 20)


def _sigmoid(v):
    return 1.0 / (1.0 + jnp.exp(-v))


def _dot(a, b):
    return jnp.dot(a, b, preferred_element_type=F32)


def _dot_nt(a, b):
    return lax.dot_general(a, b, (((1,), (1,)), ((), ())), preferred_element_type=F32)


def _dot_tn(a, b):
    return lax.dot_general(a, b, (((0,), (0,)), ((), ())), preferred_element_type=F32)


def _rms_fwd(v, g):
    r = lax.rsqrt(jnp.mean(v * v, axis=-1, keepdims=True) + EPS)
    return v * r * g, r


def _rms_bwd(dy, v, g):
    r = lax.rsqrt(jnp.mean(v * v, axis=-1, keepdims=True) + EPS)
    vh = v * r
    dvh = dy * g
    dv = r * (dvh - vh * jnp.mean(dvh * vh, axis=-1, keepdims=True))
    return dv, dy * vh


def _fwd_in_proj(x, g1, w_in):
    T = x.shape[0]
    tm = 512

    def body(x_ref, g_ref, w_ref, u_ref, a_ref, qkv_ref):
        u, _ = _rms_fwd(x_ref[...], g_ref[...])
        u = u.astype(BF16)
        u_ref[...] = u
        for s in range(N_CHIPS):
            y = _dot(u, w_ref[s])
            lo, hi = IN_SHARD * s, IN_SHARD * (s + 1)
            if hi <= 1024:
                a_ref[:, lo:hi] = y
            elif lo >= 1024:
                qkv_ref[:, lo - 1024:hi - 1024] = y.astype(BF16)
            else:
                a_ref[:, lo:1024] = y[:, :1024 - lo]
                qkv_ref[:, 0:hi - 1024] = y[:, 1024 - lo:].astype(BF16)

    return pl.pallas_call(
        body, name="fwd_in_proj", grid=(T // tm,),
        in_specs=[pl.BlockSpec((tm, D_MODEL), lambda i: (i, 0)),
                  pl.BlockSpec((1, D_MODEL), lambda i: (0, 0)),
                  pl.BlockSpec((N_CHIPS, D_MODEL, IN_SHARD), lambda i: (0, 0, 0))],
        out_specs=[pl.BlockSpec((tm, D_MODEL), lambda i: (i, 0)),
                   pl.BlockSpec((tm, 1024), lambda i: (i, 0)),
                   pl.BlockSpec((tm, 1536), lambda i: (i, 0))],
        out_shape=[jax.ShapeDtypeStruct((T, D_MODEL), BF16),
                   jax.ShapeDtypeStruct((T, 1024), F32),
                   jax.ShapeDtypeStruct((T, 1536), BF16)],
        compiler_params=_params(40, 1),
    )(x, g1, w_in)


def _fwd_conv(a, cw, cb, lg, lb):
    T = a.shape[0]
    tm = 512
    rc = 64

    def body(a_ref, w_ref, b_ref, lg_ref, lb_ref, co_ref, hc_ref, hext):
        i = pl.program_id(0)

        @pl.when(i == 0)
        def _():
            hext[0:CONV_HALO, :] = jnp.zeros((CONV_HALO, CONV_W), F32)

        @pl.when(i > 0)
        def _():
            hext[0:CONV_HALO, :] = hext[tm:tm + CONV_HALO, :]

        hext[CONV_HALO:CONV_HALO + tm, :] = a_ref[:, :CONV_W] * _sigmoid(a_ref[:, CONV_W:])
        for c in range(tm // rc):
            acc = jnp.zeros((rc, CONV_W), F32)
            for k in range(CONV_K):
                acc = acc + w_ref[k:k + 1, :] * hext[c * rc + 2 + k:c * rc + 2 + k + rc, :]
            hc = acc + b_ref[...]
            hc_ref[c * rc:(c + 1) * rc, :] = hc
            mu = jnp.mean(hc, axis=-1, keepdims=True)
            xc = hc - mu
            var = jnp.mean(xc * xc, axis=-1, keepdims=True)
            z = xc * lax.rsqrt(var + EPS) * lg_ref[...] + lb_ref[...]
            co_ref[c * rc:(c + 1) * rc, :] = (z * _sigmoid(z)).astype(BF16)

    return pl.pallas_call(
        body, name="fwd_conv", grid=(T // tm,),
        in_specs=[pl.BlockSpec((tm, 1024), lambda i: (i, 0)),
                  pl.BlockSpec((CONV_HALO, CONV_W), lambda i: (0, 0)),
                  pl.BlockSpec((1, CONV_W), lambda i: (0, 0)),
                  pl.BlockSpec((1, CONV_W), lambda i: (0, 0)),
                  pl.BlockSpec((1, CONV_W), lambda i: (0, 0))],
        out_specs=[pl.BlockSpec((tm, CONV_W), lambda i: (i, 0)),
                   pl.BlockSpec((tm, CONV_W), lambda i: (i, 0))],
        out_shape=[jax.ShapeDtypeStruct((T, CONV_W), BF16),
                   jax.ShapeDtypeStruct((T, CONV_W), F32)],
        scratch_shapes=[pltpu.VMEM((tm + CONV_HALO, CONV_W), F32)],
        compiler_params=_params(32, 1),
    )(a, cw, cb, lg, lb)


def _row_skew(v, sign):
    rows, width = v.shape
    row = lax.broadcasted_iota(jnp.int32, (rows, 1), 0)
    for b in range(int(math.log2(rows))):
        shift = (1 << b) if sign > 0 else width - (1 << b)
        v = jnp.where(((row >> b) & 1) == 1, pltpu.roll(v, shift, 1), v)
    return v


def _att_visible(d):
    rq = lax.broadcasted_iota(jnp.int32, (ATT_BLK, ATT_BLK), 0) // CHUNK
    ck = lax.broadcasted_iota(jnp.int32, (ATT_BLK, ATT_BLK), 1) // CHUNK
    slack = ATT_BLK
    above = jnp.where(d == 0, 0, slack)
    below = jnp.where(d == 2, 0, slack)
    return (ck <= rq + above) & (ck >= rq - below)


def _bias_tiles(vec):
    def body(v_ref, o_ref):
        d = pl.program_id(0) // N_HEADS
        full = _row_skew(jnp.broadcast_to(v_ref[0], (ATT_BLK, 2 * ATT_BLK)), 1)
        o_ref[0] = jnp.where(_att_visible(d), full[:, :ATT_BLK], NEG_INF)

    return pl.pallas_call(
        body, name="bias_tiles", grid=(N_ATT_TILES * N_HEADS,),
        in_specs=[pl.BlockSpec((1, 1, 2 * ATT_BLK), lambda n: (n, 0, 0))],
        out_specs=pl.BlockSpec((1, ATT_BLK, ATT_BLK), lambda n: (n, 0, 0)),
        out_shape=jax.ShapeDtypeStruct((N_ATT_TILES * N_HEADS, ATT_BLK, ATT_BLK), F32),
        compiler_params=_params(16, 1),
    )(vec)


def _diag_sums(ds):
    def body(d_ref, o_ref):
        wide = jnp.concatenate([d_ref[0], jnp.zeros((ATT_BLK, ATT_BLK), F32)], axis=1)
        o_ref[0] = jnp.sum(_row_skew(wide, -1), axis=0, keepdims=True)

    return pl.pallas_call(
        body, name="diag_sums", grid=(N_ATT_TILES * N_HEADS,),
        in_specs=[pl.BlockSpec((1, ATT_BLK, ATT_BLK), lambda n: (n, 0, 0))],
        out_specs=pl.BlockSpec((1, 1, 2 * ATT_BLK), lambda n: (n, 0, 0)),
        out_shape=jax.ShapeDtypeStruct((N_ATT_TILES * N_HEADS, 1, 2 * ATT_BLK), F32),
        compiler_params=_params(16, 1),
    )(ds)


def _head_mask(h):
    lane = lax.broadcasted_iota(jnp.int32, (1, LANES), 1)
    return (lane // HEAD_DIM) == (h % 2)


def _fwd_attn(qkv, bias):
    T = qkv.shape[0]
    nb = T // ATT_BLK
    scale = HEAD_DIM ** -0.5

    def body(q_ref, k0_ref, k1_ref, k2_ref, v0_ref, v1_ref, v2_ref, b_ref, o_ref, lse_ref):
        i = pl.program_id(0)
        k_refs = (k0_ref, k1_ref, k2_ref)
        v_refs = (v0_ref, v1_ref, v2_ref)
        lane = lax.broadcasted_iota(jnp.int32, (1, LANES), 1)
        lse_tile = jnp.zeros((ATT_BLK, LANES), F32)
        for g in range(N_HEADS // 2):
            cols = slice(g * LANES, (g + 1) * LANES)
            qg = q_ref[:, cols]
            og = jnp.zeros((ATT_BLK, LANES), F32)
            for h in (2 * g, 2 * g + 1):
                hm = _head_mask(h)
                qh = jnp.where(hm, qg, jnp.zeros_like(qg))
                s = []
                for d in range(N_ATT_TILES):
                    sd = _dot_nt(qh, k_refs[d][:, cols]) * scale + b_ref[d * N_HEADS + h]
                    if d > 0:
                        sd = jnp.where(i >= d, sd, NEG_INF)
                    s.append(sd)
                m = jnp.maximum(jnp.maximum(jnp.max(s[0], axis=-1, keepdims=True),
                                            jnp.max(s[1], axis=-1, keepdims=True)),
                                jnp.max(s[2], axis=-1, keepdims=True))
                p = [jnp.exp(sd - m) for sd in s]
                l = (jnp.sum(p[0], axis=-1, keepdims=True) + jnp.sum(p[1], axis=-1, keepdims=True)
                     + jnp.sum(p[2], axis=-1, keepdims=True))
                oh = jnp.zeros((ATT_BLK, LANES), F32)
                for d in range(N_ATT_TILES):
                    vg = v_refs[d][:, cols]
                    oh = oh + _dot(p[d].astype(BF16), jnp.where(hm, vg, jnp.zeros_like(vg)))
                og = og + oh / l
                lse_tile = jnp.where(lane == h, m + jnp.log(l), lse_tile)
            o_ref[:, cols] = og.astype(BF16)
        lse_ref[...] = lse_tile

    def kv_spec(d, col):
        return pl.BlockSpec((ATT_BLK, ATTN_W), lambda i: (jnp.maximum(i - d, 0), col))

    return pl.pallas_call(
        body, name="fwd_attn", grid=(nb,),
        in_specs=[pl.BlockSpec((ATT_BLK, ATTN_W), lambda i: (i, 0)),
                  kv_spec(0, 1), kv_spec(1, 1), kv_spec(2, 1),
                  kv_spec(0, 2), kv_spec(1, 2), kv_spec(2, 2),
                  pl.BlockSpec((N_ATT_TILES * N_HEADS, ATT_BLK, ATT_BLK), lambda i: (0, 0, 0))],
        out_specs=[pl.BlockSpec((ATT_BLK, ATTN_W), lambda i: (i, 0)),
                   pl.BlockSpec((ATT_BLK, LANES), lambda i: (i, 0))],
        out_shape=[jax.ShapeDtypeStruct((T, ATTN_W), BF16),
                   jax.ShapeDtypeStruct((T, LANES), F32)],
        compiler_params=_params(40, 1),
    )(qkv, qkv, qkv, qkv, qkv, qkv, qkv, bias)


def _fwd_out_proj(co, ao, w_out, x, g2, g3):
    T = x.shape[0]
    tm = 512

    def body(co_ref, ao_ref, w_ref, x_ref, g2_ref, g3_ref, mixed_ref, h1_ref, u2_ref):
        mixed = _dot(co_ref[...], w_ref[0:CONV_W, :]) + _dot(ao_ref[...], w_ref[CONV_W:, :])
        mixed_ref[...] = mixed
        y, _ = _rms_fwd(mixed, g2_ref[...])
        h1 = x_ref[...] + y
        h1_ref[...] = h1
        u2, _ = _rms_fwd(h1, g3_ref[...])
        u2_ref[...] = u2.astype(BF16)

    row = lambda w: pl.BlockSpec((tm, w), lambda i: (i, 0))
    vec = pl.BlockSpec((1, D_MODEL), lambda i: (0, 0))
    return pl.pallas_call(
        body, name="fwd_out_proj", grid=(T // tm,),
        in_specs=[row(CONV_W), row(ATTN_W), pl.BlockSpec((D_MODEL, D_MODEL), lambda i: (0, 0)),
                  row(D_MODEL), vec, vec],
        out_specs=[row(D_MODEL), row(D_MODEL), row(D_MODEL)],
        out_shape=[jax.ShapeDtypeStruct((T, D_MODEL), F32),
                   jax.ShapeDtypeStruct((T, D_MODEL), F32),
                   jax.ShapeDtypeStruct((T, D_MODEL), BF16)],
        compiler_params=_params(40, 1),
    )(co, ao, w_out, x, g2, g3)


GELU_C = math.sqrt(2.0 / math.pi)
GELU_A = 0.044715


def _gelu_and_grad(v):
    th = jnp.tanh(GELU_C * (v + GELU_A * v * v * v))
    gl = 0.5 * v * (1.0 + th)
    dgl = 0.5 * (1.0 + th) + 0.5 * v * (1.0 - th * th) * (GELU_C * (1.0 + 3.0 * GELU_A * v * v))
    return gl, dgl


FF_TM = 256
FF_HALO = 16


def _fwd_ffn(u2, w_up, fw, fb, w_down):
    T = u2.shape[0]
    tm = FF_TM

    def body(u_ref, wg_ref, wv_ref, fwg_ref, fwv_ref, fbg_ref, fbv_ref, wd_ref, hf_ref, f_ref, extg, extv):
        i = pl.program_id(1)
        u = u_ref[...]
        hs = (_dot(u, wg_ref[0]), _dot(u, wv_ref[0]))
        conv = []
        for n, (ext, fw_ref, fb_ref) in enumerate(((extg, fwg_ref, fbg_ref), (extv, fwv_ref, fbv_ref))):
            hf_ref[n] = hs[n].astype(BF16)

            @pl.when(i == 0)
            def _():
                ext[0:8, :] = jnp.zeros((8, FF_SHARD), F32)

            @pl.when(i > 0)
            def _():
                ext[0:8, :] = ext[tm:tm + 8, :]

            ext[8:8 + tm, :] = hs[n]
            conv.append(fw_ref[0, 0:1, :] * ext[6:6 + tm, :] + fw_ref[0, 1:2, :] * ext[7:7 + tm, :]
                        + fw_ref[0, 2:3, :] * hs[n] + fb_ref[0])
        gl, _ = _gelu_and_grad(conv[0])
        f_ref[0] = _dot((gl * conv[1]).astype(BF16), wd_ref[...])

    wspec = lambda off: pl.BlockSpec((1, D_MODEL, FF_SHARD), lambda s, i: (s + off, 0, 0))
    fwspec = lambda off: pl.BlockSpec((1, FF_HALO, FF_SHARD), lambda s, i: (s + off, 0, 0))
    fbspec = lambda off: pl.BlockSpec((1, 1, FF_SHARD), lambda s, i: (s + off, 0, 0))
    return pl.pallas_call(
        body, name="fwd_ffn", grid=(2, T // tm),
        in_specs=[pl.BlockSpec((tm, D_MODEL), lambda s, i: (i, 0)),
                  wspec(0), wspec(2), fwspec(0), fwspec(2), fbspec(0), fbspec(2),
                  pl.BlockSpec((FF_SHARD, D_MODEL), lambda s, i: (s, 0))],
        out_specs=[pl.BlockSpec((2, tm, FF_SHARD), lambda s, i: (0, i, s)),
                   pl.BlockSpec((1, tm, D_MODEL), lambda s, i: (s, i, 0))],
        out_shape=[jax.ShapeDtypeStruct((2, T, D_FF), BF16),
                   jax.ShapeDtypeStruct((2, T, D_MODEL), F32)],
        scratch_shapes=[pltpu.VMEM((tm + 8, FF_SHARD), F32), pltpu.VMEM((tm + 8, FF_SHARD), F32)],
        compiler_params=_params(48, 2),
    )(u2, w_up, w_up, fw, fw, fb, fb, w_down)


def _fwd_loss(fp, h1, tgt, g4):
    T = h1.shape[0]
    tm = 512

    def body(fp_ref, h1_ref, t_ref, g_ref, loss_ref, dy_ref, df_ref, dg_ref):
        i = pl.program_id(0)
        f = fp_ref[0] + fp_ref[1]
        r, _ = _rms_fwd(f, g_ref[...])
        e = (h1_ref[...] + r) - t_ref[...]
        dy = e * (1.0 / D_MODEL)
        dy_ref[...] = dy
        df, dg_rows = _rms_bwd(dy, f, g_ref[...])
        df_ref[...] = df.astype(BF16)
        part = 0.5 * jnp.sum(jnp.mean(e * e, axis=-1, keepdims=True), axis=0, keepdims=True)
        dg = jnp.sum(dg_rows, axis=0, keepdims=True)

        @pl.when(i == 0)
        def _():
            loss_ref[...] = part
            dg_ref[...] = dg

        @pl.when(i > 0)
        def _():
            loss_ref[...] += part
            dg_ref[...] += dg

    row = pl.BlockSpec((tm, D_MODEL), lambda i: (i, 0))
    vec = pl.BlockSpec((1, D_MODEL), lambda i: (0, 0))
    return pl.pallas_call(
        body, name="fwd_loss", grid=(T // tm,),
        in_specs=[pl.BlockSpec((2, tm, D_MODEL), lambda i: (0, i, 0)), row, row, vec],
        out_specs=[pl.BlockSpec((1, 1), lambda i: (0, 0)), row, row, vec],
        out_shape=[jax.ShapeDtypeStruct((1, 1), F32),
                   jax.ShapeDtypeStruct((T, D_MODEL), F32),
                   jax.ShapeDtypeStruct((T, D_MODEL), BF16),
                   jax.ShapeDtypeStruct((1, D_MODEL), F32)],
        compiler_params=_params(40, 1),
    )(fp, h1, tgt, g4)


def _bwd_ffn(df, hf, w_up, fw, fb, w_down):
    T = df.shape[0]
    tm = FF_TM
    ni = T // tm

    def body(df_ref, hf_ref, halo_ref, wd_ref, wg_ref, wv_ref, fwg_ref, fwv_ref, fbg_ref, fbv_ref,
             du_ref, dhf_ref, act_ref, dwg_ref, dwv_ref, extg, extv, dextg, dextv, carg, carv):
        i = pl.program_id(1)
        ri = ni - 1 - i
        pre, pre1, pre2 = [], [], []
        for n, (ext, fw_ref, fb_ref) in enumerate(((extg, fwg_ref, fbg_ref), (extv, fwv_ref, fbv_ref))):
            h0 = hf_ref[n].astype(F32)
            ext[FF_HALO:FF_HALO + tm, :] = h0
            ext[0:FF_HALO, :] = jnp.where(ri > 0, halo_ref[n].astype(F32), 0.0)
            h1 = ext[FF_HALO - 1:FF_HALO - 1 + tm, :]
            h2 = ext[FF_HALO - 2:FF_HALO - 2 + tm, :]
            pre.append(fw_ref[0, 0:1, :] * h2 + fw_ref[0, 1:2, :] * h1 + fw_ref[0, 2:3, :] * h0 + fb_ref[0])
            pre1.append(h1)
            pre2.append(h2)
        dact = _dot_nt(df_ref[...], wd_ref[...])
        gl, dgl = _gelu_and_grad(pre[0])
        act_ref[...] = (gl * pre[1]).astype(BF16)
        dpre = (dact * pre[1] * dgl, dact * gl)
        du = None
        for n, (dext, car, fw_ref, dw_ref, w_ref) in enumerate(
                ((dextg, carg, fwg_ref, dwg_ref, wg_ref), (dextv, carv, fwv_ref, dwv_ref, wv_ref))):
            dp = dpre[n]
            h0 = hf_ref[n].astype(F32)
            rows = (jnp.sum(dp * pre2[n], axis=0, keepdims=True), jnp.sum(dp * pre1[n], axis=0, keepdims=True),
                    jnp.sum(dp * h0, axis=0, keepdims=True), jnp.sum(dp, axis=0, keepdims=True))

            @pl.when(i == 0)
            def _():
                dw_ref[0] = jnp.zeros((FF_HALO, FF_SHARD), F32)
                car[...] = jnp.zeros((8, FF_SHARD), F32)

            for k in range(4):
                dw_ref[0, k:k + 1, :] += rows[k]
            dext[0:tm, :] = dp
            dext[tm:tm + 8, :] = car[...]
            car[...] = dp[0:8, :]
            dh = (fw_ref[0, 2:3, :] * dp + fw_ref[0, 1:2, :] * dext[1:1 + tm, :]
                  + fw_ref[0, 0:1, :] * dext[2:2 + tm, :]).astype(BF16)
            dhf_ref[n] = dh
            term = _dot_nt(dh, w_ref[0])
            du = term if du is None else du + term
        du_ref[0] = du

    rev = lambda i: ni - 1 - i
    wspec = lambda off: pl.BlockSpec((1, D_MODEL, FF_SHARD), lambda s, i: (s + off, 0, 0))
    fwspec = lambda off: pl.BlockSpec((1, FF_HALO, FF_SHARD), lambda s, i: (s + off, 0, 0))
    fbspec = lambda off: pl.BlockSpec((1, 1, FF_SHARD), lambda s, i: (s + off, 0, 0))
    halo_blocks = tm // FF_HALO
    dwspec = pl.BlockSpec((1, FF_HALO, FF_SHARD), lambda s, i: (s, 0, 0))
    return pl.pallas_call(
        body, name="bwd_ffn", grid=(2, ni),
        in_specs=[pl.BlockSpec((tm, D_MODEL), lambda s, i: (rev(i), 0)),
                  pl.BlockSpec((2, tm, FF_SHARD), lambda s, i: (0, rev(i), s)),
                  pl.BlockSpec((2, FF_HALO, FF_SHARD),
                               lambda s, i: (0, jnp.maximum(rev(i) * halo_blocks - 1, 0), s)),
                  pl.BlockSpec((FF_SHARD, D_MODEL), lambda s, i: (s, 0)),
                  wspec(0), wspec(2), fwspec(0), fwspec(2), fbspec(0), fbspec(2)],
        out_specs=[pl.BlockSpec((1, tm, D_MODEL), lambda s, i: (s, rev(i), 0)),
                   pl.BlockSpec((2, tm, FF_SHARD), lambda s, i: (0, rev(i), s)),
                   pl.BlockSpec((tm, FF_SHARD), lambda s, i: (rev(i), s)),
                   dwspec, dwspec],
        out_shape=[jax.ShapeDtypeStruct((2, T, D_MODEL), F32),
                   jax.ShapeDtypeStruct((2, T, D_FF), BF16),
                   jax.ShapeDtypeStruct((T, D_FF), BF16),
                   jax.ShapeDtypeStruct((2, FF_HALO, FF_SHARD), F32),
                   jax.ShapeDtypeStruct((2, FF_HALO, FF_SHARD), F32)],
        scratch_shapes=[pltpu.VMEM((tm + FF_HALO, FF_SHARD), F32), pltpu.VMEM((tm + FF_HALO, FF_SHARD), F32),
                        pltpu.VMEM((tm + 8, FF_SHARD), F32), pltpu.VMEM((tm + 8, FF_SHARD), F32),
                        pltpu.VMEM((8, FF_SHARD), F32), pltpu.VMEM((8, FF_SHARD), F32)],
        compiler_params=_params(56, 2),
    )(df, hf, hf, w_down, w_up, w_up, fw, fw, fb, fb)


def _bwd_mid(du2p, dy, h1, mixed, g3, g2, w_out):
    T = dy.shape[0]
    tm = 512

    def body(du_ref, dy_ref, h1_ref, mx_ref, g3_ref, g2_ref, w_ref,
             dh1_ref, dmx_ref, dco_ref, dao_ref, dg3_ref, dg2_ref):
        i = pl.program_id(0)
        dres, dg3_rows = _rms_bwd(du_ref[0] + du_ref[1], h1_ref[...], g3_ref[...])
        dh1 = dy_ref[...] + dres
        dh1_ref[...] = dh1
        dmx, dg2_rows = _rms_bwd(dh1, mx_ref[...], g2_ref[...])
        dmx = dmx.astype(BF16)
        dmx_ref[...] = dmx
        dcat = _dot_nt(dmx, w_ref[...])
        dco_ref[...] = dcat[:, :CONV_W]
        dao_ref[...] = dcat[:, CONV_W:].astype(BF16)
        dg3 = jnp.sum(dg3_rows, axis=0, keepdims=True)
        dg2 = jnp.sum(dg2_rows, axis=0, keepdims=True)

        @pl.when(i == 0)
        def _():
            dg3_ref[...] = dg3
            dg2_ref[...] = dg2

        @pl.when(i > 0)
        def _():
            dg3_ref[...] += dg3
            dg2_ref[...] += dg2

    row = lambda w: pl.BlockSpec((tm, w), lambda i: (i, 0))
    vec = pl.BlockSpec((1, D_MODEL), lambda i: (0, 0))
    return pl.pallas_call(
        body, name="bwd_mid", grid=(T // tm,),
        in_specs=[pl.BlockSpec((2, tm, D_MODEL), lambda i: (0, i, 0)), row(D_MODEL), row(D_MODEL),
                  row(D_MODEL), vec, vec, pl.BlockSpec((D_MODEL, D_MODEL), lambda i: (0, 0))],
        out_specs=[row(D_MODEL), row(D_MODEL), row(CONV_W), row(ATTN_W), vec, vec],
        out_shape=[jax.ShapeDtypeStruct((T, D_MODEL), F32),
                   jax.ShapeDtypeStruct((T, D_MODEL), BF16),
                   jax.ShapeDtypeStruct((T, CONV_W), F32),
                   jax.ShapeDtypeStruct((T, ATTN_W), BF16),
                   jax.ShapeDtypeStruct((1, D_MODEL), F32),
                   jax.ShapeDtypeStruct((1, D_MODEL), F32)],
        compiler_params=_params(48, 1),
    )(du2p, dy, h1, mixed, g3, g2, w_out)


def _bwd_attn(qkv, ao, dao, lse, bias):
    T = qkv.shape[0]
    nb = T // ATT_BLK
    scale = HEAD_DIM ** -0.5

    def body(k_ref, v_ref, q0, q1, q2, do0, do1, do2, o0, o1, o2, l0, l1, l2, b_ref,
             dp_ref, ds_ref, acc1, acc2):
        j = pl.program_id(0)
        q_refs, do_refs, o_refs, l_refs = (q0, q1, q2), (do0, do1, do2), (o0, o1, o2), (l0, l1, l2)

        @pl.when(j == 0)
        def _():
            ds_ref[...] = jnp.zeros(ds_ref.shape, F32)
            acc1[...] = jnp.zeros(acc1.shape, F32)
            acc2[...] = jnp.zeros(acc2.shape, F32)

        dq_new = [[], [], []]
        dk_cols, dv_cols = [], []
        for g in range(N_HEADS // 2):
            cols = slice(g * LANES, (g + 1) * LANES)
            kg = k_ref[:, cols]
            vg = v_ref[:, cols]
            dkg = jnp.zeros((ATT_BLK, LANES), F32)
            dvg = jnp.zeros((ATT_BLK, LANES), F32)
            dqg = [jnp.zeros((ATT_BLK, LANES), F32) for _ in range(N_ATT_TILES)]
            for d in range(N_ATT_TILES):
                qg = q_refs[d][:, cols]
                dog = do_refs[d][:, cols]
                prod = dog.astype(F32) * o_refs[d][:, cols].astype(F32)
                for h in (2 * g, 2 * g + 1):
                    hm = _head_mask(h)
                    qh = jnp.where(hm, qg, jnp.zeros_like(qg))
                    doh = jnp.where(hm, dog, jnp.zeros_like(dog))
                    kh = jnp.where(hm, kg, jnp.zeros_like(kg))
                    delta = jnp.sum(jnp.where(hm, prod, 0.0), axis=-1, keepdims=True)
                    s = _dot_nt(qh, kg) * scale + b_ref[d * N_HEADS + h]
                    p = jnp.exp(s - l_refs[d][:, h:h + 1])
                    p = jnp.where(j + d < nb, p, 0.0)
                    dvg = dvg + _dot_tn(p.astype(BF16), doh)
                    dpm = _dot_nt(doh, vg)
                    dsc = p * (dpm - delta)
                    ds_ref[d * N_HEADS + h] += dsc
                    dsb = (dsc * scale).astype(BF16)
                    dqg[d] = dqg[d] + _dot(dsb, kh)
                    dkg = dkg + _dot_tn(dsb, qh)
            for d in range(N_ATT_TILES):
                dq_new[d].append(dqg[d])
            dk_cols.append(dkg)
            dv_cols.append(dvg)
        x0, x1, x2 = (jnp.concatenate(c, axis=1) for c in dq_new)
        dp_ref[:, 0:1024] = jnp.zeros((ATT_BLK, 1024), BF16)
        dp_ref[:, 1024:1536] = (acc1[...] + x0).astype(BF16)
        dp_ref[:, 1536:2048] = jnp.concatenate(dk_cols, axis=1).astype(BF16)
        dp_ref[:, 2048:2560] = jnp.concatenate(dv_cols, axis=1).astype(BF16)
        acc1[...] = acc2[...] + x1
        acc2[...] = x2

    def fwd_spec(d, width, col):
        return pl.BlockSpec((ATT_BLK, width), lambda j: (jnp.minimum(j + d, nb - 1), col))

    return pl.pallas_call(
        body, name="bwd_attn", grid=(nb,),
        in_specs=[pl.BlockSpec((ATT_BLK, ATTN_W), lambda j: (j, 1)),
                  pl.BlockSpec((ATT_BLK, ATTN_W), lambda j: (j, 2)),
                  fwd_spec(0, ATTN_W, 0), fwd_spec(1, ATTN_W, 0), fwd_spec(2, ATTN_W, 0),
                  fwd_spec(0, ATTN_W, 0), fwd_spec(1, ATTN_W, 0), fwd_spec(2, ATTN_W, 0),
                  fwd_spec(0, ATTN_W, 0), fwd_spec(1, ATTN_W, 0), fwd_spec(2, ATTN_W, 0),
                  fwd_spec(0, LANES, 0), fwd_spec(1, LANES, 0), fwd_spec(2, LANES, 0),
                  pl.BlockSpec((N_ATT_TILES * N_HEADS, ATT_BLK, ATT_BLK), lambda j: (0, 0, 0))],
        out_specs=[pl.BlockSpec((ATT_BLK, IN_COLS), lambda j: (j, 0)),
                   pl.BlockSpec((N_ATT_TILES * N_HEADS, ATT_BLK, ATT_BLK), lambda j: (0, 0, 0))],
        out_shape=[jax.ShapeDtypeStruct((T, IN_COLS), BF16),
                   jax.ShapeDtypeStruct((N_ATT_TILES * N_HEADS, ATT_BLK, ATT_BLK), F32)],
        scratch_shapes=[pltpu.VMEM((ATT_BLK, ATTN_W), F32), pltpu.VMEM((ATT_BLK, ATTN_W), F32)],
        compiler_params=_params(56, 1),
    )(qkv, qkv, qkv, qkv, qkv, dao, dao, dao, ao, ao, ao, lse, lse, lse, bias)


def _bwd_conv(dproj, a, dco, hc, cw, lg, lb):
    T = a.shape[0]
    tm = 512
    rc = 64
    ni = T // tm
    hb = tm // CONV_HALO

    def body(dp_in, a_ref, ap_ref, dco_ref, dcon_ref, hc_ref, hcn_ref, w_ref, lg_ref, lb_ref,
             dp_ref, dw_ref, db_ref, dlg_ref, dlb_ref, hext, dext):
        del dp_in
        i = pl.program_id(0)

        def ln_bwd(dco_v, hc_v):
            mu = jnp.mean(hc_v, axis=-1, keepdims=True)
            xc = hc_v - mu
            rstd = lax.rsqrt(jnp.mean(xc * xc, axis=-1, keepdims=True) + EPS)
            xh = xc * rstd
            z = xh * lg_ref[...] + lb_ref[...]
            sg = _sigmoid(z)
            dz = dco_v * (sg * (1.0 + z * (1.0 - sg)))
            dxh = dz * lg_ref[...]
            dhc = rstd * (dxh - jnp.mean(dxh, axis=-1, keepdims=True)
                          - xh * jnp.mean(dxh * xh, axis=-1, keepdims=True))
            return dhc, dz * xh, dz

        hext[0:CONV_HALO, :] = jnp.where(i > 0, ap_ref[:, :CONV_W] * _sigmoid(ap_ref[:, CONV_W:]), 0.0)
        hext[CONV_HALO:CONV_HALO + tm, :] = a_ref[:, :CONV_W] * _sigmoid(a_ref[:, CONV_W:])
        dhc, dlg_rows, dlb_rows = ln_bwd(dco_ref[...], hc_ref[...])
        dext[0:tm, :] = dhc
        dhc_next, _, _ = ln_bwd(dcon_ref[...], hcn_ref[...])
        dext[tm:tm + CONV_HALO, :] = jnp.where(i < ni - 1, dhc_next, 0.0)

        @pl.when(i == 0)
        def _():
            dw_ref[...] = jnp.zeros(dw_ref.shape, F32)
            db_ref[...] = jnp.zeros(db_ref.shape, F32)
            dlg_ref[...] = jnp.zeros(dlg_ref.shape, F32)
            dlb_ref[...] = jnp.zeros(dlb_ref.shape, F32)

        db_ref[...] += jnp.sum(dhc, axis=0, keepdims=True)
        dlg_ref[...] += jnp.sum(dlg_rows, axis=0, keepdims=True)
        dlb_ref[...] += jnp.sum(dlb_rows, axis=0, keepdims=True)
        dw_rows = [jnp.zeros((1, CONV_W), F32) for _ in range(CONV_K)]
        for c in range(tm // rc):
            r0 = c * rc
            dh = jnp.zeros((rc, CONV_W), F32)
            dhc_c = dext[r0:r0 + rc, :]
            for k in range(CONV_K):
                dh = dh + w_ref[k:k + 1, :] * dext[r0 + 30 - k:r0 + 30 - k + rc, :]
                dw_rows[k] = dw_rows[k] + jnp.sum(dhc_c * hext[r0 + 2 + k:r0 + 2 + k + rc, :],
                                                  axis=0, keepdims=True)
            av = a_ref[r0:r0 + rc, :CONV_W]
            sg = _sigmoid(a_ref[r0:r0 + rc, CONV_W:])
            dp_ref[r0:r0 + rc, 0:CONV_W] = (dh * sg).astype(BF16)
            dp_ref[r0:r0 + rc, CONV_W:] = (dh * av * sg * (1.0 - sg)).astype(BF16)
        for k in range(CONV_K):
            dw_ref[k:k + 1, :] += dw_rows[k]

    row = lambda w: pl.BlockSpec((tm, w), lambda i: (i, 0))
    prev = lambda w: pl.BlockSpec((CONV_HALO, w), lambda i: (jnp.maximum(i * hb - 1, 0), 0))
    nxt = lambda w: pl.BlockSpec((CONV_HALO, w), lambda i: (jnp.minimum((i + 1) * hb, ni * hb - 1), 0))
    vec = pl.BlockSpec((1, CONV_W), lambda i: (0, 0))
    return pl.pallas_call(
        body, name="bwd_conv", grid=(ni,),
        in_specs=[ANY, row(1024), prev(1024), row(CONV_W), nxt(CONV_W), row(CONV_W), nxt(CONV_W),
                  pl.BlockSpec((CONV_HALO, CONV_W), lambda i: (0, 0)), vec, vec],
        out_specs=[pl.BlockSpec((tm, 1024), lambda i: (i, 0)),
                   pl.BlockSpec((CONV_HALO, CONV_W), lambda i: (0, 0)), vec, vec, vec],
        out_shape=[jax.ShapeDtypeStruct((T, IN_COLS), BF16),
                   jax.ShapeDtypeStruct((CONV_HALO, CONV_W), F32),
                   jax.ShapeDtypeStruct((1, CONV_W), F32),
                   jax.ShapeDtypeStruct((1, CONV_W), F32),
                   jax.ShapeDtypeStruct((1, CONV_W), F32)],
        scratch_shapes=[pltpu.VMEM((tm + CONV_HALO, CONV_W), F32), pltpu.VMEM((tm + CONV_HALO, CONV_W), F32)],
        input_output_aliases={0: 0},
        compiler_params=_params(40, 1),
    )(dproj, a, a, dco, dco, hc, hc, cw, lg, lb)


def _bwd_in_proj(dproj, w_in, x, dh1, g1):
    T = x.shape[0]
    tm = 512

    def body(dp_ref, w_ref, x_ref, dh_ref, g_ref, gx_ref, dg_ref):
        i = pl.program_id(0)
        du = None
        for s in range(N_CHIPS):
            term = _dot_nt(dp_ref[:, IN_SHARD * s:IN_SHARD * (s + 1)], w_ref[s])
            du = term if du is None else du + term
        dx, dg_rows = _rms_bwd(du, x_ref[...], g_ref[...])
        gx_ref[...] = dh_ref[...] + dx
        dg = jnp.sum(dg_rows, axis=0, keepdims=True)

        @pl.when(i == 0)
        def _():
            dg_ref[...] = dg

        @pl.when(i > 0)
        def _():
            dg_ref[...] += dg

    row = lambda w: pl.BlockSpec((tm, w), lambda i: (i, 0))
    vec = pl.BlockSpec((1, D_MODEL), lambda i: (0, 0))
    return pl.pallas_call(
        body, name="bwd_in_proj", grid=(T // tm,),
        in_specs=[row(IN_COLS), pl.BlockSpec((N_CHIPS, D_MODEL, IN_SHARD), lambda i: (0, 0, 0)),
                  row(D_MODEL), row(D_MODEL), vec],
        out_specs=[row(D_MODEL), vec],
        out_shape=[jax.ShapeDtypeStruct((T, D_MODEL), F32), jax.ShapeDtypeStruct((1, D_MODEL), F32)],
        compiler_params=_params(40, 1),
    )(dproj, w_in, x, dh1, g1)


def _wgrad(name, a_list, a_spec, b, b_spec, out_block, out_spec, out_shape, n_outer, T, tk=512, select=None):
    def body(*refs):
        a_refs, b_ref, o_ref = refs[:len(a_list)], refs[len(a_list)], refs[len(a_list) + 1]
        kt = pl.program_id(1)

        @pl.when(kt == 0)
        def _():
            o_ref[...] = jnp.zeros(o_ref.shape, F32)

        bv = b_ref[...].reshape(b_ref.shape[-2:])
        if select is None:
            o_ref[...] += _dot_tn(a_refs[0][...].reshape(a_refs[0].shape[-2:]), bv).reshape(o_ref.shape)
        else:
            for n, a_ref in enumerate(a_refs):
                @pl.when(select(pl.program_id(0)) == n)
                def _():
                    o_ref[...] += _dot_tn(a_ref[...], bv).reshape(o_ref.shape)

    del out_block
    return pl.pallas_call(
        body, name=name, grid=(n_outer, T // tk),
        in_specs=[a_spec] * len(a_list) + [b_spec],
        out_specs=out_spec, out_shape=out_shape,
        compiler_params=_params(48, 2),
    )(*a_list, b)


def _mesh_pos():
    return lax.axis_index("x"), lax.axis_index("y"), lax.axis_index("c")


def _other_chips(x, y):
    return [((1 - x, y), 2 * (1 - x) + y), ((x, 1 - y), 2 * x + (1 - y)), ((1 - x, 1 - y), 2 * (1 - x) + (1 - y))]


def _all_gather_weights(shards):
    n = len(shards)

    def body(*refs):
        in_refs, out_refs = refs[:n], refs[n:2 * n]
        send_sems, recv_sems, loc_sems = refs[2 * n:]
        x, y, c = _mesh_pos()
        me = 2 * x + y
        sibling = (x, y, 1 - c)
        chips = _other_chips(x, y)
        local, first, passed = [], [], []
        for t in range(n):
            half = shards[t].shape[0] // 2
            rows = pl.ds(c * half, half)
            cp = pltpu.make_async_copy(in_refs[t], out_refs[t].at[me], loc_sems.at[t])
            cp.start()
            local.append(cp)
            for k, (chip, _) in enumerate(chips):
                cp = pltpu.make_async_remote_copy(
                    src_ref=in_refs[t].at[rows], dst_ref=out_refs[t].at[me, rows],
                    send_sem=send_sems.at[t, k], recv_sem=recv_sems.at[t, k],
                    device_id=(*chip, c), device_id_type=MESH)
                cp.start()
                first.append(cp)
        for t in range(n):
            half = shards[t].shape[0] // 2
            rows = pl.ds(c * half, half)
            for k, (chip, s) in enumerate(chips):
                landed = out_refs[t].at[s, rows]
                pltpu.make_async_remote_copy(
                    src_ref=landed, dst_ref=landed, send_sem=send_sems.at[t, k], recv_sem=recv_sems.at[t, k],
                    device_id=(*chip, c), device_id_type=MESH).wait_recv()
                cp = pltpu.make_async_remote_copy(
                    src_ref=landed, dst_ref=landed, send_sem=send_sems.at[t, 3 + k], recv_sem=recv_sems.at[t, 3 + k],
                    device_id=sibling, device_id_type=MESH)
                cp.start()
                passed.append(cp)
        for t in range(n):
            half = shards[t].shape[0] // 2
            other = pl.ds((1 - c) * half, half)
            for k, (chip, s) in enumerate(chips):
                got = out_refs[t].at[s, other]
                pltpu.make_async_remote_copy(
                    src_ref=got, dst_ref=got, send_sem=send_sems.at[t, 3 + k], recv_sem=recv_sems.at[t, 3 + k],
                    device_id=sibling, device_id_type=MESH).wait_recv()
        for cp in first + passed:
            cp.wait_send()
        for cp in local:
            cp.wait()

    return pl.pallas_call(
        body, name="all_gather_weights",
        in_specs=[ANY] * n, out_specs=[ANY] * n,
        out_shape=[jax.ShapeDtypeStruct((N_CHIPS,) + s.shape, s.dtype) for s in shards],
        scratch_shapes=[pltpu.SemaphoreType.DMA((n, 6)), pltpu.SemaphoreType.DMA((n, 6)),
                        pltpu.SemaphoreType.DMA((n,))],
        compiler_params=pltpu.CompilerParams(has_side_effects=True),
    )(*shards)


def _pair_exchange(grads):
    n = len(grads)

    def body(*refs):
        g_refs, mine_refs, got_refs = refs[:n], refs[n:2 * n], refs[2 * n:3 * n]
        send_sems, recv_sems, loc_sems = refs[3 * n:]
        x, y, c = _mesh_pos()
        copies = []
        for t in range(n):
            half = grads[t].shape[1] // 2
            cp = pltpu.make_async_copy(g_refs[t].at[:, pl.ds(c * half, half), :], mine_refs[t], loc_sems.at[t])
            cp.start()
            copies.append(cp)
            cp = pltpu.make_async_remote_copy(
                src_ref=g_refs[t].at[:, pl.ds((1 - c) * half, half), :], dst_ref=got_refs[t],
                send_sem=send_sems.at[t], recv_sem=recv_sems.at[t],
                device_id=(x, y, 1 - c), device_id_type=MESH)
            cp.start()
            copies.append(cp)
        for cp in copies:
            cp.wait()

    halves = [jax.ShapeDtypeStruct((N_CHIPS, g.shape[1] // 2, g.shape[2]), F32) for g in grads]
    outs = pl.pallas_call(
        body, name="pair_exchange",
        in_specs=[ANY] * n, out_specs=[ANY] * (2 * n), out_shape=halves + halves,
        scratch_shapes=[pltpu.SemaphoreType.DMA((n,)), pltpu.SemaphoreType.DMA((n,)),
                        pltpu.SemaphoreType.DMA((n,))],
        compiler_params=pltpu.CompilerParams(has_side_effects=True),
    )(*grads)
    return outs[:n], outs[n:]


def _chip_exchange(pairs):
    n = len(pairs)

    def body(*refs):
        p_refs, own_refs, got_refs = refs[:n], refs[n:2 * n], refs[2 * n:3 * n]
        send_sems, recv_sems, loc_sems = refs[3 * n:]
        x, y, c = _mesh_pos()
        copies = []
        for t in range(n):
            cp = pltpu.make_async_copy(p_refs[t].at[2 * x + y], own_refs[t], loc_sems.at[t])
            cp.start()
            copies.append(cp)
            for k, (chip, s) in enumerate(_other_chips(x, y)):
                cp = pltpu.make_async_remote_copy(
                    src_ref=p_refs[t].at[s], dst_ref=got_refs[t].at[k],
                    send_sem=send_sems.at[t, k], recv_sem=recv_sems.at[t, k],
                    device_id=(*chip, c), device_id_type=MESH)
                cp.start()
                copies.append(cp)
        for cp in copies:
            cp.wait()

    own = [jax.ShapeDtypeStruct(p.shape[1:], F32) for p in pairs]
    got = [jax.ShapeDtypeStruct((3,) + p.shape[1:], F32) for p in pairs]
    outs = pl.pallas_call(
        body, name="chip_exchange",
        in_specs=[ANY] * n, out_specs=[ANY] * (2 * n), out_shape=own + got,
        scratch_shapes=[pltpu.SemaphoreType.DMA((n, 3)), pltpu.SemaphoreType.DMA((n, 3)),
                        pltpu.SemaphoreType.DMA((n,))],
        compiler_params=pltpu.CompilerParams(has_side_effects=True),
    )(*pairs)
    return outs[:n], outs[n:]


def _pair_gather(halves):
    n = len(halves)

    def body(*refs):
        h_refs, out_refs = refs[:n], refs[n:2 * n]
        send_sems, recv_sems, loc_sems = refs[2 * n:]
        x, y, c = _mesh_pos()
        copies = []
        for t in range(n):
            half = halves[t].shape[0]
            rows = pl.ds(c * half, half)
            cp = pltpu.make_async_copy(h_refs[t], out_refs[t].at[rows], loc_sems.at[t])
            cp.start()
            copies.append(cp)
            cp = pltpu.make_async_remote_copy(
                src_ref=h_refs[t], dst_ref=out_refs[t].at[rows],
                send_sem=send_sems.at[t], recv_sem=recv_sems.at[t],
                device_id=(x, y, 1 - c), device_id_type=MESH)
            cp.start()
            copies.append(cp)
        for cp in copies:
            cp.wait()

    return pl.pallas_call(
        body, name="pair_gather",
        in_specs=[ANY] * n, out_specs=[ANY] * n,
        out_shape=[jax.ShapeDtypeStruct((2 * h.shape[0], h.shape[1]), F32) for h in halves],
        scratch_shapes=[pltpu.SemaphoreType.DMA((n,)), pltpu.SemaphoreType.DMA((n,)),
                        pltpu.SemaphoreType.DMA((n,))],
        compiler_params=pltpu.CompilerParams(has_side_effects=True),
    )(*halves)


def _all_reduce_small(pack):
    rows = pack.shape[0]

    def body(p_ref, o_ref, buf, send_sems, recv_sems):
        x, y, c = _mesh_pos()
        me = 4 * x + 2 * y + c
        buf[0] = p_ref[...]
        copies = []
        for k in range(1, 8):
            peer = (x ^ (k >> 2), y ^ ((k >> 1) & 1), c ^ (k & 1))
            cp = pltpu.make_async_remote_copy(
                src_ref=p_ref, dst_ref=buf.at[k], send_sem=send_sems.at[k - 1], recv_sem=recv_sems.at[k - 1],
                device_id=peer, device_id_type=MESH)
            cp.start()
            copies.append(cp)
        for cp in copies:
            cp.wait()
        total = buf[me]
        for dev in range(1, 8):
            total = total + buf[me ^ dev]
        o_ref[...] = total

    return pl.pallas_call(
        body, name="all_reduce_small",
        in_specs=[VMEM_FULL], out_specs=VMEM_FULL,
        out_shape=jax.ShapeDtypeStruct(pack.shape, F32),
        scratch_shapes=[pltpu.VMEM((8, rows, LANES), F32),
                        pltpu.SemaphoreType.DMA((7,)), pltpu.SemaphoreType.DMA((7,))],
        compiler_params=pltpu.CompilerParams(has_side_effects=True),
    )(pack)


def _row_block(rows):
    for rb in (256, 352, 128, 184, 64, 8):
        if rows % rb == 0:
            return rb
    return rows


def _add2(name, a, b):
    S, R, C = a.shape
    rb = _row_block(R)

    def body(a_ref, b_ref, o_ref):
        o_ref[...] = a_ref[...] + b_ref[...]

    spec = pl.BlockSpec((1, rb, C), lambda s, r: (s, r, 0))
    return pl.pallas_call(
        body, name=name, grid=(S, R // rb), in_specs=[spec, spec], out_specs=spec,
        out_shape=jax.ShapeDtypeStruct(a.shape, F32), compiler_params=_params(32, 2),
    )(a, b)


def _add4(name, own, got):
    R, C = own.shape
    rb = _row_block(R)

    def body(a_ref, g_ref, o_ref):
        o_ref[...] = ((a_ref[...] + g_ref[0]) + g_ref[1]) + g_ref[2]

    return pl.pallas_call(
        body, name=name, grid=(R // rb,),
        in_specs=[pl.BlockSpec((rb, C), lambda r: (r, 0)), pl.BlockSpec((3, rb, C), lambda r: (0, r, 0))],
        out_specs=pl.BlockSpec((rb, C), lambda r: (r, 0)),
        out_shape=jax.ShapeDtypeStruct(own.shape, F32), compiler_params=_params(32, 1),
    )(own, got)


def _adamw(name, w, g, m, v):
    R, C = w.shape
    rb = _row_block(R)
    c1 = 1.0 - ADAM_B1 ** ADAM_STEP
    c2 = 1.0 - ADAM_B2 ** ADAM_STEP

    def body(w_ref, g_ref, m_ref, v_ref, d_ref, nm_ref, nv_ref):
        gv = g_ref[...]
        nm = ADAM_B1 * m_ref[...] + (1.0 - ADAM_B1) * gv
        nv = ADAM_B2 * v_ref[...] + (1.0 - ADAM_B2) * (gv * gv)
        nm_ref[...] = nm
        nv_ref[...] = nv
        d_ref[...] = -ADAM_LR * ((nm / c1) / (jnp.sqrt(nv / c2) + ADAM_EPS) + ADAM_WD * w_ref[...])

    spec = pl.BlockSpec((rb, C), lambda r: (r, 0))
    sds = jax.ShapeDtypeStruct(w.shape, F32)
    return pl.pallas_call(
        body, name=name, grid=(R // rb,), in_specs=[spec] * 4, out_specs=[spec] * 3,
        out_shape=[sds, sds, sds], compiler_params=_params(40, 1),
    )(w, g, m, v)


def _to_bf16(name, w):
    R, C = w.shape
    rb = _row_block(R)

    def body(w_ref, o_ref):
        o_ref[...] = w_ref[...].astype(BF16)

    spec = pl.BlockSpec((rb, C), lambda r: (r, 0))
    return pl.pallas_call(
        body, name=name, grid=(R // rb,), in_specs=[spec], out_specs=spec,
        out_shape=jax.ShapeDtypeStruct(w.shape, BF16), compiler_params=_params(32, 1),
    )(w)


def _rel_index():
    m = np.arange(2 * ATT_BLK)
    off = np.where(m < ATT_BLK, m, m - 2 * ATT_BLK)
    rel = np.stack([ATT_BLK * d - off for d in range(N_ATT_TILES)])
    return np.clip(rel, -MAX_REL, MAX_REL) + MAX_REL


def _local_step(x, tgt, g1, w_in, cw, cb, lg, lb, rel, w_out, g2, g3, w_up, fw, fb, w_down, g4):
    T = x.shape[0]
    idx = _rel_index()
    vec = jnp.transpose(rel[:, idx], (1, 0, 2)).reshape(N_ATT_TILES * N_HEADS, 1, 2 * ATT_BLK)
    bias = _bias_tiles(vec)

    u, a, qkv = _fwd_in_proj(x, g1, w_in)
    co, hc = _fwd_conv(a, cw, cb, lg, lb)
    ao, lse = _fwd_attn(qkv, bias)
    mixed, h1, u2 = _fwd_out_proj(co, ao, w_out, x, g2, g3)
    hf, fp = _fwd_ffn(u2, w_up, fw, fb, w_down)
    loss, dy, df, dg4 = _fwd_loss(fp, h1, tgt, g4)

    du2p, dhf, act, dfw_g, dfw_v = _bwd_ffn(df, hf, w_up, fw, fb, w_down)
    dh1, dmx, dco, dao, dg3, dg2 = _bwd_mid(du2p, dy, h1, mixed, g3, g2, w_out)
    dproj, dsacc = _bwd_attn(qkv, ao, dao, lse, bias)
    dproj, dcw, dcb, dlg, dlb = _bwd_conv(dproj, a, dco, hc, cw, lg, lb)
    gx, dg1 = _bwd_in_proj(dproj, w_in, x, dh1, g1)

    tk = 512
    gw_in = _wgrad(
        "wgrad_in", [u], pl.BlockSpec((tk, D_MODEL), lambda s, k: (k, 0)),
        dproj, pl.BlockSpec((tk, IN_SHARD), lambda s, k: (k, s)), None,
        pl.BlockSpec((1, D_MODEL, IN_SHARD), lambda s, k: (s, 0, 0)),
        jax.ShapeDtypeStruct((N_CHIPS, D_MODEL, IN_SHARD), F32), N_CHIPS, T, tk)
    gw_up = _wgrad(
        "wgrad_up", [u2], pl.BlockSpec((tk, D_MODEL), lambda s, k: (k, 0)),
        dhf, pl.BlockSpec((1, tk, FF_SHARD), lambda s, k: (s // 2, k, s % 2)), None,
        pl.BlockSpec((1, D_MODEL, FF_SHARD), lambda s, k: (s, 0, 0)),
        jax.ShapeDtypeStruct((N_CHIPS, D_MODEL, FF_SHARD), F32), N_CHIPS, T, tk)
    gw_down = _wgrad(
        "wgrad_down", [act], pl.BlockSpec((tk, FF_SHARD), lambda s, k: (k, s)),
        df, pl.BlockSpec((tk, D_MODEL), lambda s, k: (k, 0)), None,
        pl.BlockSpec((FF_SHARD, D_MODEL), lambda s, k: (s, 0)),
        jax.ShapeDtypeStruct((D_FF, D_MODEL), F32), 2, T, tk)
    gw_out = _wgrad(
        "wgrad_out", [co, ao], pl.BlockSpec((tk, CONV_W), lambda s, k: (k, 0)),
        dmx, pl.BlockSpec((tk, D_MODEL), lambda s, k: (k, 0)), None,
        pl.BlockSpec((CONV_W, D_MODEL), lambda s, k: (s, 0)),
        jax.ShapeDtypeStruct((D_MODEL, D_MODEL), F32), 2, T, tk, select=lambda s: s)

    diag = _diag_sums(dsacc).reshape(N_ATT_TILES, N_HEADS, 2 * ATT_BLK)
    onehot = np.zeros((N_ATT_TILES, 2 * ATT_BLK, 2 * MAX_REL + 1), np.float32)
    for d in range(N_ATT_TILES):
        onehot[d, np.arange(2 * ATT_BLK), idx[d]] = 1.0
    drel = jnp.einsum("dhm,dmr->hr", diag, jnp.asarray(onehot), precision=lax.Precision.HIGHEST)

    small = dict(norm_mix_pre=dg1, conv_dw_w=dcw[:CONV_K], conv_dw_b=dcb, conv_ln_g=dlg, conv_ln_b=dlb,
                 rel_bias=drel, norm_mix_post=dg2, norm_ffn_pre=dg3,
                 ffn_dw_w=jnp.concatenate([dfw_g[0, :3], dfw_g[1, :3], dfw_v[0, :3], dfw_v[1, :3]], axis=1),
                 ffn_dw_b=jnp.concatenate([dfw_g[0, 3:4], dfw_g[1, 3:4], dfw_v[0, 3:4], dfw_v[1, 3:4]], axis=1),
                 norm_ffn_post=dg4)
    big = dict(w_in=gw_in, w_out=gw_out.reshape(N_CHIPS, D_MODEL // N_CHIPS, D_MODEL), w_up=gw_up,
               w_down=gw_down.reshape(N_CHIPS, D_FF // N_CHIPS, D_MODEL))
    return loss, gx, small, big


SMALL_ORDER = ["norm_mix_pre", "conv_dw_b", "conv_ln_g", "conv_ln_b", "rel_bias", "norm_mix_post",
               "norm_ffn_pre", "ffn_dw_b", "norm_ffn_post", "conv_dw_w", "ffn_dw_w"]


def _pack(parts):
    rows = []
    for p in parts:
        width = -(-p.shape[1] // LANES) * LANES
        rows.append(jnp.pad(p, ((0, 0), (0, width - p.shape[1]))).reshape(-1, LANES))
    packed = jnp.concatenate(rows, axis=0)
    pad = -packed.shape[0] % 8
    return jnp.pad(packed, ((0, pad), (0, 0)))


def _unpack(packed, shapes):
    out, r = [], 0
    for shp in shapes:
        width = -(-shp[1] // LANES) * LANES
        n = shp[0] * width // LANES
        out.append(packed[r:r + n].reshape(shp[0], width)[:, :shp[1]])
        r += n
    return out


WEIGHTS = ["norm_mix_pre", "w_in", "conv_dw_w", "conv_dw_b", "conv_ln_g", "conv_ln_b", "rel_bias", "w_out",
           "norm_mix_post", "norm_ffn_pre", "w_up", "ffn_dw_w", "ffn_dw_b", "w_down", "norm_ffn_post"]
BIG = ["w_in", "w_out", "w_up", "w_down"]


def kernel(x, norm_mix_pre, w_in, conv_dw_w, conv_dw_b, conv_ln_g, conv_ln_b, rel_bias, w_out, norm_mix_post, norm_ffn_pre, w_up, ffn_dw_w, ffn_dw_b, w_down, norm_ffn_post, loss_target, m_norm_mix_pre, m_w_in, m_conv_dw_w, m_conv_dw_b, m_conv_ln_g, m_conv_ln_b, m_rel_bias, m_w_out, m_norm_mix_post, m_norm_ffn_pre, m_w_up, m_ffn_dw_w, m_ffn_dw_b, m_w_down, m_norm_ffn_post, v_norm_mix_pre, v_w_in, v_conv_dw_w, v_conv_dw_b, v_conv_ln_g, v_conv_ln_b, v_rel_bias, v_w_out, v_norm_mix_post, v_norm_ffn_pre, v_w_up, v_ffn_dw_w, v_ffn_dw_b, v_w_down, v_norm_ffn_post):
    args = locals()
    w = {n: args[n][0] for n in WEIGHTS}
    m = {n: args["m_" + n][0] for n in WEIGHTS}
    v = {n: args["v_" + n][0] for n in WEIGHTS}
    for d in (w, m, v):
        d["rel_bias"] = d["rel_bias"].reshape(N_HEADS, 2 * MAX_REL + 1)
        for n in ("norm_mix_pre", "conv_dw_b", "conv_ln_g", "conv_ln_b", "norm_mix_post", "norm_ffn_pre",
                  "ffn_dw_b", "norm_ffn_post"):
            d[n] = d[n].reshape(1, -1)
    shard = 2 * lax.axis_index("x") + lax.axis_index("y")

    cw_sh = jnp.pad(w["conv_dw_w"], ((0, CONV_HALO - CONV_K), (0, 0)))
    fw_sh = jnp.pad(w["ffn_dw_w"], ((0, FF_HALO - 3), (0, 0)))
    shards = [_to_bf16("cast_" + n, w[n]) for n in BIG] + [cw_sh, fw_sh]
    w_in_f, w_out_f, w_up_f, w_down_f, cw_f, fw_f = _all_gather_weights(shards)
    cw_full = jnp.transpose(cw_f, (1, 0, 2)).reshape(CONV_HALO, CONV_W)

    loss, gx, small, big = _local_step(
        x[0], loss_target[0], w["norm_mix_pre"], w_in_f, cw_full, w["conv_dw_b"], w["conv_ln_g"],
        w["conv_ln_b"], w["rel_bias"], w_out_f.reshape(D_MODEL, D_MODEL), w["norm_mix_post"],
        w["norm_ffn_pre"], w_up_f, fw_f, w["ffn_dw_b"].reshape(N_CHIPS, 1, FF_SHARD),
        w_down_f.reshape(D_FF, D_MODEL), w["norm_ffn_post"])

    mine, theirs = _pair_exchange([big[n] for n in BIG])
    pairs = [_add2("pair_sum_" + n, a, b) for n, a, b in zip(BIG, mine, theirs)]
    own, got = _chip_exchange(pairs)
    halves = [_add4("chip_sum_" + n, a, b) for n, a, b in zip(BIG, own, got)]
    full = _pair_gather(halves)
    grads, deltas, new_m, new_v = {}, {}, {}, {}
    for n, g in zip(BIG, full):
        grads[n] = g
        deltas[n], new_m[n], new_v[n] = _adamw("adamw_" + n, w[n], g, m[n], v[n])

    gsum = _all_reduce_small(_pack([small[n] for n in SMALL_ORDER]))
    shapes = [small[n].shape for n in SMALL_ORDER]
    gs = dict(zip(SMALL_ORDER, _unpack(gsum, shapes)))
    gs["conv_dw_w"] = lax.dynamic_slice_in_dim(gs["conv_dw_w"], shard * LANES, LANES, axis=1)
    gs["ffn_dw_w"] = lax.dynamic_slice_in_dim(gs["ffn_dw_w"], shard * FF_SHARD, FF_SHARD, axis=1)
    shapes = [gs[n].shape for n in SMALL_ORDER]
    d_p, m_p, v_p = _adamw("adamw_small", _pack([w[n] for n in SMALL_ORDER]), _pack([gs[n] for n in SMALL_ORDER]),
                           _pack([m[n] for n in SMALL_ORDER]), _pack([v[n] for n in SMALL_ORDER]))
    for dst, packed in ((deltas, d_p), (new_m, m_p), (new_v, v_p)):
        dst.update(zip(SMALL_ORDER, _unpack(packed, shapes)))
    grads.update(gs)

    total = lax.psum(loss[0, 0], ("x", "y", "c"))
    outs = [total, gx[None]]
    for group in (grads, deltas, new_m, new_v):
        outs += [group[n].reshape(args[n].shape) for n in WEIGHTS]
    return tuple(outs)
```

```python
import functools
import math

import numpy as np
import jax
import jax.numpy as jnp
from jax import lax
from jax.experimental import pallas as pl
from jax.experimental.pallas import tpu as pltpu

F32 = jnp.float32
BF16 = jnp.bfloat16

D_MODEL = 1024
CONV_W = 512
ATTN_W = 512
N_HEADS = 8
HEAD_DIM = 64
CHUNK = 64
N_LEFT = 8
MAX_REL = 128
CONV_K = 31
CONV_HALO = 32
D_FF = 2816
FF_SHARD = 1408
IN_COLS = 2560
IN_SHARD = 640
EPS = 1e-6
NEG_INF = -1e30
ATT_BLK = 256
N_ATT_TILES = 3
LANES = 128
N_CHIPS = 4

ADAM_LR = 0.001
ADAM_B1 = 0.9
ADAM_B2 = 0.999
ADAM_EPS = 1e-08
ADAM_WD = 0.01
ADAM_STEP = 10

MESH = pl.DeviceIdType.MESH
ANY = pl.BlockSpec(memory_space=pl.ANY)
VMEM_FULL = pl.BlockSpec(memory_space=pltpu.VMEM)


def _params(vmem_mb, n_grid=0):
    sem = ("arbitrary",) * n_grid if n_grid else None
    return pltpu.CompilerParams(dimension_semantics=sem, vmem_limit_bytes=vmem_mb << 20)


def _sigmoid(v):
    return 1.0 / (1.0 + jnp.exp(-v))


def _dot(a, b):
    return jnp.dot(a, b, preferred_element_type=F32)


def _dot_nt(a, b):
    return lax.dot_general(a, b, (((1,), (1,)), ((), ())), preferred_element_type=F32)


def _dot_tn(a, b):
    return lax.dot_general(a, b, (((0,), (0,)), ((), ())), preferred_element_type=F32)


def _rms_fwd(v, g):
    r = lax.rsqrt(jnp.mean(v * v, axis=-1, keepdims=True) + EPS)
    return v * r * g, r


def _rms_bwd(dy, v, g):
    r = lax.rsqrt(jnp.mean(v * v, axis=-1, keepdims=True) + EPS)
    vh = v * r
    dvh = dy * g
    dv = r * (dvh - vh * jnp.mean(dvh * vh, axis=-1, keepdims=True))
    return dv, dy * vh


def _fwd_in_proj(x, g1, w_in):
    T = x.shape[0]
    tm = 512

    def body(x_ref, g_ref, w_ref, u_ref, a_ref, qkv_ref):
        u, _ = _rms_fwd(x_ref[...], g_ref[...])
        u = u.astype(BF16)
        u_ref[...] = u
        for s in range(N_CHIPS):
            y = _dot(u, w_ref[s])
            lo, hi = IN_SHARD * s, IN_SHARD * (s + 1)
            if hi <= 1024:
                a_ref[:, lo:hi] = y
            elif lo >= 1024:
                qkv_ref[:, lo - 1024:hi - 1024] = y.astype(BF16)
            else:
                a_ref[:, lo:1024] = y[:, :1024 - lo]
                qkv_ref[:, 0:hi - 1024] = y[:, 1024 - lo:].astype(BF16)

    return pl.pallas_call(
        body, name="fwd_in_proj", grid=(T // tm,),
        in_specs=[pl.BlockSpec((tm, D_MODEL), lambda i: (i, 0)),
                  pl.BlockSpec((1, D_MODEL), lambda i: (0, 0)),
                  pl.BlockSpec((N_CHIPS, D_MODEL, IN_SHARD), lambda i: (0, 0, 0))],
        out_specs=[pl.BlockSpec((tm, D_MODEL), lambda i: (i, 0)),
                   pl.BlockSpec((tm, 1024), lambda i: (i, 0)),
                   pl.BlockSpec((tm, 1536), lambda i: (i, 0))],
        out_shape=[jax.ShapeDtypeStruct((T, D_MODEL), BF16),
                   jax.ShapeDtypeStruct((T, 1024), F32),
                   jax.ShapeDtypeStruct((T, 1536), BF16)],
        compiler_params=_params(40, 1),
    )(x, g1, w_in)


def _fwd_conv(a, cw, cb, lg, lb):
    T = a.shape[0]
    tm = 512
    rc = 64

    def body(a_ref, w_ref, b_ref, lg_ref, lb_ref, co_ref, hc_ref, hext):
        i = pl.program_id(0)

        @pl.when(i == 0)
        def _():
            hext[0:CONV_HALO, :] = jnp.zeros((CONV_HALO, CONV_W), F32)

        @pl.when(i > 0)
        def _():
            hext[0:CONV_HALO, :] = hext[tm:tm + CONV_HALO, :]

        hext[CONV_HALO:CONV_HALO + tm, :] = a_ref[:, :CONV_W] * _sigmoid(a_ref[:, CONV_W:])
        for c in range(tm // rc):
            acc = jnp.zeros((rc, CONV_W), F32)
            for k in range(CONV_K):
                acc = acc + w_ref[k:k + 1, :] * hext[c * rc + 2 + k:c * rc + 2 + k + rc, :]
            hc = acc + b_ref[...]
            hc_ref[c * rc:(c + 1) * rc, :] = hc
            mu = jnp.mean(hc, axis=-1, keepdims=True)
            xc = hc - mu
            var = jnp.mean(xc * xc, axis=-1, keepdims=True)
            z = xc * lax.rsqrt(var + EPS) * lg_ref[...] + lb_ref[...]
            co_ref[c * rc:(c + 1) * rc, :] = (z * _sigmoid(z)).astype(BF16)

    return pl.pallas_call(
        body, name="fwd_conv", grid=(T // tm,),
        in_specs=[pl.BlockSpec((tm, 1024), lambda i: (i, 0)),
                  pl.BlockSpec((CONV_HALO, CONV_W), lambda i: (0, 0)),
                  pl.BlockSpec((1, CONV_W), lambda i: (0, 0)),
                  pl.BlockSpec((1, CONV_W), lambda i: (0, 0)),
                  pl.BlockSpec((1, CONV_W), lambda i: (0, 0))],
        out_specs=[pl.BlockSpec((tm, CONV_W), lambda i: (i, 0)),
                   pl.BlockSpec((tm, CONV_W), lambda i: (i, 0))],
        out_shape=[jax.ShapeDtypeStruct((T, CONV_W), BF16),
                   jax.ShapeDtypeStruct((T, CONV_W), F32)],
        scratch_shapes=[pltpu.VMEM((tm + CONV_HALO, CONV_W), F32)],
        compiler_params=_params(32, 1),
    )(a, cw, cb, lg, lb)


def _row_skew(v, sign):
    rows, width = v.shape
    row = lax.broadcasted_iota(jnp.int32, (rows, 1), 0)
    for b in range(int(math.log2(rows))):
        shift = (1 << b) if sign > 0 else width - (1 << b)
        v = jnp.where(((row >> b) & 1) == 1, pltpu.roll(v, shift, 1), v)
    return v


def _att_visible(d):
    rq = lax.broadcasted_iota(jnp.int32, (ATT_BLK, ATT_BLK), 0) // CHUNK
    ck = lax.broadcasted_iota(jnp.int32, (ATT_BLK, ATT_BLK), 1) // CHUNK
    slack = ATT_BLK
    above = jnp.where(d == 0, 0, slack)
    below = jnp.where(d == 2, 0, slack)
    return (ck <= rq + above) & (ck >= rq - below)


def _bias_tiles(vec):
    def body(v_ref, o_ref):
        d = pl.program_id(0) // N_HEADS
        full = _row_skew(jnp.broadcast_to(v_ref[0], (ATT_BLK, 2 * ATT_BLK)), 1)
        o_ref[0] = jnp.where(_att_visible(d), full[:, :ATT_BLK], NEG_INF)

    return pl.pallas_call(
        body, name="bias_tiles", grid=(N_ATT_TILES * N_HEADS,),
        in_specs=[pl.BlockSpec((1, 1, 2 * ATT_BLK), lambda n: (n, 0, 0))],
        out_specs=pl.BlockSpec((1, ATT_BLK, ATT_BLK), lambda n: (n, 0, 0)),
        out_shape=jax.ShapeDtypeStruct((N_ATT_TILES * N_HEADS, ATT_BLK, ATT_BLK), F32),
        compiler_params=_params(16, 1),
    )(vec)


def _diag_sums(ds):
    def body(d_ref, o_ref):
        wide = jnp.concatenate([d_ref[0], jnp.zeros((ATT_BLK, ATT_BLK), F32)], axis=1)
        o_ref[0] = jnp.sum(_row_skew(wide, -1), axis=0, keepdims=True)

    return pl.pallas_call(
        body, name="diag_sums", grid=(N_ATT_TILES * N_HEADS,),
        in_specs=[pl.BlockSpec((1, ATT_BLK, ATT_BLK), lambda n: (n, 0, 0))],
        out_specs=pl.BlockSpec((1, 1, 2 * ATT_BLK), lambda n: (n, 0, 0)),
        out_shape=jax.ShapeDtypeStruct((N_ATT_TILES * N_HEADS, 1, 2 * ATT_BLK), F32),
        compiler_params=_params(16, 1),
    )(ds)


def _head_mask(h):
    lane = lax.broadcasted_iota(jnp.int32, (1, LANES), 1)
    return (lane // HEAD_DIM) == (h % 2)


def _fwd_attn(qkv, bias):
    T = qkv.shape[0]
    nb = T // ATT_BLK
    scale = HEAD_DIM ** -0.5

    def body(q_ref, k0_ref, k1_ref, k2_ref, v0_ref, v1_ref, v2_ref, b_ref, o_ref, lse_ref):
        i = pl.program_id(0)
        k_refs = (k0_ref, k1_ref, k2_ref)
        v_refs = (v0_ref, v1_ref, v2_ref)
        lane = lax.broadcasted_iota(jnp.int32, (1, LANES), 1)
        lse_tile = jnp.zeros((ATT_BLK, LANES), F32)
        for g in range(N_HEADS // 2):
            cols = slice(g * LANES, (g + 1) * LANES)
            qg = q_ref[:, cols]
            og = jnp.zeros((ATT_BLK, LANES), F32)
            for h in (2 * g, 2 * g + 1):
                hm = _head_mask(h)
                qh = jnp.where(hm, qg, jnp.zeros_like(qg))
                s = []
                for d in range(N_ATT_TILES):
                    sd = _dot_nt(qh, k_refs[d][:, cols]) * scale + b_ref[d * N_HEADS + h]
                    if d > 0:
                        sd = jnp.where(i >= d, sd, NEG_INF)
                    s.append(sd)
                m = jnp.maximum(jnp.maximum(jnp.max(s[0], axis=-1, keepdims=True),
                                            jnp.max(s[1], axis=-1, keepdims=True)),
                                jnp.max(s[2], axis=-1, keepdims=True))
                p = [jnp.exp(sd - m) for sd in s]
                l = (jnp.sum(p[0], axis=-1, keepdims=True) + jnp.sum(p[1], axis=-1, keepdims=True)
                     + jnp.sum(p[2], axis=-1, keepdims=True))
                oh = jnp.zeros((ATT_BLK, LANES), F32)
                for d in range(N_ATT_TILES):
                    vg = v_refs[d][:, cols]
                    oh = oh + _dot(p[d].astype(BF16), jnp.where(hm, vg, jnp.zeros_like(vg)))
                og = og + oh / l
                lse_tile = jnp.where(lane == h, m + jnp.log(l), lse_tile)
            o_ref[:, cols] = og.astype(BF16)
        lse_ref[...] = lse_tile

    def kv_spec(d, col):
        return pl.BlockSpec((ATT_BLK, ATTN_W), lambda i: (jnp.maximum(i - d, 0), col))

    return pl.pallas_call(
        body, name="fwd_attn", grid=(nb,),
        in_specs=[pl.BlockSpec((ATT_BLK, ATTN_W), lambda i: (i, 0)),
                  kv_spec(0, 1), kv_spec(1, 1), kv_spec(2, 1),
                  kv_spec(0, 2), kv_spec(1, 2), kv_spec(2, 2),
                  pl.BlockSpec((N_ATT_TILES * N_HEADS, ATT_BLK, ATT_BLK), lambda i: (0, 0, 0))],
        out_specs=[pl.BlockSpec((ATT_BLK, ATTN_W), lambda i: (i, 0)),
                   pl.BlockSpec((ATT_BLK, LANES), lambda i: (i, 0))],
        out_shape=[jax.ShapeDtypeStruct((T, ATTN_W), BF16),
                   jax.ShapeDtypeStruct((T, LANES), F32)],
        compiler_params=_params(40, 1),
    )(qkv, qkv, qkv, qkv, qkv, qkv, qkv, bias)


def _fwd_out_proj(co, ao, w_out, x, g2, g3):
    T = x.shape[0]
    tm = 512

    def body(co_ref, ao_ref, w_ref, x_ref, g2_ref, g3_ref, mixed_ref, h1_ref, u2_ref):
        mixed = _dot(co_ref[...], w_ref[0:CONV_W, :]) + _dot(ao_ref[...], w_ref[CONV_W:, :])
        mixed_ref[...] = mixed
        y, _ = _rms_fwd(mixed, g2_ref[...])
        h1 = x_ref[...] + y
        h1_ref[...] = h1
        u2, _ = _rms_fwd(h1, g3_ref[...])
        u2_ref[...] = u2.astype(BF16)

    row = lambda w: pl.BlockSpec((tm, w), lambda i: (i, 0))
    vec = pl.BlockSpec((1, D_MODEL), lambda i: (0, 0))
    return pl.pallas_call(
        body, name="fwd_out_proj", grid=(T // tm,),
        in_specs=[row(CONV_W), row(ATTN_W), pl.BlockSpec((D_MODEL, D_MODEL), lambda i: (0, 0)),
                  row(D_MODEL), vec, vec],
        out_specs=[row(D_MODEL), row(D_MODEL), row(D_MODEL)],
        out_shape=[jax.ShapeDtypeStruct((T, D_MODEL), F32),
                   jax.ShapeDtypeStruct((T, D_MODEL), F32),
                   jax.ShapeDtypeStruct((T, D_MODEL), BF16)],
        compiler_params=_params(40, 1),
    )(co, ao, w_out, x, g2, g3)


GELU_C = math.sqrt(2.0 / math.pi)
GELU_A = 0.044715


def _gelu_and_grad(v):
    th = jnp.tanh(GELU_C * (v + GELU_A * v * v * v))
    gl = 0.5 * v * (1.0 + th)
    dgl = 0.5 * (1.0 + th) + 0.5 * v * (1.0 - th * th) * (GELU_C * (1.0 + 3.0 * GELU_A * v * v))
    return gl, dgl


FF_TM = 256
FF_HALO = 16


def _fwd_ffn(u2, w_up, fw, fb, w_down):
    T = u2.shape[0]
    tm = FF_TM

    def body(u_ref, wg_ref, wv_ref, fwg_ref, fwv_ref, fbg_ref, fbv_ref, wd_ref, hf_ref, f_ref, extg, extv):
        i = pl.program_id(1)
        u = u_ref[...]
        hs = (_dot(u, wg_ref[0]), _dot(u, wv_ref[0]))
        conv = []
        for n, (ext, fw_ref, fb_ref) in enumerate(((extg, fwg_ref, fbg_ref), (extv, fwv_ref, fbv_ref))):
            hf_ref[n] = hs[n].astype(BF16)

            @pl.when(i == 0)
            def _():
                ext[0:8, :] = jnp.zeros((8, FF_SHARD), F32)

            @pl.when(i > 0)
            def _():
                ext[0:8, :] = ext[tm:tm + 8, :]

            ext[8:8 + tm, :] = hs[n]
            conv.append(fw_ref[0, 0:1, :] * ext[6:6 + tm, :] + fw_ref[0, 1:2, :] * ext[7:7 + tm, :]
                        + fw_ref[0, 2:3, :] * hs[n] + fb_ref[0])
        gl, _ = _gelu_and_grad(conv[0])
        f_ref[0] = _dot((gl * conv[1]).astype(BF16), wd_ref[...])

    wspec = lambda off: pl.BlockSpec((1, D_MODEL, FF_SHARD), lambda s, i: (s + off, 0, 0))
    fwspec = lambda off: pl.BlockSpec((1, FF_HALO, FF_SHARD), lambda s, i: (s + off, 0, 0))
    fbspec = lambda off: pl.BlockSpec((1, 1, FF_SHARD), lambda s, i: (s + off, 0, 0))
    return pl.pallas_call(
        body, name="fwd_ffn", grid=(2, T // tm),
        in_specs=[pl.BlockSpec((tm, D_MODEL), lambda s, i: (i, 0)),
                  wspec(0), wspec(2), fwspec(0), fwspec(2), fbspec(0), fbspec(2),
                  pl.BlockSpec((FF_SHARD, D_MODEL), lambda s, i: (s, 0))],
        out_specs=[pl.BlockSpec((2, tm, FF_SHARD), lambda s, i: (0, i, s)),
                   pl.BlockSpec((1, tm, D_MODEL), lambda s, i: (s, i, 0))],
        out_shape=[jax.ShapeDtypeStruct((2, T, D_FF), BF16),
                   jax.ShapeDtypeStruct((2, T, D_MODEL), F32)],
        scratch_shapes=[pltpu.VMEM((tm + 8, FF_SHARD), F32), pltpu.VMEM((tm + 8, FF_SHARD), F32)],
        compiler_params=_params(48, 2),
    )(u2, w_up, w_up, fw, fw, fb, fb, w_down)


def _fwd_loss(fp, h1, tgt, g4):
    T = h1.shape[0]
    tm = 512

    def body(fp_ref, h1_ref, t_ref, g_ref, loss_ref, dy_ref, df_ref, dg_ref):
        i = pl.program_id(0)
        f = fp_ref[0] + fp_ref[1]
        r, _ = _rms_fwd(f, g_ref[...])
        e = (h1_ref[...] + r) - t_ref[...]
        dy = e * (1.0 / D_MODEL)
        dy_ref[...] = dy
        df, dg_rows = _rms_bwd(dy, f, g_ref[...])
        df_ref[...] = df.astype(BF16)
        part = 0.5 * jnp.sum(jnp.mean(e * e, axis=-1, keepdims=True), axis=0, keepdims=True)
        dg = jnp.sum(dg_rows, axis=0, keepdims=True)

        @pl.when(i == 0)
        def _():
            loss_ref[...] = part
            dg_ref[...] = dg

        @pl.when(i > 0)
        def _():
            loss_ref[...] += part
            dg_ref[...] += dg

    row = pl.BlockSpec((tm, D_MODEL), lambda i: (i, 0))
    vec = pl.BlockSpec((1, D_MODEL), lambda i: (0, 0))
    return pl.pallas_call(
        body, name="fwd_loss", grid=(T // tm,),
        in_specs=[pl.BlockSpec((2, tm, D_MODEL), lambda i: (0, i, 0)), row, row, vec],
        out_specs=[pl.BlockSpec((1, 1), lambda i: (0, 0)), row, row, vec],
        out_shape=[jax.ShapeDtypeStruct((1, 1), F32),
                   jax.ShapeDtypeStruct((T, D_MODEL), F32),
                   jax.ShapeDtypeStruct((T, D_MODEL), BF16),
                   jax.ShapeDtypeStruct((1, D_MODEL), F32)],
        compiler_params=_params(40, 1),
    )(fp, h1, tgt, g4)


def _bwd_ffn(df, hf, w_up, fw, fb, w_down):
    T = df.shape[0]
    tm = FF_TM
    ni = T // tm

    def body(df_ref, hf_ref, halo_ref, wd_ref, wg_ref, wv_ref, fwg_ref, fwv_ref, fbg_ref, fbv_ref,
             du_ref, dhf_ref, act_ref, dwg_ref, dwv_ref, extg, extv, dextg, dextv, carg, carv):
        i = pl.program_id(1)
        ri = ni - 1 - i
        pre, pre1, pre2 = [], [], []
        for n, (ext, fw_ref, fb_ref) in enumerate(((extg, fwg_ref, fbg_ref), (extv, fwv_ref, fbv_ref))):
            h0 = hf_ref[n].astype(F32)
            ext[FF_HALO:FF_HALO + tm, :] = h0
            ext[0:FF_HALO, :] = jnp.where(ri > 0, halo_ref[n].astype(F32), 0.0)
            h1 = ext[FF_HALO - 1:FF_HALO - 1 + tm, :]
            h2 = ext[FF_HALO - 2:FF_HALO - 2 + tm, :]
            pre.append(fw_ref[0, 0:1, :] * h2 + fw_ref[0, 1:2, :] * h1 + fw_ref[0, 2:3, :] * h0 + fb_ref[0])
            pre1.append(h1)
            pre2.append(h2)
        dact = _dot_nt(df_ref[...], wd_ref[...])
        gl, dgl = _gelu_and_grad(pre[0])
        act_ref[...] = (gl * pre[1]).astype(BF16)
        dpre = (dact * pre[1] * dgl, dact * gl)
        du = None
        for n, (dext, car, fw_ref, dw_ref, w_ref) in enumerate(
                ((dextg, carg, fwg_ref, dwg_ref, wg_ref), (dextv, carv, fwv_ref, dwv_ref, wv_ref))):
            dp = dpre[n]
            h0 = hf_ref[n].astype(F32)
            rows = (jnp.sum(dp * pre2[n], axis=0, keepdims=True), jnp.sum(dp * pre1[n], axis=0, keepdims=True),
                    jnp.sum(dp * h0, axis=0, keepdims=True), jnp.sum(dp, axis=0, keepdims=True))

            @pl.when(i == 0)
            def _():
                dw_ref[0] = jnp.zeros((FF_HALO, FF_SHARD), F32)
                car[...] = jnp.zeros((8, FF_SHARD), F32)

            for k in range(4):
                dw_ref[0, k:k + 1, :] += rows[k]
            dext[0:tm, :] = dp
            dext[tm:tm + 8, :] = car[...]
            car[...] = dp[0:8, :]
            dh = (fw_ref[0, 2:3, :] * dp + fw_ref[0, 1:2, :] * dext[1:1 + tm, :]
                  + fw_ref[0, 0:1, :] * dext[2:2 + tm, :]).astype(BF16)
            dhf_ref[n] = dh
            term = _dot_nt(dh, w_ref[0])
            du = term if du is None else du + term
        du_ref[0] = du

    rev = lambda i: ni - 1 - i
    wspec = lambda off: pl.BlockSpec((1, D_MODEL, FF_SHARD), lambda s, i: (s + off, 0, 0))
    fwspec = lambda off: pl.BlockSpec((1, FF_HALO, FF_SHARD), lambda s, i: (s + off, 0, 0))
    fbspec = lambda off: pl.BlockSpec((1, 1, FF_SHARD), lambda s, i: (s + off, 0, 0))
    halo_blocks = tm // FF_HALO
    dwspec = pl.BlockSpec((1, FF_HALO, FF_SHARD), lambda s, i: (s, 0, 0))
    return pl.pallas_call(
        body, name="bwd_ffn", grid=(2, ni),
        in_specs=[pl.BlockSpec((tm, D_MODEL), lambda s, i: (rev(i), 0)),
                  pl.BlockSpec((2, tm, FF_SHARD), lambda s, i: (0, rev(i), s)),
                  pl.BlockSpec((2, FF_HALO, FF_SHARD),
                               lambda s, i: (0, jnp.maximum(rev(i) * halo_blocks - 1, 0), s)),
                  pl.BlockSpec((FF_SHARD, D_MODEL), lambda s, i: (s, 0)),
                  wspec(0), wspec(2), fwspec(0), fwspec(2), fbspec(0), fbspec(2)],
        out_specs=[pl.BlockSpec((1, tm, D_MODEL), lambda s, i: (s, rev(i), 0)),
                   pl.BlockSpec((2, tm, FF_SHARD), lambda s, i: (0, rev(i), s)),
                   pl.BlockSpec((tm, FF_SHARD), lambda s, i: (rev(i), s)),
                   dwspec, dwspec],
        out_shape=[jax.ShapeDtypeStruct((2, T, D_MODEL), F32),
                   jax.ShapeDtypeStruct((2, T, D_FF), BF16),
                   jax.ShapeDtypeStruct((T, D_FF), BF16),
                   jax.ShapeDtypeStruct((2, FF_HALO, FF_SHARD), F32),
                   jax.ShapeDtypeStruct((2, FF_HALO, FF_SHARD), F32)],
        scratch_shapes=[pltpu.VMEM((tm + FF_HALO, FF_SHARD), F32), pltpu.VMEM((tm + FF_HALO, FF_SHARD), F32),
                        pltpu.VMEM((tm + 8, FF_SHARD), F32), pltpu.VMEM((tm + 8, FF_SHARD), F32),
                        pltpu.VMEM((8, FF_SHARD), F32), pltpu.VMEM((8, FF_SHARD), F32)],
        compiler_params=_params(56, 2),
    )(df, hf, hf, w_down, w_up, w_up, fw, fw, fb, fb)


def _bwd_mid(du2p, dy, h1, mixed, g3, g2, w_out):
    T = dy.shape[0]
    tm = 512

    def body(du_ref, dy_ref, h1_ref, mx_ref, g3_ref, g2_ref, w_ref,
             dh1_ref, dmx_ref, dco_ref, dao_ref, dg3_ref, dg2_ref):
        i = pl.program_id(0)
        dres, dg3_rows = _rms_bwd(du_ref[0] + du_ref[1], h1_ref[...], g3_ref[...])
        dh1 = dy_ref[...] + dres
        dh1_ref[...] = dh1
        dmx, dg2_rows = _rms_bwd(dh1, mx_ref[...], g2_ref[...])
        dmx = dmx.astype(BF16)
        dmx_ref[...] = dmx
        dcat = _dot_nt(dmx, w_ref[...])
        dco_ref[...] = dcat[:, :CONV_W]
        dao_ref[...] = dcat[:, CONV_W:].astype(BF16)
        dg3 = jnp.sum(dg3_rows, axis=0, keepdims=True)
        dg2 = jnp.sum(dg2_rows, axis=0, keepdims=True)

        @pl.when(i == 0)
        def _():
            dg3_ref[...] = dg3
            dg2_ref[...] = dg2

        @pl.when(i > 0)
        def _():
            dg3_ref[...] += dg3
            dg2_ref[...] += dg2

    row = lambda w: pl.BlockSpec((tm, w), lambda i: (i, 0))
    vec = pl.BlockSpec((1, D_MODEL), lambda i: (0, 0))
    return pl.pallas_call(
        body, name="bwd_mid", grid=(T // tm,),
        in_specs=[pl.BlockSpec((2, tm, D_MODEL), lambda i: (0, i, 0)), row(D_MODEL), row(D_MODEL),
                  row(D_MODEL), vec, vec, pl.BlockSpec((D_MODEL, D_MODEL), lambda i: (0, 0))],
        out_specs=[row(D_MODEL), row(D_MODEL), row(CONV_W), row(ATTN_W), vec, vec],
        out_shape=[jax.ShapeDtypeStruct((T, D_MODEL), F32),
                   jax.ShapeDtypeStruct((T, D_MODEL), BF16),
                   jax.ShapeDtypeStruct((T, CONV_W), F32),
                   jax.ShapeDtypeStruct((T, ATTN_W), BF16),
                   jax.ShapeDtypeStruct((1, D_MODEL), F32),
                   jax.ShapeDtypeStruct((1, D_MODEL), F32)],
        compiler_params=_params(48, 1),
    )(du2p, dy, h1, mixed, g3, g2, w_out)


def _bwd_attn(qkv, ao, dao, lse, bias):
    T = qkv.shape[0]
    nb = T // ATT_BLK
    scale = HEAD_DIM ** -0.5

    def body(k_ref, v_ref, q0, q1, q2, do0, do1, do2, o0, o1, o2, l0, l1, l2, b_ref,
             dp_ref, ds_ref, acc1, acc2):
        j = pl.program_id(0)
        q_refs, do_refs, o_refs, l_refs = (q0, q1, q2), (do0, do1, do2), (o0, o1, o2), (l0, l1, l2)

        @pl.when(j == 0)
        def _():
            ds_ref[...] = jnp.zeros(ds_ref.shape, F32)
            acc1[...] = jnp.zeros(acc1.shape, F32)
            acc2[...] = jnp.zeros(acc2.shape, F32)

        dq_new = [[], [], []]
        dk_cols, dv_cols = [], []
        for g in range(N_HEADS // 2):
            cols = slice(g * LANES, (g + 1) * LANES)
            kg = k_ref[:, cols]
            vg = v_ref[:, cols]
            dkg = jnp.zeros((ATT_BLK, LANES), F32)
            dvg = jnp.zeros((ATT_BLK, LANES), F32)
            dqg = [jnp.zeros((ATT_BLK, LANES), F32) for _ in range(N_ATT_TILES)]
            for d in range(N_ATT_TILES):
                qg = q_refs[d][:, cols]
                dog = do_refs[d][:, cols]
                prod = dog.astype(F32) * o_refs[d][:, cols].astype(F32)
                for h in (2 * g, 2 * g + 1):
                    hm = _head_mask(h)
                    qh = jnp.where(hm, qg, jnp.zeros_like(qg))
                    doh = jnp.where(hm, dog, jnp.zeros_like(dog))
                    kh = jnp.where(hm, kg, jnp.zeros_like(kg))
                    delta = jnp.sum(jnp.where(hm, prod, 0.0), axis=-1, keepdims=True)
                    s = _dot_nt(qh, kg) * scale + b_ref[d * N_HEADS + h]
                    p = jnp.exp(s - l_refs[d][:, h:h + 1])
                    p = jnp.where(j + d < nb, p, 0.0)
                    dvg = dvg + _dot_tn(p.astype(BF16), doh)
                    dpm = _dot_nt(doh, vg)
                    dsc = p * (dpm - delta)
                    ds_ref[d * N_HEADS + h] += dsc
                    dsb = (dsc * scale).astype(BF16)
                    dqg[d] = dqg[d] + _dot(dsb, kh)
                    dkg = dkg + _dot_tn(dsb, qh)
            for d in range(N_ATT_TILES):
                dq_new[d].append(dqg[d])
            dk_cols.append(dkg)
            dv_cols.append(dvg)
        x0, x1, x2 = (jnp.concatenate(c, axis=1) for c in dq_new)
        dp_ref[:, 0:1024] = jnp.zeros((ATT_BLK, 1024), BF16)
        dp_ref[:, 1024:1536] = (acc1[...] + x0).astype(BF16)
        dp_ref[:, 1536:2048] = jnp.concatenate(dk_cols, axis=1).astype(BF16)
        dp_ref[:, 2048:2560] = jnp.concatenate(dv_cols, axis=1).astype(BF16)
        acc1[...] = acc2[...] + x1
        acc2[...] = x2

    def fwd_spec(d, width, col):
        return pl.BlockSpec((ATT_BLK, width), lambda j: (jnp.minimum(j + d, nb - 1), col))

    return pl.pallas_call(
        body, name="bwd_attn", grid=(nb,),
        in_specs=[pl.BlockSpec((ATT_BLK, ATTN_W), lambda j: (j, 1)),
                  pl.BlockSpec((ATT_BLK, ATTN_W), lambda j: (j, 2)),
                  fwd_spec(0, ATTN_W, 0), fwd_spec(1, ATTN_W, 0), fwd_spec(2, ATTN_W, 0),
                  fwd_spec(0, ATTN_W, 0), fwd_spec(1, ATTN_W, 0), fwd_spec(2, ATTN_W, 0),
                  fwd_spec(0, ATTN_W, 0), fwd_spec(1, ATTN_W, 0), fwd_spec(2, ATTN_W, 0),
                  fwd_spec(0, LANES, 0), fwd_spec(1, LANES, 0), fwd_spec(2, LANES, 0),
                  pl.BlockSpec((N_ATT_TILES * N_HEADS, ATT_BLK, ATT_BLK), lambda j: (0, 0, 0))],
        out_specs=[pl.BlockSpec((ATT_BLK, IN_COLS), lambda j: (j, 0)),
                   pl.BlockSpec((N_ATT_TILES * N_HEADS, ATT_BLK, ATT_BLK), lambda j: (0, 0, 0))],
        out_shape=[jax.ShapeDtypeStruct((T, IN_COLS), BF16),
                   jax.ShapeDtypeStruct((N_ATT_TILES * N_HEADS, ATT_BLK, ATT_BLK), F32)],
        scratch_shapes=[pltpu.VMEM((ATT_BLK, ATTN_W), F32), pltpu.VMEM((ATT_BLK, ATTN_W), F32)],
        compiler_params=_params(56, 1),
    )(qkv, qkv, qkv, qkv, qkv, dao, dao, dao, ao, ao, ao, lse, lse, lse, bias)


def _bwd_conv(dproj, a, dco, hc, cw, lg, lb):
    T = a.shape[0]
    tm = 512
    rc = 64
    ni = T // tm
    hb = tm // CONV_HALO

    def body(dp_in, a_ref, ap_ref, dco_ref, dcon_ref, hc_ref, hcn_ref, w_ref, lg_ref, lb_ref,
             dp_ref, dw_ref, db_ref, dlg_ref, dlb_ref, hext, dext):
        del dp_in
        i = pl.program_id(0)

        def ln_bwd(dco_v, hc_v):
            mu = jnp.mean(hc_v, axis=-1, keepdims=True)
            xc = hc_v - mu
            rstd = lax.rsqrt(jnp.mean(xc * xc, axis=-1, keepdims=True) + EPS)
            xh = xc * rstd
            z = xh * lg_ref[...] + lb_ref[...]
            sg = _sigmoid(z)
            dz = dco_v * (sg * (1.0 + z * (1.0 - sg)))
            dxh = dz * lg_ref[...]
            dhc = rstd * (dxh - jnp.mean(dxh, axis=-1, keepdims=True)
                          - xh * jnp.mean(dxh * xh, axis=-1, keepdims=True))
            return dhc, dz * xh, dz

        hext[0:CONV_HALO, :] = jnp.where(i > 0, ap_ref[:, :CONV_W] * _sigmoid(ap_ref[:, CONV_W:]), 0.0)
        hext[CONV_HALO:CONV_HALO + tm, :] = a_ref[:, :CONV_W] * _sigmoid(a_ref[:, CONV_W:])
        dhc, dlg_rows, dlb_rows = ln_bwd(dco_ref[...], hc_ref[...])
        dext[0:tm, :] = dhc
        dhc_next, _, _ = ln_bwd(dcon_ref[...], hcn_ref[...])
        dext[tm:tm + CONV_HALO, :] = jnp.where(i < ni - 1, dhc_next, 0.0)

        @pl.when(i == 0)
        def _():
            dw_ref[...] = jnp.zeros(dw_ref.shape, F32)
            db_ref[...] = jnp.zeros(db_ref.shape, F32)
            dlg_ref[...] = jnp.zeros(dlg_ref.shape, F32)
            dlb_ref[...] = jnp.zeros(dlb_ref.shape, F32)

        db_ref[...] += jnp.sum(dhc, axis=0, keepdims=True)
        dlg_ref[...] += jnp.sum(dlg_rows, axis=0, keepdims=True)
        dlb_ref[...] += jnp.sum(dlb_rows, axis=0, keepdims=True)
        dw_rows = [jnp.zeros((1, CONV_W), F32) for _ in range(CONV_K)]
        for c in range(tm // rc):
            r0 = c * rc
            dh = jnp.zeros((rc, CONV_W), F32)
            dhc_c = dext[r0:r0 + rc, :]
            for k in range(CONV_K):
                dh = dh + w_ref[k:k + 1, :] * dext[r0 + 30 - k:r0 + 30 - k + rc, :]
                dw_rows[k] = dw_rows[k] + jnp.sum(dhc_c * hext[r0 + 2 + k:r0 + 2 + k + rc, :],
                                                  axis=0, keepdims=True)
            av = a_ref[r0:r0 + rc, :CONV_W]
            sg = _sigmoid(a_ref[r0:r0 + rc, CONV_W:])
            dp_ref[r0:r0 + rc, 0:CONV_W] = (dh * sg).astype(BF16)
            dp_ref[r0:r0 + rc, CONV_W:] = (dh * av * sg * (1.0 - sg)).astype(BF16)
        for k in range(CONV_K):
            dw_ref[k:k + 1, :] += dw_rows[k]

    row = lambda w: pl.BlockSpec((tm, w), lambda i: (i, 0))
    prev = lambda w: pl.BlockSpec((CONV_HALO, w), lambda i: (jnp.maximum(i * hb - 1, 0), 0))
    nxt = lambda w: pl.BlockSpec((CONV_HALO, w), lambda i: (jnp.minimum((i + 1) * hb, ni * hb - 1), 0))
    vec = pl.BlockSpec((1, CONV_W), lambda i: (0, 0))
    return pl.pallas_call(
        body, name="bwd_conv", grid=(ni,),
        in_specs=[ANY, row(1024), prev(1024), row(CONV_W), nxt(CONV_W), row(CONV_W), nxt(CONV_W),
                  pl.BlockSpec((CONV_HALO, CONV_W), lambda i: (0, 0)), vec, vec],
        out_specs=[pl.BlockSpec((tm, 1024), lambda i: (i, 0)),
                   pl.BlockSpec((CONV_HALO, CONV_W), lambda i: (0, 0)), vec, vec, vec],
        out_shape=[jax.ShapeDtypeStruct((T, IN_COLS), BF16),
                   jax.ShapeDtypeStruct((CONV_HALO, CONV_W), F32),
                   jax.ShapeDtypeStruct((1, CONV_W), F32),
                   jax.ShapeDtypeStruct((1, CONV_W), F32),
                   jax.ShapeDtypeStruct((1, CONV_W), F32)],
        scratch_shapes=[pltpu.VMEM((tm + CONV_HALO, CONV_W), F32), pltpu.VMEM((tm + CONV_HALO, CONV_W), F32)],
        input_output_aliases={0: 0},
        compiler_params=_params(40, 1),
    )(dproj, a, a, dco, dco, hc, hc, cw, lg, lb)


def _bwd_in_proj(dproj, w_in, x, dh1, g1):
    T = x.shape[0]
    tm = 512

    def body(dp_ref, w_ref, x_ref, dh_ref, g_ref, gx_ref, dg_ref):
        i = pl.program_id(0)
        du = None
        for s in range(N_CHIPS):
            term = _dot_nt(dp_ref[:, IN_SHARD * s:IN_SHARD * (s + 1)], w_ref[s])
            du = term if du is None else du + term
        dx, dg_rows = _rms_bwd(du, x_ref[...], g_ref[...])
        gx_ref[...] = dh_ref[...] + dx
        dg = jnp.sum(dg_rows, axis=0, keepdims=True)

        @pl.when(i == 0)
        def _():
            dg_ref[...] = dg

        @pl.when(i > 0)
        def _():
            dg_ref[...] += dg

    row = lambda w: pl.BlockSpec((tm, w), lambda i: (i, 0))
    vec = pl.BlockSpec((1, D_MODEL), lambda i: (0, 0))
    return pl.pallas_call(
        body, name="bwd_in_proj", grid=(T // tm,),
        in_specs=[row(IN_COLS), pl.BlockSpec((N_CHIPS, D_MODEL, IN_SHARD), lambda i: (0, 0, 0)),
                  row(D_MODEL), row(D_MODEL), vec],
        out_specs=[row(D_MODEL), vec],
        out_shape=[jax.ShapeDtypeStruct((T, D_MODEL), F32), jax.ShapeDtypeStruct((1, D_MODEL), F32)],
        compiler_params=_params(40, 1),
    )(dproj, w_in, x, dh1, g1)


def _wgrad(name, a_list, a_spec, b, b_spec, out_block, out_spec, out_shape, n_outer, T, tk=512, select=None):
    def body(*refs):
        a_refs, b_ref, o_ref = refs[:len(a_list)], refs[len(a_list)], refs[len(a_list) + 1]
        kt = pl.program_id(1)

        @pl.when(kt == 0)
        def _():
            o_ref[...] = jnp.zeros(o_ref.shape, F32)

        bv = b_ref[...].reshape(b_ref.shape[-2:])
        if select is None:
            o_ref[...] += _dot_tn(a_refs[0][...].reshape(a_refs[0].shape[-2:]), bv).reshape(o_ref.shape)
        else:
            for n, a_ref in enumerate(a_refs):
                @pl.when(select(pl.program_id(0)) == n)
                def _():
                    o_ref[...] += _dot_tn(a_ref[...], bv).reshape(o_ref.shape)

    del out_block
    return pl.pallas_call(
        body, name=name, grid=(n_outer, T // tk),
        in_specs=[a_spec] * len(a_list) + [b_spec],
        out_specs=out_spec, out_shape=out_shape,
        compiler_params=_params(48, 2),
    )(*a_list, b)


def _mesh_pos():
    return lax.axis_index("x"), lax.axis_index("y"), lax.axis_index("c")


def _other_chips(x, y):
    return [((1 - x, y), 2 * (1 - x) + y), ((x, 1 - y), 2 * x + (1 - y)), ((1 - x, 1 - y), 2 * (1 - x) + (1 - y))]


def _all_gather_weights(shards):
    n = len(shards)

    def body(*refs):
        out_refs = refs[n:2 * n]
        send_sems, recv_sems = refs[2 * n:]
        x, y, c = _mesh_pos()
        me = 2 * x + y
        sibling = (x, y, 1 - c)
        chips = _other_chips(x, y)
        first, passed = [], []
        for t in range(n):
            half = shards[t].shape[1] // 2
            mine = out_refs[t].at[me, pl.ds(c * half, half)]
            for k, (chip, _) in enumerate(chips):
                cp = pltpu.make_async_remote_copy(
                    src_ref=mine, dst_ref=mine, send_sem=send_sems.at[t, k], recv_sem=recv_sems.at[t, k],
                    device_id=(*chip, c), device_id_type=MESH)
                cp.start()
                first.append(cp)
        for t in range(n):
            half = shards[t].shape[1] // 2
            rows = pl.ds(c * half, half)
            for k, (chip, s) in enumerate(chips):
                landed = out_refs[t].at[s, rows]
                pltpu.make_async_remote_copy(
                    src_ref=landed, dst_ref=landed, send_sem=send_sems.at[t, k], recv_sem=recv_sems.at[t, k],
                    device_id=(*chip, c), device_id_type=MESH).wait_recv()
                cp = pltpu.make_async_remote_copy(
                    src_ref=landed, dst_ref=landed, send_sem=send_sems.at[t, 3 + k], recv_sem=recv_sems.at[t, 3 + k],
                    device_id=sibling, device_id_type=MESH)
                cp.start()
                passed.append(cp)
        for t in range(n):
            half = shards[t].shape[1] // 2
            other = pl.ds((1 - c) * half, half)
            for k, (chip, s) in enumerate(chips):
                got = out_refs[t].at[s, other]
                pltpu.make_async_remote_copy(
                    src_ref=got, dst_ref=got, send_sem=send_sems.at[t, 3 + k], recv_sem=recv_sems.at[t, 3 + k],
                    device_id=sibling, device_id_type=MESH).wait_recv()
        for cp in first + passed:
            cp.wait_send()

    return pl.pallas_call(
        body, name="all_gather_weights",
        in_specs=[ANY] * n, out_specs=[ANY] * n,
        out_shape=[jax.ShapeDtypeStruct(s.shape, s.dtype) for s in shards],
        scratch_shapes=[pltpu.SemaphoreType.DMA((n, 6)), pltpu.SemaphoreType.DMA((n, 6))],
        input_output_aliases={t: t for t in range(n)},
    )(*shards)


def _pair_exchange(grads):
    n = len(grads)

    def body(*refs):
        g_refs, got_refs = refs[:n], refs[n:2 * n]
        send_sems, recv_sems = refs[2 * n:]
        x, y, c = _mesh_pos()
        copies = []
        for t in range(n):
            half = grads[t].shape[1] // 2
            cp = pltpu.make_async_remote_copy(
                src_ref=g_refs[t].at[:, pl.ds((1 - c) * half, half), :], dst_ref=got_refs[t],
                send_sem=send_sems.at[t], recv_sem=recv_sems.at[t],
                device_id=(x, y, 1 - c), device_id_type=MESH)
            cp.start()
            copies.append(cp)
        for cp in copies:
            cp.wait()

    return pl.pallas_call(
        body, name="pair_exchange",
        in_specs=[ANY] * n, out_specs=[ANY] * n,
        out_shape=[jax.ShapeDtypeStruct((N_CHIPS, g.shape[1] // 2, g.shape[2]), F32) for g in grads],
        scratch_shapes=[pltpu.SemaphoreType.DMA((n,)), pltpu.SemaphoreType.DMA((n,))],
    )(*grads)


def _chip_exchange(pairs):
    n = len(pairs)

    def body(*refs):
        p_refs, got_refs = refs[:n], refs[n:2 * n]
        send_sems, recv_sems = refs[2 * n:]
        x, y, c = _mesh_pos()
        copies = []
        for t in range(n):
            for k, (chip, s) in enumerate(_other_chips(x, y)):
                cp = pltpu.make_async_remote_copy(
                    src_ref=p_refs[t].at[s], dst_ref=got_refs[t].at[k],
                    send_sem=send_sems.at[t, k], recv_sem=recv_sems.at[t, k],
                    device_id=(*chip, c), device_id_type=MESH)
                cp.start()
                copies.append(cp)
        for cp in copies:
            cp.wait()

    return pl.pallas_call(
        body, name="chip_exchange",
        in_specs=[ANY] * n, out_specs=[ANY] * n,
        out_shape=[jax.ShapeDtypeStruct((3,) + p.shape[1:], F32) for p in pairs],
        scratch_shapes=[pltpu.SemaphoreType.DMA((n, 3)), pltpu.SemaphoreType.DMA((n, 3))],
    )(*pairs)


def _pair_gather(halves):
    n = len(halves)

    def body(*refs):
        out_refs = refs[n:2 * n]
        send_sems, recv_sems = refs[2 * n:]
        x, y, c = _mesh_pos()
        copies = []
        for t in range(n):
            half = halves[t].shape[0] // 2
            mine = out_refs[t].at[pl.ds(c * half, half)]
            theirs = out_refs[t].at[pl.ds((1 - c) * half, half)]
            send = pltpu.make_async_remote_copy(
                src_ref=mine, dst_ref=mine, send_sem=send_sems.at[t], recv_sem=recv_sems.at[t],
                device_id=(x, y, 1 - c), device_id_type=MESH)
            send.start()
            recv = pltpu.make_async_remote_copy(
                src_ref=theirs, dst_ref=theirs, send_sem=send_sems.at[t], recv_sem=recv_sems.at[t],
                device_id=(x, y, 1 - c), device_id_type=MESH)
            copies.append((send, recv))
        for send, recv in copies:
            send.wait_send()
            recv.wait_recv()

    return pl.pallas_call(
        body, name="pair_gather",
        in_specs=[ANY] * n, out_specs=[ANY] * n,
        out_shape=[jax.ShapeDtypeStruct(h.shape, F32) for h in halves],
        scratch_shapes=[pltpu.SemaphoreType.DMA((n,)), pltpu.SemaphoreType.DMA((n,))],
        input_output_aliases={t: t for t in range(n)},
    )(*halves)


def _all_reduce_small(pack):
    rows = pack.shape[0]

    def body(p_ref, o_ref, buf, send_sems, recv_sems):
        x, y, c = _mesh_pos()
        me = 4 * x + 2 * y + c
        buf[0] = p_ref[...]
        copies = []
        for k in range(1, 8):
            peer = (x ^ (k >> 2), y ^ ((k >> 1) & 1), c ^ (k & 1))
            cp = pltpu.make_async_remote_copy(
                src_ref=p_ref, dst_ref=buf.at[k], send_sem=send_sems.at[k - 1], recv_sem=recv_sems.at[k - 1],
                device_id=peer, device_id_type=MESH)
            cp.start()
            copies.append(cp)
        for cp in copies:
            cp.wait()
        total = buf[me]
        for dev in range(1, 8):
            total = total + buf[me ^ dev]
        o_ref[...] = total

    return pl.pallas_call(
        body, name="all_reduce_small",
        in_specs=[VMEM_FULL], out_specs=VMEM_FULL,
        out_shape=jax.ShapeDtypeStruct(pack.shape, F32),
        scratch_shapes=[pltpu.VMEM((8, rows, LANES), F32),
                        pltpu.SemaphoreType.DMA((7,)), pltpu.SemaphoreType.DMA((7,))],
    )(pack)


def _row_block(rows):
    if rows <= 512:
        return rows
    for rb in (256, 352):
        if rows % rb == 0:
            return rb
    raise ValueError(f"no row block for {rows} rows")


def _place(name, w, pos, dtype):
    R, C = w.shape
    rb = _row_block(R)

    def body(pos_ref, w_ref, o_ref):
        del pos_ref
        o_ref[0] = w_ref[...].astype(dtype)

    return pl.pallas_call(
        body, name=name,
        grid_spec=pltpu.PrefetchScalarGridSpec(
            num_scalar_prefetch=1, grid=(R // rb,),
            in_specs=[pl.BlockSpec((rb, C), lambda r, p: (r, 0))],
            out_specs=pl.BlockSpec((1, rb, C), lambda r, p: (p[0], r, 0))),
        out_shape=jax.ShapeDtypeStruct((N_CHIPS, R, C), dtype),
        compiler_params=_params(32, 1),
    )(pos, w)


def _pair_sum(name, g, got, pos):
    S, R, C = g.shape
    half = R // 2
    rb = _row_block(half)
    nh = half // rb

    def body(pos_ref, a_ref, b_ref, o_ref):
        del pos_ref
        o_ref[...] = a_ref[...] + b_ref[...]

    spec = pl.BlockSpec((1, rb, C), lambda s, r, p: (s, r, 0))
    return pl.pallas_call(
        body, name=name,
        grid_spec=pltpu.PrefetchScalarGridSpec(
            num_scalar_prefetch=1, grid=(S, nh),
            in_specs=[pl.BlockSpec((1, rb, C), lambda s, r, p: (s, p[1] * nh + r, 0)), spec],
            out_specs=spec),
        out_shape=jax.ShapeDtypeStruct((S, half, C), F32), compiler_params=_params(32, 2),
    )(pos, g, got)


def _chip_sum(name, pairs, got, pos):
    _, half, C = pairs.shape
    rb = _row_block(half)
    nh = half // rb

    def body(pos_ref, a_ref, g_ref, o_ref):
        del pos_ref
        o_ref[...] = ((a_ref[0] + g_ref[0]) + g_ref[1]) + g_ref[2]

    return pl.pallas_call(
        body, name=name,
        grid_spec=pltpu.PrefetchScalarGridSpec(
            num_scalar_prefetch=1, grid=(nh,),
            in_specs=[pl.BlockSpec((1, rb, C), lambda r, p: (p[0], r, 0)),
                      pl.BlockSpec((3, rb, C), lambda r, p: (0, r, 0))],
            out_specs=pl.BlockSpec((rb, C), lambda r, p: (p[1] * nh + r, 0))),
        out_shape=jax.ShapeDtypeStruct((2 * half, C), F32), compiler_params=_params(32, 1),
    )(pos, pairs, got)


def _adamw(name, w, g, m, v):
    R, C = w.shape
    rb = _row_block(R)
    c1 = 1.0 - ADAM_B1 ** ADAM_STEP
    c2 = 1.0 - ADAM_B2 ** ADAM_STEP

    def body(w_ref, g_ref, m_ref, v_ref, d_ref, nm_ref, nv_ref):
        gv = g_ref[...]
        nm = ADAM_B1 * m_ref[...] + (1.0 - ADAM_B1) * gv
        nv = ADAM_B2 * v_ref[...] + (1.0 - ADAM_B2) * (gv * gv)
        nm_ref[...] = nm
        nv_ref[...] = nv
        d_ref[...] = -ADAM_LR * ((nm / c1) / (jnp.sqrt(nv / c2) + ADAM_EPS) + ADAM_WD * w_ref[...])

    spec = pl.BlockSpec((rb, C), lambda r: (r, 0))
    sds = jax.ShapeDtypeStruct(w.shape, F32)
    return pl.pallas_call(
        body, name=name, grid=(R // rb,), in_specs=[spec] * 4, out_specs=[spec] * 3,
        out_shape=[sds, sds, sds], compiler_params=_params(40, 1),
    )(w, g, m, v)


def _rel_index():
    m = np.arange(2 * ATT_BLK)
    off = np.where(m < ATT_BLK, m, m - 2 * ATT_BLK)
    rel = np.stack([ATT_BLK * d - off for d in range(N_ATT_TILES)])
    return np.clip(rel, -MAX_REL, MAX_REL) + MAX_REL


def _local_step(x, tgt, g1, w_in, cw, cb, lg, lb, rel, w_out, g2, g3, w_up, fw, fb, w_down, g4):
    T = x.shape[0]
    idx = _rel_index()
    vec = jnp.transpose(rel[:, idx], (1, 0, 2)).reshape(N_ATT_TILES * N_HEADS, 1, 2 * ATT_BLK)
    bias = _bias_tiles(vec)

    u, a, qkv = _fwd_in_proj(x, g1, w_in)
    co, hc = _fwd_conv(a, cw, cb, lg, lb)
    ao, lse = _fwd_attn(qkv, bias)
    mixed, h1, u2 = _fwd_out_proj(co, ao, w_out, x, g2, g3)
    hf, fp = _fwd_ffn(u2, w_up, fw, fb, w_down)
    loss, dy, df, dg4 = _fwd_loss(fp, h1, tgt, g4)

    du2p, dhf, act, dfw_g, dfw_v = _bwd_ffn(df, hf, w_up, fw, fb, w_down)
    dh1, dmx, dco, dao, dg3, dg2 = _bwd_mid(du2p, dy, h1, mixed, g3, g2, w_out)
    dproj, dsacc = _bwd_attn(qkv, ao, dao, lse, bias)
    dproj, dcw, dcb, dlg, dlb = _bwd_conv(dproj, a, dco, hc, cw, lg, lb)
    gx, dg1 = _bwd_in_proj(dproj, w_in, x, dh1, g1)

    tk = 512
    gw_in = _wgrad(
        "wgrad_in", [u], pl.BlockSpec((tk, D_MODEL), lambda s, k: (k, 0)),
        dproj, pl.BlockSpec((tk, IN_SHARD), lambda s, k: (k, s)), None,
        pl.BlockSpec((1, D_MODEL, IN_SHARD), lambda s, k: (s, 0, 0)),
        jax.ShapeDtypeStruct((N_CHIPS, D_MODEL, IN_SHARD), F32), N_CHIPS, T, tk)
    gw_up = _wgrad(
        "wgrad_up", [u2], pl.BlockSpec((tk, D_MODEL), lambda s, k: (k, 0)),
        dhf, pl.BlockSpec((1, tk, FF_SHARD), lambda s, k: (s // 2, k, s % 2)), None,
        pl.BlockSpec((1, D_MODEL, FF_SHARD), lambda s, k: (s, 0, 0)),
        jax.ShapeDtypeStruct((N_CHIPS, D_MODEL, FF_SHARD), F32), N_CHIPS, T, tk)
    gw_down = _wgrad(
        "wgrad_down", [act], pl.BlockSpec((tk, FF_SHARD), lambda s, k: (k, s)),
        df, pl.BlockSpec((tk, D_MODEL), lambda s, k: (k, 0)), None,
        pl.BlockSpec((FF_SHARD, D_MODEL), lambda s, k: (s, 0)),
        jax.ShapeDtypeStruct((D_FF, D_MODEL), F32), 2, T, tk)
    gw_out = _wgrad(
        "wgrad_out", [co, ao], pl.BlockSpec((tk, CONV_W), lambda s, k: (k, 0)),
        dmx, pl.BlockSpec((tk, D_MODEL), lambda s, k: (k, 0)), None,
        pl.BlockSpec((CONV_W, D_MODEL), lambda s, k: (s, 0)),
        jax.ShapeDtypeStruct((D_MODEL, D_MODEL), F32), 2, T, tk, select=lambda s: s)

    diag = _diag_sums(dsacc).reshape(N_ATT_TILES, N_HEADS, 2 * ATT_BLK)
    onehot = np.zeros((N_ATT_TILES, 2 * ATT_BLK, 2 * MAX_REL + 1), np.float32)
    for d in range(N_ATT_TILES):
        onehot[d, np.arange(2 * ATT_BLK), idx[d]] = 1.0
    drel = jnp.einsum("dhm,dmr->hr", diag, jnp.asarray(onehot), precision=lax.Precision.HIGHEST)

    small = dict(norm_mix_pre=dg1, conv_dw_w=dcw[:CONV_K], conv_dw_b=dcb, conv_ln_g=dlg, conv_ln_b=dlb,
                 rel_bias=drel, norm_mix_post=dg2, norm_ffn_pre=dg3,
                 ffn_dw_w=jnp.concatenate([dfw_g[0, :3], dfw_g[1, :3], dfw_v[0, :3], dfw_v[1, :3]], axis=1),
                 ffn_dw_b=jnp.concatenate([dfw_g[0, 3:4], dfw_g[1, 3:4], dfw_v[0, 3:4], dfw_v[1, 3:4]], axis=1),
                 norm_ffn_post=dg4)
    big = dict(w_in=gw_in, w_out=gw_out.reshape(N_CHIPS, D_MODEL // N_CHIPS, D_MODEL), w_up=gw_up,
               w_down=gw_down.reshape(N_CHIPS, D_FF // N_CHIPS, D_MODEL))
    return loss, gx, small, big


SMALL_ORDER = ["norm_mix_pre", "conv_dw_b", "conv_ln_g", "conv_ln_b", "rel_bias", "norm_mix_post",
               "norm_ffn_pre", "ffn_dw_b", "norm_ffn_post", "conv_dw_w", "ffn_dw_w"]


def _pack(parts):
    rows = []
    for p in parts:
        width = -(-p.shape[1] // LANES) * LANES
        rows.append(jnp.pad(p, ((0, 0), (0, width - p.shape[1]))).reshape(-1, LANES))
    packed = jnp.concatenate(rows, axis=0)
    pad = -packed.shape[0] % 8
    return jnp.pad(packed, ((0, pad), (0, 0)))


def _unpack(packed, shapes):
    out, r = [], 0
    for shp in shapes:
        width = -(-shp[1] // LANES) * LANES
        n = shp[0] * width // LANES
        out.append(packed[r:r + n].reshape(shp[0], width)[:, :shp[1]])
        r += n
    return out


WEIGHTS = ["norm_mix_pre", "w_in", "conv_dw_w", "conv_dw_b", "conv_ln_g", "conv_ln_b", "rel_bias", "w_out",
           "norm_mix_post", "norm_ffn_pre", "w_up", "ffn_dw_w", "ffn_dw_b", "w_down", "norm_ffn_post"]
BIG = ["w_in", "w_out", "w_up", "w_down"]


def kernel(x, norm_mix_pre, w_in, conv_dw_w, conv_dw_b, conv_ln_g, conv_ln_b, rel_bias, w_out, norm_mix_post, norm_ffn_pre, w_up, ffn_dw_w, ffn_dw_b, w_down, norm_ffn_post, loss_target, m_norm_mix_pre, m_w_in, m_conv_dw_w, m_conv_dw_b, m_conv_ln_g, m_conv_ln_b, m_rel_bias, m_w_out, m_norm_mix_post, m_norm_ffn_pre, m_w_up, m_ffn_dw_w, m_ffn_dw_b, m_w_down, m_norm_ffn_post, v_norm_mix_pre, v_w_in, v_conv_dw_w, v_conv_dw_b, v_conv_ln_g, v_conv_ln_b, v_rel_bias, v_w_out, v_norm_mix_post, v_norm_ffn_pre, v_w_up, v_ffn_dw_w, v_ffn_dw_b, v_w_down, v_norm_ffn_post):
    args = locals()
    w = {n: args[n][0] for n in WEIGHTS}
    m = {n: args["m_" + n][0] for n in WEIGHTS}
    v = {n: args["v_" + n][0] for n in WEIGHTS}
    for d in (w, m, v):
        d["rel_bias"] = d["rel_bias"].reshape(N_HEADS, 2 * MAX_REL + 1)
        for n in ("norm_mix_pre", "conv_dw_b", "conv_ln_g", "conv_ln_b", "norm_mix_post", "norm_ffn_pre",
                  "ffn_dw_b", "norm_ffn_post"):
            d[n] = d[n].reshape(1, -1)
    shard = 2 * lax.axis_index("x") + lax.axis_index("y")

    cw_sh = jnp.pad(w["conv_dw_w"], ((0, CONV_HALO - CONV_K), (0, 0)))
    fw_sh = jnp.pad(w["ffn_dw_w"], ((0, FF_HALO - 3), (0, 0)))
    pos = jnp.stack([shard, lax.axis_index("c")]).astype(jnp.int32)
    shards = [_place("place_" + n, w[n], pos, BF16) for n in BIG]
    shards += [_place("place_conv_dw_w", cw_sh, pos, F32), _place("place_ffn_dw_w", fw_sh, pos, F32)]
    w_in_f, w_out_f, w_up_f, w_down_f, cw_f, fw_f = _all_gather_weights(shards)
    cw_full = jnp.transpose(cw_f, (1, 0, 2)).reshape(CONV_HALO, CONV_W)

    loss, gx, small, big = _local_step(
        x[0], loss_target[0], w["norm_mix_pre"], w_in_f, cw_full, w["conv_dw_b"], w["conv_ln_g"],
        w["conv_ln_b"], w["rel_bias"], w_out_f.reshape(D_MODEL, D_MODEL), w["norm_mix_post"],
        w["norm_ffn_pre"], w_up_f, fw_f, w["ffn_dw_b"].reshape(N_CHIPS, 1, FF_SHARD),
        w_down_f.reshape(D_FF, D_MODEL), w["norm_ffn_post"])

    theirs = _pair_exchange([big[n] for n in BIG])
    pairs = [_pair_sum("pair_sum_" + n, big[n], b, pos) for n, b in zip(BIG, theirs)]
    got = _chip_exchange(pairs)
    halves = [_chip_sum("chip_sum_" + n, a, b, pos) for n, a, b in zip(BIG, pairs, got)]
    full = _pair_gather(halves)
    grads, deltas, new_m, new_v = {}, {}, {}, {}
    for n, g in zip(BIG, full):
        grads[n] = g
        deltas[n], new_m[n], new_v[n] = _adamw("adamw_" + n, w[n], g, m[n], v[n])

    gsum = _all_reduce_small(_pack([small[n] for n in SMALL_ORDER]))
    shapes = [small[n].shape for n in SMALL_ORDER]
    gs = dict(zip(SMALL_ORDER, _unpack(gsum, shapes)))
    gs["conv_dw_w"] = lax.dynamic_slice_in_dim(gs["conv_dw_w"], shard * LANES, LANES, axis=1)
    gs["ffn_dw_w"] = lax.dynamic_slice_in_dim(gs["ffn_dw_w"], shard * FF_SHARD, FF_SHARD, axis=1)
    shapes = [gs[n].shape for n in SMALL_ORDER]
    d_p, m_p, v_p = _adamw("adamw_small", _pack([w[n] for n in SMALL_ORDER]), _pack([gs[n] for n in SMALL_ORDER]),
                           _pack([m[n] for n in SMALL_ORDER]), _pack([v[n] for n in SMALL_ORDER]))
    for dst, packed in ((deltas, d_p), (new_m, m_p), (new_v, v_p)):
        dst.update(zip(SMALL_ORDER, _unpack(packed, shapes)))
    grads.update(gs)

    total = lax.psum(loss[0, 0], ("x", "y", "c"))
    outs = [total, gx[None]]
    for group in (grads, deltas, new_m, new_v):
        outs += [group[n].reshape(args[n].shape) for n in WEIGHTS]
    return tuple(outs)
```

```python
import functools
import math

import numpy as np
import jax
import jax.numpy as jnp
from jax import lax
from jax.experimental import pallas as pl
from jax.experimental.pallas import tpu as pltpu

F32 = jnp.float32
BF16 = jnp.bfloat16

D_MODEL = 1024
CONV_W = 512
ATTN_W = 512
N_HEADS = 8
HEAD_DIM = 64
CHUNK = 64
N_LEFT = 8
MAX_REL = 128
CONV_K = 31
CONV_HALO = 32
D_FF = 2816
FF_SHARD = 1408
IN_COLS = 2560
IN_SHARD = 640
EPS = 1e-6
NEG_INF = -1e30
ATT_BLK = 256
N_ATT_TILES = 3
LANES = 128
SUBLANES = 8
N_CHIPS = 4

ADAM_LR = 0.001
ADAM_B1 = 0.9
ADAM_B2 = 0.999
ADAM_EPS = 1e-08
ADAM_WD = 0.01
ADAM_STEP = 10

MESH = pl.DeviceIdType.MESH
ANY = pl.BlockSpec(memory_space=pl.ANY)
VMEM_FULL = pl.BlockSpec(memory_space=pltpu.VMEM)


def _params(vmem_mb, n_grid=0):
    sem = ("arbitrary",) * n_grid if n_grid else None
    return pltpu.CompilerParams(dimension_semantics=sem, vmem_limit_bytes=vmem_mb << 20)


def _sigmoid(v):
    return 1.0 / (1.0 + jnp.exp(-v))


def _dot(a, b):
    return jnp.dot(a, b, preferred_element_type=F32)


def _dot_nt(a, b):
    return lax.dot_general(a, b, (((1,), (1,)), ((), ())), preferred_element_type=F32)


def _dot_tn(a, b):
    return lax.dot_general(a, b, (((0,), (0,)), ((), ())), preferred_element_type=F32)


def _rms_fwd(v, g):
    r = lax.rsqrt(jnp.mean(v * v, axis=-1, keepdims=True) + EPS)
    return v * r * g, r


def _rms_bwd(dy, v, g):
    r = lax.rsqrt(jnp.mean(v * v, axis=-1, keepdims=True) + EPS)
    vh = v * r
    dvh = dy * g
    dv = r * (dvh - vh * jnp.mean(dvh * vh, axis=-1, keepdims=True))
    return dv, dy * vh


def _fwd_in_proj(x, g1, w_in):
    T = x.shape[0]
    tm = 512

    def body(x_ref, g_ref, w_ref, u_ref, a_ref, qkv_ref):
        u, _ = _rms_fwd(x_ref[...], g_ref[...])
        u = u.astype(BF16)
        u_ref[...] = u
        for s in range(N_CHIPS):
            y = _dot(u, w_ref[s])
            lo, hi = IN_SHARD * s, IN_SHARD * (s + 1)
            if hi <= 1024:
                a_ref[:, lo:hi] = y
            elif lo >= 1024:
                qkv_ref[:, lo - 1024:hi - 1024] = y.astype(BF16)
            else:
                a_ref[:, lo:1024] = y[:, :1024 - lo]
                qkv_ref[:, 0:hi - 1024] = y[:, 1024 - lo:].astype(BF16)

    return pl.pallas_call(
        body, name="fwd_in_proj", grid=(T // tm,),
        in_specs=[pl.BlockSpec((tm, D_MODEL), lambda i: (i, 0)),
                  pl.BlockSpec((1, D_MODEL), lambda i: (0, 0)),
                  pl.BlockSpec((N_CHIPS, D_MODEL, IN_SHARD), lambda i: (0, 0, 0))],
        out_specs=[pl.BlockSpec((tm, D_MODEL), lambda i: (i, 0)),
                   pl.BlockSpec((tm, 1024), lambda i: (i, 0)),
                   pl.BlockSpec((tm, 1536), lambda i: (i, 0))],
        out_shape=[jax.ShapeDtypeStruct((T, D_MODEL), BF16),
                   jax.ShapeDtypeStruct((T, 1024), F32),
                   jax.ShapeDtypeStruct((T, 1536), BF16)],
        compiler_params=_params(40, 1),
    )(x, g1, w_in)


def _fill_shifted(ext, shifted, tm):
    n = tm + CONV_HALO - SUBLANES
    for j in range(1, SUBLANES):
        shifted[j - 1] = ext[j:j + n, :]


def _shifted_rows(ext, shifted, start, rows):
    j = start % SUBLANES
    if j == 0:
        return ext[start:start + rows, :]
    return shifted[j - 1, start - j:start - j + rows, :]


def _fwd_conv(a, cw, cb, lg, lb):
    T = a.shape[0]
    tm = 512
    rc = 64

    def body(a_ref, w_ref, b_ref, lg_ref, lb_ref, co_ref, hc_ref, hext, hsh):
        i = pl.program_id(0)

        @pl.when(i == 0)
        def _():
            hext[0:CONV_HALO, :] = jnp.zeros((CONV_HALO, CONV_W), F32)

        @pl.when(i > 0)
        def _():
            hext[0:CONV_HALO, :] = hext[tm:tm + CONV_HALO, :]

        hext[CONV_HALO:CONV_HALO + tm, :] = a_ref[:, :CONV_W] * _sigmoid(a_ref[:, CONV_W:])
        _fill_shifted(hext, hsh, tm)
        for c in range(tm // rc):
            acc = jnp.zeros((rc, CONV_W), F32)
            for k in range(CONV_K):
                acc = acc + w_ref[k:k + 1, :] * _shifted_rows(hext, hsh, c * rc + 2 + k, rc)
            hc = acc + b_ref[...]
            hc_ref[c * rc:(c + 1) * rc, :] = hc
            mu = jnp.mean(hc, axis=-1, keepdims=True)
            xc = hc - mu
            var = jnp.mean(xc * xc, axis=-1, keepdims=True)
            z = xc * lax.rsqrt(var + EPS) * lg_ref[...] + lb_ref[...]
            co_ref[c * rc:(c + 1) * rc, :] = (z * _sigmoid(z)).astype(BF16)

    return pl.pallas_call(
        body, name="fwd_conv", grid=(T // tm,),
        in_specs=[pl.BlockSpec((tm, 1024), lambda i: (i, 0)),
                  pl.BlockSpec((CONV_HALO, CONV_W), lambda i: (0, 0)),
                  pl.BlockSpec((1, CONV_W), lambda i: (0, 0)),
                  pl.BlockSpec((1, CONV_W), lambda i: (0, 0)),
                  pl.BlockSpec((1, CONV_W), lambda i: (0, 0))],
        out_specs=[pl.BlockSpec((tm, CONV_W), lambda i: (i, 0)),
                   pl.BlockSpec((tm, CONV_W), lambda i: (i, 0))],
        out_shape=[jax.ShapeDtypeStruct((T, CONV_W), BF16),
                   jax.ShapeDtypeStruct((T, CONV_W), F32)],
        scratch_shapes=[pltpu.VMEM((tm + CONV_HALO, CONV_W), F32),
                        pltpu.VMEM((SUBLANES - 1, tm + CONV_HALO - SUBLANES, CONV_W), F32)],
        compiler_params=_params(40, 1),
    )(a, cw, cb, lg, lb)


def _row_skew(v, sign):
    rows, width = v.shape
    row = lax.broadcasted_iota(jnp.int32, (rows, 1), 0)
    for b in range(int(math.log2(rows))):
        shift = (1 << b) if sign > 0 else width - (1 << b)
        v = jnp.where(((row >> b) & 1) == 1, pltpu.roll(v, shift, 1), v)
    return v


def _att_visible(d):
    rq = lax.broadcasted_iota(jnp.int32, (ATT_BLK, ATT_BLK), 0) // CHUNK
    ck = lax.broadcasted_iota(jnp.int32, (ATT_BLK, ATT_BLK), 1) // CHUNK
    slack = ATT_BLK
    above = jnp.where(d == 0, 0, slack)
    below = jnp.where(d == 2, 0, slack)
    return (ck <= rq + above) & (ck >= rq - below)


def _bias_tiles(vec):
    def body(v_ref, o_ref):
        d = pl.program_id(0) // N_HEADS
        full = _row_skew(jnp.broadcast_to(v_ref[0], (ATT_BLK, 2 * ATT_BLK)), 1)
        o_ref[0] = jnp.where(_att_visible(d), full[:, :ATT_BLK], NEG_INF)

    return pl.pallas_call(
        body, name="bias_tiles", grid=(N_ATT_TILES * N_HEADS,),
        in_specs=[pl.BlockSpec((1, 1, 2 * ATT_BLK), lambda n: (n, 0, 0))],
        out_specs=pl.BlockSpec((1, ATT_BLK, ATT_BLK), lambda n: (n, 0, 0)),
        out_shape=jax.ShapeDtypeStruct((N_ATT_TILES * N_HEADS, ATT_BLK, ATT_BLK), F32),
        compiler_params=_params(16, 1),
    )(vec)


def _diag_sums(ds):
    def body(d_ref, o_ref):
        wide = jnp.concatenate([d_ref[0], jnp.zeros((ATT_BLK, ATT_BLK), F32)], axis=1)
        o_ref[0] = jnp.sum(_row_skew(wide, -1), axis=0, keepdims=True)

    return pl.pallas_call(
        body, name="diag_sums", grid=(N_ATT_TILES * N_HEADS,),
        in_specs=[pl.BlockSpec((1, ATT_BLK, ATT_BLK), lambda n: (n, 0, 0))],
        out_specs=pl.BlockSpec((1, 1, 2 * ATT_BLK), lambda n: (n, 0, 0)),
        out_shape=jax.ShapeDtypeStruct((N_ATT_TILES * N_HEADS, 1, 2 * ATT_BLK), F32),
        compiler_params=_params(16, 1),
    )(ds)


def _head_mask(h):
    lane = lax.broadcasted_iota(jnp.int32, (1, LANES), 1)
    return (lane // HEAD_DIM) == (h % 2)


def _fwd_attn(qkv, bias):
    T = qkv.shape[0]
    nb = T // ATT_BLK
    scale = HEAD_DIM ** -0.5

    def body(q_ref, k0_ref, k1_ref, k2_ref, v0_ref, v1_ref, v2_ref, b_ref, o_ref, lse_ref):
        i = pl.program_id(0)
        k_refs = (k0_ref, k1_ref, k2_ref)
        v_refs = (v0_ref, v1_ref, v2_ref)
        lane = lax.broadcasted_iota(jnp.int32, (1, LANES), 1)
        lse_tile = jnp.zeros((ATT_BLK, LANES), F32)
        for g in range(N_HEADS // 2):
            cols = slice(g * LANES, (g + 1) * LANES)
            qg = q_ref[:, cols]
            og = jnp.zeros((ATT_BLK, LANES), F32)
            for h in (2 * g, 2 * g + 1):
                hm = _head_mask(h)
                qh = jnp.where(hm, qg, jnp.zeros_like(qg))
                s = []
                for d in range(N_ATT_TILES):
                    sd = _dot_nt(qh, k_refs[d][:, cols]) * scale + b_ref[d * N_HEADS + h]
                    if d > 0:
                        sd = jnp.where(i >= d, sd, NEG_INF)
                    s.append(sd)
                m = jnp.maximum(jnp.maximum(jnp.max(s[0], axis=-1, keepdims=True),
                                            jnp.max(s[1], axis=-1, keepdims=True)),
                                jnp.max(s[2], axis=-1, keepdims=True))
                p = [jnp.exp(sd - m) for sd in s]
                l = (jnp.sum(p[0], axis=-1, keepdims=True) + jnp.sum(p[1], axis=-1, keepdims=True)
                     + jnp.sum(p[2], axis=-1, keepdims=True))
                oh = jnp.zeros((ATT_BLK, LANES), F32)
                for d in range(N_ATT_TILES):
                    vg = v_refs[d][:, cols]
                    oh = oh + _dot(p[d].astype(BF16), jnp.where(hm, vg, jnp.zeros_like(vg)))
                og = og + oh / l
                lse_tile = jnp.where(lane == h, m + jnp.log(l), lse_tile)
            o_ref[:, cols] = og.astype(BF16)
        lse_ref[...] = lse_tile

    def kv_spec(d, col):
        return pl.BlockSpec((ATT_BLK, ATTN_W), lambda i: (jnp.maximum(i - d, 0), col))

    return pl.pallas_call(
        body, name="fwd_attn", grid=(nb,),
        in_specs=[pl.BlockSpec((ATT_BLK, ATTN_W), lambda i: (i, 0)),
                  kv_spec(0, 1), kv_spec(1, 1), kv_spec(2, 1),
                  kv_spec(0, 2), kv_spec(1, 2), kv_spec(2, 2),
                  pl.BlockSpec((N_ATT_TILES * N_HEADS, ATT_BLK, ATT_BLK), lambda i: (0, 0, 0))],
        out_specs=[pl.BlockSpec((ATT_BLK, ATTN_W), lambda i: (i, 0)),
                   pl.BlockSpec((ATT_BLK, LANES), lambda i: (i, 0))],
        out_shape=[jax.ShapeDtypeStruct((T, ATTN_W), BF16),
                   jax.ShapeDtypeStruct((T, LANES), F32)],
        compiler_params=_params(40, 1),
    )(qkv, qkv, qkv, qkv, qkv, qkv, qkv, bias)


def _fwd_out_proj(co, ao, w_out, x, g2, g3):
    T = x.shape[0]
    tm = 512

    def body(co_ref, ao_ref, w_ref, x_ref, g2_ref, g3_ref, mixed_ref, h1_ref, u2_ref):
        mixed = _dot(co_ref[...], w_ref[0:CONV_W, :]) + _dot(ao_ref[...], w_ref[CONV_W:, :])
        mixed_ref[...] = mixed
        y, _ = _rms_fwd(mixed, g2_ref[...])
        h1 = x_ref[...] + y
        h1_ref[...] = h1
        u2, _ = _rms_fwd(h1, g3_ref[...])
        u2_ref[...] = u2.astype(BF16)

    row = lambda w: pl.BlockSpec((tm, w), lambda i: (i, 0))
    vec = pl.BlockSpec((1, D_MODEL), lambda i: (0, 0))
    return pl.pallas_call(
        body, name="fwd_out_proj", grid=(T // tm,),
        in_specs=[row(CONV_W), row(ATTN_W), pl.BlockSpec((D_MODEL, D_MODEL), lambda i: (0, 0)),
                  row(D_MODEL), vec, vec],
        out_specs=[row(D_MODEL), row(D_MODEL), row(D_MODEL)],
        out_shape=[jax.ShapeDtypeStruct((T, D_MODEL), F32),
                   jax.ShapeDtypeStruct((T, D_MODEL), F32),
                   jax.ShapeDtypeStruct((T, D_MODEL), BF16)],
        compiler_params=_params(40, 1),
    )(co, ao, w_out, x, g2, g3)


GELU_C = math.sqrt(2.0 / math.pi)
GELU_A = 0.044715


def _gelu_and_grad(v):
    th = jnp.tanh(GELU_C * (v + GELU_A * v * v * v))
    gl = 0.5 * v * (1.0 + th)
    dgl = 0.5 * (1.0 + th) + 0.5 * v * (1.0 - th * th) * (GELU_C * (1.0 + 3.0 * GELU_A * v * v))
    return gl, dgl


FF_TM = 256
FF_HALO = 16
FF_CHUNKS = [(lo, min(lo + 256, FF_SHARD)) for lo in range(0, FF_SHARD, 256)]


def _rows_before(prev, cur):
    ext = jnp.concatenate([prev, cur], axis=0)
    return pltpu.roll(ext, 1, 0)[SUBLANES:], pltpu.roll(ext, 2, 0)[SUBLANES:]


def _rows_after(cur, nxt):
    ext = jnp.concatenate([cur, nxt], axis=0)
    n = ext.shape[0]
    return pltpu.roll(ext, n - 1, 0)[:cur.shape[0]], pltpu.roll(ext, n - 2, 0)[:cur.shape[0]]


def _fwd_ffn(u2, w_up, fw, fb, w_down):
    T = u2.shape[0]
    tm = FF_TM

    def body(u_ref, wg_ref, wv_ref, fwg_ref, fwv_ref, fbg_ref, fbv_ref, wd_ref, hf_ref, f_ref, carg, carv):
        i = pl.program_id(1)

        @pl.when(i == 0)
        def _():
            carg[...] = jnp.zeros(carg.shape, F32)
            carv[...] = jnp.zeros(carv.shape, F32)

        u = u_ref[...]
        f = None
        up = lambda lo, hi: (_dot(u, wg_ref[0, :, lo:hi]), _dot(u, wv_ref[0, :, lo:hi]))
        ahead = up(*FF_CHUNKS[0])
        for c, (lo, hi) in enumerate(FF_CHUNKS):
            conv = []
            hs = ahead
            if c + 1 < len(FF_CHUNKS):
                ahead = up(*FF_CHUNKS[c + 1])
            for n, (car, fw_ref, fb_ref) in enumerate(((carg, fwg_ref, fbg_ref), (carv, fwv_ref, fbv_ref))):
                h0 = hs[n]
                hf_ref[n, :, lo:hi] = h0.astype(BF16)
                h1, h2 = _rows_before(car[:, lo:hi], h0)
                car[:, lo:hi] = h0[tm - SUBLANES:, :]
                conv.append(fw_ref[0, 0:1, lo:hi] * h2 + fw_ref[0, 1:2, lo:hi] * h1
                            + fw_ref[0, 2:3, lo:hi] * h0 + fb_ref[0, :, lo:hi])
            gl, _ = _gelu_and_grad(conv[0])
            term = _dot((gl * conv[1]).astype(BF16), wd_ref[lo:hi, :])
            f = term if f is None else f + term
        f_ref[0] = f

    wspec = lambda off: pl.BlockSpec((1, D_MODEL, FF_SHARD), lambda s, i: (s + off, 0, 0))
    fwspec = lambda off: pl.BlockSpec((1, FF_HALO, FF_SHARD), lambda s, i: (s + off, 0, 0))
    fbspec = lambda off: pl.BlockSpec((1, 1, FF_SHARD), lambda s, i: (s + off, 0, 0))
    return pl.pallas_call(
        body, name="fwd_ffn", grid=(2, T // tm),
        in_specs=[pl.BlockSpec((tm, D_MODEL), lambda s, i: (i, 0)),
                  wspec(0), wspec(2), fwspec(0), fwspec(2), fbspec(0), fbspec(2),
                  pl.BlockSpec((FF_SHARD, D_MODEL), lambda s, i: (s, 0))],
        out_specs=[pl.BlockSpec((2, tm, FF_SHARD), lambda s, i: (0, i, s)),
                   pl.BlockSpec((1, tm, D_MODEL), lambda s, i: (s, i, 0))],
        out_shape=[jax.ShapeDtypeStruct((2, T, D_FF), BF16),
                   jax.ShapeDtypeStruct((2, T, D_MODEL), F32)],
        scratch_shapes=[pltpu.VMEM((SUBLANES, FF_SHARD), F32), pltpu.VMEM((SUBLANES, FF_SHARD), F32)],
        compiler_params=_params(48, 2),
    )(u2, w_up, w_up, fw, fw, fb, fb, w_down)


def _fwd_loss(fp, h1, tgt, g4):
    T = h1.shape[0]
    tm = 512

    def body(fp_ref, h1_ref, t_ref, g_ref, loss_ref, dy_ref, df_ref, dg_ref):
        i = pl.program_id(0)
        f = fp_ref[0] + fp_ref[1]
        r, _ = _rms_fwd(f, g_ref[...])
        e = (h1_ref[...] + r) - t_ref[...]
        dy = e * (1.0 / D_MODEL)
        dy_ref[...] = dy
        df, dg_rows = _rms_bwd(dy, f, g_ref[...])
        df_ref[...] = df.astype(BF16)
        part = 0.5 * jnp.sum(jnp.mean(e * e, axis=-1, keepdims=True), axis=0, keepdims=True)
        dg = jnp.sum(dg_rows, axis=0, keepdims=True)

        @pl.when(i == 0)
        def _():
            loss_ref[...] = part
            dg_ref[...] = dg

        @pl.when(i > 0)
        def _():
            loss_ref[...] += part
            dg_ref[...] += dg

    row = pl.BlockSpec((tm, D_MODEL), lambda i: (i, 0))
    vec = pl.BlockSpec((1, D_MODEL), lambda i: (0, 0))
    return pl.pallas_call(
        body, name="fwd_loss", grid=(T // tm,),
        in_specs=[pl.BlockSpec((2, tm, D_MODEL), lambda i: (0, i, 0)), row, row, vec],
        out_specs=[pl.BlockSpec((1, 1), lambda i: (0, 0)), row, row, vec],
        out_shape=[jax.ShapeDtypeStruct((1, 1), F32),
                   jax.ShapeDtypeStruct((T, D_MODEL), F32),
                   jax.ShapeDtypeStruct((T, D_MODEL), BF16),
                   jax.ShapeDtypeStruct((1, D_MODEL), F32)],
        compiler_params=_params(40, 1),
    )(fp, h1, tgt, g4)


def _bwd_ffn(df, hf, w_up, fw, fb, w_down):
    T = df.shape[0]
    tm = FF_TM
    ni = T // tm

    def body(df_ref, hf_ref, halo_ref, wd_ref, wg_ref, wv_ref, fwg_ref, fwv_ref, fbg_ref, fbv_ref,
             du_ref, dhf_ref, act_ref, dwg_ref, dwv_ref, carg, carv):
        i = pl.program_id(1)
        ri = ni - 1 - i

        @pl.when(i == 0)
        def _():
            dwg_ref[...] = jnp.zeros(dwg_ref.shape, F32)
            dwv_ref[...] = jnp.zeros(dwv_ref.shape, F32)
            carg[...] = jnp.zeros(carg.shape, F32)
            carv[...] = jnp.zeros(carv.shape, F32)

        df = df_ref[...]
        du = None
        down = lambda lo, hi: _dot_nt(df, wd_ref[lo:hi, :])
        ahead = down(*FF_CHUNKS[0])
        for c, (lo, hi) in enumerate(FF_CHUNKS):
            dact = ahead
            if c + 1 < len(FF_CHUNKS):
                ahead = down(*FF_CHUNKS[c + 1])
            hs, pre = [], []
            for n, (fw_ref, fb_ref) in enumerate(((fwg_ref, fbg_ref), (fwv_ref, fbv_ref))):
                h0 = hf_ref[n, :, lo:hi].astype(F32)
                halo = jnp.where(ri > 0, halo_ref[n, :, lo:hi].astype(F32)[FF_HALO - SUBLANES:], 0.0)
                h1, h2 = _rows_before(halo, h0)
                hs.append((h2, h1, h0))
                pre.append(fw_ref[0, 0:1, lo:hi] * h2 + fw_ref[0, 1:2, lo:hi] * h1
                           + fw_ref[0, 2:3, lo:hi] * h0 + fb_ref[0, :, lo:hi])
            gl, dgl = _gelu_and_grad(pre[0])
            act_ref[:, lo:hi] = (gl * pre[1]).astype(BF16)
            dpre = (dact * pre[1] * dgl, dact * gl)
            for n, (car, fw_ref, dw_ref, w_ref) in enumerate(
                    ((carg, fwg_ref, dwg_ref, wg_ref), (carv, fwv_ref, dwv_ref, wv_ref))):
                dp = dpre[n]
                for k in range(3):
                    dw_ref[0, k:k + 1, lo:hi] += jnp.sum(dp * hs[n][k], axis=0, keepdims=True)
                dw_ref[0, 3:4, lo:hi] += jnp.sum(dp, axis=0, keepdims=True)
                up1, up2 = _rows_after(dp, car[:, lo:hi])
                car[:, lo:hi] = dp[0:SUBLANES, :]
                dh = (fw_ref[0, 2:3, lo:hi] * dp + fw_ref[0, 1:2, lo:hi] * up1
                      + fw_ref[0, 0:1, lo:hi] * up2).astype(BF16)
                dhf_ref[n, :, lo:hi] = dh
                term = _dot_nt(dh, w_ref[0, :, lo:hi])
                du = term if du is None else du + term
        du_ref[0] = du

    rev = lambda i: ni - 1 - i
    wspec = lambda off: pl.BlockSpec((1, D_MODEL, FF_SHARD), lambda s, i: (s + off, 0, 0))
    fwspec = lambda off: pl.BlockSpec((1, FF_HALO, FF_SHARD), lambda s, i: (s + off, 0, 0))
    fbspec = lambda off: pl.BlockSpec((1, 1, FF_SHARD), lambda s, i: (s + off, 0, 0))
    halo_blocks = tm // FF_HALO
    dwspec = pl.BlockSpec((1, FF_HALO, FF_SHARD), lambda s, i: (s, 0, 0))
    return pl.pallas_call(
        body, name="bwd_ffn", grid=(2, ni),
        in_specs=[pl.BlockSpec((tm, D_MODEL), lambda s, i: (rev(i), 0)),
                  pl.BlockSpec((2, tm, FF_SHARD), lambda s, i: (0, rev(i), s)),
                  pl.BlockSpec((2, FF_HALO, FF_SHARD),
                               lambda s, i: (0, jnp.maximum(rev(i) * halo_blocks - 1, 0), s)),
                  pl.BlockSpec((FF_SHARD, D_MODEL), lambda s, i: (s, 0)),
                  wspec(0), wspec(2), fwspec(0), fwspec(2), fbspec(0), fbspec(2)],
        out_specs=[pl.BlockSpec((1, tm, D_MODEL), lambda s, i: (s, rev(i), 0)),
                   pl.BlockSpec((2, tm, FF_SHARD), lambda s, i: (0, rev(i), s)),
                   pl.BlockSpec((tm, FF_SHARD), lambda s, i: (rev(i), s)),
                   dwspec, dwspec],
        out_shape=[jax.ShapeDtypeStruct((2, T, D_MODEL), F32),
                   jax.ShapeDtypeStruct((2, T, D_FF), BF16),
                   jax.ShapeDtypeStruct((T, D_FF), BF16),
                   jax.ShapeDtypeStruct((2, FF_HALO, FF_SHARD), F32),
                   jax.ShapeDtypeStruct((2, FF_HALO, FF_SHARD), F32)],
        scratch_shapes=[pltpu.VMEM((SUBLANES, FF_SHARD), F32), pltpu.VMEM((SUBLANES, FF_SHARD), F32)],
        compiler_params=_params(56, 2),
    )(df, hf, hf, w_down, w_up, w_up, fw, fw, fb, fb)


def _bwd_mid(du2p, dy, h1, mixed, g3, g2, w_out):
    T = dy.shape[0]
    tm = 512

    def body(du_ref, dy_ref, h1_ref, mx_ref, g3_ref, g2_ref, w_ref,
             dh1_ref, dmx_ref, dco_ref, dao_ref, dg3_ref, dg2_ref):
        i = pl.program_id(0)
        dres, dg3_rows = _rms_bwd(du_ref[0] + du_ref[1], h1_ref[...], g3_ref[...])
        dh1 = dy_ref[...] + dres
        dh1_ref[...] = dh1
        dmx, dg2_rows = _rms_bwd(dh1, mx_ref[...], g2_ref[...])
        dmx = dmx.astype(BF16)
        dmx_ref[...] = dmx
        dcat = _dot_nt(dmx, w_ref[...])
        dco_ref[...] = dcat[:, :CONV_W]
        dao_ref[...] = dcat[:, CONV_W:].astype(BF16)
        dg3 = jnp.sum(dg3_rows, axis=0, keepdims=True)
        dg2 = jnp.sum(dg2_rows, axis=0, keepdims=True)

        @pl.when(i == 0)
        def _():
            dg3_ref[...] = dg3
            dg2_ref[...] = dg2

        @pl.when(i > 0)
        def _():
            dg3_ref[...] += dg3
            dg2_ref[...] += dg2

    row = lambda w: pl.BlockSpec((tm, w), lambda i: (i, 0))
    vec = pl.BlockSpec((1, D_MODEL), lambda i: (0, 0))
    return pl.pallas_call(
        body, name="bwd_mid", grid=(T // tm,),
        in_specs=[pl.BlockSpec((2, tm, D_MODEL), lambda i: (0, i, 0)), row(D_MODEL), row(D_MODEL),
                  row(D_MODEL), vec, vec, pl.BlockSpec((D_MODEL, D_MODEL), lambda i: (0, 0))],
        out_specs=[row(D_MODEL), row(D_MODEL), row(CONV_W), row(ATTN_W), vec, vec],
        out_shape=[jax.ShapeDtypeStruct((T, D_MODEL), F32),
                   jax.ShapeDtypeStruct((T, D_MODEL), BF16),
                   jax.ShapeDtypeStruct((T, CONV_W), F32),
                   jax.ShapeDtypeStruct((T, ATTN_W), BF16),
                   jax.ShapeDtypeStruct((1, D_MODEL), F32),
                   jax.ShapeDtypeStruct((1, D_MODEL), F32)],
        compiler_params=_params(48, 1),
    )(du2p, dy, h1, mixed, g3, g2, w_out)


def _bwd_attn(qkv, ao, dao, lse, bias):
    T = qkv.shape[0]
    nb = T // ATT_BLK
    scale = HEAD_DIM ** -0.5

    def body(k_ref, v_ref, q0, q1, q2, do0, do1, do2, o0, o1, o2, l0, l1, l2, b_ref,
             dp_ref, ds_ref, acc1, acc2):
        j = pl.program_id(0)
        q_refs, do_refs, o_refs, l_refs = (q0, q1, q2), (do0, do1, do2), (o0, o1, o2), (l0, l1, l2)

        @pl.when(j == 0)
        def _():
            ds_ref[...] = jnp.zeros(ds_ref.shape, F32)
            acc1[...] = jnp.zeros(acc1.shape, F32)
            acc2[...] = jnp.zeros(acc2.shape, F32)

        dq_new = [[], [], []]
        dk_cols, dv_cols = [], []
        for g in range(N_HEADS // 2):
            cols = slice(g * LANES, (g + 1) * LANES)
            kg = k_ref[:, cols]
            vg = v_ref[:, cols]
            dkg = jnp.zeros((ATT_BLK, LANES), F32)
            dvg = jnp.zeros((ATT_BLK, LANES), F32)
            dqg = [jnp.zeros((ATT_BLK, LANES), F32) for _ in range(N_ATT_TILES)]
            for d in range(N_ATT_TILES):
                qg = q_refs[d][:, cols]
                dog = do_refs[d][:, cols]
                prod = dog.astype(F32) * o_refs[d][:, cols].astype(F32)
                for h in (2 * g, 2 * g + 1):
                    hm = _head_mask(h)
                    qh = jnp.where(hm, qg, jnp.zeros_like(qg))
                    doh = jnp.where(hm, dog, jnp.zeros_like(dog))
                    kh = jnp.where(hm, kg, jnp.zeros_like(kg))
                    delta = jnp.sum(jnp.where(hm, prod, 0.0), axis=-1, keepdims=True)
                    s = _dot_nt(qh, kg) * scale + b_ref[d * N_HEADS + h]
                    p = jnp.exp(s - l_refs[d][:, h:h + 1])
                    p = jnp.where(j + d < nb, p, 0.0)
                    dvg = dvg + _dot_tn(p.astype(BF16), doh)
                    dpm = _dot_nt(doh, vg)
                    dsc = p * (dpm - delta)
                    ds_ref[d * N_HEADS + h] += dsc
                    dsb = (dsc * scale).astype(BF16)
                    dqg[d] = dqg[d] + _dot(dsb, kh)
                    dkg = dkg + _dot_tn(dsb, qh)
            for d in range(N_ATT_TILES):
                dq_new[d].append(dqg[d])
            dk_cols.append(dkg)
            dv_cols.append(dvg)
        x0, x1, x2 = (jnp.concatenate(c, axis=1) for c in dq_new)
        dp_ref[:, 0:1024] = jnp.zeros((ATT_BLK, 1024), BF16)
        dp_ref[:, 1024:1536] = (acc1[...] + x0).astype(BF16)
        dp_ref[:, 1536:2048] = jnp.concatenate(dk_cols, axis=1).astype(BF16)
        dp_ref[:, 2048:2560] = jnp.concatenate(dv_cols, axis=1).astype(BF16)
        acc1[...] = acc2[...] + x1
        acc2[...] = x2

    def fwd_spec(d, width, col):
        return pl.BlockSpec((ATT_BLK, width), lambda j: (jnp.minimum(j + d, nb - 1), col))

    return pl.pallas_call(
        body, name="bwd_attn", grid=(nb,),
        in_specs=[pl.BlockSpec((ATT_BLK, ATTN_W), lambda j: (j, 1)),
                  pl.BlockSpec((ATT_BLK, ATTN_W), lambda j: (j, 2)),
                  fwd_spec(0, ATTN_W, 0), fwd_spec(1, ATTN_W, 0), fwd_spec(2, ATTN_W, 0),
                  fwd_spec(0, ATTN_W, 0), fwd_spec(1, ATTN_W, 0), fwd_spec(2, ATTN_W, 0),
                  fwd_spec(0, ATTN_W, 0), fwd_spec(1, ATTN_W, 0), fwd_spec(2, ATTN_W, 0),
                  fwd_spec(0, LANES, 0), fwd_spec(1, LANES, 0), fwd_spec(2, LANES, 0),
                  pl.BlockSpec((N_ATT_TILES * N_HEADS, ATT_BLK, ATT_BLK), lambda j: (0, 0, 0))],
        out_specs=[pl.BlockSpec((ATT_BLK, IN_COLS), lambda j: (j, 0)),
                   pl.BlockSpec((N_ATT_TILES * N_HEADS, ATT_BLK, ATT_BLK), lambda j: (0, 0, 0))],
        out_shape=[jax.ShapeDtypeStruct((T, IN_COLS), BF16),
                   jax.ShapeDtypeStruct((N_ATT_TILES * N_HEADS, ATT_BLK, ATT_BLK), F32)],
        scratch_shapes=[pltpu.VMEM((ATT_BLK, ATTN_W), F32), pltpu.VMEM((ATT_BLK, ATTN_W), F32)],
        compiler_params=_params(56, 1),
    )(qkv, qkv, qkv, qkv, qkv, dao, dao, dao, ao, ao, ao, lse, lse, lse, bias)


def _bwd_conv(dproj, a, dco, hc, cw, lg, lb):
    T = a.shape[0]
    tm = 512
    rc = 32
    ni = T // tm
    hb = tm // CONV_HALO

    def body(dp_in, a_ref, ap_ref, dco_ref, dcon_ref, hc_ref, hcn_ref, w_ref, lg_ref, lb_ref,
             dp_ref, dw_ref, db_ref, dlg_ref, dlb_ref, hext, dext, hsh, dsh, dwacc):
        del dp_in
        i = pl.program_id(0)

        def ln_bwd(dco_v, hc_v):
            mu = jnp.mean(hc_v, axis=-1, keepdims=True)
            xc = hc_v - mu
            rstd = lax.rsqrt(jnp.mean(xc * xc, axis=-1, keepdims=True) + EPS)
            xh = xc * rstd
            z = xh * lg_ref[...] + lb_ref[...]
            sg = _sigmoid(z)
            dz = dco_v * (sg * (1.0 + z * (1.0 - sg)))
            dxh = dz * lg_ref[...]
            dhc = rstd * (dxh - jnp.mean(dxh, axis=-1, keepdims=True)
                          - xh * jnp.mean(dxh * xh, axis=-1, keepdims=True))
            return dhc, dz * xh, dz

        hext[0:CONV_HALO, :] = jnp.where(i > 0, ap_ref[:, :CONV_W] * _sigmoid(ap_ref[:, CONV_W:]), 0.0)
        hext[CONV_HALO:CONV_HALO + tm, :] = a_ref[:, :CONV_W] * _sigmoid(a_ref[:, CONV_W:])
        dhc, dlg_rows, dlb_rows = ln_bwd(dco_ref[...], hc_ref[...])
        dext[0:tm, :] = dhc
        dhc_next, _, _ = ln_bwd(dcon_ref[...], hcn_ref[...])
        dext[tm:tm + CONV_HALO, :] = jnp.where(i < ni - 1, dhc_next, 0.0)

        @pl.when(i == 0)
        def _():
            dw_ref[...] = jnp.zeros(dw_ref.shape, F32)
            db_ref[...] = jnp.zeros(db_ref.shape, F32)
            dlg_ref[...] = jnp.zeros(dlg_ref.shape, F32)
            dlb_ref[...] = jnp.zeros(dlb_ref.shape, F32)

            dwacc[...] = jnp.zeros(dwacc.shape, F32)

        db_ref[...] += jnp.sum(dhc, axis=0, keepdims=True)
        dlg_ref[...] += jnp.sum(dlg_rows, axis=0, keepdims=True)
        dlb_ref[...] += jnp.sum(dlb_rows, axis=0, keepdims=True)
        _fill_shifted(hext, hsh, tm)
        _fill_shifted(dext, dsh, tm)
        for c in range(tm // rc):
            r0 = c * rc
            dh = jnp.zeros((rc, CONV_W), F32)
            dhc_c = dext[r0:r0 + rc, :]
            for k in range(CONV_K):
                dh = dh + w_ref[k:k + 1, :] * _shifted_rows(dext, dsh, r0 + 30 - k, rc)
                prod = dhc_c * _shifted_rows(hext, hsh, r0 + 2 + k, rc)
                dwacc[k] += jnp.sum(prod.reshape(rc // SUBLANES, SUBLANES, CONV_W), axis=0)
            av = a_ref[r0:r0 + rc, :CONV_W]
            sg = _sigmoid(a_ref[r0:r0 + rc, CONV_W:])
            dp_ref[r0:r0 + rc, 0:CONV_W] = (dh * sg).astype(BF16)
            dp_ref[r0:r0 + rc, CONV_W:] = (dh * av * sg * (1.0 - sg)).astype(BF16)

        @pl.when(i == ni - 1)
        def _():
            dw_ref[...] = jnp.sum(dwacc[...], axis=1)

    row = lambda w: pl.BlockSpec((tm, w), lambda i: (i, 0))
    prev = lambda w: pl.BlockSpec((CONV_HALO, w), lambda i: (jnp.maximum(i * hb - 1, 0), 0))
    nxt = lambda w: pl.BlockSpec((CONV_HALO, w), lambda i: (jnp.minimum((i + 1) * hb, ni * hb - 1), 0))
    vec = pl.BlockSpec((1, CONV_W), lambda i: (0, 0))
    return pl.pallas_call(
        body, name="bwd_conv", grid=(ni,),
        in_specs=[ANY, row(1024), prev(1024), row(CONV_W), nxt(CONV_W), row(CONV_W), nxt(CONV_W),
                  pl.BlockSpec((CONV_HALO, CONV_W), lambda i: (0, 0)), vec, vec],
        out_specs=[pl.BlockSpec((tm, 1024), lambda i: (i, 0)),
                   pl.BlockSpec((CONV_HALO, CONV_W), lambda i: (0, 0)), vec, vec, vec],
        out_shape=[jax.ShapeDtypeStruct((T, IN_COLS), BF16),
                   jax.ShapeDtypeStruct((CONV_HALO, CONV_W), F32),
                   jax.ShapeDtypeStruct((1, CONV_W), F32),
                   jax.ShapeDtypeStruct((1, CONV_W), F32),
                   jax.ShapeDtypeStruct((1, CONV_W), F32)],
        scratch_shapes=[pltpu.VMEM((tm + CONV_HALO, CONV_W), F32), pltpu.VMEM((tm + CONV_HALO, CONV_W), F32),
                        pltpu.VMEM((SUBLANES - 1, tm + CONV_HALO - SUBLANES, CONV_W), F32),
                        pltpu.VMEM((SUBLANES - 1, tm + CONV_HALO - SUBLANES, CONV_W), F32),
                        pltpu.VMEM((CONV_HALO, SUBLANES, CONV_W), F32)],
        input_output_aliases={0: 0},
        compiler_params=_params(56, 1),
    )(dproj, a, a, dco, dco, hc, hc, cw, lg, lb)


def _bwd_in_proj(dproj, w_in, x, dh1, g1):
    T = x.shape[0]
    tm = 512

    def body(dp_ref, w_ref, x_ref, dh_ref, g_ref, gx_ref, dg_ref):
        i = pl.program_id(0)
        du = None
        for s in range(N_CHIPS):
            term = _dot_nt(dp_ref[:, IN_SHARD * s:IN_SHARD * (s + 1)], w_ref[s])
            du = term if du is None else du + term
        dx, dg_rows = _rms_bwd(du, x_ref[...], g_ref[...])
        gx_ref[...] = dh_ref[...] + dx
        dg = jnp.sum(dg_rows, axis=0, keepdims=True)

        @pl.when(i == 0)
        def _():
            dg_ref[...] = dg

        @pl.when(i > 0)
        def _():
            dg_ref[...] += dg

    row = lambda w: pl.BlockSpec((tm, w), lambda i: (i, 0))
    vec = pl.BlockSpec((1, D_MODEL), lambda i: (0, 0))
    return pl.pallas_call(
        body, name="bwd_in_proj", grid=(T // tm,),
        in_specs=[row(IN_COLS), pl.BlockSpec((N_CHIPS, D_MODEL, IN_SHARD), lambda i: (0, 0, 0)),
                  row(D_MODEL), row(D_MODEL), vec],
        out_specs=[row(D_MODEL), vec],
        out_shape=[jax.ShapeDtypeStruct((T, D_MODEL), F32), jax.ShapeDtypeStruct((1, D_MODEL), F32)],
        compiler_params=_params(40, 1),
    )(dproj, w_in, x, dh1, g1)


def _wgrad(name, a_list, a_spec, b, b_spec, out_block, out_spec, out_shape, n_outer, T, tk=512, select=None):
    def body(*refs):
        a_refs, b_ref, o_ref = refs[:len(a_list)], refs[len(a_list)], refs[len(a_list) + 1]
        kt = pl.program_id(1)

        @pl.when(kt == 0)
        def _():
            o_ref[...] = jnp.zeros(o_ref.shape, F32)

        bv = b_ref[...].reshape(b_ref.shape[-2:])
        if select is None:
            o_ref[...] += _dot_tn(a_refs[0][...].reshape(a_refs[0].shape[-2:]), bv).reshape(o_ref.shape)
        else:
            for n, a_ref in enumerate(a_refs):
                @pl.when(select(pl.program_id(0)) == n)
                def _():
                    o_ref[...] += _dot_tn(a_ref[...], bv).reshape(o_ref.shape)

    del out_block
    return pl.pallas_call(
        body, name=name, grid=(n_outer, T // tk),
        in_specs=[a_spec] * len(a_list) + [b_spec],
        out_specs=out_spec, out_shape=out_shape,
        compiler_params=_params(48, 2),
    )(*a_list, b)


def _mesh_pos():
    return lax.axis_index("x"), lax.axis_index("y"), lax.axis_index("c")


def _other_chips(x, y):
    return [((1 - x, y), 2 * (1 - x) + y), ((x, 1 - y), 2 * x + (1 - y)), ((1 - x, 1 - y), 2 * (1 - x) + (1 - y))]


def _all_gather_weights(shards):
    n = len(shards)

    def body(*refs):
        out_refs = refs[n:2 * n]
        send_sems, recv_sems = refs[2 * n:]
        x, y, c = _mesh_pos()
        me = 2 * x + y
        sibling = (x, y, 1 - c)
        chips = _other_chips(x, y)
        first, passed = [], []
        for t in range(n):
            half = shards[t].shape[1] // 2
            mine = out_refs[t].at[me, pl.ds(c * half, half)]
            for k, (chip, _) in enumerate(chips):
                cp = pltpu.make_async_remote_copy(
                    src_ref=mine, dst_ref=mine, send_sem=send_sems.at[t, k], recv_sem=recv_sems.at[t, k],
                    device_id=(*chip, c), device_id_type=MESH)
                cp.start()
                first.append(cp)
        for t in range(n):
            half = shards[t].shape[1] // 2
            rows = pl.ds(c * half, half)
            for k, (chip, s) in enumerate(chips):
                landed = out_refs[t].at[s, rows]
                pltpu.make_async_remote_copy(
                    src_ref=landed, dst_ref=landed, send_sem=send_sems.at[t, k], recv_sem=recv_sems.at[t, k],
                    device_id=(*chip, c), device_id_type=MESH).wait_recv()
                cp = pltpu.make_async_remote_copy(
                    src_ref=landed, dst_ref=landed, send_sem=send_sems.at[t, 3 + k], recv_sem=recv_sems.at[t, 3 + k],
                    device_id=sibling, device_id_type=MESH)
                cp.start()
                passed.append(cp)
        for t in range(n):
            half = shards[t].shape[1] // 2
            other = pl.ds((1 - c) * half, half)
            for k, (chip, s) in enumerate(chips):
                got = out_refs[t].at[s, other]
                pltpu.make_async_remote_copy(
                    src_ref=got, dst_ref=got, send_sem=send_sems.at[t, 3 + k], recv_sem=recv_sems.at[t, 3 + k],
                    device_id=sibling, device_id_type=MESH).wait_recv()
        for cp in first + passed:
            cp.wait_send()

    return pl.pallas_call(
        body, name="all_gather_weights",
        in_specs=[ANY] * n, out_specs=[ANY] * n,
        out_shape=[jax.ShapeDtypeStruct(s.shape, s.dtype) for s in shards],
        scratch_shapes=[pltpu.SemaphoreType.DMA((n, 6)), pltpu.SemaphoreType.DMA((n, 6))],
        input_output_aliases={t: t for t in range(n)},
    )(*shards)


def _pair_exchange(grads):
    n = len(grads)

    def body(*refs):
        g_refs, got_refs = refs[:n], refs[n:2 * n]
        send_sems, recv_sems = refs[2 * n:]
        x, y, c = _mesh_pos()
        copies = []
        for t in range(n):
            half = grads[t].shape[1] // 2
            cp = pltpu.make_async_remote_copy(
                src_ref=g_refs[t].at[:, pl.ds((1 - c) * half, half), :], dst_ref=got_refs[t],
                send_sem=send_sems.at[t], recv_sem=recv_sems.at[t],
                device_id=(x, y, 1 - c), device_id_type=MESH)
            cp.start()
            copies.append(cp)
        for cp in copies:
            cp.wait()

    return pl.pallas_call(
        body, name="pair_exchange",
        in_specs=[ANY] * n, out_specs=[ANY] * n,
        out_shape=[jax.ShapeDtypeStruct((N_CHIPS, g.shape[1] // 2, g.shape[2]), F32) for g in grads],
        scratch_shapes=[pltpu.SemaphoreType.DMA((n,)), pltpu.SemaphoreType.DMA((n,))],
    )(*grads)


def _chip_exchange(pairs):
    n = len(pairs)

    def body(*refs):
        p_refs, got_refs = refs[:n], refs[n:2 * n]
        send_sems, recv_sems = refs[2 * n:]
        x, y, c = _mesh_pos()
        copies = []
        for t in range(n):
            for k, (chip, s) in enumerate(_other_chips(x, y)):
                cp = pltpu.make_async_remote_copy(
                    src_ref=p_refs[t].at[s], dst_ref=got_refs[t].at[k],
                    send_sem=send_sems.at[t, k], recv_sem=recv_sems.at[t, k],
                    device_id=(*chip, c), device_id_type=MESH)
                cp.start()
                copies.append(cp)
        for cp in copies:
            cp.wait()

    return pl.pallas_call(
        body, name="chip_exchange",
        in_specs=[ANY] * n, out_specs=[ANY] * n,
        out_shape=[jax.ShapeDtypeStruct((3,) + p.shape[1:], p.dtype) for p in pairs],
        scratch_shapes=[pltpu.SemaphoreType.DMA((n, 3)), pltpu.SemaphoreType.DMA((n, 3))],
    )(*pairs)


def _pair_gather(halves):
    n = len(halves)

    def body(*refs):
        out_refs = refs[n:2 * n]
        send_sems, recv_sems = refs[2 * n:]
        x, y, c = _mesh_pos()
        copies = []
        for t in range(n):
            half = halves[t].shape[0] // 2
            mine = out_refs[t].at[pl.ds(c * half, half)]
            theirs = out_refs[t].at[pl.ds((1 - c) * half, half)]
            send = pltpu.make_async_remote_copy(
                src_ref=mine, dst_ref=mine, send_sem=send_sems.at[t], recv_sem=recv_sems.at[t],
                device_id=(x, y, 1 - c), device_id_type=MESH)
            send.start()
            recv = pltpu.make_async_remote_copy(
                src_ref=theirs, dst_ref=theirs, send_sem=send_sems.at[t], recv_sem=recv_sems.at[t],
                device_id=(x, y, 1 - c), device_id_type=MESH)
            copies.append((send, recv))
        for send, recv in copies:
            send.wait_send()
            recv.wait_recv()

    return pl.pallas_call(
        body, name="pair_gather",
        in_specs=[ANY] * n, out_specs=[ANY] * n,
        out_shape=[jax.ShapeDtypeStruct(h.shape, F32) for h in halves],
        scratch_shapes=[pltpu.SemaphoreType.DMA((n,)), pltpu.SemaphoreType.DMA((n,))],
        input_output_aliases={t: t for t in range(n)},
    )(*halves)


def _all_reduce_small(pack):
    rows = pack.shape[0]

    def body(p_ref, o_ref, buf, send_sems, recv_sems):
        x, y, c = _mesh_pos()
        me = 4 * x + 2 * y + c
        buf[0] = p_ref[...]
        copies = []
        for k in range(1, 8):
            peer = (x ^ (k >> 2), y ^ ((k >> 1) & 1), c ^ (k & 1))
            cp = pltpu.make_async_remote_copy(
                src_ref=p_ref, dst_ref=buf.at[k], send_sem=send_sems.at[k - 1], recv_sem=recv_sems.at[k - 1],
                device_id=peer, device_id_type=MESH)
            cp.start()
            copies.append(cp)
        for cp in copies:
            cp.wait()
        total = buf[me]
        for dev in range(1, 8):
            total = total + buf[me ^ dev]
        o_ref[...] = total

    return pl.pallas_call(
        body, name="all_reduce_small",
        in_specs=[VMEM_FULL], out_specs=VMEM_FULL,
        out_shape=jax.ShapeDtypeStruct(pack.shape, F32),
        scratch_shapes=[pltpu.VMEM((8, rows, LANES), F32),
                        pltpu.SemaphoreType.DMA((7,)), pltpu.SemaphoreType.DMA((7,))],
    )(pack)


def _row_block(rows):
    if rows <= 512:
        return rows
    for rb in (256, 352):
        if rows % rb == 0:
            return rb
    raise ValueError(f"no row block for {rows} rows")


def _place(name, w, pos, dtype):
    R, C = w.shape
    rb = _row_block(R)

    def body(pos_ref, w_ref, o_ref):
        del pos_ref
        o_ref[0] = w_ref[...].astype(dtype)

    return pl.pallas_call(
        body, name=name,
        grid_spec=pltpu.PrefetchScalarGridSpec(
            num_scalar_prefetch=1, grid=(R // rb,),
            in_specs=[pl.BlockSpec((rb, C), lambda r, p: (r, 0))],
            out_specs=pl.BlockSpec((1, rb, C), lambda r, p: (p[0], r, 0))),
        out_shape=jax.ShapeDtypeStruct((N_CHIPS, R, C), dtype),
        compiler_params=_params(32, 1),
    )(pos, w)


def _pair_sum(name, g, got, pos):
    S, R, C = g.shape
    half = R // 2
    rb = _row_block(half)
    nh = half // rb

    def body(pos_ref, a_ref, b_ref, o_ref):
        del pos_ref
        o_ref[...] = (a_ref[...] + b_ref[...]).astype(BF16)

    spec = pl.BlockSpec((1, rb, C), lambda s, r, p: (s, r, 0))
    return pl.pallas_call(
        body, name=name,
        grid_spec=pltpu.PrefetchScalarGridSpec(
            num_scalar_prefetch=1, grid=(S, nh),
            in_specs=[pl.BlockSpec((1, rb, C), lambda s, r, p: (s, p[1] * nh + r, 0)), spec],
            out_specs=spec),
        out_shape=jax.ShapeDtypeStruct((S, half, C), BF16), compiler_params=_params(32, 2),
    )(pos, g, got)


def _chip_sum(name, pairs, got, pos):
    _, half, C = pairs.shape
    rb = _row_block(half)
    nh = half // rb

    def body(pos_ref, a_ref, g_ref, o_ref):
        del pos_ref
        o_ref[...] = ((a_ref[0].astype(F32) + g_ref[0].astype(F32)) + g_ref[1].astype(F32)) + g_ref[2].astype(F32)

    return pl.pallas_call(
        body, name=name,
        grid_spec=pltpu.PrefetchScalarGridSpec(
            num_scalar_prefetch=1, grid=(nh,),
            in_specs=[pl.BlockSpec((1, rb, C), lambda r, p: (p[0], r, 0)),
                      pl.BlockSpec((3, rb, C), lambda r, p: (0, r, 0))],
            out_specs=pl.BlockSpec((rb, C), lambda r, p: (p[1] * nh + r, 0))),
        out_shape=jax.ShapeDtypeStruct((2 * half, C), F32), compiler_params=_params(32, 1),
    )(pos, pairs, got)


def _adamw(name, w, g, m, v):
    R, C = w.shape
    rb = _row_block(R)
    c1 = 1.0 - ADAM_B1 ** ADAM_STEP
    c2 = 1.0 - ADAM_B2 ** ADAM_STEP

    def body(w_ref, g_ref, m_ref, v_ref, d_ref, nm_ref, nv_ref):
        gv = g_ref[...]
        nm = ADAM_B1 * m_ref[...] + (1.0 - ADAM_B1) * gv
        nv = ADAM_B2 * v_ref[...] + (1.0 - ADAM_B2) * (gv * gv)
        nm_ref[...] = nm
        nv_ref[...] = nv
        d_ref[...] = -ADAM_LR * ((nm / c1) / (jnp.sqrt(nv / c2) + ADAM_EPS) + ADAM_WD * w_ref[...])

    spec = pl.BlockSpec((rb, C), lambda r: (r, 0))
    sds = jax.ShapeDtypeStruct(w.shape, F32)
    return pl.pallas_call(
        body, name=name, grid=(R // rb,), in_specs=[spec] * 4, out_specs=[spec] * 3,
        out_shape=[sds, sds, sds], compiler_params=_params(40, 1),
    )(w, g, m, v)


def _rel_index():
    m = np.arange(2 * ATT_BLK)
    off = np.where(m < ATT_BLK, m, m - 2 * ATT_BLK)
    rel = np.stack([ATT_BLK * d - off for d in range(N_ATT_TILES)])
    return np.clip(rel, -MAX_REL, MAX_REL) + MAX_REL


def _local_step(x, tgt, g1, w_in, cw, cb, lg, lb, rel, w_out, g2, g3, w_up, fw, fb, w_down, g4):
    T = x.shape[0]
    idx = _rel_index()
    vec = jnp.transpose(rel[:, idx], (1, 0, 2)).reshape(N_ATT_TILES * N_HEADS, 1, 2 * ATT_BLK)
    bias = _bias_tiles(vec)

    u, a, qkv = _fwd_in_proj(x, g1, w_in)
    co, hc = _fwd_conv(a, cw, cb, lg, lb)
    ao, lse = _fwd_attn(qkv, bias)
    mixed, h1, u2 = _fwd_out_proj(co, ao, w_out, x, g2, g3)
    hf, fp = _fwd_ffn(u2, w_up, fw, fb, w_down)
    loss, dy, df, dg4 = _fwd_loss(fp, h1, tgt, g4)

    du2p, dhf, act, dfw_g, dfw_v = _bwd_ffn(df, hf, w_up, fw, fb, w_down)
    dh1, dmx, dco, dao, dg3, dg2 = _bwd_mid(du2p, dy, h1, mixed, g3, g2, w_out)
    dproj, dsacc = _bwd_attn(qkv, ao, dao, lse, bias)
    dproj, dcw, dcb, dlg, dlb = _bwd_conv(dproj, a, dco, hc, cw, lg, lb)
    gx, dg1 = _bwd_in_proj(dproj, w_in, x, dh1, g1)

    tk = 512
    gw_in = _wgrad(
        "wgrad_in", [u], pl.BlockSpec((tk, D_MODEL), lambda s, k: (k, 0)),
        dproj, pl.BlockSpec((tk, IN_SHARD), lambda s, k: (k, s)), None,
        pl.BlockSpec((1, D_MODEL, IN_SHARD), lambda s, k: (s, 0, 0)),
        jax.ShapeDtypeStruct((N_CHIPS, D_MODEL, IN_SHARD), F32), N_CHIPS, T, tk)
    gw_up = _wgrad(
        "wgrad_up", [u2], pl.BlockSpec((tk, D_MODEL), lambda s, k: (k, 0)),
        dhf, pl.BlockSpec((1, tk, FF_SHARD), lambda s, k: (s // 2, k, s % 2)), None,
        pl.BlockSpec((1, D_MODEL, FF_SHARD), lambda s, k: (s, 0, 0)),
        jax.ShapeDtypeStruct((N_CHIPS, D_MODEL, FF_SHARD), F32), N_CHIPS, T, tk)
    gw_down = _wgrad(
        "wgrad_down", [act], pl.BlockSpec((tk, FF_SHARD), lambda s, k: (k, s)),
        df, pl.BlockSpec((tk, D_MODEL), lambda s, k: (k, 0)), None,
        pl.BlockSpec((FF_SHARD, D_MODEL), lambda s, k: (s, 0)),
        jax.ShapeDtypeStruct((D_FF, D_MODEL), F32), 2, T, tk)
    gw_out = _wgrad(
        "wgrad_out", [co, ao], pl.BlockSpec((tk, CONV_W), lambda s, k: (k, 0)),
        dmx, pl.BlockSpec((tk, D_MODEL), lambda s, k: (k, 0)), None,
        pl.BlockSpec((CONV_W, D_MODEL), lambda s, k: (s, 0)),
        jax.ShapeDtypeStruct((D_MODEL, D_MODEL), F32), 2, T, tk, select=lambda s: s)

    diag = _diag_sums(dsacc).reshape(N_ATT_TILES, N_HEADS, 2 * ATT_BLK)
    onehot = np.zeros((N_ATT_TILES, 2 * ATT_BLK, 2 * MAX_REL + 1), np.float32)
    for d in range(N_ATT_TILES):
        onehot[d, np.arange(2 * ATT_BLK), idx[d]] = 1.0
    drel = jnp.einsum("dhm,dmr->hr", diag, jnp.asarray(onehot), precision=lax.Precision.HIGHEST)

    small = dict(norm_mix_pre=dg1, conv_dw_w=dcw[:CONV_K], conv_dw_b=dcb, conv_ln_g=dlg, conv_ln_b=dlb,
                 rel_bias=drel, norm_mix_post=dg2, norm_ffn_pre=dg3,
                 ffn_dw_w=jnp.concatenate([dfw_g[0, :3], dfw_g[1, :3], dfw_v[0, :3], dfw_v[1, :3]], axis=1),
                 ffn_dw_b=jnp.concatenate([dfw_g[0, 3:4], dfw_g[1, 3:4], dfw_v[0, 3:4], dfw_v[1, 3:4]], axis=1),
                 norm_ffn_post=dg4)
    big = dict(w_in=gw_in, w_out=gw_out.reshape(N_CHIPS, D_MODEL // N_CHIPS, D_MODEL), w_up=gw_up,
               w_down=gw_down.reshape(N_CHIPS, D_FF // N_CHIPS, D_MODEL))
    return loss, gx, small, big


SMALL_ORDER = ["norm_mix_pre", "conv_dw_b", "conv_ln_g", "conv_ln_b", "rel_bias", "norm_mix_post",
               "norm_ffn_pre", "ffn_dw_b", "norm_ffn_post", "conv_dw_w", "ffn_dw_w"]


def _pack(parts):
    rows = []
    for p in parts:
        width = -(-p.shape[1] // LANES) * LANES
        rows.append(jnp.pad(p, ((0, 0), (0, width - p.shape[1]))).reshape(-1, LANES))
    packed = jnp.concatenate(rows, axis=0)
    pad = -packed.shape[0] % 8
    return jnp.pad(packed, ((0, pad), (0, 0)))


def _unpack(packed, shapes):
    out, r = [], 0
    for shp in shapes:
        width = -(-shp[1] // LANES) * LANES
        n = shp[0] * width // LANES
        out.append(packed[r:r + n].reshape(shp[0], width)[:, :shp[1]])
        r += n
    return out


WEIGHTS = ["norm_mix_pre", "w_in", "conv_dw_w", "conv_dw_b", "conv_ln_g", "conv_ln_b", "rel_bias", "w_out",
           "norm_mix_post", "norm_ffn_pre", "w_up", "ffn_dw_w", "ffn_dw_b", "w_down", "norm_ffn_post"]
BIG = ["w_in", "w_out", "w_up", "w_down"]


def kernel(x, norm_mix_pre, w_in, conv_dw_w, conv_dw_b, conv_ln_g, conv_ln_b, rel_bias, w_out, norm_mix_post, norm_ffn_pre, w_up, ffn_dw_w, ffn_dw_b, w_down, norm_ffn_post, loss_target, m_norm_mix_pre, m_w_in, m_conv_dw_w, m_conv_dw_b, m_conv_ln_g, m_conv_ln_b, m_rel_bias, m_w_out, m_norm_mix_post, m_norm_ffn_pre, m_w_up, m_ffn_dw_w, m_ffn_dw_b, m_w_down, m_norm_ffn_post, v_norm_mix_pre, v_w_in, v_conv_dw_w, v_conv_dw_b, v_conv_ln_g, v_conv_ln_b, v_rel_bias, v_w_out, v_norm_mix_post, v_norm_ffn_pre, v_w_up, v_ffn_dw_w, v_ffn_dw_b, v_w_down, v_norm_ffn_post):
    args = locals()
    w = {n: args[n][0] for n in WEIGHTS}
    m = {n: args["m_" + n][0] for n in WEIGHTS}
    v = {n: args["v_" + n][0] for n in WEIGHTS}
    for d in (w, m, v):
        d["rel_bias"] = d["rel_bias"].reshape(N_HEADS, 2 * MAX_REL + 1)
        for n in ("norm_mix_pre", "conv_dw_b", "conv_ln_g", "conv_ln_b", "norm_mix_post", "norm_ffn_pre",
                  "ffn_dw_b", "norm_ffn_post"):
            d[n] = d[n].reshape(1, -1)
    shard = 2 * lax.axis_index("x") + lax.axis_index("y")

    cw_sh = jnp.pad(w["conv_dw_w"], ((0, CONV_HALO - CONV_K), (0, 0)))
    fw_sh = jnp.pad(w["ffn_dw_w"], ((0, FF_HALO - 3), (0, 0)))
    pos = jnp.stack([shard, lax.axis_index("c")]).astype(jnp.int32)
    shards = [_place("place_" + n, w[n], pos, BF16) for n in BIG]
    shards += [_place("place_conv_dw_w", cw_sh, pos, F32), _place("place_ffn_dw_w", fw_sh, pos, F32)]
    w_in_f, w_out_f, w_up_f, w_down_f, cw_f, fw_f = _all_gather_weights(shards)
    cw_full = jnp.transpose(cw_f, (1, 0, 2)).reshape(CONV_HALO, CONV_W)

    loss, gx, small, big = _local_step(
        x[0], loss_target[0], w["norm_mix_pre"], w_in_f, cw_full, w["conv_dw_b"], w["conv_ln_g"],
        w["conv_ln_b"], w["rel_bias"], w_out_f.reshape(D_MODEL, D_MODEL), w["norm_mix_post"],
        w["norm_ffn_pre"], w_up_f, fw_f, w["ffn_dw_b"].reshape(N_CHIPS, 1, FF_SHARD),
        w_down_f.reshape(D_FF, D_MODEL), w["norm_ffn_post"])

    theirs = _pair_exchange([big[n] for n in BIG])
    pairs = [_pair_sum("pair_sum_" + n, big[n], b, pos) for n, b in zip(BIG, theirs)]
    got = _chip_exchange(pairs)
    halves = [_chip_sum("chip_sum_" + n, a, b, pos) for n, a, b in zip(BIG, pairs, got)]
    full = _pair_gather(halves)
    grads, deltas, new_m, new_v = {}, {}, {}, {}
    for n, g in zip(BIG, full):
        grads[n] = g
        deltas[n], new_m[n], new_v[n] = _adamw("adamw_" + n, w[n], g, m[n], v[n])

    gsum = _all_reduce_small(_pack([small[n] for n in SMALL_ORDER]))
    shapes = [small[n].shape for n in SMALL_ORDER]
    gs = dict(zip(SMALL_ORDER, _unpack(gsum, shapes)))
    gs["conv_dw_w"] = lax.dynamic_slice_in_dim(gs["conv_dw_w"], shard * LANES, LANES, axis=1)
    gs["ffn_dw_w"] = lax.dynamic_slice_in_dim(gs["ffn_dw_w"], shard * FF_SHARD, FF_SHARD, axis=1)
    shapes = [gs[n].shape for n in SMALL_ORDER]
    d_p, m_p, v_p = _adamw("adamw_small", _pack([w[n] for n in SMALL_ORDER]), _pack([gs[n] for n in SMALL_ORDER]),
                           _pack([m[n] for n in SMALL_ORDER]), _pack([v[n] for n in SMALL_ORDER]))
    for dst, packed in ((deltas, d_p), (new_m, m_p), (new_v, v_p)):
        dst.update(zip(SMALL_ORDER, _unpack(packed, shapes)))
    grads.update(gs)

    total = lax.psum(loss[0, 0], ("x", "y", "c"))
    outs = [total, gx[None]]
    for group in (grads, deltas, new_m, new_v):
        outs += [group[n].reshape(args[n].shape) for n in WEIGHTS]
    return tuple(outs)
```

```python
import functools
import math

import numpy as np
import jax
import jax.numpy as jnp
from jax import lax
from jax.experimental import pallas as pl
from jax.experimental.pallas import tpu as pltpu

F32 = jnp.float32
BF16 = jnp.bfloat16

D_MODEL = 1024
CONV_W = 512
ATTN_W = 512
N_HEADS = 8
HEAD_DIM = 64
CHUNK = 64
N_LEFT = 8
MAX_REL = 128
CONV_K = 31
CONV_HALO = 32
D_FF = 2816
FF_SHARD = 1408
IN_COLS = 2560
IN_SHARD = 640
EPS = 1e-6
NEG_INF = -1e30
ATT_BLK = 256
N_ATT_TILES = 3
LANES = 128
SUBLANES = 8
N_CHIPS = 4

ADAM_LR = 0.001
ADAM_B1 = 0.9
ADAM_B2 = 0.999
ADAM_EPS = 1e-08
ADAM_WD = 0.01
ADAM_STEP = 10

MESH = pl.DeviceIdType.MESH
ANY = pl.BlockSpec(memory_space=pl.ANY)
VMEM_FULL = pl.BlockSpec(memory_space=pltpu.VMEM)


def _params(vmem_mb, n_grid=0):
    sem = ("arbitrary",) * n_grid if n_grid else None
    return pltpu.CompilerParams(dimension_semantics=sem, vmem_limit_bytes=vmem_mb << 20)


def _hbm_call(body, *, out_shape, **kwargs):
    pinned = jax.tree.map(lambda s: pltpu.HBM(s.shape, s.dtype), out_shape)
    call = pl.pallas_call(body, out_shape=pinned, **kwargs)

    def run(*args):
        return call(*[a if jnp.issubdtype(a.dtype, jnp.integer) else pltpu.with_memory_space_constraint(a, pltpu.HBM)
                      for a in args])

    return run


def _sigmoid(v):
    return 1.0 / (1.0 + jnp.exp(-v))


def _dot(a, b):
    return jnp.dot(a, b, preferred_element_type=F32)


def _dot_nt(a, b):
    return lax.dot_general(a, b, (((1,), (1,)), ((), ())), preferred_element_type=F32)


def _dot_tn(a, b):
    return lax.dot_general(a, b, (((0,), (0,)), ((), ())), preferred_element_type=F32)


def _rms_fwd(v, g):
    r = lax.rsqrt(jnp.mean(v * v, axis=-1, keepdims=True) + EPS)
    return v * r * g, r


def _rms_bwd(dy, v, g):
    r = lax.rsqrt(jnp.mean(v * v, axis=-1, keepdims=True) + EPS)
    vh = v * r
    dvh = dy * g
    dv = r * (dvh - vh * jnp.mean(dvh * vh, axis=-1, keepdims=True))
    return dv, dy * vh


def _fwd_in_proj(x, g1, w_in):
    T = x.shape[0]
    tm = 512

    def body(x_ref, g_ref, w_ref, u_ref, a_ref, qkv_ref):
        u, _ = _rms_fwd(x_ref[...], g_ref[...])
        u = u.astype(BF16)
        u_ref[...] = u
        for s in range(N_CHIPS):
            y = _dot(u, w_ref[s])
            lo, hi = IN_SHARD * s, IN_SHARD * (s + 1)
            if hi <= 1024:
                a_ref[:, lo:hi] = y
            elif lo >= 1024:
                qkv_ref[:, lo - 1024:hi - 1024] = y.astype(BF16)
            else:
                a_ref[:, lo:1024] = y[:, :1024 - lo]
                qkv_ref[:, 0:hi - 1024] = y[:, 1024 - lo:].astype(BF16)

    return _hbm_call(
        body, name="fwd_in_proj", grid=(T // tm,),
        in_specs=[pl.BlockSpec((tm, D_MODEL), lambda i: (i, 0)),
                  pl.BlockSpec((1, D_MODEL), lambda i: (0, 0)),
                  pl.BlockSpec((N_CHIPS, D_MODEL, IN_SHARD), lambda i: (0, 0, 0))],
        out_specs=[pl.BlockSpec((tm, D_MODEL), lambda i: (i, 0)),
                   pl.BlockSpec((tm, 1024), lambda i: (i, 0)),
                   pl.BlockSpec((tm, 1536), lambda i: (i, 0))],
        out_shape=[jax.ShapeDtypeStruct((T, D_MODEL), BF16),
                   jax.ShapeDtypeStruct((T, 1024), F32),
                   jax.ShapeDtypeStruct((T, 1536), BF16)],
        compiler_params=_params(40, 1),
    )(x, g1, w_in)


def _fill_shifted(ext, shifted, tm):
    n = tm + CONV_HALO - SUBLANES
    for j in range(1, SUBLANES):
        shifted[j - 1] = ext[j:j + n, :]


def _shifted_rows(ext, shifted, start, rows):
    j = start % SUBLANES
    if j == 0:
        return ext[start:start + rows, :]
    return shifted[j - 1, start - j:start - j + rows, :]


def _fwd_conv(a, cw, cb, lg, lb):
    T = a.shape[0]
    tm = 512
    rc = 64

    def body(a_ref, w_ref, b_ref, lg_ref, lb_ref, co_ref, hc_ref, hext, hsh):
        i = pl.program_id(0)

        @pl.when(i == 0)
        def _():
            hext[0:CONV_HALO, :] = jnp.zeros((CONV_HALO, CONV_W), F32)

        @pl.when(i > 0)
        def _():
            hext[0:CONV_HALO, :] = hext[tm:tm + CONV_HALO, :]

        hext[CONV_HALO:CONV_HALO + tm, :] = a_ref[:, :CONV_W] * _sigmoid(a_ref[:, CONV_W:])
        _fill_shifted(hext, hsh, tm)
        for c in range(tm // rc):
            acc = jnp.zeros((rc, CONV_W), F32)
            for k in range(CONV_K):
                acc = acc + w_ref[k:k + 1, :] * _shifted_rows(hext, hsh, c * rc + 2 + k, rc)
            hc = acc + b_ref[...]
            hc_ref[c * rc:(c + 1) * rc, :] = hc
            mu = jnp.mean(hc, axis=-1, keepdims=True)
            xc = hc - mu
            var = jnp.mean(xc * xc, axis=-1, keepdims=True)
            z = xc * lax.rsqrt(var + EPS) * lg_ref[...] + lb_ref[...]
            co_ref[c * rc:(c + 1) * rc, :] = (z * _sigmoid(z)).astype(BF16)

    return _hbm_call(
        body, name="fwd_conv", grid=(T // tm,),
        in_specs=[pl.BlockSpec((tm, 1024), lambda i: (i, 0)),
                  pl.BlockSpec((CONV_HALO, CONV_W), lambda i: (0, 0)),
                  pl.BlockSpec((1, CONV_W), lambda i: (0, 0)),
                  pl.BlockSpec((1, CONV_W), lambda i: (0, 0)),
                  pl.BlockSpec((1, CONV_W), lambda i: (0, 0))],
        out_specs=[pl.BlockSpec((tm, CONV_W), lambda i: (i, 0)),
                   pl.BlockSpec((tm, CONV_W), lambda i: (i, 0))],
        out_shape=[jax.ShapeDtypeStruct((T, CONV_W), BF16),
                   jax.ShapeDtypeStruct((T, CONV_W), F32)],
        scratch_shapes=[pltpu.VMEM((tm + CONV_HALO, CONV_W), F32),
                        pltpu.VMEM((SUBLANES - 1, tm + CONV_HALO - SUBLANES, CONV_W), F32)],
        compiler_params=_params(40, 1),
    )(a, cw, cb, lg, lb)


def _row_skew(v, sign):
    rows, width = v.shape
    row = lax.broadcasted_iota(jnp.int32, (rows, 1), 0)
    for b in range(int(math.log2(rows))):
        shift = (1 << b) if sign > 0 else width - (1 << b)
        v = jnp.where(((row >> b) & 1) == 1, pltpu.roll(v, shift, 1), v)
    return v


def _att_visible(d):
    rq = lax.broadcasted_iota(jnp.int32, (ATT_BLK, ATT_BLK), 0) // CHUNK
    ck = lax.broadcasted_iota(jnp.int32, (ATT_BLK, ATT_BLK), 1) // CHUNK
    slack = ATT_BLK
    above = jnp.where(d == 0, 0, slack)
    below = jnp.where(d == 2, 0, slack)
    return (ck <= rq + above) & (ck >= rq - below)


def _bias_tiles(vec):
    def body(v_ref, o_ref):
        d = pl.program_id(0) // N_HEADS
        full = _row_skew(jnp.broadcast_to(v_ref[0], (ATT_BLK, 2 * ATT_BLK)), 1)
        o_ref[0] = jnp.where(_att_visible(d), full[:, :ATT_BLK], NEG_INF)

    return _hbm_call(
        body, name="bias_tiles", grid=(N_ATT_TILES * N_HEADS,),
        in_specs=[pl.BlockSpec((1, 1, 2 * ATT_BLK), lambda n: (n, 0, 0))],
        out_specs=pl.BlockSpec((1, ATT_BLK, ATT_BLK), lambda n: (n, 0, 0)),
        out_shape=jax.ShapeDtypeStruct((N_ATT_TILES * N_HEADS, ATT_BLK, ATT_BLK), F32),
        compiler_params=_params(16, 1),
    )(vec)


def _diag_sums(ds):
    def body(d_ref, o_ref):
        wide = jnp.concatenate([d_ref[0], jnp.zeros((ATT_BLK, ATT_BLK), F32)], axis=1)
        o_ref[0] = jnp.sum(_row_skew(wide, -1), axis=0, keepdims=True)

    return _hbm_call(
        body, name="diag_sums", grid=(N_ATT_TILES * N_HEADS,),
        in_specs=[pl.BlockSpec((1, ATT_BLK, ATT_BLK), lambda n: (n, 0, 0))],
        out_specs=pl.BlockSpec((1, 1, 2 * ATT_BLK), lambda n: (n, 0, 0)),
        out_shape=jax.ShapeDtypeStruct((N_ATT_TILES * N_HEADS, 1, 2 * ATT_BLK), F32),
        compiler_params=_params(16, 1),
    )(ds)


def _head_mask(h):
    lane = lax.broadcasted_iota(jnp.int32, (1, LANES), 1)
    return (lane // HEAD_DIM) == (h % 2)


def _fwd_attn(qkv, bias):
    T = qkv.shape[0]
    nb = T // ATT_BLK
    scale = HEAD_DIM ** -0.5

    def body(q_ref, k0_ref, k1_ref, k2_ref, v0_ref, v1_ref, v2_ref, b_ref, o_ref, lse_ref):
        i = pl.program_id(0)
        k_refs = (k0_ref, k1_ref, k2_ref)
        v_refs = (v0_ref, v1_ref, v2_ref)
        lane = lax.broadcasted_iota(jnp.int32, (1, LANES), 1)
        lse_tile = jnp.zeros((ATT_BLK, LANES), F32)
        for g in range(N_HEADS // 2):
            cols = slice(g * LANES, (g + 1) * LANES)
            qg = q_ref[:, cols]
            og = jnp.zeros((ATT_BLK, LANES), F32)
            for h in (2 * g, 2 * g + 1):
                hm = _head_mask(h)
                qh = jnp.where(hm, qg, jnp.zeros_like(qg))
                s = []
                for d in range(N_ATT_TILES):
                    sd = _dot_nt(qh, k_refs[d][:, cols]) * scale + b_ref[d * N_HEADS + h]
                    if d > 0:
                        sd = jnp.where(i >= d, sd, NEG_INF)
                    s.append(sd)
                m = jnp.maximum(jnp.maximum(jnp.max(s[0], axis=-1, keepdims=True),
                                            jnp.max(s[1], axis=-1, keepdims=True)),
                                jnp.max(s[2], axis=-1, keepdims=True))
                p = [jnp.exp(sd - m) for sd in s]
                l = (jnp.sum(p[0], axis=-1, keepdims=True) + jnp.sum(p[1], axis=-1, keepdims=True)
                     + jnp.sum(p[2], axis=-1, keepdims=True))
                oh = jnp.zeros((ATT_BLK, LANES), F32)
                for d in range(N_ATT_TILES):
                    vg = v_refs[d][:, cols]
                    oh = oh + _dot(p[d].astype(BF16), jnp.where(hm, vg, jnp.zeros_like(vg)))
                og = og + oh / l
                lse_tile = jnp.where(lane == h, m + jnp.log(l), lse_tile)
            o_ref[:, cols] = og.astype(BF16)
        lse_ref[...] = lse_tile

    def kv_spec(d, col):
        return pl.BlockSpec((ATT_BLK, ATTN_W), lambda i: (jnp.maximum(i - d, 0), col))

    return _hbm_call(
        body, name="fwd_attn", grid=(nb,),
        in_specs=[pl.BlockSpec((ATT_BLK, ATTN_W), lambda i: (i, 0)),
                  kv_spec(0, 1), kv_spec(1, 1), kv_spec(2, 1),
                  kv_spec(0, 2), kv_spec(1, 2), kv_spec(2, 2),
                  pl.BlockSpec((N_ATT_TILES * N_HEADS, ATT_BLK, ATT_BLK), lambda i: (0, 0, 0))],
        out_specs=[pl.BlockSpec((ATT_BLK, ATTN_W), lambda i: (i, 0)),
                   pl.BlockSpec((ATT_BLK, LANES), lambda i: (i, 0))],
        out_shape=[jax.ShapeDtypeStruct((T, ATTN_W), BF16),
                   jax.ShapeDtypeStruct((T, LANES), F32)],
        compiler_params=_params(40, 1),
    )(qkv, qkv, qkv, qkv, qkv, qkv, qkv, bias)


def _fwd_out_proj(co, ao, w_out, x, g2, g3):
    T = x.shape[0]
    tm = 512

    def body(co_ref, ao_ref, w_ref, x_ref, g2_ref, g3_ref, mixed_ref, h1_ref, u2_ref):
        mixed = _dot(co_ref[...], w_ref[0:CONV_W, :]) + _dot(ao_ref[...], w_ref[CONV_W:, :])
        mixed_ref[...] = mixed
        y, _ = _rms_fwd(mixed, g2_ref[...])
        h1 = x_ref[...] + y
        h1_ref[...] = h1
        u2, _ = _rms_fwd(h1, g3_ref[...])
        u2_ref[...] = u2.astype(BF16)

    row = lambda w: pl.BlockSpec((tm, w), lambda i: (i, 0))
    vec = pl.BlockSpec((1, D_MODEL), lambda i: (0, 0))
    return _hbm_call(
        body, name="fwd_out_proj", grid=(T // tm,),
        in_specs=[row(CONV_W), row(ATTN_W), pl.BlockSpec((D_MODEL, D_MODEL), lambda i: (0, 0)),
                  row(D_MODEL), vec, vec],
        out_specs=[row(D_MODEL), row(D_MODEL), row(D_MODEL)],
        out_shape=[jax.ShapeDtypeStruct((T, D_MODEL), F32),
                   jax.ShapeDtypeStruct((T, D_MODEL), F32),
                   jax.ShapeDtypeStruct((T, D_MODEL), BF16)],
        compiler_params=_params(40, 1),
    )(co, ao, w_out, x, g2, g3)


GELU_C = math.sqrt(2.0 / math.pi)
GELU_A = 0.044715


def _gelu_and_grad(v):
    th = jnp.tanh(GELU_C * (v + GELU_A * v * v * v))
    gl = 0.5 * v * (1.0 + th)
    dgl = 0.5 * (1.0 + th) + 0.5 * v * (1.0 - th * th) * (GELU_C * (1.0 + 3.0 * GELU_A * v * v))
    return gl, dgl


FF_TM = 256
FF_HALO = 16
FF_CHUNKS = [(lo, min(lo + 256, FF_SHARD)) for lo in range(0, FF_SHARD, 256)]


def _rows_before(prev, cur):
    ext = jnp.concatenate([prev, cur], axis=0)
    return pltpu.roll(ext, 1, 0)[SUBLANES:], pltpu.roll(ext, 2, 0)[SUBLANES:]


def _rows_after(cur, nxt):
    ext = jnp.concatenate([cur, nxt], axis=0)
    n = ext.shape[0]
    return pltpu.roll(ext, n - 1, 0)[:cur.shape[0]], pltpu.roll(ext, n - 2, 0)[:cur.shape[0]]


def _fwd_ffn(u2, w_up, fw, fb, w_down):
    T = u2.shape[0]
    tm = FF_TM

    def body(u_ref, wg_ref, wv_ref, fwg_ref, fwv_ref, fbg_ref, fbv_ref, wd_ref, hf_ref, f_ref, carg, carv):
        i = pl.program_id(1)

        @pl.when(i == 0)
        def _():
            carg[...] = jnp.zeros(carg.shape, F32)
            carv[...] = jnp.zeros(carv.shape, F32)

        u = u_ref[...]
        f = None
        up = lambda lo, hi: (_dot(u, wg_ref[0, :, lo:hi]), _dot(u, wv_ref[0, :, lo:hi]))
        ahead = up(*FF_CHUNKS[0])
        for c, (lo, hi) in enumerate(FF_CHUNKS):
            conv = []
            hs = ahead
            if c + 1 < len(FF_CHUNKS):
                ahead = up(*FF_CHUNKS[c + 1])
            for n, (car, fw_ref, fb_ref) in enumerate(((carg, fwg_ref, fbg_ref), (carv, fwv_ref, fbv_ref))):
                h0 = hs[n]
                hf_ref[n, :, lo:hi] = h0.astype(BF16)
                h1, h2 = _rows_before(car[:, lo:hi], h0)
                car[:, lo:hi] = h0[tm - SUBLANES:, :]
                conv.append(fw_ref[0, 0:1, lo:hi] * h2 + fw_ref[0, 1:2, lo:hi] * h1
                            + fw_ref[0, 2:3, lo:hi] * h0 + fb_ref[0, :, lo:hi])
            gl, _ = _gelu_and_grad(conv[0])
            term = _dot((gl * conv[1]).astype(BF16), wd_ref[lo:hi, :])
            f = term if f is None else f + term
        f_ref[0] = f

    wspec = lambda off: pl.BlockSpec((1, D_MODEL, FF_SHARD), lambda s, i: (s + off, 0, 0))
    fwspec = lambda off: pl.BlockSpec((1, FF_HALO, FF_SHARD), lambda s, i: (s + off, 0, 0))
    fbspec = lambda off: pl.BlockSpec((1, 1, FF_SHARD), lambda s, i: (s + off, 0, 0))
    return _hbm_call(
        body, name="fwd_ffn", grid=(2, T // tm),
        in_specs=[pl.BlockSpec((tm, D_MODEL), lambda s, i: (i, 0)),
                  wspec(0), wspec(2), fwspec(0), fwspec(2), fbspec(0), fbspec(2),
                  pl.BlockSpec((FF_SHARD, D_MODEL), lambda s, i: (s, 0))],
        out_specs=[pl.BlockSpec((2, tm, FF_SHARD), lambda s, i: (0, i, s)),
                   pl.BlockSpec((1, tm, D_MODEL), lambda s, i: (s, i, 0))],
        out_shape=[jax.ShapeDtypeStruct((2, T, D_FF), BF16),
                   jax.ShapeDtypeStruct((2, T, D_MODEL), F32)],
        scratch_shapes=[pltpu.VMEM((SUBLANES, FF_SHARD), F32), pltpu.VMEM((SUBLANES, FF_SHARD), F32)],
        compiler_params=_params(48, 2),
    )(u2, w_up, w_up, fw, fw, fb, fb, w_down)


def _fwd_loss(fp, h1, tgt, g4):
    T = h1.shape[0]
    tm = 512

    def body(fp_ref, h1_ref, t_ref, g_ref, loss_ref, dy_ref, df_ref, dg_ref):
        i = pl.program_id(0)
        f = fp_ref[0] + fp_ref[1]
        r, _ = _rms_fwd(f, g_ref[...])
        e = (h1_ref[...] + r) - t_ref[...]
        dy = e * (1.0 / D_MODEL)
        dy_ref[...] = dy
        df, dg_rows = _rms_bwd(dy, f, g_ref[...])
        df_ref[...] = df.astype(BF16)
        part = 0.5 * jnp.sum(jnp.mean(e * e, axis=-1, keepdims=True), axis=0, keepdims=True)
        dg = jnp.sum(dg_rows, axis=0, keepdims=True)

        @pl.when(i == 0)
        def _():
            loss_ref[...] = part
            dg_ref[...] = dg

        @pl.when(i > 0)
        def _():
            loss_ref[...] += part
            dg_ref[...] += dg

    row = pl.BlockSpec((tm, D_MODEL), lambda i: (i, 0))
    vec = pl.BlockSpec((1, D_MODEL), lambda i: (0, 0))
    return _hbm_call(
        body, name="fwd_loss", grid=(T // tm,),
        in_specs=[pl.BlockSpec((2, tm, D_MODEL), lambda i: (0, i, 0)), row, row, vec],
        out_specs=[pl.BlockSpec((1, 1), lambda i: (0, 0)), row, row, vec],
        out_shape=[jax.ShapeDtypeStruct((1, 1), F32),
                   jax.ShapeDtypeStruct((T, D_MODEL), F32),
                   jax.ShapeDtypeStruct((T, D_MODEL), BF16),
                   jax.ShapeDtypeStruct((1, D_MODEL), F32)],
        compiler_params=_params(40, 1),
    )(fp, h1, tgt, g4)


def _bwd_ffn(df, hf, w_up, fw, fb, w_down):
    T = df.shape[0]
    tm = FF_TM
    ni = T // tm

    def body(df_ref, hf_ref, halo_ref, wd_ref, wg_ref, wv_ref, fwg_ref, fwv_ref, fbg_ref, fbv_ref,
             du_ref, dhf_ref, act_ref, dwg_ref, dwv_ref, carg, carv):
        i = pl.program_id(1)
        ri = ni - 1 - i

        @pl.when(i == 0)
        def _():
            dwg_ref[...] = jnp.zeros(dwg_ref.shape, F32)
            dwv_ref[...] = jnp.zeros(dwv_ref.shape, F32)
            carg[...] = jnp.zeros(carg.shape, F32)
            carv[...] = jnp.zeros(carv.shape, F32)

        df = df_ref[...]
        du = None
        down = lambda lo, hi: _dot_nt(df, wd_ref[lo:hi, :])
        ahead = down(*FF_CHUNKS[0])
        for c, (lo, hi) in enumerate(FF_CHUNKS):
            dact = ahead
            if c + 1 < len(FF_CHUNKS):
                ahead = down(*FF_CHUNKS[c + 1])
            hs, pre = [], []
            for n, (fw_ref, fb_ref) in enumerate(((fwg_ref, fbg_ref), (fwv_ref, fbv_ref))):
                h0 = hf_ref[n, :, lo:hi].astype(F32)
                halo = jnp.where(ri > 0, halo_ref[n, :, lo:hi].astype(F32)[FF_HALO - SUBLANES:], 0.0)
                h1, h2 = _rows_before(halo, h0)
                hs.append((h2, h1, h0))
                pre.append(fw_ref[0, 0:1, lo:hi] * h2 + fw_ref[0, 1:2, lo:hi] * h1
                           + fw_ref[0, 2:3, lo:hi] * h0 + fb_ref[0, :, lo:hi])
            gl, dgl = _gelu_and_grad(pre[0])
            act_ref[:, lo:hi] = (gl * pre[1]).astype(BF16)
            dpre = (dact * pre[1] * dgl, dact * gl)
            for n, (car, fw_ref, dw_ref, w_ref) in enumerate(
                    ((carg, fwg_ref, dwg_ref, wg_ref), (carv, fwv_ref, dwv_ref, wv_ref))):
                dp = dpre[n]
                for k in range(3):
                    dw_ref[0, k:k + 1, lo:hi] += jnp.sum(dp * hs[n][k], axis=0, keepdims=True)
                dw_ref[0, 3:4, lo:hi] += jnp.sum(dp, axis=0, keepdims=True)
                up1, up2 = _rows_after(dp, car[:, lo:hi])
                car[:, lo:hi] = dp[0:SUBLANES, :]
                dh = (fw_ref[0, 2:3, lo:hi] * dp + fw_ref[0, 1:2, lo:hi] * up1
                      + fw_ref[0, 0:1, lo:hi] * up2).astype(BF16)
                dhf_ref[n, :, lo:hi] = dh
                term = _dot_nt(dh, w_ref[0, :, lo:hi])
                du = term if du is None else du + term
        du_ref[0] = du

    rev = lambda i: ni - 1 - i
    wspec = lambda off: pl.BlockSpec((1, D_MODEL, FF_SHARD), lambda s, i: (s + off, 0, 0))
    fwspec = lambda off: pl.BlockSpec((1, FF_HALO, FF_SHARD), lambda s, i: (s + off, 0, 0))
    fbspec = lambda off: pl.BlockSpec((1, 1, FF_SHARD), lambda s, i: (s + off, 0, 0))
    halo_blocks = tm // FF_HALO
    dwspec = pl.BlockSpec((1, FF_HALO, FF_SHARD), lambda s, i: (s, 0, 0))
    return _hbm_call(
        body, name="bwd_ffn", grid=(2, ni),
        in_specs=[pl.BlockSpec((tm, D_MODEL), lambda s, i: (rev(i), 0)),
                  pl.BlockSpec((2, tm, FF_SHARD), lambda s, i: (0, rev(i), s)),
                  pl.BlockSpec((2, FF_HALO, FF_SHARD),
                               lambda s, i: (0, jnp.maximum(rev(i) * halo_blocks - 1, 0), s)),
                  pl.BlockSpec((FF_SHARD, D_MODEL), lambda s, i: (s, 0)),
                  wspec(0), wspec(2), fwspec(0), fwspec(2), fbspec(0), fbspec(2)],
        out_specs=[pl.BlockSpec((1, tm, D_MODEL), lambda s, i: (s, rev(i), 0)),
                   pl.BlockSpec((2, tm, FF_SHARD), lambda s, i: (0, rev(i), s)),
                   pl.BlockSpec((tm, FF_SHARD), lambda s, i: (rev(i), s)),
                   dwspec, dwspec],
        out_shape=[jax.ShapeDtypeStruct((2, T, D_MODEL), F32),
                   jax.ShapeDtypeStruct((2, T, D_FF), BF16),
                   jax.ShapeDtypeStruct((T, D_FF), BF16),
                   jax.ShapeDtypeStruct((2, FF_HALO, FF_SHARD), F32),
                   jax.ShapeDtypeStruct((2, FF_HALO, FF_SHARD), F32)],
        scratch_shapes=[pltpu.VMEM((SUBLANES, FF_SHARD), F32), pltpu.VMEM((SUBLANES, FF_SHARD), F32)],
        compiler_params=_params(56, 2),
    )(df, hf, hf, w_down, w_up, w_up, fw, fw, fb, fb)


def _bwd_mid(du2p, dy, h1, mixed, g3, g2, w_out):
    T = dy.shape[0]
    tm = 512

    def body(du_ref, dy_ref, h1_ref, mx_ref, g3_ref, g2_ref, w_ref,
             dh1_ref, dmx_ref, dco_ref, dao_ref, dg3_ref, dg2_ref):
        i = pl.program_id(0)
        dres, dg3_rows = _rms_bwd(du_ref[0] + du_ref[1], h1_ref[...], g3_ref[...])
        dh1 = dy_ref[...] + dres
        dh1_ref[...] = dh1
        dmx, dg2_rows = _rms_bwd(dh1, mx_ref[...], g2_ref[...])
        dmx = dmx.astype(BF16)
        dmx_ref[...] = dmx
        dcat = _dot_nt(dmx, w_ref[...])
        dco_ref[...] = dcat[:, :CONV_W]
        dao_ref[...] = dcat[:, CONV_W:].astype(BF16)
        dg3 = jnp.sum(dg3_rows, axis=0, keepdims=True)
        dg2 = jnp.sum(dg2_rows, axis=0, keepdims=True)

        @pl.when(i == 0)
        def _():
            dg3_ref[...] = dg3
            dg2_ref[...] = dg2

        @pl.when(i > 0)
        def _():
            dg3_ref[...] += dg3
            dg2_ref[...] += dg2

    row = lambda w: pl.BlockSpec((tm, w), lambda i: (i, 0))
    vec = pl.BlockSpec((1, D_MODEL), lambda i: (0, 0))
    return _hbm_call(
        body, name="bwd_mid", grid=(T // tm,),
        in_specs=[pl.BlockSpec((2, tm, D_MODEL), lambda i: (0, i, 0)), row(D_MODEL), row(D_MODEL),
                  row(D_MODEL), vec, vec, pl.BlockSpec((D_MODEL, D_MODEL), lambda i: (0, 0))],
        out_specs=[row(D_MODEL), row(D_MODEL), row(CONV_W), row(ATTN_W), vec, vec],
        out_shape=[jax.ShapeDtypeStruct((T, D_MODEL), F32),
                   jax.ShapeDtypeStruct((T, D_MODEL), BF16),
                   jax.ShapeDtypeStruct((T, CONV_W), F32),
                   jax.ShapeDtypeStruct((T, ATTN_W), BF16),
                   jax.ShapeDtypeStruct((1, D_MODEL), F32),
                   jax.ShapeDtypeStruct((1, D_MODEL), F32)],
        compiler_params=_params(48, 1),
    )(du2p, dy, h1, mixed, g3, g2, w_out)


def _bwd_attn(qkv, ao, dao, lse, bias):
    T = qkv.shape[0]
    nb = T // ATT_BLK
    scale = HEAD_DIM ** -0.5

    def body(k_ref, v_ref, q0, q1, q2, do0, do1, do2, o0, o1, o2, l0, l1, l2, b_ref,
             dp_ref, ds_ref, acc1, acc2):
        j = pl.program_id(0)
        q_refs, do_refs, o_refs, l_refs = (q0, q1, q2), (do0, do1, do2), (o0, o1, o2), (l0, l1, l2)

        @pl.when(j == 0)
        def _():
            ds_ref[...] = jnp.zeros(ds_ref.shape, F32)
            acc1[...] = jnp.zeros(acc1.shape, F32)
            acc2[...] = jnp.zeros(acc2.shape, F32)

        dq_new = [[], [], []]
        dk_cols, dv_cols = [], []
        for g in range(N_HEADS // 2):
            cols = slice(g * LANES, (g + 1) * LANES)
            kg = k_ref[:, cols]
            vg = v_ref[:, cols]
            dkg = jnp.zeros((ATT_BLK, LANES), F32)
            dvg = jnp.zeros((ATT_BLK, LANES), F32)
            dqg = [jnp.zeros((ATT_BLK, LANES), F32) for _ in range(N_ATT_TILES)]
            for d in range(N_ATT_TILES):
                qg = q_refs[d][:, cols]
                dog = do_refs[d][:, cols]
                prod = dog.astype(F32) * o_refs[d][:, cols].astype(F32)
                for h in (2 * g, 2 * g + 1):
                    hm = _head_mask(h)
                    qh = jnp.where(hm, qg, jnp.zeros_like(qg))
                    doh = jnp.where(hm, dog, jnp.zeros_like(dog))
                    kh = jnp.where(hm, kg, jnp.zeros_like(kg))
                    delta = jnp.sum(jnp.where(hm, prod, 0.0), axis=-1, keepdims=True)
                    s = _dot_nt(qh, kg) * scale + b_ref[d * N_HEADS + h]
                    p = jnp.exp(s - l_refs[d][:, h:h + 1])
                    p = jnp.where(j + d < nb, p, 0.0)
                    dvg = dvg + _dot_tn(p.astype(BF16), doh)
                    dpm = _dot_nt(doh, vg)
                    dsc = p * (dpm - delta)
                    ds_ref[d * N_HEADS + h] += dsc
                    dsb = (dsc * scale).astype(BF16)
                    dqg[d] = dqg[d] + _dot(dsb, kh)
                    dkg = dkg + _dot_tn(dsb, qh)
            for d in range(N_ATT_TILES):
                dq_new[d].append(dqg[d])
            dk_cols.append(dkg)
            dv_cols.append(dvg)
        x0, x1, x2 = (jnp.concatenate(c, axis=1) for c in dq_new)
        dp_ref[:, 0:1024] = jnp.zeros((ATT_BLK, 1024), BF16)
        dp_ref[:, 1024:1536] = (acc1[...] + x0).astype(BF16)
        dp_ref[:, 1536:2048] = jnp.concatenate(dk_cols, axis=1).astype(BF16)
        dp_ref[:, 2048:2560] = jnp.concatenate(dv_cols, axis=1).astype(BF16)
        acc1[...] = acc2[...] + x1
        acc2[...] = x2

    def fwd_spec(d, width, col):
        return pl.BlockSpec((ATT_BLK, width), lambda j: (jnp.minimum(j + d, nb - 1), col))

    return _hbm_call(
        body, name="bwd_attn", grid=(nb,),
        in_specs=[pl.BlockSpec((ATT_BLK, ATTN_W), lambda j: (j, 1)),
                  pl.BlockSpec((ATT_BLK, ATTN_W), lambda j: (j, 2)),
                  fwd_spec(0, ATTN_W, 0), fwd_spec(1, ATTN_W, 0), fwd_spec(2, ATTN_W, 0),
                  fwd_spec(0, ATTN_W, 0), fwd_spec(1, ATTN_W, 0), fwd_spec(2, ATTN_W, 0),
                  fwd_spec(0, ATTN_W, 0), fwd_spec(1, ATTN_W, 0), fwd_spec(2, ATTN_W, 0),
                  fwd_spec(0, LANES, 0), fwd_spec(1, LANES, 0), fwd_spec(2, LANES, 0),
                  pl.BlockSpec((N_ATT_TILES * N_HEADS, ATT_BLK, ATT_BLK), lambda j: (0, 0, 0))],
        out_specs=[pl.BlockSpec((ATT_BLK, IN_COLS), lambda j: (j, 0)),
                   pl.BlockSpec((N_ATT_TILES * N_HEADS, ATT_BLK, ATT_BLK), lambda j: (0, 0, 0))],
        out_shape=[jax.ShapeDtypeStruct((T, IN_COLS), BF16),
                   jax.ShapeDtypeStruct((N_ATT_TILES * N_HEADS, ATT_BLK, ATT_BLK), F32)],
        scratch_shapes=[pltpu.VMEM((ATT_BLK, ATTN_W), F32), pltpu.VMEM((ATT_BLK, ATTN_W), F32)],
        compiler_params=_params(56, 1),
    )(qkv, qkv, qkv, qkv, qkv, dao, dao, dao, ao, ao, ao, lse, lse, lse, bias)


def _bwd_conv(dproj, a, dco, hc, cw, lg, lb):
    T = a.shape[0]
    tm = 512
    rc = 32
    ni = T // tm
    hb = tm // CONV_HALO

    def body(dp_in, a_ref, ap_ref, dco_ref, dcon_ref, hc_ref, hcn_ref, w_ref, lg_ref, lb_ref,
             dp_ref, dw_ref, db_ref, dlg_ref, dlb_ref, hext, dext, hsh, dsh, dwacc):
        del dp_in
        i = pl.program_id(0)

        def ln_bwd(dco_v, hc_v):
            mu = jnp.mean(hc_v, axis=-1, keepdims=True)
            xc = hc_v - mu
            rstd = lax.rsqrt(jnp.mean(xc * xc, axis=-1, keepdims=True) + EPS)
            xh = xc * rstd
            z = xh * lg_ref[...] + lb_ref[...]
            sg = _sigmoid(z)
            dz = dco_v * (sg * (1.0 + z * (1.0 - sg)))
            dxh = dz * lg_ref[...]
            dhc = rstd * (dxh - jnp.mean(dxh, axis=-1, keepdims=True)
                          - xh * jnp.mean(dxh * xh, axis=-1, keepdims=True))
            return dhc, dz * xh, dz

        hext[0:CONV_HALO, :] = jnp.where(i > 0, ap_ref[:, :CONV_W] * _sigmoid(ap_ref[:, CONV_W:]), 0.0)
        hext[CONV_HALO:CONV_HALO + tm, :] = a_ref[:, :CONV_W] * _sigmoid(a_ref[:, CONV_W:])
        dhc, dlg_rows, dlb_rows = ln_bwd(dco_ref[...], hc_ref[...])
        dext[0:tm, :] = dhc
        dhc_next, _, _ = ln_bwd(dcon_ref[...], hcn_ref[...])
        dext[tm:tm + CONV_HALO, :] = jnp.where(i < ni - 1, dhc_next, 0.0)

        @pl.when(i == 0)
        def _():
            dw_ref[...] = jnp.zeros(dw_ref.shape, F32)
            db_ref[...] = jnp.zeros(db_ref.shape, F32)
            dlg_ref[...] = jnp.zeros(dlg_ref.shape, F32)
            dlb_ref[...] = jnp.zeros(dlb_ref.shape, F32)

            dwacc[...] = jnp.zeros(dwacc.shape, F32)

        db_ref[...] += jnp.sum(dhc, axis=0, keepdims=True)
        dlg_ref[...] += jnp.sum(dlg_rows, axis=0, keepdims=True)
        dlb_ref[...] += jnp.sum(dlb_rows, axis=0, keepdims=True)
        _fill_shifted(hext, hsh, tm)
        _fill_shifted(dext, dsh, tm)
        for c in range(tm // rc):
            r0 = c * rc
            dh = jnp.zeros((rc, CONV_W), F32)
            dhc_c = dext[r0:r0 + rc, :]
            for k in range(CONV_K):
                dh = dh + w_ref[k:k + 1, :] * _shifted_rows(dext, dsh, r0 + 30 - k, rc)
                prod = dhc_c * _shifted_rows(hext, hsh, r0 + 2 + k, rc)
                dwacc[k] += jnp.sum(prod.reshape(rc // SUBLANES, SUBLANES, CONV_W), axis=0)
            av = a_ref[r0:r0 + rc, :CONV_W]
            sg = _sigmoid(a_ref[r0:r0 + rc, CONV_W:])
            dp_ref[r0:r0 + rc, 0:CONV_W] = (dh * sg).astype(BF16)
            dp_ref[r0:r0 + rc, CONV_W:] = (dh * av * sg * (1.0 - sg)).astype(BF16)

        @pl.when(i == ni - 1)
        def _():
            dw_ref[...] = jnp.sum(dwacc[...], axis=1)

    row = lambda w: pl.BlockSpec((tm, w), lambda i: (i, 0))
    prev = lambda w: pl.BlockSpec((CONV_HALO, w), lambda i: (jnp.maximum(i * hb - 1, 0), 0))
    nxt = lambda w: pl.BlockSpec((CONV_HALO, w), lambda i: (jnp.minimum((i + 1) * hb, ni * hb - 1), 0))
    vec = pl.BlockSpec((1, CONV_W), lambda i: (0, 0))
    return _hbm_call(
        body, name="bwd_conv", grid=(ni,),
        in_specs=[ANY, row(1024), prev(1024), row(CONV_W), nxt(CONV_W), row(CONV_W), nxt(CONV_W),
                  pl.BlockSpec((CONV_HALO, CONV_W), lambda i: (0, 0)), vec, vec],
        out_specs=[pl.BlockSpec((tm, 1024), lambda i: (i, 0)),
                   pl.BlockSpec((CONV_HALO, CONV_W), lambda i: (0, 0)), vec, vec, vec],
        out_shape=[jax.ShapeDtypeStruct((T, IN_COLS), BF16),
                   jax.ShapeDtypeStruct((CONV_HALO, CONV_W), F32),
                   jax.ShapeDtypeStruct((1, CONV_W), F32),
                   jax.ShapeDtypeStruct((1, CONV_W), F32),
                   jax.ShapeDtypeStruct((1, CONV_W), F32)],
        scratch_shapes=[pltpu.VMEM((tm + CONV_HALO, CONV_W), F32), pltpu.VMEM((tm + CONV_HALO, CONV_W), F32),
                        pltpu.VMEM((SUBLANES - 1, tm + CONV_HALO - SUBLANES, CONV_W), F32),
                        pltpu.VMEM((SUBLANES - 1, tm + CONV_HALO - SUBLANES, CONV_W), F32),
                        pltpu.VMEM((CONV_HALO, SUBLANES, CONV_W), F32)],
        input_output_aliases={0: 0},
        compiler_params=_params(56, 1),
    )(dproj, a, a, dco, dco, hc, hc, cw, lg, lb)


def _bwd_in_proj(dproj, w_in, x, dh1, g1):
    T = x.shape[0]
    tm = 512

    def body(dp_ref, w_ref, x_ref, dh_ref, g_ref, gx_ref, dg_ref):
        i = pl.program_id(0)
        du = None
        for s in range(N_CHIPS):
            term = _dot_nt(dp_ref[:, IN_SHARD * s:IN_SHARD * (s + 1)], w_ref[s])
            du = term if du is None else du + term
        dx, dg_rows = _rms_bwd(du, x_ref[...], g_ref[...])
        gx_ref[...] = dh_ref[...] + dx
        dg = jnp.sum(dg_rows, axis=0, keepdims=True)

        @pl.when(i == 0)
        def _():
            dg_ref[...] = dg

        @pl.when(i > 0)
        def _():
            dg_ref[...] += dg

    row = lambda w: pl.BlockSpec((tm, w), lambda i: (i, 0))
    vec = pl.BlockSpec((1, D_MODEL), lambda i: (0, 0))
    return _hbm_call(
        body, name="bwd_in_proj", grid=(T // tm,),
        in_specs=[row(IN_COLS), pl.BlockSpec((N_CHIPS, D_MODEL, IN_SHARD), lambda i: (0, 0, 0)),
                  row(D_MODEL), row(D_MODEL), vec],
        out_specs=[row(D_MODEL), vec],
        out_shape=[jax.ShapeDtypeStruct((T, D_MODEL), F32), jax.ShapeDtypeStruct((1, D_MODEL), F32)],
        compiler_params=_params(40, 1),
    )(dproj, w_in, x, dh1, g1)


def _wgrad(name, a_list, a_spec, b, b_spec, out_block, out_spec, out_shape, n_outer, T, tk=512, select=None):
    def body(*refs):
        a_refs, b_ref, o_ref = refs[:len(a_list)], refs[len(a_list)], refs[len(a_list) + 1]
        kt = pl.program_id(1)

        @pl.when(kt == 0)
        def _():
            o_ref[...] = jnp.zeros(o_ref.shape, F32)

        bv = b_ref[...].reshape(b_ref.shape[-2:])
        if select is None:
            o_ref[...] += _dot_tn(a_refs[0][...].reshape(a_refs[0].shape[-2:]), bv).reshape(o_ref.shape)
        else:
            for n, a_ref in enumerate(a_refs):
                @pl.when(select(pl.program_id(0)) == n)
                def _():
                    o_ref[...] += _dot_tn(a_ref[...], bv).reshape(o_ref.shape)

    del out_block
    return _hbm_call(
        body, name=name, grid=(n_outer, T // tk),
        in_specs=[a_spec] * len(a_list) + [b_spec],
        out_specs=out_spec, out_shape=out_shape,
        compiler_params=_params(48, 2),
    )(*a_list, b)


def _mesh_pos():
    return lax.axis_index("x"), lax.axis_index("y"), lax.axis_index("c")


def _other_chips(x, y):
    return [((1 - x, y), 2 * (1 - x) + y), ((x, 1 - y), 2 * x + (1 - y)), ((1 - x, 1 - y), 2 * (1 - x) + (1 - y))]


def _all_gather_weights(shards):
    n = len(shards)

    def body(*refs):
        out_refs = refs[n:2 * n]
        send_sems, recv_sems = refs[2 * n:]
        x, y, c = _mesh_pos()
        me = 2 * x + y
        sibling = (x, y, 1 - c)
        chips = _other_chips(x, y)
        first, passed = [], []
        for t in range(n):
            half = shards[t].shape[1] // 2
            mine = out_refs[t].at[me, pl.ds(c * half, half)]
            for k, (chip, _) in enumerate(chips):
                cp = pltpu.make_async_remote_copy(
                    src_ref=mine, dst_ref=mine, send_sem=send_sems.at[t, k], recv_sem=recv_sems.at[t, k],
                    device_id=(*chip, c), device_id_type=MESH)
                cp.start()
                first.append(cp)
        for t in range(n):
            half = shards[t].shape[1] // 2
            rows = pl.ds(c * half, half)
            for k, (chip, s) in enumerate(chips):
                landed = out_refs[t].at[s, rows]
                pltpu.make_async_remote_copy(
                    src_ref=landed, dst_ref=landed, send_sem=send_sems.at[t, k], recv_sem=recv_sems.at[t, k],
                    device_id=(*chip, c), device_id_type=MESH).wait_recv()
                cp = pltpu.make_async_remote_copy(
                    src_ref=landed, dst_ref=landed, send_sem=send_sems.at[t, 3 + k], recv_sem=recv_sems.at[t, 3 + k],
                    device_id=sibling, device_id_type=MESH)
                cp.start()
                passed.append(cp)
        for t in range(n):
            half = shards[t].shape[1] // 2
            other = pl.ds((1 - c) * half, half)
            for k, (chip, s) in enumerate(chips):
                got = out_refs[t].at[s, other]
                pltpu.make_async_remote_copy(
                    src_ref=got, dst_ref=got, send_sem=send_sems.at[t, 3 + k], recv_sem=recv_sems.at[t, 3 + k],
                    device_id=sibling, device_id_type=MESH).wait_recv()
        for cp in first + passed:
            cp.wait_send()

    return _hbm_call(
        body, name="all_gather_weights",
        in_specs=[ANY] * n, out_specs=[ANY] * n,
        out_shape=[jax.ShapeDtypeStruct(s.shape, s.dtype) for s in shards],
        scratch_shapes=[pltpu.SemaphoreType.DMA((n, 6)), pltpu.SemaphoreType.DMA((n, 6))],
        input_output_aliases={t: t for t in range(n)},
    )(*shards)


def _pair_exchange(grads):
    n = len(grads)

    def body(*refs):
        g_refs, got_refs = refs[:n], refs[n:2 * n]
        send_sems, recv_sems = refs[2 * n:]
        x, y, c = _mesh_pos()
        copies = []
        for t in range(n):
            half = grads[t].shape[1] // 2
            cp = pltpu.make_async_remote_copy(
                src_ref=g_refs[t].at[:, pl.ds((1 - c) * half, half), :], dst_ref=got_refs[t],
                send_sem=send_sems.at[t], recv_sem=recv_sems.at[t],
                device_id=(x, y, 1 - c), device_id_type=MESH)
            cp.start()
            copies.append(cp)
        for cp in copies:
            cp.wait()

    return _hbm_call(
        body, name="pair_exchange",
        in_specs=[ANY] * n, out_specs=[ANY] * n,
        out_shape=[jax.ShapeDtypeStruct((N_CHIPS, g.shape[1] // 2, g.shape[2]), F32) for g in grads],
        scratch_shapes=[pltpu.SemaphoreType.DMA((n,)), pltpu.SemaphoreType.DMA((n,))],
    )(*grads)


def _chip_exchange(pairs):
    n = len(pairs)

    def body(*refs):
        p_refs, got_refs = refs[:n], refs[n:2 * n]
        send_sems, recv_sems = refs[2 * n:]
        x, y, c = _mesh_pos()
        copies = []
        for t in range(n):
            for k, (chip, s) in enumerate(_other_chips(x, y)):
                cp = pltpu.make_async_remote_copy(
                    src_ref=p_refs[t].at[s], dst_ref=got_refs[t].at[k],
                    send_sem=send_sems.at[t, k], recv_sem=recv_sems.at[t, k],
                    device_id=(*chip, c), device_id_type=MESH)
                cp.start()
                copies.append(cp)
        for cp in copies:
            cp.wait()

    return _hbm_call(
        body, name="chip_exchange",
        in_specs=[ANY] * n, out_specs=[ANY] * n,
        out_shape=[jax.ShapeDtypeStruct((3,) + p.shape[1:], p.dtype) for p in pairs],
        scratch_shapes=[pltpu.SemaphoreType.DMA((n, 3)), pltpu.SemaphoreType.DMA((n, 3))],
    )(*pairs)


def _pair_gather(halves):
    n = len(halves)

    def body(*refs):
        out_refs = refs[n:2 * n]
        send_sems, recv_sems = refs[2 * n:]
        x, y, c = _mesh_pos()
        copies = []
        for t in range(n):
            half = halves[t].shape[0] // 2
            mine = out_refs[t].at[pl.ds(c * half, half)]
            theirs = out_refs[t].at[pl.ds((1 - c) * half, half)]
            send = pltpu.make_async_remote_copy(
                src_ref=mine, dst_ref=mine, send_sem=send_sems.at[t], recv_sem=recv_sems.at[t],
                device_id=(x, y, 1 - c), device_id_type=MESH)
            send.start()
            recv = pltpu.make_async_remote_copy(
                src_ref=theirs, dst_ref=theirs, send_sem=send_sems.at[t], recv_sem=recv_sems.at[t],
                device_id=(x, y, 1 - c), device_id_type=MESH)
            copies.append((send, recv))
        for send, recv in copies:
            send.wait_send()
            recv.wait_recv()

    return _hbm_call(
        body, name="pair_gather",
        in_specs=[ANY] * n, out_specs=[ANY] * n,
        out_shape=[jax.ShapeDtypeStruct(h.shape, F32) for h in halves],
        scratch_shapes=[pltpu.SemaphoreType.DMA((n,)), pltpu.SemaphoreType.DMA((n,))],
        input_output_aliases={t: t for t in range(n)},
    )(*halves)


def _all_reduce_small(pack):
    rows = pack.shape[0]

    def body(p_ref, o_ref, buf, send_sems, recv_sems):
        x, y, c = _mesh_pos()
        me = 4 * x + 2 * y + c
        buf[0] = p_ref[...]
        copies = []
        for k in range(1, 8):
            peer = (x ^ (k >> 2), y ^ ((k >> 1) & 1), c ^ (k & 1))
            cp = pltpu.make_async_remote_copy(
                src_ref=p_ref, dst_ref=buf.at[k], send_sem=send_sems.at[k - 1], recv_sem=recv_sems.at[k - 1],
                device_id=peer, device_id_type=MESH)
            cp.start()
            copies.append(cp)
        for cp in copies:
            cp.wait()
        total = buf[me]
        for dev in range(1, 8):
            total = total + buf[me ^ dev]
        o_ref[...] = total

    return pl.pallas_call(
        body, name="all_reduce_small",
        in_specs=[VMEM_FULL], out_specs=VMEM_FULL,
        out_shape=jax.ShapeDtypeStruct(pack.shape, F32),
        scratch_shapes=[pltpu.VMEM((8, rows, LANES), F32),
                        pltpu.SemaphoreType.DMA((7,)), pltpu.SemaphoreType.DMA((7,))],
    )(pack)


def _row_block(rows):
    if rows <= 512:
        return rows
    for rb in (256, 352):
        if rows % rb == 0:
            return rb
    raise ValueError(f"no row block for {rows} rows")


def _place(name, w, pos, dtype):
    R, C = w.shape
    rb = _row_block(R)

    def body(pos_ref, w_ref, o_ref):
        del pos_ref
        o_ref[0] = w_ref[...].astype(dtype)

    return _hbm_call(
        body, name=name,
        grid_spec=pltpu.PrefetchScalarGridSpec(
            num_scalar_prefetch=1, grid=(R // rb,),
            in_specs=[pl.BlockSpec((rb, C), lambda r, p: (r, 0))],
            out_specs=pl.BlockSpec((1, rb, C), lambda r, p: (p[0], r, 0))),
        out_shape=jax.ShapeDtypeStruct((N_CHIPS, R, C), dtype),
        compiler_params=_params(32, 1),
    )(pos, w)


def _pair_sum(name, g, got, pos):
    S, R, C = g.shape
    half = R // 2
    rb = _row_block(half)
    nh = half // rb

    def body(pos_ref, a_ref, b_ref, o_ref):
        del pos_ref
        o_ref[...] = (a_ref[...] + b_ref[...]).astype(BF16)

    spec = pl.BlockSpec((1, rb, C), lambda s, r, p: (s, r, 0))
    return _hbm_call(
        body, name=name,
        grid_spec=pltpu.PrefetchScalarGridSpec(
            num_scalar_prefetch=1, grid=(S, nh),
            in_specs=[pl.BlockSpec((1, rb, C), lambda s, r, p: (s, p[1] * nh + r, 0)), spec],
            out_specs=spec),
        out_shape=jax.ShapeDtypeStruct((S, half, C), BF16), compiler_params=_params(32, 2),
    )(pos, g, got)


def _chip_sum(name, pairs, got, pos):
    _, half, C = pairs.shape
    rb = _row_block(half)
    nh = half // rb

    def body(pos_ref, a_ref, g_ref, o_ref):
        del pos_ref
        o_ref[...] = ((a_ref[0].astype(F32) + g_ref[0].astype(F32)) + g_ref[1].astype(F32)) + g_ref[2].astype(F32)

    return _hbm_call(
        body, name=name,
        grid_spec=pltpu.PrefetchScalarGridSpec(
            num_scalar_prefetch=1, grid=(nh,),
            in_specs=[pl.BlockSpec((1, rb, C), lambda r, p: (p[0], r, 0)),
                      pl.BlockSpec((3, rb, C), lambda r, p: (0, r, 0))],
            out_specs=pl.BlockSpec((rb, C), lambda r, p: (p[1] * nh + r, 0))),
        out_shape=jax.ShapeDtypeStruct((2 * half, C), F32), compiler_params=_params(32, 1),
    )(pos, pairs, got)


def _adamw(name, w, g, m, v):
    R, C = w.shape
    rb = _row_block(R)
    c1 = 1.0 - ADAM_B1 ** ADAM_STEP
    c2 = 1.0 - ADAM_B2 ** ADAM_STEP

    def body(w_ref, g_ref, m_ref, v_ref, go_ref, d_ref, nm_ref, nv_ref):
        gv = g_ref[...]
        go_ref[...] = gv
        nm = ADAM_B1 * m_ref[...] + (1.0 - ADAM_B1) * gv
        nv = ADAM_B2 * v_ref[...] + (1.0 - ADAM_B2) * (gv * gv)
        nm_ref[...] = nm
        nv_ref[...] = nv
        d_ref[...] = -ADAM_LR * ((nm / c1) / (jnp.sqrt(nv / c2) + ADAM_EPS) + ADAM_WD * w_ref[...])

    spec = pl.BlockSpec((rb, C), lambda r: (r, 0))
    sds = jax.ShapeDtypeStruct(w.shape, F32)
    return _hbm_call(
        body, name=name, grid=(R // rb,), in_specs=[spec] * 4, out_specs=[spec] * 4,
        out_shape=[sds, sds, sds, sds], compiler_params=_params(48, 1),
    )(w, g, m, v)


def _rel_index():
    m = np.arange(2 * ATT_BLK)
    off = np.where(m < ATT_BLK, m, m - 2 * ATT_BLK)
    rel = np.stack([ATT_BLK * d - off for d in range(N_ATT_TILES)])
    return np.clip(rel, -MAX_REL, MAX_REL) + MAX_REL


def _local_step(x, tgt, g1, w_in, cw, cb, lg, lb, rel, w_out, g2, g3, w_up, fw, fb, w_down, g4):
    T = x.shape[0]
    idx = _rel_index()
    vec = jnp.transpose(rel[:, idx], (1, 0, 2)).reshape(N_ATT_TILES * N_HEADS, 1, 2 * ATT_BLK)
    bias = _bias_tiles(vec)

    u, a, qkv = _fwd_in_proj(x, g1, w_in)
    co, hc = _fwd_conv(a, cw, cb, lg, lb)
    ao, lse = _fwd_attn(qkv, bias)
    mixed, h1, u2 = _fwd_out_proj(co, ao, w_out, x, g2, g3)
    hf, fp = _fwd_ffn(u2, w_up, fw, fb, w_down)
    loss, dy, df, dg4 = _fwd_loss(fp, h1, tgt, g4)

    du2p, dhf, act, dfw_g, dfw_v = _bwd_ffn(df, hf, w_up, fw, fb, w_down)
    dh1, dmx, dco, dao, dg3, dg2 = _bwd_mid(du2p, dy, h1, mixed, g3, g2, w_out)
    dproj, dsacc = _bwd_attn(qkv, ao, dao, lse, bias)
    dproj, dcw, dcb, dlg, dlb = _bwd_conv(dproj, a, dco, hc, cw, lg, lb)
    gx, dg1 = _bwd_in_proj(dproj, w_in, x, dh1, g1)

    tk = 512
    gw_in = _wgrad(
        "wgrad_in", [u], pl.BlockSpec((tk, D_MODEL), lambda s, k: (k, 0)),
        dproj, pl.BlockSpec((tk, IN_SHARD), lambda s, k: (k, s)), None,
        pl.BlockSpec((1, D_MODEL, IN_SHARD), lambda s, k: (s, 0, 0)),
        jax.ShapeDtypeStruct((N_CHIPS, D_MODEL, IN_SHARD), F32), N_CHIPS, T, tk)
    gw_up = _wgrad(
        "wgrad_up", [u2], pl.BlockSpec((tk, D_MODEL), lambda s, k: (k, 0)),
        dhf, pl.BlockSpec((1, tk, FF_SHARD), lambda s, k: (s // 2, k, s % 2)), None,
        pl.BlockSpec((1, D_MODEL, FF_SHARD), lambda s, k: (s, 0, 0)),
        jax.ShapeDtypeStruct((N_CHIPS, D_MODEL, FF_SHARD), F32), N_CHIPS, T, tk)
    gw_down = _wgrad(
        "wgrad_down", [act], pl.BlockSpec((tk, FF_SHARD), lambda s, k: (k, s)),
        df, pl.BlockSpec((tk, D_MODEL), lambda s, k: (k, 0)), None,
        pl.BlockSpec((FF_SHARD, D_MODEL), lambda s, k: (s, 0)),
        jax.ShapeDtypeStruct((D_FF, D_MODEL), F32), 2, T, tk)
    gw_out = _wgrad(
        "wgrad_out", [co, ao], pl.BlockSpec((tk, CONV_W), lambda s, k: (k, 0)),
        dmx, pl.BlockSpec((tk, D_MODEL), lambda s, k: (k, 0)), None,
        pl.BlockSpec((CONV_W, D_MODEL), lambda s, k: (s, 0)),
        jax.ShapeDtypeStruct((D_MODEL, D_MODEL), F32), 2, T, tk, select=lambda s: s)

    diag = _diag_sums(dsacc).reshape(N_ATT_TILES, N_HEADS, 2 * ATT_BLK)
    onehot = np.zeros((N_ATT_TILES, 2 * ATT_BLK, 2 * MAX_REL + 1), np.float32)
    for d in range(N_ATT_TILES):
        onehot[d, np.arange(2 * ATT_BLK), idx[d]] = 1.0
    drel = jnp.einsum("dhm,dmr->hr", diag, jnp.asarray(onehot), precision=lax.Precision.HIGHEST)

    small = dict(norm_mix_pre=dg1, conv_dw_w=dcw[:CONV_K], conv_dw_b=dcb, conv_ln_g=dlg, conv_ln_b=dlb,
                 rel_bias=drel, norm_mix_post=dg2, norm_ffn_pre=dg3,
                 ffn_dw_w=jnp.concatenate([dfw_g[0, :3], dfw_g[1, :3], dfw_v[0, :3], dfw_v[1, :3]], axis=1),
                 ffn_dw_b=jnp.concatenate([dfw_g[0, 3:4], dfw_g[1, 3:4], dfw_v[0, 3:4], dfw_v[1, 3:4]], axis=1),
                 norm_ffn_post=dg4)
    big = dict(w_in=gw_in, w_out=gw_out.reshape(N_CHIPS, D_MODEL // N_CHIPS, D_MODEL), w_up=gw_up,
               w_down=gw_down.reshape(N_CHIPS, D_FF // N_CHIPS, D_MODEL))
    return loss, gx, small, big


SMALL_ORDER = ["norm_mix_pre", "conv_dw_b", "conv_ln_g", "conv_ln_b", "rel_bias", "norm_mix_post",
               "norm_ffn_pre", "ffn_dw_b", "norm_ffn_post", "conv_dw_w", "ffn_dw_w"]


def _pack(parts):
    rows = []
    for p in parts:
        width = -(-p.shape[1] // LANES) * LANES
        rows.append(jnp.pad(p, ((0, 0), (0, width - p.shape[1]))).reshape(-1, LANES))
    packed = jnp.concatenate(rows, axis=0)
    pad = -packed.shape[0] % 8
    return jnp.pad(packed, ((0, pad), (0, 0)))


def _unpack(packed, shapes):
    out, r = [], 0
    for shp in shapes:
        width = -(-shp[1] // LANES) * LANES
        n = shp[0] * width // LANES
        out.append(packed[r:r + n].reshape(shp[0], width)[:, :shp[1]])
        r += n
    return out


WEIGHTS = ["norm_mix_pre", "w_in", "conv_dw_w", "conv_dw_b", "conv_ln_g", "conv_ln_b", "rel_bias", "w_out",
           "norm_mix_post", "norm_ffn_pre", "w_up", "ffn_dw_w", "ffn_dw_b", "w_down", "norm_ffn_post"]
BIG = ["w_in", "w_out", "w_up", "w_down"]


def kernel(x, norm_mix_pre, w_in, conv_dw_w, conv_dw_b, conv_ln_g, conv_ln_b, rel_bias, w_out, norm_mix_post, norm_ffn_pre, w_up, ffn_dw_w, ffn_dw_b, w_down, norm_ffn_post, loss_target, m_norm_mix_pre, m_w_in, m_conv_dw_w, m_conv_dw_b, m_conv_ln_g, m_conv_ln_b, m_rel_bias, m_w_out, m_norm_mix_post, m_norm_ffn_pre, m_w_up, m_ffn_dw_w, m_ffn_dw_b, m_w_down, m_norm_ffn_post, v_norm_mix_pre, v_w_in, v_conv_dw_w, v_conv_dw_b, v_conv_ln_g, v_conv_ln_b, v_rel_bias, v_w_out, v_norm_mix_post, v_norm_ffn_pre, v_w_up, v_ffn_dw_w, v_ffn_dw_b, v_w_down, v_norm_ffn_post):
    args = locals()
    w = {n: args[n][0] for n in WEIGHTS}
    m = {n: args["m_" + n][0] for n in WEIGHTS}
    v = {n: args["v_" + n][0] for n in WEIGHTS}
    for d in (w, m, v):
        d["rel_bias"] = d["rel_bias"].reshape(N_HEADS, 2 * MAX_REL + 1)
        for n in ("norm_mix_pre", "conv_dw_b", "conv_ln_g", "conv_ln_b", "norm_mix_post", "norm_ffn_pre",
                  "ffn_dw_b", "norm_ffn_post"):
            d[n] = d[n].reshape(1, -1)
    shard = 2 * lax.axis_index("x") + lax.axis_index("y")

    cw_sh = jnp.pad(w["conv_dw_w"], ((0, CONV_HALO - CONV_K), (0, 0)))
    fw_sh = jnp.pad(w["ffn_dw_w"], ((0, FF_HALO - 3), (0, 0)))
    pos = jnp.stack([shard, lax.axis_index("c")]).astype(jnp.int32)
    shards = [_place("place_" + n, w[n], pos, BF16) for n in BIG]
    shards += [_place("place_conv_dw_w", cw_sh, pos, F32), _place("place_ffn_dw_w", fw_sh, pos, F32)]
    w_in_f, w_out_f, w_up_f, w_down_f, cw_f, fw_f = _all_gather_weights(shards)
    cw_full = jnp.transpose(cw_f, (1, 0, 2)).reshape(CONV_HALO, CONV_W)

    loss, gx, small, big = _local_step(
        x[0], loss_target[0], w["norm_mix_pre"], w_in_f, cw_full, w["conv_dw_b"], w["conv_ln_g"],
        w["conv_ln_b"], w["rel_bias"], w_out_f.reshape(D_MODEL, D_MODEL), w["norm_mix_post"],
        w["norm_ffn_pre"], w_up_f, fw_f, w["ffn_dw_b"].reshape(N_CHIPS, 1, FF_SHARD),
        w_down_f.reshape(D_FF, D_MODEL), w["norm_ffn_post"])

    theirs = _pair_exchange([big[n] for n in BIG])
    pairs = [_pair_sum("pair_sum_" + n, big[n], b, pos) for n, b in zip(BIG, theirs)]
    got = _chip_exchange(pairs)
    halves = [_chip_sum("chip_sum_" + n, a, b, pos) for n, a, b in zip(BIG, pairs, got)]
    full = _pair_gather(halves)
    grads, deltas, new_m, new_v = {}, {}, {}, {}
    for n, g in zip(BIG, full):
        grads[n], deltas[n], new_m[n], new_v[n] = _adamw("adamw_" + n, w[n], g, m[n], v[n])

    gsum = _all_reduce_small(_pack([small[n] for n in SMALL_ORDER]))
    shapes = [small[n].shape for n in SMALL_ORDER]
    gs = dict(zip(SMALL_ORDER, _unpack(gsum, shapes)))
    gs["conv_dw_w"] = lax.dynamic_slice_in_dim(gs["conv_dw_w"], shard * LANES, LANES, axis=1)
    gs["ffn_dw_w"] = lax.dynamic_slice_in_dim(gs["ffn_dw_w"], shard * FF_SHARD, FF_SHARD, axis=1)
    shapes = [gs[n].shape for n in SMALL_ORDER]
    _, d_p, m_p, v_p = _adamw("adamw_small",_pack([w[n] for n in SMALL_ORDER]), _pack([gs[n] for n in SMALL_ORDER]),
                           _pack([m[n] for n in SMALL_ORDER]), _pack([v[n] for n in SMALL_ORDER]))
    for dst, packed in ((deltas, d_p), (new_m, m_p), (new_v, v_p)):
        dst.update(zip(SMALL_ORDER, _unpack(packed, shapes)))
    grads.update(gs)

    total = lax.psum(loss[0, 0], ("x", "y", "c"))
    outs = [total, gx[None]]
    for group in (grads, deltas, new_m, new_v):
        outs += [group[n].reshape(args[n].shape) for n in WEIGHTS]
    return tuple(outs)
```

```python
import functools
import math
from typing import Callable, NamedTuple

import numpy as np
import jax
import jax.numpy as jnp
from jax import lax
from jax.experimental import pallas as pl
from jax.experimental.pallas import tpu as pltpu

F32 = jnp.float32
BF16 = jnp.bfloat16

D_MODEL = 1024
CONV_W = 512
ATTN_W = 512
N_HEADS = 8
HEAD_DIM = 64
CHUNK = 64
N_LEFT = 8
MAX_REL = 128
CONV_K = 31
CONV_HALO = 32
D_FF = 2816
FF_SHARD = 1408
IN_COLS = 2560
IN_SHARD = 640
EPS = 1e-6
NEG_INF = -1e30
ATT_BLK = 256
N_ATT_TILES = 3
LANES = 128
SUBLANES = 8
N_CHIPS = 4

ADAM_LR = 0.001
ADAM_B1 = 0.9
ADAM_B2 = 0.999
ADAM_EPS = 1e-08
ADAM_WD = 0.01
ADAM_STEP = 10

MESH = pl.DeviceIdType.MESH
ANY = pl.BlockSpec(memory_space=pl.ANY)
VMEM_FULL = pl.BlockSpec(memory_space=pltpu.VMEM)


def _params(vmem_mb, n_grid=0):
    sem = ("arbitrary",) * n_grid if n_grid else None
    return pltpu.CompilerParams(dimension_semantics=sem, vmem_limit_bytes=vmem_mb << 20)


class _Rider(NamedTuple):
    operands: list
    out_shape: list
    aliases: dict
    sems: list
    start: Callable
    finish: Callable


def _merge_riders(a, b):
    ia, oa, sa = len(a.operands), len(a.out_shape), len(a.sems)

    def start(ins, outs, sems):
        a.start(ins[:ia], outs[:oa], sems[:sa])
        b.start(ins[ia:], outs[oa:], sems[sa:])

    def finish(ins, outs, sems):
        a.finish(ins[:ia], outs[:oa], sems[:sa])
        b.finish(ins[ia:], outs[oa:], sems[sa:])

    aliases = {**a.aliases, **{k + ia: v + oa for k, v in b.aliases.items()}}
    return _Rider(a.operands + b.operands, a.out_shape + b.out_shape, aliases, a.sems + b.sems, start, finish)


def _call(body, rider, *, grid=(), in_specs=(), out_specs=(), out_shape=(), scratch_shapes=(),
          input_output_aliases=None, **kwargs):
    in_specs, out_specs, out_shape = list(in_specs), list(out_specs), list(out_shape)
    scratch, aliases = list(scratch_shapes), dict(input_output_aliases or {})
    if rider is None:
        plain = pl.pallas_call(body, grid=grid, in_specs=in_specs, out_specs=out_specs, out_shape=out_shape,
                               scratch_shapes=scratch, input_output_aliases=aliases, **kwargs)
        return lambda *args: (plain(*args), [])
    n_in, n_out, n_scr = len(in_specs), len(out_specs), len(scratch)
    r_in, r_out = len(rider.operands), len(rider.out_shape)

    def carried(*refs):
        ins, r_ins, refs = refs[:n_in], refs[n_in:n_in + r_in], refs[n_in + r_in:]
        outs, r_outs, refs = refs[:n_out], refs[n_out:n_out + r_out], refs[n_out + r_out:]
        scr, r_sems = refs[:n_scr], refs[n_scr:]
        if not grid:
            rider.start(r_ins, r_outs, r_sems)
            body(*ins, *outs, *scr)
            rider.finish(r_ins, r_outs, r_sems)
            return
        at = [pl.program_id(d) for d in range(len(grid))]
        first = functools.reduce(jnp.logical_and, [p == 0 for p in at])
        last = functools.reduce(jnp.logical_and, [p == n - 1 for p, n in zip(at, grid)])

        @pl.when(first)
        def _():
            rider.start(r_ins, r_outs, r_sems)

        body(*ins, *outs, *scr)

        @pl.when(last)
        def _():
            rider.finish(r_ins, r_outs, r_sems)

    aliases.update({n_in + k: n_out + v for k, v in rider.aliases.items()})
    both = pl.pallas_call(carried, grid=grid, in_specs=in_specs + [ANY] * r_in, out_specs=out_specs + [ANY] * r_out,
                          out_shape=out_shape + rider.out_shape, scratch_shapes=scratch + rider.sems,
                          input_output_aliases=aliases, **kwargs)

    def run(*args):
        res = both(*args, *rider.operands)
        return res[:n_out], res[n_out:]

    return run


def _sigmoid(v):
    return 1.0 / (1.0 + jnp.exp(-v))


def _dot(a, b):
    return jnp.dot(a, b, preferred_element_type=F32)


def _dot_nt(a, b):
    return lax.dot_general(a, b, (((1,), (1,)), ((), ())), preferred_element_type=F32)


def _dot_tn(a, b):
    return lax.dot_general(a, b, (((0,), (0,)), ((), ())), preferred_element_type=F32)


def _rms_fwd(v, g):
    r = lax.rsqrt(jnp.mean(v * v, axis=-1, keepdims=True) + EPS)
    return v * r * g, r


def _rms_bwd(dy, v, g):
    r = lax.rsqrt(jnp.mean(v * v, axis=-1, keepdims=True) + EPS)
    vh = v * r
    dvh = dy * g
    dv = r * (dvh - vh * jnp.mean(dvh * vh, axis=-1, keepdims=True))
    return dv, dy * vh


def _fwd_in_proj(x, g1, w_in, rider=None):
    T = x.shape[0]
    tm = 512

    def body(x_ref, g_ref, w_ref, u_ref, a_ref, qkv_ref):
        u, _ = _rms_fwd(x_ref[...], g_ref[...])
        u = u.astype(BF16)
        u_ref[...] = u
        for s in range(N_CHIPS):
            y = _dot(u, w_ref[s])
            lo, hi = IN_SHARD * s, IN_SHARD * (s + 1)
            if hi <= 1024:
                a_ref[:, lo:hi] = y
            elif lo >= 1024:
                qkv_ref[:, lo - 1024:hi - 1024] = y.astype(BF16)
            else:
                a_ref[:, lo:1024] = y[:, :1024 - lo]
                qkv_ref[:, 0:hi - 1024] = y[:, 1024 - lo:].astype(BF16)

    return _call(
        body, rider, name="fwd_in_proj", grid=(T // tm,),
        in_specs=[pl.BlockSpec((tm, D_MODEL), lambda i: (i, 0)),
                  pl.BlockSpec((1, D_MODEL), lambda i: (0, 0)),
                  pl.BlockSpec((N_CHIPS, D_MODEL, IN_SHARD), lambda i: (0, 0, 0))],
        out_specs=[pl.BlockSpec((tm, D_MODEL), lambda i: (i, 0)),
                   pl.BlockSpec((tm, 1024), lambda i: (i, 0)),
                   pl.BlockSpec((tm, 1536), lambda i: (i, 0))],
        out_shape=[jax.ShapeDtypeStruct((T, D_MODEL), BF16),
                   jax.ShapeDtypeStruct((T, 1024), F32),
                   jax.ShapeDtypeStruct((T, 1536), BF16)],
        compiler_params=_params(40, 1),
    )(x, g1, w_in)


def _fill_shifted(ext, shifted, tm):
    n = tm + CONV_HALO - SUBLANES
    for j in range(1, SUBLANES):
        shifted[j - 1] = ext[j:j + n, :]


def _shifted_rows(ext, shifted, start, rows):
    j = start % SUBLANES
    if j == 0:
        return ext[start:start + rows, :]
    return shifted[j - 1, start - j:start - j + rows, :]


def _fwd_conv(a, cw, cb, lg, lb, rider=None):
    T = a.shape[0]
    tm = 512
    rc = 64

    def body(a_ref, w_ref, b_ref, lg_ref, lb_ref, co_ref, hc_ref, hext, hsh):
        i = pl.program_id(0)

        @pl.when(i == 0)
        def _():
            hext[0:CONV_HALO, :] = jnp.zeros((CONV_HALO, CONV_W), F32)

        @pl.when(i > 0)
        def _():
            hext[0:CONV_HALO, :] = hext[tm:tm + CONV_HALO, :]

        hext[CONV_HALO:CONV_HALO + tm, :] = a_ref[:, :CONV_W] * _sigmoid(a_ref[:, CONV_W:])
        _fill_shifted(hext, hsh, tm)
        for c in range(tm // rc):
            acc = jnp.zeros((rc, CONV_W), F32)
            for k in range(CONV_K):
                acc = acc + w_ref[k:k + 1, :] * _shifted_rows(hext, hsh, c * rc + 2 + k, rc)
            hc = acc + b_ref[...]
            hc_ref[c * rc:(c + 1) * rc, :] = hc
            mu = jnp.mean(hc, axis=-1, keepdims=True)
            xc = hc - mu
            var = jnp.mean(xc * xc, axis=-1, keepdims=True)
            z = xc * lax.rsqrt(var + EPS) * lg_ref[...] + lb_ref[...]
            co_ref[c * rc:(c + 1) * rc, :] = (z * _sigmoid(z)).astype(BF16)

    return _call(
        body, rider, name="fwd_conv", grid=(T // tm,),
        in_specs=[pl.BlockSpec((tm, 1024), lambda i: (i, 0)),
                  pl.BlockSpec((CONV_HALO, CONV_W), lambda i: (0, 0)),
                  pl.BlockSpec((1, CONV_W), lambda i: (0, 0)),
                  pl.BlockSpec((1, CONV_W), lambda i: (0, 0)),
                  pl.BlockSpec((1, CONV_W), lambda i: (0, 0))],
        out_specs=[pl.BlockSpec((tm, CONV_W), lambda i: (i, 0)),
                   pl.BlockSpec((tm, CONV_W), lambda i: (i, 0))],
        out_shape=[jax.ShapeDtypeStruct((T, CONV_W), BF16),
                   jax.ShapeDtypeStruct((T, CONV_W), F32)],
        scratch_shapes=[pltpu.VMEM((tm + CONV_HALO, CONV_W), F32),
                        pltpu.VMEM((SUBLANES - 1, tm + CONV_HALO - SUBLANES, CONV_W), F32)],
        compiler_params=_params(40, 1),
    )(a, cw, cb, lg, lb)


def _row_skew(v, sign):
    rows, width = v.shape
    row = lax.broadcasted_iota(jnp.int32, (rows, 1), 0)
    for b in range(int(math.log2(rows))):
        shift = (1 << b) if sign > 0 else width - (1 << b)
        v = jnp.where(((row >> b) & 1) == 1, pltpu.roll(v, shift, 1), v)
    return v


def _att_visible(d):
    rq = lax.broadcasted_iota(jnp.int32, (ATT_BLK, ATT_BLK), 0) // CHUNK
    ck = lax.broadcasted_iota(jnp.int32, (ATT_BLK, ATT_BLK), 1) // CHUNK
    slack = ATT_BLK
    above = jnp.where(d == 0, 0, slack)
    below = jnp.where(d == 2, 0, slack)
    return (ck <= rq + above) & (ck >= rq - below)


def _bias_tiles(vec):
    def body(v_ref, o_ref):
        d = pl.program_id(0) // N_HEADS
        full = _row_skew(jnp.broadcast_to(v_ref[0], (ATT_BLK, 2 * ATT_BLK)), 1)
        o_ref[0] = jnp.where(_att_visible(d), full[:, :ATT_BLK], NEG_INF)

    return pl.pallas_call(
        body, name="bias_tiles", grid=(N_ATT_TILES * N_HEADS,),
        in_specs=[pl.BlockSpec((1, 1, 2 * ATT_BLK), lambda n: (n, 0, 0))],
        out_specs=pl.BlockSpec((1, ATT_BLK, ATT_BLK), lambda n: (n, 0, 0)),
        out_shape=jax.ShapeDtypeStruct((N_ATT_TILES * N_HEADS, ATT_BLK, ATT_BLK), F32),
        compiler_params=_params(16, 1),
    )(vec)


def _diag_sums(ds):
    def body(d_ref, o_ref):
        wide = jnp.concatenate([d_ref[0], jnp.zeros((ATT_BLK, ATT_BLK), F32)], axis=1)
        o_ref[0] = jnp.sum(_row_skew(wide, -1), axis=0, keepdims=True)

    return pl.pallas_call(
        body, name="diag_sums", grid=(N_ATT_TILES * N_HEADS,),
        in_specs=[pl.BlockSpec((1, ATT_BLK, ATT_BLK), lambda n: (n, 0, 0))],
        out_specs=pl.BlockSpec((1, 1, 2 * ATT_BLK), lambda n: (n, 0, 0)),
        out_shape=jax.ShapeDtypeStruct((N_ATT_TILES * N_HEADS, 1, 2 * ATT_BLK), F32),
        compiler_params=_params(16, 1),
    )(ds)


def _head_mask(h):
    lane = lax.broadcasted_iota(jnp.int32, (1, LANES), 1)
    return (lane // HEAD_DIM) == (h % 2)


def _fwd_attn(qkv, bias, rider=None):
    T = qkv.shape[0]
    nb = T // ATT_BLK
    scale = HEAD_DIM ** -0.5

    def body(q_ref, k0_ref, k1_ref, k2_ref, v0_ref, v1_ref, v2_ref, b_ref, o_ref, lse_ref):
        i = pl.program_id(0)
        k_refs = (k0_ref, k1_ref, k2_ref)
        v_refs = (v0_ref, v1_ref, v2_ref)
        lane = lax.broadcasted_iota(jnp.int32, (1, LANES), 1)
        lse_tile = jnp.zeros((ATT_BLK, LANES), F32)
        for g in range(N_HEADS // 2):
            cols = slice(g * LANES, (g + 1) * LANES)
            qg = q_ref[:, cols]
            og = jnp.zeros((ATT_BLK, LANES), F32)
            for h in (2 * g, 2 * g + 1):
                hm = _head_mask(h)
                qh = jnp.where(hm, qg, jnp.zeros_like(qg))
                s = []
                for d in range(N_ATT_TILES):
                    sd = _dot_nt(qh, k_refs[d][:, cols]) * scale + b_ref[d * N_HEADS + h]
                    if d > 0:
                        sd = jnp.where(i >= d, sd, NEG_INF)
                    s.append(sd)
                m = jnp.maximum(jnp.maximum(jnp.max(s[0], axis=-1, keepdims=True),
                                            jnp.max(s[1], axis=-1, keepdims=True)),
                                jnp.max(s[2], axis=-1, keepdims=True))
                p = [jnp.exp(sd - m) for sd in s]
                l = (jnp.sum(p[0], axis=-1, keepdims=True) + jnp.sum(p[1], axis=-1, keepdims=True)
                     + jnp.sum(p[2], axis=-1, keepdims=True))
                oh = jnp.zeros((ATT_BLK, LANES), F32)
                for d in range(N_ATT_TILES):
                    vg = v_refs[d][:, cols]
                    oh = oh + _dot(p[d].astype(BF16), jnp.where(hm, vg, jnp.zeros_like(vg)))
                og = og + oh / l
                lse_tile = jnp.where(lane == h, m + jnp.log(l), lse_tile)
            o_ref[:, cols] = og.astype(BF16)
        lse_ref[...] = lse_tile

    def kv_spec(d, col):
        return pl.BlockSpec((ATT_BLK, ATTN_W), lambda i: (jnp.maximum(i - d, 0), col))

    return _call(
        body, rider, name="fwd_attn", grid=(nb,),
        in_specs=[pl.BlockSpec((ATT_BLK, ATTN_W), lambda i: (i, 0)),
                  kv_spec(0, 1), kv_spec(1, 1), kv_spec(2, 1),
                  kv_spec(0, 2), kv_spec(1, 2), kv_spec(2, 2),
                  pl.BlockSpec((N_ATT_TILES * N_HEADS, ATT_BLK, ATT_BLK), lambda i: (0, 0, 0))],
        out_specs=[pl.BlockSpec((ATT_BLK, ATTN_W), lambda i: (i, 0)),
                   pl.BlockSpec((ATT_BLK, LANES), lambda i: (i, 0))],
        out_shape=[jax.ShapeDtypeStruct((T, ATTN_W), BF16),
                   jax.ShapeDtypeStruct((T, LANES), F32)],
        compiler_params=_params(40, 1),
    )(qkv, qkv, qkv, qkv, qkv, qkv, qkv, bias)


def _fwd_out_proj(co, ao, w_out, x, g2, g3):
    T = x.shape[0]
    tm = 512

    def body(co_ref, ao_ref, w_ref, x_ref, g2_ref, g3_ref, mixed_ref, h1_ref, u2_ref):
        mixed = _dot(co_ref[...], w_ref[0:CONV_W, :]) + _dot(ao_ref[...], w_ref[CONV_W:, :])
        mixed_ref[...] = mixed
        y, _ = _rms_fwd(mixed, g2_ref[...])
        h1 = x_ref[...] + y
        h1_ref[...] = h1
        u2, _ = _rms_fwd(h1, g3_ref[...])
        u2_ref[...] = u2.astype(BF16)

    row = lambda w: pl.BlockSpec((tm, w), lambda i: (i, 0))
    vec = pl.BlockSpec((1, D_MODEL), lambda i: (0, 0))
    return pl.pallas_call(
        body, name="fwd_out_proj", grid=(T // tm,),
        in_specs=[row(CONV_W), row(ATTN_W), pl.BlockSpec((D_MODEL, D_MODEL), lambda i: (0, 0)),
                  row(D_MODEL), vec, vec],
        out_specs=[row(D_MODEL), row(D_MODEL), row(D_MODEL)],
        out_shape=[jax.ShapeDtypeStruct((T, D_MODEL), F32),
                   jax.ShapeDtypeStruct((T, D_MODEL), F32),
                   jax.ShapeDtypeStruct((T, D_MODEL), BF16)],
        compiler_params=_params(40, 1),
    )(co, ao, w_out, x, g2, g3)


GELU_C = math.sqrt(2.0 / math.pi)
GELU_A = 0.044715


def _gelu_and_grad(v):
    th = jnp.tanh(GELU_C * (v + GELU_A * v * v * v))
    gl = 0.5 * v * (1.0 + th)
    dgl = 0.5 * (1.0 + th) + 0.5 * v * (1.0 - th * th) * (GELU_C * (1.0 + 3.0 * GELU_A * v * v))
    return gl, dgl


FF_TM = 256
FF_HALO = 16
FF_CHUNKS = [(lo, min(lo + 256, FF_SHARD)) for lo in range(0, FF_SHARD, 256)]


def _rows_before(prev, cur):
    ext = jnp.concatenate([prev, cur], axis=0)
    return pltpu.roll(ext, 1, 0)[SUBLANES:], pltpu.roll(ext, 2, 0)[SUBLANES:]


def _rows_after(cur, nxt):
    ext = jnp.concatenate([cur, nxt], axis=0)
    n = ext.shape[0]
    return pltpu.roll(ext, n - 1, 0)[:cur.shape[0]], pltpu.roll(ext, n - 2, 0)[:cur.shape[0]]


def _fwd_ffn(u2, w_up, fw, fb, w_down):
    T = u2.shape[0]
    tm = FF_TM

    def body(u_ref, wg_ref, wv_ref, fwg_ref, fwv_ref, fbg_ref, fbv_ref, wd_ref, hf_ref, f_ref, carg, carv):
        i = pl.program_id(1)

        @pl.when(i == 0)
        def _():
            carg[...] = jnp.zeros(carg.shape, F32)
            carv[...] = jnp.zeros(carv.shape, F32)

        u = u_ref[...]
        f = None
        up = lambda lo, hi: (_dot(u, wg_ref[0, :, lo:hi]), _dot(u, wv_ref[0, :, lo:hi]))
        ahead = up(*FF_CHUNKS[0])
        for c, (lo, hi) in enumerate(FF_CHUNKS):
            conv = []
            hs = ahead
            if c + 1 < len(FF_CHUNKS):
                ahead = up(*FF_CHUNKS[c + 1])
            for n, (car, fw_ref, fb_ref) in enumerate(((carg, fwg_ref, fbg_ref), (carv, fwv_ref, fbv_ref))):
                h0 = hs[n]
                hf_ref[n, :, lo:hi] = h0.astype(BF16)
                h1, h2 = _rows_before(car[:, lo:hi], h0)
                car[:, lo:hi] = h0[tm - SUBLANES:, :]
                conv.append(fw_ref[0, 0:1, lo:hi] * h2 + fw_ref[0, 1:2, lo:hi] * h1
                            + fw_ref[0, 2:3, lo:hi] * h0 + fb_ref[0, :, lo:hi])
            gl, _ = _gelu_and_grad(conv[0])
            term = _dot((gl * conv[1]).astype(BF16), wd_ref[lo:hi, :])
            f = term if f is None else f + term
        f_ref[0] = f

    wspec = lambda off: pl.BlockSpec((1, D_MODEL, FF_SHARD), lambda s, i: (s + off, 0, 0))
    fwspec = lambda off: pl.BlockSpec((1, FF_HALO, FF_SHARD), lambda s, i: (s + off, 0, 0))
    fbspec = lambda off: pl.BlockSpec((1, 1, FF_SHARD), lambda s, i: (s + off, 0, 0))
    return pl.pallas_call(
        body, name="fwd_ffn", grid=(2, T // tm),
        in_specs=[pl.BlockSpec((tm, D_MODEL), lambda s, i: (i, 0)),
                  wspec(0), wspec(2), fwspec(0), fwspec(2), fbspec(0), fbspec(2),
                  pl.BlockSpec((FF_SHARD, D_MODEL), lambda s, i: (s, 0))],
        out_specs=[pl.BlockSpec((2, tm, FF_SHARD), lambda s, i: (0, i, s)),
                   pl.BlockSpec((1, tm, D_MODEL), lambda s, i: (s, i, 0))],
        out_shape=[jax.ShapeDtypeStruct((2, T, D_FF), BF16),
                   jax.ShapeDtypeStruct((2, T, D_MODEL), F32)],
        scratch_shapes=[pltpu.VMEM((SUBLANES, FF_SHARD), F32), pltpu.VMEM((SUBLANES, FF_SHARD), F32)],
        compiler_params=_params(48, 2),
    )(u2, w_up, w_up, fw, fw, fb, fb, w_down)


def _fwd_loss(fp, h1, tgt, g4):
    T = h1.shape[0]
    tm = 512

    def body(fp_ref, h1_ref, t_ref, g_ref, loss_ref, dy_ref, df_ref, dg_ref):
        i = pl.program_id(0)
        f = fp_ref[0] + fp_ref[1]
        r, _ = _rms_fwd(f, g_ref[...])
        e = (h1_ref[...] + r) - t_ref[...]
        dy = e * (1.0 / D_MODEL)
        dy_ref[...] = dy
        df, dg_rows = _rms_bwd(dy, f, g_ref[...])
        df_ref[...] = df.astype(BF16)
        part = 0.5 * jnp.sum(jnp.mean(e * e, axis=-1, keepdims=True), axis=0, keepdims=True)
        dg = jnp.sum(dg_rows, axis=0, keepdims=True)

        @pl.when(i == 0)
        def _():
            loss_ref[...] = part
            dg_ref[...] = dg

        @pl.when(i > 0)
        def _():
            loss_ref[...] += part
            dg_ref[...] += dg

    row = pl.BlockSpec((tm, D_MODEL), lambda i: (i, 0))
    vec = pl.BlockSpec((1, D_MODEL), lambda i: (0, 0))
    return pl.pallas_call(
        body, name="fwd_loss", grid=(T // tm,),
        in_specs=[pl.BlockSpec((2, tm, D_MODEL), lambda i: (0, i, 0)), row, row, vec],
        out_specs=[pl.BlockSpec((1, 1), lambda i: (0, 0)), row, row, vec],
        out_shape=[jax.ShapeDtypeStruct((1, 1), F32),
                   jax.ShapeDtypeStruct((T, D_MODEL), F32),
                   jax.ShapeDtypeStruct((T, D_MODEL), BF16),
                   jax.ShapeDtypeStruct((1, D_MODEL), F32)],
        compiler_params=_params(40, 1),
    )(fp, h1, tgt, g4)


def _bwd_ffn(df, hf, w_up, fw, fb, w_down):
    T = df.shape[0]
    tm = FF_TM
    ni = T // tm

    def body(df_ref, hf_ref, halo_ref, wd_ref, wg_ref, wv_ref, fwg_ref, fwv_ref, fbg_ref, fbv_ref,
             du_ref, dhf_ref, act_ref, dwg_ref, dwv_ref, carg, carv):
        i = pl.program_id(1)
        ri = ni - 1 - i

        @pl.when(i == 0)
        def _():
            dwg_ref[...] = jnp.zeros(dwg_ref.shape, F32)
            dwv_ref[...] = jnp.zeros(dwv_ref.shape, F32)
            carg[...] = jnp.zeros(carg.shape, F32)
            carv[...] = jnp.zeros(carv.shape, F32)

        df = df_ref[...]
        du = None
        down = lambda lo, hi: _dot_nt(df, wd_ref[lo:hi, :])
        ahead = down(*FF_CHUNKS[0])
        for c, (lo, hi) in enumerate(FF_CHUNKS):
            dact = ahead
            if c + 1 < len(FF_CHUNKS):
                ahead = down(*FF_CHUNKS[c + 1])
            hs, pre = [], []
            for n, (fw_ref, fb_ref) in enumerate(((fwg_ref, fbg_ref), (fwv_ref, fbv_ref))):
                h0 = hf_ref[n, :, lo:hi].astype(F32)
                halo = jnp.where(ri > 0, halo_ref[n, :, lo:hi].astype(F32)[FF_HALO - SUBLANES:], 0.0)
                h1, h2 = _rows_before(halo, h0)
                hs.append((h2, h1, h0))
                pre.append(fw_ref[0, 0:1, lo:hi] * h2 + fw_ref[0, 1:2, lo:hi] * h1
                           + fw_ref[0, 2:3, lo:hi] * h0 + fb_ref[0, :, lo:hi])
            gl, dgl = _gelu_and_grad(pre[0])
            act_ref[:, lo:hi] = (gl * pre[1]).astype(BF16)
            dpre = (dact * pre[1] * dgl, dact * gl)
            for n, (car, fw_ref, dw_ref, w_ref) in enumerate(
                    ((carg, fwg_ref, dwg_ref, wg_ref), (carv, fwv_ref, dwv_ref, wv_ref))):
                dp = dpre[n]
                for k in range(3):
                    dw_ref[0, k:k + 1, lo:hi] += jnp.sum(dp * hs[n][k], axis=0, keepdims=True)
                dw_ref[0, 3:4, lo:hi] += jnp.sum(dp, axis=0, keepdims=True)
                up1, up2 = _rows_after(dp, car[:, lo:hi])
                car[:, lo:hi] = dp[0:SUBLANES, :]
                dh = (fw_ref[0, 2:3, lo:hi] * dp + fw_ref[0, 1:2, lo:hi] * up1
                      + fw_ref[0, 0:1, lo:hi] * up2).astype(BF16)
                dhf_ref[n, :, lo:hi] = dh
                term = _dot_nt(dh, w_ref[0, :, lo:hi])
                du = term if du is None else du + term
        du_ref[0] = du

    rev = lambda i: ni - 1 - i
    wspec = lambda off: pl.BlockSpec((1, D_MODEL, FF_SHARD), lambda s, i: (s + off, 0, 0))
    fwspec = lambda off: pl.BlockSpec((1, FF_HALO, FF_SHARD), lambda s, i: (s + off, 0, 0))
    fbspec = lambda off: pl.BlockSpec((1, 1, FF_SHARD), lambda s, i: (s + off, 0, 0))
    halo_blocks = tm // FF_HALO
    dwspec = pl.BlockSpec((1, FF_HALO, FF_SHARD), lambda s, i: (s, 0, 0))
    return pl.pallas_call(
        body, name="bwd_ffn", grid=(2, ni),
        in_specs=[pl.BlockSpec((tm, D_MODEL), lambda s, i: (rev(i), 0)),
                  pl.BlockSpec((2, tm, FF_SHARD), lambda s, i: (0, rev(i), s)),
                  pl.BlockSpec((2, FF_HALO, FF_SHARD),
                               lambda s, i: (0, jnp.maximum(rev(i) * halo_blocks - 1, 0), s)),
                  pl.BlockSpec((FF_SHARD, D_MODEL), lambda s, i: (s, 0)),
                  wspec(0), wspec(2), fwspec(0), fwspec(2), fbspec(0), fbspec(2)],
        out_specs=[pl.BlockSpec((1, tm, D_MODEL), lambda s, i: (s, rev(i), 0)),
                   pl.BlockSpec((2, tm, FF_SHARD), lambda s, i: (0, rev(i), s)),
                   pl.BlockSpec((tm, FF_SHARD), lambda s, i: (rev(i), s)),
                   dwspec, dwspec],
        out_shape=[jax.ShapeDtypeStruct((2, T, D_MODEL), F32),
                   jax.ShapeDtypeStruct((2, T, D_FF), BF16),
                   jax.ShapeDtypeStruct((T, D_FF), BF16),
                   jax.ShapeDtypeStruct((2, FF_HALO, FF_SHARD), F32),
                   jax.ShapeDtypeStruct((2, FF_HALO, FF_SHARD), F32)],
        scratch_shapes=[pltpu.VMEM((SUBLANES, FF_SHARD), F32), pltpu.VMEM((SUBLANES, FF_SHARD), F32)],
        compiler_params=_params(56, 2),
    )(df, hf, hf, w_down, w_up, w_up, fw, fw, fb, fb)


def _bwd_mid(du2p, dy, h1, mixed, g3, g2, w_out, rider=None):
    T = dy.shape[0]
    tm = 512

    def body(du_ref, dy_ref, h1_ref, mx_ref, g3_ref, g2_ref, w_ref,
             dh1_ref, dmx_ref, dco_ref, dao_ref, dg3_ref, dg2_ref):
        i = pl.program_id(0)
        dres, dg3_rows = _rms_bwd(du_ref[0] + du_ref[1], h1_ref[...], g3_ref[...])
        dh1 = dy_ref[...] + dres
        dh1_ref[...] = dh1
        dmx, dg2_rows = _rms_bwd(dh1, mx_ref[...], g2_ref[...])
        dmx = dmx.astype(BF16)
        dmx_ref[...] = dmx
        dcat = _dot_nt(dmx, w_ref[...])
        dco_ref[...] = dcat[:, :CONV_W]
        dao_ref[...] = dcat[:, CONV_W:].astype(BF16)
        dg3 = jnp.sum(dg3_rows, axis=0, keepdims=True)
        dg2 = jnp.sum(dg2_rows, axis=0, keepdims=True)

        @pl.when(i == 0)
        def _():
            dg3_ref[...] = dg3
            dg2_ref[...] = dg2

        @pl.when(i > 0)
        def _():
            dg3_ref[...] += dg3
            dg2_ref[...] += dg2

    row = lambda w: pl.BlockSpec((tm, w), lambda i: (i, 0))
    vec = pl.BlockSpec((1, D_MODEL), lambda i: (0, 0))
    return _call(
        body, rider, name="bwd_mid", grid=(T // tm,),
        in_specs=[pl.BlockSpec((2, tm, D_MODEL), lambda i: (0, i, 0)), row(D_MODEL), row(D_MODEL),
                  row(D_MODEL), vec, vec, pl.BlockSpec((D_MODEL, D_MODEL), lambda i: (0, 0))],
        out_specs=[row(D_MODEL), row(D_MODEL), row(CONV_W), row(ATTN_W), vec, vec],
        out_shape=[jax.ShapeDtypeStruct((T, D_MODEL), F32),
                   jax.ShapeDtypeStruct((T, D_MODEL), BF16),
                   jax.ShapeDtypeStruct((T, CONV_W), F32),
                   jax.ShapeDtypeStruct((T, ATTN_W), BF16),
                   jax.ShapeDtypeStruct((1, D_MODEL), F32),
                   jax.ShapeDtypeStruct((1, D_MODEL), F32)],
        compiler_params=_params(48, 1),
    )(du2p, dy, h1, mixed, g3, g2, w_out)


def _bwd_attn(qkv, ao, dao, lse, bias, rider=None):
    T = qkv.shape[0]
    nb = T // ATT_BLK
    scale = HEAD_DIM ** -0.5

    def body(k_ref, v_ref, q0, q1, q2, do0, do1, do2, o0, o1, o2, l0, l1, l2, b_ref,
             dp_ref, ds_ref, acc1, acc2):
        j = pl.program_id(0)
        q_refs, do_refs, o_refs, l_refs = (q0, q1, q2), (do0, do1, do2), (o0, o1, o2), (l0, l1, l2)

        @pl.when(j == 0)
        def _():
            ds_ref[...] = jnp.zeros(ds_ref.shape, F32)
            acc1[...] = jnp.zeros(acc1.shape, F32)
            acc2[...] = jnp.zeros(acc2.shape, F32)

        dq_new = [[], [], []]
        dk_cols, dv_cols = [], []
        for g in range(N_HEADS // 2):
            cols = slice(g * LANES, (g + 1) * LANES)
            kg = k_ref[:, cols]
            vg = v_ref[:, cols]
            dkg = jnp.zeros((ATT_BLK, LANES), F32)
            dvg = jnp.zeros((ATT_BLK, LANES), F32)
            dqg = [jnp.zeros((ATT_BLK, LANES), F32) for _ in range(N_ATT_TILES)]
            for d in range(N_ATT_TILES):
                qg = q_refs[d][:, cols]
                dog = do_refs[d][:, cols]
                prod = dog.astype(F32) * o_refs[d][:, cols].astype(F32)
                for h in (2 * g, 2 * g + 1):
                    hm = _head_mask(h)
                    qh = jnp.where(hm, qg, jnp.zeros_like(qg))
                    doh = jnp.where(hm, dog, jnp.zeros_like(dog))
                    kh = jnp.where(hm, kg, jnp.zeros_like(kg))
                    delta = jnp.sum(jnp.where(hm, prod, 0.0), axis=-1, keepdims=True)
                    s = _dot_nt(qh, kg) * scale + b_ref[d * N_HEADS + h]
                    p = jnp.exp(s - l_refs[d][:, h:h + 1])
                    p = jnp.where(j + d < nb, p, 0.0)
                    dvg = dvg + _dot_tn(p.astype(BF16), doh)
                    dpm = _dot_nt(doh, vg)
                    dsc = p * (dpm - delta)
                    ds_ref[d * N_HEADS + h] += dsc
                    dsb = (dsc * scale).astype(BF16)
                    dqg[d] = dqg[d] + _dot(dsb, kh)
                    dkg = dkg + _dot_tn(dsb, qh)
            for d in range(N_ATT_TILES):
                dq_new[d].append(dqg[d])
            dk_cols.append(dkg)
            dv_cols.append(dvg)
        x0, x1, x2 = (jnp.concatenate(c, axis=1) for c in dq_new)
        dp_ref[:, 0:1024] = jnp.zeros((ATT_BLK, 1024), BF16)
        dp_ref[:, 1024:1536] = (acc1[...] + x0).astype(BF16)
        dp_ref[:, 1536:2048] = jnp.concatenate(dk_cols, axis=1).astype(BF16)
        dp_ref[:, 2048:2560] = jnp.concatenate(dv_cols, axis=1).astype(BF16)
        acc1[...] = acc2[...] + x1
        acc2[...] = x2

    def fwd_spec(d, width, col):
        return pl.BlockSpec((ATT_BLK, width), lambda j: (jnp.minimum(j + d, nb - 1), col))

    return _call(
        body, rider, name="bwd_attn", grid=(nb,),
        in_specs=[pl.BlockSpec((ATT_BLK, ATTN_W), lambda j: (j, 1)),
                  pl.BlockSpec((ATT_BLK, ATTN_W), lambda j: (j, 2)),
                  fwd_spec(0, ATTN_W, 0), fwd_spec(1, ATTN_W, 0), fwd_spec(2, ATTN_W, 0),
                  fwd_spec(0, ATTN_W, 0), fwd_spec(1, ATTN_W, 0), fwd_spec(2, ATTN_W, 0),
                  fwd_spec(0, ATTN_W, 0), fwd_spec(1, ATTN_W, 0), fwd_spec(2, ATTN_W, 0),
                  fwd_spec(0, LANES, 0), fwd_spec(1, LANES, 0), fwd_spec(2, LANES, 0),
                  pl.BlockSpec((N_ATT_TILES * N_HEADS, ATT_BLK, ATT_BLK), lambda j: (0, 0, 0))],
        out_specs=[pl.BlockSpec((ATT_BLK, IN_COLS), lambda j: (j, 0)),
                   pl.BlockSpec((N_ATT_TILES * N_HEADS, ATT_BLK, ATT_BLK), lambda j: (0, 0, 0))],
        out_shape=[jax.ShapeDtypeStruct((T, IN_COLS), BF16),
                   jax.ShapeDtypeStruct((N_ATT_TILES * N_HEADS, ATT_BLK, ATT_BLK), F32)],
        scratch_shapes=[pltpu.VMEM((ATT_BLK, ATTN_W), F32), pltpu.VMEM((ATT_BLK, ATTN_W), F32)],
        compiler_params=_params(56, 1),
    )(qkv, qkv, qkv, qkv, qkv, dao, dao, dao, ao, ao, ao, lse, lse, lse, bias)


def _bwd_conv(dproj, a, dco, hc, cw, lg, lb, rider=None):
    T = a.shape[0]
    tm = 512
    rc = 32
    ni = T // tm
    hb = tm // CONV_HALO

    def body(dp_in, a_ref, ap_ref, dco_ref, dcon_ref, hc_ref, hcn_ref, w_ref, lg_ref, lb_ref,
             dp_ref, dw_ref, db_ref, dlg_ref, dlb_ref, hext, dext, hsh, dsh, dwacc):
        del dp_in
        i = pl.program_id(0)

        def ln_bwd(dco_v, hc_v):
            mu = jnp.mean(hc_v, axis=-1, keepdims=True)
            xc = hc_v - mu
            rstd = lax.rsqrt(jnp.mean(xc * xc, axis=-1, keepdims=True) + EPS)
            xh = xc * rstd
            z = xh * lg_ref[...] + lb_ref[...]
            sg = _sigmoid(z)
            dz = dco_v * (sg * (1.0 + z * (1.0 - sg)))
            dxh = dz * lg_ref[...]
            dhc = rstd * (dxh - jnp.mean(dxh, axis=-1, keepdims=True)
                          - xh * jnp.mean(dxh * xh, axis=-1, keepdims=True))
            return dhc, dz * xh, dz

        hext[0:CONV_HALO, :] = jnp.where(i > 0, ap_ref[:, :CONV_W] * _sigmoid(ap_ref[:, CONV_W:]), 0.0)
        hext[CONV_HALO:CONV_HALO + tm, :] = a_ref[:, :CONV_W] * _sigmoid(a_ref[:, CONV_W:])
        dhc, dlg_rows, dlb_rows = ln_bwd(dco_ref[...], hc_ref[...])
        dext[0:tm, :] = dhc
        dhc_next, _, _ = ln_bwd(dcon_ref[...], hcn_ref[...])
        dext[tm:tm + CONV_HALO, :] = jnp.where(i < ni - 1, dhc_next, 0.0)

        @pl.when(i == 0)
        def _():
            dw_ref[...] = jnp.zeros(dw_ref.shape, F32)
            db_ref[...] = jnp.zeros(db_ref.shape, F32)
            dlg_ref[...] = jnp.zeros(dlg_ref.shape, F32)
            dlb_ref[...] = jnp.zeros(dlb_ref.shape, F32)

            dwacc[...] = jnp.zeros(dwacc.shape, F32)

        db_ref[...] += jnp.sum(dhc, axis=0, keepdims=True)
        dlg_ref[...] += jnp.sum(dlg_rows, axis=0, keepdims=True)
        dlb_ref[...] += jnp.sum(dlb_rows, axis=0, keepdims=True)
        _fill_shifted(hext, hsh, tm)
        _fill_shifted(dext, dsh, tm)
        for c in range(tm // rc):
            r0 = c * rc
            dh = jnp.zeros((rc, CONV_W), F32)
            dhc_c = dext[r0:r0 + rc, :]
            for k in range(CONV_K):
                dh = dh + w_ref[k:k + 1, :] * _shifted_rows(dext, dsh, r0 + 30 - k, rc)
                prod = dhc_c * _shifted_rows(hext, hsh, r0 + 2 + k, rc)
                dwacc[k] += jnp.sum(prod.reshape(rc // SUBLANES, SUBLANES, CONV_W), axis=0)
            av = a_ref[r0:r0 + rc, :CONV_W]
            sg = _sigmoid(a_ref[r0:r0 + rc, CONV_W:])
            dp_ref[r0:r0 + rc, 0:CONV_W] = (dh * sg).astype(BF16)
            dp_ref[r0:r0 + rc, CONV_W:] = (dh * av * sg * (1.0 - sg)).astype(BF16)

        @pl.when(i == ni - 1)
        def _():
            dw_ref[...] = jnp.sum(dwacc[...], axis=1)

    row = lambda w: pl.BlockSpec((tm, w), lambda i: (i, 0))
    prev = lambda w: pl.BlockSpec((CONV_HALO, w), lambda i: (jnp.maximum(i * hb - 1, 0), 0))
    nxt = lambda w: pl.BlockSpec((CONV_HALO, w), lambda i: (jnp.minimum((i + 1) * hb, ni * hb - 1), 0))
    vec = pl.BlockSpec((1, CONV_W), lambda i: (0, 0))
    return _call(
        body, rider, name="bwd_conv", grid=(ni,),
        in_specs=[ANY, row(1024), prev(1024), row(CONV_W), nxt(CONV_W), row(CONV_W), nxt(CONV_W),
                  pl.BlockSpec((CONV_HALO, CONV_W), lambda i: (0, 0)), vec, vec],
        out_specs=[pl.BlockSpec((tm, 1024), lambda i: (i, 0)),
                   pl.BlockSpec((CONV_HALO, CONV_W), lambda i: (0, 0)), vec, vec, vec],
        out_shape=[jax.ShapeDtypeStruct((T, IN_COLS), BF16),
                   jax.ShapeDtypeStruct((CONV_HALO, CONV_W), F32),
                   jax.ShapeDtypeStruct((1, CONV_W), F32),
                   jax.ShapeDtypeStruct((1, CONV_W), F32),
                   jax.ShapeDtypeStruct((1, CONV_W), F32)],
        scratch_shapes=[pltpu.VMEM((tm + CONV_HALO, CONV_W), F32), pltpu.VMEM((tm + CONV_HALO, CONV_W), F32),
                        pltpu.VMEM((SUBLANES - 1, tm + CONV_HALO - SUBLANES, CONV_W), F32),
                        pltpu.VMEM((SUBLANES - 1, tm + CONV_HALO - SUBLANES, CONV_W), F32),
                        pltpu.VMEM((CONV_HALO, SUBLANES, CONV_W), F32)],
        input_output_aliases={0: 0},
        compiler_params=_params(56, 1),
    )(dproj, a, a, dco, dco, hc, hc, cw, lg, lb)


def _bwd_in_proj(dproj, w_in, x, dh1, g1, rider=None):
    T = x.shape[0]
    tm = 512

    def body(dp_ref, w_ref, x_ref, dh_ref, g_ref, gx_ref, dg_ref):
        i = pl.program_id(0)
        du = None
        for s in range(N_CHIPS):
            term = _dot_nt(dp_ref[:, IN_SHARD * s:IN_SHARD * (s + 1)], w_ref[s])
            du = term if du is None else du + term
        dx, dg_rows = _rms_bwd(du, x_ref[...], g_ref[...])
        gx_ref[...] = dh_ref[...] + dx
        dg = jnp.sum(dg_rows, axis=0, keepdims=True)

        @pl.when(i == 0)
        def _():
            dg_ref[...] = dg

        @pl.when(i > 0)
        def _():
            dg_ref[...] += dg

    row = lambda w: pl.BlockSpec((tm, w), lambda i: (i, 0))
    vec = pl.BlockSpec((1, D_MODEL), lambda i: (0, 0))
    return _call(
        body, rider, name="bwd_in_proj", grid=(T // tm,),
        in_specs=[row(IN_COLS), pl.BlockSpec((N_CHIPS, D_MODEL, IN_SHARD), lambda i: (0, 0, 0)),
                  row(D_MODEL), row(D_MODEL), vec],
        out_specs=[row(D_MODEL), vec],
        out_shape=[jax.ShapeDtypeStruct((T, D_MODEL), F32), jax.ShapeDtypeStruct((1, D_MODEL), F32)],
        compiler_params=_params(40, 1),
    )(dproj, w_in, x, dh1, g1)


def _wgrad(name, a_list, a_spec, b, b_spec, out_block, out_spec, out_shape, n_outer, T, tk=512, select=None):
    def body(*refs):
        a_refs, b_ref, o_ref = refs[:len(a_list)], refs[len(a_list)], refs[len(a_list) + 1]
        kt = pl.program_id(1)

        @pl.when(kt == 0)
        def _():
            o_ref[...] = jnp.zeros(o_ref.shape, F32)

        bv = b_ref[...].reshape(b_ref.shape[-2:])
        if select is None:
            o_ref[...] += _dot_tn(a_refs[0][...].reshape(a_refs[0].shape[-2:]), bv).reshape(o_ref.shape)
        else:
            for n, a_ref in enumerate(a_refs):
                @pl.when(select(pl.program_id(0)) == n)
                def _():
                    o_ref[...] += _dot_tn(a_ref[...], bv).reshape(o_ref.shape)

    del out_block
    return pl.pallas_call(
        body, name=name, grid=(n_outer, T // tk),
        in_specs=[a_spec] * len(a_list) + [b_spec],
        out_specs=out_spec, out_shape=out_shape,
        compiler_params=_params(48, 2),
    )(*a_list, b)


def _mesh_pos():
    return lax.axis_index("x"), lax.axis_index("y"), lax.axis_index("c")


def _other_chips(x, y):
    return [((1 - x, y), 2 * (1 - x) + y), ((x, 1 - y), 2 * x + (1 - y)), ((1 - x, 1 - y), 2 * (1 - x) + (1 - y))]


def _all_gather_weights(shards):
    n = len(shards)

    def body(*refs):
        out_refs = refs[n:2 * n]
        send_sems, recv_sems = refs[2 * n:]
        x, y, c = _mesh_pos()
        me = 2 * x + y
        sibling = (x, y, 1 - c)
        chips = _other_chips(x, y)
        first, passed = [], []
        for t in range(n):
            half = shards[t].shape[1] // 2
            mine = out_refs[t].at[me, pl.ds(c * half, half)]
            for k, (chip, _) in enumerate(chips):
                cp = pltpu.make_async_remote_copy(
                    src_ref=mine, dst_ref=mine, send_sem=send_sems.at[t, k], recv_sem=recv_sems.at[t, k],
                    device_id=(*chip, c), device_id_type=MESH)
                cp.start()
                first.append(cp)
        for t in range(n):
            half = shards[t].shape[1] // 2
            rows = pl.ds(c * half, half)
            for k, (chip, s) in enumerate(chips):
                landed = out_refs[t].at[s, rows]
                pltpu.make_async_remote_copy(
                    src_ref=landed, dst_ref=landed, send_sem=send_sems.at[t, k], recv_sem=recv_sems.at[t, k],
                    device_id=(*chip, c), device_id_type=MESH).wait_recv()
                cp = pltpu.make_async_remote_copy(
                    src_ref=landed, dst_ref=landed, send_sem=send_sems.at[t, 3 + k], recv_sem=recv_sems.at[t, 3 + k],
                    device_id=sibling, device_id_type=MESH)
                cp.start()
                passed.append(cp)
        for t in range(n):
            half = shards[t].shape[1] // 2
            other = pl.ds((1 - c) * half, half)
            for k, (chip, s) in enumerate(chips):
                got = out_refs[t].at[s, other]
                pltpu.make_async_remote_copy(
                    src_ref=got, dst_ref=got, send_sem=send_sems.at[t, 3 + k], recv_sem=recv_sems.at[t, 3 + k],
                    device_id=sibling, device_id_type=MESH).wait_recv()
        for cp in first + passed:
            cp.wait_send()

    return pl.pallas_call(
        body, name="all_gather_weights",
        in_specs=[ANY] * n, out_specs=[ANY] * n,
        out_shape=[jax.ShapeDtypeStruct(s.shape, s.dtype) for s in shards],
        scratch_shapes=[pltpu.SemaphoreType.DMA((n, 6)), pltpu.SemaphoreType.DMA((n, 6))],
        input_output_aliases={t: t for t in range(n)},
    )(*shards)


def _exchange_rider(operands, out_shape, aliases, sem_shape, pairs):
    def start(ins, outs, sems):
        for send, _ in pairs(ins, outs, *sems):
            send.start()

    def finish(ins, outs, sems):
        for send, recv in pairs(ins, outs, *sems):
            send.wait_send()
            recv.wait_recv()

    sems = [pltpu.SemaphoreType.DMA(sem_shape), pltpu.SemaphoreType.DMA(sem_shape)]
    return _Rider(list(operands), list(out_shape), aliases, sems, start, finish)


def _remote(src, dst, send_sem, recv_sem, device):
    return pltpu.make_async_remote_copy(src_ref=src, dst_ref=dst, send_sem=send_sem, recv_sem=recv_sem,
                                        device_id=device, device_id_type=MESH)


def _fetch_rider(bufs):
    def pairs(ins, outs, send_sems, recv_sems):
        x, y, c = _mesh_pos()
        res = []
        for t, buf in enumerate(bufs):
            rows = pl.ds(c * (buf.shape[1] // 2), buf.shape[1] // 2)
            mine = outs[t].at[2 * x + y, rows]
            for k, (chip, s) in enumerate(_other_chips(x, y)):
                landed = outs[t].at[s, rows]
                res.append((_remote(mine, mine, send_sems.at[t, k], recv_sems.at[t, k], (*chip, c)),
                            _remote(landed, landed, send_sems.at[t, k], recv_sems.at[t, k], (*chip, c))))
        return res

    shapes = [jax.ShapeDtypeStruct(b.shape, b.dtype) for b in bufs]
    return _exchange_rider(bufs, shapes, {t: t for t in range(len(bufs))}, (len(bufs), 3), pairs)


def _forward_rider(bufs):
    def pairs(ins, outs, send_sems, recv_sems):
        x, y, c = _mesh_pos()
        res = []
        for t, buf in enumerate(bufs):
            half = buf.shape[1] // 2
            for k, (_, s) in enumerate(_other_chips(x, y)):
                landed = outs[t].at[s, pl.ds(c * half, half)]
                theirs = outs[t].at[s, pl.ds((1 - c) * half, half)]
                res.append((_remote(landed, landed, send_sems.at[t, k], recv_sems.at[t, k], (x, y, 1 - c)),
                            _remote(theirs, theirs, send_sems.at[t, k], recv_sems.at[t, k], (x, y, 1 - c))))
        return res

    shapes = [jax.ShapeDtypeStruct(b.shape, b.dtype) for b in bufs]
    return _exchange_rider(bufs, shapes, {t: t for t in range(len(bufs))}, (len(bufs), 3), pairs)


def _pair_exchange_rider(grads):
    def pairs(ins, outs, send_sems, recv_sems):
        x, y, c = _mesh_pos()
        res = []
        for t, g in enumerate(grads):
            half = g.shape[1] // 2
            cp = _remote(ins[t].at[:, pl.ds((1 - c) * half, half), :], outs[t], send_sems.at[t], recv_sems.at[t],
                         (x, y, 1 - c))
            res.append((cp, cp))
        return res

    shapes = [jax.ShapeDtypeStruct((N_CHIPS, g.shape[1] // 2, g.shape[2]), F32) for g in grads]
    return _exchange_rider(grads, shapes, {}, (len(grads),), pairs)


def _chip_exchange_rider(sums):
    def pairs(ins, outs, send_sems, recv_sems):
        x, y, c = _mesh_pos()
        res = []
        for t in range(len(sums)):
            for k, (chip, s) in enumerate(_other_chips(x, y)):
                cp = _remote(ins[t].at[s], outs[t].at[k], send_sems.at[t, k], recv_sems.at[t, k], (*chip, c))
                res.append((cp, cp))
        return res

    shapes = [jax.ShapeDtypeStruct((3,) + p.shape[1:], p.dtype) for p in sums]
    return _exchange_rider(sums, shapes, {}, (len(sums), 3), pairs)


def _pair_gather_rider(fulls):
    def pairs(ins, outs, send_sems, recv_sems):
        x, y, c = _mesh_pos()
        res = []
        for t, f in enumerate(fulls):
            half = f.shape[0] // 2
            mine = outs[t].at[pl.ds(c * half, half)]
            theirs = outs[t].at[pl.ds((1 - c) * half, half)]
            res.append((_remote(mine, mine, send_sems.at[t], recv_sems.at[t], (x, y, 1 - c)),
                        _remote(theirs, theirs, send_sems.at[t], recv_sems.at[t], (x, y, 1 - c))))
        return res

    shapes = [jax.ShapeDtypeStruct(f.shape, F32) for f in fulls]
    return _exchange_rider(fulls, shapes, {t: t for t in range(len(fulls))}, (len(fulls),), pairs)


def _alone(name, rider):
    return _call(lambda: None, rider, name=name)()[1]


def _all_reduce_small(pack):
    rows = pack.shape[0]

    def body(p_ref, o_ref, buf, send_sems, recv_sems):
        x, y, c = _mesh_pos()
        me = 4 * x + 2 * y + c
        buf[0] = p_ref[...]
        copies = []
        for k in range(1, 8):
            peer = (x ^ (k >> 2), y ^ ((k >> 1) & 1), c ^ (k & 1))
            cp = pltpu.make_async_remote_copy(
                src_ref=p_ref, dst_ref=buf.at[k], send_sem=send_sems.at[k - 1], recv_sem=recv_sems.at[k - 1],
                device_id=peer, device_id_type=MESH)
            cp.start()
            copies.append(cp)
        for cp in copies:
            cp.wait()
        total = buf[me]
        for dev in range(1, 8):
            total = total + buf[me ^ dev]
        o_ref[...] = total

    return pl.pallas_call(
        body, name="all_reduce_small",
        in_specs=[VMEM_FULL], out_specs=VMEM_FULL,
        out_shape=jax.ShapeDtypeStruct(pack.shape, F32),
        scratch_shapes=[pltpu.VMEM((8, rows, LANES), F32),
                        pltpu.SemaphoreType.DMA((7,)), pltpu.SemaphoreType.DMA((7,))],
    )(pack)


def _row_block(rows):
    if rows <= 512:
        return rows
    for rb in (256, 352):
        if rows % rb == 0:
            return rb
    raise ValueError(f"no row block for {rows} rows")


def _place(name, w, pos, dtype):
    R, C = w.shape
    rb = _row_block(R)

    def body(pos_ref, w_ref, o_ref):
        del pos_ref
        o_ref[0] = w_ref[...].astype(dtype)

    return pl.pallas_call(
        body, name=name,
        grid_spec=pltpu.PrefetchScalarGridSpec(
            num_scalar_prefetch=1, grid=(R // rb,),
            in_specs=[pl.BlockSpec((rb, C), lambda r, p: (r, 0))],
            out_specs=pl.BlockSpec((1, rb, C), lambda r, p: (p[0], r, 0))),
        out_shape=jax.ShapeDtypeStruct((N_CHIPS, R, C), dtype),
        compiler_params=_params(32, 1),
    )(pos, w)


def _pair_sum(name, g, got, pos):
    S, R, C = g.shape
    half = R // 2
    rb = _row_block(half)
    nh = half // rb

    def body(pos_ref, a_ref, b_ref, o_ref):
        del pos_ref
        o_ref[...] = (a_ref[...] + b_ref[...]).astype(BF16)

    spec = pl.BlockSpec((1, rb, C), lambda s, r, p: (s, r, 0))
    return pl.pallas_call(
        body, name=name,
        grid_spec=pltpu.PrefetchScalarGridSpec(
            num_scalar_prefetch=1, grid=(S, nh),
            in_specs=[pl.BlockSpec((1, rb, C), lambda s, r, p: (s, p[1] * nh + r, 0)), spec],
            out_specs=spec),
        out_shape=jax.ShapeDtypeStruct((S, half, C), BF16), compiler_params=_params(32, 2),
    )(pos, g, got)


def _chip_sum(name, pairs, got, pos):
    _, half, C = pairs.shape
    rb = _row_block(half)
    nh = half // rb

    def body(pos_ref, a_ref, g_ref, o_ref):
        del pos_ref
        o_ref[...] = ((a_ref[0].astype(F32) + g_ref[0].astype(F32)) + g_ref[1].astype(F32)) + g_ref[2].astype(F32)

    return pl.pallas_call(
        body, name=name,
        grid_spec=pltpu.PrefetchScalarGridSpec(
            num_scalar_prefetch=1, grid=(nh,),
            in_specs=[pl.BlockSpec((1, rb, C), lambda r, p: (p[0], r, 0)),
                      pl.BlockSpec((3, rb, C), lambda r, p: (0, r, 0))],
            out_specs=pl.BlockSpec((rb, C), lambda r, p: (p[1] * nh + r, 0))),
        out_shape=jax.ShapeDtypeStruct((2 * half, C), F32), compiler_params=_params(32, 1),
    )(pos, pairs, got)


def _adamw(name, w, g, m, v):
    R, C = w.shape
    rb = _row_block(R)
    c1 = 1.0 - ADAM_B1 ** ADAM_STEP
    c2 = 1.0 - ADAM_B2 ** ADAM_STEP

    def body(w_ref, g_ref, m_ref, v_ref, d_ref, nm_ref, nv_ref):
        gv = g_ref[...]
        nm = ADAM_B1 * m_ref[...] + (1.0 - ADAM_B1) * gv
        nv = ADAM_B2 * v_ref[...] + (1.0 - ADAM_B2) * (gv * gv)
        nm_ref[...] = nm
        nv_ref[...] = nv
        d_ref[...] = -ADAM_LR * ((nm / c1) / (jnp.sqrt(nv / c2) + ADAM_EPS) + ADAM_WD * w_ref[...])

    spec = pl.BlockSpec((rb, C), lambda r: (r, 0))
    sds = jax.ShapeDtypeStruct(w.shape, F32)
    return pl.pallas_call(
        body, name=name, grid=(R // rb,), in_specs=[spec] * 4, out_specs=[spec] * 3,
        out_shape=[sds, sds, sds], compiler_params=_params(40, 1),
    )(w, g, m, v)


def _rel_index():
    m = np.arange(2 * ATT_BLK)
    off = np.where(m < ATT_BLK, m, m - 2 * ATT_BLK)
    rel = np.stack([ATT_BLK * d - off for d in range(N_ATT_TILES)])
    return np.clip(rel, -MAX_REL, MAX_REL) + MAX_REL


def _local_step(x, tgt, g1, w_in, cw, cb, lg, lb, rel, w_out, g2, g3, w_up, fw, fb, w_down, g4, pos=None):
    T = x.shape[0]
    dist = pos is not None
    idx = _rel_index()
    vec = jnp.transpose(rel[:, idx], (1, 0, 2)).reshape(N_ATT_TILES * N_HEADS, 1, 2 * ATT_BLK)
    bias = _bias_tiles(vec)

    (u, a, qkv), got = _fwd_in_proj(x, g1, w_in, _fetch_rider([w_out, w_down]) if dist else None)
    if dist:
        w_out, w_down = got
    (co, hc), got = _fwd_conv(a, cw, cb, lg, lb, _merge_riders(_forward_rider([w_out, w_down]),
                                                               _fetch_rider([w_up])) if dist else None)
    if dist:
        w_out, w_down, w_up = got
    (ao, lse), got = _fwd_attn(qkv, bias, _forward_rider([w_up]) if dist else None)
    if dist:
        (w_up,) = got
        w_out, w_down = w_out.reshape(D_MODEL, D_MODEL), w_down.reshape(D_FF, D_MODEL)
    mixed, h1, u2 = _fwd_out_proj(co, ao, w_out, x, g2, g3)
    hf, fp = _fwd_ffn(u2, w_up, fw, fb, w_down)
    loss, dy, df, dg4 = _fwd_loss(fp, h1, tgt, g4)

    tk = 512
    du2p, dhf, act, dfw_g, dfw_v = _bwd_ffn(df, hf, w_up, fw, fb, w_down)
    gw_up = _wgrad(
        "wgrad_up", [u2], pl.BlockSpec((tk, D_MODEL), lambda s, k: (k, 0)),
        dhf, pl.BlockSpec((1, tk, FF_SHARD), lambda s, k: (s // 2, k, s % 2)), None,
        pl.BlockSpec((1, D_MODEL, FF_SHARD), lambda s, k: (s, 0, 0)),
        jax.ShapeDtypeStruct((N_CHIPS, D_MODEL, FF_SHARD), F32), N_CHIPS, T, tk)
    gw_down = _wgrad(
        "wgrad_down", [act], pl.BlockSpec((tk, FF_SHARD), lambda s, k: (k, s)),
        df, pl.BlockSpec((tk, D_MODEL), lambda s, k: (k, 0)), None,
        pl.BlockSpec((FF_SHARD, D_MODEL), lambda s, k: (s, 0)),
        jax.ShapeDtypeStruct((D_FF, D_MODEL), F32), 2, T, tk).reshape(N_CHIPS, D_FF // N_CHIPS, D_MODEL)
    (dh1, dmx, dco, dao, dg3, dg2), got = _bwd_mid(
        du2p, dy, h1, mixed, g3, g2, w_out, _pair_exchange_rider([gw_up, gw_down]) if dist else None)
    gw_out = _wgrad(
        "wgrad_out", [co, ao], pl.BlockSpec((tk, CONV_W), lambda s, k: (k, 0)),
        dmx, pl.BlockSpec((tk, D_MODEL), lambda s, k: (k, 0)), None,
        pl.BlockSpec((CONV_W, D_MODEL), lambda s, k: (s, 0)),
        jax.ShapeDtypeStruct((D_MODEL, D_MODEL), F32), 2, T, tk,
        select=lambda s: s).reshape(N_CHIPS, D_MODEL // N_CHIPS, D_MODEL)
    if dist:
        p_up = _pair_sum("pair_sum_w_up", gw_up, got[0], pos)
        p_down = _pair_sum("pair_sum_w_down", gw_down, got[1], pos)
    (dproj, dsacc), got = _bwd_attn(
        qkv, ao, dao, lse, bias,
        _merge_riders(_chip_exchange_rider([p_up, p_down]), _pair_exchange_rider([gw_out])) if dist else None)
    if dist:
        gw_up = _chip_sum("chip_sum_w_up", p_up, got[0], pos)
        gw_down = _chip_sum("chip_sum_w_down", p_down, got[1], pos)
        p_out = _pair_sum("pair_sum_w_out", gw_out, got[2], pos)
    (dproj, dcw, dcb, dlg, dlb), got = _bwd_conv(
        dproj, a, dco, hc, cw, lg, lb,
        _merge_riders(_pair_gather_rider([gw_up, gw_down]), _chip_exchange_rider([p_out])) if dist else None)
    if dist:
        gw_up, gw_down = got[:2]
        gw_out = _chip_sum("chip_sum_w_out", p_out, got[2], pos)
    gw_in = _wgrad(
        "wgrad_in", [u], pl.BlockSpec((tk, D_MODEL), lambda s, k: (k, 0)),
        dproj, pl.BlockSpec((tk, IN_SHARD), lambda s, k: (k, s)), None,
        pl.BlockSpec((1, D_MODEL, IN_SHARD), lambda s, k: (s, 0, 0)),
        jax.ShapeDtypeStruct((N_CHIPS, D_MODEL, IN_SHARD), F32), N_CHIPS, T, tk)
    if dist:
        got = _alone("pair_exchange_w_in", _merge_riders(_pair_exchange_rider([gw_in]), _pair_gather_rider([gw_out])))
        p_in, gw_out = _pair_sum("pair_sum_w_in", gw_in, got[0], pos), got[1]
    (gx, dg1), got = _bwd_in_proj(dproj, w_in, x, dh1, g1, _chip_exchange_rider([p_in]) if dist else None)
    if dist:
        (gw_in,) = _alone("pair_gather_w_in", _pair_gather_rider([_chip_sum("chip_sum_w_in", p_in, got[0], pos)]))

    diag = _diag_sums(dsacc).reshape(N_ATT_TILES, N_HEADS, 2 * ATT_BLK)
    onehot = np.zeros((N_ATT_TILES, 2 * ATT_BLK, 2 * MAX_REL + 1), np.float32)
    for d in range(N_ATT_TILES):
        onehot[d, np.arange(2 * ATT_BLK), idx[d]] = 1.0
    drel = jnp.einsum("dhm,dmr->hr", diag, jnp.asarray(onehot), precision=lax.Precision.HIGHEST)

    small = dict(norm_mix_pre=dg1, conv_dw_w=dcw[:CONV_K], conv_dw_b=dcb, conv_ln_g=dlg, conv_ln_b=dlb,
                 rel_bias=drel, norm_mix_post=dg2, norm_ffn_pre=dg3,
                 ffn_dw_w=jnp.concatenate([dfw_g[0, :3], dfw_g[1, :3], dfw_v[0, :3], dfw_v[1, :3]], axis=1),
                 ffn_dw_b=jnp.concatenate([dfw_g[0, 3:4], dfw_g[1, 3:4], dfw_v[0, 3:4], dfw_v[1, 3:4]], axis=1),
                 norm_ffn_post=dg4)
    return loss, gx, small, dict(w_in=gw_in, w_out=gw_out, w_up=gw_up, w_down=gw_down)


SMALL_ORDER = ["norm_mix_pre", "conv_dw_b", "conv_ln_g", "conv_ln_b", "rel_bias", "norm_mix_post",
               "norm_ffn_pre", "ffn_dw_b", "norm_ffn_post", "conv_dw_w", "ffn_dw_w"]


def _pack(parts):
    rows = []
    for p in parts:
        width = -(-p.shape[1] // LANES) * LANES
        rows.append(jnp.pad(p, ((0, 0), (0, width - p.shape[1]))).reshape(-1, LANES))
    packed = jnp.concatenate(rows, axis=0)
    pad = -packed.shape[0] % 8
    return jnp.pad(packed, ((0, pad), (0, 0)))


def _unpack(packed, shapes):
    out, r = [], 0
    for shp in shapes:
        width = -(-shp[1] // LANES) * LANES
        n = shp[0] * width // LANES
        out.append(packed[r:r + n].reshape(shp[0], width)[:, :shp[1]])
        r += n
    return out


WEIGHTS = ["norm_mix_pre", "w_in", "conv_dw_w", "conv_dw_b", "conv_ln_g", "conv_ln_b", "rel_bias", "w_out",
           "norm_mix_post", "norm_ffn_pre", "w_up", "ffn_dw_w", "ffn_dw_b", "w_down", "norm_ffn_post"]
BIG = ["w_in", "w_out", "w_up", "w_down"]


def kernel(x, norm_mix_pre, w_in, conv_dw_w, conv_dw_b, conv_ln_g, conv_ln_b, rel_bias, w_out, norm_mix_post, norm_ffn_pre, w_up, ffn_dw_w, ffn_dw_b, w_down, norm_ffn_post, loss_target, m_norm_mix_pre, m_w_in, m_conv_dw_w, m_conv_dw_b, m_conv_ln_g, m_conv_ln_b, m_rel_bias, m_w_out, m_norm_mix_post, m_norm_ffn_pre, m_w_up, m_ffn_dw_w, m_ffn_dw_b, m_w_down, m_norm_ffn_post, v_norm_mix_pre, v_w_in, v_conv_dw_w, v_conv_dw_b, v_conv_ln_g, v_conv_ln_b, v_rel_bias, v_w_out, v_norm_mix_post, v_norm_ffn_pre, v_w_up, v_ffn_dw_w, v_ffn_dw_b, v_w_down, v_norm_ffn_post):
    args = locals()
    w = {n: args[n][0] for n in WEIGHTS}
    m = {n: args["m_" + n][0] for n in WEIGHTS}
    v = {n: args["v_" + n][0] for n in WEIGHTS}
    for d in (w, m, v):
        d["rel_bias"] = d["rel_bias"].reshape(N_HEADS, 2 * MAX_REL + 1)
        for n in ("norm_mix_pre", "conv_dw_b", "conv_ln_g", "conv_ln_b", "norm_mix_post", "norm_ffn_pre",
                  "ffn_dw_b", "norm_ffn_post"):
            d[n] = d[n].reshape(1, -1)
    shard = 2 * lax.axis_index("x") + lax.axis_index("y")

    cw_sh = jnp.pad(w["conv_dw_w"], ((0, CONV_HALO - CONV_K), (0, 0)))
    fw_sh = jnp.pad(w["ffn_dw_w"], ((0, FF_HALO - 3), (0, 0)))
    pos = jnp.stack([shard, lax.axis_index("c")]).astype(jnp.int32)
    bufs = {n: _place("place_" + n, w[n], pos, BF16) for n in BIG}
    w_in_f, cw_f, fw_f = _all_gather_weights(
        [bufs["w_in"], _place("place_conv_dw_w", cw_sh, pos, F32), _place("place_ffn_dw_w", fw_sh, pos, F32)])
    cw_full = jnp.transpose(cw_f, (1, 0, 2)).reshape(CONV_HALO, CONV_W)

    loss, gx, small, big = _local_step(
        x[0], loss_target[0], w["norm_mix_pre"], w_in_f, cw_full, w["conv_dw_b"], w["conv_ln_g"],
        w["conv_ln_b"], w["rel_bias"], bufs["w_out"], w["norm_mix_post"],
        w["norm_ffn_pre"], bufs["w_up"], fw_f, w["ffn_dw_b"].reshape(N_CHIPS, 1, FF_SHARD),
        bufs["w_down"], w["norm_ffn_post"], pos)
    grads, deltas, new_m, new_v = {}, {}, {}, {}
    for n in BIG:
        grads[n] = big[n]
        deltas[n], new_m[n], new_v[n] = _adamw("adamw_" + n, w[n], big[n], m[n], v[n])

    gsum = _all_reduce_small(_pack([small[n] for n in SMALL_ORDER]))
    shapes = [small[n].shape for n in SMALL_ORDER]
    gs = dict(zip(SMALL_ORDER, _unpack(gsum, shapes)))
    gs["conv_dw_w"] = lax.dynamic_slice_in_dim(gs["conv_dw_w"], shard * LANES, LANES, axis=1)
    gs["ffn_dw_w"] = lax.dynamic_slice_in_dim(gs["ffn_dw_w"], shard * FF_SHARD, FF_SHARD, axis=1)
    shapes = [gs[n].shape for n in SMALL_ORDER]
    d_p, m_p, v_p = _adamw("adamw_small", _pack([w[n] for n in SMALL_ORDER]), _pack([gs[n] for n in SMALL_ORDER]),
                           _pack([m[n] for n in SMALL_ORDER]), _pack([v[n] for n in SMALL_ORDER]))
    for dst, packed in ((deltas, d_p), (new_m, m_p), (new_v, v_p)):
        dst.update(zip(SMALL_ORDER, _unpack(packed, shapes)))
    grads.update(gs)

    total = lax.psum(loss[0, 0], ("x", "y", "c"))
    outs = [total, gx[None]]
    for group in (grads, deltas, new_m, new_v):
        outs += [group[n].reshape(args[n].shape) for n in WEIGHTS]
    return tuple(outs)
```

```python
import functools
import math
from typing import Callable, NamedTuple

import numpy as np
import jax
import jax.numpy as jnp
from jax import lax
from jax.experimental import pallas as pl
from jax.experimental.pallas import tpu as pltpu

F32 = jnp.float32
BF16 = jnp.bfloat16

D_MODEL = 1024
CONV_W = 512
ATTN_W = 512
N_HEADS = 8
HEAD_DIM = 64
CHUNK = 64
N_LEFT = 8
MAX_REL = 128
CONV_K = 31
CONV_HALO = 32
D_FF = 2816
FF_SHARD = 1408
IN_COLS = 2560
IN_SHARD = 640
EPS = 1e-6
NEG_INF = -1e30
ATT_BLK = 256
N_ATT_TILES = 3
LANES = 128
SUBLANES = 8
N_CHIPS = 4

ADAM_LR = 0.001
ADAM_B1 = 0.9
ADAM_B2 = 0.999
ADAM_EPS = 1e-08
ADAM_WD = 0.01
ADAM_STEP = 10

MESH = pl.DeviceIdType.MESH
ANY = pl.BlockSpec(memory_space=pl.ANY)
VMEM_FULL = pl.BlockSpec(memory_space=pltpu.VMEM)


def _params(vmem_mb, n_grid=0):
    sem = ("arbitrary",) * n_grid if n_grid else None
    return pltpu.CompilerParams(dimension_semantics=sem, vmem_limit_bytes=vmem_mb << 20)


class _Rider(NamedTuple):
    operands: list
    out_shape: list
    aliases: dict
    sems: list
    start: Callable
    finish: Callable


def _merge_riders(a, b):
    ia, oa, sa = len(a.operands), len(a.out_shape), len(a.sems)

    def start(ins, outs, sems):
        a.start(ins[:ia], outs[:oa], sems[:sa])
        b.start(ins[ia:], outs[oa:], sems[sa:])

    def finish(ins, outs, sems):
        a.finish(ins[:ia], outs[:oa], sems[:sa])
        b.finish(ins[ia:], outs[oa:], sems[sa:])

    aliases = {**a.aliases, **{k + ia: v + oa for k, v in b.aliases.items()}}
    return _Rider(a.operands + b.operands, a.out_shape + b.out_shape, aliases, a.sems + b.sems, start, finish)


def _call(body, rider, *, grid=(), in_specs=(), out_specs=(), out_shape=(), scratch_shapes=(),
          input_output_aliases=None, **kwargs):
    in_specs, out_specs, out_shape = list(in_specs), list(out_specs), list(out_shape)
    scratch, aliases = list(scratch_shapes), dict(input_output_aliases or {})
    if rider is None:
        plain = pl.pallas_call(body, grid=grid, in_specs=in_specs, out_specs=out_specs, out_shape=out_shape,
                               scratch_shapes=scratch, input_output_aliases=aliases, **kwargs)
        return lambda *args: (plain(*args), [])
    n_in, n_out, n_scr = len(in_specs), len(out_specs), len(scratch)
    r_in, r_out = len(rider.operands), len(rider.out_shape)

    def carried(*refs):
        ins, r_ins, refs = refs[:n_in], refs[n_in:n_in + r_in], refs[n_in + r_in:]
        outs, r_outs, refs = refs[:n_out], refs[n_out:n_out + r_out], refs[n_out + r_out:]
        scr, r_sems = refs[:n_scr], refs[n_scr:]
        if not grid:
            rider.start(r_ins, r_outs, r_sems)
            body(*ins, *outs, *scr)
            rider.finish(r_ins, r_outs, r_sems)
            return
        at = [pl.program_id(d) for d in range(len(grid))]
        first = functools.reduce(jnp.logical_and, [p == 0 for p in at])
        last = functools.reduce(jnp.logical_and, [p == n - 1 for p, n in zip(at, grid)])

        @pl.when(first)
        def _():
            rider.start(r_ins, r_outs, r_sems)

        body(*ins, *outs, *scr)

        @pl.when(last)
        def _():
            rider.finish(r_ins, r_outs, r_sems)

    aliases.update({n_in + k: n_out + v for k, v in rider.aliases.items()})
    both = pl.pallas_call(carried, grid=grid, in_specs=in_specs + [ANY] * r_in, out_specs=out_specs + [ANY] * r_out,
                          out_shape=out_shape + rider.out_shape, scratch_shapes=scratch + rider.sems,
                          input_output_aliases=aliases, **kwargs)

    def run(*args):
        res = both(*args, *rider.operands)
        return res[:n_out], res[n_out:]

    return run


def _sigmoid(v):
    return 1.0 / (1.0 + jnp.exp(-v))


def _dot(a, b):
    return jnp.dot(a, b, preferred_element_type=F32)


def _dot_nt(a, b):
    return lax.dot_general(a, b, (((1,), (1,)), ((), ())), preferred_element_type=F32)


def _dot_tn(a, b):
    return lax.dot_general(a, b, (((0,), (0,)), ((), ())), preferred_element_type=F32)


def _rms_fwd(v, g):
    r = lax.rsqrt(jnp.mean(v * v, axis=-1, keepdims=True) + EPS)
    return v * r * g, r


def _rms_bwd(dy, v, g):
    r = lax.rsqrt(jnp.mean(v * v, axis=-1, keepdims=True) + EPS)
    vh = v * r
    dvh = dy * g
    dv = r * (dvh - vh * jnp.mean(dvh * vh, axis=-1, keepdims=True))
    return dv, dy * vh


def _fwd_in_proj(x, g1, w_in, rider=None):
    T = x.shape[0]
    tm = 512

    def body(x_ref, g_ref, w_ref, u_ref, a_ref, qkv_ref):
        u, _ = _rms_fwd(x_ref[...], g_ref[...])
        u = u.astype(BF16)
        u_ref[...] = u
        for s in range(N_CHIPS):
            y = _dot(u, w_ref[s])
            lo, hi = IN_SHARD * s, IN_SHARD * (s + 1)
            if hi <= 1024:
                a_ref[:, lo:hi] = y
            elif lo >= 1024:
                qkv_ref[:, lo - 1024:hi - 1024] = y.astype(BF16)
            else:
                a_ref[:, lo:1024] = y[:, :1024 - lo]
                qkv_ref[:, 0:hi - 1024] = y[:, 1024 - lo:].astype(BF16)

    return _call(
        body, rider, name="fwd_in_proj", grid=(T // tm,),
        in_specs=[pl.BlockSpec((tm, D_MODEL), lambda i: (i, 0)),
                  pl.BlockSpec((1, D_MODEL), lambda i: (0, 0)),
                  pl.BlockSpec((N_CHIPS, D_MODEL, IN_SHARD), lambda i: (0, 0, 0))],
        out_specs=[pl.BlockSpec((tm, D_MODEL), lambda i: (i, 0)),
                   pl.BlockSpec((tm, 1024), lambda i: (i, 0)),
                   pl.BlockSpec((tm, 1536), lambda i: (i, 0))],
        out_shape=[jax.ShapeDtypeStruct((T, D_MODEL), BF16),
                   jax.ShapeDtypeStruct((T, 1024), F32),
                   jax.ShapeDtypeStruct((T, 1536), BF16)],
        compiler_params=_params(40, 1),
    )(x, g1, w_in)


def _fill_shifted(ext, shifted, tm):
    n = tm + CONV_HALO - SUBLANES
    for j in range(1, SUBLANES):
        shifted[j - 1] = ext[j:j + n, :]


def _shifted_rows(ext, shifted, start, rows):
    j = start % SUBLANES
    if j == 0:
        return ext[start:start + rows, :]
    return shifted[j - 1, start - j:start - j + rows, :]


def _fwd_conv(a, cw, cb, lg, lb, rider=None):
    T = a.shape[0]
    tm = 512
    rc = 64

    def body(a_ref, w_ref, b_ref, lg_ref, lb_ref, co_ref, hc_ref, hext, hsh):
        i = pl.program_id(0)

        @pl.when(i == 0)
        def _():
            hext[0:CONV_HALO, :] = jnp.zeros((CONV_HALO, CONV_W), F32)

        @pl.when(i > 0)
        def _():
            hext[0:CONV_HALO, :] = hext[tm:tm + CONV_HALO, :]

        hext[CONV_HALO:CONV_HALO + tm, :] = a_ref[:, :CONV_W] * _sigmoid(a_ref[:, CONV_W:])
        _fill_shifted(hext, hsh, tm)
        for c in range(tm // rc):
            acc = jnp.zeros((rc, CONV_W), F32)
            for k in range(CONV_K):
                acc = acc + w_ref[k:k + 1, :] * _shifted_rows(hext, hsh, c * rc + 2 + k, rc)
            hc = acc + b_ref[...]
            hc_ref[c * rc:(c + 1) * rc, :] = hc
            mu = jnp.mean(hc, axis=-1, keepdims=True)
            xc = hc - mu
            var = jnp.mean(xc * xc, axis=-1, keepdims=True)
            z = xc * lax.rsqrt(var + EPS) * lg_ref[...] + lb_ref[...]
            co_ref[c * rc:(c + 1) * rc, :] = (z * _sigmoid(z)).astype(BF16)

    return _call(
        body, rider, name="fwd_conv", grid=(T // tm,),
        in_specs=[pl.BlockSpec((tm, 1024), lambda i: (i, 0)),
                  pl.BlockSpec((CONV_HALO, CONV_W), lambda i: (0, 0)),
                  pl.BlockSpec((1, CONV_W), lambda i: (0, 0)),
                  pl.BlockSpec((1, CONV_W), lambda i: (0, 0)),
                  pl.BlockSpec((1, CONV_W), lambda i: (0, 0))],
        out_specs=[pl.BlockSpec((tm, CONV_W), lambda i: (i, 0)),
                   pl.BlockSpec((tm, CONV_W), lambda i: (i, 0))],
        out_shape=[jax.ShapeDtypeStruct((T, CONV_W), BF16),
                   jax.ShapeDtypeStruct((T, CONV_W), F32)],
        scratch_shapes=[pltpu.VMEM((tm + CONV_HALO, CONV_W), F32),
                        pltpu.VMEM((SUBLANES - 1, tm + CONV_HALO - SUBLANES, CONV_W), F32)],
        compiler_params=_params(40, 1),
    )(a, cw, cb, lg, lb)


def _row_skew(v, sign):
    rows, width = v.shape
    row = lax.broadcasted_iota(jnp.int32, (rows, 1), 0)
    for b in range(int(math.log2(rows))):
        shift = (1 << b) if sign > 0 else width - (1 << b)
        v = jnp.where(((row >> b) & 1) == 1, pltpu.roll(v, shift, 1), v)
    return v


def _att_visible(d):
    rq = lax.broadcasted_iota(jnp.int32, (ATT_BLK, ATT_BLK), 0) // CHUNK
    ck = lax.broadcasted_iota(jnp.int32, (ATT_BLK, ATT_BLK), 1) // CHUNK
    slack = ATT_BLK
    above = jnp.where(d == 0, 0, slack)
    below = jnp.where(d == 2, 0, slack)
    return (ck <= rq + above) & (ck >= rq - below)


def _bias_tiles(vec):
    def body(v_ref, o_ref):
        d = pl.program_id(0) // N_HEADS
        full = _row_skew(jnp.broadcast_to(v_ref[0], (ATT_BLK, 2 * ATT_BLK)), 1)
        o_ref[0] = jnp.where(_att_visible(d), full[:, :ATT_BLK], NEG_INF)

    return pl.pallas_call(
        body, name="bias_tiles", grid=(N_ATT_TILES * N_HEADS,),
        in_specs=[pl.BlockSpec((1, 1, 2 * ATT_BLK), lambda n: (n, 0, 0))],
        out_specs=pl.BlockSpec((1, ATT_BLK, ATT_BLK), lambda n: (n, 0, 0)),
        out_shape=jax.ShapeDtypeStruct((N_ATT_TILES * N_HEADS, ATT_BLK, ATT_BLK), F32),
        compiler_params=_params(16, 1),
    )(vec)


def _diag_sums(ds, rider=None):
    def body(d_ref, o_ref):
        wide = jnp.concatenate([d_ref[0], jnp.zeros((ATT_BLK, ATT_BLK), F32)], axis=1)
        o_ref[0] = jnp.sum(_row_skew(wide, -1), axis=0, keepdims=True)

    return _call(
        body, rider, name="diag_sums", grid=(N_ATT_TILES * N_HEADS,),
        in_specs=[pl.BlockSpec((1, ATT_BLK, ATT_BLK), lambda n: (n, 0, 0))],
        out_specs=[pl.BlockSpec((1, 1, 2 * ATT_BLK), lambda n: (n, 0, 0))],
        out_shape=[jax.ShapeDtypeStruct((N_ATT_TILES * N_HEADS, 1, 2 * ATT_BLK), F32)],
        compiler_params=_params(16, 1),
    )(ds)


def _head_mask(h):
    lane = lax.broadcasted_iota(jnp.int32, (1, LANES), 1)
    return (lane // HEAD_DIM) == (h % 2)


def _fwd_attn(qkv, bias, rider=None):
    T = qkv.shape[0]
    nb = T // ATT_BLK
    scale = HEAD_DIM ** -0.5

    def body(q_ref, k0_ref, k1_ref, k2_ref, v0_ref, v1_ref, v2_ref, b_ref, o_ref, lse_ref):
        i = pl.program_id(0)
        k_refs = (k0_ref, k1_ref, k2_ref)
        v_refs = (v0_ref, v1_ref, v2_ref)
        lane = lax.broadcasted_iota(jnp.int32, (1, LANES), 1)
        lse_tile = jnp.zeros((ATT_BLK, LANES), F32)
        for g in range(N_HEADS // 2):
            cols = slice(g * LANES, (g + 1) * LANES)
            qg = q_ref[:, cols]
            og = jnp.zeros((ATT_BLK, LANES), F32)
            for h in (2 * g, 2 * g + 1):
                hm = _head_mask(h)
                qh = jnp.where(hm, qg, jnp.zeros_like(qg))
                s = []
                for d in range(N_ATT_TILES):
                    sd = _dot_nt(qh, k_refs[d][:, cols]) * scale + b_ref[d * N_HEADS + h]
                    if d > 0:
                        sd = jnp.where(i >= d, sd, NEG_INF)
                    s.append(sd)
                m = jnp.maximum(jnp.maximum(jnp.max(s[0], axis=-1, keepdims=True),
                                            jnp.max(s[1], axis=-1, keepdims=True)),
                                jnp.max(s[2], axis=-1, keepdims=True))
                p = [jnp.exp(sd - m) for sd in s]
                l = (jnp.sum(p[0], axis=-1, keepdims=True) + jnp.sum(p[1], axis=-1, keepdims=True)
                     + jnp.sum(p[2], axis=-1, keepdims=True))
                oh = jnp.zeros((ATT_BLK, LANES), F32)
                for d in range(N_ATT_TILES):
                    vg = v_refs[d][:, cols]
                    oh = oh + _dot(p[d].astype(BF16), jnp.where(hm, vg, jnp.zeros_like(vg)))
                og = og + oh / l
                lse_tile = jnp.where(lane == h, m + jnp.log(l), lse_tile)
            o_ref[:, cols] = og.astype(BF16)
        lse_ref[...] = lse_tile

    def kv_spec(d, col):
        return pl.BlockSpec((ATT_BLK, ATTN_W), lambda i: (jnp.maximum(i - d, 0), col))

    return _call(
        body, rider, name="fwd_attn", grid=(nb,),
        in_specs=[pl.BlockSpec((ATT_BLK, ATTN_W), lambda i: (i, 0)),
                  kv_spec(0, 1), kv_spec(1, 1), kv_spec(2, 1),
                  kv_spec(0, 2), kv_spec(1, 2), kv_spec(2, 2),
                  pl.BlockSpec((N_ATT_TILES * N_HEADS, ATT_BLK, ATT_BLK), lambda i: (0, 0, 0))],
        out_specs=[pl.BlockSpec((ATT_BLK, ATTN_W), lambda i: (i, 0)),
                   pl.BlockSpec((ATT_BLK, LANES), lambda i: (i, 0))],
        out_shape=[jax.ShapeDtypeStruct((T, ATTN_W), BF16),
                   jax.ShapeDtypeStruct((T, LANES), F32)],
        compiler_params=_params(40, 1),
    )(qkv, qkv, qkv, qkv, qkv, qkv, qkv, bias)


def _fwd_out_proj(co, ao, w_out, x, g2, g3):
    T = x.shape[0]
    tm = 512

    def body(co_ref, ao_ref, w_ref, x_ref, g2_ref, g3_ref, mixed_ref, h1_ref, u2_ref):
        mixed = _dot(co_ref[...], w_ref[0:CONV_W, :]) + _dot(ao_ref[...], w_ref[CONV_W:, :])
        mixed_ref[...] = mixed
        y, _ = _rms_fwd(mixed, g2_ref[...])
        h1 = x_ref[...] + y
        h1_ref[...] = h1
        u2, _ = _rms_fwd(h1, g3_ref[...])
        u2_ref[...] = u2.astype(BF16)

    row = lambda w: pl.BlockSpec((tm, w), lambda i: (i, 0))
    vec = pl.BlockSpec((1, D_MODEL), lambda i: (0, 0))
    return pl.pallas_call(
        body, name="fwd_out_proj", grid=(T // tm,),
        in_specs=[row(CONV_W), row(ATTN_W), pl.BlockSpec((D_MODEL, D_MODEL), lambda i: (0, 0)),
                  row(D_MODEL), vec, vec],
        out_specs=[row(D_MODEL), row(D_MODEL), row(D_MODEL)],
        out_shape=[jax.ShapeDtypeStruct((T, D_MODEL), F32),
                   jax.ShapeDtypeStruct((T, D_MODEL), F32),
                   jax.ShapeDtypeStruct((T, D_MODEL), BF16)],
        compiler_params=_params(40, 1),
    )(co, ao, w_out, x, g2, g3)


GELU_C = math.sqrt(2.0 / math.pi)
GELU_A = 0.044715


def _gelu_and_grad(v):
    th = jnp.tanh(GELU_C * (v + GELU_A * v * v * v))
    gl = 0.5 * v * (1.0 + th)
    dgl = 0.5 * (1.0 + th) + 0.5 * v * (1.0 - th * th) * (GELU_C * (1.0 + 3.0 * GELU_A * v * v))
    return gl, dgl


FF_TM = 256
FF_HALO = 16
FF_CHUNKS = [(lo, min(lo + 256, FF_SHARD)) for lo in range(0, FF_SHARD, 256)]


def _rows_before(prev, cur):
    ext = jnp.concatenate([prev, cur], axis=0)
    return pltpu.roll(ext, 1, 0)[SUBLANES:], pltpu.roll(ext, 2, 0)[SUBLANES:]


def _rows_after(cur, nxt):
    ext = jnp.concatenate([cur, nxt], axis=0)
    n = ext.shape[0]
    return pltpu.roll(ext, n - 1, 0)[:cur.shape[0]], pltpu.roll(ext, n - 2, 0)[:cur.shape[0]]


def _fwd_ffn(u2, w_up, fw, fb, w_down):
    T = u2.shape[0]
    tm = FF_TM

    def body(u_ref, wg_ref, wv_ref, fwg_ref, fwv_ref, fbg_ref, fbv_ref, wd_ref, hf_ref, f_ref, carg, carv):
        i = pl.program_id(1)

        @pl.when(i == 0)
        def _():
            carg[...] = jnp.zeros(carg.shape, F32)
            carv[...] = jnp.zeros(carv.shape, F32)

        u = u_ref[...]
        f = None
        up = lambda lo, hi: (_dot(u, wg_ref[0, :, lo:hi]), _dot(u, wv_ref[0, :, lo:hi]))
        ahead = up(*FF_CHUNKS[0])
        for c, (lo, hi) in enumerate(FF_CHUNKS):
            conv = []
            hs = ahead
            if c + 1 < len(FF_CHUNKS):
                ahead = up(*FF_CHUNKS[c + 1])
            for n, (car, fw_ref, fb_ref) in enumerate(((carg, fwg_ref, fbg_ref), (carv, fwv_ref, fbv_ref))):
                h0 = hs[n]
                hf_ref[n, :, lo:hi] = h0.astype(BF16)
                h1, h2 = _rows_before(car[:, lo:hi], h0)
                car[:, lo:hi] = h0[tm - SUBLANES:, :]
                conv.append(fw_ref[0, 0:1, lo:hi] * h2 + fw_ref[0, 1:2, lo:hi] * h1
                            + fw_ref[0, 2:3, lo:hi] * h0 + fb_ref[0, :, lo:hi])
            gl, _ = _gelu_and_grad(conv[0])
            term = _dot((gl * conv[1]).astype(BF16), wd_ref[lo:hi, :])
            f = term if f is None else f + term
        f_ref[0] = f

    wspec = lambda off: pl.BlockSpec((1, D_MODEL, FF_SHARD), lambda s, i: (s + off, 0, 0))
    fwspec = lambda off: pl.BlockSpec((1, FF_HALO, FF_SHARD), lambda s, i: (s + off, 0, 0))
    fbspec = lambda off: pl.BlockSpec((1, 1, FF_SHARD), lambda s, i: (s + off, 0, 0))
    return pl.pallas_call(
        body, name="fwd_ffn", grid=(2, T // tm),
        in_specs=[pl.BlockSpec((tm, D_MODEL), lambda s, i: (i, 0)),
                  wspec(0), wspec(2), fwspec(0), fwspec(2), fbspec(0), fbspec(2),
                  pl.BlockSpec((FF_SHARD, D_MODEL), lambda s, i: (s, 0))],
        out_specs=[pl.BlockSpec((2, tm, FF_SHARD), lambda s, i: (0, i, s)),
                   pl.BlockSpec((1, tm, D_MODEL), lambda s, i: (s, i, 0))],
        out_shape=[jax.ShapeDtypeStruct((2, T, D_FF), BF16),
                   jax.ShapeDtypeStruct((2, T, D_MODEL), F32)],
        scratch_shapes=[pltpu.VMEM((SUBLANES, FF_SHARD), F32), pltpu.VMEM((SUBLANES, FF_SHARD), F32)],
        compiler_params=_params(48, 2),
    )(u2, w_up, w_up, fw, fw, fb, fb, w_down)


def _fwd_loss(fp, h1, tgt, g4):
    T = h1.shape[0]
    tm = 512

    def body(fp_ref, h1_ref, t_ref, g_ref, loss_ref, dy_ref, df_ref, dg_ref):
        i = pl.program_id(0)
        f = fp_ref[0] + fp_ref[1]
        r, _ = _rms_fwd(f, g_ref[...])
        e = (h1_ref[...] + r) - t_ref[...]
        dy = e * (1.0 / D_MODEL)
        dy_ref[...] = dy
        df, dg_rows = _rms_bwd(dy, f, g_ref[...])
        df_ref[...] = df.astype(BF16)
        part = 0.5 * jnp.sum(jnp.mean(e * e, axis=-1, keepdims=True), axis=0, keepdims=True)
        dg = jnp.sum(dg_rows, axis=0, keepdims=True)

        @pl.when(i == 0)
        def _():
            loss_ref[...] = part
            dg_ref[...] = dg

        @pl.when(i > 0)
        def _():
            loss_ref[...] += part
            dg_ref[...] += dg

    row = pl.BlockSpec((tm, D_MODEL), lambda i: (i, 0))
    vec = pl.BlockSpec((1, D_MODEL), lambda i: (0, 0))
    return pl.pallas_call(
        body, name="fwd_loss", grid=(T // tm,),
        in_specs=[pl.BlockSpec((2, tm, D_MODEL), lambda i: (0, i, 0)), row, row, vec],
        out_specs=[pl.BlockSpec((1, 1), lambda i: (0, 0)), row, row, vec],
        out_shape=[jax.ShapeDtypeStruct((1, 1), F32),
                   jax.ShapeDtypeStruct((T, D_MODEL), F32),
                   jax.ShapeDtypeStruct((T, D_MODEL), BF16),
                   jax.ShapeDtypeStruct((1, D_MODEL), F32)],
        compiler_params=_params(40, 1),
    )(fp, h1, tgt, g4)


def _bwd_ffn(df, hf, w_up, fw, fb, w_down):
    T = df.shape[0]
    tm = FF_TM
    ni = T // tm

    def body(df_ref, hf_ref, halo_ref, wd_ref, wg_ref, wv_ref, fwg_ref, fwv_ref, fbg_ref, fbv_ref,
             du_ref, dhf_ref, act_ref, dwg_ref, dwv_ref, carg, carv):
        i = pl.program_id(1)
        ri = ni - 1 - i

        @pl.when(i == 0)
        def _():
            dwg_ref[...] = jnp.zeros(dwg_ref.shape, F32)
            dwv_ref[...] = jnp.zeros(dwv_ref.shape, F32)
            carg[...] = jnp.zeros(carg.shape, F32)
            carv[...] = jnp.zeros(carv.shape, F32)

        df = df_ref[...]
        du = None
        down = lambda lo, hi: _dot_nt(df, wd_ref[lo:hi, :])
        ahead = down(*FF_CHUNKS[0])
        for c, (lo, hi) in enumerate(FF_CHUNKS):
            dact = ahead
            if c + 1 < len(FF_CHUNKS):
                ahead = down(*FF_CHUNKS[c + 1])
            hs, pre = [], []
            for n, (fw_ref, fb_ref) in enumerate(((fwg_ref, fbg_ref), (fwv_ref, fbv_ref))):
                h0 = hf_ref[n, :, lo:hi].astype(F32)
                halo = jnp.where(ri > 0, halo_ref[n, :, lo:hi].astype(F32)[FF_HALO - SUBLANES:], 0.0)
                h1, h2 = _rows_before(halo, h0)
                hs.append((h2, h1, h0))
                pre.append(fw_ref[0, 0:1, lo:hi] * h2 + fw_ref[0, 1:2, lo:hi] * h1
                           + fw_ref[0, 2:3, lo:hi] * h0 + fb_ref[0, :, lo:hi])
            gl, dgl = _gelu_and_grad(pre[0])
            act_ref[:, lo:hi] = (gl * pre[1]).astype(BF16)
            dpre = (dact * pre[1] * dgl, dact * gl)
            for n, (car, fw_ref, dw_ref, w_ref) in enumerate(
                    ((carg, fwg_ref, dwg_ref, wg_ref), (carv, fwv_ref, dwv_ref, wv_ref))):
                dp = dpre[n]
                for k in range(3):
                    dw_ref[0, k:k + 1, lo:hi] += jnp.sum(dp * hs[n][k], axis=0, keepdims=True)
                dw_ref[0, 3:4, lo:hi] += jnp.sum(dp, axis=0, keepdims=True)
                up1, up2 = _rows_after(dp, car[:, lo:hi])
                car[:, lo:hi] = dp[0:SUBLANES, :]
                dh = (fw_ref[0, 2:3, lo:hi] * dp + fw_ref[0, 1:2, lo:hi] * up1
                      + fw_ref[0, 0:1, lo:hi] * up2).astype(BF16)
                dhf_ref[n, :, lo:hi] = dh
                term = _dot_nt(dh, w_ref[0, :, lo:hi])
                du = term if du is None else du + term
        du_ref[0] = du

    rev = lambda i: ni - 1 - i
    wspec = lambda off: pl.BlockSpec((1, D_MODEL, FF_SHARD), lambda s, i: (s + off, 0, 0))
    fwspec = lambda off: pl.BlockSpec((1, FF_HALO, FF_SHARD), lambda s, i: (s + off, 0, 0))
    fbspec = lambda off: pl.BlockSpec((1, 1, FF_SHARD), lambda s, i: (s + off, 0, 0))
    halo_blocks = tm // FF_HALO
    dwspec = pl.BlockSpec((1, FF_HALO, FF_SHARD), lambda s, i: (s, 0, 0))
    return pl.pallas_call(
        body, name="bwd_ffn", grid=(2, ni),
        in_specs=[pl.BlockSpec((tm, D_MODEL), lambda s, i: (rev(i), 0)),
                  pl.BlockSpec((2, tm, FF_SHARD), lambda s, i: (0, rev(i), s)),
                  pl.BlockSpec((2, FF_HALO, FF_SHARD),
                               lambda s, i: (0, jnp.maximum(rev(i) * halo_blocks - 1, 0), s)),
                  pl.BlockSpec((FF_SHARD, D_MODEL), lambda s, i: (s, 0)),
                  wspec(0), wspec(2), fwspec(0), fwspec(2), fbspec(0), fbspec(2)],
        out_specs=[pl.BlockSpec((1, tm, D_MODEL), lambda s, i: (s, rev(i), 0)),
                   pl.BlockSpec((2, tm, FF_SHARD), lambda s, i: (0, rev(i), s)),
                   pl.BlockSpec((tm, FF_SHARD), lambda s, i: (rev(i), s)),
                   dwspec, dwspec],
        out_shape=[jax.ShapeDtypeStruct((2, T, D_MODEL), F32),
                   jax.ShapeDtypeStruct((2, T, D_FF), BF16),
                   jax.ShapeDtypeStruct((T, D_FF), BF16),
                   jax.ShapeDtypeStruct((2, FF_HALO, FF_SHARD), F32),
                   jax.ShapeDtypeStruct((2, FF_HALO, FF_SHARD), F32)],
        scratch_shapes=[pltpu.VMEM((SUBLANES, FF_SHARD), F32), pltpu.VMEM((SUBLANES, FF_SHARD), F32)],
        compiler_params=_params(56, 2),
    )(df, hf, hf, w_down, w_up, w_up, fw, fw, fb, fb)


def _bwd_mid(du2p, dy, h1, mixed, g3, g2, w_out, rider=None):
    T = dy.shape[0]
    tm = 512

    def body(du_ref, dy_ref, h1_ref, mx_ref, g3_ref, g2_ref, w_ref,
             dh1_ref, dmx_ref, dco_ref, dao_ref, dg3_ref, dg2_ref):
        i = pl.program_id(0)
        dres, dg3_rows = _rms_bwd(du_ref[0] + du_ref[1], h1_ref[...], g3_ref[...])
        dh1 = dy_ref[...] + dres
        dh1_ref[...] = dh1
        dmx, dg2_rows = _rms_bwd(dh1, mx_ref[...], g2_ref[...])
        dmx = dmx.astype(BF16)
        dmx_ref[...] = dmx
        dcat = _dot_nt(dmx, w_ref[...])
        dco_ref[...] = dcat[:, :CONV_W]
        dao_ref[...] = dcat[:, CONV_W:].astype(BF16)
        dg3 = jnp.sum(dg3_rows, axis=0, keepdims=True)
        dg2 = jnp.sum(dg2_rows, axis=0, keepdims=True)

        @pl.when(i == 0)
        def _():
            dg3_ref[...] = dg3
            dg2_ref[...] = dg2

        @pl.when(i > 0)
        def _():
            dg3_ref[...] += dg3
            dg2_ref[...] += dg2

    row = lambda w: pl.BlockSpec((tm, w), lambda i: (i, 0))
    vec = pl.BlockSpec((1, D_MODEL), lambda i: (0, 0))
    return _call(
        body, rider, name="bwd_mid", grid=(T // tm,),
        in_specs=[pl.BlockSpec((2, tm, D_MODEL), lambda i: (0, i, 0)), row(D_MODEL), row(D_MODEL),
                  row(D_MODEL), vec, vec, pl.BlockSpec((D_MODEL, D_MODEL), lambda i: (0, 0))],
        out_specs=[row(D_MODEL), row(D_MODEL), row(CONV_W), row(ATTN_W), vec, vec],
        out_shape=[jax.ShapeDtypeStruct((T, D_MODEL), F32),
                   jax.ShapeDtypeStruct((T, D_MODEL), BF16),
                   jax.ShapeDtypeStruct((T, CONV_W), F32),
                   jax.ShapeDtypeStruct((T, ATTN_W), BF16),
                   jax.ShapeDtypeStruct((1, D_MODEL), F32),
                   jax.ShapeDtypeStruct((1, D_MODEL), F32)],
        compiler_params=_params(48, 1),
    )(du2p, dy, h1, mixed, g3, g2, w_out)


def _bwd_attn(qkv, ao, dao, lse, bias, rider=None):
    T = qkv.shape[0]
    nb = T // ATT_BLK
    scale = HEAD_DIM ** -0.5

    def body(k_ref, v_ref, q0, q1, q2, do0, do1, do2, o0, o1, o2, l0, l1, l2, b_ref,
             dp_ref, ds_ref, acc1, acc2):
        j = pl.program_id(0)
        q_refs, do_refs, o_refs, l_refs = (q0, q1, q2), (do0, do1, do2), (o0, o1, o2), (l0, l1, l2)

        @pl.when(j == 0)
        def _():
            ds_ref[...] = jnp.zeros(ds_ref.shape, F32)
            acc1[...] = jnp.zeros(acc1.shape, F32)
            acc2[...] = jnp.zeros(acc2.shape, F32)

        dq_new = [[], [], []]
        dk_cols, dv_cols = [], []
        for g in range(N_HEADS // 2):
            cols = slice(g * LANES, (g + 1) * LANES)
            kg = k_ref[:, cols]
            vg = v_ref[:, cols]
            dkg = jnp.zeros((ATT_BLK, LANES), F32)
            dvg = jnp.zeros((ATT_BLK, LANES), F32)
            dqg = [jnp.zeros((ATT_BLK, LANES), F32) for _ in range(N_ATT_TILES)]
            for d in range(N_ATT_TILES):
                qg = q_refs[d][:, cols]
                dog = do_refs[d][:, cols]
                prod = dog.astype(F32) * o_refs[d][:, cols].astype(F32)
                for h in (2 * g, 2 * g + 1):
                    hm = _head_mask(h)
                    qh = jnp.where(hm, qg, jnp.zeros_like(qg))
                    doh = jnp.where(hm, dog, jnp.zeros_like(dog))
                    kh = jnp.where(hm, kg, jnp.zeros_like(kg))
                    delta = jnp.sum(jnp.where(hm, prod, 0.0), axis=-1, keepdims=True)
                    s = _dot_nt(qh, kg) * scale + b_ref[d * N_HEADS + h]
                    p = jnp.exp(s - l_refs[d][:, h:h + 1])
                    p = jnp.where(j + d < nb, p, 0.0)
                    dvg = dvg + _dot_tn(p.astype(BF16), doh)
                    dpm = _dot_nt(doh, vg)
                    dsc = p * (dpm - delta)
                    ds_ref[d * N_HEADS + h] += dsc
                    dsb = (dsc * scale).astype(BF16)
                    dqg[d] = dqg[d] + _dot(dsb, kh)
                    dkg = dkg + _dot_tn(dsb, qh)
            for d in range(N_ATT_TILES):
                dq_new[d].append(dqg[d])
            dk_cols.append(dkg)
            dv_cols.append(dvg)
        x0, x1, x2 = (jnp.concatenate(c, axis=1) for c in dq_new)
        dp_ref[:, 0:1024] = jnp.zeros((ATT_BLK, 1024), BF16)
        dp_ref[:, 1024:1536] = (acc1[...] + x0).astype(BF16)
        dp_ref[:, 1536:2048] = jnp.concatenate(dk_cols, axis=1).astype(BF16)
        dp_ref[:, 2048:2560] = jnp.concatenate(dv_cols, axis=1).astype(BF16)
        acc1[...] = acc2[...] + x1
        acc2[...] = x2

    def fwd_spec(d, width, col):
        return pl.BlockSpec((ATT_BLK, width), lambda j: (jnp.minimum(j + d, nb - 1), col))

    return _call(
        body, rider, name="bwd_attn", grid=(nb,),
        in_specs=[pl.BlockSpec((ATT_BLK, ATTN_W), lambda j: (j, 1)),
                  pl.BlockSpec((ATT_BLK, ATTN_W), lambda j: (j, 2)),
                  fwd_spec(0, ATTN_W, 0), fwd_spec(1, ATTN_W, 0), fwd_spec(2, ATTN_W, 0),
                  fwd_spec(0, ATTN_W, 0), fwd_spec(1, ATTN_W, 0), fwd_spec(2, ATTN_W, 0),
                  fwd_spec(0, ATTN_W, 0), fwd_spec(1, ATTN_W, 0), fwd_spec(2, ATTN_W, 0),
                  fwd_spec(0, LANES, 0), fwd_spec(1, LANES, 0), fwd_spec(2, LANES, 0),
                  pl.BlockSpec((N_ATT_TILES * N_HEADS, ATT_BLK, ATT_BLK), lambda j: (0, 0, 0))],
        out_specs=[pl.BlockSpec((ATT_BLK, IN_COLS), lambda j: (j, 0)),
                   pl.BlockSpec((N_ATT_TILES * N_HEADS, ATT_BLK, ATT_BLK), lambda j: (0, 0, 0))],
        out_shape=[jax.ShapeDtypeStruct((T, IN_COLS), BF16),
                   jax.ShapeDtypeStruct((N_ATT_TILES * N_HEADS, ATT_BLK, ATT_BLK), F32)],
        scratch_shapes=[pltpu.VMEM((ATT_BLK, ATTN_W), F32), pltpu.VMEM((ATT_BLK, ATTN_W), F32)],
        compiler_params=_params(56, 1),
    )(qkv, qkv, qkv, qkv, qkv, dao, dao, dao, ao, ao, ao, lse, lse, lse, bias)


def _bwd_conv(dproj, a, dco, hc, cw, lg, lb, rider=None):
    T = a.shape[0]
    tm = 512
    rc = 32
    ni = T // tm
    hb = tm // CONV_HALO

    def body(dp_in, a_ref, ap_ref, dco_ref, dcon_ref, hc_ref, hcn_ref, w_ref, lg_ref, lb_ref,
             dp_ref, dw_ref, db_ref, dlg_ref, dlb_ref, hext, dext, hsh, dsh, dwacc):
        del dp_in
        i = pl.program_id(0)

        def ln_bwd(dco_v, hc_v):
            mu = jnp.mean(hc_v, axis=-1, keepdims=True)
            xc = hc_v - mu
            rstd = lax.rsqrt(jnp.mean(xc * xc, axis=-1, keepdims=True) + EPS)
            xh = xc * rstd
            z = xh * lg_ref[...] + lb_ref[...]
            sg = _sigmoid(z)
            dz = dco_v * (sg * (1.0 + z * (1.0 - sg)))
            dxh = dz * lg_ref[...]
            dhc = rstd * (dxh - jnp.mean(dxh, axis=-1, keepdims=True)
                          - xh * jnp.mean(dxh * xh, axis=-1, keepdims=True))
            return dhc, dz * xh, dz

        hext[0:CONV_HALO, :] = jnp.where(i > 0, ap_ref[:, :CONV_W] * _sigmoid(ap_ref[:, CONV_W:]), 0.0)
        hext[CONV_HALO:CONV_HALO + tm, :] = a_ref[:, :CONV_W] * _sigmoid(a_ref[:, CONV_W:])
        dhc, dlg_rows, dlb_rows = ln_bwd(dco_ref[...], hc_ref[...])
        dext[0:tm, :] = dhc
        dhc_next, _, _ = ln_bwd(dcon_ref[...], hcn_ref[...])
        dext[tm:tm + CONV_HALO, :] = jnp.where(i < ni - 1, dhc_next, 0.0)

        @pl.when(i == 0)
        def _():
            dw_ref[...] = jnp.zeros(dw_ref.shape, F32)
            db_ref[...] = jnp.zeros(db_ref.shape, F32)
            dlg_ref[...] = jnp.zeros(dlg_ref.shape, F32)
            dlb_ref[...] = jnp.zeros(dlb_ref.shape, F32)

            dwacc[...] = jnp.zeros(dwacc.shape, F32)

        db_ref[...] += jnp.sum(dhc, axis=0, keepdims=True)
        dlg_ref[...] += jnp.sum(dlg_rows, axis=0, keepdims=True)
        dlb_ref[...] += jnp.sum(dlb_rows, axis=0, keepdims=True)
        _fill_shifted(hext, hsh, tm)
        _fill_shifted(dext, dsh, tm)
        for c in range(tm // rc):
            r0 = c * rc
            dh = jnp.zeros((rc, CONV_W), F32)
            dhc_c = dext[r0:r0 + rc, :]
            for k in range(CONV_K):
                dh = dh + w_ref[k:k + 1, :] * _shifted_rows(dext, dsh, r0 + 30 - k, rc)
                prod = dhc_c * _shifted_rows(hext, hsh, r0 + 2 + k, rc)
                dwacc[k] += jnp.sum(prod.reshape(rc // SUBLANES, SUBLANES, CONV_W), axis=0)
            av = a_ref[r0:r0 + rc, :CONV_W]
            sg = _sigmoid(a_ref[r0:r0 + rc, CONV_W:])
            dp_ref[r0:r0 + rc, 0:CONV_W] = (dh * sg).astype(BF16)
            dp_ref[r0:r0 + rc, CONV_W:] = (dh * av * sg * (1.0 - sg)).astype(BF16)

        @pl.when(i == ni - 1)
        def _():
            dw_ref[...] = jnp.sum(dwacc[...], axis=1)

    row = lambda w: pl.BlockSpec((tm, w), lambda i: (i, 0))
    prev = lambda w: pl.BlockSpec((CONV_HALO, w), lambda i: (jnp.maximum(i * hb - 1, 0), 0))
    nxt = lambda w: pl.BlockSpec((CONV_HALO, w), lambda i: (jnp.minimum((i + 1) * hb, ni * hb - 1), 0))
    vec = pl.BlockSpec((1, CONV_W), lambda i: (0, 0))
    return _call(
        body, rider, name="bwd_conv", grid=(ni,),
        in_specs=[ANY, row(1024), prev(1024), row(CONV_W), nxt(CONV_W), row(CONV_W), nxt(CONV_W),
                  pl.BlockSpec((CONV_HALO, CONV_W), lambda i: (0, 0)), vec, vec],
        out_specs=[pl.BlockSpec((tm, 1024), lambda i: (i, 0)),
                   pl.BlockSpec((CONV_HALO, CONV_W), lambda i: (0, 0)), vec, vec, vec],
        out_shape=[jax.ShapeDtypeStruct((T, IN_COLS), BF16),
                   jax.ShapeDtypeStruct((CONV_HALO, CONV_W), F32),
                   jax.ShapeDtypeStruct((1, CONV_W), F32),
                   jax.ShapeDtypeStruct((1, CONV_W), F32),
                   jax.ShapeDtypeStruct((1, CONV_W), F32)],
        scratch_shapes=[pltpu.VMEM((tm + CONV_HALO, CONV_W), F32), pltpu.VMEM((tm + CONV_HALO, CONV_W), F32),
                        pltpu.VMEM((SUBLANES - 1, tm + CONV_HALO - SUBLANES, CONV_W), F32),
                        pltpu.VMEM((SUBLANES - 1, tm + CONV_HALO - SUBLANES, CONV_W), F32),
                        pltpu.VMEM((CONV_HALO, SUBLANES, CONV_W), F32)],
        input_output_aliases={0: 0},
        compiler_params=_params(56, 1),
    )(dproj, a, a, dco, dco, hc, hc, cw, lg, lb)


def _bwd_in_proj(dproj, w_in, x, dh1, g1, rider=None):
    T = x.shape[0]
    tm = 512

    def body(dp_ref, w_ref, x_ref, dh_ref, g_ref, gx_ref, dg_ref):
        i = pl.program_id(0)
        du = None
        for s in range(N_CHIPS):
            term = _dot_nt(dp_ref[:, IN_SHARD * s:IN_SHARD * (s + 1)], w_ref[s])
            du = term if du is None else du + term
        dx, dg_rows = _rms_bwd(du, x_ref[...], g_ref[...])
        gx_ref[...] = dh_ref[...] + dx
        dg = jnp.sum(dg_rows, axis=0, keepdims=True)

        @pl.when(i == 0)
        def _():
            dg_ref[...] = dg

        @pl.when(i > 0)
        def _():
            dg_ref[...] += dg

    row = lambda w: pl.BlockSpec((tm, w), lambda i: (i, 0))
    vec = pl.BlockSpec((1, D_MODEL), lambda i: (0, 0))
    return _call(
        body, rider, name="bwd_in_proj", grid=(T // tm,),
        in_specs=[row(IN_COLS), pl.BlockSpec((N_CHIPS, D_MODEL, IN_SHARD), lambda i: (0, 0, 0)),
                  row(D_MODEL), row(D_MODEL), vec],
        out_specs=[row(D_MODEL), vec],
        out_shape=[jax.ShapeDtypeStruct((T, D_MODEL), F32), jax.ShapeDtypeStruct((1, D_MODEL), F32)],
        compiler_params=_params(40, 1),
    )(dproj, w_in, x, dh1, g1)


def _wgrad(name, a_list, a_spec, b, b_spec, out_spec, out_shape, n_outer, T, tk, select=None, rider=None):
    def body(*refs):
        a_refs, b_ref, o_ref = refs[:len(a_list)], refs[len(a_list)], refs[len(a_list) + 1]
        kt = pl.program_id(1)

        @pl.when(kt == 0)
        def _():
            o_ref[...] = jnp.zeros(o_ref.shape, F32)

        bv = b_ref[...].reshape(b_ref.shape[-2:])
        if select is None:
            o_ref[...] += _dot_tn(a_refs[0][...].reshape(a_refs[0].shape[-2:]), bv).reshape(o_ref.shape)
        else:
            for n, a_ref in enumerate(a_refs):
                @pl.when(select(pl.program_id(0)) == n)
                def _():
                    o_ref[...] += _dot_tn(a_ref[...], bv).reshape(o_ref.shape)

    (res,), got = _call(
        body, rider, name=name, grid=(n_outer, T // tk),
        in_specs=[a_spec] * len(a_list) + [b_spec],
        out_specs=[out_spec], out_shape=[out_shape],
        compiler_params=_params(48, 2),
    )(*a_list, b)
    return (res, got) if rider is not None else res


def _mesh_pos():
    return lax.axis_index("x"), lax.axis_index("y"), lax.axis_index("c")


def _other_chips(x, y):
    return [((1 - x, y), 2 * (1 - x) + y), ((x, 1 - y), 2 * x + (1 - y)), ((1 - x, 1 - y), 2 * (1 - x) + (1 - y))]


def _all_gather_weights(shards):
    n = len(shards)

    def body(*refs):
        out_refs = refs[n:2 * n]
        send_sems, recv_sems = refs[2 * n:]
        x, y, c = _mesh_pos()
        me = 2 * x + y
        sibling = (x, y, 1 - c)
        chips = _other_chips(x, y)
        first, passed = [], []
        for t in range(n):
            half = shards[t].shape[1] // 2
            mine = out_refs[t].at[me, pl.ds(c * half, half)]
            for k, (chip, _) in enumerate(chips):
                cp = pltpu.make_async_remote_copy(
                    src_ref=mine, dst_ref=mine, send_sem=send_sems.at[t, k], recv_sem=recv_sems.at[t, k],
                    device_id=(*chip, c), device_id_type=MESH)
                cp.start()
                first.append(cp)
        for t in range(n):
            half = shards[t].shape[1] // 2
            rows = pl.ds(c * half, half)
            for k, (chip, s) in enumerate(chips):
                landed = out_refs[t].at[s, rows]
                pltpu.make_async_remote_copy(
                    src_ref=landed, dst_ref=landed, send_sem=send_sems.at[t, k], recv_sem=recv_sems.at[t, k],
                    device_id=(*chip, c), device_id_type=MESH).wait_recv()
                cp = pltpu.make_async_remote_copy(
                    src_ref=landed, dst_ref=landed, send_sem=send_sems.at[t, 3 + k], recv_sem=recv_sems.at[t, 3 + k],
                    device_id=sibling, device_id_type=MESH)
                cp.start()
                passed.append(cp)
        for t in range(n):
            half = shards[t].shape[1] // 2
            other = pl.ds((1 - c) * half, half)
            for k, (chip, s) in enumerate(chips):
                got = out_refs[t].at[s, other]
                pltpu.make_async_remote_copy(
                    src_ref=got, dst_ref=got, send_sem=send_sems.at[t, 3 + k], recv_sem=recv_sems.at[t, 3 + k],
                    device_id=sibling, device_id_type=MESH).wait_recv()
        for cp in first + passed:
            cp.wait_send()

    return pl.pallas_call(
        body, name="all_gather_weights",
        in_specs=[ANY] * n, out_specs=[ANY] * n,
        out_shape=[jax.ShapeDtypeStruct(s.shape, s.dtype) for s in shards],
        scratch_shapes=[pltpu.SemaphoreType.DMA((n, 6)), pltpu.SemaphoreType.DMA((n, 6))],
        input_output_aliases={t: t for t in range(n)},
    )(*shards)


def _exchange_rider(operands, out_shape, aliases, sem_shape, pairs):
    def start(ins, outs, sems):
        for send, _ in pairs(ins, outs, *sems):
            send.start()

    def finish(ins, outs, sems):
        for send, recv in pairs(ins, outs, *sems):
            send.wait_send()
            recv.wait_recv()

    sems = [pltpu.SemaphoreType.DMA(sem_shape), pltpu.SemaphoreType.DMA(sem_shape)]
    return _Rider(list(operands), list(out_shape), aliases, sems, start, finish)


def _remote(src, dst, send_sem, recv_sem, device):
    return pltpu.make_async_remote_copy(src_ref=src, dst_ref=dst, send_sem=send_sem, recv_sem=recv_sem,
                                        device_id=device, device_id_type=MESH)


def _fetch_rider(bufs):
    def pairs(ins, outs, send_sems, recv_sems):
        x, y, c = _mesh_pos()
        res = []
        for t, buf in enumerate(bufs):
            rows = pl.ds(c * (buf.shape[1] // 2), buf.shape[1] // 2)
            mine = outs[t].at[2 * x + y, rows]
            for k, (chip, s) in enumerate(_other_chips(x, y)):
                landed = outs[t].at[s, rows]
                res.append((_remote(mine, mine, send_sems.at[t, k], recv_sems.at[t, k], (*chip, c)),
                            _remote(landed, landed, send_sems.at[t, k], recv_sems.at[t, k], (*chip, c))))
        return res

    shapes = [jax.ShapeDtypeStruct(b.shape, b.dtype) for b in bufs]
    return _exchange_rider(bufs, shapes, {t: t for t in range(len(bufs))}, (len(bufs), 3), pairs)


def _forward_rider(bufs):
    def pairs(ins, outs, send_sems, recv_sems):
        x, y, c = _mesh_pos()
        res = []
        for t, buf in enumerate(bufs):
            half = buf.shape[1] // 2
            for k, (_, s) in enumerate(_other_chips(x, y)):
                landed = outs[t].at[s, pl.ds(c * half, half)]
                theirs = outs[t].at[s, pl.ds((1 - c) * half, half)]
                res.append((_remote(landed, landed, send_sems.at[t, k], recv_sems.at[t, k], (x, y, 1 - c)),
                            _remote(theirs, theirs, send_sems.at[t, k], recv_sems.at[t, k], (x, y, 1 - c))))
        return res

    shapes = [jax.ShapeDtypeStruct(b.shape, b.dtype) for b in bufs]
    return _exchange_rider(bufs, shapes, {t: t for t in range(len(bufs))}, (len(bufs), 3), pairs)


def _pair_exchange_rider(grads):
    def pairs(ins, outs, send_sems, recv_sems):
        x, y, c = _mesh_pos()
        res = []
        for t, g in enumerate(grads):
            half = g.shape[1] // 2
            cp = _remote(ins[t].at[:, pl.ds((1 - c) * half, half), :], outs[t], send_sems.at[t], recv_sems.at[t],
                         (x, y, 1 - c))
            res.append((cp, cp))
        return res

    shapes = [jax.ShapeDtypeStruct((N_CHIPS, g.shape[1] // 2, g.shape[2]), F32) for g in grads]
    return _exchange_rider(grads, shapes, {}, (len(grads),), pairs)


def _chip_exchange_rider(sums):
    def pairs(ins, outs, send_sems, recv_sems):
        x, y, c = _mesh_pos()
        res = []
        for t in range(len(sums)):
            for k, (chip, s) in enumerate(_other_chips(x, y)):
                cp = _remote(ins[t].at[s], outs[t].at[k], send_sems.at[t, k], recv_sems.at[t, k], (*chip, c))
                res.append((cp, cp))
        return res

    shapes = [jax.ShapeDtypeStruct((3,) + p.shape[1:], p.dtype) for p in sums]
    return _exchange_rider(sums, shapes, {}, (len(sums), 3), pairs)


def _pair_gather_rider(fulls):
    def pairs(ins, outs, send_sems, recv_sems):
        x, y, c = _mesh_pos()
        res = []
        for t, f in enumerate(fulls):
            half = f.shape[0] // 2
            mine = outs[t].at[pl.ds(c * half, half)]
            theirs = outs[t].at[pl.ds((1 - c) * half, half)]
            res.append((_remote(mine, mine, send_sems.at[t], recv_sems.at[t], (x, y, 1 - c)),
                        _remote(theirs, theirs, send_sems.at[t], recv_sems.at[t], (x, y, 1 - c))))
        return res

    shapes = [jax.ShapeDtypeStruct(f.shape, F32) for f in fulls]
    return _exchange_rider(fulls, shapes, {t: t for t in range(len(fulls))}, (len(fulls),), pairs)


def _alone(name, rider):
    return _call(lambda: None, rider, name=name)()[1]


def _all_reduce_small(pack):
    rows = pack.shape[0]

    def body(p_ref, o_ref, buf, send_sems, recv_sems):
        x, y, c = _mesh_pos()
        me = 4 * x + 2 * y + c
        buf[0] = p_ref[...]
        copies = []
        for k in range(1, 8):
            peer = (x ^ (k >> 2), y ^ ((k >> 1) & 1), c ^ (k & 1))
            cp = pltpu.make_async_remote_copy(
                src_ref=p_ref, dst_ref=buf.at[k], send_sem=send_sems.at[k - 1], recv_sem=recv_sems.at[k - 1],
                device_id=peer, device_id_type=MESH)
            cp.start()
            copies.append(cp)
        for cp in copies:
            cp.wait()
        total = buf[me]
        for dev in range(1, 8):
            total = total + buf[me ^ dev]
        o_ref[...] = total

    return pl.pallas_call(
        body, name="all_reduce_small",
        in_specs=[VMEM_FULL], out_specs=VMEM_FULL,
        out_shape=jax.ShapeDtypeStruct(pack.shape, F32),
        scratch_shapes=[pltpu.VMEM((8, rows, LANES), F32),
                        pltpu.SemaphoreType.DMA((7,)), pltpu.SemaphoreType.DMA((7,))],
    )(pack)


def _row_block(rows):
    if rows <= 512:
        return rows
    for rb in (256, 352):
        if rows % rb == 0:
            return rb
    raise ValueError(f"no row block for {rows} rows")


def _place(name, w, pos, dtype):
    R, C = w.shape
    rb = _row_block(R)

    def body(pos_ref, w_ref, o_ref):
        del pos_ref
        o_ref[0] = w_ref[...].astype(dtype)

    return pl.pallas_call(
        body, name=name,
        grid_spec=pltpu.PrefetchScalarGridSpec(
            num_scalar_prefetch=1, grid=(R // rb,),
            in_specs=[pl.BlockSpec((rb, C), lambda r, p: (r, 0))],
            out_specs=pl.BlockSpec((1, rb, C), lambda r, p: (p[0], r, 0))),
        out_shape=jax.ShapeDtypeStruct((N_CHIPS, R, C), dtype),
        compiler_params=_params(32, 1),
    )(pos, w)


def _pair_sum(name, g, got, pos):
    S, R, C = g.shape
    half = R // 2
    rb = _row_block(half)
    nh = half // rb

    def body(pos_ref, a_ref, b_ref, o_ref):
        del pos_ref
        o_ref[...] = (a_ref[...] + b_ref[...]).astype(BF16)

    spec = pl.BlockSpec((1, rb, C), lambda s, r, p: (s, r, 0))
    return pl.pallas_call(
        body, name=name,
        grid_spec=pltpu.PrefetchScalarGridSpec(
            num_scalar_prefetch=1, grid=(S, nh),
            in_specs=[pl.BlockSpec((1, rb, C), lambda s, r, p: (s, p[1] * nh + r, 0)), spec],
            out_specs=spec),
        out_shape=jax.ShapeDtypeStruct((S, half, C), BF16), compiler_params=_params(32, 2),
    )(pos, g, got)


def _chip_sum(name, pairs, got, pos):
    _, half, C = pairs.shape
    rb = _row_block(half)
    nh = half // rb

    def body(pos_ref, a_ref, g_ref, o_ref):
        del pos_ref
        o_ref[...] = ((a_ref[0].astype(F32) + g_ref[0].astype(F32)) + g_ref[1].astype(F32)) + g_ref[2].astype(F32)

    return pl.pallas_call(
        body, name=name,
        grid_spec=pltpu.PrefetchScalarGridSpec(
            num_scalar_prefetch=1, grid=(nh,),
            in_specs=[pl.BlockSpec((1, rb, C), lambda r, p: (p[0], r, 0)),
                      pl.BlockSpec((3, rb, C), lambda r, p: (0, r, 0))],
            out_specs=pl.BlockSpec((rb, C), lambda r, p: (p[1] * nh + r, 0))),
        out_shape=jax.ShapeDtypeStruct((2 * half, C), F32), compiler_params=_params(32, 1),
    )(pos, pairs, got)


def _adamw(name, w, g, m, v):
    R, C = w.shape
    rb = _row_block(R)
    c1 = 1.0 - ADAM_B1 ** ADAM_STEP
    c2 = 1.0 - ADAM_B2 ** ADAM_STEP

    def body(w_ref, g_ref, m_ref, v_ref, d_ref, nm_ref, nv_ref):
        gv = g_ref[...]
        nm = ADAM_B1 * m_ref[...] + (1.0 - ADAM_B1) * gv
        nv = ADAM_B2 * v_ref[...] + (1.0 - ADAM_B2) * (gv * gv)
        nm_ref[...] = nm
        nv_ref[...] = nv
        d_ref[...] = -ADAM_LR * ((nm / c1) / (jnp.sqrt(nv / c2) + ADAM_EPS) + ADAM_WD * w_ref[...])

    spec = pl.BlockSpec((rb, C), lambda r: (r, 0))
    sds = jax.ShapeDtypeStruct(w.shape, F32)
    return pl.pallas_call(
        body, name=name, grid=(R // rb,), in_specs=[spec] * 4, out_specs=[spec] * 3,
        out_shape=[sds, sds, sds], compiler_params=_params(40, 1),
    )(w, g, m, v)


def _rel_index():
    m = np.arange(2 * ATT_BLK)
    off = np.where(m < ATT_BLK, m, m - 2 * ATT_BLK)
    rel = np.stack([ATT_BLK * d - off for d in range(N_ATT_TILES)])
    return np.clip(rel, -MAX_REL, MAX_REL) + MAX_REL


def _local_step(x, tgt, g1, w_in, cw, cb, lg, lb, rel, w_out, g2, g3, w_up, fw, fb, w_down, g4, pos=None):
    T = x.shape[0]
    dist = pos is not None
    idx = _rel_index()
    vec = jnp.transpose(rel[:, idx], (1, 0, 2)).reshape(N_ATT_TILES * N_HEADS, 1, 2 * ATT_BLK)
    bias = _bias_tiles(vec)

    (u, a, qkv), got = _fwd_in_proj(x, g1, w_in, _fetch_rider([w_out, w_down]) if dist else None)
    if dist:
        w_out, w_down = got
    (co, hc), got = _fwd_conv(a, cw, cb, lg, lb, _merge_riders(_forward_rider([w_out, w_down]),
                                                               _fetch_rider([w_up])) if dist else None)
    if dist:
        w_out, w_down, w_up = got
    (ao, lse), got = _fwd_attn(qkv, bias, _forward_rider([w_up]) if dist else None)
    if dist:
        (w_up,) = got
        w_out, w_down = w_out.reshape(D_MODEL, D_MODEL), w_down.reshape(D_FF, D_MODEL)
    mixed, h1, u2 = _fwd_out_proj(co, ao, w_out, x, g2, g3)
    hf, fp = _fwd_ffn(u2, w_up, fw, fb, w_down)
    loss, dy, df, dg4 = _fwd_loss(fp, h1, tgt, g4)

    tk = 1024
    du2p, dhf, act, dfw_g, dfw_v = _bwd_ffn(df, hf, w_up, fw, fb, w_down)
    gw_up = _wgrad(
        "wgrad_up", [u2], pl.BlockSpec((tk, D_MODEL), lambda s, k: (k, 0)),
        dhf, pl.BlockSpec((1, tk, FF_SHARD), lambda s, k: (s // 2, k, s % 2)),
        pl.BlockSpec((1, D_MODEL, FF_SHARD), lambda s, k: (s, 0, 0)),
        jax.ShapeDtypeStruct((N_CHIPS, D_MODEL, FF_SHARD), F32), N_CHIPS, T, tk)
    gw_down = _wgrad(
        "wgrad_down", [act], pl.BlockSpec((tk, FF_SHARD), lambda s, k: (k, s)),
        df, pl.BlockSpec((tk, D_MODEL), lambda s, k: (k, 0)),
        pl.BlockSpec((FF_SHARD, D_MODEL), lambda s, k: (s, 0)),
        jax.ShapeDtypeStruct((D_FF, D_MODEL), F32), 2, T, tk).reshape(N_CHIPS, D_FF // N_CHIPS, D_MODEL)
    (dh1, dmx, dco, dao, dg3, dg2), _ = _bwd_mid(du2p, dy, h1, mixed, g3, g2, w_out)
    gw_out = _wgrad(
        "wgrad_out", [co, ao], pl.BlockSpec((tk, CONV_W), lambda s, k: (k, 0)),
        dmx, pl.BlockSpec((tk, D_MODEL), lambda s, k: (k, 0)),
        pl.BlockSpec((CONV_W, D_MODEL), lambda s, k: (s, 0)),
        jax.ShapeDtypeStruct((D_MODEL, D_MODEL), F32), 2, T, tk,
        select=lambda s: s, rider=_pair_exchange_rider([gw_up, gw_down]) if dist else None)
    if dist:
        gw_out, got = gw_out
        p_up = _pair_sum("pair_sum_w_up", gw_up, got[0], pos)
        p_down = _pair_sum("pair_sum_w_down", gw_down, got[1], pos)
    gw_out = gw_out.reshape(N_CHIPS, D_MODEL // N_CHIPS, D_MODEL)
    (dproj, dsacc), got = _bwd_attn(
        qkv, ao, dao, lse, bias,
        _merge_riders(_chip_exchange_rider([p_up, p_down]), _pair_exchange_rider([gw_out])) if dist else None)
    if dist:
        gw_up = _chip_sum("chip_sum_w_up", p_up, got[0], pos)
        gw_down = _chip_sum("chip_sum_w_down", p_down, got[1], pos)
        p_out = _pair_sum("pair_sum_w_out", gw_out, got[2], pos)
    (dproj, dcw, dcb, dlg, dlb), got = _bwd_conv(
        dproj, a, dco, hc, cw, lg, lb,
        _merge_riders(_pair_gather_rider([gw_up, gw_down]), _chip_exchange_rider([p_out])) if dist else None)
    if dist:
        gw_up, gw_down = got[:2]
        gw_out = _chip_sum("chip_sum_w_out", p_out, got[2], pos)
    gw_in = _wgrad(
        "wgrad_in", [u], pl.BlockSpec((tk, D_MODEL), lambda s, k: (k, 0)),
        dproj, pl.BlockSpec((tk, IN_SHARD), lambda s, k: (k, s)),
        pl.BlockSpec((1, D_MODEL, IN_SHARD), lambda s, k: (s, 0, 0)),
        jax.ShapeDtypeStruct((N_CHIPS, D_MODEL, IN_SHARD), F32), N_CHIPS, T, tk)
    if dist:
        got = _alone("pair_exchange_w_in", _merge_riders(_pair_exchange_rider([gw_in]), _pair_gather_rider([gw_out])))
        p_in, gw_out = _pair_sum("pair_sum_w_in", gw_in, got[0], pos), got[1]
    (gx, dg1), _ = _bwd_in_proj(dproj, w_in, x, dh1, g1)
    (diag,), got = _diag_sums(dsacc, _chip_exchange_rider([p_in]) if dist else None)
    if dist:
        (gw_in,) = _alone("pair_gather_w_in", _pair_gather_rider([_chip_sum("chip_sum_w_in", p_in, got[0], pos)]))

    diag = diag.reshape(N_ATT_TILES, N_HEADS, 2 * ATT_BLK)
    onehot = np.zeros((N_ATT_TILES, 2 * ATT_BLK, 2 * MAX_REL + 1), np.float32)
    for d in range(N_ATT_TILES):
        onehot[d, np.arange(2 * ATT_BLK), idx[d]] = 1.0
    drel = jnp.einsum("dhm,dmr->hr", diag, jnp.asarray(onehot), precision=lax.Precision.HIGHEST)

    small = dict(norm_mix_pre=dg1, conv_dw_w=dcw[:CONV_K], conv_dw_b=dcb, conv_ln_g=dlg, conv_ln_b=dlb,
                 rel_bias=drel, norm_mix_post=dg2, norm_ffn_pre=dg3,
                 ffn_dw_w=jnp.concatenate([dfw_g[0, :3], dfw_g[1, :3], dfw_v[0, :3], dfw_v[1, :3]], axis=1),
                 ffn_dw_b=jnp.concatenate([dfw_g[0, 3:4], dfw_g[1, 3:4], dfw_v[0, 3:4], dfw_v[1, 3:4]], axis=1),
                 norm_ffn_post=dg4)
    return loss, gx, small, dict(w_in=gw_in, w_out=gw_out, w_up=gw_up, w_down=gw_down)


SMALL_ORDER = ["norm_mix_pre", "conv_dw_b", "conv_ln_g", "conv_ln_b", "rel_bias", "norm_mix_post",
               "norm_ffn_pre", "ffn_dw_b", "norm_ffn_post", "conv_dw_w", "ffn_dw_w"]


def _pack(parts):
    rows = []
    for p in parts:
        width = -(-p.shape[1] // LANES) * LANES
        rows.append(jnp.pad(p, ((0, 0), (0, width - p.shape[1]))).reshape(-1, LANES))
    packed = jnp.concatenate(rows, axis=0)
    pad = -packed.shape[0] % 8
    return jnp.pad(packed, ((0, pad), (0, 0)))


def _unpack(packed, shapes):
    out, r = [], 0
    for shp in shapes:
        width = -(-shp[1] // LANES) * LANES
        n = shp[0] * width // LANES
        out.append(packed[r:r + n].reshape(shp[0], width)[:, :shp[1]])
        r += n
    return out


WEIGHTS = ["norm_mix_pre", "w_in", "conv_dw_w", "conv_dw_b", "conv_ln_g", "conv_ln_b", "rel_bias", "w_out",
           "norm_mix_post", "norm_ffn_pre", "w_up", "ffn_dw_w", "ffn_dw_b", "w_down", "norm_ffn_post"]
BIG = ["w_in", "w_out", "w_up", "w_down"]


def kernel(x, norm_mix_pre, w_in, conv_dw_w, conv_dw_b, conv_ln_g, conv_ln_b, rel_bias, w_out, norm_mix_post, norm_ffn_pre, w_up, ffn_dw_w, ffn_dw_b, w_down, norm_ffn_post, loss_target, m_norm_mix_pre, m_w_in, m_conv_dw_w, m_conv_dw_b, m_conv_ln_g, m_conv_ln_b, m_rel_bias, m_w_out, m_norm_mix_post, m_norm_ffn_pre, m_w_up, m_ffn_dw_w, m_ffn_dw_b, m_w_down, m_norm_ffn_post, v_norm_mix_pre, v_w_in, v_conv_dw_w, v_conv_dw_b, v_conv_ln_g, v_conv_ln_b, v_rel_bias, v_w_out, v_norm_mix_post, v_norm_ffn_pre, v_w_up, v_ffn_dw_w, v_ffn_dw_b, v_w_down, v_norm_ffn_post):
    args = locals()
    w = {n: args[n][0] for n in WEIGHTS}
    m = {n: args["m_" + n][0] for n in WEIGHTS}
    v = {n: args["v_" + n][0] for n in WEIGHTS}
    for d in (w, m, v):
        d["rel_bias"] = d["rel_bias"].reshape(N_HEADS, 2 * MAX_REL + 1)
        for n in ("norm_mix_pre", "conv_dw_b", "conv_ln_g", "conv_ln_b", "norm_mix_post", "norm_ffn_pre",
                  "ffn_dw_b", "norm_ffn_post"):
            d[n] = d[n].reshape(1, -1)
    shard = 2 * lax.axis_index("x") + lax.axis_index("y")

    cw_sh = jnp.pad(w["conv_dw_w"], ((0, CONV_HALO - CONV_K), (0, 0)))
    fw_sh = jnp.pad(w["ffn_dw_w"], ((0, FF_HALO - 3), (0, 0)))
    pos = jnp.stack([shard, lax.axis_index("c")]).astype(jnp.int32)
    bufs = {n: _place("place_" + n, w[n], pos, BF16) for n in BIG}
    w_in_f, cw_f, fw_f = _all_gather_weights(
        [bufs["w_in"], _place("place_conv_dw_w", cw_sh, pos, F32), _place("place_ffn_dw_w", fw_sh, pos, F32)])
    cw_full = jnp.transpose(cw_f, (1, 0, 2)).reshape(CONV_HALO, CONV_W)

    loss, gx, small, big = _local_step(
        x[0], loss_target[0], w["norm_mix_pre"], w_in_f, cw_full, w["conv_dw_b"], w["conv_ln_g"],
        w["conv_ln_b"], w["rel_bias"], bufs["w_out"], w["norm_mix_post"],
        w["norm_ffn_pre"], bufs["w_up"], fw_f, w["ffn_dw_b"].reshape(N_CHIPS, 1, FF_SHARD),
        bufs["w_down"], w["norm_ffn_post"], pos)
    grads, deltas, new_m, new_v = {}, {}, {}, {}
    for n in BIG:
        grads[n] = big[n]
        deltas[n], new_m[n], new_v[n] = _adamw("adamw_" + n, w[n], big[n], m[n], v[n])

    gsum = _all_reduce_small(_pack([small[n] for n in SMALL_ORDER]))
    shapes = [small[n].shape for n in SMALL_ORDER]
    gs = dict(zip(SMALL_ORDER, _unpack(gsum, shapes)))
    gs["conv_dw_w"] = lax.dynamic_slice_in_dim(gs["conv_dw_w"], shard * LANES, LANES, axis=1)
    gs["ffn_dw_w"] = lax.dynamic_slice_in_dim(gs["ffn_dw_w"], shard * FF_SHARD, FF_SHARD, axis=1)
    shapes = [gs[n].shape for n in SMALL_ORDER]
    d_p, m_p, v_p = _adamw("adamw_small", _pack([w[n] for n in SMALL_ORDER]), _pack([gs[n] for n in SMALL_ORDER]),
                           _pack([m[n] for n in SMALL_ORDER]), _pack([v[n] for n in SMALL_ORDER]))
    for dst, packed in ((deltas, d_p), (new_m, m_p), (new_v, v_p)):
        dst.update(zip(SMALL_ORDER, _unpack(packed, shapes)))
    grads.update(gs)

    total = lax.psum(loss[0, 0], ("x", "y", "c"))
    outs = [total, gx[None]]
    for group in (grads, deltas, new_m, new_v):
        outs += [group[n].reshape(args[n].shape) for n in WEIGHTS]
    return tuple(outs)
```

```python
import functools
import math
from typing import Callable, NamedTuple

import numpy as np
import jax
import jax.numpy as jnp
from jax import lax
from jax.experimental import pallas as pl
from jax.experimental.pallas import tpu as pltpu

F32 = jnp.float32
BF16 = jnp.bfloat16

D_MODEL = 1024
CONV_W = 512
ATTN_W = 512
N_HEADS = 8
HEAD_DIM = 64
CHUNK = 64
N_LEFT = 8
MAX_REL = 128
CONV_K = 31
CONV_HALO = 32
D_FF = 2816
FF_SHARD = 1408
IN_COLS = 2560
IN_SHARD = 640
EPS = 1e-6
NEG_INF = -1e30
ATT_BLK = 256
N_ATT_TILES = 3
LANES = 128
SUBLANES = 8
N_CHIPS = 4

ADAM_LR = 0.001
ADAM_B1 = 0.9
ADAM_B2 = 0.999
ADAM_EPS = 1e-08
ADAM_WD = 0.01
ADAM_STEP = 10

MESH = pl.DeviceIdType.MESH
ANY = pl.BlockSpec(memory_space=pl.ANY)
VMEM_FULL = pl.BlockSpec(memory_space=pltpu.VMEM)


def _params(vmem_mb, n_grid=0):
    sem = ("arbitrary",) * n_grid if n_grid else None
    return pltpu.CompilerParams(dimension_semantics=sem, vmem_limit_bytes=vmem_mb << 20)


class _Rider(NamedTuple):
    operands: list
    out_shape: list
    aliases: dict
    sems: list
    start: Callable
    finish: Callable


def _merge_riders(a, b):
    ia, oa, sa = len(a.operands), len(a.out_shape), len(a.sems)

    def start(ins, outs, sems):
        a.start(ins[:ia], outs[:oa], sems[:sa])
        b.start(ins[ia:], outs[oa:], sems[sa:])

    def finish(ins, outs, sems):
        a.finish(ins[:ia], outs[:oa], sems[:sa])
        b.finish(ins[ia:], outs[oa:], sems[sa:])

    aliases = {**a.aliases, **{k + ia: v + oa for k, v in b.aliases.items()}}
    return _Rider(a.operands + b.operands, a.out_shape + b.out_shape, aliases, a.sems + b.sems, start, finish)


PIN_BYTES = 1 << 20


def _big(a):
    return math.prod(a.shape) * jnp.dtype(a.dtype).itemsize >= PIN_BYTES


def _pin_args(args):
    return [pltpu.with_memory_space_constraint(a, pltpu.HBM) if _big(a) else a for a in args]


def _call(body, rider, *, grid=(), in_specs=(), out_specs=(), out_shape=(), scratch_shapes=(),
          input_output_aliases=None, **kwargs):
    in_specs, out_specs = list(in_specs), list(out_specs)
    out_shape = [pltpu.HBM(s.shape, s.dtype) if _big(s) else s for s in out_shape]
    scratch, aliases = list(scratch_shapes), dict(input_output_aliases or {})
    if rider is None:
        plain = pl.pallas_call(body, grid=grid, in_specs=in_specs, out_specs=out_specs, out_shape=out_shape,
                               scratch_shapes=scratch, input_output_aliases=aliases, **kwargs)
        return lambda *args: (plain(*_pin_args(args)), [])
    n_in, n_out, n_scr = len(in_specs), len(out_specs), len(scratch)
    r_in, r_out = len(rider.operands), len(rider.out_shape)

    def carried(*refs):
        ins, r_ins, refs = refs[:n_in], refs[n_in:n_in + r_in], refs[n_in + r_in:]
        outs, r_outs, refs = refs[:n_out], refs[n_out:n_out + r_out], refs[n_out + r_out:]
        scr, r_sems = refs[:n_scr], refs[n_scr:]
        if not grid:
            rider.start(r_ins, r_outs, r_sems)
            body(*ins, *outs, *scr)
            rider.finish(r_ins, r_outs, r_sems)
            return
        at = [pl.program_id(d) for d in range(len(grid))]
        first = functools.reduce(jnp.logical_and, [p == 0 for p in at])
        last = functools.reduce(jnp.logical_and, [p == n - 1 for p, n in zip(at, grid)])

        @pl.when(first)
        def _():
            rider.start(r_ins, r_outs, r_sems)

        body(*ins, *outs, *scr)

        @pl.when(last)
        def _():
            rider.finish(r_ins, r_outs, r_sems)

    aliases.update({n_in + k: n_out + v for k, v in rider.aliases.items()})
    both = pl.pallas_call(carried, grid=grid, in_specs=in_specs + [ANY] * r_in, out_specs=out_specs + [ANY] * r_out,
                          out_shape=out_shape + rider.out_shape, scratch_shapes=scratch + rider.sems,
                          input_output_aliases=aliases, **kwargs)

    def run(*args):
        res = both(*_pin_args(args), *rider.operands)
        return res[:n_out], res[n_out:]

    return run


def _sigmoid(v):
    return 1.0 / (1.0 + jnp.exp(-v))


def _dot(a, b):
    return jnp.dot(a, b, preferred_element_type=F32)


def _dot_nt(a, b):
    return lax.dot_general(a, b, (((1,), (1,)), ((), ())), preferred_element_type=F32)


def _dot_tn(a, b):
    return lax.dot_general(a, b, (((0,), (0,)), ((), ())), preferred_element_type=F32)


def _rms_fwd(v, g):
    r = lax.rsqrt(jnp.mean(v * v, axis=-1, keepdims=True) + EPS)
    return v * r * g, r


def _rms_bwd(dy, v, g):
    r = lax.rsqrt(jnp.mean(v * v, axis=-1, keepdims=True) + EPS)
    vh = v * r
    dvh = dy * g
    dv = r * (dvh - vh * jnp.mean(dvh * vh, axis=-1, keepdims=True))
    return dv, dy * vh


def _fwd_in_proj(x, g1, w_in, rider=None):
    T = x.shape[0]
    tm = 512

    def body(x_ref, g_ref, w_ref, u_ref, a_ref, qkv_ref):
        u, _ = _rms_fwd(x_ref[...], g_ref[...])
        u = u.astype(BF16)
        u_ref[...] = u
        for s in range(N_CHIPS):
            y = _dot(u, w_ref[s])
            lo, hi = IN_SHARD * s, IN_SHARD * (s + 1)
            if hi <= 1024:
                a_ref[:, lo:hi] = y
            elif lo >= 1024:
                qkv_ref[:, lo - 1024:hi - 1024] = y.astype(BF16)
            else:
                a_ref[:, lo:1024] = y[:, :1024 - lo]
                qkv_ref[:, 0:hi - 1024] = y[:, 1024 - lo:].astype(BF16)

    return _call(
        body, rider, name="fwd_in_proj", grid=(T // tm,),
        in_specs=[pl.BlockSpec((tm, D_MODEL), lambda i: (i, 0)),
                  pl.BlockSpec((1, D_MODEL), lambda i: (0, 0)),
                  pl.BlockSpec((N_CHIPS, D_MODEL, IN_SHARD), lambda i: (0, 0, 0))],
        out_specs=[pl.BlockSpec((tm, D_MODEL), lambda i: (i, 0)),
                   pl.BlockSpec((tm, 1024), lambda i: (i, 0)),
                   pl.BlockSpec((tm, 1536), lambda i: (i, 0))],
        out_shape=[jax.ShapeDtypeStruct((T, D_MODEL), BF16),
                   jax.ShapeDtypeStruct((T, 1024), F32),
                   jax.ShapeDtypeStruct((T, 1536), BF16)],
        compiler_params=_params(40, 1),
    )(x, g1, w_in)


def _fill_shifted(ext, shifted, tm):
    n = tm + CONV_HALO - SUBLANES
    for j in range(1, SUBLANES):
        shifted[j - 1] = ext[j:j + n, :]


def _shifted_rows(ext, shifted, start, rows):
    j = start % SUBLANES
    if j == 0:
        return ext[start:start + rows, :]
    return shifted[j - 1, start - j:start - j + rows, :]


def _fwd_conv(a, cw, cb, lg, lb, rider=None):
    T = a.shape[0]
    tm = 512
    rc = 64

    def body(a_ref, w_ref, b_ref, lg_ref, lb_ref, co_ref, hc_ref, hext, hsh):
        i = pl.program_id(0)

        @pl.when(i == 0)
        def _():
            hext[0:CONV_HALO, :] = jnp.zeros((CONV_HALO, CONV_W), F32)

        @pl.when(i > 0)
        def _():
            hext[0:CONV_HALO, :] = hext[tm:tm + CONV_HALO, :]

        hext[CONV_HALO:CONV_HALO + tm, :] = a_ref[:, :CONV_W] * _sigmoid(a_ref[:, CONV_W:])
        _fill_shifted(hext, hsh, tm)
        for c in range(tm // rc):
            acc = jnp.zeros((rc, CONV_W), F32)
            for k in range(CONV_K):
                acc = acc + w_ref[k:k + 1, :] * _shifted_rows(hext, hsh, c * rc + 2 + k, rc)
            hc = acc + b_ref[...]
            hc_ref[c * rc:(c + 1) * rc, :] = hc
            mu = jnp.mean(hc, axis=-1, keepdims=True)
            xc = hc - mu
            var = jnp.mean(xc * xc, axis=-1, keepdims=True)
            z = xc * lax.rsqrt(var + EPS) * lg_ref[...] + lb_ref[...]
            co_ref[c * rc:(c + 1) * rc, :] = (z * _sigmoid(z)).astype(BF16)

    return _call(
        body, rider, name="fwd_conv", grid=(T // tm,),
        in_specs=[pl.BlockSpec((tm, 1024), lambda i: (i, 0)),
                  pl.BlockSpec((CONV_HALO, CONV_W), lambda i: (0, 0)),
                  pl.BlockSpec((1, CONV_W), lambda i: (0, 0)),
                  pl.BlockSpec((1, CONV_W), lambda i: (0, 0)),
                  pl.BlockSpec((1, CONV_W), lambda i: (0, 0))],
        out_specs=[pl.BlockSpec((tm, CONV_W), lambda i: (i, 0)),
                   pl.BlockSpec((tm, CONV_W), lambda i: (i, 0))],
        out_shape=[jax.ShapeDtypeStruct((T, CONV_W), BF16),
                   jax.ShapeDtypeStruct((T, CONV_W), F32)],
        scratch_shapes=[pltpu.VMEM((tm + CONV_HALO, CONV_W), F32),
                        pltpu.VMEM((SUBLANES - 1, tm + CONV_HALO - SUBLANES, CONV_W), F32)],
        compiler_params=_params(40, 1),
    )(a, cw, cb, lg, lb)


def _row_skew(v, sign):
    rows, width = v.shape
    row = lax.broadcasted_iota(jnp.int32, (rows, 1), 0)
    for b in range(int(math.log2(rows))):
        shift = (1 << b) if sign > 0 else width - (1 << b)
        v = jnp.where(((row >> b) & 1) == 1, pltpu.roll(v, shift, 1), v)
    return v


def _att_visible(d):
    rq = lax.broadcasted_iota(jnp.int32, (ATT_BLK, ATT_BLK), 0) // CHUNK
    ck = lax.broadcasted_iota(jnp.int32, (ATT_BLK, ATT_BLK), 1) // CHUNK
    slack = ATT_BLK
    above = jnp.where(d == 0, 0, slack)
    below = jnp.where(d == 2, 0, slack)
    return (ck <= rq + above) & (ck >= rq - below)


def _bias_tiles(vec):
    def body(v_ref, o_ref):
        d = pl.program_id(0) // N_HEADS
        full = _row_skew(jnp.broadcast_to(v_ref[0], (ATT_BLK, 2 * ATT_BLK)), 1)
        o_ref[0] = jnp.where(_att_visible(d), full[:, :ATT_BLK], NEG_INF)

    return pl.pallas_call(
        body, name="bias_tiles", grid=(N_ATT_TILES * N_HEADS,),
        in_specs=[pl.BlockSpec((1, 1, 2 * ATT_BLK), lambda n: (n, 0, 0))],
        out_specs=pl.BlockSpec((1, ATT_BLK, ATT_BLK), lambda n: (n, 0, 0)),
        out_shape=jax.ShapeDtypeStruct((N_ATT_TILES * N_HEADS, ATT_BLK, ATT_BLK), F32),
        compiler_params=_params(16, 1),
    )(vec)


def _diag_sums(ds, rider=None):
    def body(d_ref, o_ref):
        wide = jnp.concatenate([d_ref[0], jnp.zeros((ATT_BLK, ATT_BLK), F32)], axis=1)
        o_ref[0] = jnp.sum(_row_skew(wide, -1), axis=0, keepdims=True)

    return _call(
        body, rider, name="diag_sums", grid=(N_ATT_TILES * N_HEADS,),
        in_specs=[pl.BlockSpec((1, ATT_BLK, ATT_BLK), lambda n: (n, 0, 0))],
        out_specs=[pl.BlockSpec((1, 1, 2 * ATT_BLK), lambda n: (n, 0, 0))],
        out_shape=[jax.ShapeDtypeStruct((N_ATT_TILES * N_HEADS, 1, 2 * ATT_BLK), F32)],
        compiler_params=_params(16, 1),
    )(ds)


def _head_mask(h):
    lane = lax.broadcasted_iota(jnp.int32, (1, LANES), 1)
    return (lane // HEAD_DIM) == (h % 2)


def _fwd_attn(qkv, bias, rider=None):
    T = qkv.shape[0]
    nb = T // ATT_BLK
    scale = HEAD_DIM ** -0.5

    def body(q_ref, k0_ref, k1_ref, k2_ref, v0_ref, v1_ref, v2_ref, b_ref, o_ref, lse_ref):
        i = pl.program_id(0)

        @pl.when(i >= N_ATT_TILES - 1)
        def _():
            block(i, False, q_ref, k0_ref, k1_ref, k2_ref, v0_ref, v1_ref, v2_ref, b_ref, o_ref, lse_ref)

        @pl.when(i < N_ATT_TILES - 1)
        def _():
            block(i, True, q_ref, k0_ref, k1_ref, k2_ref, v0_ref, v1_ref, v2_ref, b_ref, o_ref, lse_ref)

    def block(i, hide_absent, q_ref, k0_ref, k1_ref, k2_ref, v0_ref, v1_ref, v2_ref, b_ref, o_ref, lse_ref):
        k_refs = (k0_ref, k1_ref, k2_ref)
        v_refs = (v0_ref, v1_ref, v2_ref)
        lane = lax.broadcasted_iota(jnp.int32, (1, LANES), 1)
        lse_tile = jnp.zeros((ATT_BLK, LANES), F32)
        for g in range(N_HEADS // 2):
            cols = slice(g * LANES, (g + 1) * LANES)
            qg = q_ref[:, cols] * scale
            og = jnp.zeros((ATT_BLK, LANES), F32)
            for h in (2 * g, 2 * g + 1):
                hm = _head_mask(h)
                qh = jnp.where(hm, qg, jnp.zeros_like(qg))
                s = []
                for d in range(N_ATT_TILES):
                    sd = _dot_nt(qh, k_refs[d][:, cols]) + b_ref[d * N_HEADS + h]
                    if d > 0 and hide_absent:
                        sd = jnp.where(i >= d, sd, NEG_INF)
                    s.append(sd)
                m = jnp.maximum(jnp.maximum(jnp.max(s[0], axis=-1, keepdims=True),
                                            jnp.max(s[1], axis=-1, keepdims=True)),
                                jnp.max(s[2], axis=-1, keepdims=True))
                p = [jnp.exp(sd - m) for sd in s]
                l = (jnp.sum(p[0], axis=-1, keepdims=True) + jnp.sum(p[1], axis=-1, keepdims=True)
                     + jnp.sum(p[2], axis=-1, keepdims=True))
                oh = jnp.zeros((ATT_BLK, LANES), F32)
                for d in range(N_ATT_TILES):
                    vg = v_refs[d][:, cols]
                    oh = oh + _dot(p[d].astype(BF16), jnp.where(hm, vg, jnp.zeros_like(vg)))
                og = og + oh / l
                lse_tile = jnp.where(lane == h, m + jnp.log(l), lse_tile)
            o_ref[:, cols] = og.astype(BF16)
        lse_ref[...] = lse_tile

    def kv_spec(d, col):
        return pl.BlockSpec((ATT_BLK, ATTN_W), lambda i: (jnp.maximum(i - d, 0), col))

    return _call(
        body, rider, name="fwd_attn", grid=(nb,),
        in_specs=[pl.BlockSpec((ATT_BLK, ATTN_W), lambda i: (i, 0)),
                  kv_spec(0, 1), kv_spec(1, 1), kv_spec(2, 1),
                  kv_spec(0, 2), kv_spec(1, 2), kv_spec(2, 2),
                  pl.BlockSpec((N_ATT_TILES * N_HEADS, ATT_BLK, ATT_BLK), lambda i: (0, 0, 0))],
        out_specs=[pl.BlockSpec((ATT_BLK, ATTN_W), lambda i: (i, 0)),
                   pl.BlockSpec((ATT_BLK, LANES), lambda i: (i, 0))],
        out_shape=[jax.ShapeDtypeStruct((T, ATTN_W), BF16),
                   jax.ShapeDtypeStruct((T, LANES), F32)],
        compiler_params=_params(40, 1),
    )(qkv, qkv, qkv, qkv, qkv, qkv, qkv, bias)


def _fwd_out_proj(co, ao, w_out, x, g2, g3):
    T = x.shape[0]
    tm = 512

    def body(co_ref, ao_ref, w_ref, x_ref, g2_ref, g3_ref, mixed_ref, h1_ref, u2_ref):
        mixed = _dot(co_ref[...], w_ref[0:CONV_W, :]) + _dot(ao_ref[...], w_ref[CONV_W:, :])
        mixed_ref[...] = mixed
        y, _ = _rms_fwd(mixed, g2_ref[...])
        h1 = x_ref[...] + y
        h1_ref[...] = h1
        u2, _ = _rms_fwd(h1, g3_ref[...])
        u2_ref[...] = u2.astype(BF16)

    row = lambda w: pl.BlockSpec((tm, w), lambda i: (i, 0))
    vec = pl.BlockSpec((1, D_MODEL), lambda i: (0, 0))
    return _call(
        body, None, name="fwd_out_proj", grid=(T // tm,),
        in_specs=[row(CONV_W), row(ATTN_W), pl.BlockSpec((D_MODEL, D_MODEL), lambda i: (0, 0)),
                  row(D_MODEL), vec, vec],
        out_specs=[row(D_MODEL), row(D_MODEL), row(D_MODEL)],
        out_shape=[jax.ShapeDtypeStruct((T, D_MODEL), F32),
                   jax.ShapeDtypeStruct((T, D_MODEL), F32),
                   jax.ShapeDtypeStruct((T, D_MODEL), BF16)],
        compiler_params=_params(40, 1),
    )(co, ao, w_out, x, g2, g3)[0]


GELU_C = math.sqrt(2.0 / math.pi)
GELU_A = 0.044715


def _gelu_and_grad(v):
    th = jnp.tanh(GELU_C * (v + GELU_A * v * v * v))
    gl = 0.5 * v * (1.0 + th)
    dgl = 0.5 * (1.0 + th) + 0.5 * v * (1.0 - th * th) * (GELU_C * (1.0 + 3.0 * GELU_A * v * v))
    return gl, dgl


FF_TM = 256
FF_HALO = 16
FF_CHUNKS = [(lo, min(lo + 256, FF_SHARD)) for lo in range(0, FF_SHARD, 256)]


def _rows_before(prev, cur):
    ext = jnp.concatenate([prev, cur], axis=0)
    return pltpu.roll(ext, 1, 0)[SUBLANES:], pltpu.roll(ext, 2, 0)[SUBLANES:]


def _rows_after(cur, nxt):
    ext = jnp.concatenate([cur, nxt], axis=0)
    n = ext.shape[0]
    return pltpu.roll(ext, n - 1, 0)[:cur.shape[0]], pltpu.roll(ext, n - 2, 0)[:cur.shape[0]]


def _fwd_ffn(u2, w_up, fw, fb, w_down):
    T = u2.shape[0]
    tm = FF_TM

    def body(u_ref, wg_ref, wv_ref, fwg_ref, fwv_ref, fbg_ref, fbv_ref, wd_ref, hf_ref, pre_ref, f_ref, carg, carv):
        i = pl.program_id(1)

        @pl.when(i == 0)
        def _():
            carg[...] = jnp.zeros(carg.shape, F32)
            carv[...] = jnp.zeros(carv.shape, F32)

        u = u_ref[...]
        f = None
        up = lambda lo, hi: (_dot(u, wg_ref[0, :, lo:hi]), _dot(u, wv_ref[0, :, lo:hi]))
        ahead = up(*FF_CHUNKS[0])
        for c, (lo, hi) in enumerate(FF_CHUNKS):
            conv = []
            hs = ahead
            if c + 1 < len(FF_CHUNKS):
                ahead = up(*FF_CHUNKS[c + 1])
            for n, (car, fw_ref, fb_ref) in enumerate(((carg, fwg_ref, fbg_ref), (carv, fwv_ref, fbv_ref))):
                h0 = hs[n]
                hf_ref[n, :, lo:hi] = h0.astype(BF16)
                h1, h2 = _rows_before(car[:, lo:hi], h0)
                car[:, lo:hi] = h0[tm - SUBLANES:, :]
                conv.append(fw_ref[0, 0:1, lo:hi] * h2 + fw_ref[0, 1:2, lo:hi] * h1
                            + fw_ref[0, 2:3, lo:hi] * h0 + fb_ref[0, :, lo:hi])
            pre_ref[0, :, lo:hi] = conv[0].astype(BF16)
            pre_ref[1, :, lo:hi] = conv[1].astype(BF16)
            gl, _ = _gelu_and_grad(conv[0])
            term = _dot((gl * conv[1]).astype(BF16), wd_ref[lo:hi, :])
            f = term if f is None else f + term
        f_ref[0] = f

    wspec = lambda off: pl.BlockSpec((1, D_MODEL, FF_SHARD), lambda s, i: (s + off, 0, 0))
    fwspec = lambda off: pl.BlockSpec((1, FF_HALO, FF_SHARD), lambda s, i: (s + off, 0, 0))
    fbspec = lambda off: pl.BlockSpec((1, 1, FF_SHARD), lambda s, i: (s + off, 0, 0))
    return _call(
        body, None, name="fwd_ffn", grid=(2, T // tm),
        in_specs=[pl.BlockSpec((tm, D_MODEL), lambda s, i: (i, 0)),
                  wspec(0), wspec(2), fwspec(0), fwspec(2), fbspec(0), fbspec(2),
                  pl.BlockSpec((FF_SHARD, D_MODEL), lambda s, i: (s, 0))],
        out_specs=[pl.BlockSpec((2, tm, FF_SHARD), lambda s, i: (0, i, s)),
                   pl.BlockSpec((2, tm, FF_SHARD), lambda s, i: (0, i, s)),
                   pl.BlockSpec((1, tm, D_MODEL), lambda s, i: (s, i, 0))],
        out_shape=[jax.ShapeDtypeStruct((2, T, D_FF), BF16),
                   jax.ShapeDtypeStruct((2, T, D_FF), BF16),
                   jax.ShapeDtypeStruct((2, T, D_MODEL), F32)],
        scratch_shapes=[pltpu.VMEM((SUBLANES, FF_SHARD), F32), pltpu.VMEM((SUBLANES, FF_SHARD), F32)],
        compiler_params=_params(48, 2),
    )(u2, w_up, w_up, fw, fw, fb, fb, w_down)[0]


def _fwd_loss(fp, h1, tgt, g4):
    T = h1.shape[0]
    tm = 512

    def body(fp_ref, h1_ref, t_ref, g_ref, loss_ref, dy_ref, df_ref, dg_ref):
        i = pl.program_id(0)
        f = fp_ref[0] + fp_ref[1]
        r, _ = _rms_fwd(f, g_ref[...])
        e = (h1_ref[...] + r) - t_ref[...]
        dy = e * (1.0 / D_MODEL)
        dy_ref[...] = dy
        df, dg_rows = _rms_bwd(dy, f, g_ref[...])
        df_ref[...] = df.astype(BF16)
        part = 0.5 * jnp.sum(jnp.mean(e * e, axis=-1, keepdims=True), axis=0, keepdims=True)
        dg = jnp.sum(dg_rows, axis=0, keepdims=True)

        @pl.when(i == 0)
        def _():
            loss_ref[...] = part
            dg_ref[...] = dg

        @pl.when(i > 0)
        def _():
            loss_ref[...] += part
            dg_ref[...] += dg

    row = pl.BlockSpec((tm, D_MODEL), lambda i: (i, 0))
    vec = pl.BlockSpec((1, D_MODEL), lambda i: (0, 0))
    return _call(
        body, None, name="fwd_loss", grid=(T // tm,),
        in_specs=[pl.BlockSpec((2, tm, D_MODEL), lambda i: (0, i, 0)), row, row, vec],
        out_specs=[pl.BlockSpec((1, 1), lambda i: (0, 0)), row, row, vec],
        out_shape=[jax.ShapeDtypeStruct((1, 1), F32),
                   jax.ShapeDtypeStruct((T, D_MODEL), F32),
                   jax.ShapeDtypeStruct((T, D_MODEL), BF16),
                   jax.ShapeDtypeStruct((1, D_MODEL), F32)],
        compiler_params=_params(40, 1),
    )(fp, h1, tgt, g4)[0]


def _bwd_ffn(df, hf, pre, w_up, fw, w_down):
    T = df.shape[0]
    tm = FF_TM
    ni = T // tm

    def body(df_ref, hf_ref, pre_ref, wd_ref, wg_ref, wv_ref, fwg_ref, fwv_ref,
             du_ref, dhf_ref, act_ref, dwg_ref, dwv_ref, carg, carv):
        i = pl.program_id(1)

        @pl.when(i == 0)
        def _():
            dwg_ref[...] = jnp.zeros(dwg_ref.shape, F32)
            dwv_ref[...] = jnp.zeros(dwv_ref.shape, F32)
            carg[...] = jnp.zeros(carg.shape, F32)
            carv[...] = jnp.zeros(carv.shape, F32)

        df = df_ref[...]
        du = None
        down = lambda lo, hi: _dot_nt(df, wd_ref[lo:hi, :])
        ahead = down(*FF_CHUNKS[0])
        for c, (lo, hi) in enumerate(FF_CHUNKS):
            dact = ahead
            if c + 1 < len(FF_CHUNKS):
                ahead = down(*FF_CHUNKS[c + 1])
            pre_g = pre_ref[0, :, lo:hi].astype(F32)
            pre_v = pre_ref[1, :, lo:hi].astype(F32)
            gl, dgl = _gelu_and_grad(pre_g)
            act_ref[:, lo:hi] = (gl * pre_v).astype(BF16)
            dpre = (dact * pre_v * dgl, dact * gl)
            for n, (car, fw_ref, dw_ref, w_ref) in enumerate(
                    ((carg, fwg_ref, dwg_ref, wg_ref), (carv, fwv_ref, dwv_ref, wv_ref))):
                dp = dpre[n]
                h0 = hf_ref[n, :, lo:hi].astype(F32)
                up1, up2 = _rows_after(dp, car[:, lo:hi])
                car[:, lo:hi] = dp[0:SUBLANES, :]
                for k, shifted in enumerate((up2, up1, dp)):
                    dw_ref[0, k:k + 1, lo:hi] += jnp.sum(shifted * h0, axis=0, keepdims=True)
                dw_ref[0, 3:4, lo:hi] += jnp.sum(dp, axis=0, keepdims=True)
                dh = (fw_ref[0, 2:3, lo:hi] * dp + fw_ref[0, 1:2, lo:hi] * up1
                      + fw_ref[0, 0:1, lo:hi] * up2).astype(BF16)
                dhf_ref[n, :, lo:hi] = dh
                term = _dot_nt(dh, w_ref[0, :, lo:hi])
                du = term if du is None else du + term
        du_ref[0] = du

    rev = lambda i: ni - 1 - i
    wspec = lambda off: pl.BlockSpec((1, D_MODEL, FF_SHARD), lambda s, i: (s + off, 0, 0))
    fwspec = lambda off: pl.BlockSpec((1, FF_HALO, FF_SHARD), lambda s, i: (s + off, 0, 0))
    dwspec = pl.BlockSpec((1, FF_HALO, FF_SHARD), lambda s, i: (s, 0, 0))
    return _call(
        body, None, name="bwd_ffn", grid=(2, ni),
        in_specs=[pl.BlockSpec((tm, D_MODEL), lambda s, i: (rev(i), 0)),
                  pl.BlockSpec((2, tm, FF_SHARD), lambda s, i: (0, rev(i), s)),
                  pl.BlockSpec((2, tm, FF_SHARD), lambda s, i: (0, rev(i), s)),
                  pl.BlockSpec((FF_SHARD, D_MODEL), lambda s, i: (s, 0)),
                  wspec(0), wspec(2), fwspec(0), fwspec(2)],
        out_specs=[pl.BlockSpec((1, tm, D_MODEL), lambda s, i: (s, rev(i), 0)),
                   pl.BlockSpec((2, tm, FF_SHARD), lambda s, i: (0, rev(i), s)),
                   pl.BlockSpec((tm, FF_SHARD), lambda s, i: (rev(i), s)),
                   dwspec, dwspec],
        out_shape=[jax.ShapeDtypeStruct((2, T, D_MODEL), F32),
                   jax.ShapeDtypeStruct((2, T, D_FF), BF16),
                   jax.ShapeDtypeStruct((T, D_FF), BF16),
                   jax.ShapeDtypeStruct((2, FF_HALO, FF_SHARD), F32),
                   jax.ShapeDtypeStruct((2, FF_HALO, FF_SHARD), F32)],
        scratch_shapes=[pltpu.VMEM((SUBLANES, FF_SHARD), F32), pltpu.VMEM((SUBLANES, FF_SHARD), F32)],
        compiler_params=_params(56, 2),
    )(df, hf, pre, w_down, w_up, w_up, fw, fw)[0]


def _bwd_mid(du2p, dy, h1, mixed, g3, g2, w_out, rider=None):
    T = dy.shape[0]
    tm = 512

    def body(du_ref, dy_ref, h1_ref, mx_ref, g3_ref, g2_ref, w_ref,
             dh1_ref, dmx_ref, dco_ref, dao_ref, dg3_ref, dg2_ref):
        i = pl.program_id(0)
        dres, dg3_rows = _rms_bwd(du_ref[0] + du_ref[1], h1_ref[...], g3_ref[...])
        dh1 = dy_ref[...] + dres
        dh1_ref[...] = dh1
        dmx, dg2_rows = _rms_bwd(dh1, mx_ref[...], g2_ref[...])
        dmx = dmx.astype(BF16)
        dmx_ref[...] = dmx
        dcat = _dot_nt(dmx, w_ref[...])
        dco_ref[...] = dcat[:, :CONV_W]
        dao_ref[...] = dcat[:, CONV_W:].astype(BF16)
        dg3 = jnp.sum(dg3_rows, axis=0, keepdims=True)
        dg2 = jnp.sum(dg2_rows, axis=0, keepdims=True)

        @pl.when(i == 0)
        def _():
            dg3_ref[...] = dg3
            dg2_ref[...] = dg2

        @pl.when(i > 0)
        def _():
            dg3_ref[...] += dg3
            dg2_ref[...] += dg2

    row = lambda w: pl.BlockSpec((tm, w), lambda i: (i, 0))
    vec = pl.BlockSpec((1, D_MODEL), lambda i: (0, 0))
    return _call(
        body, rider, name="bwd_mid", grid=(T // tm,),
        in_specs=[pl.BlockSpec((2, tm, D_MODEL), lambda i: (0, i, 0)), row(D_MODEL), row(D_MODEL),
                  row(D_MODEL), vec, vec, pl.BlockSpec((D_MODEL, D_MODEL), lambda i: (0, 0))],
        out_specs=[row(D_MODEL), row(D_MODEL), row(CONV_W), row(ATTN_W), vec, vec],
        out_shape=[jax.ShapeDtypeStruct((T, D_MODEL), F32),
                   jax.ShapeDtypeStruct((T, D_MODEL), BF16),
                   jax.ShapeDtypeStruct((T, CONV_W), F32),
                   jax.ShapeDtypeStruct((T, ATTN_W), BF16),
                   jax.ShapeDtypeStruct((1, D_MODEL), F32),
                   jax.ShapeDtypeStruct((1, D_MODEL), F32)],
        compiler_params=_params(48, 1),
    )(du2p, dy, h1, mixed, g3, g2, w_out)


def _bwd_attn(qkv, ao, dao, lse, bias, rider=None):
    T = qkv.shape[0]
    nb = T // ATT_BLK
    scale = HEAD_DIM ** -0.5

    def body(k_ref, v_ref, q0, q1, q2, do0, do1, do2, o0, o1, o2, l0, l1, l2, b_ref,
             dp_ref, ds_ref, acc1, acc2):
        j = pl.program_id(0)
        q_refs, do_refs, o_refs, l_refs = (q0, q1, q2), (do0, do1, do2), (o0, o1, o2), (l0, l1, l2)

        @pl.when(j == 0)
        def _():
            ds_ref[...] = jnp.zeros(ds_ref.shape, F32)
            acc1[...] = jnp.zeros(acc1.shape, F32)
            acc2[...] = jnp.zeros(acc2.shape, F32)

        dq_new = [[], [], []]
        dk_cols, dv_cols = [], []
        for g in range(N_HEADS // 2):
            cols = slice(g * LANES, (g + 1) * LANES)
            kg = k_ref[:, cols]
            vg = v_ref[:, cols]
            dkg = jnp.zeros((ATT_BLK, LANES), F32)
            dvg = jnp.zeros((ATT_BLK, LANES), F32)
            dqg = [jnp.zeros((ATT_BLK, LANES), F32) for _ in range(N_ATT_TILES)]
            for d in range(N_ATT_TILES):
                qg = q_refs[d][:, cols] * scale
                dog = do_refs[d][:, cols]
                if d > 0:
                    dog = jnp.where(j + d < nb, dog, jnp.zeros_like(dog))
                prod = dog.astype(F32) * o_refs[d][:, cols].astype(F32)
                for h in (2 * g, 2 * g + 1):
                    hm = _head_mask(h)
                    qh = jnp.where(hm, qg, jnp.zeros_like(qg))
                    doh = jnp.where(hm, dog, jnp.zeros_like(dog))
                    kh = jnp.where(hm, kg, jnp.zeros_like(kg))
                    delta = jnp.sum(jnp.where(hm, prod, 0.0), axis=-1, keepdims=True)
                    s = _dot_nt(qh, kg) + b_ref[d * N_HEADS + h]
                    p = jnp.exp(s - l_refs[d][:, h:h + 1])
                    dvg = dvg + _dot_tn(p.astype(BF16), doh)
                    dpm = _dot_nt(doh, vg)
                    dsc = p * (dpm - delta)
                    ds_ref[d * N_HEADS + h] += dsc
                    dsb = dsc.astype(BF16)
                    dqg[d] = dqg[d] + _dot(dsb, kh)
                    dkg = dkg + _dot_tn(dsb, qh)
            for d in range(N_ATT_TILES):
                dq_new[d].append(dqg[d])
            dk_cols.append(dkg)
            dv_cols.append(dvg)
        x0, x1, x2 = (jnp.concatenate(c, axis=1) * scale for c in dq_new)
        dp_ref[:, 0:1024] = jnp.zeros((ATT_BLK, 1024), BF16)
        dp_ref[:, 1024:1536] = (acc1[...] + x0).astype(BF16)
        dp_ref[:, 1536:2048] = jnp.concatenate(dk_cols, axis=1).astype(BF16)
        dp_ref[:, 2048:2560] = jnp.concatenate(dv_cols, axis=1).astype(BF16)
        acc1[...] = acc2[...] + x1
        acc2[...] = x2

    def fwd_spec(d, width, col):
        return pl.BlockSpec((ATT_BLK, width), lambda j: (jnp.minimum(j + d, nb - 1), col))

    return _call(
        body, rider, name="bwd_attn", grid=(nb,),
        in_specs=[pl.BlockSpec((ATT_BLK, ATTN_W), lambda j: (j, 1)),
                  pl.BlockSpec((ATT_BLK, ATTN_W), lambda j: (j, 2)),
                  fwd_spec(0, ATTN_W, 0), fwd_spec(1, ATTN_W, 0), fwd_spec(2, ATTN_W, 0),
                  fwd_spec(0, ATTN_W, 0), fwd_spec(1, ATTN_W, 0), fwd_spec(2, ATTN_W, 0),
                  fwd_spec(0, ATTN_W, 0), fwd_spec(1, ATTN_W, 0), fwd_spec(2, ATTN_W, 0),
                  fwd_spec(0, LANES, 0), fwd_spec(1, LANES, 0), fwd_spec(2, LANES, 0),
                  pl.BlockSpec((N_ATT_TILES * N_HEADS, ATT_BLK, ATT_BLK), lambda j: (0, 0, 0))],
        out_specs=[pl.BlockSpec((ATT_BLK, IN_COLS), lambda j: (j, 0)),
                   pl.BlockSpec((N_ATT_TILES * N_HEADS, ATT_BLK, ATT_BLK), lambda j: (0, 0, 0))],
        out_shape=[jax.ShapeDtypeStruct((T, IN_COLS), BF16),
                   jax.ShapeDtypeStruct((N_ATT_TILES * N_HEADS, ATT_BLK, ATT_BLK), F32)],
        scratch_shapes=[pltpu.VMEM((ATT_BLK, ATTN_W), F32), pltpu.VMEM((ATT_BLK, ATTN_W), F32)],
        compiler_params=_params(56, 1),
    )(qkv, qkv, qkv, qkv, qkv, dao, dao, dao, ao, ao, ao, lse, lse, lse, bias)


def _bwd_conv(dproj, a, dco, hc, cw, lg, lb, rider=None):
    T = a.shape[0]
    tm = 512
    rc = 32
    ni = T // tm
    hb = tm // CONV_HALO

    def body(dp_in, a_ref, ap_ref, dco_ref, dcon_ref, hc_ref, hcn_ref, w_ref, lg_ref, lb_ref,
             dp_ref, dw_ref, db_ref, dlg_ref, dlb_ref, hext, dext, hsh, dsh, dwacc):
        del dp_in
        i = pl.program_id(0)

        def ln_bwd(dco_v, hc_v):
            mu = jnp.mean(hc_v, axis=-1, keepdims=True)
            xc = hc_v - mu
            rstd = lax.rsqrt(jnp.mean(xc * xc, axis=-1, keepdims=True) + EPS)
            xh = xc * rstd
            z = xh * lg_ref[...] + lb_ref[...]
            sg = _sigmoid(z)
            dz = dco_v * (sg * (1.0 + z * (1.0 - sg)))
            dxh = dz * lg_ref[...]
            dhc = rstd * (dxh - jnp.mean(dxh, axis=-1, keepdims=True)
                          - xh * jnp.mean(dxh * xh, axis=-1, keepdims=True))
            return dhc, dz * xh, dz

        hext[0:CONV_HALO, :] = jnp.where(i > 0, ap_ref[:, :CONV_W] * _sigmoid(ap_ref[:, CONV_W:]), 0.0)
        hext[CONV_HALO:CONV_HALO + tm, :] = a_ref[:, :CONV_W] * _sigmoid(a_ref[:, CONV_W:])
        dhc, dlg_rows, dlb_rows = ln_bwd(dco_ref[...], hc_ref[...])
        dext[0:tm, :] = dhc
        dhc_next, _, _ = ln_bwd(dcon_ref[...], hcn_ref[...])
        dext[tm:tm + CONV_HALO, :] = jnp.where(i < ni - 1, dhc_next, 0.0)

        @pl.when(i == 0)
        def _():
            dw_ref[...] = jnp.zeros(dw_ref.shape, F32)
            db_ref[...] = jnp.zeros(db_ref.shape, F32)
            dlg_ref[...] = jnp.zeros(dlg_ref.shape, F32)
            dlb_ref[...] = jnp.zeros(dlb_ref.shape, F32)

            dwacc[...] = jnp.zeros(dwacc.shape, F32)

        db_ref[...] += jnp.sum(dhc, axis=0, keepdims=True)
        dlg_ref[...] += jnp.sum(dlg_rows, axis=0, keepdims=True)
        dlb_ref[...] += jnp.sum(dlb_rows, axis=0, keepdims=True)
        _fill_shifted(hext, hsh, tm)
        _fill_shifted(dext, dsh, tm)
        for c in range(tm // rc):
            r0 = c * rc
            dh = jnp.zeros((rc, CONV_W), F32)
            dhc_c = dext[r0:r0 + rc, :]
            for k in range(CONV_K):
                dh = dh + w_ref[k:k + 1, :] * _shifted_rows(dext, dsh, r0 + 30 - k, rc)
                prod = dhc_c * _shifted_rows(hext, hsh, r0 + 2 + k, rc)
                dwacc[k] += jnp.sum(prod.reshape(rc // SUBLANES, SUBLANES, CONV_W), axis=0)
            av = a_ref[r0:r0 + rc, :CONV_W]
            sg = _sigmoid(a_ref[r0:r0 + rc, CONV_W:])
            dp_ref[r0:r0 + rc, 0:CONV_W] = (dh * sg).astype(BF16)
            dp_ref[r0:r0 + rc, CONV_W:] = (dh * av * sg * (1.0 - sg)).astype(BF16)

        @pl.when(i == ni - 1)
        def _():
            dw_ref[...] = jnp.sum(dwacc[...], axis=1)

    row = lambda w: pl.BlockSpec((tm, w), lambda i: (i, 0))
    prev = lambda w: pl.BlockSpec((CONV_HALO, w), lambda i: (jnp.maximum(i * hb - 1, 0), 0))
    nxt = lambda w: pl.BlockSpec((CONV_HALO, w), lambda i: (jnp.minimum((i + 1) * hb, ni * hb - 1), 0))
    vec = pl.BlockSpec((1, CONV_W), lambda i: (0, 0))
    return _call(
        body, rider, name="bwd_conv", grid=(ni,),
        in_specs=[ANY, row(1024), prev(1024), row(CONV_W), nxt(CONV_W), row(CONV_W), nxt(CONV_W),
                  pl.BlockSpec((CONV_HALO, CONV_W), lambda i: (0, 0)), vec, vec],
        out_specs=[pl.BlockSpec((tm, 1024), lambda i: (i, 0)),
                   pl.BlockSpec((CONV_HALO, CONV_W), lambda i: (0, 0)), vec, vec, vec],
        out_shape=[jax.ShapeDtypeStruct((T, IN_COLS), BF16),
                   jax.ShapeDtypeStruct((CONV_HALO, CONV_W), F32),
                   jax.ShapeDtypeStruct((1, CONV_W), F32),
                   jax.ShapeDtypeStruct((1, CONV_W), F32),
                   jax.ShapeDtypeStruct((1, CONV_W), F32)],
        scratch_shapes=[pltpu.VMEM((tm + CONV_HALO, CONV_W), F32), pltpu.VMEM((tm + CONV_HALO, CONV_W), F32),
                        pltpu.VMEM((SUBLANES - 1, tm + CONV_HALO - SUBLANES, CONV_W), F32),
                        pltpu.VMEM((SUBLANES - 1, tm + CONV_HALO - SUBLANES, CONV_W), F32),
                        pltpu.VMEM((CONV_HALO, SUBLANES, CONV_W), F32)],
        input_output_aliases={0: 0},
        compiler_params=_params(56, 1),
    )(dproj, a, a, dco, dco, hc, hc, cw, lg, lb)


def _bwd_in_proj(dproj, w_in, x, dh1, g1, rider=None):
    T = x.shape[0]
    tm = 512

    def body(dp_ref, w_ref, x_ref, dh_ref, g_ref, gx_ref, dg_ref):
        i = pl.program_id(0)
        du = None
        for s in range(N_CHIPS):
            term = _dot_nt(dp_ref[:, IN_SHARD * s:IN_SHARD * (s + 1)], w_ref[s])
            du = term if du is None else du + term
        dx, dg_rows = _rms_bwd(du, x_ref[...], g_ref[...])
        gx_ref[...] = dh_ref[...] + dx
        dg = jnp.sum(dg_rows, axis=0, keepdims=True)

        @pl.when(i == 0)
        def _():
            dg_ref[...] = dg

        @pl.when(i > 0)
        def _():
            dg_ref[...] += dg

    row = lambda w: pl.BlockSpec((tm, w), lambda i: (i, 0))
    vec = pl.BlockSpec((1, D_MODEL), lambda i: (0, 0))
    return _call(
        body, rider, name="bwd_in_proj", grid=(T // tm,),
        in_specs=[row(IN_COLS), pl.BlockSpec((N_CHIPS, D_MODEL, IN_SHARD), lambda i: (0, 0, 0)),
                  row(D_MODEL), row(D_MODEL), vec],
        out_specs=[row(D_MODEL), vec],
        out_shape=[jax.ShapeDtypeStruct((T, D_MODEL), F32), jax.ShapeDtypeStruct((1, D_MODEL), F32)],
        compiler_params=_params(40, 1),
    )(dproj, w_in, x, dh1, g1)


def _wgrad(name, a_list, a_spec, b, b_spec, out_spec, out_shape, n_outer, T, tk, select=None, rider=None):
    def body(*refs):
        a_refs, b_ref, o_ref = refs[:len(a_list)], refs[len(a_list)], refs[len(a_list) + 1]
        kt = pl.program_id(1)

        @pl.when(kt == 0)
        def _():
            o_ref[...] = jnp.zeros(o_ref.shape, F32)

        bv = b_ref[...].reshape(b_ref.shape[-2:])
        if select is None:
            o_ref[...] += _dot_tn(a_refs[0][...].reshape(a_refs[0].shape[-2:]), bv).reshape(o_ref.shape)
        else:
            for n, a_ref in enumerate(a_refs):
                @pl.when(select(pl.program_id(0)) == n)
                def _():
                    o_ref[...] += _dot_tn(a_ref[...], bv).reshape(o_ref.shape)

    (res,), got = _call(
        body, rider, name=name, grid=(n_outer, T // tk),
        in_specs=[a_spec] * len(a_list) + [b_spec],
        out_specs=[out_spec], out_shape=[out_shape],
        compiler_params=_params(48, 2),
    )(*a_list, b)
    return (res, got) if rider is not None else res


def _mesh_pos():
    return lax.axis_index("x"), lax.axis_index("y"), lax.axis_index("c")


def _other_chips(x, y):
    return [((1 - x, y), 2 * (1 - x) + y), ((x, 1 - y), 2 * x + (1 - y)), ((1 - x, 1 - y), 2 * (1 - x) + (1 - y))]


def _all_gather_weights(shards):
    n = len(shards)

    def body(*refs):
        out_refs = refs[n:2 * n]
        send_sems, recv_sems = refs[2 * n:]
        x, y, c = _mesh_pos()
        me = 2 * x + y
        sibling = (x, y, 1 - c)
        chips = _other_chips(x, y)
        first, passed = [], []
        for t in range(n):
            half = shards[t].shape[1] // 2
            mine = out_refs[t].at[me, pl.ds(c * half, half)]
            for k, (chip, _) in enumerate(chips):
                cp = pltpu.make_async_remote_copy(
                    src_ref=mine, dst_ref=mine, send_sem=send_sems.at[t, k], recv_sem=recv_sems.at[t, k],
                    device_id=(*chip, c), device_id_type=MESH)
                cp.start()
                first.append(cp)
        for t in range(n):
            half = shards[t].shape[1] // 2
            rows = pl.ds(c * half, half)
            for k, (chip, s) in enumerate(chips):
                landed = out_refs[t].at[s, rows]
                pltpu.make_async_remote_copy(
                    src_ref=landed, dst_ref=landed, send_sem=send_sems.at[t, k], recv_sem=recv_sems.at[t, k],
                    device_id=(*chip, c), device_id_type=MESH).wait_recv()
                cp = pltpu.make_async_remote_copy(
                    src_ref=landed, dst_ref=landed, send_sem=send_sems.at[t, 3 + k], recv_sem=recv_sems.at[t, 3 + k],
                    device_id=sibling, device_id_type=MESH)
                cp.start()
                passed.append(cp)
        for t in range(n):
            half = shards[t].shape[1] // 2
            other = pl.ds((1 - c) * half, half)
            for k, (chip, s) in enumerate(chips):
                got = out_refs[t].at[s, other]
                pltpu.make_async_remote_copy(
                    src_ref=got, dst_ref=got, send_sem=send_sems.at[t, 3 + k], recv_sem=recv_sems.at[t, 3 + k],
                    device_id=sibling, device_id_type=MESH).wait_recv()
        for cp in first + passed:
            cp.wait_send()

    return pl.pallas_call(
        body, name="all_gather_weights",
        in_specs=[ANY] * n, out_specs=[ANY] * n,
        out_shape=[jax.ShapeDtypeStruct(s.shape, s.dtype) for s in shards],
        scratch_shapes=[pltpu.SemaphoreType.DMA((n, 6)), pltpu.SemaphoreType.DMA((n, 6))],
        input_output_aliases={t: t for t in range(n)},
    )(*shards)


def _exchange_rider(operands, out_shape, aliases, sem_shape, pairs):
    def start(ins, outs, sems):
        for send, _ in pairs(ins, outs, *sems):
            send.start()

    def finish(ins, outs, sems):
        for send, recv in pairs(ins, outs, *sems):
            send.wait_send()
            recv.wait_recv()

    sems = [pltpu.SemaphoreType.DMA(sem_shape), pltpu.SemaphoreType.DMA(sem_shape)]
    return _Rider(list(operands), list(out_shape), aliases, sems, start, finish)


def _remote(src, dst, send_sem, recv_sem, device):
    return pltpu.make_async_remote_copy(src_ref=src, dst_ref=dst, send_sem=send_sem, recv_sem=recv_sem,
                                        device_id=device, device_id_type=MESH)


def _fetch_rider(bufs):
    def pairs(ins, outs, send_sems, recv_sems):
        x, y, c = _mesh_pos()
        res = []
        for t, buf in enumerate(bufs):
            rows = pl.ds(c * (buf.shape[1] // 2), buf.shape[1] // 2)
            mine = outs[t].at[2 * x + y, rows]
            for k, (chip, s) in enumerate(_other_chips(x, y)):
                landed = outs[t].at[s, rows]
                res.append((_remote(mine, mine, send_sems.at[t, k], recv_sems.at[t, k], (*chip, c)),
                            _remote(landed, landed, send_sems.at[t, k], recv_sems.at[t, k], (*chip, c))))
        return res

    shapes = [jax.ShapeDtypeStruct(b.shape, b.dtype) for b in bufs]
    return _exchange_rider(bufs, shapes, {t: t for t in range(len(bufs))}, (len(bufs), 3), pairs)


def _forward_rider(bufs):
    def pairs(ins, outs, send_sems, recv_sems):
        x, y, c = _mesh_pos()
        res = []
        for t, buf in enumerate(bufs):
            half = buf.shape[1] // 2
            for k, (_, s) in enumerate(_other_chips(x, y)):
                landed = outs[t].at[s, pl.ds(c * half, half)]
                theirs = outs[t].at[s, pl.ds((1 - c) * half, half)]
                res.append((_remote(landed, landed, send_sems.at[t, k], recv_sems.at[t, k], (x, y, 1 - c)),
                            _remote(theirs, theirs, send_sems.at[t, k], recv_sems.at[t, k], (x, y, 1 - c))))
        return res

    shapes = [jax.ShapeDtypeStruct(b.shape, b.dtype) for b in bufs]
    return _exchange_rider(bufs, shapes, {t: t for t in range(len(bufs))}, (len(bufs), 3), pairs)


def _pair_exchange_rider(grads):
    def pairs(ins, outs, send_sems, recv_sems):
        x, y, c = _mesh_pos()
        res = []
        for t, g in enumerate(grads):
            half = g.shape[1] // 2
            cp = _remote(ins[t].at[:, pl.ds((1 - c) * half, half), :], outs[t], send_sems.at[t], recv_sems.at[t],
                         (x, y, 1 - c))
            res.append((cp, cp))
        return res

    shapes = [jax.ShapeDtypeStruct((N_CHIPS, g.shape[1] // 2, g.shape[2]), F32) for g in grads]
    return _exchange_rider(grads, shapes, {}, (len(grads),), pairs)


def _chip_exchange_rider(sums):
    def pairs(ins, outs, send_sems, recv_sems):
        x, y, c = _mesh_pos()
        res = []
        for t in range(len(sums)):
            for k, (chip, s) in enumerate(_other_chips(x, y)):
                cp = _remote(ins[t].at[s], outs[t].at[k], send_sems.at[t, k], recv_sems.at[t, k], (*chip, c))
                res.append((cp, cp))
        return res

    shapes = [jax.ShapeDtypeStruct((3,) + p.shape[1:], p.dtype) for p in sums]
    return _exchange_rider(sums, shapes, {}, (len(sums), 3), pairs)


def _pair_gather_rider(fulls):
    def pairs(ins, outs, send_sems, recv_sems):
        x, y, c = _mesh_pos()
        res = []
        for t, f in enumerate(fulls):
            half = f.shape[0] // 2
            mine = outs[t].at[pl.ds(c * half, half)]
            theirs = outs[t].at[pl.ds((1 - c) * half, half)]
            res.append((_remote(mine, mine, send_sems.at[t], recv_sems.at[t], (x, y, 1 - c)),
                        _remote(theirs, theirs, send_sems.at[t], recv_sems.at[t], (x, y, 1 - c))))
        return res

    shapes = [jax.ShapeDtypeStruct(f.shape, F32) for f in fulls]
    return _exchange_rider(fulls, shapes, {t: t for t in range(len(fulls))}, (len(fulls),), pairs)


def _alone(name, rider):
    return _call(lambda: None, rider, name=name)()[1]


def _all_reduce_small(pack):
    rows = pack.shape[0]

    def body(p_ref, o_ref, buf, send_sems, recv_sems):
        x, y, c = _mesh_pos()
        me = 4 * x + 2 * y + c
        buf[0] = p_ref[...]
        copies = []
        for k in range(1, 8):
            peer = (x ^ (k >> 2), y ^ ((k >> 1) & 1), c ^ (k & 1))
            cp = pltpu.make_async_remote_copy(
                src_ref=p_ref, dst_ref=buf.at[k], send_sem=send_sems.at[k - 1], recv_sem=recv_sems.at[k - 1],
                device_id=peer, device_id_type=MESH)
            cp.start()
            copies.append(cp)
        for cp in copies:
            cp.wait()
        total = buf[me]
        for dev in range(1, 8):
            total = total + buf[me ^ dev]
        o_ref[...] = total

    return pl.pallas_call(
        body, name="all_reduce_small",
        in_specs=[VMEM_FULL], out_specs=VMEM_FULL,
        out_shape=jax.ShapeDtypeStruct(pack.shape, F32),
        scratch_shapes=[pltpu.VMEM((8, rows, LANES), F32),
                        pltpu.SemaphoreType.DMA((7,)), pltpu.SemaphoreType.DMA((7,))],
    )(pack)


def _row_block(rows):
    if rows <= 512:
        return rows
    for rb in (256, 352):
        if rows % rb == 0:
            return rb
    raise ValueError(f"no row block for {rows} rows")


def _place(name, w, pos, dtype):
    R, C = w.shape
    rb = _row_block(R)

    def body(pos_ref, w_ref, o_ref):
        del pos_ref
        o_ref[0] = w_ref[...].astype(dtype)

    return pl.pallas_call(
        body, name=name,
        grid_spec=pltpu.PrefetchScalarGridSpec(
            num_scalar_prefetch=1, grid=(R // rb,),
            in_specs=[pl.BlockSpec((rb, C), lambda r, p: (r, 0))],
            out_specs=pl.BlockSpec((1, rb, C), lambda r, p: (p[0], r, 0))),
        out_shape=jax.ShapeDtypeStruct((N_CHIPS, R, C), dtype),
        compiler_params=_params(32, 1),
    )(pos, w)


def _pair_sum(name, g, got, pos):
    S, R, C = g.shape
    half = R // 2
    rb = _row_block(half)
    nh = half // rb

    def body(pos_ref, a_ref, b_ref, o_ref):
        del pos_ref
        o_ref[...] = (a_ref[...] + b_ref[...]).astype(BF16)

    spec = pl.BlockSpec((1, rb, C), lambda s, r, p: (s, r, 0))
    return pl.pallas_call(
        body, name=name,
        grid_spec=pltpu.PrefetchScalarGridSpec(
            num_scalar_prefetch=1, grid=(S, nh),
            in_specs=[pl.BlockSpec((1, rb, C), lambda s, r, p: (s, p[1] * nh + r, 0)), spec],
            out_specs=spec),
        out_shape=jax.ShapeDtypeStruct((S, half, C), BF16), compiler_params=_params(32, 2),
    )(pos, g, got)


def _chip_sum(name, pairs, got, pos):
    _, half, C = pairs.shape
    rb = _row_block(half)
    nh = half // rb

    def body(pos_ref, a_ref, g_ref, o_ref):
        del pos_ref
        o_ref[...] = ((a_ref[0].astype(F32) + g_ref[0].astype(F32)) + g_ref[1].astype(F32)) + g_ref[2].astype(F32)

    return pl.pallas_call(
        body, name=name,
        grid_spec=pltpu.PrefetchScalarGridSpec(
            num_scalar_prefetch=1, grid=(nh,),
            in_specs=[pl.BlockSpec((1, rb, C), lambda r, p: (p[0], r, 0)),
                      pl.BlockSpec((3, rb, C), lambda r, p: (0, r, 0))],
            out_specs=pl.BlockSpec((rb, C), lambda r, p: (p[1] * nh + r, 0))),
        out_shape=jax.ShapeDtypeStruct((2 * half, C), F32), compiler_params=_params(32, 1),
    )(pos, pairs, got)


def _adamw(name, w, g, m, v):
    R, C = w.shape
    rb = _row_block(R)
    c1 = 1.0 - ADAM_B1 ** ADAM_STEP
    c2 = 1.0 - ADAM_B2 ** ADAM_STEP

    def body(w_ref, g_ref, m_ref, v_ref, d_ref, nm_ref, nv_ref):
        gv = g_ref[...]
        nm = ADAM_B1 * m_ref[...] + (1.0 - ADAM_B1) * gv
        nv = ADAM_B2 * v_ref[...] + (1.0 - ADAM_B2) * (gv * gv)
        nm_ref[...] = nm
        nv_ref[...] = nv
        d_ref[...] = -ADAM_LR * ((nm / c1) / (jnp.sqrt(nv / c2) + ADAM_EPS) + ADAM_WD * w_ref[...])

    spec = pl.BlockSpec((rb, C), lambda r: (r, 0))
    sds = jax.ShapeDtypeStruct(w.shape, F32)
    return pl.pallas_call(
        body, name=name, grid=(R // rb,), in_specs=[spec] * 4, out_specs=[spec] * 3,
        out_shape=[sds, sds, sds], compiler_params=_params(40, 1),
    )(w, g, m, v)


def _rel_index():
    m = np.arange(2 * ATT_BLK)
    off = np.where(m < ATT_BLK, m, m - 2 * ATT_BLK)
    rel = np.stack([ATT_BLK * d - off for d in range(N_ATT_TILES)])
    return np.clip(rel, -MAX_REL, MAX_REL) + MAX_REL


def _local_step(x, tgt, g1, w_in, cw, cb, lg, lb, rel, w_out, g2, g3, w_up, fw, fb, w_down, g4, pos=None):
    T = x.shape[0]
    dist = pos is not None
    idx = _rel_index()
    vec = jnp.transpose(rel[:, idx], (1, 0, 2)).reshape(N_ATT_TILES * N_HEADS, 1, 2 * ATT_BLK)
    bias = _bias_tiles(vec)

    (u, a, qkv), got = _fwd_in_proj(x, g1, w_in, _fetch_rider([w_out, w_down]) if dist else None)
    if dist:
        w_out, w_down = got
    (co, hc), got = _fwd_conv(a, cw, cb, lg, lb, _merge_riders(_forward_rider([w_out, w_down]),
                                                               _fetch_rider([w_up])) if dist else None)
    if dist:
        w_out, w_down, w_up = got
    (ao, lse), got = _fwd_attn(qkv, bias, _forward_rider([w_up]) if dist else None)
    if dist:
        (w_up,) = got
        w_out, w_down = w_out.reshape(D_MODEL, D_MODEL), w_down.reshape(D_FF, D_MODEL)
    mixed, h1, u2 = _fwd_out_proj(co, ao, w_out, x, g2, g3)
    hf, pre, fp = _fwd_ffn(u2, w_up, fw, fb, w_down)
    loss, dy, df, dg4 = _fwd_loss(fp, h1, tgt, g4)

    tk = 1024
    du2p, dhf, act, dfw_g, dfw_v = _bwd_ffn(df, hf, pre, w_up, fw, w_down)
    gw_up = _wgrad(
        "wgrad_up", [u2], pl.BlockSpec((tk, D_MODEL), lambda s, k: (k, 0)),
        dhf, pl.BlockSpec((1, tk, FF_SHARD), lambda s, k: (s // 2, k, s % 2)),
        pl.BlockSpec((1, D_MODEL, FF_SHARD), lambda s, k: (s, 0, 0)),
        jax.ShapeDtypeStruct((N_CHIPS, D_MODEL, FF_SHARD), F32), N_CHIPS, T, tk)
    gw_down = _wgrad(
        "wgrad_down", [act], pl.BlockSpec((tk, FF_SHARD), lambda s, k: (k, s)),
        df, pl.BlockSpec((tk, D_MODEL), lambda s, k: (k, 0)),
        pl.BlockSpec((FF_SHARD, D_MODEL), lambda s, k: (s, 0)),
        jax.ShapeDtypeStruct((D_FF, D_MODEL), F32), 2, T, tk).reshape(N_CHIPS, D_FF // N_CHIPS, D_MODEL)
    (dh1, dmx, dco, dao, dg3, dg2), _ = _bwd_mid(du2p, dy, h1, mixed, g3, g2, w_out)
    gw_out = _wgrad(
        "wgrad_out", [co, ao], pl.BlockSpec((tk, CONV_W), lambda s, k: (k, 0)),
        dmx, pl.BlockSpec((tk, D_MODEL), lambda s, k: (k, 0)),
        pl.BlockSpec((CONV_W, D_MODEL), lambda s, k: (s, 0)),
        jax.ShapeDtypeStruct((D_MODEL, D_MODEL), F32), 2, T, tk,
        select=lambda s: s, rider=_pair_exchange_rider([gw_up, gw_down]) if dist else None)
    if dist:
        gw_out, got = gw_out
        p_up = _pair_sum("pair_sum_w_up", gw_up, got[0], pos)
        p_down = _pair_sum("pair_sum_w_down", gw_down, got[1], pos)
    gw_out = gw_out.reshape(N_CHIPS, D_MODEL // N_CHIPS, D_MODEL)
    (dproj, dsacc), got = _bwd_attn(
        qkv, ao, dao, lse, bias,
        _merge_riders(_chip_exchange_rider([p_up, p_down]), _pair_exchange_rider([gw_out])) if dist else None)
    if dist:
        gw_up = _chip_sum("chip_sum_w_up", p_up, got[0], pos)
        gw_down = _chip_sum("chip_sum_w_down", p_down, got[1], pos)
        p_out = _pair_sum("pair_sum_w_out", gw_out, got[2], pos)
    (dproj, dcw, dcb, dlg, dlb), got = _bwd_conv(
        dproj, a, dco, hc, cw, lg, lb,
        _merge_riders(_pair_gather_rider([gw_up, gw_down]), _chip_exchange_rider([p_out])) if dist else None)
    if dist:
        gw_up, gw_down = got[:2]
        gw_out = _chip_sum("chip_sum_w_out", p_out, got[2], pos)
    gw_in = _wgrad(
        "wgrad_in", [u], pl.BlockSpec((tk, D_MODEL), lambda s, k: (k, 0)),
        dproj, pl.BlockSpec((tk, IN_SHARD), lambda s, k: (k, s)),
        pl.BlockSpec((1, D_MODEL, IN_SHARD), lambda s, k: (s, 0, 0)),
        jax.ShapeDtypeStruct((N_CHIPS, D_MODEL, IN_SHARD), F32), N_CHIPS, T, tk)
    if dist:
        got = _alone("pair_exchange_w_in", _merge_riders(_pair_exchange_rider([gw_in]), _pair_gather_rider([gw_out])))
        p_in, gw_out = _pair_sum("pair_sum_w_in", gw_in, got[0], pos), got[1]
    (gx, dg1), _ = _bwd_in_proj(dproj, w_in, x, dh1, g1)
    (diag,), got = _diag_sums(dsacc, _chip_exchange_rider([p_in]) if dist else None)
    if dist:
        (gw_in,) = _alone("pair_gather_w_in", _pair_gather_rider([_chip_sum("chip_sum_w_in", p_in, got[0], pos)]))

    diag = diag.reshape(N_ATT_TILES, N_HEADS, 2 * ATT_BLK)
    onehot = np.zeros((N_ATT_TILES, 2 * ATT_BLK, 2 * MAX_REL + 1), np.float32)
    for d in range(N_ATT_TILES):
        onehot[d, np.arange(2 * ATT_BLK), idx[d]] = 1.0
    drel = jnp.einsum("dhm,dmr->hr", diag, jnp.asarray(onehot), precision=lax.Precision.HIGHEST)

    small = dict(norm_mix_pre=dg1, conv_dw_w=dcw[:CONV_K], conv_dw_b=dcb, conv_ln_g=dlg, conv_ln_b=dlb,
                 rel_bias=drel, norm_mix_post=dg2, norm_ffn_pre=dg3,
                 ffn_dw_w=jnp.concatenate([dfw_g[0, :3], dfw_g[1, :3], dfw_v[0, :3], dfw_v[1, :3]], axis=1),
                 ffn_dw_b=jnp.concatenate([dfw_g[0, 3:4], dfw_g[1, 3:4], dfw_v[0, 3:4], dfw_v[1, 3:4]], axis=1),
                 norm_ffn_post=dg4)
    return loss, gx, small, dict(w_in=gw_in, w_out=gw_out, w_up=gw_up, w_down=gw_down)


SMALL_ORDER = ["norm_mix_pre", "conv_dw_b", "conv_ln_g", "conv_ln_b", "rel_bias", "norm_mix_post",
               "norm_ffn_pre", "ffn_dw_b", "norm_ffn_post", "conv_dw_w", "ffn_dw_w"]


def _pack(parts):
    rows = []
    for p in parts:
        width = -(-p.shape[1] // LANES) * LANES
        rows.append(jnp.pad(p, ((0, 0), (0, width - p.shape[1]))).reshape(-1, LANES))
    packed = jnp.concatenate(rows, axis=0)
    pad = -packed.shape[0] % 8
    return jnp.pad(packed, ((0, pad), (0, 0)))


def _unpack(packed, shapes):
    out, r = [], 0
    for shp in shapes:
        width = -(-shp[1] // LANES) * LANES
        n = shp[0] * width // LANES
        out.append(packed[r:r + n].reshape(shp[0], width)[:, :shp[1]])
        r += n
    return out


WEIGHTS = ["norm_mix_pre", "w_in", "conv_dw_w", "conv_dw_b", "conv_ln_g", "conv_ln_b", "rel_bias", "w_out",
           "norm_mix_post", "norm_ffn_pre", "w_up", "ffn_dw_w", "ffn_dw_b", "w_down", "norm_ffn_post"]
BIG = ["w_in", "w_out", "w_up", "w_down"]


def kernel(x, norm_mix_pre, w_in, conv_dw_w, conv_dw_b, conv_ln_g, conv_ln_b, rel_bias, w_out, norm_mix_post, norm_ffn_pre, w_up, ffn_dw_w, ffn_dw_b, w_down, norm_ffn_post, loss_target, m_norm_mix_pre, m_w_in, m_conv_dw_w, m_conv_dw_b, m_conv_ln_g, m_conv_ln_b, m_rel_bias, m_w_out, m_norm_mix_post, m_norm_ffn_pre, m_w_up, m_ffn_dw_w, m_ffn_dw_b, m_w_down, m_norm_ffn_post, v_norm_mix_pre, v_w_in, v_conv_dw_w, v_conv_dw_b, v_conv_ln_g, v_conv_ln_b, v_rel_bias, v_w_out, v_norm_mix_post, v_norm_ffn_pre, v_w_up, v_ffn_dw_w, v_ffn_dw_b, v_w_down, v_norm_ffn_post):
    args = locals()
    w = {n: args[n][0] for n in WEIGHTS}
    m = {n: args["m_" + n][0] for n in WEIGHTS}
    v = {n: args["v_" + n][0] for n in WEIGHTS}
    for d in (w, m, v):
        d["rel_bias"] = d["rel_bias"].reshape(N_HEADS, 2 * MAX_REL + 1)
        for n in ("norm_mix_pre", "conv_dw_b", "conv_ln_g", "conv_ln_b", "norm_mix_post", "norm_ffn_pre",
                  "ffn_dw_b", "norm_ffn_post"):
            d[n] = d[n].reshape(1, -1)
    shard = 2 * lax.axis_index("x") + lax.axis_index("y")

    cw_sh = jnp.pad(w["conv_dw_w"], ((0, CONV_HALO - CONV_K), (0, 0)))
    fw_sh = jnp.pad(w["ffn_dw_w"], ((0, FF_HALO - 3), (0, 0)))
    pos = jnp.stack([shard, lax.axis_index("c")]).astype(jnp.int32)
    bufs = {n: _place("place_" + n, w[n], pos, BF16) for n in BIG}
    w_in_f, cw_f, fw_f = _all_gather_weights(
        [bufs["w_in"], _place("place_conv_dw_w", cw_sh, pos, F32), _place("place_ffn_dw_w", fw_sh, pos, F32)])
    cw_full = jnp.transpose(cw_f, (1, 0, 2)).reshape(CONV_HALO, CONV_W)

    loss, gx, small, big = _local_step(
        x[0], loss_target[0], w["norm_mix_pre"], w_in_f, cw_full, w["conv_dw_b"], w["conv_ln_g"],
        w["conv_ln_b"], w["rel_bias"], bufs["w_out"], w["norm_mix_post"],
        w["norm_ffn_pre"], bufs["w_up"], fw_f, w["ffn_dw_b"].reshape(N_CHIPS, 1, FF_SHARD),
        bufs["w_down"], w["norm_ffn_post"], pos)
    grads, deltas, new_m, new_v = {}, {}, {}, {}
    for n in BIG:
        grads[n] = big[n]
        deltas[n], new_m[n], new_v[n] = _adamw("adamw_" + n, w[n], big[n], m[n], v[n])

    gsum = _all_reduce_small(_pack([small[n] for n in SMALL_ORDER]))
    shapes = [small[n].shape for n in SMALL_ORDER]
    gs = dict(zip(SMALL_ORDER, _unpack(gsum, shapes)))
    gs["conv_dw_w"] = lax.dynamic_slice_in_dim(gs["conv_dw_w"], shard * LANES, LANES, axis=1)
    gs["ffn_dw_w"] = lax.dynamic_slice_in_dim(gs["ffn_dw_w"], shard * FF_SHARD, FF_SHARD, axis=1)
    shapes = [gs[n].shape for n in SMALL_ORDER]
    d_p, m_p, v_p = _adamw("adamw_small", _pack([w[n] for n in SMALL_ORDER]), _pack([gs[n] for n in SMALL_ORDER]),
                           _pack([m[n] for n in SMALL_ORDER]), _pack([v[n] for n in SMALL_ORDER]))
    for dst, packed in ((deltas, d_p), (new_m, m_p), (new_v, v_p)):
        dst.update(zip(SMALL_ORDER, _unpack(packed, shapes)))
    grads.update(gs)

    total = lax.psum(loss[0, 0], ("x", "y", "c"))
    outs = [total, gx[None]]
    for group in (grads, deltas, new_m, new_v):
        outs += [group[n].reshape(args[n].shape) for n in WEIGHTS]
    return tuple(outs)
```

```python
import functools
import math
from typing import Callable, NamedTuple

import numpy as np
import jax
import jax.numpy as jnp
from jax import lax
from jax.experimental import pallas as pl
from jax.experimental.pallas import tpu as pltpu

F32 = jnp.float32
BF16 = jnp.bfloat16

D_MODEL = 1024
CONV_W = 512
ATTN_W = 512
N_HEADS = 8
HEAD_DIM = 64
CHUNK = 64
N_LEFT = 8
MAX_REL = 128
CONV_K = 31
CONV_HALO = 32
D_FF = 2816
FF_SHARD = 1408
IN_COLS = 2560
IN_SHARD = 640
EPS = 1e-6
NEG_INF = -1e30
ATT_BLK = 256
N_ATT_TILES = 3
LANES = 128
SUBLANES = 8
N_CHIPS = 4

ADAM_LR = 0.001
ADAM_B1 = 0.9
ADAM_B2 = 0.999
ADAM_EPS = 1e-08
ADAM_WD = 0.01
ADAM_STEP = 10

MESH = pl.DeviceIdType.MESH
ANY = pl.BlockSpec(memory_space=pl.ANY)
VMEM_FULL = pl.BlockSpec(memory_space=pltpu.VMEM)


def _params(vmem_mb, n_grid=0):
    sem = ("arbitrary",) * n_grid if n_grid else None
    return pltpu.CompilerParams(dimension_semantics=sem, vmem_limit_bytes=vmem_mb << 20)


class _Rider(NamedTuple):
    operands: list
    out_shape: list
    aliases: dict
    sems: list
    start: Callable
    finish: Callable


def _merge_riders(a, b):
    ia, oa, sa = len(a.operands), len(a.out_shape), len(a.sems)

    def start(ins, outs, sems):
        a.start(ins[:ia], outs[:oa], sems[:sa])
        b.start(ins[ia:], outs[oa:], sems[sa:])

    def finish(ins, outs, sems):
        a.finish(ins[:ia], outs[:oa], sems[:sa])
        b.finish(ins[ia:], outs[oa:], sems[sa:])

    aliases = {**a.aliases, **{k + ia: v + oa for k, v in b.aliases.items()}}
    return _Rider(a.operands + b.operands, a.out_shape + b.out_shape, aliases, a.sems + b.sems, start, finish)


PIN_BYTES = 1 << 20


def _big(a):
    return math.prod(a.shape) * jnp.dtype(a.dtype).itemsize >= PIN_BYTES


def _pin_args(args):
    return [pltpu.with_memory_space_constraint(a, pltpu.HBM) if _big(a) else a for a in args]


def _call(body, rider, *, grid=(), in_specs=(), out_specs=(), out_shape=(), scratch_shapes=(),
          input_output_aliases=None, **kwargs):
    in_specs, out_specs = list(in_specs), list(out_specs)
    out_shape = [pltpu.HBM(s.shape, s.dtype) if _big(s) else s for s in out_shape]
    scratch, aliases = list(scratch_shapes), dict(input_output_aliases or {})
    if rider is None:
        plain = pl.pallas_call(body, grid=grid, in_specs=in_specs, out_specs=out_specs, out_shape=out_shape,
                               scratch_shapes=scratch, input_output_aliases=aliases, **kwargs)
        return lambda *args: (plain(*_pin_args(args)), [])
    n_in, n_out, n_scr = len(in_specs), len(out_specs), len(scratch)
    r_in, r_out = len(rider.operands), len(rider.out_shape)

    def carried(*refs):
        ins, r_ins, refs = refs[:n_in], refs[n_in:n_in + r_in], refs[n_in + r_in:]
        outs, r_outs, refs = refs[:n_out], refs[n_out:n_out + r_out], refs[n_out + r_out:]
        scr, r_sems = refs[:n_scr], refs[n_scr:]
        if not grid:
            rider.start(r_ins, r_outs, r_sems)
            body(*ins, *outs, *scr)
            rider.finish(r_ins, r_outs, r_sems)
            return
        at = [pl.program_id(d) for d in range(len(grid))]
        first = functools.reduce(jnp.logical_and, [p == 0 for p in at])
        last = functools.reduce(jnp.logical_and, [p == n - 1 for p, n in zip(at, grid)])

        @pl.when(first)
        def _():
            rider.start(r_ins, r_outs, r_sems)

        body(*ins, *outs, *scr)

        @pl.when(last)
        def _():
            rider.finish(r_ins, r_outs, r_sems)

    aliases.update({n_in + k: n_out + v for k, v in rider.aliases.items()})
    both = pl.pallas_call(carried, grid=grid, in_specs=in_specs + [ANY] * r_in, out_specs=out_specs + [ANY] * r_out,
                          out_shape=out_shape + rider.out_shape, scratch_shapes=scratch + rider.sems,
                          input_output_aliases=aliases, **kwargs)

    def run(*args):
        res = both(*_pin_args(args), *rider.operands)
        return res[:n_out], res[n_out:]

    return run


def _sigmoid(v):
    return 1.0 / (1.0 + jnp.exp(-v))


def _dot(a, b):
    return jnp.dot(a, b, preferred_element_type=F32)


def _dot_nt(a, b):
    return lax.dot_general(a, b, (((1,), (1,)), ((), ())), preferred_element_type=F32)


def _dot_tn(a, b):
    return lax.dot_general(a, b, (((0,), (0,)), ((), ())), preferred_element_type=F32)


def _rms_fwd(v, g):
    r = lax.rsqrt(jnp.mean(v * v, axis=-1, keepdims=True) + EPS)
    return v * r * g, r


def _rms_bwd(dy, v, g):
    r = lax.rsqrt(jnp.mean(v * v, axis=-1, keepdims=True) + EPS)
    vh = v * r
    dvh = dy * g
    dv = r * (dvh - vh * jnp.mean(dvh * vh, axis=-1, keepdims=True))
    return dv, dy * vh


def _fwd_in_proj(x, g1, w_in, rider=None):
    T = x.shape[0]
    tm = 512

    def body(x_ref, g_ref, w_ref, u_ref, a_ref, qkv_ref):
        u, _ = _rms_fwd(x_ref[...], g_ref[...])
        u = u.astype(BF16)
        u_ref[...] = u
        for s in range(N_CHIPS):
            y = _dot(u, w_ref[s])
            lo, hi = IN_SHARD * s, IN_SHARD * (s + 1)
            if hi <= 1024:
                a_ref[:, lo:hi] = y
            elif lo >= 1024:
                qkv_ref[:, lo - 1024:hi - 1024] = y.astype(BF16)
            else:
                a_ref[:, lo:1024] = y[:, :1024 - lo]
                qkv_ref[:, 0:hi - 1024] = y[:, 1024 - lo:].astype(BF16)

    return _call(
        body, rider, name="fwd_in_proj", grid=(T // tm,),
        in_specs=[pl.BlockSpec((tm, D_MODEL), lambda i: (i, 0)),
                  pl.BlockSpec((1, D_MODEL), lambda i: (0, 0)),
                  pl.BlockSpec((N_CHIPS, D_MODEL, IN_SHARD), lambda i: (0, 0, 0))],
        out_specs=[pl.BlockSpec((tm, D_MODEL), lambda i: (i, 0)),
                   pl.BlockSpec((tm, 1024), lambda i: (i, 0)),
                   pl.BlockSpec((tm, 1536), lambda i: (i, 0))],
        out_shape=[jax.ShapeDtypeStruct((T, D_MODEL), BF16),
                   jax.ShapeDtypeStruct((T, 1024), F32),
                   jax.ShapeDtypeStruct((T, 1536), BF16)],
        compiler_params=_params(40, 1),
    )(x, g1, w_in)


def _fill_shifted(ext, shifted, tm):
    n = tm + CONV_HALO - SUBLANES
    for j in range(1, SUBLANES):
        shifted[j - 1] = ext[j:j + n, :]


def _shifted_rows(ext, shifted, start, rows):
    j = start % SUBLANES
    if j == 0:
        return ext[start:start + rows, :]
    return shifted[j - 1, start - j:start - j + rows, :]


def _fwd_conv(a, cw, cb, lg, lb, rider=None):
    T = a.shape[0]
    tm = 512
    rc = 64

    def body(a_ref, w_ref, b_ref, lg_ref, lb_ref, co_ref, hc_ref, hext, hsh):
        i = pl.program_id(0)

        @pl.when(i == 0)
        def _():
            hext[0:CONV_HALO, :] = jnp.zeros((CONV_HALO, CONV_W), F32)

        @pl.when(i > 0)
        def _():
            hext[0:CONV_HALO, :] = hext[tm:tm + CONV_HALO, :]

        hext[CONV_HALO:CONV_HALO + tm, :] = a_ref[:, :CONV_W] * _sigmoid(a_ref[:, CONV_W:])
        _fill_shifted(hext, hsh, tm)
        for c in range(tm // rc):
            acc = jnp.zeros((rc, CONV_W), F32)
            for k in range(CONV_K):
                acc = acc + w_ref[k:k + 1, :] * _shifted_rows(hext, hsh, c * rc + 2 + k, rc)
            hc = acc + b_ref[...]
            hc_ref[c * rc:(c + 1) * rc, :] = hc
            mu = jnp.mean(hc, axis=-1, keepdims=True)
            xc = hc - mu
            var = jnp.mean(xc * xc, axis=-1, keepdims=True)
            z = xc * lax.rsqrt(var + EPS) * lg_ref[...] + lb_ref[...]
            co_ref[c * rc:(c + 1) * rc, :] = (z * _sigmoid(z)).astype(BF16)

    return _call(
        body, rider, name="fwd_conv", grid=(T // tm,),
        in_specs=[pl.BlockSpec((tm, 1024), lambda i: (i, 0)),
                  pl.BlockSpec((CONV_HALO, CONV_W), lambda i: (0, 0)),
                  pl.BlockSpec((1, CONV_W), lambda i: (0, 0)),
                  pl.BlockSpec((1, CONV_W), lambda i: (0, 0)),
                  pl.BlockSpec((1, CONV_W), lambda i: (0, 0))],
        out_specs=[pl.BlockSpec((tm, CONV_W), lambda i: (i, 0)),
                   pl.BlockSpec((tm, CONV_W), lambda i: (i, 0))],
        out_shape=[jax.ShapeDtypeStruct((T, CONV_W), BF16),
                   jax.ShapeDtypeStruct((T, CONV_W), F32)],
        scratch_shapes=[pltpu.VMEM((tm + CONV_HALO, CONV_W), F32),
                        pltpu.VMEM((SUBLANES - 1, tm + CONV_HALO - SUBLANES, CONV_W), F32)],
        compiler_params=_params(40, 1),
    )(a, cw, cb, lg, lb)


def _row_skew(v, sign):
    rows, width = v.shape
    row = lax.broadcasted_iota(jnp.int32, (rows, 1), 0)
    for b in range(int(math.log2(rows))):
        shift = (1 << b) if sign > 0 else width - (1 << b)
        v = jnp.where(((row >> b) & 1) == 1, pltpu.roll(v, shift, 1), v)
    return v


def _att_visible(d):
    rq = lax.broadcasted_iota(jnp.int32, (ATT_BLK, ATT_BLK), 0) // CHUNK
    ck = lax.broadcasted_iota(jnp.int32, (ATT_BLK, ATT_BLK), 1) // CHUNK
    slack = ATT_BLK
    above = jnp.where(d == 0, 0, slack)
    below = jnp.where(d == 2, 0, slack)
    return (ck <= rq + above) & (ck >= rq - below)


def _bias_tiles(rel, rider=None):
    vec = jnp.transpose(rel[:, _rel_index()], (1, 0, 2)).reshape(N_ATT_TILES * N_HEADS, 1, 2 * ATT_BLK)

    def body(v_ref, o_ref):
        visible = _att_visible(pl.program_id(0))
        for h in range(N_HEADS):
            full = _row_skew(jnp.broadcast_to(v_ref[h], (ATT_BLK, 2 * ATT_BLK)), 1)
            o_ref[h] = jnp.where(visible, full[:, :ATT_BLK], NEG_INF)

    return _call(
        body, rider, name="bias_tiles", grid=(N_ATT_TILES,),
        in_specs=[pl.BlockSpec((N_HEADS, 1, 2 * ATT_BLK), lambda d: (d, 0, 0))],
        out_specs=[pl.BlockSpec((N_HEADS, ATT_BLK, ATT_BLK), lambda d: (d, 0, 0))],
        out_shape=[jax.ShapeDtypeStruct((N_ATT_TILES * N_HEADS, ATT_BLK, ATT_BLK), F32)],
        compiler_params=_params(32, 1),
    )(vec)


def _diag_sums(ds, rider=None):
    def body(d_ref, o_ref):
        wide = jnp.concatenate([d_ref[0], jnp.zeros((ATT_BLK, ATT_BLK), F32)], axis=1)
        o_ref[0] = jnp.sum(_row_skew(wide, -1), axis=0, keepdims=True)

    return _call(
        body, rider, name="diag_sums", grid=(N_ATT_TILES * N_HEADS,),
        in_specs=[pl.BlockSpec((1, ATT_BLK, ATT_BLK), lambda n: (n, 0, 0))],
        out_specs=[pl.BlockSpec((1, 1, 2 * ATT_BLK), lambda n: (n, 0, 0))],
        out_shape=[jax.ShapeDtypeStruct((N_ATT_TILES * N_HEADS, 1, 2 * ATT_BLK), F32)],
        compiler_params=_params(16, 1),
    )(ds)


def _head_mask(h):
    lane = lax.broadcasted_iota(jnp.int32, (1, LANES), 1)
    return (lane // HEAD_DIM) == (h % 2)


def _fwd_attn(qkv, bias, rider=None):
    T = qkv.shape[0]
    nb = T // ATT_BLK
    scale = HEAD_DIM ** -0.5

    def body(q_ref, k0_ref, k1_ref, k2_ref, v0_ref, v1_ref, v2_ref, b_ref, o_ref, lse_ref):
        i = pl.program_id(0)

        @pl.when(i >= N_ATT_TILES - 1)
        def _():
            block(i, False, q_ref, k0_ref, k1_ref, k2_ref, v0_ref, v1_ref, v2_ref, b_ref, o_ref, lse_ref)

        @pl.when(i < N_ATT_TILES - 1)
        def _():
            block(i, True, q_ref, k0_ref, k1_ref, k2_ref, v0_ref, v1_ref, v2_ref, b_ref, o_ref, lse_ref)

    def block(i, hide_absent, q_ref, k0_ref, k1_ref, k2_ref, v0_ref, v1_ref, v2_ref, b_ref, o_ref, lse_ref):
        k_refs = (k0_ref, k1_ref, k2_ref)
        v_refs = (v0_ref, v1_ref, v2_ref)
        lane = lax.broadcasted_iota(jnp.int32, (1, LANES), 1)
        lse_tile = jnp.zeros((ATT_BLK, LANES), F32)
        for g in range(N_HEADS // 2):
            cols = slice(g * LANES, (g + 1) * LANES)
            qg = q_ref[:, cols] * scale
            og = jnp.zeros((ATT_BLK, LANES), F32)
            for h in (2 * g, 2 * g + 1):
                hm = _head_mask(h)
                qh = jnp.where(hm, qg, jnp.zeros_like(qg))
                s = []
                for d in range(N_ATT_TILES):
                    sd = _dot_nt(qh, k_refs[d][:, cols]) + b_ref[d * N_HEADS + h]
                    if d > 0 and hide_absent:
                        sd = jnp.where(i >= d, sd, NEG_INF)
                    s.append(sd)
                m = jnp.maximum(jnp.maximum(jnp.max(s[0], axis=-1, keepdims=True),
                                            jnp.max(s[1], axis=-1, keepdims=True)),
                                jnp.max(s[2], axis=-1, keepdims=True))
                p = [jnp.exp(sd - m) for sd in s]
                l = (jnp.sum(p[0], axis=-1, keepdims=True) + jnp.sum(p[1], axis=-1, keepdims=True)
                     + jnp.sum(p[2], axis=-1, keepdims=True))
                oh = jnp.zeros((ATT_BLK, LANES), F32)
                for d in range(N_ATT_TILES):
                    vg = v_refs[d][:, cols]
                    oh = oh + _dot(p[d].astype(BF16), jnp.where(hm, vg, jnp.zeros_like(vg)))
                og = og + oh / l
                lse_tile = jnp.where(lane == h, m + jnp.log(l), lse_tile)
            o_ref[:, cols] = og.astype(BF16)
        lse_ref[...] = lse_tile

    def kv_spec(d, col):
        return pl.BlockSpec((ATT_BLK, ATTN_W), lambda i: (jnp.maximum(i - d, 0), col))

    return _call(
        body, rider, name="fwd_attn", grid=(nb,),
        in_specs=[pl.BlockSpec((ATT_BLK, ATTN_W), lambda i: (i, 0)),
                  kv_spec(0, 1), kv_spec(1, 1), kv_spec(2, 1),
                  kv_spec(0, 2), kv_spec(1, 2), kv_spec(2, 2),
                  pl.BlockSpec((N_ATT_TILES * N_HEADS, ATT_BLK, ATT_BLK), lambda i: (0, 0, 0))],
        out_specs=[pl.BlockSpec((ATT_BLK, ATTN_W), lambda i: (i, 0)),
                   pl.BlockSpec((ATT_BLK, LANES), lambda i: (i, 0))],
        out_shape=[jax.ShapeDtypeStruct((T, ATTN_W), BF16),
                   jax.ShapeDtypeStruct((T, LANES), F32)],
        compiler_params=_params(40, 1),
    )(qkv, qkv, qkv, qkv, qkv, qkv, qkv, bias)


def _fwd_out_proj(co, ao, w_out, x, g2, g3):
    T = x.shape[0]
    tm = 512

    def body(co_ref, ao_ref, w_ref, x_ref, g2_ref, g3_ref, mixed_ref, h1_ref, u2_ref):
        mixed = _dot(co_ref[...], w_ref[0:CONV_W, :]) + _dot(ao_ref[...], w_ref[CONV_W:, :])
        mixed_ref[...] = mixed
        y, _ = _rms_fwd(mixed, g2_ref[...])
        h1 = x_ref[...] + y
        h1_ref[...] = h1
        u2, _ = _rms_fwd(h1, g3_ref[...])
        u2_ref[...] = u2.astype(BF16)

    row = lambda w: pl.BlockSpec((tm, w), lambda i: (i, 0))
    vec = pl.BlockSpec((1, D_MODEL), lambda i: (0, 0))
    return _call(
        body, None, name="fwd_out_proj", grid=(T // tm,),
        in_specs=[row(CONV_W), row(ATTN_W), pl.BlockSpec((D_MODEL, D_MODEL), lambda i: (0, 0)),
                  row(D_MODEL), vec, vec],
        out_specs=[row(D_MODEL), row(D_MODEL), row(D_MODEL)],
        out_shape=[jax.ShapeDtypeStruct((T, D_MODEL), F32),
                   jax.ShapeDtypeStruct((T, D_MODEL), F32),
                   jax.ShapeDtypeStruct((T, D_MODEL), BF16)],
        compiler_params=_params(40, 1),
    )(co, ao, w_out, x, g2, g3)[0]


GELU_C = math.sqrt(2.0 / math.pi)
GELU_A = 0.044715


def _gelu_and_grad(v):
    sq = v * v
    th = jnp.tanh(v * (GELU_C + (GELU_C * GELU_A) * sq))
    half = 0.5 + 0.5 * th
    gl = v * half
    dgl = half + (v * (half * (1.0 - th))) * (GELU_C + (3.0 * GELU_C * GELU_A) * sq)
    return gl, dgl


FF_TM = 256
FF_HALO = 16
FF_CHUNKS = [(lo, min(lo + 256, FF_SHARD)) for lo in range(0, FF_SHARD, 256)]


def _rows_before(prev, cur):
    ext = jnp.concatenate([prev, cur], axis=0)
    return pltpu.roll(ext, 1, 0)[SUBLANES:], pltpu.roll(ext, 2, 0)[SUBLANES:]


def _rows_after(cur, nxt):
    ext = jnp.concatenate([cur, nxt], axis=0)
    n = ext.shape[0]
    return pltpu.roll(ext, n - 1, 0)[:cur.shape[0]], pltpu.roll(ext, n - 2, 0)[:cur.shape[0]]


def _fwd_ffn(u2, w_up, fw, fb, w_down):
    T = u2.shape[0]
    tm = FF_TM

    def body(u_ref, wg_ref, wv_ref, fwg_ref, fwv_ref, fbg_ref, fbv_ref, wd_ref, hf_ref, pre_ref, f_ref, carg, carv):
        i = pl.program_id(1)

        @pl.when(i == 0)
        def _():
            carg[...] = jnp.zeros(carg.shape, F32)
            carv[...] = jnp.zeros(carv.shape, F32)

        u = u_ref[...]
        f = None
        up = lambda lo, hi: (_dot(u, wg_ref[0, :, lo:hi]), _dot(u, wv_ref[0, :, lo:hi]))
        ahead = up(*FF_CHUNKS[0])
        for c, (lo, hi) in enumerate(FF_CHUNKS):
            conv = []
            hs = ahead
            if c + 1 < len(FF_CHUNKS):
                ahead = up(*FF_CHUNKS[c + 1])
            for n, (car, fw_ref, fb_ref) in enumerate(((carg, fwg_ref, fbg_ref), (carv, fwv_ref, fbv_ref))):
                h0 = hs[n]
                hf_ref[n, :, lo:hi] = h0.astype(BF16)
                h1, h2 = _rows_before(car[:, lo:hi], h0)
                car[:, lo:hi] = h0[tm - SUBLANES:, :]
                conv.append(fw_ref[0, 0:1, lo:hi] * h2 + fw_ref[0, 1:2, lo:hi] * h1
                            + fw_ref[0, 2:3, lo:hi] * h0 + fb_ref[0, :, lo:hi])
            pre_ref[0, :, lo:hi] = conv[0].astype(BF16)
            pre_ref[1, :, lo:hi] = conv[1].astype(BF16)
            gl, _ = _gelu_and_grad(conv[0])
            term = _dot((gl * conv[1]).astype(BF16), wd_ref[lo:hi, :])
            f = term if f is None else f + term
        f_ref[0] = f

    wspec = lambda off: pl.BlockSpec((1, D_MODEL, FF_SHARD), lambda s, i: (s + off, 0, 0))
    fwspec = lambda off: pl.BlockSpec((1, FF_HALO, FF_SHARD), lambda s, i: (s + off, 0, 0))
    fbspec = lambda off: pl.BlockSpec((1, 1, FF_SHARD), lambda s, i: (s + off, 0, 0))
    return _call(
        body, None, name="fwd_ffn", grid=(2, T // tm),
        in_specs=[pl.BlockSpec((tm, D_MODEL), lambda s, i: (i, 0)),
                  wspec(0), wspec(2), fwspec(0), fwspec(2), fbspec(0), fbspec(2),
                  pl.BlockSpec((FF_SHARD, D_MODEL), lambda s, i: (s, 0))],
        out_specs=[pl.BlockSpec((2, tm, FF_SHARD), lambda s, i: (0, i, s)),
                   pl.BlockSpec((2, tm, FF_SHARD), lambda s, i: (0, i, s)),
                   pl.BlockSpec((1, tm, D_MODEL), lambda s, i: (s, i, 0))],
        out_shape=[jax.ShapeDtypeStruct((2, T, D_FF), BF16),
                   jax.ShapeDtypeStruct((2, T, D_FF), BF16),
                   jax.ShapeDtypeStruct((2, T, D_MODEL), F32)],
        scratch_shapes=[pltpu.VMEM((SUBLANES, FF_SHARD), F32), pltpu.VMEM((SUBLANES, FF_SHARD), F32)],
        compiler_params=_params(48, 2),
    )(u2, w_up, w_up, fw, fw, fb, fb, w_down)[0]


def _fwd_loss(fp, h1, tgt, g4):
    T = h1.shape[0]
    tm = 512

    def body(fp_ref, h1_ref, t_ref, g_ref, loss_ref, dy_ref, df_ref, dg_ref):
        i = pl.program_id(0)
        f = fp_ref[0] + fp_ref[1]
        r, _ = _rms_fwd(f, g_ref[...])
        e = (h1_ref[...] + r) - t_ref[...]
        dy = e * (1.0 / D_MODEL)
        dy_ref[...] = dy
        df, dg_rows = _rms_bwd(dy, f, g_ref[...])
        df_ref[...] = df.astype(BF16)
        part = 0.5 * jnp.sum(jnp.mean(e * e, axis=-1, keepdims=True), axis=0, keepdims=True)
        dg = jnp.sum(dg_rows, axis=0, keepdims=True)

        @pl.when(i == 0)
        def _():
            loss_ref[...] = part
            dg_ref[...] = dg

        @pl.when(i > 0)
        def _():
            loss_ref[...] += part
            dg_ref[...] += dg

    row = pl.BlockSpec((tm, D_MODEL), lambda i: (i, 0))
    vec = pl.BlockSpec((1, D_MODEL), lambda i: (0, 0))
    return _call(
        body, None, name="fwd_loss", grid=(T // tm,),
        in_specs=[pl.BlockSpec((2, tm, D_MODEL), lambda i: (0, i, 0)), row, row, vec],
        out_specs=[pl.BlockSpec((1, 1), lambda i: (0, 0)), row, row, vec],
        out_shape=[jax.ShapeDtypeStruct((1, 1), F32),
                   jax.ShapeDtypeStruct((T, D_MODEL), F32),
                   jax.ShapeDtypeStruct((T, D_MODEL), BF16),
                   jax.ShapeDtypeStruct((1, D_MODEL), F32)],
        compiler_params=_params(40, 1),
    )(fp, h1, tgt, g4)[0]


def _bwd_ffn(df, hf, pre, w_up, fw, w_down):
    T = df.shape[0]
    tm = FF_TM
    ni = T // tm

    def body(df_ref, hf_ref, pre_ref, wd_ref, wg_ref, wv_ref, fwg_ref, fwv_ref,
             du_ref, dhf_ref, act_ref, dwg_ref, dwv_ref, carg, carv):
        i = pl.program_id(1)

        @pl.when(i == 0)
        def _():
            dwg_ref[...] = jnp.zeros(dwg_ref.shape, F32)
            dwv_ref[...] = jnp.zeros(dwv_ref.shape, F32)
            carg[...] = jnp.zeros(carg.shape, F32)
            carv[...] = jnp.zeros(carv.shape, F32)

        df = df_ref[...]
        du = None
        down = lambda lo, hi: _dot_nt(df, wd_ref[lo:hi, :])
        ahead = down(*FF_CHUNKS[0])
        for c, (lo, hi) in enumerate(FF_CHUNKS):
            dact = ahead
            if c + 1 < len(FF_CHUNKS):
                ahead = down(*FF_CHUNKS[c + 1])
            pre_g = pre_ref[0, :, lo:hi].astype(F32)
            pre_v = pre_ref[1, :, lo:hi].astype(F32)
            gl, dgl = _gelu_and_grad(pre_g)
            act_ref[:, lo:hi] = (gl * pre_v).astype(BF16)
            dpre = (dact * pre_v * dgl, dact * gl)
            for n, (car, fw_ref, dw_ref, w_ref) in enumerate(
                    ((carg, fwg_ref, dwg_ref, wg_ref), (carv, fwv_ref, dwv_ref, wv_ref))):
                dp = dpre[n]
                h0 = hf_ref[n, :, lo:hi].astype(F32)
                up1, up2 = _rows_after(dp, car[:, lo:hi])
                car[:, lo:hi] = dp[0:SUBLANES, :]
                for k, shifted in enumerate((up2, up1, dp)):
                    dw_ref[0, k:k + 1, lo:hi] += jnp.sum(shifted * h0, axis=0, keepdims=True)
                dw_ref[0, 3:4, lo:hi] += jnp.sum(dp, axis=0, keepdims=True)
                dh = (fw_ref[0, 2:3, lo:hi] * dp + fw_ref[0, 1:2, lo:hi] * up1
                      + fw_ref[0, 0:1, lo:hi] * up2).astype(BF16)
                dhf_ref[n, :, lo:hi] = dh
                term = _dot_nt(dh, w_ref[0, :, lo:hi])
                du = term if du is None else du + term
        du_ref[0] = du

    rev = lambda i: ni - 1 - i
    wspec = lambda off: pl.BlockSpec((1, D_MODEL, FF_SHARD), lambda s, i: (s + off, 0, 0))
    fwspec = lambda off: pl.BlockSpec((1, FF_HALO, FF_SHARD), lambda s, i: (s + off, 0, 0))
    dwspec = pl.BlockSpec((1, FF_HALO, FF_SHARD), lambda s, i: (s, 0, 0))
    return _call(
        body, None, name="bwd_ffn", grid=(2, ni),
        in_specs=[pl.BlockSpec((tm, D_MODEL), lambda s, i: (rev(i), 0)),
                  pl.BlockSpec((2, tm, FF_SHARD), lambda s, i: (0, rev(i), s)),
                  pl.BlockSpec((2, tm, FF_SHARD), lambda s, i: (0, rev(i), s)),
                  pl.BlockSpec((FF_SHARD, D_MODEL), lambda s, i: (s, 0)),
                  wspec(0), wspec(2), fwspec(0), fwspec(2)],
        out_specs=[pl.BlockSpec((1, tm, D_MODEL), lambda s, i: (s, rev(i), 0)),
                   pl.BlockSpec((2, tm, FF_SHARD), lambda s, i: (0, rev(i), s)),
                   pl.BlockSpec((tm, FF_SHARD), lambda s, i: (rev(i), s)),
                   dwspec, dwspec],
        out_shape=[jax.ShapeDtypeStruct((2, T, D_MODEL), F32),
                   jax.ShapeDtypeStruct((2, T, D_FF), BF16),
                   jax.ShapeDtypeStruct((T, D_FF), BF16),
                   jax.ShapeDtypeStruct((2, FF_HALO, FF_SHARD), F32),
                   jax.ShapeDtypeStruct((2, FF_HALO, FF_SHARD), F32)],
        scratch_shapes=[pltpu.VMEM((SUBLANES, FF_SHARD), F32), pltpu.VMEM((SUBLANES, FF_SHARD), F32)],
        compiler_params=_params(56, 2),
    )(df, hf, pre, w_down, w_up, w_up, fw, fw)[0]


def _bwd_mid(du2p, dy, h1, mixed, g3, g2, w_out, rider=None):
    T = dy.shape[0]
    tm = 512

    def body(du_ref, dy_ref, h1_ref, mx_ref, g3_ref, g2_ref, w_ref,
             dh1_ref, dmx_ref, dco_ref, dao_ref, dg3_ref, dg2_ref):
        i = pl.program_id(0)
        dres, dg3_rows = _rms_bwd(du_ref[0] + du_ref[1], h1_ref[...], g3_ref[...])
        dh1 = dy_ref[...] + dres
        dh1_ref[...] = dh1
        dmx, dg2_rows = _rms_bwd(dh1, mx_ref[...], g2_ref[...])
        dmx = dmx.astype(BF16)
        dmx_ref[...] = dmx
        dcat = _dot_nt(dmx, w_ref[...])
        dco_ref[...] = dcat[:, :CONV_W]
        dao_ref[...] = dcat[:, CONV_W:].astype(BF16)
        dg3 = jnp.sum(dg3_rows, axis=0, keepdims=True)
        dg2 = jnp.sum(dg2_rows, axis=0, keepdims=True)

        @pl.when(i == 0)
        def _():
            dg3_ref[...] = dg3
            dg2_ref[...] = dg2

        @pl.when(i > 0)
        def _():
            dg3_ref[...] += dg3
            dg2_ref[...] += dg2

    row = lambda w: pl.BlockSpec((tm, w), lambda i: (i, 0))
    vec = pl.BlockSpec((1, D_MODEL), lambda i: (0, 0))
    return _call(
        body, rider, name="bwd_mid", grid=(T // tm,),
        in_specs=[pl.BlockSpec((2, tm, D_MODEL), lambda i: (0, i, 0)), row(D_MODEL), row(D_MODEL),
                  row(D_MODEL), vec, vec, pl.BlockSpec((D_MODEL, D_MODEL), lambda i: (0, 0))],
        out_specs=[row(D_MODEL), row(D_MODEL), row(CONV_W), row(ATTN_W), vec, vec],
        out_shape=[jax.ShapeDtypeStruct((T, D_MODEL), F32),
                   jax.ShapeDtypeStruct((T, D_MODEL), BF16),
                   jax.ShapeDtypeStruct((T, CONV_W), F32),
                   jax.ShapeDtypeStruct((T, ATTN_W), BF16),
                   jax.ShapeDtypeStruct((1, D_MODEL), F32),
                   jax.ShapeDtypeStruct((1, D_MODEL), F32)],
        compiler_params=_params(48, 1),
    )(du2p, dy, h1, mixed, g3, g2, w_out)


def _bwd_attn(qkv, ao, dao, lse, bias, rider=None):
    T = qkv.shape[0]
    nb = T // ATT_BLK
    scale = HEAD_DIM ** -0.5

    def body(k_ref, v_ref, q0, q1, q2, do0, do1, do2, o0, o1, o2, l0, l1, l2, b_ref,
             dp_ref, ds_ref, acc1, acc2):
        j = pl.program_id(0)
        q_refs, do_refs, o_refs, l_refs = (q0, q1, q2), (do0, do1, do2), (o0, o1, o2), (l0, l1, l2)

        @pl.when(j == 0)
        def _():
            ds_ref[...] = jnp.zeros(ds_ref.shape, F32)
            acc1[...] = jnp.zeros(acc1.shape, F32)
            acc2[...] = jnp.zeros(acc2.shape, F32)

        dq_new = [[], [], []]
        dk_cols, dv_cols = [], []
        for g in range(N_HEADS // 2):
            cols = slice(g * LANES, (g + 1) * LANES)
            kg = k_ref[:, cols]
            vg = v_ref[:, cols]
            dkg = jnp.zeros((ATT_BLK, LANES), F32)
            dvg = jnp.zeros((ATT_BLK, LANES), F32)
            dqg = [jnp.zeros((ATT_BLK, LANES), F32) for _ in range(N_ATT_TILES)]
            for d in range(N_ATT_TILES):
                qg = q_refs[d][:, cols] * scale
                dog = do_refs[d][:, cols]
                if d > 0:
                    dog = jnp.where(j + d < nb, dog, jnp.zeros_like(dog))
                prod = dog.astype(F32) * o_refs[d][:, cols].astype(F32)
                for h in (2 * g, 2 * g + 1):
                    hm = _head_mask(h)
                    qh = jnp.where(hm, qg, jnp.zeros_like(qg))
                    doh = jnp.where(hm, dog, jnp.zeros_like(dog))
                    kh = jnp.where(hm, kg, jnp.zeros_like(kg))
                    delta = jnp.sum(jnp.where(hm, prod, 0.0), axis=-1, keepdims=True)
                    s = _dot_nt(qh, kg) + b_ref[d * N_HEADS + h]
                    p = jnp.exp(s - l_refs[d][:, h:h + 1])
                    dvg = dvg + _dot_tn(p.astype(BF16), doh)
                    dpm = _dot_nt(doh, vg)
                    dsc = p * (dpm - delta)
                    ds_ref[d * N_HEADS + h] += dsc
                    dsb = dsc.astype(BF16)
                    dqg[d] = dqg[d] + _dot(dsb, kh)
                    dkg = dkg + _dot_tn(dsb, qh)
            for d in range(N_ATT_TILES):
                dq_new[d].append(dqg[d])
            dk_cols.append(dkg)
            dv_cols.append(dvg)
        x0, x1, x2 = (jnp.concatenate(c, axis=1) * scale for c in dq_new)
        dp_ref[:, 0:1024] = jnp.zeros((ATT_BLK, 1024), BF16)
        dp_ref[:, 1024:1536] = (acc1[...] + x0).astype(BF16)
        dp_ref[:, 1536:2048] = jnp.concatenate(dk_cols, axis=1).astype(BF16)
        dp_ref[:, 2048:2560] = jnp.concatenate(dv_cols, axis=1).astype(BF16)
        acc1[...] = acc2[...] + x1
        acc2[...] = x2

    def fwd_spec(d, width, col):
        return pl.BlockSpec((ATT_BLK, width), lambda j: (jnp.minimum(j + d, nb - 1), col))

    return _call(
        body, rider, name="bwd_attn", grid=(nb,),
        in_specs=[pl.BlockSpec((ATT_BLK, ATTN_W), lambda j: (j, 1)),
                  pl.BlockSpec((ATT_BLK, ATTN_W), lambda j: (j, 2)),
                  fwd_spec(0, ATTN_W, 0), fwd_spec(1, ATTN_W, 0), fwd_spec(2, ATTN_W, 0),
                  fwd_spec(0, ATTN_W, 0), fwd_spec(1, ATTN_W, 0), fwd_spec(2, ATTN_W, 0),
                  fwd_spec(0, ATTN_W, 0), fwd_spec(1, ATTN_W, 0), fwd_spec(2, ATTN_W, 0),
                  fwd_spec(0, LANES, 0), fwd_spec(1, LANES, 0), fwd_spec(2, LANES, 0),
                  pl.BlockSpec((N_ATT_TILES * N_HEADS, ATT_BLK, ATT_BLK), lambda j: (0, 0, 0))],
        out_specs=[pl.BlockSpec((ATT_BLK, IN_COLS), lambda j: (j, 0)),
                   pl.BlockSpec((N_ATT_TILES * N_HEADS, ATT_BLK, ATT_BLK), lambda j: (0, 0, 0))],
        out_shape=[jax.ShapeDtypeStruct((T, IN_COLS), BF16),
                   jax.ShapeDtypeStruct((N_ATT_TILES * N_HEADS, ATT_BLK, ATT_BLK), F32)],
        scratch_shapes=[pltpu.VMEM((ATT_BLK, ATTN_W), F32), pltpu.VMEM((ATT_BLK, ATTN_W), F32)],
        compiler_params=_params(56, 1),
    )(qkv, qkv, qkv, qkv, qkv, dao, dao, dao, ao, ao, ao, lse, lse, lse, bias)


def _bwd_conv(dproj, a, dco, hc, cw, lg, lb, rider=None):
    T = a.shape[0]
    tm = 512
    rc = 32
    ni = T // tm
    hb = tm // CONV_HALO

    def body(dp_in, a_ref, ap_ref, dco_ref, dcon_ref, hc_ref, hcn_ref, w_ref, lg_ref, lb_ref,
             dp_ref, dw_ref, db_ref, dlg_ref, dlb_ref, hext, dext, hsh, dsh, dwacc):
        del dp_in
        i = pl.program_id(0)

        def ln_bwd(dco_v, hc_v):
            mu = jnp.mean(hc_v, axis=-1, keepdims=True)
            xc = hc_v - mu
            rstd = lax.rsqrt(jnp.mean(xc * xc, axis=-1, keepdims=True) + EPS)
            xh = xc * rstd
            z = xh * lg_ref[...] + lb_ref[...]
            sg = _sigmoid(z)
            dz = dco_v * (sg * (1.0 + z * (1.0 - sg)))
            dxh = dz * lg_ref[...]
            dhc = rstd * (dxh - jnp.mean(dxh, axis=-1, keepdims=True)
                          - xh * jnp.mean(dxh * xh, axis=-1, keepdims=True))
            return dhc, dz * xh, dz

        hext[0:CONV_HALO, :] = jnp.where(i > 0, ap_ref[:, :CONV_W] * _sigmoid(ap_ref[:, CONV_W:]), 0.0)
        hext[CONV_HALO:CONV_HALO + tm, :] = a_ref[:, :CONV_W] * _sigmoid(a_ref[:, CONV_W:])
        dhc, dlg_rows, dlb_rows = ln_bwd(dco_ref[...], hc_ref[...])
        dext[0:tm, :] = dhc
        dhc_next, _, _ = ln_bwd(dcon_ref[...], hcn_ref[...])
        dext[tm:tm + CONV_HALO, :] = jnp.where(i < ni - 1, dhc_next, 0.0)

        @pl.when(i == 0)
        def _():
            dw_ref[...] = jnp.zeros(dw_ref.shape, F32)
            db_ref[...] = jnp.zeros(db_ref.shape, F32)
            dlg_ref[...] = jnp.zeros(dlg_ref.shape, F32)
            dlb_ref[...] = jnp.zeros(dlb_ref.shape, F32)

            dwacc[...] = jnp.zeros(dwacc.shape, F32)

        db_ref[...] += jnp.sum(dhc, axis=0, keepdims=True)
        dlg_ref[...] += jnp.sum(dlg_rows, axis=0, keepdims=True)
        dlb_ref[...] += jnp.sum(dlb_rows, axis=0, keepdims=True)
        _fill_shifted(hext, hsh, tm)
        _fill_shifted(dext, dsh, tm)
        for c in range(tm // rc):
            r0 = c * rc
            dh = jnp.zeros((rc, CONV_W), F32)
            dhc_c = dext[r0:r0 + rc, :]
            for k in range(CONV_K):
                dh = dh + w_ref[k:k + 1, :] * _shifted_rows(dext, dsh, r0 + 30 - k, rc)
                prod = dhc_c * _shifted_rows(hext, hsh, r0 + 2 + k, rc)
                dwacc[k] += jnp.sum(prod.reshape(rc // SUBLANES, SUBLANES, CONV_W), axis=0)
            av = a_ref[r0:r0 + rc, :CONV_W]
            sg = _sigmoid(a_ref[r0:r0 + rc, CONV_W:])
            dp_ref[r0:r0 + rc, 0:CONV_W] = (dh * sg).astype(BF16)
            dp_ref[r0:r0 + rc, CONV_W:] = (dh * av * sg * (1.0 - sg)).astype(BF16)

        @pl.when(i == ni - 1)
        def _():
            dw_ref[...] = jnp.sum(dwacc[...], axis=1)

    row = lambda w: pl.BlockSpec((tm, w), lambda i: (i, 0))
    prev = lambda w: pl.BlockSpec((CONV_HALO, w), lambda i: (jnp.maximum(i * hb - 1, 0), 0))
    nxt = lambda w: pl.BlockSpec((CONV_HALO, w), lambda i: (jnp.minimum((i + 1) * hb, ni * hb - 1), 0))
    vec = pl.BlockSpec((1, CONV_W), lambda i: (0, 0))
    return _call(
        body, rider, name="bwd_conv", grid=(ni,),
        in_specs=[ANY, row(1024), prev(1024), row(CONV_W), nxt(CONV_W), row(CONV_W), nxt(CONV_W),
                  pl.BlockSpec((CONV_HALO, CONV_W), lambda i: (0, 0)), vec, vec],
        out_specs=[pl.BlockSpec((tm, 1024), lambda i: (i, 0)),
                   pl.BlockSpec((CONV_HALO, CONV_W), lambda i: (0, 0)), vec, vec, vec],
        out_shape=[jax.ShapeDtypeStruct((T, IN_COLS), BF16),
                   jax.ShapeDtypeStruct((CONV_HALO, CONV_W), F32),
                   jax.ShapeDtypeStruct((1, CONV_W), F32),
                   jax.ShapeDtypeStruct((1, CONV_W), F32),
                   jax.ShapeDtypeStruct((1, CONV_W), F32)],
        scratch_shapes=[pltpu.VMEM((tm + CONV_HALO, CONV_W), F32), pltpu.VMEM((tm + CONV_HALO, CONV_W), F32),
                        pltpu.VMEM((SUBLANES - 1, tm + CONV_HALO - SUBLANES, CONV_W), F32),
                        pltpu.VMEM((SUBLANES - 1, tm + CONV_HALO - SUBLANES, CONV_W), F32),
                        pltpu.VMEM((CONV_HALO, SUBLANES, CONV_W), F32)],
        input_output_aliases={0: 0},
        compiler_params=_params(56, 1),
    )(dproj, a, a, dco, dco, hc, hc, cw, lg, lb)


def _bwd_in_proj(dproj, w_in, x, dh1, g1, rider=None):
    T = x.shape[0]
    tm = 512

    def body(dp_ref, w_ref, x_ref, dh_ref, g_ref, gx_ref, dg_ref):
        i = pl.program_id(0)
        du = None
        for s in range(N_CHIPS):
            term = _dot_nt(dp_ref[:, IN_SHARD * s:IN_SHARD * (s + 1)], w_ref[s])
            du = term if du is None else du + term
        dx, dg_rows = _rms_bwd(du, x_ref[...], g_ref[...])
        gx_ref[...] = dh_ref[...] + dx
        dg = jnp.sum(dg_rows, axis=0, keepdims=True)

        @pl.when(i == 0)
        def _():
            dg_ref[...] = dg

        @pl.when(i > 0)
        def _():
            dg_ref[...] += dg

    row = lambda w: pl.BlockSpec((tm, w), lambda i: (i, 0))
    vec = pl.BlockSpec((1, D_MODEL), lambda i: (0, 0))
    return _call(
        body, rider, name="bwd_in_proj", grid=(T // tm,),
        in_specs=[row(IN_COLS), pl.BlockSpec((N_CHIPS, D_MODEL, IN_SHARD), lambda i: (0, 0, 0)),
                  row(D_MODEL), row(D_MODEL), vec],
        out_specs=[row(D_MODEL), vec],
        out_shape=[jax.ShapeDtypeStruct((T, D_MODEL), F32), jax.ShapeDtypeStruct((1, D_MODEL), F32)],
        compiler_params=_params(40, 1),
    )(dproj, w_in, x, dh1, g1)


def _wgrad(name, a_list, a_spec, b, b_spec, out_spec, out_shape, n_outer, T, tk, select=None, rider=None):
    def body(*refs):
        a_refs, b_ref, o_ref = refs[:len(a_list)], refs[len(a_list)], refs[len(a_list) + 1]
        kt = pl.program_id(1)

        @pl.when(kt == 0)
        def _():
            o_ref[...] = jnp.zeros(o_ref.shape, F32)

        bv = b_ref[...].reshape(b_ref.shape[-2:])
        if select is None:
            o_ref[...] += _dot_tn(a_refs[0][...].reshape(a_refs[0].shape[-2:]), bv).reshape(o_ref.shape)
        else:
            for n, a_ref in enumerate(a_refs):
                @pl.when(select(pl.program_id(0)) == n)
                def _():
                    o_ref[...] += _dot_tn(a_ref[...], bv).reshape(o_ref.shape)

    (res,), got = _call(
        body, rider, name=name, grid=(n_outer, T // tk),
        in_specs=[a_spec] * len(a_list) + [b_spec],
        out_specs=[out_spec], out_shape=[out_shape],
        compiler_params=_params(48, 2),
    )(*a_list, b)
    return (res, got) if rider is not None else res


def _mesh_pos():
    return lax.axis_index("x"), lax.axis_index("y"), lax.axis_index("c")


def _other_chips(x, y):
    return [((1 - x, y), 2 * (1 - x) + y), ((x, 1 - y), 2 * x + (1 - y)), ((1 - x, 1 - y), 2 * (1 - x) + (1 - y))]


def _exchange_rider(operands, out_shape, aliases, sem_shape, pairs):
    def start(ins, outs, sems):
        for send, _ in pairs(ins, outs, *sems):
            send.start()

    def finish(ins, outs, sems):
        for send, recv in pairs(ins, outs, *sems):
            send.wait_send()
            recv.wait_recv()

    sems = [pltpu.SemaphoreType.DMA(sem_shape), pltpu.SemaphoreType.DMA(sem_shape)]
    return _Rider(list(operands), list(out_shape), aliases, sems, start, finish)


def _remote(src, dst, send_sem, recv_sem, device):
    return pltpu.make_async_remote_copy(src_ref=src, dst_ref=dst, send_sem=send_sem, recv_sem=recv_sem,
                                        device_id=device, device_id_type=MESH)


def _fetch_rider(bufs):
    def pairs(ins, outs, send_sems, recv_sems):
        x, y, c = _mesh_pos()
        res = []
        for t, buf in enumerate(bufs):
            rows = pl.ds(c * (buf.shape[1] // 2), buf.shape[1] // 2)
            mine = outs[t].at[2 * x + y, rows]
            for k, (chip, s) in enumerate(_other_chips(x, y)):
                landed = outs[t].at[s, rows]
                res.append((_remote(mine, mine, send_sems.at[t, k], recv_sems.at[t, k], (*chip, c)),
                            _remote(landed, landed, send_sems.at[t, k], recv_sems.at[t, k], (*chip, c))))
        return res

    shapes = [jax.ShapeDtypeStruct(b.shape, b.dtype) for b in bufs]
    return _exchange_rider(bufs, shapes, {t: t for t in range(len(bufs))}, (len(bufs), 3), pairs)


def _forward_rider(bufs):
    def pairs(ins, outs, send_sems, recv_sems):
        x, y, c = _mesh_pos()
        res = []
        for t, buf in enumerate(bufs):
            half = buf.shape[1] // 2
            for k, (_, s) in enumerate(_other_chips(x, y)):
                landed = outs[t].at[s, pl.ds(c * half, half)]
                theirs = outs[t].at[s, pl.ds((1 - c) * half, half)]
                res.append((_remote(landed, landed, send_sems.at[t, k], recv_sems.at[t, k], (x, y, 1 - c)),
                            _remote(theirs, theirs, send_sems.at[t, k], recv_sems.at[t, k], (x, y, 1 - c))))
        return res

    shapes = [jax.ShapeDtypeStruct(b.shape, b.dtype) for b in bufs]
    return _exchange_rider(bufs, shapes, {t: t for t in range(len(bufs))}, (len(bufs), 3), pairs)


def _pair_exchange_rider(grads):
    def pairs(ins, outs, send_sems, recv_sems):
        x, y, c = _mesh_pos()
        res = []
        for t, g in enumerate(grads):
            half = g.shape[1] // 2
            cp = _remote(ins[t].at[:, pl.ds((1 - c) * half, half), :], outs[t], send_sems.at[t], recv_sems.at[t],
                         (x, y, 1 - c))
            res.append((cp, cp))
        return res

    shapes = [jax.ShapeDtypeStruct((N_CHIPS, g.shape[1] // 2, g.shape[2]), F32) for g in grads]
    return _exchange_rider(grads, shapes, {}, (len(grads),), pairs)


def _chip_exchange_rider(sums):
    def pairs(ins, outs, send_sems, recv_sems):
        x, y, c = _mesh_pos()
        res = []
        for t in range(len(sums)):
            for k, (chip, s) in enumerate(_other_chips(x, y)):
                cp = _remote(ins[t].at[s], outs[t].at[k], send_sems.at[t, k], recv_sems.at[t, k], (*chip, c))
                res.append((cp, cp))
        return res

    shapes = [jax.ShapeDtypeStruct((3,) + p.shape[1:], p.dtype) for p in sums]
    return _exchange_rider(sums, shapes, {}, (len(sums), 3), pairs)


def _pair_gather_rider(fulls):
    def pairs(ins, outs, send_sems, recv_sems):
        x, y, c = _mesh_pos()
        res = []
        for t, f in enumerate(fulls):
            half = f.shape[0] // 2
            mine = outs[t].at[pl.ds(c * half, half)]
            theirs = outs[t].at[pl.ds((1 - c) * half, half)]
            res.append((_remote(mine, mine, send_sems.at[t], recv_sems.at[t], (x, y, 1 - c)),
                        _remote(theirs, theirs, send_sems.at[t], recv_sems.at[t], (x, y, 1 - c))))
        return res

    shapes = [jax.ShapeDtypeStruct(f.shape, F32) for f in fulls]
    return _exchange_rider(fulls, shapes, {t: t for t in range(len(fulls))}, (len(fulls),), pairs)


def _alone(name, rider):
    return _call(lambda: None, rider, name=name)()[1]


def _all_reduce_small(pack):
    rows = pack.shape[0]

    def body(p_ref, o_ref, buf, send_sems, recv_sems):
        x, y, c = _mesh_pos()
        me = 4 * x + 2 * y + c
        buf[0] = p_ref[...]
        copies = []
        for k in range(1, 8):
            peer = (x ^ (k >> 2), y ^ ((k >> 1) & 1), c ^ (k & 1))
            cp = pltpu.make_async_remote_copy(
                src_ref=p_ref, dst_ref=buf.at[k], send_sem=send_sems.at[k - 1], recv_sem=recv_sems.at[k - 1],
                device_id=peer, device_id_type=MESH)
            cp.start()
            copies.append(cp)
        for cp in copies:
            cp.wait()
        total = buf[me]
        for dev in range(1, 8):
            total = total + buf[me ^ dev]
        o_ref[...] = total

    return pl.pallas_call(
        body, name="all_reduce_small",
        in_specs=[VMEM_FULL], out_specs=VMEM_FULL,
        out_shape=jax.ShapeDtypeStruct(pack.shape, F32),
        scratch_shapes=[pltpu.VMEM((8, rows, LANES), F32),
                        pltpu.SemaphoreType.DMA((7,)), pltpu.SemaphoreType.DMA((7,))],
    )(pack)


def _row_block(rows):
    if rows <= 512:
        return rows
    for rb in (256, 352):
        if rows % rb == 0:
            return rb
    raise ValueError(f"no row block for {rows} rows")


def _place(name, w, pos, dtype):
    R, C = w.shape
    rb = _row_block(R)

    def body(pos_ref, w_ref, o_ref):
        del pos_ref
        o_ref[0] = w_ref[...].astype(dtype)

    return pl.pallas_call(
        body, name=name,
        grid_spec=pltpu.PrefetchScalarGridSpec(
            num_scalar_prefetch=1, grid=(R // rb,),
            in_specs=[pl.BlockSpec((rb, C), lambda r, p: (r, 0))],
            out_specs=pl.BlockSpec((1, rb, C), lambda r, p: (p[0], r, 0))),
        out_shape=jax.ShapeDtypeStruct((N_CHIPS, R, C), dtype),
        compiler_params=_params(32, 1),
    )(pos, w)


def _pair_sum(name, g, got, pos):
    S, R, C = g.shape
    half = R // 2
    rb = _row_block(half)
    nh = half // rb

    def body(pos_ref, a_ref, b_ref, o_ref):
        del pos_ref
        o_ref[...] = (a_ref[...] + b_ref[...]).astype(BF16)

    spec = pl.BlockSpec((1, rb, C), lambda s, r, p: (s, r, 0))
    return pl.pallas_call(
        body, name=name,
        grid_spec=pltpu.PrefetchScalarGridSpec(
            num_scalar_prefetch=1, grid=(S, nh),
            in_specs=[pl.BlockSpec((1, rb, C), lambda s, r, p: (s, p[1] * nh + r, 0)), spec],
            out_specs=spec),
        out_shape=jax.ShapeDtypeStruct((S, half, C), BF16), compiler_params=_params(32, 2),
    )(pos, g, got)


def _chip_sum(name, pairs, got, pos):
    _, half, C = pairs.shape
    rb = _row_block(half)
    nh = half // rb

    def body(pos_ref, a_ref, g_ref, o_ref):
        del pos_ref
        o_ref[...] = ((a_ref[0].astype(F32) + g_ref[0].astype(F32)) + g_ref[1].astype(F32)) + g_ref[2].astype(F32)

    return pl.pallas_call(
        body, name=name,
        grid_spec=pltpu.PrefetchScalarGridSpec(
            num_scalar_prefetch=1, grid=(nh,),
            in_specs=[pl.BlockSpec((1, rb, C), lambda r, p: (p[0], r, 0)),
                      pl.BlockSpec((3, rb, C), lambda r, p: (0, r, 0))],
            out_specs=pl.BlockSpec((rb, C), lambda r, p: (p[1] * nh + r, 0))),
        out_shape=jax.ShapeDtypeStruct((2 * half, C), F32), compiler_params=_params(32, 1),
    )(pos, pairs, got)


def _adamw(name, w, g, m, v):
    R, C = w.shape
    rb = _row_block(R)
    c1 = 1.0 - ADAM_B1 ** ADAM_STEP
    c2 = 1.0 - ADAM_B2 ** ADAM_STEP

    def body(w_ref, g_ref, m_ref, v_ref, d_ref, nm_ref, nv_ref):
        gv = g_ref[...]
        nm = ADAM_B1 * m_ref[...] + (1.0 - ADAM_B1) * gv
        nv = ADAM_B2 * v_ref[...] + (1.0 - ADAM_B2) * (gv * gv)
        nm_ref[...] = nm
        nv_ref[...] = nv
        d_ref[...] = -ADAM_LR * ((nm / c1) / (jnp.sqrt(nv / c2) + ADAM_EPS) + ADAM_WD * w_ref[...])

    spec = pl.BlockSpec((rb, C), lambda r: (r, 0))
    sds = jax.ShapeDtypeStruct(w.shape, F32)
    return pl.pallas_call(
        body, name=name, grid=(R // rb,), in_specs=[spec] * 4, out_specs=[spec] * 3,
        out_shape=[sds, sds, sds], compiler_params=_params(40, 1),
    )(w, g, m, v)


def _rel_index():
    m = np.arange(2 * ATT_BLK)
    off = np.where(m < ATT_BLK, m, m - 2 * ATT_BLK)
    rel = np.stack([ATT_BLK * d - off for d in range(N_ATT_TILES)])
    return np.clip(rel, -MAX_REL, MAX_REL) + MAX_REL


def _local_step(x, tgt, g1, w_in, cw, cb, lg, lb, bias, w_out, g2, g3, w_up, fw, fb, w_down, g4, pos=None):
    T = x.shape[0]
    dist = pos is not None
    idx = _rel_index()

    (u, a, qkv), got = _fwd_in_proj(x, g1, w_in, _fetch_rider([w_out, w_down]) if dist else None)
    if dist:
        w_out, w_down = got
    (co, hc), got = _fwd_conv(a, cw, cb, lg, lb, _merge_riders(_forward_rider([w_out, w_down]),
                                                               _fetch_rider([w_up])) if dist else None)
    if dist:
        w_out, w_down, w_up = got
    (ao, lse), got = _fwd_attn(qkv, bias, _forward_rider([w_up]) if dist else None)
    if dist:
        (w_up,) = got
        w_out, w_down = w_out.reshape(D_MODEL, D_MODEL), w_down.reshape(D_FF, D_MODEL)
    mixed, h1, u2 = _fwd_out_proj(co, ao, w_out, x, g2, g3)
    hf, pre, fp = _fwd_ffn(u2, w_up, fw, fb, w_down)
    loss, dy, df, dg4 = _fwd_loss(fp, h1, tgt, g4)

    tk = 1024
    du2p, dhf, act, dfw_g, dfw_v = _bwd_ffn(df, hf, pre, w_up, fw, w_down)
    gw_up = _wgrad(
        "wgrad_up", [u2], pl.BlockSpec((tk, D_MODEL), lambda s, k: (k, 0)),
        dhf, pl.BlockSpec((1, tk, FF_SHARD), lambda s, k: (s // 2, k, s % 2)),
        pl.BlockSpec((1, D_MODEL, FF_SHARD), lambda s, k: (s, 0, 0)),
        jax.ShapeDtypeStruct((N_CHIPS, D_MODEL, FF_SHARD), F32), N_CHIPS, T, tk)
    gw_down = _wgrad(
        "wgrad_down", [act], pl.BlockSpec((tk, FF_SHARD), lambda s, k: (k, s)),
        df, pl.BlockSpec((tk, D_MODEL), lambda s, k: (k, 0)),
        pl.BlockSpec((FF_SHARD, D_MODEL), lambda s, k: (s, 0)),
        jax.ShapeDtypeStruct((D_FF, D_MODEL), F32), 2, T, tk).reshape(N_CHIPS, D_FF // N_CHIPS, D_MODEL)
    (dh1, dmx, dco, dao, dg3, dg2), _ = _bwd_mid(du2p, dy, h1, mixed, g3, g2, w_out)
    gw_out = _wgrad(
        "wgrad_out", [co, ao], pl.BlockSpec((tk, CONV_W), lambda s, k: (k, 0)),
        dmx, pl.BlockSpec((tk, D_MODEL), lambda s, k: (k, 0)),
        pl.BlockSpec((CONV_W, D_MODEL), lambda s, k: (s, 0)),
        jax.ShapeDtypeStruct((D_MODEL, D_MODEL), F32), 2, T, tk,
        select=lambda s: s, rider=_pair_exchange_rider([gw_up, gw_down]) if dist else None)
    if dist:
        gw_out, got = gw_out
        p_up = _pair_sum("pair_sum_w_up", gw_up, got[0], pos)
        p_down = _pair_sum("pair_sum_w_down", gw_down, got[1], pos)
    gw_out = gw_out.reshape(N_CHIPS, D_MODEL // N_CHIPS, D_MODEL)
    (dproj, dsacc), got = _bwd_attn(
        qkv, ao, dao, lse, bias,
        _merge_riders(_chip_exchange_rider([p_up, p_down]), _pair_exchange_rider([gw_out])) if dist else None)
    if dist:
        gw_up = _chip_sum("chip_sum_w_up", p_up, got[0], pos)
        gw_down = _chip_sum("chip_sum_w_down", p_down, got[1], pos)
        p_out = _pair_sum("pair_sum_w_out", gw_out, got[2], pos)
    (dproj, dcw, dcb, dlg, dlb), got = _bwd_conv(
        dproj, a, dco, hc, cw, lg, lb,
        _merge_riders(_pair_gather_rider([gw_up, gw_down]), _chip_exchange_rider([p_out])) if dist else None)
    if dist:
        gw_up, gw_down = got[:2]
        gw_out = _chip_sum("chip_sum_w_out", p_out, got[2], pos)
    gw_in = _wgrad(
        "wgrad_in", [u], pl.BlockSpec((tk, D_MODEL), lambda s, k: (k, 0)),
        dproj, pl.BlockSpec((tk, IN_SHARD), lambda s, k: (k, s)),
        pl.BlockSpec((1, D_MODEL, IN_SHARD), lambda s, k: (s, 0, 0)),
        jax.ShapeDtypeStruct((N_CHIPS, D_MODEL, IN_SHARD), F32), N_CHIPS, T, tk)
    if dist:
        got = _alone("pair_exchange_w_in", _merge_riders(_pair_exchange_rider([gw_in]), _pair_gather_rider([gw_out])))
        p_in, gw_out = _pair_sum("pair_sum_w_in", gw_in, got[0], pos), got[1]
    (gx, dg1), _ = _bwd_in_proj(dproj, w_in, x, dh1, g1)
    (diag,), got = _diag_sums(dsacc, _chip_exchange_rider([p_in]) if dist else None)
    if dist:
        (gw_in,) = _alone("pair_gather_w_in", _pair_gather_rider([_chip_sum("chip_sum_w_in", p_in, got[0], pos)]))

    diag = diag.reshape(N_ATT_TILES, N_HEADS, 2 * ATT_BLK)
    onehot = np.zeros((N_ATT_TILES, 2 * ATT_BLK, 2 * MAX_REL + 1), np.float32)
    for d in range(N_ATT_TILES):
        onehot[d, np.arange(2 * ATT_BLK), idx[d]] = 1.0
    drel = jnp.einsum("dhm,dmr->hr", diag, jnp.asarray(onehot), precision=lax.Precision.HIGHEST)

    small = dict(norm_mix_pre=dg1, conv_dw_w=dcw[:CONV_K], conv_dw_b=dcb, conv_ln_g=dlg, conv_ln_b=dlb,
                 rel_bias=drel, norm_mix_post=dg2, norm_ffn_pre=dg3,
                 ffn_dw_w=jnp.concatenate([dfw_g[0, :3], dfw_g[1, :3], dfw_v[0, :3], dfw_v[1, :3]], axis=1),
                 ffn_dw_b=jnp.concatenate([dfw_g[0, 3:4], dfw_g[1, 3:4], dfw_v[0, 3:4], dfw_v[1, 3:4]], axis=1),
                 norm_ffn_post=dg4)
    return loss, gx, small, dict(w_in=gw_in, w_out=gw_out, w_up=gw_up, w_down=gw_down)


SMALL_ORDER = ["norm_mix_pre", "conv_dw_b", "conv_ln_g", "conv_ln_b", "rel_bias", "norm_mix_post",
               "norm_ffn_pre", "ffn_dw_b", "norm_ffn_post", "conv_dw_w", "ffn_dw_w"]


def _pack(parts):
    rows = []
    for p in parts:
        width = -(-p.shape[1] // LANES) * LANES
        rows.append(jnp.pad(p, ((0, 0), (0, width - p.shape[1]))).reshape(-1, LANES))
    packed = jnp.concatenate(rows, axis=0)
    pad = -packed.shape[0] % 8
    return jnp.pad(packed, ((0, pad), (0, 0)))


def _unpack(packed, shapes):
    out, r = [], 0
    for shp in shapes:
        width = -(-shp[1] // LANES) * LANES
        n = shp[0] * width // LANES
        out.append(packed[r:r + n].reshape(shp[0], width)[:, :shp[1]])
        r += n
    return out


WEIGHTS = ["norm_mix_pre", "w_in", "conv_dw_w", "conv_dw_b", "conv_ln_g", "conv_ln_b", "rel_bias", "w_out",
           "norm_mix_post", "norm_ffn_pre", "w_up", "ffn_dw_w", "ffn_dw_b", "w_down", "norm_ffn_post"]
BIG = ["w_in", "w_out", "w_up", "w_down"]


def kernel(x, norm_mix_pre, w_in, conv_dw_w, conv_dw_b, conv_ln_g, conv_ln_b, rel_bias, w_out, norm_mix_post, norm_ffn_pre, w_up, ffn_dw_w, ffn_dw_b, w_down, norm_ffn_post, loss_target, m_norm_mix_pre, m_w_in, m_conv_dw_w, m_conv_dw_b, m_conv_ln_g, m_conv_ln_b, m_rel_bias, m_w_out, m_norm_mix_post, m_norm_ffn_pre, m_w_up, m_ffn_dw_w, m_ffn_dw_b, m_w_down, m_norm_ffn_post, v_norm_mix_pre, v_w_in, v_conv_dw_w, v_conv_dw_b, v_conv_ln_g, v_conv_ln_b, v_rel_bias, v_w_out, v_norm_mix_post, v_norm_ffn_pre, v_w_up, v_ffn_dw_w, v_ffn_dw_b, v_w_down, v_norm_ffn_post):
    args = locals()
    w = {n: args[n][0] for n in WEIGHTS}
    m = {n: args["m_" + n][0] for n in WEIGHTS}
    v = {n: args["v_" + n][0] for n in WEIGHTS}
    for d in (w, m, v):
        d["rel_bias"] = d["rel_bias"].reshape(N_HEADS, 2 * MAX_REL + 1)
        for n in ("norm_mix_pre", "conv_dw_b", "conv_ln_g", "conv_ln_b", "norm_mix_post", "norm_ffn_pre",
                  "ffn_dw_b", "norm_ffn_post"):
            d[n] = d[n].reshape(1, -1)
    shard = 2 * lax.axis_index("x") + lax.axis_index("y")

    cw_sh = jnp.pad(w["conv_dw_w"], ((0, CONV_HALO - CONV_K), (0, 0)))
    fw_sh = jnp.pad(w["ffn_dw_w"], ((0, FF_HALO - 3), (0, 0)))
    pos = jnp.stack([shard, lax.axis_index("c")]).astype(jnp.int32)
    bufs = {n: _place("place_" + n, w[n], pos, BF16) for n in BIG}
    first = [bufs["w_in"], _place("place_conv_dw_w", cw_sh, pos, F32), _place("place_ffn_dw_w", fw_sh, pos, F32)]
    (bias,), first = _bias_tiles(w["rel_bias"], _fetch_rider(first))
    w_in_f, cw_f, fw_f = _alone("all_gather_forward", _forward_rider(list(first)))
    cw_full = jnp.transpose(cw_f, (1, 0, 2)).reshape(CONV_HALO, CONV_W)

    loss, gx, small, big = _local_step(
        x[0], loss_target[0], w["norm_mix_pre"], w_in_f, cw_full, w["conv_dw_b"], w["conv_ln_g"],
        w["conv_ln_b"], bias, bufs["w_out"], w["norm_mix_post"],
        w["norm_ffn_pre"], bufs["w_up"], fw_f, w["ffn_dw_b"].reshape(N_CHIPS, 1, FF_SHARD),
        bufs["w_down"], w["norm_ffn_post"], pos)
    grads, deltas, new_m, new_v = {}, {}, {}, {}
    for n in BIG:
        grads[n] = big[n]
        deltas[n], new_m[n], new_v[n] = _adamw("adamw_" + n, w[n], big[n], m[n], v[n])

    gsum = _all_reduce_small(_pack([small[n] for n in SMALL_ORDER] + [loss]))
    shapes = [small[n].shape for n in SMALL_ORDER]
    *reduced, total = _unpack(gsum, shapes + [loss.shape])
    gs = dict(zip(SMALL_ORDER, reduced))
    gs["conv_dw_w"] = lax.dynamic_slice_in_dim(gs["conv_dw_w"], shard * LANES, LANES, axis=1)
    gs["ffn_dw_w"] = lax.dynamic_slice_in_dim(gs["ffn_dw_w"], shard * FF_SHARD, FF_SHARD, axis=1)
    shapes = [gs[n].shape for n in SMALL_ORDER]
    d_p, m_p, v_p = _adamw("adamw_small", _pack([w[n] for n in SMALL_ORDER]), _pack([gs[n] for n in SMALL_ORDER]),
                           _pack([m[n] for n in SMALL_ORDER]), _pack([v[n] for n in SMALL_ORDER]))
    for dst, packed in ((deltas, d_p), (new_m, m_p), (new_v, v_p)):
        dst.update(zip(SMALL_ORDER, _unpack(packed, shapes)))
    grads.update(gs)

    outs = [total[0, 0], gx[None]]
    for group in (grads, deltas, new_m, new_v):
        outs += [group[n].reshape(args[n].shape) for n in WEIGHTS]
    return tuple(outs)
```

```python
import functools
import math
from typing import Callable, NamedTuple

import numpy as np
import jax
import jax.numpy as jnp
from jax import lax
from jax.experimental import pallas as pl
from jax.experimental.pallas import tpu as pltpu

F32 = jnp.float32
BF16 = jnp.bfloat16

D_MODEL = 1024
CONV_W = 512
ATTN_W = 512
N_HEADS = 8
HEAD_DIM = 64
CHUNK = 64
N_LEFT = 8
MAX_REL = 128
CONV_K = 31
CONV_HALO = 32
D_FF = 2816
FF_SHARD = 1408
IN_COLS = 2560
IN_SHARD = 640
EPS = 1e-6
NEG_INF = -1e30
ATT_BLK = 256
N_ATT_TILES = 3
LANES = 128
SUBLANES = 8
N_CHIPS = 4

ADAM_LR = 0.001
ADAM_B1 = 0.9
ADAM_B2 = 0.999
ADAM_EPS = 1e-08
ADAM_WD = 0.01
ADAM_STEP = 10

MESH = pl.DeviceIdType.MESH
ANY = pl.BlockSpec(memory_space=pl.ANY)
VMEM_FULL = pl.BlockSpec(memory_space=pltpu.VMEM)


def _params(vmem_mb, n_grid=0):
    sem = ("arbitrary",) * n_grid if n_grid else None
    return pltpu.CompilerParams(dimension_semantics=sem, vmem_limit_bytes=vmem_mb << 20)


class _Rider(NamedTuple):
    operands: list
    out_shape: list
    aliases: dict
    sems: list
    start: Callable
    finish: Callable


def _merge_riders(a, b):
    ia, oa, sa = len(a.operands), len(a.out_shape), len(a.sems)

    def start(ins, outs, sems):
        a.start(ins[:ia], outs[:oa], sems[:sa])
        b.start(ins[ia:], outs[oa:], sems[sa:])

    def finish(ins, outs, sems):
        a.finish(ins[:ia], outs[:oa], sems[:sa])
        b.finish(ins[ia:], outs[oa:], sems[sa:])

    aliases = {**a.aliases, **{k + ia: v + oa for k, v in b.aliases.items()}}
    return _Rider(a.operands + b.operands, a.out_shape + b.out_shape, aliases, a.sems + b.sems, start, finish)


PIN_BYTES = 1 << 20


def _big(a):
    return math.prod(a.shape) * jnp.dtype(a.dtype).itemsize >= PIN_BYTES


def _pin_args(args):
    return [pltpu.with_memory_space_constraint(a, pltpu.HBM) if _big(a) else a for a in args]


def _call(body, rider, *, grid=(), in_specs=(), out_specs=(), out_shape=(), scratch_shapes=(),
          input_output_aliases=None, **kwargs):
    in_specs, out_specs = list(in_specs), list(out_specs)
    pin_out = lambda shapes: [pltpu.HBM(s.shape, s.dtype) if _big(s) else s for s in shapes]
    out_shape = pin_out(out_shape)
    scratch, aliases = list(scratch_shapes), dict(input_output_aliases or {})
    if rider is None:
        plain = pl.pallas_call(body, grid=grid, in_specs=in_specs, out_specs=out_specs, out_shape=out_shape,
                               scratch_shapes=scratch, input_output_aliases=aliases, **kwargs)
        return lambda *args: (plain(*_pin_args(args)), [])
    n_in, n_out, n_scr = len(in_specs), len(out_specs), len(scratch)
    r_in, r_out = len(rider.operands), len(rider.out_shape)

    def carried(*refs):
        ins, r_ins, refs = refs[:n_in], refs[n_in:n_in + r_in], refs[n_in + r_in:]
        outs, r_outs, refs = refs[:n_out], refs[n_out:n_out + r_out], refs[n_out + r_out:]
        scr, r_sems = refs[:n_scr], refs[n_scr:]
        if not grid:
            rider.start(r_ins, r_outs, r_sems)
            body(*ins, *outs, *scr)
            rider.finish(r_ins, r_outs, r_sems)
            return
        at = [pl.program_id(d) for d in range(len(grid))]
        first = functools.reduce(jnp.logical_and, [p == 0 for p in at])
        last = functools.reduce(jnp.logical_and, [p == n - 1 for p, n in zip(at, grid)])

        @pl.when(first)
        def _():
            rider.start(r_ins, r_outs, r_sems)

        body(*ins, *outs, *scr)

        @pl.when(last)
        def _():
            rider.finish(r_ins, r_outs, r_sems)

    aliases.update({n_in + k: n_out + v for k, v in rider.aliases.items()})
    both = pl.pallas_call(carried, grid=grid, in_specs=in_specs + [ANY] * r_in, out_specs=out_specs + [ANY] * r_out,
                          out_shape=out_shape + pin_out(rider.out_shape), scratch_shapes=scratch + rider.sems,
                          input_output_aliases=aliases, **kwargs)

    def run(*args):
        res = both(*_pin_args(args), *rider.operands)
        return res[:n_out], res[n_out:]

    return run


def _sigmoid(v):
    return 1.0 / (1.0 + jnp.exp(-v))


def _dot(a, b):
    return jnp.dot(a, b, preferred_element_type=F32)


def _dot_nt(a, b):
    return lax.dot_general(a, b, (((1,), (1,)), ((), ())), preferred_element_type=F32)


def _dot_tn(a, b):
    return lax.dot_general(a, b, (((0,), (0,)), ((), ())), preferred_element_type=F32)


def _rms_fwd(v, g):
    r = lax.rsqrt(jnp.mean(v * v, axis=-1, keepdims=True) + EPS)
    return v * r * g, r


def _rms_bwd(dy, v, g):
    r = lax.rsqrt(jnp.mean(v * v, axis=-1, keepdims=True) + EPS)
    vh = v * r
    dvh = dy * g
    dv = r * (dvh - vh * jnp.mean(dvh * vh, axis=-1, keepdims=True))
    return dv, dy * vh


def _fwd_in_proj(x, g1, w_in, rider=None):
    T = x.shape[0]
    tm = 512

    def body(x_ref, g_ref, w_ref, u_ref, a_ref, qkv_ref):
        u, _ = _rms_fwd(x_ref[...], g_ref[...])
        u = u.astype(BF16)
        u_ref[...] = u
        for s in range(N_CHIPS):
            y = _dot(u, w_ref[s])
            lo, hi = IN_SHARD * s, IN_SHARD * (s + 1)
            if hi <= 1024:
                a_ref[:, lo:hi] = y
            elif lo >= 1024:
                qkv_ref[:, lo - 1024:hi - 1024] = y.astype(BF16)
            else:
                a_ref[:, lo:1024] = y[:, :1024 - lo]
                qkv_ref[:, 0:hi - 1024] = y[:, 1024 - lo:].astype(BF16)

    return _call(
        body, rider, name="fwd_in_proj", grid=(T // tm,),
        in_specs=[pl.BlockSpec((tm, D_MODEL), lambda i: (i, 0)),
                  pl.BlockSpec((1, D_MODEL), lambda i: (0, 0)),
                  pl.BlockSpec((N_CHIPS, D_MODEL, IN_SHARD), lambda i: (0, 0, 0))],
        out_specs=[pl.BlockSpec((tm, D_MODEL), lambda i: (i, 0)),
                   pl.BlockSpec((tm, 1024), lambda i: (i, 0)),
                   pl.BlockSpec((tm, 1536), lambda i: (i, 0))],
        out_shape=[jax.ShapeDtypeStruct((T, D_MODEL), BF16),
                   jax.ShapeDtypeStruct((T, 1024), F32),
                   jax.ShapeDtypeStruct((T, 1536), BF16)],
        compiler_params=_params(40, 1),
    )(x, g1, w_in)


def _fill_shifted(ext, shifted, tm):
    n = tm + CONV_HALO - SUBLANES
    for j in range(1, SUBLANES):
        shifted[j - 1] = ext[j:j + n, :]


def _shifted_rows(ext, shifted, start, rows):
    j = start % SUBLANES
    if j == 0:
        return ext[start:start + rows, :]
    return shifted[j - 1, start - j:start - j + rows, :]


def _fwd_conv(a, cw, cb, lg, lb, rider=None):
    T = a.shape[0]
    tm = 512
    rc = 64

    def body(a_ref, w_ref, b_ref, lg_ref, lb_ref, co_ref, hc_ref, hext, hsh):
        i = pl.program_id(0)

        @pl.when(i == 0)
        def _():
            hext[0:CONV_HALO, :] = jnp.zeros((CONV_HALO, CONV_W), F32)

        @pl.when(i > 0)
        def _():
            hext[0:CONV_HALO, :] = hext[tm:tm + CONV_HALO, :]

        hext[CONV_HALO:CONV_HALO + tm, :] = a_ref[:, :CONV_W] * _sigmoid(a_ref[:, CONV_W:])
        _fill_shifted(hext, hsh, tm)
        for c in range(tm // rc):
            acc = jnp.zeros((rc, CONV_W), F32)
            for k in range(CONV_K):
                acc = acc + w_ref[k:k + 1, :] * _shifted_rows(hext, hsh, c * rc + 2 + k, rc)
            hc = acc + b_ref[...]
            hc_ref[c * rc:(c + 1) * rc, :] = hc
            mu = jnp.mean(hc, axis=-1, keepdims=True)
            xc = hc - mu
            var = jnp.mean(xc * xc, axis=-1, keepdims=True)
            z = xc * lax.rsqrt(var + EPS) * lg_ref[...] + lb_ref[...]
            co_ref[c * rc:(c + 1) * rc, :] = (z * _sigmoid(z)).astype(BF16)

    return _call(
        body, rider, name="fwd_conv", grid=(T // tm,),
        in_specs=[pl.BlockSpec((tm, 1024), lambda i: (i, 0)),
                  pl.BlockSpec((CONV_HALO, CONV_W), lambda i: (0, 0)),
                  pl.BlockSpec((1, CONV_W), lambda i: (0, 0)),
                  pl.BlockSpec((1, CONV_W), lambda i: (0, 0)),
                  pl.BlockSpec((1, CONV_W), lambda i: (0, 0))],
        out_specs=[pl.BlockSpec((tm, CONV_W), lambda i: (i, 0)),
                   pl.BlockSpec((tm, CONV_W), lambda i: (i, 0))],
        out_shape=[jax.ShapeDtypeStruct((T, CONV_W), BF16),
                   jax.ShapeDtypeStruct((T, CONV_W), F32)],
        scratch_shapes=[pltpu.VMEM((tm + CONV_HALO, CONV_W), F32),
                        pltpu.VMEM((SUBLANES - 1, tm + CONV_HALO - SUBLANES, CONV_W), F32)],
        compiler_params=_params(40, 1),
    )(a, cw, cb, lg, lb)


def _row_skew(v, sign):
    rows, width = v.shape
    row = lax.broadcasted_iota(jnp.int32, (rows, 1), 0)
    for b in range(int(math.log2(rows))):
        shift = (1 << b) if sign > 0 else width - (1 << b)
        v = jnp.where(((row >> b) & 1) == 1, pltpu.roll(v, shift, 1), v)
    return v


def _att_visible(d):
    rq = lax.broadcasted_iota(jnp.int32, (ATT_BLK, ATT_BLK), 0) // CHUNK
    ck = lax.broadcasted_iota(jnp.int32, (ATT_BLK, ATT_BLK), 1) // CHUNK
    slack = ATT_BLK
    above = jnp.where(d == 0, 0, slack)
    below = jnp.where(d == 2, 0, slack)
    return (ck <= rq + above) & (ck >= rq - below)


def _bias_tiles(rel, rider=None):
    vec = jnp.transpose(rel[:, _rel_index()], (1, 0, 2)).reshape(N_ATT_TILES * N_HEADS, 1, 2 * ATT_BLK)

    def body(v_ref, o_ref):
        visible = _att_visible(pl.program_id(0))
        for h in range(N_HEADS):
            full = _row_skew(jnp.broadcast_to(v_ref[h], (ATT_BLK, 2 * ATT_BLK)), 1)
            o_ref[h] = jnp.where(visible, full[:, :ATT_BLK], NEG_INF)

    return _call(
        body, rider, name="bias_tiles", grid=(N_ATT_TILES,),
        in_specs=[pl.BlockSpec((N_HEADS, 1, 2 * ATT_BLK), lambda d: (d, 0, 0))],
        out_specs=[pl.BlockSpec((N_HEADS, ATT_BLK, ATT_BLK), lambda d: (d, 0, 0))],
        out_shape=[jax.ShapeDtypeStruct((N_ATT_TILES * N_HEADS, ATT_BLK, ATT_BLK), F32)],
        compiler_params=_params(32, 1),
    )(vec)


def _diag_sums(ds, rider=None):
    def body(d_ref, o_ref):
        wide = jnp.concatenate([d_ref[0], jnp.zeros((ATT_BLK, ATT_BLK), F32)], axis=1)
        o_ref[0] = jnp.sum(_row_skew(wide, -1), axis=0, keepdims=True)

    return _call(
        body, rider, name="diag_sums", grid=(N_ATT_TILES * N_HEADS,),
        in_specs=[pl.BlockSpec((1, ATT_BLK, ATT_BLK), lambda n: (n, 0, 0))],
        out_specs=[pl.BlockSpec((1, 1, 2 * ATT_BLK), lambda n: (n, 0, 0))],
        out_shape=[jax.ShapeDtypeStruct((N_ATT_TILES * N_HEADS, 1, 2 * ATT_BLK), F32)],
        compiler_params=_params(16, 1),
    )(ds)


def _head_mask(h):
    lane = lax.broadcasted_iota(jnp.int32, (1, LANES), 1)
    return (lane // HEAD_DIM) == (h % 2)


def _fwd_attn(qkv, bias, rider=None):
    T = qkv.shape[0]
    nb = T // ATT_BLK
    scale = HEAD_DIM ** -0.5

    def body(q_ref, k0_ref, k1_ref, k2_ref, v0_ref, v1_ref, v2_ref, b_ref, o_ref, lse_ref):
        i = pl.program_id(0)

        @pl.when(i >= N_ATT_TILES - 1)
        def _():
            block(i, False, q_ref, k0_ref, k1_ref, k2_ref, v0_ref, v1_ref, v2_ref, b_ref, o_ref, lse_ref)

        @pl.when(i < N_ATT_TILES - 1)
        def _():
            block(i, True, q_ref, k0_ref, k1_ref, k2_ref, v0_ref, v1_ref, v2_ref, b_ref, o_ref, lse_ref)

    def block(i, hide_absent, q_ref, k0_ref, k1_ref, k2_ref, v0_ref, v1_ref, v2_ref, b_ref, o_ref, lse_ref):
        k_refs = (k0_ref, k1_ref, k2_ref)
        v_refs = (v0_ref, v1_ref, v2_ref)
        lane = lax.broadcasted_iota(jnp.int32, (1, LANES), 1)
        lse_tile = jnp.zeros((ATT_BLK, LANES), F32)
        for g in range(N_HEADS // 2):
            cols = slice(g * LANES, (g + 1) * LANES)
            qg = q_ref[:, cols] * scale
            og = jnp.zeros((ATT_BLK, LANES), F32)
            for h in (2 * g, 2 * g + 1):
                hm = _head_mask(h)
                qh = jnp.where(hm, qg, jnp.zeros_like(qg))
                s = []
                for d in range(N_ATT_TILES):
                    sd = _dot_nt(qh, k_refs[d][:, cols]) + b_ref[d * N_HEADS + h]
                    if d > 0 and hide_absent:
                        sd = jnp.where(i >= d, sd, NEG_INF)
                    s.append(sd)
                m = jnp.maximum(jnp.maximum(jnp.max(s[0], axis=-1, keepdims=True),
                                            jnp.max(s[1], axis=-1, keepdims=True)),
                                jnp.max(s[2], axis=-1, keepdims=True))
                p = [jnp.exp(sd - m) for sd in s]
                l = (jnp.sum(p[0], axis=-1, keepdims=True) + jnp.sum(p[1], axis=-1, keepdims=True)
                     + jnp.sum(p[2], axis=-1, keepdims=True))
                oh = jnp.zeros((ATT_BLK, LANES), F32)
                for d in range(N_ATT_TILES):
                    vg = v_refs[d][:, cols]
                    oh = oh + _dot(p[d].astype(BF16), jnp.where(hm, vg, jnp.zeros_like(vg)))
                og = og + oh / l
                lse_tile = jnp.where(lane == h, m + jnp.log(l), lse_tile)
            o_ref[:, cols] = og.astype(BF16)
        lse_ref[...] = lse_tile

    def kv_spec(d, col):
        return pl.BlockSpec((ATT_BLK, ATTN_W), lambda i: (jnp.maximum(i - d, 0), col))

    return _call(
        body, rider, name="fwd_attn", grid=(nb,),
        in_specs=[pl.BlockSpec((ATT_BLK, ATTN_W), lambda i: (i, 0)),
                  kv_spec(0, 1), kv_spec(1, 1), kv_spec(2, 1),
                  kv_spec(0, 2), kv_spec(1, 2), kv_spec(2, 2),
                  pl.BlockSpec((N_ATT_TILES * N_HEADS, ATT_BLK, ATT_BLK), lambda i: (0, 0, 0))],
        out_specs=[pl.BlockSpec((ATT_BLK, ATTN_W), lambda i: (i, 0)),
                   pl.BlockSpec((ATT_BLK, LANES), lambda i: (i, 0))],
        out_shape=[jax.ShapeDtypeStruct((T, ATTN_W), BF16),
                   jax.ShapeDtypeStruct((T, LANES), F32)],
        compiler_params=_params(40, 1),
    )(qkv, qkv, qkv, qkv, qkv, qkv, qkv, bias)


def _fwd_out_proj(co, ao, w_out, x, g2, g3):
    T = x.shape[0]
    tm = 512

    def body(co_ref, ao_ref, w_ref, x_ref, g2_ref, g3_ref, mixed_ref, h1_ref, u2_ref):
        mixed = _dot(co_ref[...], w_ref[0:CONV_W, :]) + _dot(ao_ref[...], w_ref[CONV_W:, :])
        mixed_ref[...] = mixed
        y, _ = _rms_fwd(mixed, g2_ref[...])
        h1 = x_ref[...] + y
        h1_ref[...] = h1
        u2, _ = _rms_fwd(h1, g3_ref[...])
        u2_ref[...] = u2.astype(BF16)

    row = lambda w: pl.BlockSpec((tm, w), lambda i: (i, 0))
    vec = pl.BlockSpec((1, D_MODEL), lambda i: (0, 0))
    return _call(
        body, None, name="fwd_out_proj", grid=(T // tm,),
        in_specs=[row(CONV_W), row(ATTN_W), pl.BlockSpec((D_MODEL, D_MODEL), lambda i: (0, 0)),
                  row(D_MODEL), vec, vec],
        out_specs=[row(D_MODEL), row(D_MODEL), row(D_MODEL)],
        out_shape=[jax.ShapeDtypeStruct((T, D_MODEL), F32),
                   jax.ShapeDtypeStruct((T, D_MODEL), F32),
                   jax.ShapeDtypeStruct((T, D_MODEL), BF16)],
        compiler_params=_params(40, 1),
    )(co, ao, w_out, x, g2, g3)[0]


GELU_C = math.sqrt(2.0 / math.pi)
GELU_A = 0.044715


def _gelu_and_grad(v):
    sq = v * v
    th = jnp.tanh(v * (GELU_C + (GELU_C * GELU_A) * sq))
    half = 0.5 + 0.5 * th
    gl = v * half
    dgl = half + (v * (half * (1.0 - th))) * (GELU_C + (3.0 * GELU_C * GELU_A) * sq)
    return gl, dgl


FF_TM = 256
FF_HALO = 16
FF_CHUNKS = [(lo, min(lo + 256, FF_SHARD)) for lo in range(0, FF_SHARD, 256)]


def _rows_before(prev, cur):
    ext = jnp.concatenate([prev, cur], axis=0)
    return pltpu.roll(ext, 1, 0)[SUBLANES:], pltpu.roll(ext, 2, 0)[SUBLANES:]


def _rows_after(cur, nxt):
    ext = jnp.concatenate([cur, nxt], axis=0)
    n = ext.shape[0]
    return pltpu.roll(ext, n - 1, 0)[:cur.shape[0]], pltpu.roll(ext, n - 2, 0)[:cur.shape[0]]


def _fwd_ffn(u2, w_up, fw, fb, w_down):
    T = u2.shape[0]
    tm = FF_TM

    def body(u_ref, wg_ref, wv_ref, fwg_ref, fwv_ref, fbg_ref, fbv_ref, wd_ref, hf_ref, pre_ref, f_ref, carg, carv):
        i = pl.program_id(1)

        @pl.when(i == 0)
        def _():
            carg[...] = jnp.zeros(carg.shape, F32)
            carv[...] = jnp.zeros(carv.shape, F32)

        u = u_ref[...]
        f = None
        up = lambda lo, hi: (_dot(u, wg_ref[0, :, lo:hi]), _dot(u, wv_ref[0, :, lo:hi]))
        ahead = up(*FF_CHUNKS[0])
        for c, (lo, hi) in enumerate(FF_CHUNKS):
            conv = []
            hs = ahead
            if c + 1 < len(FF_CHUNKS):
                ahead = up(*FF_CHUNKS[c + 1])
            for n, (car, fw_ref, fb_ref) in enumerate(((carg, fwg_ref, fbg_ref), (carv, fwv_ref, fbv_ref))):
                h0 = hs[n]
                hf_ref[n, :, lo:hi] = h0.astype(BF16)
                h1, h2 = _rows_before(car[:, lo:hi], h0)
                car[:, lo:hi] = h0[tm - SUBLANES:, :]
                conv.append(fw_ref[0, 0:1, lo:hi] * h2 + fw_ref[0, 1:2, lo:hi] * h1
                            + fw_ref[0, 2:3, lo:hi] * h0 + fb_ref[0, :, lo:hi])
            pre_ref[0, :, lo:hi] = conv[0].astype(BF16)
            pre_ref[1, :, lo:hi] = conv[1].astype(BF16)
            gl, _ = _gelu_and_grad(conv[0])
            term = _dot((gl * conv[1]).astype(BF16), wd_ref[lo:hi, :])
            f = term if f is None else f + term
        f_ref[0] = f.astype(BF16)

    wspec = lambda off: pl.BlockSpec((1, D_MODEL, FF_SHARD), lambda s, i: (s + off, 0, 0))
    fwspec = lambda off: pl.BlockSpec((1, FF_HALO, FF_SHARD), lambda s, i: (s + off, 0, 0))
    fbspec = lambda off: pl.BlockSpec((1, 1, FF_SHARD), lambda s, i: (s + off, 0, 0))
    return _call(
        body, None, name="fwd_ffn", grid=(2, T // tm),
        in_specs=[pl.BlockSpec((tm, D_MODEL), lambda s, i: (i, 0)),
                  wspec(0), wspec(2), fwspec(0), fwspec(2), fbspec(0), fbspec(2),
                  pl.BlockSpec((FF_SHARD, D_MODEL), lambda s, i: (s, 0))],
        out_specs=[pl.BlockSpec((2, tm, FF_SHARD), lambda s, i: (0, i, s)),
                   pl.BlockSpec((2, tm, FF_SHARD), lambda s, i: (0, i, s)),
                   pl.BlockSpec((1, tm, D_MODEL), lambda s, i: (s, i, 0))],
        out_shape=[jax.ShapeDtypeStruct((2, T, D_FF), BF16),
                   jax.ShapeDtypeStruct((2, T, D_FF), BF16),
                   jax.ShapeDtypeStruct((2, T, D_MODEL), BF16)],
        scratch_shapes=[pltpu.VMEM((SUBLANES, FF_SHARD), F32), pltpu.VMEM((SUBLANES, FF_SHARD), F32)],
        compiler_params=_params(48, 2),
    )(u2, w_up, w_up, fw, fw, fb, fb, w_down)[0]


def _fwd_loss(fp, h1, tgt, g4):
    T = h1.shape[0]
    tm = 512

    def body(fp_ref, h1_ref, t_ref, g_ref, loss_ref, dy_ref, df_ref, dg_ref):
        i = pl.program_id(0)
        f = fp_ref[0].astype(F32) + fp_ref[1].astype(F32)
        r, _ = _rms_fwd(f, g_ref[...])
        e = (h1_ref[...] + r) - t_ref[...]
        dy = e * (1.0 / D_MODEL)
        dy_ref[...] = dy
        df, dg_rows = _rms_bwd(dy, f, g_ref[...])
        df_ref[...] = df.astype(BF16)
        part = 0.5 * jnp.sum(jnp.mean(e * e, axis=-1, keepdims=True), axis=0, keepdims=True)
        dg = jnp.sum(dg_rows, axis=0, keepdims=True)

        @pl.when(i == 0)
        def _():
            loss_ref[...] = part
            dg_ref[...] = dg

        @pl.when(i > 0)
        def _():
            loss_ref[...] += part
            dg_ref[...] += dg

    row = pl.BlockSpec((tm, D_MODEL), lambda i: (i, 0))
    vec = pl.BlockSpec((1, D_MODEL), lambda i: (0, 0))
    return _call(
        body, None, name="fwd_loss", grid=(T // tm,),
        in_specs=[pl.BlockSpec((2, tm, D_MODEL), lambda i: (0, i, 0)), row, row, vec],
        out_specs=[pl.BlockSpec((1, 1), lambda i: (0, 0)), row, row, vec],
        out_shape=[jax.ShapeDtypeStruct((1, 1), F32),
                   jax.ShapeDtypeStruct((T, D_MODEL), F32),
                   jax.ShapeDtypeStruct((T, D_MODEL), BF16),
                   jax.ShapeDtypeStruct((1, D_MODEL), F32)],
        compiler_params=_params(40, 1),
    )(fp, h1, tgt, g4)[0]


def _bwd_ffn(df, hf, pre, w_up, fw, w_down):
    T = df.shape[0]
    tm = FF_TM
    ni = T // tm

    def body(df_ref, hf_ref, pre_ref, wd_ref, wg_ref, wv_ref, fwg_ref, fwv_ref,
             du_ref, dhf_ref, act_ref, dwg_ref, dwv_ref, carg, carv):
        i = pl.program_id(1)

        @pl.when(i == 0)
        def _():
            dwg_ref[...] = jnp.zeros(dwg_ref.shape, F32)
            dwv_ref[...] = jnp.zeros(dwv_ref.shape, F32)
            carg[...] = jnp.zeros(carg.shape, F32)
            carv[...] = jnp.zeros(carv.shape, F32)

        df = df_ref[...]
        du = None
        down = lambda lo, hi: _dot_nt(df, wd_ref[lo:hi, :])
        ahead = down(*FF_CHUNKS[0])
        for c, (lo, hi) in enumerate(FF_CHUNKS):
            dact = ahead
            if c + 1 < len(FF_CHUNKS):
                ahead = down(*FF_CHUNKS[c + 1])
            pre_g = pre_ref[0, :, lo:hi].astype(F32)
            pre_v = pre_ref[1, :, lo:hi].astype(F32)
            gl, dgl = _gelu_and_grad(pre_g)
            act_ref[:, lo:hi] = (gl * pre_v).astype(BF16)
            dpre = (dact * pre_v * dgl, dact * gl)
            for n, (car, fw_ref, dw_ref, w_ref) in enumerate(
                    ((carg, fwg_ref, dwg_ref, wg_ref), (carv, fwv_ref, dwv_ref, wv_ref))):
                dp = dpre[n]
                h0 = hf_ref[n, :, lo:hi].astype(F32)
                up1, up2 = _rows_after(dp, car[:, lo:hi])
                car[:, lo:hi] = dp[0:SUBLANES, :]
                for k, shifted in enumerate((up2, up1, dp)):
                    dw_ref[0, k:k + 1, lo:hi] += jnp.sum(shifted * h0, axis=0, keepdims=True)
                dw_ref[0, 3:4, lo:hi] += jnp.sum(dp, axis=0, keepdims=True)
                dh = (fw_ref[0, 2:3, lo:hi] * dp + fw_ref[0, 1:2, lo:hi] * up1
                      + fw_ref[0, 0:1, lo:hi] * up2).astype(BF16)
                dhf_ref[n, :, lo:hi] = dh
                term = _dot_nt(dh, w_ref[0, :, lo:hi])
                du = term if du is None else du + term
        du_ref[0] = du.astype(BF16)

    rev = lambda i: ni - 1 - i
    wspec = lambda off: pl.BlockSpec((1, D_MODEL, FF_SHARD), lambda s, i: (s + off, 0, 0))
    fwspec = lambda off: pl.BlockSpec((1, FF_HALO, FF_SHARD), lambda s, i: (s + off, 0, 0))
    dwspec = pl.BlockSpec((1, FF_HALO, FF_SHARD), lambda s, i: (s, 0, 0))
    return _call(
        body, None, name="bwd_ffn", grid=(2, ni),
        in_specs=[pl.BlockSpec((tm, D_MODEL), lambda s, i: (rev(i), 0)),
                  pl.BlockSpec((2, tm, FF_SHARD), lambda s, i: (0, rev(i), s)),
                  pl.BlockSpec((2, tm, FF_SHARD), lambda s, i: (0, rev(i), s)),
                  pl.BlockSpec((FF_SHARD, D_MODEL), lambda s, i: (s, 0)),
                  wspec(0), wspec(2), fwspec(0), fwspec(2)],
        out_specs=[pl.BlockSpec((1, tm, D_MODEL), lambda s, i: (s, rev(i), 0)),
                   pl.BlockSpec((2, tm, FF_SHARD), lambda s, i: (0, rev(i), s)),
                   pl.BlockSpec((tm, FF_SHARD), lambda s, i: (rev(i), s)),
                   dwspec, dwspec],
        out_shape=[jax.ShapeDtypeStruct((2, T, D_MODEL), BF16),
                   jax.ShapeDtypeStruct((2, T, D_FF), BF16),
                   jax.ShapeDtypeStruct((T, D_FF), BF16),
                   jax.ShapeDtypeStruct((2, FF_HALO, FF_SHARD), F32),
                   jax.ShapeDtypeStruct((2, FF_HALO, FF_SHARD), F32)],
        scratch_shapes=[pltpu.VMEM((SUBLANES, FF_SHARD), F32), pltpu.VMEM((SUBLANES, FF_SHARD), F32)],
        compiler_params=_params(56, 2),
    )(df, hf, pre, w_down, w_up, w_up, fw, fw)[0]


def _bwd_mid(du2p, dy, h1, mixed, g3, g2, w_out, rider=None):
    T = dy.shape[0]
    tm = 512

    def body(du_ref, dy_ref, h1_ref, mx_ref, g3_ref, g2_ref, w_ref,
             dh1_ref, dmx_ref, dco_ref, dao_ref, dg3_ref, dg2_ref):
        i = pl.program_id(0)
        dres, dg3_rows = _rms_bwd(du_ref[0].astype(F32) + du_ref[1].astype(F32), h1_ref[...], g3_ref[...])
        dh1 = dy_ref[...] + dres
        dh1_ref[...] = dh1
        dmx, dg2_rows = _rms_bwd(dh1, mx_ref[...], g2_ref[...])
        dmx = dmx.astype(BF16)
        dmx_ref[...] = dmx
        dcat = _dot_nt(dmx, w_ref[...])
        dco_ref[...] = dcat[:, :CONV_W]
        dao_ref[...] = dcat[:, CONV_W:].astype(BF16)
        dg3 = jnp.sum(dg3_rows, axis=0, keepdims=True)
        dg2 = jnp.sum(dg2_rows, axis=0, keepdims=True)

        @pl.when(i == 0)
        def _():
            dg3_ref[...] = dg3
            dg2_ref[...] = dg2

        @pl.when(i > 0)
        def _():
            dg3_ref[...] += dg3
            dg2_ref[...] += dg2

    row = lambda w: pl.BlockSpec((tm, w), lambda i: (i, 0))
    vec = pl.BlockSpec((1, D_MODEL), lambda i: (0, 0))
    return _call(
        body, rider, name="bwd_mid", grid=(T // tm,),
        in_specs=[pl.BlockSpec((2, tm, D_MODEL), lambda i: (0, i, 0)), row(D_MODEL), row(D_MODEL),
                  row(D_MODEL), vec, vec, pl.BlockSpec((D_MODEL, D_MODEL), lambda i: (0, 0))],
        out_specs=[row(D_MODEL), row(D_MODEL), row(CONV_W), row(ATTN_W), vec, vec],
        out_shape=[jax.ShapeDtypeStruct((T, D_MODEL), F32),
                   jax.ShapeDtypeStruct((T, D_MODEL), BF16),
                   jax.ShapeDtypeStruct((T, CONV_W), F32),
                   jax.ShapeDtypeStruct((T, ATTN_W), BF16),
                   jax.ShapeDtypeStruct((1, D_MODEL), F32),
                   jax.ShapeDtypeStruct((1, D_MODEL), F32)],
        compiler_params=_params(48, 1),
    )(du2p, dy, h1, mixed, g3, g2, w_out)


def _bwd_attn(qkv, ao, dao, lse, bias, rider=None):
    T = qkv.shape[0]
    nb = T // ATT_BLK
    scale = HEAD_DIM ** -0.5

    def body(k_ref, v_ref, q0, q1, q2, do0, do1, do2, o0, o1, o2, l0, l1, l2, b_ref,
             dp_ref, ds_ref, acc1, acc2):
        j = pl.program_id(0)
        q_refs, do_refs, o_refs, l_refs = (q0, q1, q2), (do0, do1, do2), (o0, o1, o2), (l0, l1, l2)

        @pl.when(j == 0)
        def _():
            ds_ref[...] = jnp.zeros(ds_ref.shape, F32)
            acc1[...] = jnp.zeros(acc1.shape, F32)
            acc2[...] = jnp.zeros(acc2.shape, F32)

        dq_new = [[], [], []]
        dk_cols, dv_cols = [], []
        for g in range(N_HEADS // 2):
            cols = slice(g * LANES, (g + 1) * LANES)
            kg = k_ref[:, cols]
            vg = v_ref[:, cols]
            dkg = jnp.zeros((ATT_BLK, LANES), F32)
            dvg = jnp.zeros((ATT_BLK, LANES), F32)
            dqg = [jnp.zeros((ATT_BLK, LANES), F32) for _ in range(N_ATT_TILES)]
            for d in range(N_ATT_TILES):
                qg = q_refs[d][:, cols] * scale
                dog = do_refs[d][:, cols]
                if d > 0:
                    dog = jnp.where(j + d < nb, dog, jnp.zeros_like(dog))
                prod = dog.astype(F32) * o_refs[d][:, cols].astype(F32)
                for h in (2 * g, 2 * g + 1):
                    hm = _head_mask(h)
                    qh = jnp.where(hm, qg, jnp.zeros_like(qg))
                    doh = jnp.where(hm, dog, jnp.zeros_like(dog))
                    kh = jnp.where(hm, kg, jnp.zeros_like(kg))
                    delta = jnp.sum(jnp.where(hm, prod, 0.0), axis=-1, keepdims=True)
                    s = _dot_nt(qh, kg) + b_ref[d * N_HEADS + h]
                    p = jnp.exp(s - l_refs[d][:, h:h + 1])
                    dvg = dvg + _dot_tn(p.astype(BF16), doh)
                    dpm = _dot_nt(doh, vg)
                    dsc = p * (dpm - delta)
                    ds_ref[d * N_HEADS + h] += dsc
                    dsb = dsc.astype(BF16)
                    dqg[d] = dqg[d] + _dot(dsb, kh)
                    dkg = dkg + _dot_tn(dsb, qh)
            for d in range(N_ATT_TILES):
                dq_new[d].append(dqg[d])
            dk_cols.append(dkg)
            dv_cols.append(dvg)
        x0, x1, x2 = (jnp.concatenate(c, axis=1) * scale for c in dq_new)
        dp_ref[:, 0:1024] = jnp.zeros((ATT_BLK, 1024), BF16)
        dp_ref[:, 1024:1536] = (acc1[...] + x0).astype(BF16)
        dp_ref[:, 1536:2048] = jnp.concatenate(dk_cols, axis=1).astype(BF16)
        dp_ref[:, 2048:2560] = jnp.concatenate(dv_cols, axis=1).astype(BF16)
        acc1[...] = acc2[...] + x1
        acc2[...] = x2

    def fwd_spec(d, width, col):
        return pl.BlockSpec((ATT_BLK, width), lambda j: (jnp.minimum(j + d, nb - 1), col))

    return _call(
        body, rider, name="bwd_attn", grid=(nb,),
        in_specs=[pl.BlockSpec((ATT_BLK, ATTN_W), lambda j: (j, 1)),
                  pl.BlockSpec((ATT_BLK, ATTN_W), lambda j: (j, 2)),
                  fwd_spec(0, ATTN_W, 0), fwd_spec(1, ATTN_W, 0), fwd_spec(2, ATTN_W, 0),
                  fwd_spec(0, ATTN_W, 0), fwd_spec(1, ATTN_W, 0), fwd_spec(2, ATTN_W, 0),
                  fwd_spec(0, ATTN_W, 0), fwd_spec(1, ATTN_W, 0), fwd_spec(2, ATTN_W, 0),
                  fwd_spec(0, LANES, 0), fwd_spec(1, LANES, 0), fwd_spec(2, LANES, 0),
                  pl.BlockSpec((N_ATT_TILES * N_HEADS, ATT_BLK, ATT_BLK), lambda j: (0, 0, 0))],
        out_specs=[pl.BlockSpec((ATT_BLK, IN_COLS), lambda j: (j, 0)),
                   pl.BlockSpec((N_ATT_TILES * N_HEADS, ATT_BLK, ATT_BLK), lambda j: (0, 0, 0))],
        out_shape=[jax.ShapeDtypeStruct((T, IN_COLS), BF16),
                   jax.ShapeDtypeStruct((N_ATT_TILES * N_HEADS, ATT_BLK, ATT_BLK), F32)],
        scratch_shapes=[pltpu.VMEM((ATT_BLK, ATTN_W), F32), pltpu.VMEM((ATT_BLK, ATTN_W), F32)],
        compiler_params=_params(56, 1),
    )(qkv, qkv, qkv, qkv, qkv, dao, dao, dao, ao, ao, ao, lse, lse, lse, bias)


def _bwd_conv(dproj, a, dco, hc, cw, lg, lb, rider=None):
    T = a.shape[0]
    tm = 512
    rc = 32
    ni = T // tm
    hb = tm // CONV_HALO

    def body(dp_in, a_ref, ap_ref, dco_ref, dcon_ref, hc_ref, hcn_ref, w_ref, lg_ref, lb_ref,
             dp_ref, dw_ref, db_ref, dlg_ref, dlb_ref, hext, dext, hsh, dsh, dwacc):
        del dp_in
        i = pl.program_id(0)

        def ln_bwd(dco_v, hc_v):
            mu = jnp.mean(hc_v, axis=-1, keepdims=True)
            xc = hc_v - mu
            rstd = lax.rsqrt(jnp.mean(xc * xc, axis=-1, keepdims=True) + EPS)
            xh = xc * rstd
            z = xh * lg_ref[...] + lb_ref[...]
            sg = _sigmoid(z)
            dz = dco_v * (sg * (1.0 + z * (1.0 - sg)))
            dxh = dz * lg_ref[...]
            dhc = rstd * (dxh - jnp.mean(dxh, axis=-1, keepdims=True)
                          - xh * jnp.mean(dxh * xh, axis=-1, keepdims=True))
            return dhc, dz * xh, dz

        hext[0:CONV_HALO, :] = jnp.where(i > 0, ap_ref[:, :CONV_W] * _sigmoid(ap_ref[:, CONV_W:]), 0.0)
        hext[CONV_HALO:CONV_HALO + tm, :] = a_ref[:, :CONV_W] * _sigmoid(a_ref[:, CONV_W:])
        dhc, dlg_rows, dlb_rows = ln_bwd(dco_ref[...], hc_ref[...])
        dext[0:tm, :] = dhc
        dhc_next, _, _ = ln_bwd(dcon_ref[...], hcn_ref[...])
        dext[tm:tm + CONV_HALO, :] = jnp.where(i < ni - 1, dhc_next, 0.0)

        @pl.when(i == 0)
        def _():
            dw_ref[...] = jnp.zeros(dw_ref.shape, F32)
            db_ref[...] = jnp.zeros(db_ref.shape, F32)
            dlg_ref[...] = jnp.zeros(dlg_ref.shape, F32)
            dlb_ref[...] = jnp.zeros(dlb_ref.shape, F32)

            dwacc[...] = jnp.zeros(dwacc.shape, F32)

        db_ref[...] += jnp.sum(dhc, axis=0, keepdims=True)
        dlg_ref[...] += jnp.sum(dlg_rows, axis=0, keepdims=True)
        dlb_ref[...] += jnp.sum(dlb_rows, axis=0, keepdims=True)
        _fill_shifted(hext, hsh, tm)
        _fill_shifted(dext, dsh, tm)
        for c in range(tm // rc):
            r0 = c * rc
            dh = jnp.zeros((rc, CONV_W), F32)
            dhc_c = dext[r0:r0 + rc, :]
            for k in range(CONV_K):
                dh = dh + w_ref[k:k + 1, :] * _shifted_rows(dext, dsh, r0 + 30 - k, rc)
                prod = dhc_c * _shifted_rows(hext, hsh, r0 + 2 + k, rc)
                dwacc[k] += jnp.sum(prod.reshape(rc // SUBLANES, SUBLANES, CONV_W), axis=0)
            av = a_ref[r0:r0 + rc, :CONV_W]
            sg = _sigmoid(a_ref[r0:r0 + rc, CONV_W:])
            dp_ref[r0:r0 + rc, 0:CONV_W] = (dh * sg).astype(BF16)
            dp_ref[r0:r0 + rc, CONV_W:] = (dh * av * sg * (1.0 - sg)).astype(BF16)

        @pl.when(i == ni - 1)
        def _():
            dw_ref[...] = jnp.sum(dwacc[...], axis=1)

    row = lambda w: pl.BlockSpec((tm, w), lambda i: (i, 0))
    prev = lambda w: pl.BlockSpec((CONV_HALO, w), lambda i: (jnp.maximum(i * hb - 1, 0), 0))
    nxt = lambda w: pl.BlockSpec((CONV_HALO, w), lambda i: (jnp.minimum((i + 1) * hb, ni * hb - 1), 0))
    vec = pl.BlockSpec((1, CONV_W), lambda i: (0, 0))
    return _call(
        body, rider, name="bwd_conv", grid=(ni,),
        in_specs=[ANY, row(1024), prev(1024), row(CONV_W), nxt(CONV_W), row(CONV_W), nxt(CONV_W),
                  pl.BlockSpec((CONV_HALO, CONV_W), lambda i: (0, 0)), vec, vec],
        out_specs=[pl.BlockSpec((tm, 1024), lambda i: (i, 0)),
                   pl.BlockSpec((CONV_HALO, CONV_W), lambda i: (0, 0)), vec, vec, vec],
        out_shape=[jax.ShapeDtypeStruct((T, IN_COLS), BF16),
                   jax.ShapeDtypeStruct((CONV_HALO, CONV_W), F32),
                   jax.ShapeDtypeStruct((1, CONV_W), F32),
                   jax.ShapeDtypeStruct((1, CONV_W), F32),
                   jax.ShapeDtypeStruct((1, CONV_W), F32)],
        scratch_shapes=[pltpu.VMEM((tm + CONV_HALO, CONV_W), F32), pltpu.VMEM((tm + CONV_HALO, CONV_W), F32),
                        pltpu.VMEM((SUBLANES - 1, tm + CONV_HALO - SUBLANES, CONV_W), F32),
                        pltpu.VMEM((SUBLANES - 1, tm + CONV_HALO - SUBLANES, CONV_W), F32),
                        pltpu.VMEM((CONV_HALO, SUBLANES, CONV_W), F32)],
        input_output_aliases={0: 0},
        compiler_params=_params(56, 1),
    )(dproj, a, a, dco, dco, hc, hc, cw, lg, lb)


def _bwd_in_proj(dproj, w_in, x, dh1, g1, rider=None):
    T = x.shape[0]
    tm = 512

    def body(dp_ref, w_ref, x_ref, dh_ref, g_ref, gx_ref, dg_ref):
        i = pl.program_id(0)
        du = None
        for s in range(N_CHIPS):
            term = _dot_nt(dp_ref[:, IN_SHARD * s:IN_SHARD * (s + 1)], w_ref[s])
            du = term if du is None else du + term
        dx, dg_rows = _rms_bwd(du, x_ref[...], g_ref[...])
        gx_ref[...] = dh_ref[...] + dx
        dg = jnp.sum(dg_rows, axis=0, keepdims=True)

        @pl.when(i == 0)
        def _():
            dg_ref[...] = dg

        @pl.when(i > 0)
        def _():
            dg_ref[...] += dg

    row = lambda w: pl.BlockSpec((tm, w), lambda i: (i, 0))
    vec = pl.BlockSpec((1, D_MODEL), lambda i: (0, 0))
    return _call(
        body, rider, name="bwd_in_proj", grid=(T // tm,),
        in_specs=[row(IN_COLS), pl.BlockSpec((N_CHIPS, D_MODEL, IN_SHARD), lambda i: (0, 0, 0)),
                  row(D_MODEL), row(D_MODEL), vec],
        out_specs=[row(D_MODEL), vec],
        out_shape=[jax.ShapeDtypeStruct((T, D_MODEL), F32), jax.ShapeDtypeStruct((1, D_MODEL), F32)],
        compiler_params=_params(40, 1),
    )(dproj, w_in, x, dh1, g1)


def _wgrad(name, a_list, a_spec, b, b_spec, out_spec, out_shape, n_outer, T, tk, select=None, rider=None):
    def body(*refs):
        a_refs, b_ref, o_ref = refs[:len(a_list)], refs[len(a_list)], refs[len(a_list) + 1]
        kt = pl.program_id(1)

        @pl.when(kt == 0)
        def _():
            o_ref[...] = jnp.zeros(o_ref.shape, F32)

        bv = b_ref[...].reshape(b_ref.shape[-2:])
        if select is None:
            o_ref[...] += _dot_tn(a_refs[0][...].reshape(a_refs[0].shape[-2:]), bv).reshape(o_ref.shape)
        else:
            for n, a_ref in enumerate(a_refs):
                @pl.when(select(pl.program_id(0)) == n)
                def _():
                    o_ref[...] += _dot_tn(a_ref[...], bv).reshape(o_ref.shape)

    (res,), got = _call(
        body, rider, name=name, grid=(n_outer, T // tk),
        in_specs=[a_spec] * len(a_list) + [b_spec],
        out_specs=[out_spec], out_shape=[out_shape],
        compiler_params=_params(48, 2),
    )(*a_list, b)
    return (res, got) if rider is not None else res


def _mesh_pos():
    return lax.axis_index("x"), lax.axis_index("y"), lax.axis_index("c")


def _other_chips(x, y):
    return [((1 - x, y), 2 * (1 - x) + y), ((x, 1 - y), 2 * x + (1 - y)), ((1 - x, 1 - y), 2 * (1 - x) + (1 - y))]


def _exchange_rider(operands, out_shape, aliases, sem_shape, pairs):
    def start(ins, outs, sems):
        for send, _ in pairs(ins, outs, *sems):
            send.start()

    def finish(ins, outs, sems):
        for send, recv in pairs(ins, outs, *sems):
            send.wait_send()
            recv.wait_recv()

    sems = [pltpu.SemaphoreType.DMA(sem_shape), pltpu.SemaphoreType.DMA(sem_shape)]
    return _Rider(list(operands), list(out_shape), aliases, sems, start, finish)


def _remote(src, dst, send_sem, recv_sem, device):
    return pltpu.make_async_remote_copy(src_ref=src, dst_ref=dst, send_sem=send_sem, recv_sem=recv_sem,
                                        device_id=device, device_id_type=MESH)


def _fetch_rider(bufs):
    def pairs(ins, outs, send_sems, recv_sems):
        x, y, c = _mesh_pos()
        res = []
        for t, buf in enumerate(bufs):
            rows = pl.ds(c * (buf.shape[1] // 2), buf.shape[1] // 2)
            mine = outs[t].at[2 * x + y, rows]
            for k, (chip, s) in enumerate(_other_chips(x, y)):
                landed = outs[t].at[s, rows]
                res.append((_remote(mine, mine, send_sems.at[t, k], recv_sems.at[t, k], (*chip, c)),
                            _remote(landed, landed, send_sems.at[t, k], recv_sems.at[t, k], (*chip, c))))
        return res

    shapes = [jax.ShapeDtypeStruct(b.shape, b.dtype) for b in bufs]
    return _exchange_rider(bufs, shapes, {t: t for t in range(len(bufs))}, (len(bufs), 3), pairs)


def _forward_rider(bufs):
    def pairs(ins, outs, send_sems, recv_sems):
        x, y, c = _mesh_pos()
        res = []
        for t, buf in enumerate(bufs):
            half = buf.shape[1] // 2
            for k, (_, s) in enumerate(_other_chips(x, y)):
                landed = outs[t].at[s, pl.ds(c * half, half)]
                theirs = outs[t].at[s, pl.ds((1 - c) * half, half)]
                res.append((_remote(landed, landed, send_sems.at[t, k], recv_sems.at[t, k], (x, y, 1 - c)),
                            _remote(theirs, theirs, send_sems.at[t, k], recv_sems.at[t, k], (x, y, 1 - c))))
        return res

    shapes = [jax.ShapeDtypeStruct(b.shape, b.dtype) for b in bufs]
    return _exchange_rider(bufs, shapes, {t: t for t in range(len(bufs))}, (len(bufs), 3), pairs)


def _pair_exchange_rider(grads):
    def pairs(ins, outs, send_sems, recv_sems):
        x, y, c = _mesh_pos()
        res = []
        for t, g in enumerate(grads):
            half = g.shape[1] // 2
            cp = _remote(ins[t].at[:, pl.ds((1 - c) * half, half), :], outs[t], send_sems.at[t], recv_sems.at[t],
                         (x, y, 1 - c))
            res.append((cp, cp))
        return res

    shapes = [jax.ShapeDtypeStruct((N_CHIPS, g.shape[1] // 2, g.shape[2]), F32) for g in grads]
    return _exchange_rider(grads, shapes, {}, (len(grads),), pairs)


def _chip_exchange_rider(sums):
    def pairs(ins, outs, send_sems, recv_sems):
        x, y, c = _mesh_pos()
        res = []
        for t in range(len(sums)):
            for k, (chip, s) in enumerate(_other_chips(x, y)):
                cp = _remote(ins[t].at[s], outs[t].at[k], send_sems.at[t, k], recv_sems.at[t, k], (*chip, c))
                res.append((cp, cp))
        return res

    shapes = [jax.ShapeDtypeStruct((3,) + p.shape[1:], p.dtype) for p in sums]
    return _exchange_rider(sums, shapes, {}, (len(sums), 3), pairs)


def _pair_gather_rider(fulls):
    def pairs(ins, outs, send_sems, recv_sems):
        x, y, c = _mesh_pos()
        res = []
        for t, f in enumerate(fulls):
            half = f.shape[0] // 2
            mine = outs[t].at[pl.ds(c * half, half)]
            theirs = outs[t].at[pl.ds((1 - c) * half, half)]
            res.append((_remote(mine, mine, send_sems.at[t], recv_sems.at[t], (x, y, 1 - c)),
                        _remote(theirs, theirs, send_sems.at[t], recv_sems.at[t], (x, y, 1 - c))))
        return res

    shapes = [jax.ShapeDtypeStruct(f.shape, F32) for f in fulls]
    return _exchange_rider(fulls, shapes, {t: t for t in range(len(fulls))}, (len(fulls),), pairs)


def _alone(name, rider):
    return _call(lambda: None, rider, name=name)()[1]


def _all_reduce_small(pack):
    rows = pack.shape[0]

    def body(p_ref, o_ref, buf, send_sems, recv_sems):
        x, y, c = _mesh_pos()
        me = 4 * x + 2 * y + c
        buf[0] = p_ref[...]
        copies = []
        for k in range(1, 8):
            peer = (x ^ (k >> 2), y ^ ((k >> 1) & 1), c ^ (k & 1))
            cp = pltpu.make_async_remote_copy(
                src_ref=p_ref, dst_ref=buf.at[k], send_sem=send_sems.at[k - 1], recv_sem=recv_sems.at[k - 1],
                device_id=peer, device_id_type=MESH)
            cp.start()
            copies.append(cp)
        for cp in copies:
            cp.wait()
        total = buf[me]
        for dev in range(1, 8):
            total = total + buf[me ^ dev]
        o_ref[...] = total

    return pl.pallas_call(
        body, name="all_reduce_small",
        in_specs=[VMEM_FULL], out_specs=VMEM_FULL,
        out_shape=jax.ShapeDtypeStruct(pack.shape, F32),
        scratch_shapes=[pltpu.VMEM((8, rows, LANES), F32),
                        pltpu.SemaphoreType.DMA((7,)), pltpu.SemaphoreType.DMA((7,))],
    )(pack)


def _row_block(rows):
    if rows <= 512:
        return rows
    for rb in (256, 352):
        if rows % rb == 0:
            return rb
    raise ValueError(f"no row block for {rows} rows")


def _place(name, w, pos, dtype):
    R, C = w.shape
    rb = _row_block(R)

    def body(pos_ref, w_ref, o_ref):
        del pos_ref
        o_ref[0] = w_ref[...].astype(dtype)

    return pl.pallas_call(
        body, name=name,
        grid_spec=pltpu.PrefetchScalarGridSpec(
            num_scalar_prefetch=1, grid=(R // rb,),
            in_specs=[pl.BlockSpec((rb, C), lambda r, p: (r, 0))],
            out_specs=pl.BlockSpec((1, rb, C), lambda r, p: (p[0], r, 0))),
        out_shape=(pltpu.HBM if N_CHIPS * R * C * jnp.dtype(dtype).itemsize >= PIN_BYTES
                   else jax.ShapeDtypeStruct)((N_CHIPS, R, C), dtype),
        compiler_params=_params(32, 1),
    )(pos, w)


def _pair_sum(name, g, got, pos):
    S, R, C = g.shape
    half = R // 2
    rb = _row_block(half)
    nh = half // rb

    def body(pos_ref, a_ref, b_ref, o_ref):
        del pos_ref
        o_ref[...] = (a_ref[...] + b_ref[...]).astype(BF16)

    spec = pl.BlockSpec((1, rb, C), lambda s, r, p: (s, r, 0))
    return pl.pallas_call(
        body, name=name,
        grid_spec=pltpu.PrefetchScalarGridSpec(
            num_scalar_prefetch=1, grid=(S, nh),
            in_specs=[pl.BlockSpec((1, rb, C), lambda s, r, p: (s, p[1] * nh + r, 0)), spec],
            out_specs=spec),
        out_shape=jax.ShapeDtypeStruct((S, half, C), BF16), compiler_params=_params(32, 2),
    )(pos, g, got)


def _chip_sum(name, pairs, got, pos):
    _, half, C = pairs.shape
    rb = _row_block(half)
    nh = half // rb

    def body(pos_ref, a_ref, g_ref, o_ref):
        del pos_ref
        o_ref[...] = ((a_ref[0].astype(F32) + g_ref[0].astype(F32)) + g_ref[1].astype(F32)) + g_ref[2].astype(F32)

    return pl.pallas_call(
        body, name=name,
        grid_spec=pltpu.PrefetchScalarGridSpec(
            num_scalar_prefetch=1, grid=(nh,),
            in_specs=[pl.BlockSpec((1, rb, C), lambda r, p: (p[0], r, 0)),
                      pl.BlockSpec((3, rb, C), lambda r, p: (0, r, 0))],
            out_specs=pl.BlockSpec((rb, C), lambda r, p: (p[1] * nh + r, 0))),
        out_shape=jax.ShapeDtypeStruct((2 * half, C), F32), compiler_params=_params(32, 1),
    )(pos, pairs, got)


def _adamw(name, w, g, m, v):
    R, C = w.shape
    rb = _row_block(R)
    c1 = 1.0 - ADAM_B1 ** ADAM_STEP
    c2 = 1.0 - ADAM_B2 ** ADAM_STEP

    def body(w_ref, g_ref, m_ref, v_ref, d_ref, nm_ref, nv_ref):
        gv = g_ref[...]
        nm = ADAM_B1 * m_ref[...] + (1.0 - ADAM_B1) * gv
        nv = ADAM_B2 * v_ref[...] + (1.0 - ADAM_B2) * (gv * gv)
        nm_ref[...] = nm
        nv_ref[...] = nv
        d_ref[...] = -ADAM_LR * ((nm / c1) / (jnp.sqrt(nv / c2) + ADAM_EPS) + ADAM_WD * w_ref[...])

    spec = pl.BlockSpec((rb, C), lambda r: (r, 0))
    sds = jax.ShapeDtypeStruct(w.shape, F32)
    return pl.pallas_call(
        body, name=name, grid=(R // rb,), in_specs=[spec] * 4, out_specs=[spec] * 3,
        out_shape=[sds, sds, sds], compiler_params=_params(40, 1),
    )(w, g, m, v)


def _rel_index():
    m = np.arange(2 * ATT_BLK)
    off = np.where(m < ATT_BLK, m, m - 2 * ATT_BLK)
    rel = np.stack([ATT_BLK * d - off for d in range(N_ATT_TILES)])
    return np.clip(rel, -MAX_REL, MAX_REL) + MAX_REL


def _local_step(x, tgt, g1, w_in, cw, cb, lg, lb, bias, w_out, g2, g3, w_up, fw, fb, w_down, g4, pos=None):
    T = x.shape[0]
    dist = pos is not None
    idx = _rel_index()

    (u, a, qkv), got = _fwd_in_proj(x, g1, w_in, _fetch_rider([w_out, w_down]) if dist else None)
    if dist:
        w_out, w_down = got
    (co, hc), got = _fwd_conv(a, cw, cb, lg, lb, _merge_riders(_forward_rider([w_out, w_down]),
                                                               _fetch_rider([w_up])) if dist else None)
    if dist:
        w_out, w_down, w_up = got
    (ao, lse), got = _fwd_attn(qkv, bias, _forward_rider([w_up]) if dist else None)
    if dist:
        (w_up,) = got
        w_out, w_down = w_out.reshape(D_MODEL, D_MODEL), w_down.reshape(D_FF, D_MODEL)
    mixed, h1, u2 = _fwd_out_proj(co, ao, w_out, x, g2, g3)
    hf, pre, fp = _fwd_ffn(u2, w_up, fw, fb, w_down)
    loss, dy, df, dg4 = _fwd_loss(fp, h1, tgt, g4)

    tk = 1024
    du2p, dhf, act, dfw_g, dfw_v = _bwd_ffn(df, hf, pre, w_up, fw, w_down)
    gw_up = _wgrad(
        "wgrad_up", [u2], pl.BlockSpec((tk, D_MODEL), lambda s, k: (k, 0)),
        dhf, pl.BlockSpec((1, tk, FF_SHARD), lambda s, k: (s // 2, k, s % 2)),
        pl.BlockSpec((1, D_MODEL, FF_SHARD), lambda s, k: (s, 0, 0)),
        jax.ShapeDtypeStruct((N_CHIPS, D_MODEL, FF_SHARD), F32), N_CHIPS, T, tk)
    gw_down = _wgrad(
        "wgrad_down", [act], pl.BlockSpec((tk, FF_SHARD), lambda s, k: (k, s)),
        df, pl.BlockSpec((tk, D_MODEL), lambda s, k: (k, 0)),
        pl.BlockSpec((FF_SHARD, D_MODEL), lambda s, k: (s, 0)),
        jax.ShapeDtypeStruct((D_FF, D_MODEL), F32), 2, T, tk).reshape(N_CHIPS, D_FF // N_CHIPS, D_MODEL)
    (dh1, dmx, dco, dao, dg3, dg2), _ = _bwd_mid(du2p, dy, h1, mixed, g3, g2, w_out)
    gw_out = _wgrad(
        "wgrad_out", [co, ao], pl.BlockSpec((tk, CONV_W), lambda s, k: (k, 0)),
        dmx, pl.BlockSpec((tk, D_MODEL), lambda s, k: (k, 0)),
        pl.BlockSpec((CONV_W, D_MODEL), lambda s, k: (s, 0)),
        jax.ShapeDtypeStruct((D_MODEL, D_MODEL), F32), 2, T, tk,
        select=lambda s: s, rider=_pair_exchange_rider([gw_up, gw_down]) if dist else None)
    if dist:
        gw_out, got = gw_out
        p_up = _pair_sum("pair_sum_w_up", gw_up, got[0], pos)
        p_down = _pair_sum("pair_sum_w_down", gw_down, got[1], pos)
    gw_out = gw_out.reshape(N_CHIPS, D_MODEL // N_CHIPS, D_MODEL)
    (dproj, dsacc), got = _bwd_attn(
        qkv, ao, dao, lse, bias,
        _merge_riders(_chip_exchange_rider([p_up, p_down]), _pair_exchange_rider([gw_out])) if dist else None)
    if dist:
        gw_up = _chip_sum("chip_sum_w_up", p_up, got[0], pos)
        gw_down = _chip_sum("chip_sum_w_down", p_down, got[1], pos)
        p_out = _pair_sum("pair_sum_w_out", gw_out, got[2], pos)
    (dproj, dcw, dcb, dlg, dlb), got = _bwd_conv(
        dproj, a, dco, hc, cw, lg, lb,
        _merge_riders(_pair_gather_rider([gw_up, gw_down]), _chip_exchange_rider([p_out])) if dist else None)
    if dist:
        gw_up, gw_down = got[:2]
        gw_out = _chip_sum("chip_sum_w_out", p_out, got[2], pos)
    gw_in = _wgrad(
        "wgrad_in", [u], pl.BlockSpec((tk, D_MODEL), lambda s, k: (k, 0)),
        dproj, pl.BlockSpec((tk, IN_SHARD), lambda s, k: (k, s)),
        pl.BlockSpec((1, D_MODEL, IN_SHARD), lambda s, k: (s, 0, 0)),
        jax.ShapeDtypeStruct((N_CHIPS, D_MODEL, IN_SHARD), F32), N_CHIPS, T, tk)
    if dist:
        got = _alone("pair_exchange_w_in", _merge_riders(_pair_exchange_rider([gw_in]), _pair_gather_rider([gw_out])))
        p_in, gw_out = _pair_sum("pair_sum_w_in", gw_in, got[0], pos), got[1]
    (gx, dg1), _ = _bwd_in_proj(dproj, w_in, x, dh1, g1)
    (diag,), got = _diag_sums(dsacc, _chip_exchange_rider([p_in]) if dist else None)
    if dist:
        (gw_in,) = _alone("pair_gather_w_in", _pair_gather_rider([_chip_sum("chip_sum_w_in", p_in, got[0], pos)]))

    diag = diag.reshape(N_ATT_TILES, N_HEADS, 2 * ATT_BLK)
    onehot = np.zeros((N_ATT_TILES, 2 * ATT_BLK, 2 * MAX_REL + 1), np.float32)
    for d in range(N_ATT_TILES):
        onehot[d, np.arange(2 * ATT_BLK), idx[d]] = 1.0
    drel = jnp.einsum("dhm,dmr->hr", diag, jnp.asarray(onehot), precision=lax.Precision.HIGHEST)

    small = dict(norm_mix_pre=dg1, conv_dw_w=dcw[:CONV_K], conv_dw_b=dcb, conv_ln_g=dlg, conv_ln_b=dlb,
                 rel_bias=drel, norm_mix_post=dg2, norm_ffn_pre=dg3,
                 ffn_dw_w=jnp.concatenate([dfw_g[0, :3], dfw_g[1, :3], dfw_v[0, :3], dfw_v[1, :3]], axis=1),
                 ffn_dw_b=jnp.concatenate([dfw_g[0, 3:4], dfw_g[1, 3:4], dfw_v[0, 3:4], dfw_v[1, 3:4]], axis=1),
                 norm_ffn_post=dg4)
    return loss, gx, small, dict(w_in=gw_in, w_out=gw_out, w_up=gw_up, w_down=gw_down)


SMALL_ORDER = ["norm_mix_pre", "conv_dw_b", "conv_ln_g", "conv_ln_b", "rel_bias", "norm_mix_post",
               "norm_ffn_pre", "ffn_dw_b", "norm_ffn_post", "conv_dw_w", "ffn_dw_w"]


def _pack(parts):
    rows = []
    for p in parts:
        width = -(-p.shape[1] // LANES) * LANES
        rows.append(jnp.pad(p, ((0, 0), (0, width - p.shape[1]))).reshape(-1, LANES))
    packed = jnp.concatenate(rows, axis=0)
    pad = -packed.shape[0] % 8
    return jnp.pad(packed, ((0, pad), (0, 0)))


def _unpack(packed, shapes):
    out, r = [], 0
    for shp in shapes:
        width = -(-shp[1] // LANES) * LANES
        n = shp[0] * width // LANES
        out.append(packed[r:r + n].reshape(shp[0], width)[:, :shp[1]])
        r += n
    return out


WEIGHTS = ["norm_mix_pre", "w_in", "conv_dw_w", "conv_dw_b", "conv_ln_g", "conv_ln_b", "rel_bias", "w_out",
           "norm_mix_post", "norm_ffn_pre", "w_up", "ffn_dw_w", "ffn_dw_b", "w_down", "norm_ffn_post"]
BIG = ["w_in", "w_out", "w_up", "w_down"]


def kernel(x, norm_mix_pre, w_in, conv_dw_w, conv_dw_b, conv_ln_g, conv_ln_b, rel_bias, w_out, norm_mix_post, norm_ffn_pre, w_up, ffn_dw_w, ffn_dw_b, w_down, norm_ffn_post, loss_target, m_norm_mix_pre, m_w_in, m_conv_dw_w, m_conv_dw_b, m_conv_ln_g, m_conv_ln_b, m_rel_bias, m_w_out, m_norm_mix_post, m_norm_ffn_pre, m_w_up, m_ffn_dw_w, m_ffn_dw_b, m_w_down, m_norm_ffn_post, v_norm_mix_pre, v_w_in, v_conv_dw_w, v_conv_dw_b, v_conv_ln_g, v_conv_ln_b, v_rel_bias, v_w_out, v_norm_mix_post, v_norm_ffn_pre, v_w_up, v_ffn_dw_w, v_ffn_dw_b, v_w_down, v_norm_ffn_post):
    args = locals()
    w = {n: args[n][0] for n in WEIGHTS}
    m = {n: args["m_" + n][0] for n in WEIGHTS}
    v = {n: args["v_" + n][0] for n in WEIGHTS}
    for d in (w, m, v):
        d["rel_bias"] = d["rel_bias"].reshape(N_HEADS, 2 * MAX_REL + 1)
        for n in ("norm_mix_pre", "conv_dw_b", "conv_ln_g", "conv_ln_b", "norm_mix_post", "norm_ffn_pre",
                  "ffn_dw_b", "norm_ffn_post"):
            d[n] = d[n].reshape(1, -1)
    shard = 2 * lax.axis_index("x") + lax.axis_index("y")

    cw_sh = jnp.pad(w["conv_dw_w"], ((0, CONV_HALO - CONV_K), (0, 0)))
    fw_sh = jnp.pad(w["ffn_dw_w"], ((0, FF_HALO - 3), (0, 0)))
    pos = jnp.stack([shard, lax.axis_index("c")]).astype(jnp.int32)
    bufs = {n: _place("place_" + n, w[n], pos, BF16) for n in BIG}
    first = [bufs["w_in"], _place("place_conv_dw_w", cw_sh, pos, F32), _place("place_ffn_dw_w", fw_sh, pos, F32)]
    (bias,), first = _bias_tiles(w["rel_bias"], _fetch_rider(first))
    w_in_f, cw_f, fw_f = _alone("all_gather_forward", _forward_rider(list(first)))
    cw_full = jnp.transpose(cw_f, (1, 0, 2)).reshape(CONV_HALO, CONV_W)

    loss, gx, small, big = _local_step(
        x[0], loss_target[0], w["norm_mix_pre"], w_in_f, cw_full, w["conv_dw_b"], w["conv_ln_g"],
        w["conv_ln_b"], bias, bufs["w_out"], w["norm_mix_post"],
        w["norm_ffn_pre"], bufs["w_up"], fw_f, w["ffn_dw_b"].reshape(N_CHIPS, 1, FF_SHARD),
        bufs["w_down"], w["norm_ffn_post"], pos)
    grads, deltas, new_m, new_v = {}, {}, {}, {}
    for n in BIG:
        grads[n] = big[n]
        deltas[n], new_m[n], new_v[n] = _adamw("adamw_" + n, w[n], big[n], m[n], v[n])

    gsum = _all_reduce_small(_pack([small[n] for n in SMALL_ORDER] + [loss]))
    shapes = [small[n].shape for n in SMALL_ORDER]
    *reduced, total = _unpack(gsum, shapes + [loss.shape])
    gs = dict(zip(SMALL_ORDER, reduced))
    gs["conv_dw_w"] = lax.dynamic_slice_in_dim(gs["conv_dw_w"], shard * LANES, LANES, axis=1)
    gs["ffn_dw_w"] = lax.dynamic_slice_in_dim(gs["ffn_dw_w"], shard * FF_SHARD, FF_SHARD, axis=1)
    shapes = [gs[n].shape for n in SMALL_ORDER]
    d_p, m_p, v_p = _adamw("adamw_small", _pack([w[n] for n in SMALL_ORDER]), _pack([gs[n] for n in SMALL_ORDER]),
                           _pack([m[n] for n in SMALL_ORDER]), _pack([v[n] for n in SMALL_ORDER]))
    for dst, packed in ((deltas, d_p), (new_m, m_p), (new_v, v_p)):
        dst.update(zip(SMALL_ORDER, _unpack(packed, shapes)))
    grads.update(gs)

    outs = [total[0, 0], gx[None]]
    for group in (grads, deltas, new_m, new_v):
        outs += [group[n].reshape(args[n].shape) for n in WEIGHTS]
    return tuple(outs)
```

```python
import functools
import math
from typing import Callable, NamedTuple

import numpy as np
import jax
import jax.numpy as jnp
from jax import lax
from jax.experimental import pallas as pl
from jax.experimental.pallas import tpu as pltpu

F32 = jnp.float32
BF16 = jnp.bfloat16

D_MODEL = 1024
CONV_W = 512
ATTN_W = 512
N_HEADS = 8
HEAD_DIM = 64
CHUNK = 64
N_LEFT = 8
MAX_REL = 128
CONV_K = 31
CONV_HALO = 32
D_FF = 2816
FF_SHARD = 1408
IN_COLS = 2560
IN_SHARD = 640
EPS = 1e-6
NEG_INF = -1e30
ATT_BLK = 256
N_ATT_TILES = 3
LANES = 128
SUBLANES = 8
N_CHIPS = 4

ADAM_LR = 0.001
ADAM_B1 = 0.9
ADAM_B2 = 0.999
ADAM_EPS = 1e-08
ADAM_WD = 0.01
ADAM_STEP = 10

MESH = pl.DeviceIdType.MESH
ANY = pl.BlockSpec(memory_space=pl.ANY)
VMEM_FULL = pl.BlockSpec(memory_space=pltpu.VMEM)


def _params(vmem_mb, n_grid=0):
    sem = ("arbitrary",) * n_grid if n_grid else None
    return pltpu.CompilerParams(dimension_semantics=sem, vmem_limit_bytes=vmem_mb << 20)


class _Rider(NamedTuple):
    operands: list
    out_shape: list
    aliases: dict
    sems: list
    start: Callable
    finish: Callable


def _merge_riders(a, b):
    ia, oa, sa = len(a.operands), len(a.out_shape), len(a.sems)

    def start(ins, outs, sems):
        a.start(ins[:ia], outs[:oa], sems[:sa])
        b.start(ins[ia:], outs[oa:], sems[sa:])

    def finish(ins, outs, sems):
        a.finish(ins[:ia], outs[:oa], sems[:sa])
        b.finish(ins[ia:], outs[oa:], sems[sa:])

    aliases = {**a.aliases, **{k + ia: v + oa for k, v in b.aliases.items()}}
    return _Rider(a.operands + b.operands, a.out_shape + b.out_shape, aliases, a.sems + b.sems, start, finish)


PIN_BYTES = 1 << 20


def _big(a):
    return math.prod(a.shape) * jnp.dtype(a.dtype).itemsize >= PIN_BYTES


def _pin_args(args):
    return [pltpu.with_memory_space_constraint(a, pltpu.HBM) if _big(a) else a for a in args]


def _call(body, rider, *, grid=(), in_specs=(), out_specs=(), out_shape=(), scratch_shapes=(),
          input_output_aliases=None, **kwargs):
    in_specs, out_specs = list(in_specs), list(out_specs)
    pin_out = lambda shapes: [pltpu.HBM(s.shape, s.dtype) if _big(s) else s for s in shapes]
    out_shape = pin_out(out_shape)
    scratch, aliases = list(scratch_shapes), dict(input_output_aliases or {})
    if rider is None:
        plain = pl.pallas_call(body, grid=grid, in_specs=in_specs, out_specs=out_specs, out_shape=out_shape,
                               scratch_shapes=scratch, input_output_aliases=aliases, **kwargs)
        return lambda *args: (plain(*_pin_args(args)), [])
    n_in, n_out, n_scr = len(in_specs), len(out_specs), len(scratch)
    r_in, r_out = len(rider.operands), len(rider.out_shape)

    def carried(*refs):
        ins, r_ins, refs = refs[:n_in], refs[n_in:n_in + r_in], refs[n_in + r_in:]
        outs, r_outs, refs = refs[:n_out], refs[n_out:n_out + r_out], refs[n_out + r_out:]
        scr, r_sems = refs[:n_scr], refs[n_scr:]
        if not grid:
            rider.start(r_ins, r_outs, r_sems)
            body(*ins, *outs, *scr)
            rider.finish(r_ins, r_outs, r_sems)
            return
        at = [pl.program_id(d) for d in range(len(grid))]
        first = functools.reduce(jnp.logical_and, [p == 0 for p in at])
        last = functools.reduce(jnp.logical_and, [p == n - 1 for p, n in zip(at, grid)])

        @pl.when(first)
        def _():
            rider.start(r_ins, r_outs, r_sems)

        body(*ins, *outs, *scr)

        @pl.when(last)
        def _():
            rider.finish(r_ins, r_outs, r_sems)

    aliases.update({n_in + k: n_out + v for k, v in rider.aliases.items()})
    both = pl.pallas_call(carried, grid=grid, in_specs=in_specs + [ANY] * r_in, out_specs=out_specs + [ANY] * r_out,
                          out_shape=out_shape + pin_out(rider.out_shape), scratch_shapes=scratch + rider.sems,
                          input_output_aliases=aliases, **kwargs)

    def run(*args):
        res = both(*_pin_args(args), *rider.operands)
        return res[:n_out], res[n_out:]

    return run


def _sigmoid(v):
    return 1.0 / (1.0 + jnp.exp(-v))


def _dot(a, b):
    return jnp.dot(a, b, preferred_element_type=F32)


def _dot_nt(a, b):
    return lax.dot_general(a, b, (((1,), (1,)), ((), ())), preferred_element_type=F32)


def _dot_tn(a, b):
    return lax.dot_general(a, b, (((0,), (0,)), ((), ())), preferred_element_type=F32)


def _rms_fwd(v, g):
    r = lax.rsqrt(jnp.mean(v * v, axis=-1, keepdims=True) + EPS)
    return v * r * g, r


def _rms_bwd(dy, v, g):
    r = lax.rsqrt(jnp.mean(v * v, axis=-1, keepdims=True) + EPS)
    vh = v * r
    dvh = dy * g
    dv = r * (dvh - vh * jnp.mean(dvh * vh, axis=-1, keepdims=True))
    return dv, dy * vh


def _fwd_in_proj(x, g1, w_in, rider=None):
    T = x.shape[0]
    tm = 512

    def body(x_ref, g_ref, w_ref, u_ref, a_ref, qkv_ref):
        u, _ = _rms_fwd(x_ref[...], g_ref[...])
        u = u.astype(BF16)
        u_ref[...] = u
        for s in range(N_CHIPS):
            y = _dot(u, w_ref[s])
            lo, hi = IN_SHARD * s, IN_SHARD * (s + 1)
            if hi <= 1024:
                a_ref[:, lo:hi] = y
            elif lo >= 1024:
                qkv_ref[:, lo - 1024:hi - 1024] = y.astype(BF16)
            else:
                a_ref[:, lo:1024] = y[:, :1024 - lo]
                qkv_ref[:, 0:hi - 1024] = y[:, 1024 - lo:].astype(BF16)

    return _call(
        body, rider, name="fwd_in_proj", grid=(T // tm,),
        in_specs=[pl.BlockSpec((tm, D_MODEL), lambda i: (i, 0)),
                  pl.BlockSpec((1, D_MODEL), lambda i: (0, 0)),
                  pl.BlockSpec((N_CHIPS, D_MODEL, IN_SHARD), lambda i: (0, 0, 0))],
        out_specs=[pl.BlockSpec((tm, D_MODEL), lambda i: (i, 0)),
                   pl.BlockSpec((tm, 1024), lambda i: (i, 0)),
                   pl.BlockSpec((tm, 1536), lambda i: (i, 0))],
        out_shape=[jax.ShapeDtypeStruct((T, D_MODEL), BF16),
                   jax.ShapeDtypeStruct((T, 1024), F32),
                   jax.ShapeDtypeStruct((T, 1536), BF16)],
        compiler_params=_params(40, 1),
    )(x, g1, w_in)


def _fill_shifted(ext, shifted, tm):
    n = tm + CONV_HALO - SUBLANES
    for j in range(1, SUBLANES):
        shifted[j - 1] = ext[j:j + n, :]


def _shifted_rows(ext, shifted, start, rows):
    j = start % SUBLANES
    if j == 0:
        return ext[start:start + rows, :]
    return shifted[j - 1, start - j:start - j + rows, :]


def _fwd_conv(a, cw, cb, lg, lb, rider=None):
    T = a.shape[0]
    tm = 512
    rc = 64

    def body(a_ref, w_ref, b_ref, lg_ref, lb_ref, co_ref, hc_ref, hext, hsh):
        i = pl.program_id(0)

        @pl.when(i == 0)
        def _():
            hext[0:CONV_HALO, :] = jnp.zeros((CONV_HALO, CONV_W), F32)

        @pl.when(i > 0)
        def _():
            hext[0:CONV_HALO, :] = hext[tm:tm + CONV_HALO, :]

        hext[CONV_HALO:CONV_HALO + tm, :] = a_ref[:, :CONV_W] * _sigmoid(a_ref[:, CONV_W:])
        _fill_shifted(hext, hsh, tm)
        for c in range(tm // rc):
            acc = jnp.zeros((rc, CONV_W), F32)
            for k in range(CONV_K):
                acc = acc + w_ref[k:k + 1, :] * _shifted_rows(hext, hsh, c * rc + 2 + k, rc)
            hc = acc + b_ref[...]
            hc_ref[c * rc:(c + 1) * rc, :] = hc
            mu = jnp.mean(hc, axis=-1, keepdims=True)
            xc = hc - mu
            var = jnp.mean(xc * xc, axis=-1, keepdims=True)
            z = xc * lax.rsqrt(var + EPS) * lg_ref[...] + lb_ref[...]
            co_ref[c * rc:(c + 1) * rc, :] = (z * _sigmoid(z)).astype(BF16)

    return _call(
        body, rider, name="fwd_conv", grid=(T // tm,),
        in_specs=[pl.BlockSpec((tm, 1024), lambda i: (i, 0)),
                  pl.BlockSpec((CONV_HALO, CONV_W), lambda i: (0, 0)),
                  pl.BlockSpec((1, CONV_W), lambda i: (0, 0)),
                  pl.BlockSpec((1, CONV_W), lambda i: (0, 0)),
                  pl.BlockSpec((1, CONV_W), lambda i: (0, 0))],
        out_specs=[pl.BlockSpec((tm, CONV_W), lambda i: (i, 0)),
                   pl.BlockSpec((tm, CONV_W), lambda i: (i, 0))],
        out_shape=[jax.ShapeDtypeStruct((T, CONV_W), BF16),
                   jax.ShapeDtypeStruct((T, CONV_W), F32)],
        scratch_shapes=[pltpu.VMEM((tm + CONV_HALO, CONV_W), F32),
                        pltpu.VMEM((SUBLANES - 1, tm + CONV_HALO - SUBLANES, CONV_W), F32)],
        compiler_params=_params(40, 1),
    )(a, cw, cb, lg, lb)


def _row_skew(v, sign):
    rows, width = v.shape
    row = lax.broadcasted_iota(jnp.int32, (rows, 1), 0)
    for b in range(int(math.log2(rows))):
        shift = (1 << b) if sign > 0 else width - (1 << b)
        v = jnp.where(((row >> b) & 1) == 1, pltpu.roll(v, shift, 1), v)
    return v


def _att_visible(d):
    rq = lax.broadcasted_iota(jnp.int32, (ATT_BLK, ATT_BLK), 0) // CHUNK
    ck = lax.broadcasted_iota(jnp.int32, (ATT_BLK, ATT_BLK), 1) // CHUNK
    slack = ATT_BLK
    above = jnp.where(d == 0, 0, slack)
    below = jnp.where(d == 2, 0, slack)
    return (ck <= rq + above) & (ck >= rq - below)


def _bias_tiles(rel, rider=None):
    vec = jnp.transpose(rel[:, _rel_index()], (1, 0, 2)).reshape(N_ATT_TILES * N_HEADS, 1, 2 * ATT_BLK)

    def body(v_ref, o_ref):
        visible = _att_visible(pl.program_id(0))
        for h in range(N_HEADS):
            full = _row_skew(jnp.broadcast_to(v_ref[h], (ATT_BLK, 2 * ATT_BLK)), 1)
            o_ref[h] = jnp.where(visible, full[:, :ATT_BLK], NEG_INF)

    return _call(
        body, rider, name="bias_tiles", grid=(N_ATT_TILES,),
        in_specs=[pl.BlockSpec((N_HEADS, 1, 2 * ATT_BLK), lambda d: (d, 0, 0))],
        out_specs=[pl.BlockSpec((N_HEADS, ATT_BLK, ATT_BLK), lambda d: (d, 0, 0))],
        out_shape=[jax.ShapeDtypeStruct((N_ATT_TILES * N_HEADS, ATT_BLK, ATT_BLK), F32)],
        compiler_params=_params(32, 1),
    )(vec)


def _diag_sums(ds, rider=None):
    def body(d_ref, o_ref):
        wide = jnp.concatenate([d_ref[0], jnp.zeros((ATT_BLK, ATT_BLK), F32)], axis=1)
        o_ref[0] = jnp.sum(_row_skew(wide, -1), axis=0, keepdims=True)

    return _call(
        body, rider, name="diag_sums", grid=(N_ATT_TILES * N_HEADS,),
        in_specs=[pl.BlockSpec((1, ATT_BLK, ATT_BLK), lambda n: (n, 0, 0))],
        out_specs=[pl.BlockSpec((1, 1, 2 * ATT_BLK), lambda n: (n, 0, 0))],
        out_shape=[jax.ShapeDtypeStruct((N_ATT_TILES * N_HEADS, 1, 2 * ATT_BLK), F32)],
        compiler_params=_params(16, 1),
    )(ds)


def _head_mask(h):
    lane = lax.broadcasted_iota(jnp.int32, (1, LANES), 1)
    return (lane // HEAD_DIM) == (h % 2)


def _fwd_attn(qkv, bias, rider=None):
    T = qkv.shape[0]
    nb = T // ATT_BLK
    scale = HEAD_DIM ** -0.5

    def body(q_ref, k0_ref, k1_ref, k2_ref, v0_ref, v1_ref, v2_ref, b_ref, o_ref, lse_ref):
        i = pl.program_id(0)

        @pl.when(i >= N_ATT_TILES - 1)
        def _():
            block(i, False, q_ref, k0_ref, k1_ref, k2_ref, v0_ref, v1_ref, v2_ref, b_ref, o_ref, lse_ref)

        @pl.when(i < N_ATT_TILES - 1)
        def _():
            block(i, True, q_ref, k0_ref, k1_ref, k2_ref, v0_ref, v1_ref, v2_ref, b_ref, o_ref, lse_ref)

    def block(i, hide_absent, q_ref, k0_ref, k1_ref, k2_ref, v0_ref, v1_ref, v2_ref, b_ref, o_ref, lse_ref):
        k_refs = (k0_ref, k1_ref, k2_ref)
        v_refs = (v0_ref, v1_ref, v2_ref)
        lane = lax.broadcasted_iota(jnp.int32, (1, LANES), 1)
        lse_tile = jnp.zeros((ATT_BLK, LANES), F32)
        for g in range(N_HEADS // 2):
            cols = slice(g * LANES, (g + 1) * LANES)
            qg = q_ref[:, cols] * scale
            og = jnp.zeros((ATT_BLK, LANES), F32)
            for h in (2 * g, 2 * g + 1):
                hm = _head_mask(h)
                qh = jnp.where(hm, qg, jnp.zeros_like(qg))
                s = []
                for d in range(N_ATT_TILES):
                    sd = _dot_nt(qh, k_refs[d][:, cols]) + b_ref[d * N_HEADS + h]
                    if d > 0 and hide_absent:
                        sd = jnp.where(i >= d, sd, NEG_INF)
                    s.append(sd)
                m = jnp.maximum(jnp.maximum(jnp.max(s[0], axis=-1, keepdims=True),
                                            jnp.max(s[1], axis=-1, keepdims=True)),
                                jnp.max(s[2], axis=-1, keepdims=True))
                p = [jnp.exp(sd - m) for sd in s]
                l = (jnp.sum(p[0], axis=-1, keepdims=True) + jnp.sum(p[1], axis=-1, keepdims=True)
                     + jnp.sum(p[2], axis=-1, keepdims=True))
                oh = jnp.zeros((ATT_BLK, LANES), F32)
                for d in range(N_ATT_TILES):
                    vg = v_refs[d][:, cols]
                    oh = oh + _dot(p[d].astype(BF16), jnp.where(hm, vg, jnp.zeros_like(vg)))
                og = og + oh / l
                lse_tile = jnp.where(lane == h, m + jnp.log(l), lse_tile)
            o_ref[:, cols] = og.astype(BF16)
        lse_ref[...] = lse_tile

    def kv_spec(d, col):
        return pl.BlockSpec((ATT_BLK, ATTN_W), lambda i: (jnp.maximum(i - d, 0), col))

    return _call(
        body, rider, name="fwd_attn", grid=(nb,),
        in_specs=[pl.BlockSpec((ATT_BLK, ATTN_W), lambda i: (i, 0)),
                  kv_spec(0, 1), kv_spec(1, 1), kv_spec(2, 1),
                  kv_spec(0, 2), kv_spec(1, 2), kv_spec(2, 2),
                  pl.BlockSpec((N_ATT_TILES * N_HEADS, ATT_BLK, ATT_BLK), lambda i: (0, 0, 0))],
        out_specs=[pl.BlockSpec((ATT_BLK, ATTN_W), lambda i: (i, 0)),
                   pl.BlockSpec((ATT_BLK, LANES), lambda i: (i, 0))],
        out_shape=[jax.ShapeDtypeStruct((T, ATTN_W), BF16),
                   jax.ShapeDtypeStruct((T, LANES), F32)],
        compiler_params=_params(40, 1),
    )(qkv, qkv, qkv, qkv, qkv, qkv, qkv, bias)


def _fwd_out_proj(co, ao, w_out, x, g2, g3):
    T = x.shape[0]
    tm = 512

    def body(co_ref, ao_ref, w_ref, x_ref, g2_ref, g3_ref, mixed_ref, h1_ref, u2_ref):
        mixed = _dot(co_ref[...], w_ref[0:CONV_W, :]) + _dot(ao_ref[...], w_ref[CONV_W:, :])
        mixed_ref[...] = mixed.astype(BF16)
        y, _ = _rms_fwd(mixed, g2_ref[...])
        h1 = x_ref[...] + y
        h1_ref[...] = h1
        u2, _ = _rms_fwd(h1, g3_ref[...])
        u2_ref[...] = u2.astype(BF16)

    row = lambda w: pl.BlockSpec((tm, w), lambda i: (i, 0))
    vec = pl.BlockSpec((1, D_MODEL), lambda i: (0, 0))
    return _call(
        body, None, name="fwd_out_proj", grid=(T // tm,),
        in_specs=[row(CONV_W), row(ATTN_W), pl.BlockSpec((D_MODEL, D_MODEL), lambda i: (0, 0)),
                  row(D_MODEL), vec, vec],
        out_specs=[row(D_MODEL), row(D_MODEL), row(D_MODEL)],
        out_shape=[jax.ShapeDtypeStruct((T, D_MODEL), BF16),
                   jax.ShapeDtypeStruct((T, D_MODEL), F32),
                   jax.ShapeDtypeStruct((T, D_MODEL), BF16)],
        compiler_params=_params(40, 1),
    )(co, ao, w_out, x, g2, g3)[0]


GELU_C = math.sqrt(2.0 / math.pi)
GELU_A = 0.044715


def _gelu_and_grad(v):
    sq = v * v
    th = jnp.tanh(v * (GELU_C + (GELU_C * GELU_A) * sq))
    half = 0.5 + 0.5 * th
    gl = v * half
    dgl = half + (v * (half * (1.0 - th))) * (GELU_C + (3.0 * GELU_C * GELU_A) * sq)
    return gl, dgl


FF_TM = 256
FF_HALO = 16
FF_CHUNKS = [(lo, min(lo + 256, FF_SHARD)) for lo in range(0, FF_SHARD, 256)]


def _rows_before(prev, cur):
    ext = jnp.concatenate([prev, cur], axis=0)
    return pltpu.roll(ext, 1, 0)[SUBLANES:], pltpu.roll(ext, 2, 0)[SUBLANES:]


def _rows_after(cur, nxt):
    ext = jnp.concatenate([cur, nxt], axis=0)
    n = ext.shape[0]
    return pltpu.roll(ext, n - 1, 0)[:cur.shape[0]], pltpu.roll(ext, n - 2, 0)[:cur.shape[0]]


def _pair_up_weights(w_up):
    rb = 256

    def body(g_ref, v_ref, o_ref):
        for lo, hi in FF_CHUNKS:
            o_ref[0, :, 2 * lo:lo + hi] = g_ref[0, :, lo:hi]
            o_ref[0, :, lo + hi:2 * hi] = v_ref[0, :, lo:hi]

    return pl.pallas_call(
        body, name="pair_up_weights", grid=(2, D_MODEL // rb),
        in_specs=[pl.BlockSpec((1, rb, FF_SHARD), lambda s, r: (s, r, 0)),
                  pl.BlockSpec((1, rb, FF_SHARD), lambda s, r: (s + 2, r, 0))],
        out_specs=pl.BlockSpec((1, rb, 2 * FF_SHARD), lambda s, r: (s, r, 0)),
        out_shape=jax.ShapeDtypeStruct((2, D_MODEL, 2 * FF_SHARD), w_up.dtype),
        compiler_params=_params(32, 2),
    )(w_up, w_up)


def _fwd_ffn(u2, w_cat, fw, fb, w_down):
    T = u2.shape[0]
    tm = FF_TM

    def body(u_ref, w_ref, fwg_ref, fwv_ref, fbg_ref, fbv_ref, wd_ref, hf_ref, pre_ref, f_ref, carg, carv):
        i = pl.program_id(1)

        @pl.when(i == 0)
        def _():
            carg[...] = jnp.zeros(carg.shape, F32)
            carv[...] = jnp.zeros(carv.shape, F32)

        u = u_ref[...]
        f = None
        up = lambda lo, hi: _dot(u, w_ref[0, :, 2 * lo:2 * hi])
        ahead = up(*FF_CHUNKS[0])
        for c, (lo, hi) in enumerate(FF_CHUNKS):
            conv = []
            hs = (ahead[:, :hi - lo], ahead[:, hi - lo:])
            if c + 1 < len(FF_CHUNKS):
                ahead = up(*FF_CHUNKS[c + 1])
            for n, (car, fw_ref, fb_ref) in enumerate(((carg, fwg_ref, fbg_ref), (carv, fwv_ref, fbv_ref))):
                h0 = hs[n]
                hf_ref[n, :, lo:hi] = h0.astype(BF16)
                h1, h2 = _rows_before(car[:, lo:hi], h0)
                car[:, lo:hi] = h0[tm - SUBLANES:, :]
                conv.append(fw_ref[0, 0:1, lo:hi] * h2 + fw_ref[0, 1:2, lo:hi] * h1
                            + fw_ref[0, 2:3, lo:hi] * h0 + fb_ref[0, :, lo:hi])
            pre_ref[0, :, lo:hi] = conv[0].astype(BF16)
            pre_ref[1, :, lo:hi] = conv[1].astype(BF16)
            gl, _ = _gelu_and_grad(conv[0])
            term = _dot((gl * conv[1]).astype(BF16), wd_ref[lo:hi, :])
            f = term if f is None else f + term
        f_ref[0] = f.astype(BF16)

    fwspec = lambda off: pl.BlockSpec((1, FF_HALO, FF_SHARD), lambda s, i: (s + off, 0, 0))
    fbspec = lambda off: pl.BlockSpec((1, 1, FF_SHARD), lambda s, i: (s + off, 0, 0))
    return _call(
        body, None, name="fwd_ffn", grid=(2, T // tm),
        in_specs=[pl.BlockSpec((tm, D_MODEL), lambda s, i: (i, 0)),
                  pl.BlockSpec((1, D_MODEL, 2 * FF_SHARD), lambda s, i: (s, 0, 0)),
                  fwspec(0), fwspec(2), fbspec(0), fbspec(2),
                  pl.BlockSpec((FF_SHARD, D_MODEL), lambda s, i: (s, 0))],
        out_specs=[pl.BlockSpec((2, tm, FF_SHARD), lambda s, i: (0, i, s)),
                   pl.BlockSpec((2, tm, FF_SHARD), lambda s, i: (0, i, s)),
                   pl.BlockSpec((1, tm, D_MODEL), lambda s, i: (s, i, 0))],
        out_shape=[jax.ShapeDtypeStruct((2, T, D_FF), BF16),
                   jax.ShapeDtypeStruct((2, T, D_FF), BF16),
                   jax.ShapeDtypeStruct((2, T, D_MODEL), BF16)],
        scratch_shapes=[pltpu.VMEM((SUBLANES, FF_SHARD), F32), pltpu.VMEM((SUBLANES, FF_SHARD), F32)],
        compiler_params=_params(48, 2),
    )(u2, w_cat, fw, fw, fb, fb, w_down)[0]


def _fwd_loss(fp, h1, tgt, g4):
    T = h1.shape[0]
    tm = 512

    def body(fp_ref, h1_ref, t_ref, g_ref, loss_ref, dy_ref, df_ref, dg_ref):
        i = pl.program_id(0)
        f = fp_ref[0].astype(F32) + fp_ref[1].astype(F32)
        r, _ = _rms_fwd(f, g_ref[...])
        e = (h1_ref[...] + r) - t_ref[...]
        dy = e * (1.0 / D_MODEL)
        dy_ref[...] = dy
        df, dg_rows = _rms_bwd(dy, f, g_ref[...])
        df_ref[...] = df.astype(BF16)
        part = 0.5 * jnp.sum(jnp.mean(e * e, axis=-1, keepdims=True), axis=0, keepdims=True)
        dg = jnp.sum(dg_rows, axis=0, keepdims=True)

        @pl.when(i == 0)
        def _():
            loss_ref[...] = part
            dg_ref[...] = dg

        @pl.when(i > 0)
        def _():
            loss_ref[...] += part
            dg_ref[...] += dg

    row = pl.BlockSpec((tm, D_MODEL), lambda i: (i, 0))
    vec = pl.BlockSpec((1, D_MODEL), lambda i: (0, 0))
    return _call(
        body, None, name="fwd_loss", grid=(T // tm,),
        in_specs=[pl.BlockSpec((2, tm, D_MODEL), lambda i: (0, i, 0)), row, row, vec],
        out_specs=[pl.BlockSpec((1, 1), lambda i: (0, 0)), row, row, vec],
        out_shape=[jax.ShapeDtypeStruct((1, 1), F32),
                   jax.ShapeDtypeStruct((T, D_MODEL), F32),
                   jax.ShapeDtypeStruct((T, D_MODEL), BF16),
                   jax.ShapeDtypeStruct((1, D_MODEL), F32)],
        compiler_params=_params(40, 1),
    )(fp, h1, tgt, g4)[0]


def _bwd_ffn(df, hf, pre, w_cat, fw, w_down):
    T = df.shape[0]
    tm = FF_TM
    ni = T // tm

    def body(df_ref, hf_ref, pre_ref, wd_ref, w_ref, fwg_ref, fwv_ref,
             du_ref, dhf_ref, act_ref, dwg_ref, dwv_ref, carg, carv):
        i = pl.program_id(1)

        @pl.when(i == 0)
        def _():
            dwg_ref[...] = jnp.zeros(dwg_ref.shape, F32)
            dwv_ref[...] = jnp.zeros(dwv_ref.shape, F32)
            carg[...] = jnp.zeros(carg.shape, F32)
            carv[...] = jnp.zeros(carv.shape, F32)

        df = df_ref[...]
        du = None
        down = lambda lo, hi: _dot_nt(df, wd_ref[lo:hi, :])
        ahead = down(*FF_CHUNKS[0])
        for c, (lo, hi) in enumerate(FF_CHUNKS):
            dact = ahead
            if c + 1 < len(FF_CHUNKS):
                ahead = down(*FF_CHUNKS[c + 1])
            pre_g = pre_ref[0, :, lo:hi].astype(F32)
            pre_v = pre_ref[1, :, lo:hi].astype(F32)
            gl, dgl = _gelu_and_grad(pre_g)
            act_ref[:, lo:hi] = (gl * pre_v).astype(BF16)
            dpre = (dact * pre_v * dgl, dact * gl)
            dhs = []
            for n, (car, fw_ref, dw_ref) in enumerate(((carg, fwg_ref, dwg_ref), (carv, fwv_ref, dwv_ref))):
                dp = dpre[n]
                h0 = hf_ref[n, :, lo:hi].astype(F32)
                up1, up2 = _rows_after(dp, car[:, lo:hi])
                car[:, lo:hi] = dp[0:SUBLANES, :]
                for k, shifted in enumerate((up2, up1, dp)):
                    dw_ref[0, k:k + 1, lo:hi] += jnp.sum(shifted * h0, axis=0, keepdims=True)
                dw_ref[0, 3:4, lo:hi] += jnp.sum(dp, axis=0, keepdims=True)
                dh = (fw_ref[0, 2:3, lo:hi] * dp + fw_ref[0, 1:2, lo:hi] * up1
                      + fw_ref[0, 0:1, lo:hi] * up2).astype(BF16)
                dhf_ref[n, :, lo:hi] = dh
                dhs.append(dh)
            term = _dot_nt(jnp.concatenate(dhs, axis=1), w_ref[0, :, 2 * lo:2 * hi])
            du = term if du is None else du + term
        du_ref[0] = du.astype(BF16)

    rev = lambda i: ni - 1 - i
    fwspec = lambda off: pl.BlockSpec((1, FF_HALO, FF_SHARD), lambda s, i: (s + off, 0, 0))
    dwspec = pl.BlockSpec((1, FF_HALO, FF_SHARD), lambda s, i: (s, 0, 0))
    return _call(
        body, None, name="bwd_ffn", grid=(2, ni),
        in_specs=[pl.BlockSpec((tm, D_MODEL), lambda s, i: (rev(i), 0)),
                  pl.BlockSpec((2, tm, FF_SHARD), lambda s, i: (0, rev(i), s)),
                  pl.BlockSpec((2, tm, FF_SHARD), lambda s, i: (0, rev(i), s)),
                  pl.BlockSpec((FF_SHARD, D_MODEL), lambda s, i: (s, 0)),
                  pl.BlockSpec((1, D_MODEL, 2 * FF_SHARD), lambda s, i: (s, 0, 0)),
                  fwspec(0), fwspec(2)],
        out_specs=[pl.BlockSpec((1, tm, D_MODEL), lambda s, i: (s, rev(i), 0)),
                   pl.BlockSpec((2, tm, FF_SHARD), lambda s, i: (0, rev(i), s)),
                   pl.BlockSpec((tm, FF_SHARD), lambda s, i: (rev(i), s)),
                   dwspec, dwspec],
        out_shape=[jax.ShapeDtypeStruct((2, T, D_MODEL), BF16),
                   jax.ShapeDtypeStruct((2, T, D_FF), BF16),
                   jax.ShapeDtypeStruct((T, D_FF), BF16),
                   jax.ShapeDtypeStruct((2, FF_HALO, FF_SHARD), F32),
                   jax.ShapeDtypeStruct((2, FF_HALO, FF_SHARD), F32)],
        scratch_shapes=[pltpu.VMEM((SUBLANES, FF_SHARD), F32), pltpu.VMEM((SUBLANES, FF_SHARD), F32)],
        compiler_params=_params(56, 2),
    )(df, hf, pre, w_down, w_cat, fw, fw)[0]


def _bwd_mid(du2p, dy, h1, mixed, g3, g2, w_out, rider=None):
    T = dy.shape[0]
    tm = 512

    def body(du_ref, dy_ref, h1_ref, mx_ref, g3_ref, g2_ref, w_ref,
             dh1_ref, dmx_ref, dco_ref, dao_ref, dg3_ref, dg2_ref):
        i = pl.program_id(0)
        dres, dg3_rows = _rms_bwd(du_ref[0].astype(F32) + du_ref[1].astype(F32), h1_ref[...], g3_ref[...])
        dh1 = dy_ref[...] + dres
        dh1_ref[...] = dh1
        dmx, dg2_rows = _rms_bwd(dh1, mx_ref[...].astype(F32), g2_ref[...])
        dmx = dmx.astype(BF16)
        dmx_ref[...] = dmx
        dcat = _dot_nt(dmx, w_ref[...])
        dco_ref[...] = dcat[:, :CONV_W]
        dao_ref[...] = dcat[:, CONV_W:].astype(BF16)
        dg3 = jnp.sum(dg3_rows, axis=0, keepdims=True)
        dg2 = jnp.sum(dg2_rows, axis=0, keepdims=True)

        @pl.when(i == 0)
        def _():
            dg3_ref[...] = dg3
            dg2_ref[...] = dg2

        @pl.when(i > 0)
        def _():
            dg3_ref[...] += dg3
            dg2_ref[...] += dg2

    row = lambda w: pl.BlockSpec((tm, w), lambda i: (i, 0))
    vec = pl.BlockSpec((1, D_MODEL), lambda i: (0, 0))
    return _call(
        body, rider, name="bwd_mid", grid=(T // tm,),
        in_specs=[pl.BlockSpec((2, tm, D_MODEL), lambda i: (0, i, 0)), row(D_MODEL), row(D_MODEL),
                  row(D_MODEL), vec, vec, pl.BlockSpec((D_MODEL, D_MODEL), lambda i: (0, 0))],
        out_specs=[row(D_MODEL), row(D_MODEL), row(CONV_W), row(ATTN_W), vec, vec],
        out_shape=[jax.ShapeDtypeStruct((T, D_MODEL), F32),
                   jax.ShapeDtypeStruct((T, D_MODEL), BF16),
                   jax.ShapeDtypeStruct((T, CONV_W), F32),
                   jax.ShapeDtypeStruct((T, ATTN_W), BF16),
                   jax.ShapeDtypeStruct((1, D_MODEL), F32),
                   jax.ShapeDtypeStruct((1, D_MODEL), F32)],
        compiler_params=_params(48, 1),
    )(du2p, dy, h1, mixed, g3, g2, w_out)


def _bwd_attn(qkv, ao, dao, lse, bias, rider=None):
    T = qkv.shape[0]
    nb = T // ATT_BLK
    scale = HEAD_DIM ** -0.5

    def body(k_ref, v_ref, q0, q1, q2, do0, do1, do2, o0, o1, o2, l0, l1, l2, b_ref,
             dp_ref, ds_ref, acc1, acc2):
        j = pl.program_id(0)
        q_refs, do_refs, o_refs, l_refs = (q0, q1, q2), (do0, do1, do2), (o0, o1, o2), (l0, l1, l2)

        @pl.when(j == 0)
        def _():
            ds_ref[...] = jnp.zeros(ds_ref.shape, F32)
            acc1[...] = jnp.zeros(acc1.shape, F32)
            acc2[...] = jnp.zeros(acc2.shape, F32)

        dq_new = [[], [], []]
        dk_cols, dv_cols = [], []
        for g in range(N_HEADS // 2):
            cols = slice(g * LANES, (g + 1) * LANES)
            kg = k_ref[:, cols]
            vg = v_ref[:, cols]
            dkg = jnp.zeros((ATT_BLK, LANES), F32)
            dvg = jnp.zeros((ATT_BLK, LANES), F32)
            dqg = [jnp.zeros((ATT_BLK, LANES), F32) for _ in range(N_ATT_TILES)]
            for d in range(N_ATT_TILES):
                qg = q_refs[d][:, cols] * scale
                dog = do_refs[d][:, cols]
                if d > 0:
                    dog = jnp.where(j + d < nb, dog, jnp.zeros_like(dog))
                prod = dog.astype(F32) * o_refs[d][:, cols].astype(F32)
                for h in (2 * g, 2 * g + 1):
                    hm = _head_mask(h)
                    qh = jnp.where(hm, qg, jnp.zeros_like(qg))
                    doh = jnp.where(hm, dog, jnp.zeros_like(dog))
                    kh = jnp.where(hm, kg, jnp.zeros_like(kg))
                    delta = jnp.sum(jnp.where(hm, prod, 0.0), axis=-1, keepdims=True)
                    s = _dot_nt(qh, kg) + b_ref[d * N_HEADS + h]
                    p = jnp.exp(s - l_refs[d][:, h:h + 1])
                    dvg = dvg + _dot_tn(p.astype(BF16), doh)
                    dpm = _dot_nt(doh, vg)
                    dsc = p * (dpm - delta)
                    ds_ref[d * N_HEADS + h] += dsc
                    dsb = dsc.astype(BF16)
                    dqg[d] = dqg[d] + _dot(dsb, kh)
                    dkg = dkg + _dot_tn(dsb, qh)
            for d in range(N_ATT_TILES):
                dq_new[d].append(dqg[d])
            dk_cols.append(dkg)
            dv_cols.append(dvg)
        x0, x1, x2 = (jnp.concatenate(c, axis=1) * scale for c in dq_new)
        dp_ref[:, 0:1024] = jnp.zeros((ATT_BLK, 1024), BF16)
        dp_ref[:, 1024:1536] = (acc1[...] + x0).astype(BF16)
        dp_ref[:, 1536:2048] = jnp.concatenate(dk_cols, axis=1).astype(BF16)
        dp_ref[:, 2048:2560] = jnp.concatenate(dv_cols, axis=1).astype(BF16)
        acc1[...] = acc2[...] + x1
        acc2[...] = x2

    def fwd_spec(d, width, col):
        return pl.BlockSpec((ATT_BLK, width), lambda j: (jnp.minimum(j + d, nb - 1), col))

    return _call(
        body, rider, name="bwd_attn", grid=(nb,),
        in_specs=[pl.BlockSpec((ATT_BLK, ATTN_W), lambda j: (j, 1)),
                  pl.BlockSpec((ATT_BLK, ATTN_W), lambda j: (j, 2)),
                  fwd_spec(0, ATTN_W, 0), fwd_spec(1, ATTN_W, 0), fwd_spec(2, ATTN_W, 0),
                  fwd_spec(0, ATTN_W, 0), fwd_spec(1, ATTN_W, 0), fwd_spec(2, ATTN_W, 0),
                  fwd_spec(0, ATTN_W, 0), fwd_spec(1, ATTN_W, 0), fwd_spec(2, ATTN_W, 0),
                  fwd_spec(0, LANES, 0), fwd_spec(1, LANES, 0), fwd_spec(2, LANES, 0),
                  pl.BlockSpec((N_ATT_TILES * N_HEADS, ATT_BLK, ATT_BLK), lambda j: (0, 0, 0))],
        out_specs=[pl.BlockSpec((ATT_BLK, IN_COLS), lambda j: (j, 0)),
                   pl.BlockSpec((N_ATT_TILES * N_HEADS, ATT_BLK, ATT_BLK), lambda j: (0, 0, 0))],
        out_shape=[jax.ShapeDtypeStruct((T, IN_COLS), BF16),
                   jax.ShapeDtypeStruct((N_ATT_TILES * N_HEADS, ATT_BLK, ATT_BLK), F32)],
        scratch_shapes=[pltpu.VMEM((ATT_BLK, ATTN_W), F32), pltpu.VMEM((ATT_BLK, ATTN_W), F32)],
        compiler_params=_params(56, 1),
    )(qkv, qkv, qkv, qkv, qkv, dao, dao, dao, ao, ao, ao, lse, lse, lse, bias)


def _bwd_conv(dproj, a, dco, hc, cw, lg, lb, rider=None):
    T = a.shape[0]
    tm = 512
    rc = 32
    ni = T // tm
    hb = tm // CONV_HALO

    def body(dp_in, a_ref, ap_ref, dco_ref, dcon_ref, hc_ref, hcn_ref, w_ref, lg_ref, lb_ref,
             dp_ref, dw_ref, db_ref, dlg_ref, dlb_ref, hext, dext, hsh, dsh, dwacc):
        del dp_in
        i = pl.program_id(0)

        def ln_bwd(dco_v, hc_v):
            mu = jnp.mean(hc_v, axis=-1, keepdims=True)
            xc = hc_v - mu
            rstd = lax.rsqrt(jnp.mean(xc * xc, axis=-1, keepdims=True) + EPS)
            xh = xc * rstd
            z = xh * lg_ref[...] + lb_ref[...]
            sg = _sigmoid(z)
            dz = dco_v * (sg * (1.0 + z * (1.0 - sg)))
            dxh = dz * lg_ref[...]
            dhc = rstd * (dxh - jnp.mean(dxh, axis=-1, keepdims=True)
                          - xh * jnp.mean(dxh * xh, axis=-1, keepdims=True))
            return dhc, dz * xh, dz

        hext[0:CONV_HALO, :] = jnp.where(i > 0, ap_ref[:, :CONV_W] * _sigmoid(ap_ref[:, CONV_W:]), 0.0)
        hext[CONV_HALO:CONV_HALO + tm, :] = a_ref[:, :CONV_W] * _sigmoid(a_ref[:, CONV_W:])
        dhc, dlg_rows, dlb_rows = ln_bwd(dco_ref[...], hc_ref[...])
        dext[0:tm, :] = dhc
        dhc_next, _, _ = ln_bwd(dcon_ref[...], hcn_ref[...])
        dext[tm:tm + CONV_HALO, :] = jnp.where(i < ni - 1, dhc_next, 0.0)

        @pl.when(i == 0)
        def _():
            dw_ref[...] = jnp.zeros(dw_ref.shape, F32)
            db_ref[...] = jnp.zeros(db_ref.shape, F32)
            dlg_ref[...] = jnp.zeros(dlg_ref.shape, F32)
            dlb_ref[...] = jnp.zeros(dlb_ref.shape, F32)

            dwacc[...] = jnp.zeros(dwacc.shape, F32)

        db_ref[...] += jnp.sum(dhc, axis=0, keepdims=True)
        dlg_ref[...] += jnp.sum(dlg_rows, axis=0, keepdims=True)
        dlb_ref[...] += jnp.sum(dlb_rows, axis=0, keepdims=True)
        _fill_shifted(hext, hsh, tm)
        _fill_shifted(dext, dsh, tm)
        for c in range(tm // rc):
            r0 = c * rc
            dh = jnp.zeros((rc, CONV_W), F32)
            dhc_c = dext[r0:r0 + rc, :]
            for k in range(CONV_K):
                dh = dh + w_ref[k:k + 1, :] * _shifted_rows(dext, dsh, r0 + 30 - k, rc)
                prod = dhc_c * _shifted_rows(hext, hsh, r0 + 2 + k, rc)
                dwacc[k] += jnp.sum(prod.reshape(rc // SUBLANES, SUBLANES, CONV_W), axis=0)
            av = a_ref[r0:r0 + rc, :CONV_W]
            sg = _sigmoid(a_ref[r0:r0 + rc, CONV_W:])
            dp_ref[r0:r0 + rc, 0:CONV_W] = (dh * sg).astype(BF16)
            dp_ref[r0:r0 + rc, CONV_W:] = (dh * av * sg * (1.0 - sg)).astype(BF16)

        @pl.when(i == ni - 1)
        def _():
            dw_ref[...] = jnp.sum(dwacc[...], axis=1)

    row = lambda w: pl.BlockSpec((tm, w), lambda i: (i, 0))
    prev = lambda w: pl.BlockSpec((CONV_HALO, w), lambda i: (jnp.maximum(i * hb - 1, 0), 0))
    nxt = lambda w: pl.BlockSpec((CONV_HALO, w), lambda i: (jnp.minimum((i + 1) * hb, ni * hb - 1), 0))
    vec = pl.BlockSpec((1, CONV_W), lambda i: (0, 0))
    return _call(
        body, rider, name="bwd_conv", grid=(ni,),
        in_specs=[ANY, row(1024), prev(1024), row(CONV_W), nxt(CONV_W), row(CONV_W), nxt(CONV_W),
                  pl.BlockSpec((CONV_HALO, CONV_W), lambda i: (0, 0)), vec, vec],
        out_specs=[pl.BlockSpec((tm, 1024), lambda i: (i, 0)),
                   pl.BlockSpec((CONV_HALO, CONV_W), lambda i: (0, 0)), vec, vec, vec],
        out_shape=[jax.ShapeDtypeStruct((T, IN_COLS), BF16),
                   jax.ShapeDtypeStruct((CONV_HALO, CONV_W), F32),
                   jax.ShapeDtypeStruct((1, CONV_W), F32),
                   jax.ShapeDtypeStruct((1, CONV_W), F32),
                   jax.ShapeDtypeStruct((1, CONV_W), F32)],
        scratch_shapes=[pltpu.VMEM((tm + CONV_HALO, CONV_W), F32), pltpu.VMEM((tm + CONV_HALO, CONV_W), F32),
                        pltpu.VMEM((SUBLANES - 1, tm + CONV_HALO - SUBLANES, CONV_W), F32),
                        pltpu.VMEM((SUBLANES - 1, tm + CONV_HALO - SUBLANES, CONV_W), F32),
                        pltpu.VMEM((CONV_HALO, SUBLANES, CONV_W), F32)],
        input_output_aliases={0: 0},
        compiler_params=_params(56, 1),
    )(dproj, a, a, dco, dco, hc, hc, cw, lg, lb)


def _bwd_in_proj(dproj, w_in, x, dh1, g1, rider=None):
    T = x.shape[0]
    tm = 512

    def body(dp_ref, w_ref, x_ref, dh_ref, g_ref, gx_ref, dg_ref):
        i = pl.program_id(0)
        du = None
        for s in range(N_CHIPS):
            term = _dot_nt(dp_ref[:, IN_SHARD * s:IN_SHARD * (s + 1)], w_ref[s])
            du = term if du is None else du + term
        dx, dg_rows = _rms_bwd(du, x_ref[...], g_ref[...])
        gx_ref[...] = dh_ref[...] + dx
        dg = jnp.sum(dg_rows, axis=0, keepdims=True)

        @pl.when(i == 0)
        def _():
            dg_ref[...] = dg

        @pl.when(i > 0)
        def _():
            dg_ref[...] += dg

    row = lambda w: pl.BlockSpec((tm, w), lambda i: (i, 0))
    vec = pl.BlockSpec((1, D_MODEL), lambda i: (0, 0))
    return _call(
        body, rider, name="bwd_in_proj", grid=(T // tm,),
        in_specs=[row(IN_COLS), pl.BlockSpec((N_CHIPS, D_MODEL, IN_SHARD), lambda i: (0, 0, 0)),
                  row(D_MODEL), row(D_MODEL), vec],
        out_specs=[row(D_MODEL), vec],
        out_shape=[jax.ShapeDtypeStruct((T, D_MODEL), F32), jax.ShapeDtypeStruct((1, D_MODEL), F32)],
        compiler_params=_params(40, 1),
    )(dproj, w_in, x, dh1, g1)


def _wgrad(name, a_list, a_spec, b, b_spec, out_spec, out_shape, n_outer, T, tk, select=None, rider=None):
    def body(*refs):
        a_refs, b_ref, o_ref = refs[:len(a_list)], refs[len(a_list)], refs[len(a_list) + 1]
        kt = pl.program_id(1)

        @pl.when(kt == 0)
        def _():
            o_ref[...] = jnp.zeros(o_ref.shape, F32)

        bv = b_ref[...].reshape(b_ref.shape[-2:])
        if select is None:
            o_ref[...] += _dot_tn(a_refs[0][...].reshape(a_refs[0].shape[-2:]), bv).reshape(o_ref.shape)
        else:
            for n, a_ref in enumerate(a_refs):
                @pl.when(select(pl.program_id(0)) == n)
                def _():
                    o_ref[...] += _dot_tn(a_ref[...], bv).reshape(o_ref.shape)

    (res,), got = _call(
        body, rider, name=name, grid=(n_outer, T // tk),
        in_specs=[a_spec] * len(a_list) + [b_spec],
        out_specs=[out_spec], out_shape=[out_shape],
        compiler_params=_params(48, 2),
    )(*a_list, b)
    return (res, got) if rider is not None else res


def _mesh_pos():
    return lax.axis_index("x"), lax.axis_index("y"), lax.axis_index("c")


def _other_chips(x, y):
    return [((1 - x, y), 2 * (1 - x) + y), ((x, 1 - y), 2 * x + (1 - y)), ((1 - x, 1 - y), 2 * (1 - x) + (1 - y))]


def _exchange_rider(operands, out_shape, aliases, sem_shape, pairs):
    def start(ins, outs, sems):
        for send, _ in pairs(ins, outs, *sems):
            send.start()

    def finish(ins, outs, sems):
        for send, recv in pairs(ins, outs, *sems):
            send.wait_send()
            recv.wait_recv()

    sems = [pltpu.SemaphoreType.DMA(sem_shape), pltpu.SemaphoreType.DMA(sem_shape)]
    return _Rider(list(operands), list(out_shape), aliases, sems, start, finish)


def _remote(src, dst, send_sem, recv_sem, device):
    return pltpu.make_async_remote_copy(src_ref=src, dst_ref=dst, send_sem=send_sem, recv_sem=recv_sem,
                                        device_id=device, device_id_type=MESH)


def _fetch_rider(bufs):
    def pairs(ins, outs, send_sems, recv_sems):
        x, y, c = _mesh_pos()
        res = []
        for t, buf in enumerate(bufs):
            rows = pl.ds(c * (buf.shape[1] // 2), buf.shape[1] // 2)
            mine = outs[t].at[2 * x + y, rows]
            for k, (chip, s) in enumerate(_other_chips(x, y)):
                landed = outs[t].at[s, rows]
                res.append((_remote(mine, mine, send_sems.at[t, k], recv_sems.at[t, k], (*chip, c)),
                            _remote(landed, landed, send_sems.at[t, k], recv_sems.at[t, k], (*chip, c))))
        return res

    shapes = [jax.ShapeDtypeStruct(b.shape, b.dtype) for b in bufs]
    return _exchange_rider(bufs, shapes, {t: t for t in range(len(bufs))}, (len(bufs), 3), pairs)


def _forward_rider(bufs):
    def pairs(ins, outs, send_sems, recv_sems):
        x, y, c = _mesh_pos()
        res = []
        for t, buf in enumerate(bufs):
            half = buf.shape[1] // 2
            for k, (_, s) in enumerate(_other_chips(x, y)):
                landed = outs[t].at[s, pl.ds(c * half, half)]
                theirs = outs[t].at[s, pl.ds((1 - c) * half, half)]
                res.append((_remote(landed, landed, send_sems.at[t, k], recv_sems.at[t, k], (x, y, 1 - c)),
                            _remote(theirs, theirs, send_sems.at[t, k], recv_sems.at[t, k], (x, y, 1 - c))))
        return res

    shapes = [jax.ShapeDtypeStruct(b.shape, b.dtype) for b in bufs]
    return _exchange_rider(bufs, shapes, {t: t for t in range(len(bufs))}, (len(bufs), 3), pairs)


def _pair_exchange_rider(grads):
    def pairs(ins, outs, send_sems, recv_sems):
        x, y, c = _mesh_pos()
        res = []
        for t, g in enumerate(grads):
            half = g.shape[1] // 2
            cp = _remote(ins[t].at[:, pl.ds((1 - c) * half, half), :], outs[t], send_sems.at[t], recv_sems.at[t],
                         (x, y, 1 - c))
            res.append((cp, cp))
        return res

    shapes = [jax.ShapeDtypeStruct((N_CHIPS, g.shape[1] // 2, g.shape[2]), F32) for g in grads]
    return _exchange_rider(grads, shapes, {}, (len(grads),), pairs)


def _chip_exchange_rider(sums):
    def pairs(ins, outs, send_sems, recv_sems):
        x, y, c = _mesh_pos()
        res = []
        for t in range(len(sums)):
            for k, (chip, s) in enumerate(_other_chips(x, y)):
                cp = _remote(ins[t].at[s], outs[t].at[k], send_sems.at[t, k], recv_sems.at[t, k], (*chip, c))
                res.append((cp, cp))
        return res

    shapes = [jax.ShapeDtypeStruct((3,) + p.shape[1:], p.dtype) for p in sums]
    return _exchange_rider(sums, shapes, {}, (len(sums), 3), pairs)


def _pair_gather_rider(fulls):
    def pairs(ins, outs, send_sems, recv_sems):
        x, y, c = _mesh_pos()
        res = []
        for t, f in enumerate(fulls):
            half = f.shape[0] // 2
            mine = outs[t].at[pl.ds(c * half, half)]
            theirs = outs[t].at[pl.ds((1 - c) * half, half)]
            res.append((_remote(mine, mine, send_sems.at[t], recv_sems.at[t], (x, y, 1 - c)),
                        _remote(theirs, theirs, send_sems.at[t], recv_sems.at[t], (x, y, 1 - c))))
        return res

    shapes = [jax.ShapeDtypeStruct(f.shape, F32) for f in fulls]
    return _exchange_rider(fulls, shapes, {t: t for t in range(len(fulls))}, (len(fulls),), pairs)


def _alone(name, rider):
    return _call(lambda: None, rider, name=name)()[1]


def _all_reduce_small(pack):
    rows = pack.shape[0]

    def body(p_ref, o_ref, buf, send_sems, recv_sems):
        x, y, c = _mesh_pos()
        me = 4 * x + 2 * y + c
        buf[0] = p_ref[...]
        copies = []
        for k in range(1, 8):
            peer = (x ^ (k >> 2), y ^ ((k >> 1) & 1), c ^ (k & 1))
            cp = pltpu.make_async_remote_copy(
                src_ref=p_ref, dst_ref=buf.at[k], send_sem=send_sems.at[k - 1], recv_sem=recv_sems.at[k - 1],
                device_id=peer, device_id_type=MESH)
            cp.start()
            copies.append(cp)
        for cp in copies:
            cp.wait()
        total = buf[me]
        for dev in range(1, 8):
            total = total + buf[me ^ dev]
        o_ref[...] = total

    return pl.pallas_call(
        body, name="all_reduce_small",
        in_specs=[VMEM_FULL], out_specs=VMEM_FULL,
        out_shape=jax.ShapeDtypeStruct(pack.shape, F32),
        scratch_shapes=[pltpu.VMEM((8, rows, LANES), F32),
                        pltpu.SemaphoreType.DMA((7,)), pltpu.SemaphoreType.DMA((7,))],
    )(pack)


def _row_block(rows):
    if rows <= 512:
        return rows
    for rb in (256, 352):
        if rows % rb == 0:
            return rb
    raise ValueError(f"no row block for {rows} rows")


def _place(name, w, pos, dtype):
    R, C = w.shape
    rb = _row_block(R)

    def body(pos_ref, w_ref, o_ref):
        del pos_ref
        o_ref[0] = w_ref[...].astype(dtype)

    return pl.pallas_call(
        body, name=name,
        grid_spec=pltpu.PrefetchScalarGridSpec(
            num_scalar_prefetch=1, grid=(R // rb,),
            in_specs=[pl.BlockSpec((rb, C), lambda r, p: (r, 0))],
            out_specs=pl.BlockSpec((1, rb, C), lambda r, p: (p[0], r, 0))),
        out_shape=(pltpu.HBM if N_CHIPS * R * C * jnp.dtype(dtype).itemsize >= PIN_BYTES
                   else jax.ShapeDtypeStruct)((N_CHIPS, R, C), dtype),
        compiler_params=_params(32, 1),
    )(pos, w)


def _pair_sum(name, g, got, pos):
    S, R, C = g.shape
    half = R // 2
    rb = _row_block(half)
    nh = half // rb

    def body(pos_ref, a_ref, b_ref, o_ref):
        del pos_ref
        o_ref[...] = (a_ref[...] + b_ref[...]).astype(BF16)

    spec = pl.BlockSpec((1, rb, C), lambda s, r, p: (s, r, 0))
    return pl.pallas_call(
        body, name=name,
        grid_spec=pltpu.PrefetchScalarGridSpec(
            num_scalar_prefetch=1, grid=(S, nh),
            in_specs=[pl.BlockSpec((1, rb, C), lambda s, r, p: (s, p[1] * nh + r, 0)), spec],
            out_specs=spec),
        out_shape=jax.ShapeDtypeStruct((S, half, C), BF16), compiler_params=_params(32, 2),
    )(pos, g, got)


def _chip_sum(name, pairs, got, pos):
    _, half, C = pairs.shape
    rb = _row_block(half)
    nh = half // rb

    def body(pos_ref, a_ref, g_ref, o_ref):
        del pos_ref
        o_ref[...] = ((a_ref[0].astype(F32) + g_ref[0].astype(F32)) + g_ref[1].astype(F32)) + g_ref[2].astype(F32)

    return pl.pallas_call(
        body, name=name,
        grid_spec=pltpu.PrefetchScalarGridSpec(
            num_scalar_prefetch=1, grid=(nh,),
            in_specs=[pl.BlockSpec((1, rb, C), lambda r, p: (p[0], r, 0)),
                      pl.BlockSpec((3, rb, C), lambda r, p: (0, r, 0))],
            out_specs=pl.BlockSpec((rb, C), lambda r, p: (p[1] * nh + r, 0))),
        out_shape=jax.ShapeDtypeStruct((2 * half, C), F32), compiler_params=_params(32, 1),
    )(pos, pairs, got)


def _adamw(name, w, g, m, v):
    R, C = w.shape
    rb = _row_block(R)
    c1 = 1.0 - ADAM_B1 ** ADAM_STEP
    c2 = 1.0 - ADAM_B2 ** ADAM_STEP

    def body(w_ref, g_ref, m_ref, v_ref, d_ref, nm_ref, nv_ref):
        gv = g_ref[...]
        nm = ADAM_B1 * m_ref[...] + (1.0 - ADAM_B1) * gv
        nv = ADAM_B2 * v_ref[...] + (1.0 - ADAM_B2) * (gv * gv)
        nm_ref[...] = nm
        nv_ref[...] = nv
        d_ref[...] = -ADAM_LR * ((nm / c1) / (jnp.sqrt(nv / c2) + ADAM_EPS) + ADAM_WD * w_ref[...])

    spec = pl.BlockSpec((rb, C), lambda r: (r, 0))
    sds = jax.ShapeDtypeStruct(w.shape, F32)
    return pl.pallas_call(
        body, name=name, grid=(R // rb,), in_specs=[spec] * 4, out_specs=[spec] * 3,
        out_shape=[sds, sds, sds], compiler_params=_params(40, 1),
    )(w, g, m, v)


def _rel_index():
    m = np.arange(2 * ATT_BLK)
    off = np.where(m < ATT_BLK, m, m - 2 * ATT_BLK)
    rel = np.stack([ATT_BLK * d - off for d in range(N_ATT_TILES)])
    return np.clip(rel, -MAX_REL, MAX_REL) + MAX_REL


def _local_step(x, tgt, g1, w_in, cw, cb, lg, lb, bias, w_out, g2, g3, w_up, fw, fb, w_down, g4, pos=None):
    T = x.shape[0]
    dist = pos is not None
    idx = _rel_index()

    (u, a, qkv), got = _fwd_in_proj(x, g1, w_in, _fetch_rider([w_out, w_down]) if dist else None)
    if dist:
        w_out, w_down = got
    (co, hc), got = _fwd_conv(a, cw, cb, lg, lb, _merge_riders(_forward_rider([w_out, w_down]),
                                                               _fetch_rider([w_up])) if dist else None)
    if dist:
        w_out, w_down, w_up = got
    (ao, lse), got = _fwd_attn(qkv, bias, _forward_rider([w_up]) if dist else None)
    if dist:
        (w_up,) = got
        w_out, w_down = w_out.reshape(D_MODEL, D_MODEL), w_down.reshape(D_FF, D_MODEL)
    mixed, h1, u2 = _fwd_out_proj(co, ao, w_out, x, g2, g3)
    w_cat = _pair_up_weights(w_up)
    hf, pre, fp = _fwd_ffn(u2, w_cat, fw, fb, w_down)
    loss, dy, df, dg4 = _fwd_loss(fp, h1, tgt, g4)

    tk = 1024
    du2p, dhf, act, dfw_g, dfw_v = _bwd_ffn(df, hf, pre, w_cat, fw, w_down)
    gw_up = _wgrad(
        "wgrad_up", [u2], pl.BlockSpec((tk, D_MODEL), lambda s, k: (k, 0)),
        dhf, pl.BlockSpec((1, tk, FF_SHARD), lambda s, k: (s // 2, k, s % 2)),
        pl.BlockSpec((1, D_MODEL, FF_SHARD), lambda s, k: (s, 0, 0)),
        jax.ShapeDtypeStruct((N_CHIPS, D_MODEL, FF_SHARD), F32), N_CHIPS, T, tk)
    gw_down = _wgrad(
        "wgrad_down", [act], pl.BlockSpec((tk, FF_SHARD), lambda s, k: (k, s)),
        df, pl.BlockSpec((tk, D_MODEL), lambda s, k: (k, 0)),
        pl.BlockSpec((FF_SHARD, D_MODEL), lambda s, k: (s, 0)),
        jax.ShapeDtypeStruct((D_FF, D_MODEL), F32), 2, T, tk).reshape(N_CHIPS, D_FF // N_CHIPS, D_MODEL)
    (dh1, dmx, dco, dao, dg3, dg2), _ = _bwd_mid(du2p, dy, h1, mixed, g3, g2, w_out)
    gw_out = _wgrad(
        "wgrad_out", [co, ao], pl.BlockSpec((tk, CONV_W), lambda s, k: (k, 0)),
        dmx, pl.BlockSpec((tk, D_MODEL), lambda s, k: (k, 0)),
        pl.BlockSpec((CONV_W, D_MODEL), lambda s, k: (s, 0)),
        jax.ShapeDtypeStruct((D_MODEL, D_MODEL), F32), 2, T, tk,
        select=lambda s: s, rider=_pair_exchange_rider([gw_up, gw_down]) if dist else None)
    if dist:
        gw_out, got = gw_out
        p_up = _pair_sum("pair_sum_w_up", gw_up, got[0], pos)
        p_down = _pair_sum("pair_sum_w_down", gw_down, got[1], pos)
    gw_out = gw_out.reshape(N_CHIPS, D_MODEL // N_CHIPS, D_MODEL)
    (dproj, dsacc), got = _bwd_attn(
        qkv, ao, dao, lse, bias,
        _merge_riders(_chip_exchange_rider([p_up, p_down]), _pair_exchange_rider([gw_out])) if dist else None)
    if dist:
        gw_up = _chip_sum("chip_sum_w_up", p_up, got[0], pos)
        gw_down = _chip_sum("chip_sum_w_down", p_down, got[1], pos)
        p_out = _pair_sum("pair_sum_w_out", gw_out, got[2], pos)
    (dproj, dcw, dcb, dlg, dlb), got = _bwd_conv(
        dproj, a, dco, hc, cw, lg, lb,
        _merge_riders(_pair_gather_rider([gw_up, gw_down]), _chip_exchange_rider([p_out])) if dist else None)
    if dist:
        gw_up, gw_down = got[:2]
        gw_out = _chip_sum("chip_sum_w_out", p_out, got[2], pos)
    gw_in = _wgrad(
        "wgrad_in", [u], pl.BlockSpec((tk, D_MODEL), lambda s, k: (k, 0)),
        dproj, pl.BlockSpec((tk, IN_SHARD), lambda s, k: (k, s)),
        pl.BlockSpec((1, D_MODEL, IN_SHARD), lambda s, k: (s, 0, 0)),
        jax.ShapeDtypeStruct((N_CHIPS, D_MODEL, IN_SHARD), F32), N_CHIPS, T, tk)
    if dist:
        got = _alone("pair_exchange_w_in", _merge_riders(_pair_exchange_rider([gw_in]), _pair_gather_rider([gw_out])))
        p_in, gw_out = _pair_sum("pair_sum_w_in", gw_in, got[0], pos), got[1]
    (gx, dg1), _ = _bwd_in_proj(dproj, w_in, x, dh1, g1)
    (diag,), got = _diag_sums(dsacc, _chip_exchange_rider([p_in]) if dist else None)
    if dist:
        (gw_in,) = _alone("pair_gather_w_in", _pair_gather_rider([_chip_sum("chip_sum_w_in", p_in, got[0], pos)]))

    diag = diag.reshape(N_ATT_TILES, N_HEADS, 2 * ATT_BLK)
    onehot = np.zeros((N_ATT_TILES, 2 * ATT_BLK, 2 * MAX_REL + 1), np.float32)
    for d in range(N_ATT_TILES):
        onehot[d, np.arange(2 * ATT_BLK), idx[d]] = 1.0
    drel = jnp.einsum("dhm,dmr->hr", diag, jnp.asarray(onehot), precision=lax.Precision.HIGHEST)

    small = dict(norm_mix_pre=dg1, conv_dw_w=dcw[:CONV_K], conv_dw_b=dcb, conv_ln_g=dlg, conv_ln_b=dlb,
                 rel_bias=drel, norm_mix_post=dg2, norm_ffn_pre=dg3,
                 ffn_dw_w=jnp.concatenate([dfw_g[0, :3], dfw_g[1, :3], dfw_v[0, :3], dfw_v[1, :3]], axis=1),
                 ffn_dw_b=jnp.concatenate([dfw_g[0, 3:4], dfw_g[1, 3:4], dfw_v[0, 3:4], dfw_v[1, 3:4]], axis=1),
                 norm_ffn_post=dg4)
    return loss, gx, small, dict(w_in=gw_in, w_out=gw_out, w_up=gw_up, w_down=gw_down)


SMALL_ORDER = ["norm_mix_pre", "conv_dw_b", "conv_ln_g", "conv_ln_b", "rel_bias", "norm_mix_post",
               "norm_ffn_pre", "ffn_dw_b", "norm_ffn_post", "conv_dw_w", "ffn_dw_w"]


def _pack(parts):
    rows = []
    for p in parts:
        width = -(-p.shape[1] // LANES) * LANES
        rows.append(jnp.pad(p, ((0, 0), (0, width - p.shape[1]))).reshape(-1, LANES))
    packed = jnp.concatenate(rows, axis=0)
    pad = -packed.shape[0] % 8
    return jnp.pad(packed, ((0, pad), (0, 0)))


def _unpack(packed, shapes):
    out, r = [], 0
    for shp in shapes:
        width = -(-shp[1] // LANES) * LANES
        n = shp[0] * width // LANES
        out.append(packed[r:r + n].reshape(shp[0], width)[:, :shp[1]])
        r += n
    return out


WEIGHTS = ["norm_mix_pre", "w_in", "conv_dw_w", "conv_dw_b", "conv_ln_g", "conv_ln_b", "rel_bias", "w_out",
           "norm_mix_post", "norm_ffn_pre", "w_up", "ffn_dw_w", "ffn_dw_b", "w_down", "norm_ffn_post"]
BIG = ["w_in", "w_out", "w_up", "w_down"]


def kernel(x, norm_mix_pre, w_in, conv_dw_w, conv_dw_b, conv_ln_g, conv_ln_b, rel_bias, w_out, norm_mix_post, norm_ffn_pre, w_up, ffn_dw_w, ffn_dw_b, w_down, norm_ffn_post, loss_target, m_norm_mix_pre, m_w_in, m_conv_dw_w, m_conv_dw_b, m_conv_ln_g, m_conv_ln_b, m_rel_bias, m_w_out, m_norm_mix_post, m_norm_ffn_pre, m_w_up, m_ffn_dw_w, m_ffn_dw_b, m_w_down, m_norm_ffn_post, v_norm_mix_pre, v_w_in, v_conv_dw_w, v_conv_dw_b, v_conv_ln_g, v_conv_ln_b, v_rel_bias, v_w_out, v_norm_mix_post, v_norm_ffn_pre, v_w_up, v_ffn_dw_w, v_ffn_dw_b, v_w_down, v_norm_ffn_post):
    args = locals()
    w = {n: args[n][0] for n in WEIGHTS}
    m = {n: args["m_" + n][0] for n in WEIGHTS}
    v = {n: args["v_" + n][0] for n in WEIGHTS}
    for d in (w, m, v):
        d["rel_bias"] = d["rel_bias"].reshape(N_HEADS, 2 * MAX_REL + 1)
        for n in ("norm_mix_pre", "conv_dw_b", "conv_ln_g", "conv_ln_b", "norm_mix_post", "norm_ffn_pre",
                  "ffn_dw_b", "norm_ffn_post"):
            d[n] = d[n].reshape(1, -1)
    shard = 2 * lax.axis_index("x") + lax.axis_index("y")

    cw_sh = jnp.pad(w["conv_dw_w"], ((0, CONV_HALO - CONV_K), (0, 0)))
    fw_sh = jnp.pad(w["ffn_dw_w"], ((0, FF_HALO - 3), (0, 0)))
    pos = jnp.stack([shard, lax.axis_index("c")]).astype(jnp.int32)
    bufs = {n: _place("place_" + n, w[n], pos, BF16) for n in BIG}
    first = [bufs["w_in"], _place("place_conv_dw_w", cw_sh, pos, F32), _place("place_ffn_dw_w", fw_sh, pos, F32)]
    (bias,), first = _bias_tiles(w["rel_bias"], _fetch_rider(first))
    w_in_f, cw_f, fw_f = _alone("all_gather_forward", _forward_rider(list(first)))
    cw_full = jnp.transpose(cw_f, (1, 0, 2)).reshape(CONV_HALO, CONV_W)

    loss, gx, small, big = _local_step(
        x[0], loss_target[0], w["norm_mix_pre"], w_in_f, cw_full, w["conv_dw_b"], w["conv_ln_g"],
        w["conv_ln_b"], bias, bufs["w_out"], w["norm_mix_post"],
        w["norm_ffn_pre"], bufs["w_up"], fw_f, w["ffn_dw_b"].reshape(N_CHIPS, 1, FF_SHARD),
        bufs["w_down"], w["norm_ffn_post"], pos)
    grads, deltas, new_m, new_v = {}, {}, {}, {}
    for n in BIG:
        grads[n] = big[n]
        deltas[n], new_m[n], new_v[n] = _adamw("adamw_" + n, w[n], big[n], m[n], v[n])

    gsum = _all_reduce_small(_pack([small[n] for n in SMALL_ORDER] + [loss]))
    shapes = [small[n].shape for n in SMALL_ORDER]
    *reduced, total = _unpack(gsum, shapes + [loss.shape])
    gs = dict(zip(SMALL_ORDER, reduced))
    gs["conv_dw_w"] = lax.dynamic_slice_in_dim(gs["conv_dw_w"], shard * LANES, LANES, axis=1)
    gs["ffn_dw_w"] = lax.dynamic_slice_in_dim(gs["ffn_dw_w"], shard * FF_SHARD, FF_SHARD, axis=1)
    shapes = [gs[n].shape for n in SMALL_ORDER]
    d_p, m_p, v_p = _adamw("adamw_small", _pack([w[n] for n in SMALL_ORDER]), _pack([gs[n] for n in SMALL_ORDER]),
                           _pack([m[n] for n in SMALL_ORDER]), _pack([v[n] for n in SMALL_ORDER]))
    for dst, packed in ((deltas, d_p), (new_m, m_p), (new_v, v_p)):
        dst.update(zip(SMALL_ORDER, _unpack(packed, shapes)))
    grads.update(gs)

    outs = [total[0, 0], gx[None]]
    for group in (grads, deltas, new_m, new_v):
        outs += [group[n].reshape(args[n].shape) for n in WEIGHTS]
    return tuple(outs)
```

```python
import functools
import math
from typing import Callable, NamedTuple

import numpy as np
import jax
import jax.numpy as jnp
from jax import lax
from jax.experimental import pallas as pl
from jax.experimental.pallas import tpu as pltpu

F32 = jnp.float32
BF16 = jnp.bfloat16

D_MODEL = 1024
CONV_W = 512
ATTN_W = 512
N_HEADS = 8
HEAD_DIM = 64
CHUNK = 64
N_LEFT = 8
MAX_REL = 128
CONV_K = 31
CONV_HALO = 32
D_FF = 2816
FF_SHARD = 1408
IN_COLS = 2560
IN_SHARD = 640
EPS = 1e-6
NEG_INF = -1e30
ATT_BLK = 256
N_ATT_TILES = 3
LANES = 128
SUBLANES = 8
N_CHIPS = 4

ADAM_LR = 0.001
ADAM_B1 = 0.9
ADAM_B2 = 0.999
ADAM_EPS = 1e-08
ADAM_WD = 0.01
ADAM_STEP = 10

MESH = pl.DeviceIdType.MESH
ANY = pl.BlockSpec(memory_space=pl.ANY)
VMEM_FULL = pl.BlockSpec(memory_space=pltpu.VMEM)


def _params(vmem_mb, n_grid=0):
    sem = ("arbitrary",) * n_grid if n_grid else None
    return pltpu.CompilerParams(dimension_semantics=sem, vmem_limit_bytes=vmem_mb << 20)


class _Rider(NamedTuple):
    operands: list
    out_shape: list
    aliases: dict
    sems: list
    start: Callable
    finish: Callable


def _merge_riders(a, b):
    ia, oa, sa = len(a.operands), len(a.out_shape), len(a.sems)

    def start(ins, outs, sems):
        a.start(ins[:ia], outs[:oa], sems[:sa])
        b.start(ins[ia:], outs[oa:], sems[sa:])

    def finish(ins, outs, sems):
        a.finish(ins[:ia], outs[:oa], sems[:sa])
        b.finish(ins[ia:], outs[oa:], sems[sa:])

    aliases = {**a.aliases, **{k + ia: v + oa for k, v in b.aliases.items()}}
    return _Rider(a.operands + b.operands, a.out_shape + b.out_shape, aliases, a.sems + b.sems, start, finish)


PIN_BYTES = 1 << 20


def _big(a):
    return math.prod(a.shape) * jnp.dtype(a.dtype).itemsize >= PIN_BYTES


def _pin_args(args):
    return [pltpu.with_memory_space_constraint(a, pltpu.HBM) if _big(a) else a for a in args]


def _call(body, rider, *, grid=(), in_specs=(), out_specs=(), out_shape=(), scratch_shapes=(),
          input_output_aliases=None, **kwargs):
    in_specs, out_specs = list(in_specs), list(out_specs)
    pin_out = lambda shapes: [pltpu.HBM(s.shape, s.dtype) if _big(s) else s for s in shapes]
    out_shape = pin_out(out_shape)
    scratch, aliases = list(scratch_shapes), dict(input_output_aliases or {})
    if rider is None:
        plain = pl.pallas_call(body, grid=grid, in_specs=in_specs, out_specs=out_specs, out_shape=out_shape,
                               scratch_shapes=scratch, input_output_aliases=aliases, **kwargs)
        return lambda *args: (plain(*_pin_args(args)), [])
    n_in, n_out, n_scr = len(in_specs), len(out_specs), len(scratch)
    r_in, r_out = len(rider.operands), len(rider.out_shape)

    def carried(*refs):
        ins, r_ins, refs = refs[:n_in], refs[n_in:n_in + r_in], refs[n_in + r_in:]
        outs, r_outs, refs = refs[:n_out], refs[n_out:n_out + r_out], refs[n_out + r_out:]
        scr, r_sems = refs[:n_scr], refs[n_scr:]
        if not grid:
            rider.start(r_ins, r_outs, r_sems)
            body(*ins, *outs, *scr)
            rider.finish(r_ins, r_outs, r_sems)
            return
        at = [pl.program_id(d) for d in range(len(grid))]
        first = functools.reduce(jnp.logical_and, [p == 0 for p in at])
        last = functools.reduce(jnp.logical_and, [p == n - 1 for p, n in zip(at, grid)])

        @pl.when(first)
        def _():
            rider.start(r_ins, r_outs, r_sems)

        body(*ins, *outs, *scr)

        @pl.when(last)
        def _():
            rider.finish(r_ins, r_outs, r_sems)

    aliases.update({n_in + k: n_out + v for k, v in rider.aliases.items()})
    both = pl.pallas_call(carried, grid=grid, in_specs=in_specs + [ANY] * r_in, out_specs=out_specs + [ANY] * r_out,
                          out_shape=out_shape + pin_out(rider.out_shape), scratch_shapes=scratch + rider.sems,
                          input_output_aliases=aliases, **kwargs)

    def run(*args):
        res = both(*_pin_args(args), *rider.operands)
        return res[:n_out], res[n_out:]

    return run


def _sigmoid(v):
    return 1.0 / (1.0 + jnp.exp(-v))


def _dot(a, b):
    return jnp.dot(a, b, preferred_element_type=F32)


def _dot_nt(a, b):
    return lax.dot_general(a, b, (((1,), (1,)), ((), ())), preferred_element_type=F32)


def _dot_tn(a, b):
    return lax.dot_general(a, b, (((0,), (0,)), ((), ())), preferred_element_type=F32)


def _rms_fwd(v, g):
    r = lax.rsqrt(jnp.mean(v * v, axis=-1, keepdims=True) + EPS)
    return v * r * g, r


def _rms_bwd(dy, v, g):
    r = lax.rsqrt(jnp.mean(v * v, axis=-1, keepdims=True) + EPS)
    vh = v * r
    dvh = dy * g
    dv = r * (dvh - vh * jnp.mean(dvh * vh, axis=-1, keepdims=True))
    return dv, dy * vh


def _join_columns(w):
    S, R, C = w.shape
    rb = 256

    def body(w_ref, o_ref):
        for s in range(S):
            o_ref[:, s * C:(s + 1) * C] = w_ref[s]

    return pl.pallas_call(
        body, name="join_columns", grid=(R // rb,),
        in_specs=[pl.BlockSpec((S, rb, C), lambda r: (0, r, 0))],
        out_specs=pl.BlockSpec((rb, S * C), lambda r: (r, 0)),
        out_shape=jax.ShapeDtypeStruct((R, S * C), w.dtype),
        compiler_params=_params(32, 1),
    )(w)


GLU_COLS = 2 * CONV_W


def _fwd_in_proj(x, g1, w_in, rider=None):
    T = x.shape[0]
    tm = 512

    def body(x_ref, g_ref, w_ref, u_ref, a_ref, qkv_ref):
        u, _ = _rms_fwd(x_ref[...], g_ref[...])
        u = u.astype(BF16)
        u_ref[...] = u
        a_ref[...] = _dot(u, w_ref[:, :GLU_COLS])
        qkv_ref[...] = _dot(u, w_ref[:, GLU_COLS:]).astype(BF16)

    return _call(
        body, rider, name="fwd_in_proj", grid=(T // tm,),
        in_specs=[pl.BlockSpec((tm, D_MODEL), lambda i: (i, 0)),
                  pl.BlockSpec((1, D_MODEL), lambda i: (0, 0)),
                  pl.BlockSpec((D_MODEL, IN_COLS), lambda i: (0, 0))],
        out_specs=[pl.BlockSpec((tm, D_MODEL), lambda i: (i, 0)),
                   pl.BlockSpec((tm, 1024), lambda i: (i, 0)),
                   pl.BlockSpec((tm, 1536), lambda i: (i, 0))],
        out_shape=[jax.ShapeDtypeStruct((T, D_MODEL), BF16),
                   jax.ShapeDtypeStruct((T, 1024), F32),
                   jax.ShapeDtypeStruct((T, 1536), BF16)],
        compiler_params=_params(40, 1),
    )(x, g1, w_in)


def _fill_shifted(ext, shifted, tm):
    n = tm + CONV_HALO - SUBLANES
    for j in range(1, SUBLANES):
        shifted[j - 1] = ext[j:j + n, :]


def _shifted_rows(ext, shifted, start, rows):
    j = start % SUBLANES
    if j == 0:
        return ext[start:start + rows, :]
    return shifted[j - 1, start - j:start - j + rows, :]


def _fwd_conv(a, cw, cb, lg, lb, rider=None):
    T = a.shape[0]
    tm = 512
    rc = 64

    def body(a_ref, w_ref, b_ref, lg_ref, lb_ref, co_ref, hc_ref, hext, hsh):
        i = pl.program_id(0)

        @pl.when(i == 0)
        def _():
            hext[0:CONV_HALO, :] = jnp.zeros((CONV_HALO, CONV_W), F32)

        @pl.when(i > 0)
        def _():
            hext[0:CONV_HALO, :] = hext[tm:tm + CONV_HALO, :]

        hext[CONV_HALO:CONV_HALO + tm, :] = a_ref[:, :CONV_W] * _sigmoid(a_ref[:, CONV_W:])
        _fill_shifted(hext, hsh, tm)
        for c in range(tm // rc):
            acc = jnp.zeros((rc, CONV_W), F32)
            for k in range(CONV_K):
                acc = acc + w_ref[k:k + 1, :] * _shifted_rows(hext, hsh, c * rc + 2 + k, rc)
            hc = acc + b_ref[...]
            hc_ref[c * rc:(c + 1) * rc, :] = hc
            mu = jnp.mean(hc, axis=-1, keepdims=True)
            xc = hc - mu
            var = jnp.mean(xc * xc, axis=-1, keepdims=True)
            z = xc * lax.rsqrt(var + EPS) * lg_ref[...] + lb_ref[...]
            co_ref[c * rc:(c + 1) * rc, :] = (z * _sigmoid(z)).astype(BF16)

    return _call(
        body, rider, name="fwd_conv", grid=(T // tm,),
        in_specs=[pl.BlockSpec((tm, 1024), lambda i: (i, 0)),
                  pl.BlockSpec((CONV_HALO, CONV_W), lambda i: (0, 0)),
                  pl.BlockSpec((1, CONV_W), lambda i: (0, 0)),
                  pl.BlockSpec((1, CONV_W), lambda i: (0, 0)),
                  pl.BlockSpec((1, CONV_W), lambda i: (0, 0))],
        out_specs=[pl.BlockSpec((tm, CONV_W), lambda i: (i, 0)),
                   pl.BlockSpec((tm, CONV_W), lambda i: (i, 0))],
        out_shape=[jax.ShapeDtypeStruct((T, CONV_W), BF16),
                   jax.ShapeDtypeStruct((T, CONV_W), F32)],
        scratch_shapes=[pltpu.VMEM((tm + CONV_HALO, CONV_W), F32),
                        pltpu.VMEM((SUBLANES - 1, tm + CONV_HALO - SUBLANES, CONV_W), F32)],
        compiler_params=_params(40, 1),
    )(a, cw, cb, lg, lb)


def _row_skew(v, sign):
    rows, width = v.shape
    row = lax.broadcasted_iota(jnp.int32, (rows, 1), 0)
    for b in range(int(math.log2(rows))):
        shift = (1 << b) if sign > 0 else width - (1 << b)
        v = jnp.where(((row >> b) & 1) == 1, pltpu.roll(v, shift, 1), v)
    return v


def _att_visible(d):
    rq = lax.broadcasted_iota(jnp.int32, (ATT_BLK, ATT_BLK), 0) // CHUNK
    ck = lax.broadcasted_iota(jnp.int32, (ATT_BLK, ATT_BLK), 1) // CHUNK
    slack = ATT_BLK
    above = jnp.where(d == 0, 0, slack)
    below = jnp.where(d == 2, 0, slack)
    return (ck <= rq + above) & (ck >= rq - below)


def _bias_tiles(rel, rider=None):
    vec = jnp.transpose(rel[:, _rel_index()], (1, 0, 2)).reshape(N_ATT_TILES * N_HEADS, 1, 2 * ATT_BLK)

    def body(v_ref, o_ref):
        visible = _att_visible(pl.program_id(0))
        for h in range(N_HEADS):
            full = _row_skew(jnp.broadcast_to(v_ref[h], (ATT_BLK, 2 * ATT_BLK)), 1)
            o_ref[h] = jnp.where(visible, full[:, :ATT_BLK], NEG_INF)

    return _call(
        body, rider, name="bias_tiles", grid=(N_ATT_TILES,),
        in_specs=[pl.BlockSpec((N_HEADS, 1, 2 * ATT_BLK), lambda d: (d, 0, 0))],
        out_specs=[pl.BlockSpec((N_HEADS, ATT_BLK, ATT_BLK), lambda d: (d, 0, 0))],
        out_shape=[jax.ShapeDtypeStruct((N_ATT_TILES * N_HEADS, ATT_BLK, ATT_BLK), F32)],
        compiler_params=_params(32, 1),
    )(vec)


def _diag_sums(ds, rider=None):
    def body(d_ref, o_ref):
        wide = jnp.concatenate([d_ref[0], jnp.zeros((ATT_BLK, ATT_BLK), F32)], axis=1)
        o_ref[0] = jnp.sum(_row_skew(wide, -1), axis=0, keepdims=True)

    return _call(
        body, rider, name="diag_sums", grid=(N_ATT_TILES * N_HEADS,),
        in_specs=[pl.BlockSpec((1, ATT_BLK, ATT_BLK), lambda n: (n, 0, 0))],
        out_specs=[pl.BlockSpec((1, 1, 2 * ATT_BLK), lambda n: (n, 0, 0))],
        out_shape=[jax.ShapeDtypeStruct((N_ATT_TILES * N_HEADS, 1, 2 * ATT_BLK), F32)],
        compiler_params=_params(16, 1),
    )(ds)


def _head_mask(h):
    lane = lax.broadcasted_iota(jnp.int32, (1, LANES), 1)
    return (lane // HEAD_DIM) == (h % 2)


def _fwd_attn(qkv, bias, rider=None):
    T = qkv.shape[0]
    nb = T // ATT_BLK
    scale = HEAD_DIM ** -0.5

    def body(q_ref, k0_ref, k1_ref, k2_ref, v0_ref, v1_ref, v2_ref, b_ref, o_ref, lse_ref):
        i = pl.program_id(0)

        @pl.when(i >= N_ATT_TILES - 1)
        def _():
            block(i, False, q_ref, k0_ref, k1_ref, k2_ref, v0_ref, v1_ref, v2_ref, b_ref, o_ref, lse_ref)

        @pl.when(i < N_ATT_TILES - 1)
        def _():
            block(i, True, q_ref, k0_ref, k1_ref, k2_ref, v0_ref, v1_ref, v2_ref, b_ref, o_ref, lse_ref)

    def block(i, hide_absent, q_ref, k0_ref, k1_ref, k2_ref, v0_ref, v1_ref, v2_ref, b_ref, o_ref, lse_ref):
        k_refs = (k0_ref, k1_ref, k2_ref)
        v_refs = (v0_ref, v1_ref, v2_ref)
        lane = lax.broadcasted_iota(jnp.int32, (1, LANES), 1)
        lse_tile = jnp.zeros((ATT_BLK, LANES), F32)
        for g in range(N_HEADS // 2):
            cols = slice(g * LANES, (g + 1) * LANES)
            qg = q_ref[:, cols] * scale
            og = jnp.zeros((ATT_BLK, LANES), F32)
            for h in (2 * g, 2 * g + 1):
                hm = _head_mask(h)
                qh = jnp.where(hm, qg, jnp.zeros_like(qg))
                s = []
                for d in range(N_ATT_TILES):
                    sd = _dot_nt(qh, k_refs[d][:, cols]) + b_ref[d * N_HEADS + h]
                    if d > 0 and hide_absent:
                        sd = jnp.where(i >= d, sd, NEG_INF)
                    s.append(sd)
                m = jnp.maximum(jnp.maximum(jnp.max(s[0], axis=-1, keepdims=True),
                                            jnp.max(s[1], axis=-1, keepdims=True)),
                                jnp.max(s[2], axis=-1, keepdims=True))
                p = [jnp.exp(sd - m) for sd in s]
                l = (jnp.sum(p[0], axis=-1, keepdims=True) + jnp.sum(p[1], axis=-1, keepdims=True)
                     + jnp.sum(p[2], axis=-1, keepdims=True))
                oh = jnp.zeros((ATT_BLK, LANES), F32)
                for d in range(N_ATT_TILES):
                    vg = v_refs[d][:, cols]
                    oh = oh + _dot(p[d].astype(BF16), jnp.where(hm, vg, jnp.zeros_like(vg)))
                og = og + oh / l
                lse_tile = jnp.where(lane == h, m + jnp.log(l), lse_tile)
            o_ref[:, cols] = og.astype(BF16)
        lse_ref[...] = lse_tile

    def kv_spec(d, col):
        return pl.BlockSpec((ATT_BLK, ATTN_W), lambda i: (jnp.maximum(i - d, 0), col))

    return _call(
        body, rider, name="fwd_attn", grid=(nb,),
        in_specs=[pl.BlockSpec((ATT_BLK, ATTN_W), lambda i: (i, 0)),
                  kv_spec(0, 1), kv_spec(1, 1), kv_spec(2, 1),
                  kv_spec(0, 2), kv_spec(1, 2), kv_spec(2, 2),
                  pl.BlockSpec((N_ATT_TILES * N_HEADS, ATT_BLK, ATT_BLK), lambda i: (0, 0, 0))],
        out_specs=[pl.BlockSpec((ATT_BLK, ATTN_W), lambda i: (i, 0)),
                   pl.BlockSpec((ATT_BLK, LANES), lambda i: (i, 0))],
        out_shape=[jax.ShapeDtypeStruct((T, ATTN_W), BF16),
                   jax.ShapeDtypeStruct((T, LANES), F32)],
        compiler_params=_params(40, 1),
    )(qkv, qkv, qkv, qkv, qkv, qkv, qkv, bias)


def _fwd_out_proj(co, ao, w_out, x, g2, g3):
    T = x.shape[0]
    tm = 512

    def body(co_ref, ao_ref, w_ref, x_ref, g2_ref, g3_ref, mixed_ref, h1_ref, u2_ref):
        mixed = _dot(co_ref[...], w_ref[0:CONV_W, :]) + _dot(ao_ref[...], w_ref[CONV_W:, :])
        mixed_ref[...] = mixed.astype(BF16)
        y, _ = _rms_fwd(mixed, g2_ref[...])
        h1 = x_ref[...] + y
        h1_ref[...] = h1
        u2, _ = _rms_fwd(h1, g3_ref[...])
        u2_ref[...] = u2.astype(BF16)

    row = lambda w: pl.BlockSpec((tm, w), lambda i: (i, 0))
    vec = pl.BlockSpec((1, D_MODEL), lambda i: (0, 0))
    return _call(
        body, None, name="fwd_out_proj", grid=(T // tm,),
        in_specs=[row(CONV_W), row(ATTN_W), pl.BlockSpec((D_MODEL, D_MODEL), lambda i: (0, 0)),
                  row(D_MODEL), vec, vec],
        out_specs=[row(D_MODEL), row(D_MODEL), row(D_MODEL)],
        out_shape=[jax.ShapeDtypeStruct((T, D_MODEL), BF16),
                   jax.ShapeDtypeStruct((T, D_MODEL), F32),
                   jax.ShapeDtypeStruct((T, D_MODEL), BF16)],
        compiler_params=_params(40, 1),
    )(co, ao, w_out, x, g2, g3)[0]


GELU_C = math.sqrt(2.0 / math.pi)
GELU_A = 0.044715


def _gelu_and_grad(v):
    sq = v * v
    th = jnp.tanh(v * (GELU_C + (GELU_C * GELU_A) * sq))
    half = 0.5 + 0.5 * th
    gl = v * half
    dgl = half + (v * (half * (1.0 - th))) * (GELU_C + (3.0 * GELU_C * GELU_A) * sq)
    return gl, dgl


FF_TM = 256
FF_HALO = 16
FF_CHUNKS = [(lo, min(lo + 256, FF_SHARD)) for lo in range(0, FF_SHARD, 256)]


def _rows_before(prev, cur):
    ext = jnp.concatenate([prev, cur], axis=0)
    return pltpu.roll(ext, 1, 0)[SUBLANES:], pltpu.roll(ext, 2, 0)[SUBLANES:]


def _rows_after(cur, nxt):
    ext = jnp.concatenate([cur, nxt], axis=0)
    n = ext.shape[0]
    return pltpu.roll(ext, n - 1, 0)[:cur.shape[0]], pltpu.roll(ext, n - 2, 0)[:cur.shape[0]]


def _pair_up_weights(w_up):
    rb = 256

    def body(g_ref, v_ref, o_ref):
        for lo, hi in FF_CHUNKS:
            o_ref[0, :, 2 * lo:lo + hi] = g_ref[0, :, lo:hi]
            o_ref[0, :, lo + hi:2 * hi] = v_ref[0, :, lo:hi]

    return pl.pallas_call(
        body, name="pair_up_weights", grid=(2, D_MODEL // rb),
        in_specs=[pl.BlockSpec((1, rb, FF_SHARD), lambda s, r: (s, r, 0)),
                  pl.BlockSpec((1, rb, FF_SHARD), lambda s, r: (s + 2, r, 0))],
        out_specs=pl.BlockSpec((1, rb, 2 * FF_SHARD), lambda s, r: (s, r, 0)),
        out_shape=jax.ShapeDtypeStruct((2, D_MODEL, 2 * FF_SHARD), w_up.dtype),
        compiler_params=_params(32, 2),
    )(w_up, w_up)


def _fwd_ffn(u2, w_cat, fw, fb, w_down):
    T = u2.shape[0]
    tm = FF_TM

    def body(u_ref, w_ref, fwg_ref, fwv_ref, fbg_ref, fbv_ref, wd_ref, hf_ref, pre_ref, act_ref, f_ref, carg, carv):
        i = pl.program_id(1)

        @pl.when(i == 0)
        def _():
            carg[...] = jnp.zeros(carg.shape, F32)
            carv[...] = jnp.zeros(carv.shape, F32)

        u = u_ref[...]
        f = None
        up = lambda lo, hi: _dot(u, w_ref[0, :, 2 * lo:2 * hi])
        ahead = up(*FF_CHUNKS[0])
        for c, (lo, hi) in enumerate(FF_CHUNKS):
            conv = []
            hs = (ahead[:, :hi - lo], ahead[:, hi - lo:])
            if c + 1 < len(FF_CHUNKS):
                ahead = up(*FF_CHUNKS[c + 1])
            for n, (car, fw_ref, fb_ref) in enumerate(((carg, fwg_ref, fbg_ref), (carv, fwv_ref, fbv_ref))):
                h0 = hs[n]
                hf_ref[n, :, lo:hi] = h0.astype(BF16)
                h1, h2 = _rows_before(car[:, lo:hi], h0)
                car[:, lo:hi] = h0[tm - SUBLANES:, :]
                conv.append(fw_ref[0, 0:1, lo:hi] * h2 + fw_ref[0, 1:2, lo:hi] * h1
                            + fw_ref[0, 2:3, lo:hi] * h0 + fb_ref[0, :, lo:hi])
            pre_ref[0, :, lo:hi] = conv[0].astype(BF16)
            pre_ref[1, :, lo:hi] = conv[1].astype(BF16)
            gl, _ = _gelu_and_grad(conv[0])
            act = (gl * conv[1]).astype(BF16)
            act_ref[:, lo:hi] = act
            term = _dot(act, wd_ref[lo:hi, :])
            f = term if f is None else f + term
        f_ref[0] = f.astype(BF16)

    fwspec = lambda off: pl.BlockSpec((1, FF_HALO, FF_SHARD), lambda s, i: (s + off, 0, 0))
    fbspec = lambda off: pl.BlockSpec((1, 1, FF_SHARD), lambda s, i: (s + off, 0, 0))
    return _call(
        body, None, name="fwd_ffn", grid=(2, T // tm),
        in_specs=[pl.BlockSpec((tm, D_MODEL), lambda s, i: (i, 0)),
                  pl.BlockSpec((1, D_MODEL, 2 * FF_SHARD), lambda s, i: (s, 0, 0)),
                  fwspec(0), fwspec(2), fbspec(0), fbspec(2),
                  pl.BlockSpec((FF_SHARD, D_MODEL), lambda s, i: (s, 0))],
        out_specs=[pl.BlockSpec((2, tm, FF_SHARD), lambda s, i: (0, i, s)),
                   pl.BlockSpec((2, tm, FF_SHARD), lambda s, i: (0, i, s)),
                   pl.BlockSpec((tm, FF_SHARD), lambda s, i: (i, s)),
                   pl.BlockSpec((1, tm, D_MODEL), lambda s, i: (s, i, 0))],
        out_shape=[jax.ShapeDtypeStruct((2, T, D_FF), BF16),
                   jax.ShapeDtypeStruct((2, T, D_FF), BF16),
                   jax.ShapeDtypeStruct((T, D_FF), BF16),
                   jax.ShapeDtypeStruct((2, T, D_MODEL), BF16)],
        scratch_shapes=[pltpu.VMEM((SUBLANES, FF_SHARD), F32), pltpu.VMEM((SUBLANES, FF_SHARD), F32)],
        compiler_params=_params(48, 2),
    )(u2, w_cat, fw, fw, fb, fb, w_down)[0]


def _fwd_loss(fp, h1, tgt, g4):
    T = h1.shape[0]
    tm = 512

    def body(fp_ref, h1_ref, t_ref, g_ref, loss_ref, dy_ref, df_ref, dg_ref):
        i = pl.program_id(0)
        f = fp_ref[0].astype(F32) + fp_ref[1].astype(F32)
        r, _ = _rms_fwd(f, g_ref[...])
        e = (h1_ref[...] + r) - t_ref[...]
        dy = e * (1.0 / D_MODEL)
        dy_ref[...] = dy
        df, dg_rows = _rms_bwd(dy, f, g_ref[...])
        df_ref[...] = df.astype(BF16)
        part = 0.5 * jnp.sum(jnp.mean(e * e, axis=-1, keepdims=True), axis=0, keepdims=True)
        dg = jnp.sum(dg_rows, axis=0, keepdims=True)

        @pl.when(i == 0)
        def _():
            loss_ref[...] = part
            dg_ref[...] = dg

        @pl.when(i > 0)
        def _():
            loss_ref[...] += part
            dg_ref[...] += dg

    row = pl.BlockSpec((tm, D_MODEL), lambda i: (i, 0))
    vec = pl.BlockSpec((1, D_MODEL), lambda i: (0, 0))
    return _call(
        body, None, name="fwd_loss", grid=(T // tm,),
        in_specs=[pl.BlockSpec((2, tm, D_MODEL), lambda i: (0, i, 0)), row, row, vec],
        out_specs=[pl.BlockSpec((1, 1), lambda i: (0, 0)), row, row, vec],
        out_shape=[jax.ShapeDtypeStruct((1, 1), F32),
                   jax.ShapeDtypeStruct((T, D_MODEL), F32),
                   jax.ShapeDtypeStruct((T, D_MODEL), BF16),
                   jax.ShapeDtypeStruct((1, D_MODEL), F32)],
        compiler_params=_params(40, 1),
    )(fp, h1, tgt, g4)[0]


def _bwd_ffn(df, hf, pre, w_cat, fw, w_down):
    T = df.shape[0]
    tm = FF_TM
    ni = T // tm

    def body(df_ref, hf_ref, pre_ref, wd_ref, w_ref, fwg_ref, fwv_ref,
             du_ref, dhf_ref, dwg_ref, dwv_ref, carg, carv):
        i = pl.program_id(1)

        @pl.when(i == 0)
        def _():
            dwg_ref[...] = jnp.zeros(dwg_ref.shape, F32)
            dwv_ref[...] = jnp.zeros(dwv_ref.shape, F32)
            carg[...] = jnp.zeros(carg.shape, F32)
            carv[...] = jnp.zeros(carv.shape, F32)

        df = df_ref[...]
        du = None
        down = lambda lo, hi: _dot_nt(df, wd_ref[lo:hi, :])
        ahead = down(*FF_CHUNKS[0])
        for c, (lo, hi) in enumerate(FF_CHUNKS):
            dact = ahead
            if c + 1 < len(FF_CHUNKS):
                ahead = down(*FF_CHUNKS[c + 1])
            pre_g = pre_ref[0, :, lo:hi].astype(F32)
            pre_v = pre_ref[1, :, lo:hi].astype(F32)
            gl, dgl = _gelu_and_grad(pre_g)
            dpre = (dact * pre_v * dgl, dact * gl)
            dhs = []
            for n, (car, fw_ref, dw_ref) in enumerate(((carg, fwg_ref, dwg_ref), (carv, fwv_ref, dwv_ref))):
                dp = dpre[n]
                h0 = hf_ref[n, :, lo:hi].astype(F32)
                up1, up2 = _rows_after(dp, car[:, lo:hi])
                car[:, lo:hi] = dp[0:SUBLANES, :]
                for k, shifted in enumerate((up2, up1, dp)):
                    dw_ref[0, k:k + 1, lo:hi] += jnp.sum(shifted * h0, axis=0, keepdims=True)
                dw_ref[0, 3:4, lo:hi] += jnp.sum(dp, axis=0, keepdims=True)
                dh = (fw_ref[0, 2:3, lo:hi] * dp + fw_ref[0, 1:2, lo:hi] * up1
                      + fw_ref[0, 0:1, lo:hi] * up2).astype(BF16)
                dhf_ref[n, :, lo:hi] = dh
                dhs.append(dh)
            term = _dot_nt(jnp.concatenate(dhs, axis=1), w_ref[0, :, 2 * lo:2 * hi])
            du = term if du is None else du + term
        du_ref[0] = du.astype(BF16)

    rev = lambda i: ni - 1 - i
    fwspec = lambda off: pl.BlockSpec((1, FF_HALO, FF_SHARD), lambda s, i: (s + off, 0, 0))
    dwspec = pl.BlockSpec((1, FF_HALO, FF_SHARD), lambda s, i: (s, 0, 0))
    return _call(
        body, None, name="bwd_ffn", grid=(2, ni),
        in_specs=[pl.BlockSpec((tm, D_MODEL), lambda s, i: (rev(i), 0)),
                  pl.BlockSpec((2, tm, FF_SHARD), lambda s, i: (0, rev(i), s)),
                  pl.BlockSpec((2, tm, FF_SHARD), lambda s, i: (0, rev(i), s)),
                  pl.BlockSpec((FF_SHARD, D_MODEL), lambda s, i: (s, 0)),
                  pl.BlockSpec((1, D_MODEL, 2 * FF_SHARD), lambda s, i: (s, 0, 0)),
                  fwspec(0), fwspec(2)],
        out_specs=[pl.BlockSpec((1, tm, D_MODEL), lambda s, i: (s, rev(i), 0)),
                   pl.BlockSpec((2, tm, FF_SHARD), lambda s, i: (0, rev(i), s)),
                   dwspec, dwspec],
        out_shape=[jax.ShapeDtypeStruct((2, T, D_MODEL), BF16),
                   jax.ShapeDtypeStruct((2, T, D_FF), BF16),
                   jax.ShapeDtypeStruct((2, FF_HALO, FF_SHARD), F32),
                   jax.ShapeDtypeStruct((2, FF_HALO, FF_SHARD), F32)],
        scratch_shapes=[pltpu.VMEM((SUBLANES, FF_SHARD), F32), pltpu.VMEM((SUBLANES, FF_SHARD), F32)],
        compiler_params=_params(56, 2),
    )(df, hf, pre, w_down, w_cat, fw, fw)[0]


def _bwd_mid(du2p, dy, h1, mixed, g3, g2, w_out, rider=None):
    T = dy.shape[0]
    tm = 512

    def body(du_ref, dy_ref, h1_ref, mx_ref, g3_ref, g2_ref, w_ref,
             dh1_ref, dmx_ref, dco_ref, dao_ref, dg3_ref, dg2_ref):
        i = pl.program_id(0)
        dres, dg3_rows = _rms_bwd(du_ref[0].astype(F32) + du_ref[1].astype(F32), h1_ref[...], g3_ref[...])
        dh1 = dy_ref[...] + dres
        dh1_ref[...] = dh1
        dmx, dg2_rows = _rms_bwd(dh1, mx_ref[...].astype(F32), g2_ref[...])
        dmx = dmx.astype(BF16)
        dmx_ref[...] = dmx
        dcat = _dot_nt(dmx, w_ref[...])
        dco_ref[...] = dcat[:, :CONV_W]
        dao_ref[...] = dcat[:, CONV_W:].astype(BF16)
        dg3 = jnp.sum(dg3_rows, axis=0, keepdims=True)
        dg2 = jnp.sum(dg2_rows, axis=0, keepdims=True)

        @pl.when(i == 0)
        def _():
            dg3_ref[...] = dg3
            dg2_ref[...] = dg2

        @pl.when(i > 0)
        def _():
            dg3_ref[...] += dg3
            dg2_ref[...] += dg2

    row = lambda w: pl.BlockSpec((tm, w), lambda i: (i, 0))
    vec = pl.BlockSpec((1, D_MODEL), lambda i: (0, 0))
    return _call(
        body, rider, name="bwd_mid", grid=(T // tm,),
        in_specs=[pl.BlockSpec((2, tm, D_MODEL), lambda i: (0, i, 0)), row(D_MODEL), row(D_MODEL),
                  row(D_MODEL), vec, vec, pl.BlockSpec((D_MODEL, D_MODEL), lambda i: (0, 0))],
        out_specs=[row(D_MODEL), row(D_MODEL), row(CONV_W), row(ATTN_W), vec, vec],
        out_shape=[jax.ShapeDtypeStruct((T, D_MODEL), F32),
                   jax.ShapeDtypeStruct((T, D_MODEL), BF16),
                   jax.ShapeDtypeStruct((T, CONV_W), F32),
                   jax.ShapeDtypeStruct((T, ATTN_W), BF16),
                   jax.ShapeDtypeStruct((1, D_MODEL), F32),
                   jax.ShapeDtypeStruct((1, D_MODEL), F32)],
        compiler_params=_params(48, 1),
    )(du2p, dy, h1, mixed, g3, g2, w_out)


def _bwd_attn(qkv, ao, dao, lse, bias, rider=None):
    T = qkv.shape[0]
    nb = T // ATT_BLK
    scale = HEAD_DIM ** -0.5

    def body(k_ref, v_ref, q0, q1, q2, do0, do1, do2, o0, o1, o2, l0, l1, l2, b_ref,
             dp_ref, ds_ref, acc1, acc2):
        j = pl.program_id(0)
        q_refs, do_refs, o_refs, l_refs = (q0, q1, q2), (do0, do1, do2), (o0, o1, o2), (l0, l1, l2)

        @pl.when(j == 0)
        def _():
            ds_ref[...] = jnp.zeros(ds_ref.shape, F32)
            acc1[...] = jnp.zeros(acc1.shape, F32)
            acc2[...] = jnp.zeros(acc2.shape, F32)

        dq_new = [[], [], []]
        dk_cols, dv_cols = [], []
        for g in range(N_HEADS // 2):
            cols = slice(g * LANES, (g + 1) * LANES)
            kg = k_ref[:, cols]
            vg = v_ref[:, cols]
            dkg = jnp.zeros((ATT_BLK, LANES), F32)
            dvg = jnp.zeros((ATT_BLK, LANES), F32)
            dqg = [jnp.zeros((ATT_BLK, LANES), F32) for _ in range(N_ATT_TILES)]
            for d in range(N_ATT_TILES):
                qg = q_refs[d][:, cols] * scale
                dog = do_refs[d][:, cols]
                if d > 0:
                    dog = jnp.where(j + d < nb, dog, jnp.zeros_like(dog))
                prod = dog.astype(F32) * o_refs[d][:, cols].astype(F32)
                for h in (2 * g, 2 * g + 1):
                    hm = _head_mask(h)
                    qh = jnp.where(hm, qg, jnp.zeros_like(qg))
                    doh = jnp.where(hm, dog, jnp.zeros_like(dog))
                    kh = jnp.where(hm, kg, jnp.zeros_like(kg))
                    delta = jnp.sum(jnp.where(hm, prod, 0.0), axis=-1, keepdims=True)
                    s = _dot_nt(qh, kg) + b_ref[d * N_HEADS + h]
                    p = jnp.exp(s - l_refs[d][:, h:h + 1])
                    dvg = dvg + _dot_tn(p.astype(BF16), doh)
                    dpm = _dot_nt(doh, vg)
                    dsc = p * (dpm - delta)
                    ds_ref[d * N_HEADS + h] += dsc
                    dsb = dsc.astype(BF16)
                    dqg[d] = dqg[d] + _dot(dsb, kh)
                    dkg = dkg + _dot_tn(dsb, qh)
            for d in range(N_ATT_TILES):
                dq_new[d].append(dqg[d])
            dk_cols.append(dkg)
            dv_cols.append(dvg)
        x0, x1, x2 = (jnp.concatenate(c, axis=1) * scale for c in dq_new)
        dp_ref[:, 0:1024] = jnp.zeros((ATT_BLK, 1024), BF16)
        dp_ref[:, 1024:1536] = (acc1[...] + x0).astype(BF16)
        dp_ref[:, 1536:2048] = jnp.concatenate(dk_cols, axis=1).astype(BF16)
        dp_ref[:, 2048:2560] = jnp.concatenate(dv_cols, axis=1).astype(BF16)
        acc1[...] = acc2[...] + x1
        acc2[...] = x2

    def fwd_spec(d, width, col):
        return pl.BlockSpec((ATT_BLK, width), lambda j: (jnp.minimum(j + d, nb - 1), col))

    return _call(
        body, rider, name="bwd_attn", grid=(nb,),
        in_specs=[pl.BlockSpec((ATT_BLK, ATTN_W), lambda j: (j, 1)),
                  pl.BlockSpec((ATT_BLK, ATTN_W), lambda j: (j, 2)),
                  fwd_spec(0, ATTN_W, 0), fwd_spec(1, ATTN_W, 0), fwd_spec(2, ATTN_W, 0),
                  fwd_spec(0, ATTN_W, 0), fwd_spec(1, ATTN_W, 0), fwd_spec(2, ATTN_W, 0),
                  fwd_spec(0, ATTN_W, 0), fwd_spec(1, ATTN_W, 0), fwd_spec(2, ATTN_W, 0),
                  fwd_spec(0, LANES, 0), fwd_spec(1, LANES, 0), fwd_spec(2, LANES, 0),
                  pl.BlockSpec((N_ATT_TILES * N_HEADS, ATT_BLK, ATT_BLK), lambda j: (0, 0, 0))],
        out_specs=[pl.BlockSpec((ATT_BLK, IN_COLS), lambda j: (j, 0)),
                   pl.BlockSpec((N_ATT_TILES * N_HEADS, ATT_BLK, ATT_BLK), lambda j: (0, 0, 0))],
        out_shape=[jax.ShapeDtypeStruct((T, IN_COLS), BF16),
                   jax.ShapeDtypeStruct((N_ATT_TILES * N_HEADS, ATT_BLK, ATT_BLK), F32)],
        scratch_shapes=[pltpu.VMEM((ATT_BLK, ATTN_W), F32), pltpu.VMEM((ATT_BLK, ATTN_W), F32)],
        compiler_params=_params(56, 1),
    )(qkv, qkv, qkv, qkv, qkv, dao, dao, dao, ao, ao, ao, lse, lse, lse, bias)


def _bwd_conv(dproj, a, dco, hc, cw, lg, lb, rider=None):
    T = a.shape[0]
    tm = 512
    rc = 32
    ni = T // tm
    hb = tm // CONV_HALO

    def body(dp_in, a_ref, ap_ref, dco_ref, dcon_ref, hc_ref, hcn_ref, w_ref, lg_ref, lb_ref,
             dp_ref, dw_ref, db_ref, dlg_ref, dlb_ref, hext, dext, hsh, dsh, dwacc):
        del dp_in
        i = pl.program_id(0)

        def ln_bwd(dco_v, hc_v):
            mu = jnp.mean(hc_v, axis=-1, keepdims=True)
            xc = hc_v - mu
            rstd = lax.rsqrt(jnp.mean(xc * xc, axis=-1, keepdims=True) + EPS)
            xh = xc * rstd
            z = xh * lg_ref[...] + lb_ref[...]
            sg = _sigmoid(z)
            dz = dco_v * (sg * (1.0 + z * (1.0 - sg)))
            dxh = dz * lg_ref[...]
            dhc = rstd * (dxh - jnp.mean(dxh, axis=-1, keepdims=True)
                          - xh * jnp.mean(dxh * xh, axis=-1, keepdims=True))
            return dhc, dz * xh, dz

        hext[0:CONV_HALO, :] = jnp.where(i > 0, ap_ref[:, :CONV_W] * _sigmoid(ap_ref[:, CONV_W:]), 0.0)
        hext[CONV_HALO:CONV_HALO + tm, :] = a_ref[:, :CONV_W] * _sigmoid(a_ref[:, CONV_W:])
        dhc, dlg_rows, dlb_rows = ln_bwd(dco_ref[...], hc_ref[...])
        dext[0:tm, :] = dhc
        dhc_next, _, _ = ln_bwd(dcon_ref[...], hcn_ref[...])
        dext[tm:tm + CONV_HALO, :] = jnp.where(i < ni - 1, dhc_next, 0.0)

        @pl.when(i == 0)
        def _():
            dw_ref[...] = jnp.zeros(dw_ref.shape, F32)
            db_ref[...] = jnp.zeros(db_ref.shape, F32)
            dlg_ref[...] = jnp.zeros(dlg_ref.shape, F32)
            dlb_ref[...] = jnp.zeros(dlb_ref.shape, F32)

            dwacc[...] = jnp.zeros(dwacc.shape, F32)

        db_ref[...] += jnp.sum(dhc, axis=0, keepdims=True)
        dlg_ref[...] += jnp.sum(dlg_rows, axis=0, keepdims=True)
        dlb_ref[...] += jnp.sum(dlb_rows, axis=0, keepdims=True)
        _fill_shifted(hext, hsh, tm)
        _fill_shifted(dext, dsh, tm)
        for c in range(tm // rc):
            r0 = c * rc
            dh = jnp.zeros((rc, CONV_W), F32)
            dhc_c = dext[r0:r0 + rc, :]
            for k in range(CONV_K):
                dh = dh + w_ref[k:k + 1, :] * _shifted_rows(dext, dsh, r0 + 30 - k, rc)
                prod = dhc_c * _shifted_rows(hext, hsh, r0 + 2 + k, rc)
                dwacc[k] += jnp.sum(prod.reshape(rc // SUBLANES, SUBLANES, CONV_W), axis=0)
            av = a_ref[r0:r0 + rc, :CONV_W]
            sg = _sigmoid(a_ref[r0:r0 + rc, CONV_W:])
            dp_ref[r0:r0 + rc, 0:CONV_W] = (dh * sg).astype(BF16)
            dp_ref[r0:r0 + rc, CONV_W:] = (dh * av * sg * (1.0 - sg)).astype(BF16)

        @pl.when(i == ni - 1)
        def _():
            dw_ref[...] = jnp.sum(dwacc[...], axis=1)

    row = lambda w: pl.BlockSpec((tm, w), lambda i: (i, 0))
    prev = lambda w: pl.BlockSpec((CONV_HALO, w), lambda i: (jnp.maximum(i * hb - 1, 0), 0))
    nxt = lambda w: pl.BlockSpec((CONV_HALO, w), lambda i: (jnp.minimum((i + 1) * hb, ni * hb - 1), 0))
    vec = pl.BlockSpec((1, CONV_W), lambda i: (0, 0))
    return _call(
        body, rider, name="bwd_conv", grid=(ni,),
        in_specs=[ANY, row(1024), prev(1024), row(CONV_W), nxt(CONV_W), row(CONV_W), nxt(CONV_W),
                  pl.BlockSpec((CONV_HALO, CONV_W), lambda i: (0, 0)), vec, vec],
        out_specs=[pl.BlockSpec((tm, 1024), lambda i: (i, 0)),
                   pl.BlockSpec((CONV_HALO, CONV_W), lambda i: (0, 0)), vec, vec, vec],
        out_shape=[jax.ShapeDtypeStruct((T, IN_COLS), BF16),
                   jax.ShapeDtypeStruct((CONV_HALO, CONV_W), F32),
                   jax.ShapeDtypeStruct((1, CONV_W), F32),
                   jax.ShapeDtypeStruct((1, CONV_W), F32),
                   jax.ShapeDtypeStruct((1, CONV_W), F32)],
        scratch_shapes=[pltpu.VMEM((tm + CONV_HALO, CONV_W), F32), pltpu.VMEM((tm + CONV_HALO, CONV_W), F32),
                        pltpu.VMEM((SUBLANES - 1, tm + CONV_HALO - SUBLANES, CONV_W), F32),
                        pltpu.VMEM((SUBLANES - 1, tm + CONV_HALO - SUBLANES, CONV_W), F32),
                        pltpu.VMEM((CONV_HALO, SUBLANES, CONV_W), F32)],
        input_output_aliases={0: 0},
        compiler_params=_params(56, 1),
    )(dproj, a, a, dco, dco, hc, hc, cw, lg, lb)


def _bwd_in_proj(dproj, w_in, x, dh1, g1, rider=None):
    T = x.shape[0]
    tm = 512

    def body(dp_ref, w_ref, x_ref, dh_ref, g_ref, gx_ref, dg_ref):
        i = pl.program_id(0)
        du = _dot_nt(dp_ref[...], w_ref[...])
        dx, dg_rows = _rms_bwd(du, x_ref[...], g_ref[...])
        gx_ref[...] = dh_ref[...] + dx
        dg = jnp.sum(dg_rows, axis=0, keepdims=True)

        @pl.when(i == 0)
        def _():
            dg_ref[...] = dg

        @pl.when(i > 0)
        def _():
            dg_ref[...] += dg

    row = lambda w: pl.BlockSpec((tm, w), lambda i: (i, 0))
    vec = pl.BlockSpec((1, D_MODEL), lambda i: (0, 0))
    return _call(
        body, rider, name="bwd_in_proj", grid=(T // tm,),
        in_specs=[row(IN_COLS), pl.BlockSpec((D_MODEL, IN_COLS), lambda i: (0, 0)),
                  row(D_MODEL), row(D_MODEL), vec],
        out_specs=[row(D_MODEL), vec],
        out_shape=[jax.ShapeDtypeStruct((T, D_MODEL), F32), jax.ShapeDtypeStruct((1, D_MODEL), F32)],
        compiler_params=_params(40, 1),
    )(dproj, w_in, x, dh1, g1)


def _wgrad(name, a_list, a_spec, b, b_spec, out_spec, out_shape, n_outer, T, tk, select=None, rider=None):
    def body(*refs):
        a_refs, b_ref, o_ref = refs[:len(a_list)], refs[len(a_list)], refs[len(a_list) + 1]
        kt = pl.program_id(1)

        @pl.when(kt == 0)
        def _():
            o_ref[...] = jnp.zeros(o_ref.shape, F32)

        bv = b_ref[...].reshape(b_ref.shape[-2:])
        if select is None:
            o_ref[...] += _dot_tn(a_refs[0][...].reshape(a_refs[0].shape[-2:]), bv).reshape(o_ref.shape)
        else:
            for n, a_ref in enumerate(a_refs):
                @pl.when(select(pl.program_id(0)) == n)
                def _():
                    o_ref[...] += _dot_tn(a_ref[...], bv).reshape(o_ref.shape)

    (res,), got = _call(
        body, rider, name=name, grid=(n_outer, T // tk),
        in_specs=[a_spec] * len(a_list) + [b_spec],
        out_specs=[out_spec], out_shape=[out_shape],
        compiler_params=_params(48, 2),
    )(*a_list, b)
    return (res, got) if rider is not None else res


def _mesh_pos():
    return lax.axis_index("x"), lax.axis_index("y"), lax.axis_index("c")


def _other_chips(x, y):
    return [((1 - x, y), 2 * (1 - x) + y), ((x, 1 - y), 2 * x + (1 - y)), ((1 - x, 1 - y), 2 * (1 - x) + (1 - y))]


def _exchange_rider(operands, out_shape, aliases, sem_shape, pairs):
    def start(ins, outs, sems):
        for send, _ in pairs(ins, outs, *sems):
            send.start()

    def finish(ins, outs, sems):
        for send, recv in pairs(ins, outs, *sems):
            send.wait_send()
            recv.wait_recv()

    sems = [pltpu.SemaphoreType.DMA(sem_shape), pltpu.SemaphoreType.DMA(sem_shape)]
    return _Rider(list(operands), list(out_shape), aliases, sems, start, finish)


def _remote(src, dst, send_sem, recv_sem, device):
    return pltpu.make_async_remote_copy(src_ref=src, dst_ref=dst, send_sem=send_sem, recv_sem=recv_sem,
                                        device_id=device, device_id_type=MESH)


def _fetch_rider(bufs):
    def pairs(ins, outs, send_sems, recv_sems):
        x, y, c = _mesh_pos()
        res = []
        for t, buf in enumerate(bufs):
            rows = pl.ds(c * (buf.shape[1] // 2), buf.shape[1] // 2)
            mine = outs[t].at[2 * x + y, rows]
            for k, (chip, s) in enumerate(_other_chips(x, y)):
                landed = outs[t].at[s, rows]
                res.append((_remote(mine, mine, send_sems.at[t, k], recv_sems.at[t, k], (*chip, c)),
                            _remote(landed, landed, send_sems.at[t, k], recv_sems.at[t, k], (*chip, c))))
        return res

    shapes = [jax.ShapeDtypeStruct(b.shape, b.dtype) for b in bufs]
    return _exchange_rider(bufs, shapes, {t: t for t in range(len(bufs))}, (len(bufs), 3), pairs)


def _forward_rider(bufs):
    def pairs(ins, outs, send_sems, recv_sems):
        x, y, c = _mesh_pos()
        res = []
        for t, buf in enumerate(bufs):
            half = buf.shape[1] // 2
            for k, (_, s) in enumerate(_other_chips(x, y)):
                landed = outs[t].at[s, pl.ds(c * half, half)]
                theirs = outs[t].at[s, pl.ds((1 - c) * half, half)]
                res.append((_remote(landed, landed, send_sems.at[t, k], recv_sems.at[t, k], (x, y, 1 - c)),
                            _remote(theirs, theirs, send_sems.at[t, k], recv_sems.at[t, k], (x, y, 1 - c))))
        return res

    shapes = [jax.ShapeDtypeStruct(b.shape, b.dtype) for b in bufs]
    return _exchange_rider(bufs, shapes, {t: t for t in range(len(bufs))}, (len(bufs), 3), pairs)


def _pair_exchange_rider(grads):
    def pairs(ins, outs, send_sems, recv_sems):
        x, y, c = _mesh_pos()
        res = []
        for t, g in enumerate(grads):
            half = g.shape[1] // 2
            cp = _remote(ins[t].at[:, pl.ds((1 - c) * half, half), :], outs[t], send_sems.at[t], recv_sems.at[t],
                         (x, y, 1 - c))
            res.append((cp, cp))
        return res

    shapes = [jax.ShapeDtypeStruct((N_CHIPS, g.shape[1] // 2, g.shape[2]), F32) for g in grads]
    return _exchange_rider(grads, shapes, {}, (len(grads),), pairs)


def _chip_exchange_rider(sums):
    def pairs(ins, outs, send_sems, recv_sems):
        x, y, c = _mesh_pos()
        res = []
        for t in range(len(sums)):
            for k, (chip, s) in enumerate(_other_chips(x, y)):
                cp = _remote(ins[t].at[s], outs[t].at[k], send_sems.at[t, k], recv_sems.at[t, k], (*chip, c))
                res.append((cp, cp))
        return res

    shapes = [jax.ShapeDtypeStruct((3,) + p.shape[1:], p.dtype) for p in sums]
    return _exchange_rider(sums, shapes, {}, (len(sums), 3), pairs)


def _pair_gather_rider(fulls):
    def pairs(ins, outs, send_sems, recv_sems):
        x, y, c = _mesh_pos()
        res = []
        for t, f in enumerate(fulls):
            half = f.shape[0] // 2
            mine = outs[t].at[pl.ds(c * half, half)]
            theirs = outs[t].at[pl.ds((1 - c) * half, half)]
            res.append((_remote(mine, mine, send_sems.at[t], recv_sems.at[t], (x, y, 1 - c)),
                        _remote(theirs, theirs, send_sems.at[t], recv_sems.at[t], (x, y, 1 - c))))
        return res

    shapes = [jax.ShapeDtypeStruct(f.shape, F32) for f in fulls]
    return _exchange_rider(fulls, shapes, {t: t for t in range(len(fulls))}, (len(fulls),), pairs)


def _alone(name, rider):
    return _call(lambda: None, rider, name=name)()[1]


def _all_reduce_small(pack):
    rows = pack.shape[0]

    def body(p_ref, o_ref, buf, send_sems, recv_sems):
        x, y, c = _mesh_pos()
        me = 4 * x + 2 * y + c
        buf[0] = p_ref[...]
        copies = []
        for k in range(1, 8):
            peer = (x ^ (k >> 2), y ^ ((k >> 1) & 1), c ^ (k & 1))
            cp = pltpu.make_async_remote_copy(
                src_ref=p_ref, dst_ref=buf.at[k], send_sem=send_sems.at[k - 1], recv_sem=recv_sems.at[k - 1],
                device_id=peer, device_id_type=MESH)
            cp.start()
            copies.append(cp)
        for cp in copies:
            cp.wait()
        total = buf[me]
        for dev in range(1, 8):
            total = total + buf[me ^ dev]
        o_ref[...] = total

    return pl.pallas_call(
        body, name="all_reduce_small",
        in_specs=[VMEM_FULL], out_specs=VMEM_FULL,
        out_shape=jax.ShapeDtypeStruct(pack.shape, F32),
        scratch_shapes=[pltpu.VMEM((8, rows, LANES), F32),
                        pltpu.SemaphoreType.DMA((7,)), pltpu.SemaphoreType.DMA((7,))],
    )(pack)


def _row_block(rows):
    if rows <= 512:
        return rows
    for rb in (256, 352):
        if rows % rb == 0:
            return rb
    raise ValueError(f"no row block for {rows} rows")


def _place(name, w, pos, dtype):
    R, C = w.shape
    rb = _row_block(R)

    def body(pos_ref, w_ref, o_ref):
        del pos_ref
        o_ref[0] = w_ref[...].astype(dtype)

    return pl.pallas_call(
        body, name=name,
        grid_spec=pltpu.PrefetchScalarGridSpec(
            num_scalar_prefetch=1, grid=(R // rb,),
            in_specs=[pl.BlockSpec((rb, C), lambda r, p: (r, 0))],
            out_specs=pl.BlockSpec((1, rb, C), lambda r, p: (p[0], r, 0))),
        out_shape=(pltpu.HBM if N_CHIPS * R * C * jnp.dtype(dtype).itemsize >= PIN_BYTES
                   else jax.ShapeDtypeStruct)((N_CHIPS, R, C), dtype),
        compiler_params=_params(32, 1),
    )(pos, w)


def _pair_sum(name, g, got, pos):
    S, R, C = g.shape
    half = R // 2
    rb = _row_block(half)
    nh = half // rb

    def body(pos_ref, a_ref, b_ref, o_ref):
        del pos_ref
        o_ref[...] = (a_ref[...] + b_ref[...]).astype(BF16)

    spec = pl.BlockSpec((1, rb, C), lambda s, r, p: (s, r, 0))
    return pl.pallas_call(
        body, name=name,
        grid_spec=pltpu.PrefetchScalarGridSpec(
            num_scalar_prefetch=1, grid=(S, nh),
            in_specs=[pl.BlockSpec((1, rb, C), lambda s, r, p: (s, p[1] * nh + r, 0)), spec],
            out_specs=spec),
        out_shape=jax.ShapeDtypeStruct((S, half, C), BF16), compiler_params=_params(32, 2),
    )(pos, g, got)


def _chip_sum(name, pairs, got, pos):
    _, half, C = pairs.shape
    rb = _row_block(half)
    nh = half // rb

    def body(pos_ref, a_ref, g_ref, o_ref):
        del pos_ref
        o_ref[...] = ((a_ref[0].astype(F32) + g_ref[0].astype(F32)) + g_ref[1].astype(F32)) + g_ref[2].astype(F32)

    return pl.pallas_call(
        body, name=name,
        grid_spec=pltpu.PrefetchScalarGridSpec(
            num_scalar_prefetch=1, grid=(nh,),
            in_specs=[pl.BlockSpec((1, rb, C), lambda r, p: (p[0], r, 0)),
                      pl.BlockSpec((3, rb, C), lambda r, p: (0, r, 0))],
            out_specs=pl.BlockSpec((rb, C), lambda r, p: (p[1] * nh + r, 0))),
        out_shape=jax.ShapeDtypeStruct((2 * half, C), F32), compiler_params=_params(32, 1),
    )(pos, pairs, got)


def _adamw(name, w, g, m, v):
    R, C = w.shape
    rb = _row_block(R)
    c1 = 1.0 - ADAM_B1 ** ADAM_STEP
    c2 = 1.0 - ADAM_B2 ** ADAM_STEP

    def body(w_ref, g_ref, m_ref, v_ref, d_ref, nm_ref, nv_ref):
        gv = g_ref[...]
        nm = ADAM_B1 * m_ref[...] + (1.0 - ADAM_B1) * gv
        nv = ADAM_B2 * v_ref[...] + (1.0 - ADAM_B2) * (gv * gv)
        nm_ref[...] = nm
        nv_ref[...] = nv
        d_ref[...] = -ADAM_LR * ((nm / c1) / (jnp.sqrt(nv / c2) + ADAM_EPS) + ADAM_WD * w_ref[...])

    spec = pl.BlockSpec((rb, C), lambda r: (r, 0))
    sds = jax.ShapeDtypeStruct(w.shape, F32)
    return pl.pallas_call(
        body, name=name, grid=(R // rb,), in_specs=[spec] * 4, out_specs=[spec] * 3,
        out_shape=[sds, sds, sds], compiler_params=_params(40, 1),
    )(w, g, m, v)


def _rel_index():
    m = np.arange(2 * ATT_BLK)
    off = np.where(m < ATT_BLK, m, m - 2 * ATT_BLK)
    rel = np.stack([ATT_BLK * d - off for d in range(N_ATT_TILES)])
    return np.clip(rel, -MAX_REL, MAX_REL) + MAX_REL


def _local_step(x, tgt, g1, w_in, cw, cb, lg, lb, bias, w_out, g2, g3, w_up, fw, fb, w_down, g4, pos=None):
    T = x.shape[0]
    dist = pos is not None
    idx = _rel_index()

    (u, a, qkv), got = _fwd_in_proj(x, g1, w_in, _fetch_rider([w_out, w_down]) if dist else None)
    if dist:
        w_out, w_down = got
    (co, hc), got = _fwd_conv(a, cw, cb, lg, lb, _merge_riders(_forward_rider([w_out, w_down]),
                                                               _fetch_rider([w_up])) if dist else None)
    if dist:
        w_out, w_down, w_up = got
    (ao, lse), got = _fwd_attn(qkv, bias, _forward_rider([w_up]) if dist else None)
    if dist:
        (w_up,) = got
        w_out, w_down = w_out.reshape(D_MODEL, D_MODEL), w_down.reshape(D_FF, D_MODEL)
    mixed, h1, u2 = _fwd_out_proj(co, ao, w_out, x, g2, g3)
    w_cat = _pair_up_weights(w_up)
    hf, pre, act, fp = _fwd_ffn(u2, w_cat, fw, fb, w_down)
    loss, dy, df, dg4 = _fwd_loss(fp, h1, tgt, g4)

    tk = 1024
    du2p, dhf, dfw_g, dfw_v = _bwd_ffn(df, hf, pre, w_cat, fw, w_down)
    gw_up = _wgrad(
        "wgrad_up", [u2], pl.BlockSpec((tk, D_MODEL), lambda s, k: (k, 0)),
        dhf, pl.BlockSpec((1, tk, FF_SHARD), lambda s, k: (s // 2, k, s % 2)),
        pl.BlockSpec((1, D_MODEL, FF_SHARD), lambda s, k: (s, 0, 0)),
        jax.ShapeDtypeStruct((N_CHIPS, D_MODEL, FF_SHARD), F32), N_CHIPS, T, tk)
    gw_down = _wgrad(
        "wgrad_down", [act], pl.BlockSpec((tk, FF_SHARD), lambda s, k: (k, s)),
        df, pl.BlockSpec((tk, D_MODEL), lambda s, k: (k, 0)),
        pl.BlockSpec((FF_SHARD, D_MODEL), lambda s, k: (s, 0)),
        jax.ShapeDtypeStruct((D_FF, D_MODEL), F32), 2, T, tk).reshape(N_CHIPS, D_FF // N_CHIPS, D_MODEL)
    (dh1, dmx, dco, dao, dg3, dg2), _ = _bwd_mid(du2p, dy, h1, mixed, g3, g2, w_out)
    gw_out = _wgrad(
        "wgrad_out", [co, ao], pl.BlockSpec((tk, CONV_W), lambda s, k: (k, 0)),
        dmx, pl.BlockSpec((tk, D_MODEL), lambda s, k: (k, 0)),
        pl.BlockSpec((CONV_W, D_MODEL), lambda s, k: (s, 0)),
        jax.ShapeDtypeStruct((D_MODEL, D_MODEL), F32), 2, T, tk,
        select=lambda s: s, rider=_pair_exchange_rider([gw_up, gw_down]) if dist else None)
    if dist:
        gw_out, got = gw_out
        p_up = _pair_sum("pair_sum_w_up", gw_up, got[0], pos)
        p_down = _pair_sum("pair_sum_w_down", gw_down, got[1], pos)
    gw_out = gw_out.reshape(N_CHIPS, D_MODEL // N_CHIPS, D_MODEL)
    (dproj, dsacc), got = _bwd_attn(
        qkv, ao, dao, lse, bias,
        _merge_riders(_chip_exchange_rider([p_up, p_down]), _pair_exchange_rider([gw_out])) if dist else None)
    if dist:
        gw_up = _chip_sum("chip_sum_w_up", p_up, got[0], pos)
        gw_down = _chip_sum("chip_sum_w_down", p_down, got[1], pos)
        p_out = _pair_sum("pair_sum_w_out", gw_out, got[2], pos)
    (dproj, dcw, dcb, dlg, dlb), got = _bwd_conv(
        dproj, a, dco, hc, cw, lg, lb,
        _merge_riders(_pair_gather_rider([gw_up, gw_down]), _chip_exchange_rider([p_out])) if dist else None)
    if dist:
        gw_up, gw_down = got[:2]
        gw_out = _chip_sum("chip_sum_w_out", p_out, got[2], pos)
    gw_in = _wgrad(
        "wgrad_in", [u], pl.BlockSpec((tk, D_MODEL), lambda s, k: (k, 0)),
        dproj, pl.BlockSpec((tk, IN_SHARD), lambda s, k: (k, s)),
        pl.BlockSpec((1, D_MODEL, IN_SHARD), lambda s, k: (s, 0, 0)),
        jax.ShapeDtypeStruct((N_CHIPS, D_MODEL, IN_SHARD), F32), N_CHIPS, T, tk)
    if dist:
        got = _alone("pair_exchange_w_in", _merge_riders(_pair_exchange_rider([gw_in]), _pair_gather_rider([gw_out])))
        p_in, gw_out = _pair_sum("pair_sum_w_in", gw_in, got[0], pos), got[1]
    (gx, dg1), _ = _bwd_in_proj(dproj, w_in, x, dh1, g1)
    (diag,), got = _diag_sums(dsacc, _chip_exchange_rider([p_in]) if dist else None)
    if dist:
        (gw_in,) = _alone("pair_gather_w_in", _pair_gather_rider([_chip_sum("chip_sum_w_in", p_in, got[0], pos)]))

    diag = diag.reshape(N_ATT_TILES, N_HEADS, 2 * ATT_BLK)
    onehot = np.zeros((N_ATT_TILES, 2 * ATT_BLK, 2 * MAX_REL + 1), np.float32)
    for d in range(N_ATT_TILES):
        onehot[d, np.arange(2 * ATT_BLK), idx[d]] = 1.0
    drel = jnp.einsum("dhm,dmr->hr", diag, jnp.asarray(onehot), precision=lax.Precision.HIGHEST)

    small = dict(norm_mix_pre=dg1, conv_dw_w=dcw[:CONV_K], conv_dw_b=dcb, conv_ln_g=dlg, conv_ln_b=dlb,
                 rel_bias=drel, norm_mix_post=dg2, norm_ffn_pre=dg3,
                 ffn_dw_w=jnp.concatenate([dfw_g[0, :3], dfw_g[1, :3], dfw_v[0, :3], dfw_v[1, :3]], axis=1),
                 ffn_dw_b=jnp.concatenate([dfw_g[0, 3:4], dfw_g[1, 3:4], dfw_v[0, 3:4], dfw_v[1, 3:4]], axis=1),
                 norm_ffn_post=dg4)
    return loss, gx, small, dict(w_in=gw_in, w_out=gw_out, w_up=gw_up, w_down=gw_down)


SMALL_ORDER = ["norm_mix_pre", "conv_dw_b", "conv_ln_g", "conv_ln_b", "rel_bias", "norm_mix_post",
               "norm_ffn_pre", "ffn_dw_b", "norm_ffn_post", "conv_dw_w", "ffn_dw_w"]


def _pack(parts):
    rows = []
    for p in parts:
        width = -(-p.shape[1] // LANES) * LANES
        rows.append(jnp.pad(p, ((0, 0), (0, width - p.shape[1]))).reshape(-1, LANES))
    packed = jnp.concatenate(rows, axis=0)
    pad = -packed.shape[0] % 8
    return jnp.pad(packed, ((0, pad), (0, 0)))


def _unpack(packed, shapes):
    out, r = [], 0
    for shp in shapes:
        width = -(-shp[1] // LANES) * LANES
        n = shp[0] * width // LANES
        out.append(packed[r:r + n].reshape(shp[0], width)[:, :shp[1]])
        r += n
    return out


WEIGHTS = ["norm_mix_pre", "w_in", "conv_dw_w", "conv_dw_b", "conv_ln_g", "conv_ln_b", "rel_bias", "w_out",
           "norm_mix_post", "norm_ffn_pre", "w_up", "ffn_dw_w", "ffn_dw_b", "w_down", "norm_ffn_post"]
BIG = ["w_in", "w_out", "w_up", "w_down"]


def kernel(x, norm_mix_pre, w_in, conv_dw_w, conv_dw_b, conv_ln_g, conv_ln_b, rel_bias, w_out, norm_mix_post, norm_ffn_pre, w_up, ffn_dw_w, ffn_dw_b, w_down, norm_ffn_post, loss_target, m_norm_mix_pre, m_w_in, m_conv_dw_w, m_conv_dw_b, m_conv_ln_g, m_conv_ln_b, m_rel_bias, m_w_out, m_norm_mix_post, m_norm_ffn_pre, m_w_up, m_ffn_dw_w, m_ffn_dw_b, m_w_down, m_norm_ffn_post, v_norm_mix_pre, v_w_in, v_conv_dw_w, v_conv_dw_b, v_conv_ln_g, v_conv_ln_b, v_rel_bias, v_w_out, v_norm_mix_post, v_norm_ffn_pre, v_w_up, v_ffn_dw_w, v_ffn_dw_b, v_w_down, v_norm_ffn_post):
    args = locals()
    w = {n: args[n][0] for n in WEIGHTS}
    m = {n: args["m_" + n][0] for n in WEIGHTS}
    v = {n: args["v_" + n][0] for n in WEIGHTS}
    for d in (w, m, v):
        d["rel_bias"] = d["rel_bias"].reshape(N_HEADS, 2 * MAX_REL + 1)
        for n in ("norm_mix_pre", "conv_dw_b", "conv_ln_g", "conv_ln_b", "norm_mix_post", "norm_ffn_pre",
                  "ffn_dw_b", "norm_ffn_post"):
            d[n] = d[n].reshape(1, -1)
    shard = 2 * lax.axis_index("x") + lax.axis_index("y")

    cw_sh = jnp.pad(w["conv_dw_w"], ((0, CONV_HALO - CONV_K), (0, 0)))
    fw_sh = jnp.pad(w["ffn_dw_w"], ((0, FF_HALO - 3), (0, 0)))
    pos = jnp.stack([shard, lax.axis_index("c")]).astype(jnp.int32)
    bufs = {n: _place("place_" + n, w[n], pos, BF16) for n in BIG}
    first = [bufs["w_in"], _place("place_conv_dw_w", cw_sh, pos, F32), _place("place_ffn_dw_w", fw_sh, pos, F32)]
    (bias,), first = _bias_tiles(w["rel_bias"], _fetch_rider(first))
    w_in_f, cw_f, fw_f = _alone("all_gather_forward", _forward_rider(list(first)))
    cw_full = jnp.transpose(cw_f, (1, 0, 2)).reshape(CONV_HALO, CONV_W)

    loss, gx, small, big = _local_step(
        x[0], loss_target[0], w["norm_mix_pre"], _join_columns(w_in_f), cw_full, w["conv_dw_b"], w["conv_ln_g"],
        w["conv_ln_b"], bias, bufs["w_out"], w["norm_mix_post"],
        w["norm_ffn_pre"], bufs["w_up"], fw_f, w["ffn_dw_b"].reshape(N_CHIPS, 1, FF_SHARD),
        bufs["w_down"], w["norm_ffn_post"], pos)
    grads, deltas, new_m, new_v = {}, {}, {}, {}
    for n in BIG:
        grads[n] = big[n]
        deltas[n], new_m[n], new_v[n] = _adamw("adamw_" + n, w[n], big[n], m[n], v[n])

    gsum = _all_reduce_small(_pack([small[n] for n in SMALL_ORDER] + [loss]))
    shapes = [small[n].shape for n in SMALL_ORDER]
    *reduced, total = _unpack(gsum, shapes + [loss.shape])
    gs = dict(zip(SMALL_ORDER, reduced))
    gs["conv_dw_w"] = lax.dynamic_slice_in_dim(gs["conv_dw_w"], shard * LANES, LANES, axis=1)
    gs["ffn_dw_w"] = lax.dynamic_slice_in_dim(gs["ffn_dw_w"], shard * FF_SHARD, FF_SHARD, axis=1)
    shapes = [gs[n].shape for n in SMALL_ORDER]
    d_p, m_p, v_p = _adamw("adamw_small", _pack([w[n] for n in SMALL_ORDER]), _pack([gs[n] for n in SMALL_ORDER]),
                           _pack([m[n] for n in SMALL_ORDER]), _pack([v[n] for n in SMALL_ORDER]))
    for dst, packed in ((deltas, d_p), (new_m, m_p), (new_v, v_p)):
        dst.update(zip(SMALL_ORDER, _unpack(packed, shapes)))
    grads.update(gs)

    outs = [total[0, 0], gx[None]]
    for group in (grads, deltas, new_m, new_v):
        outs += [group[n].reshape(args[n].shape) for n in WEIGHTS]
    return tuple(outs)
```

```python
import functools
import math
from typing import Callable, NamedTuple

import numpy as np
import jax
import jax.numpy as jnp
from jax import lax
from jax.experimental import pallas as pl
from jax.experimental.pallas import tpu as pltpu

F32 = jnp.float32
BF16 = jnp.bfloat16

D_MODEL = 1024
CONV_W = 512
ATTN_W = 512
N_HEADS = 8
HEAD_DIM = 64
CHUNK = 64
N_LEFT = 8
MAX_REL = 128
CONV_K = 31
CONV_HALO = 32
D_FF = 2816
FF_SHARD = 1408
IN_COLS = 2560
IN_SHARD = 640
EPS = 1e-6
NEG_INF = -1e30
ATT_BLK = 256
N_ATT_TILES = 3
LANES = 128
SUBLANES = 8
N_CHIPS = 4

ADAM_LR = 0.001
ADAM_B1 = 0.9
ADAM_B2 = 0.999
ADAM_EPS = 1e-08
ADAM_WD = 0.01
ADAM_STEP = 10

MESH = pl.DeviceIdType.MESH
ANY = pl.BlockSpec(memory_space=pl.ANY)
VMEM_FULL = pl.BlockSpec(memory_space=pltpu.VMEM)


def _params(vmem_mb, n_grid=0):
    sem = ("arbitrary",) * n_grid if n_grid else None
    return pltpu.CompilerParams(dimension_semantics=sem, vmem_limit_bytes=vmem_mb << 20)


class _Rider(NamedTuple):
    operands: list
    out_shape: list
    aliases: dict
    sems: list
    start: Callable
    finish: Callable


def _merge_riders(a, b):
    ia, oa, sa = len(a.operands), len(a.out_shape), len(a.sems)

    def start(ins, outs, sems):
        a.start(ins[:ia], outs[:oa], sems[:sa])
        b.start(ins[ia:], outs[oa:], sems[sa:])

    def finish(ins, outs, sems):
        a.finish(ins[:ia], outs[:oa], sems[:sa])
        b.finish(ins[ia:], outs[oa:], sems[sa:])

    aliases = {**a.aliases, **{k + ia: v + oa for k, v in b.aliases.items()}}
    return _Rider(a.operands + b.operands, a.out_shape + b.out_shape, aliases, a.sems + b.sems, start, finish)


PIN_BYTES = 1 << 20


def _big(a):
    return math.prod(a.shape) * jnp.dtype(a.dtype).itemsize >= PIN_BYTES


def _pin_args(args):
    return [pltpu.with_memory_space_constraint(a, pltpu.HBM) if _big(a) else a for a in args]


def _call(body, rider, *, grid=(), in_specs=(), out_specs=(), out_shape=(), scratch_shapes=(),
          input_output_aliases=None, **kwargs):
    in_specs, out_specs = list(in_specs), list(out_specs)
    pin_out = lambda shapes: [pltpu.HBM(s.shape, s.dtype) if _big(s) else s for s in shapes]
    out_shape = pin_out(out_shape)
    scratch, aliases = list(scratch_shapes), dict(input_output_aliases or {})
    if rider is None:
        plain = pl.pallas_call(body, grid=grid, in_specs=in_specs, out_specs=out_specs, out_shape=out_shape,
                               scratch_shapes=scratch, input_output_aliases=aliases, **kwargs)
        return lambda *args: (plain(*_pin_args(args)), [])
    n_in, n_out, n_scr = len(in_specs), len(out_specs), len(scratch)
    r_in, r_out = len(rider.operands), len(rider.out_shape)

    def carried(*refs):
        ins, r_ins, refs = refs[:n_in], refs[n_in:n_in + r_in], refs[n_in + r_in:]
        outs, r_outs, refs = refs[:n_out], refs[n_out:n_out + r_out], refs[n_out + r_out:]
        scr, r_sems = refs[:n_scr], refs[n_scr:]
        if not grid:
            rider.start(r_ins, r_outs, r_sems)
            body(*ins, *outs, *scr)
            rider.finish(r_ins, r_outs, r_sems)
            return
        at = [pl.program_id(d) for d in range(len(grid))]
        first = functools.reduce(jnp.logical_and, [p == 0 for p in at])
        last = functools.reduce(jnp.logical_and, [p == n - 1 for p, n in zip(at, grid)])

        @pl.when(first)
        def _():
            rider.start(r_ins, r_outs, r_sems)

        body(*ins, *outs, *scr)

        @pl.when(last)
        def _():
            rider.finish(r_ins, r_outs, r_sems)

    aliases.update({n_in + k: n_out + v for k, v in rider.aliases.items()})
    both = pl.pallas_call(carried, grid=grid, in_specs=in_specs + [ANY] * r_in, out_specs=out_specs + [ANY] * r_out,
                          out_shape=out_shape + pin_out(rider.out_shape), scratch_shapes=scratch + rider.sems,
                          input_output_aliases=aliases, **kwargs)

    def run(*args):
        res = both(*_pin_args(args), *rider.operands)
        return res[:n_out], res[n_out:]

    return run


def _sigmoid(v):
    return 1.0 / (1.0 + jnp.exp(-v))


def _dot(a, b):
    return jnp.dot(a, b, preferred_element_type=F32)


def _dot_nt(a, b):
    return lax.dot_general(a, b, (((1,), (1,)), ((), ())), preferred_element_type=F32)


def _dot_tn(a, b):
    return lax.dot_general(a, b, (((0,), (0,)), ((), ())), preferred_element_type=F32)


def _rms_fwd(v, g):
    r = lax.rsqrt(jnp.mean(v * v, axis=-1, keepdims=True) + EPS)
    return v * r * g, r


def _rms_bwd(dy, v, g):
    r = lax.rsqrt(jnp.mean(v * v, axis=-1, keepdims=True) + EPS)
    vh = v * r
    dvh = dy * g
    dv = r * (dvh - vh * jnp.mean(dvh * vh, axis=-1, keepdims=True))
    return dv, dy * vh


def _join_columns(w):
    S, R, C = w.shape
    rb = 256

    def body(w_ref, o_ref):
        for s in range(S):
            o_ref[:, s * C:(s + 1) * C] = w_ref[s]

    return pl.pallas_call(
        body, name="join_columns", grid=(R // rb,),
        in_specs=[pl.BlockSpec((S, rb, C), lambda r: (0, r, 0))],
        out_specs=pl.BlockSpec((rb, S * C), lambda r: (r, 0)),
        out_shape=jax.ShapeDtypeStruct((R, S * C), w.dtype),
        compiler_params=_params(32, 1),
    )(w)


GLU_COLS = 2 * CONV_W


def _fwd_in_proj(x, g1, w_in, rider=None):
    T = x.shape[0]
    tm = 512

    def body(x_ref, g_ref, w_ref, u_ref, a_ref, qkv_ref):
        u, _ = _rms_fwd(x_ref[...], g_ref[...])
        u = u.astype(BF16)
        u_ref[...] = u
        a_ref[...] = _dot(u, w_ref[:, :GLU_COLS])
        qkv_ref[...] = _dot(u, w_ref[:, GLU_COLS:]).astype(BF16)

    return _call(
        body, rider, name="fwd_in_proj", grid=(T // tm,),
        in_specs=[pl.BlockSpec((tm, D_MODEL), lambda i: (i, 0)),
                  pl.BlockSpec((1, D_MODEL), lambda i: (0, 0)),
                  pl.BlockSpec((D_MODEL, IN_COLS), lambda i: (0, 0))],
        out_specs=[pl.BlockSpec((tm, D_MODEL), lambda i: (i, 0)),
                   pl.BlockSpec((tm, 1024), lambda i: (i, 0)),
                   pl.BlockSpec((tm, 1536), lambda i: (i, 0))],
        out_shape=[jax.ShapeDtypeStruct((T, D_MODEL), BF16),
                   jax.ShapeDtypeStruct((T, 1024), F32),
                   jax.ShapeDtypeStruct((T, 1536), BF16)],
        compiler_params=_params(40, 1),
    )(x, g1, w_in)


def _fill_shifted(ext, shifted, tm):
    n = tm + CONV_HALO - SUBLANES
    for j in range(1, SUBLANES):
        shifted[j - 1] = ext[j:j + n, :]


def _shifted_rows(ext, shifted, start, rows):
    j = start % SUBLANES
    if j == 0:
        return ext[start:start + rows, :]
    return shifted[j - 1, start - j:start - j + rows, :]


def _fwd_conv(a, cw, cb, lg, lb, rider=None):
    T = a.shape[0]
    tm = 512
    rc = 64

    def body(a_ref, w_ref, b_ref, lg_ref, lb_ref, co_ref, hc_ref, hext, hsh):
        i = pl.program_id(0)

        @pl.when(i == 0)
        def _():
            hext[0:CONV_HALO, :] = jnp.zeros((CONV_HALO, CONV_W), F32)

        @pl.when(i > 0)
        def _():
            hext[0:CONV_HALO, :] = hext[tm:tm + CONV_HALO, :]

        hext[CONV_HALO:CONV_HALO + tm, :] = a_ref[:, :CONV_W] * _sigmoid(a_ref[:, CONV_W:])
        _fill_shifted(hext, hsh, tm)
        for c in range(tm // rc):
            acc = jnp.zeros((rc, CONV_W), F32)
            for k in range(CONV_K):
                acc = acc + w_ref[k:k + 1, :] * _shifted_rows(hext, hsh, c * rc + 2 + k, rc)
            hc = acc + b_ref[...]
            hc_ref[c * rc:(c + 1) * rc, :] = hc
            mu = jnp.mean(hc, axis=-1, keepdims=True)
            xc = hc - mu
            var = jnp.mean(xc * xc, axis=-1, keepdims=True)
            z = xc * lax.rsqrt(var + EPS) * lg_ref[...] + lb_ref[...]
            co_ref[c * rc:(c + 1) * rc, :] = (z * _sigmoid(z)).astype(BF16)

    return _call(
        body, rider, name="fwd_conv", grid=(T // tm,),
        in_specs=[pl.BlockSpec((tm, 1024), lambda i: (i, 0)),
                  pl.BlockSpec((CONV_HALO, CONV_W), lambda i: (0, 0)),
                  pl.BlockSpec((1, CONV_W), lambda i: (0, 0)),
                  pl.BlockSpec((1, CONV_W), lambda i: (0, 0)),
                  pl.BlockSpec((1, CONV_W), lambda i: (0, 0))],
        out_specs=[pl.BlockSpec((tm, CONV_W), lambda i: (i, 0)),
                   pl.BlockSpec((tm, CONV_W), lambda i: (i, 0))],
        out_shape=[jax.ShapeDtypeStruct((T, CONV_W), BF16),
                   jax.ShapeDtypeStruct((T, CONV_W), F32)],
        scratch_shapes=[pltpu.VMEM((tm + CONV_HALO, CONV_W), F32),
                        pltpu.VMEM((SUBLANES - 1, tm + CONV_HALO - SUBLANES, CONV_W), F32)],
        compiler_params=_params(40, 1),
    )(a, cw, cb, lg, lb)


def _row_skew(v, sign):
    rows, width = v.shape
    row = lax.broadcasted_iota(jnp.int32, (rows, 1), 0)
    for b in range(int(math.log2(rows))):
        shift = (1 << b) if sign > 0 else width - (1 << b)
        v = jnp.where(((row >> b) & 1) == 1, pltpu.roll(v, shift, 1), v)
    return v


def _att_visible(d):
    rq = lax.broadcasted_iota(jnp.int32, (ATT_BLK, ATT_BLK), 0) // CHUNK
    ck = lax.broadcasted_iota(jnp.int32, (ATT_BLK, ATT_BLK), 1) // CHUNK
    slack = ATT_BLK
    above = jnp.where(d == 0, 0, slack)
    below = jnp.where(d == 2, 0, slack)
    return (ck <= rq + above) & (ck >= rq - below)


def _bias_tiles(rel, rider=None):
    vec = jnp.transpose(rel[:, _rel_index()], (1, 0, 2)).reshape(N_ATT_TILES * N_HEADS, 1, 2 * ATT_BLK)

    def body(v_ref, o_ref):
        visible = _att_visible(pl.program_id(0))
        for h in range(N_HEADS):
            full = _row_skew(jnp.broadcast_to(v_ref[h], (ATT_BLK, 2 * ATT_BLK)), 1)
            o_ref[h] = jnp.where(visible, full[:, :ATT_BLK], NEG_INF)

    return _call(
        body, rider, name="bias_tiles", grid=(N_ATT_TILES,),
        in_specs=[pl.BlockSpec((N_HEADS, 1, 2 * ATT_BLK), lambda d: (d, 0, 0))],
        out_specs=[pl.BlockSpec((N_HEADS, ATT_BLK, ATT_BLK), lambda d: (d, 0, 0))],
        out_shape=[jax.ShapeDtypeStruct((N_ATT_TILES * N_HEADS, ATT_BLK, ATT_BLK), F32)],
        compiler_params=_params(32, 1),
    )(vec)


def _diag_sums(ds, rider=None):
    def body(d_ref, o_ref):
        wide = jnp.concatenate([d_ref[0], jnp.zeros((ATT_BLK, ATT_BLK), F32)], axis=1)
        o_ref[0] = jnp.sum(_row_skew(wide, -1), axis=0, keepdims=True)

    return _call(
        body, rider, name="diag_sums", grid=(N_ATT_TILES * N_HEADS,),
        in_specs=[pl.BlockSpec((1, ATT_BLK, ATT_BLK), lambda n: (n, 0, 0))],
        out_specs=[pl.BlockSpec((1, 1, 2 * ATT_BLK), lambda n: (n, 0, 0))],
        out_shape=[jax.ShapeDtypeStruct((N_ATT_TILES * N_HEADS, 1, 2 * ATT_BLK), F32)],
        compiler_params=_params(16, 1),
    )(ds)


def _head_mask(h):
    lane = lax.broadcasted_iota(jnp.int32, (1, LANES), 1)
    return (lane // HEAD_DIM) == (h % 2)


def _fwd_attn(qkv, bias, rider=None):
    T = qkv.shape[0]
    nb = T // ATT_BLK
    scale = HEAD_DIM ** -0.5

    def body(q_ref, k0_ref, k1_ref, k2_ref, v0_ref, v1_ref, v2_ref, b_ref, o_ref, lse_ref):
        i = pl.program_id(0)

        @pl.when(i >= N_ATT_TILES - 1)
        def _():
            block(i, False, q_ref, k0_ref, k1_ref, k2_ref, v0_ref, v1_ref, v2_ref, b_ref, o_ref, lse_ref)

        @pl.when(i < N_ATT_TILES - 1)
        def _():
            block(i, True, q_ref, k0_ref, k1_ref, k2_ref, v0_ref, v1_ref, v2_ref, b_ref, o_ref, lse_ref)

    def block(i, hide_absent, q_ref, k0_ref, k1_ref, k2_ref, v0_ref, v1_ref, v2_ref, b_ref, o_ref, lse_ref):
        k_refs = (k0_ref, k1_ref, k2_ref)
        v_refs = (v0_ref, v1_ref, v2_ref)
        lane = lax.broadcasted_iota(jnp.int32, (1, LANES), 1)
        lse_tile = jnp.zeros((ATT_BLK, LANES), F32)
        for g in range(N_HEADS // 2):
            cols = slice(g * LANES, (g + 1) * LANES)
            qg = q_ref[:, cols] * scale
            og = jnp.zeros((ATT_BLK, LANES), F32)
            for h in (2 * g, 2 * g + 1):
                hm = _head_mask(h)
                qh = jnp.where(hm, qg, jnp.zeros_like(qg))
                s = []
                for d in range(N_ATT_TILES):
                    sd = _dot_nt(qh, k_refs[d][:, cols]) + b_ref[d * N_HEADS + h]
                    if d > 0 and hide_absent:
                        sd = jnp.where(i >= d, sd, NEG_INF)
                    s.append(sd)
                m = jnp.maximum(jnp.maximum(jnp.max(s[0], axis=-1, keepdims=True),
                                            jnp.max(s[1], axis=-1, keepdims=True)),
                                jnp.max(s[2], axis=-1, keepdims=True))
                p = [jnp.exp(sd - m) for sd in s]
                l = (jnp.sum(p[0], axis=-1, keepdims=True) + jnp.sum(p[1], axis=-1, keepdims=True)
                     + jnp.sum(p[2], axis=-1, keepdims=True))
                oh = jnp.zeros((ATT_BLK, LANES), F32)
                for d in range(N_ATT_TILES):
                    vg = v_refs[d][:, cols]
                    oh = oh + _dot(p[d].astype(BF16), jnp.where(hm, vg, jnp.zeros_like(vg)))
                og = og + oh / l
                lse_tile = jnp.where(lane == h, m + jnp.log(l), lse_tile)
            o_ref[:, cols] = og.astype(BF16)
        lse_ref[...] = lse_tile

    def kv_spec(d, col):
        return pl.BlockSpec((ATT_BLK, ATTN_W), lambda i: (jnp.maximum(i - d, 0), col))

    return _call(
        body, rider, name="fwd_attn", grid=(nb,),
        in_specs=[pl.BlockSpec((ATT_BLK, ATTN_W), lambda i: (i, 0)),
                  kv_spec(0, 1), kv_spec(1, 1), kv_spec(2, 1),
                  kv_spec(0, 2), kv_spec(1, 2), kv_spec(2, 2),
                  pl.BlockSpec((N_ATT_TILES * N_HEADS, ATT_BLK, ATT_BLK), lambda i: (0, 0, 0))],
        out_specs=[pl.BlockSpec((ATT_BLK, ATTN_W), lambda i: (i, 0)),
                   pl.BlockSpec((ATT_BLK, LANES), lambda i: (i, 0))],
        out_shape=[jax.ShapeDtypeStruct((T, ATTN_W), BF16),
                   jax.ShapeDtypeStruct((T, LANES), F32)],
        compiler_params=_params(40, 1),
    )(qkv, qkv, qkv, qkv, qkv, qkv, qkv, bias)


def _fwd_out_proj(co, ao, w_out, x, g2, g3):
    T = x.shape[0]
    tm = 512

    def body(co_ref, ao_ref, w_ref, x_ref, g2_ref, g3_ref, mixed_ref, h1_ref, u2_ref):
        mixed = _dot(co_ref[...], w_ref[0:CONV_W, :]) + _dot(ao_ref[...], w_ref[CONV_W:, :])
        mixed_ref[...] = mixed.astype(BF16)
        y, _ = _rms_fwd(mixed, g2_ref[...])
        h1 = x_ref[...] + y
        h1_ref[...] = h1
        u2, _ = _rms_fwd(h1, g3_ref[...])
        u2_ref[...] = u2.astype(BF16)

    row = lambda w: pl.BlockSpec((tm, w), lambda i: (i, 0))
    vec = pl.BlockSpec((1, D_MODEL), lambda i: (0, 0))
    return _call(
        body, None, name="fwd_out_proj", grid=(T // tm,),
        in_specs=[row(CONV_W), row(ATTN_W), pl.BlockSpec((D_MODEL, D_MODEL), lambda i: (0, 0)),
                  row(D_MODEL), vec, vec],
        out_specs=[row(D_MODEL), row(D_MODEL), row(D_MODEL)],
        out_shape=[jax.ShapeDtypeStruct((T, D_MODEL), BF16),
                   jax.ShapeDtypeStruct((T, D_MODEL), F32),
                   jax.ShapeDtypeStruct((T, D_MODEL), BF16)],
        compiler_params=_params(40, 1),
    )(co, ao, w_out, x, g2, g3)[0]


GELU_C = math.sqrt(2.0 / math.pi)
GELU_A = 0.044715


def _gelu_and_grad(v):
    sq = v * v
    th = jnp.tanh(v * (GELU_C + (GELU_C * GELU_A) * sq))
    half = 0.5 + 0.5 * th
    gl = v * half
    dgl = half + (v * (half * (1.0 - th))) * (GELU_C + (3.0 * GELU_C * GELU_A) * sq)
    return gl, dgl


FF_TM = 256
FF_HALO = 16
FF_CHUNKS = [(lo, min(lo + 256, FF_SHARD)) for lo in range(0, FF_SHARD, 256)]


def _rows_before(prev, cur):
    ext = jnp.concatenate([prev, cur], axis=0)
    return pltpu.roll(ext, 1, 0)[SUBLANES:], pltpu.roll(ext, 2, 0)[SUBLANES:]


def _rows_after(cur, nxt):
    ext = jnp.concatenate([cur, nxt], axis=0)
    n = ext.shape[0]
    return pltpu.roll(ext, n - 1, 0)[:cur.shape[0]], pltpu.roll(ext, n - 2, 0)[:cur.shape[0]]


def _pair_up_weights(w_up):
    rb = 256

    def body(g_ref, v_ref, o_ref):
        for lo, hi in FF_CHUNKS:
            o_ref[0, :, 2 * lo:lo + hi] = g_ref[0, :, lo:hi]
            o_ref[0, :, lo + hi:2 * hi] = v_ref[0, :, lo:hi]

    return pl.pallas_call(
        body, name="pair_up_weights", grid=(2, D_MODEL // rb),
        in_specs=[pl.BlockSpec((1, rb, FF_SHARD), lambda s, r: (s, r, 0)),
                  pl.BlockSpec((1, rb, FF_SHARD), lambda s, r: (s + 2, r, 0))],
        out_specs=pl.BlockSpec((1, rb, 2 * FF_SHARD), lambda s, r: (s, r, 0)),
        out_shape=jax.ShapeDtypeStruct((2, D_MODEL, 2 * FF_SHARD), w_up.dtype),
        compiler_params=_params(32, 2),
    )(w_up, w_up)


def _fwd_ffn(u2, w_cat, fw, fb, w_down):
    T = u2.shape[0]
    tm = FF_TM

    def body(u_ref, w_ref, fwg_ref, fwv_ref, fbg_ref, fbv_ref, wd_ref, hf_ref, pre_ref, act_ref, f_ref, carg, carv):
        i = pl.program_id(1)

        @pl.when(i == 0)
        def _():
            carg[...] = jnp.zeros(carg.shape, F32)
            carv[...] = jnp.zeros(carv.shape, F32)

        u = u_ref[...]
        f = None
        up = lambda lo, hi: _dot(u, w_ref[0, :, 2 * lo:2 * hi])
        ahead = up(*FF_CHUNKS[0])
        for c, (lo, hi) in enumerate(FF_CHUNKS):
            conv = []
            hs = (ahead[:, :hi - lo], ahead[:, hi - lo:])
            if c + 1 < len(FF_CHUNKS):
                ahead = up(*FF_CHUNKS[c + 1])
            for n, (car, fw_ref, fb_ref) in enumerate(((carg, fwg_ref, fbg_ref), (carv, fwv_ref, fbv_ref))):
                h0 = hs[n]
                hf_ref[n, :, lo:hi] = h0.astype(BF16)
                h1, h2 = _rows_before(car[:, lo:hi], h0)
                car[:, lo:hi] = h0[tm - SUBLANES:, :]
                conv.append(fw_ref[0, 0:1, lo:hi] * h2 + fw_ref[0, 1:2, lo:hi] * h1
                            + fw_ref[0, 2:3, lo:hi] * h0 + fb_ref[0, :, lo:hi])
            pre_ref[0, :, lo:hi] = conv[0].astype(BF16)
            pre_ref[1, :, lo:hi] = conv[1].astype(BF16)
            gl, _ = _gelu_and_grad(conv[0])
            act = (gl * conv[1]).astype(BF16)
            act_ref[:, lo:hi] = act
            term = _dot(act, wd_ref[lo:hi, :])
            f = term if f is None else f + term
        f_ref[0] = f.astype(BF16)

    fwspec = lambda off: pl.BlockSpec((1, FF_HALO, FF_SHARD), lambda s, i: (s + off, 0, 0))
    fbspec = lambda off: pl.BlockSpec((1, 1, FF_SHARD), lambda s, i: (s + off, 0, 0))
    return _call(
        body, None, name="fwd_ffn", grid=(2, T // tm),
        in_specs=[pl.BlockSpec((tm, D_MODEL), lambda s, i: (i, 0)),
                  pl.BlockSpec((1, D_MODEL, 2 * FF_SHARD), lambda s, i: (s, 0, 0)),
                  fwspec(0), fwspec(2), fbspec(0), fbspec(2),
                  pl.BlockSpec((FF_SHARD, D_MODEL), lambda s, i: (s, 0))],
        out_specs=[pl.BlockSpec((2, tm, FF_SHARD), lambda s, i: (0, i, s)),
                   pl.BlockSpec((2, tm, FF_SHARD), lambda s, i: (0, i, s)),
                   pl.BlockSpec((tm, FF_SHARD), lambda s, i: (i, s)),
                   pl.BlockSpec((1, tm, D_MODEL), lambda s, i: (s, i, 0))],
        out_shape=[jax.ShapeDtypeStruct((2, T, D_FF), BF16),
                   jax.ShapeDtypeStruct((2, T, D_FF), BF16),
                   jax.ShapeDtypeStruct((T, D_FF), BF16),
                   jax.ShapeDtypeStruct((2, T, D_MODEL), BF16)],
        scratch_shapes=[pltpu.VMEM((SUBLANES, FF_SHARD), F32), pltpu.VMEM((SUBLANES, FF_SHARD), F32)],
        compiler_params=_params(48, 2),
    )(u2, w_cat, fw, fw, fb, fb, w_down)[0]


def _fwd_loss(fp, h1, tgt, g4):
    T = h1.shape[0]
    tm = 512

    def body(fp_ref, h1_ref, t_ref, g_ref, loss_ref, dy_ref, df_ref, dg_ref):
        i = pl.program_id(0)
        f = fp_ref[0].astype(F32) + fp_ref[1].astype(F32)
        r, _ = _rms_fwd(f, g_ref[...])
        e = (h1_ref[...] + r) - t_ref[...]
        dy = e * (1.0 / D_MODEL)
        dy_ref[...] = dy
        df, dg_rows = _rms_bwd(dy, f, g_ref[...])
        df_ref[...] = df.astype(BF16)
        part = 0.5 * jnp.sum(jnp.mean(e * e, axis=-1, keepdims=True), axis=0, keepdims=True)
        dg = jnp.sum(dg_rows, axis=0, keepdims=True)

        @pl.when(i == 0)
        def _():
            loss_ref[...] = part
            dg_ref[...] = dg

        @pl.when(i > 0)
        def _():
            loss_ref[...] += part
            dg_ref[...] += dg

    row = pl.BlockSpec((tm, D_MODEL), lambda i: (i, 0))
    vec = pl.BlockSpec((1, D_MODEL), lambda i: (0, 0))
    return _call(
        body, None, name="fwd_loss", grid=(T // tm,),
        in_specs=[pl.BlockSpec((2, tm, D_MODEL), lambda i: (0, i, 0)), row, row, vec],
        out_specs=[pl.BlockSpec((1, 1), lambda i: (0, 0)), row, row, vec],
        out_shape=[jax.ShapeDtypeStruct((1, 1), F32),
                   jax.ShapeDtypeStruct((T, D_MODEL), F32),
                   jax.ShapeDtypeStruct((T, D_MODEL), BF16),
                   jax.ShapeDtypeStruct((1, D_MODEL), F32)],
        compiler_params=_params(40, 1),
    )(fp, h1, tgt, g4)[0]


def _bwd_ffn(df, hf, pre, w_cat, fw, w_down):
    T = df.shape[0]
    tm = FF_TM
    ni = T // tm

    def body(df_ref, hf_ref, pre_ref, wd_ref, w_ref, fwg_ref, fwv_ref,
             du_ref, dhf_ref, dwg_ref, dwv_ref, carg, carv):
        i = pl.program_id(1)

        @pl.when(i == 0)
        def _():
            dwg_ref[...] = jnp.zeros(dwg_ref.shape, F32)
            dwv_ref[...] = jnp.zeros(dwv_ref.shape, F32)
            carg[...] = jnp.zeros(carg.shape, F32)
            carv[...] = jnp.zeros(carv.shape, F32)

        df = df_ref[...]
        du = None
        down = lambda lo, hi: _dot_nt(df, wd_ref[lo:hi, :])
        ahead = down(*FF_CHUNKS[0])
        for c, (lo, hi) in enumerate(FF_CHUNKS):
            dact = ahead
            if c + 1 < len(FF_CHUNKS):
                ahead = down(*FF_CHUNKS[c + 1])
            pre_g = pre_ref[0, :, lo:hi].astype(F32)
            pre_v = pre_ref[1, :, lo:hi].astype(F32)
            gl, dgl = _gelu_and_grad(pre_g)
            dpre = (dact * pre_v * dgl, dact * gl)
            dhs = []
            for n, (car, fw_ref, dw_ref) in enumerate(((carg, fwg_ref, dwg_ref), (carv, fwv_ref, dwv_ref))):
                dp = dpre[n]
                h0 = hf_ref[n, :, lo:hi].astype(F32)
                up1, up2 = _rows_after(dp, car[:, lo:hi])
                car[:, lo:hi] = dp[0:SUBLANES, :]
                for k, shifted in enumerate((up2, up1, dp)):
                    dw_ref[0, k:k + 1, lo:hi] += jnp.sum(shifted * h0, axis=0, keepdims=True)
                dw_ref[0, 3:4, lo:hi] += jnp.sum(dp, axis=0, keepdims=True)
                dh = (fw_ref[0, 2:3, lo:hi] * dp + fw_ref[0, 1:2, lo:hi] * up1
                      + fw_ref[0, 0:1, lo:hi] * up2).astype(BF16)
                dhf_ref[n, :, lo:hi] = dh
                dhs.append(dh)
            term = _dot_nt(jnp.concatenate(dhs, axis=1), w_ref[0, :, 2 * lo:2 * hi])
            du = term if du is None else du + term
        du_ref[0] = du.astype(BF16)

    rev = lambda i: ni - 1 - i
    fwspec = lambda off: pl.BlockSpec((1, FF_HALO, FF_SHARD), lambda s, i: (s + off, 0, 0))
    dwspec = pl.BlockSpec((1, FF_HALO, FF_SHARD), lambda s, i: (s, 0, 0))
    return _call(
        body, None, name="bwd_ffn", grid=(2, ni),
        in_specs=[pl.BlockSpec((tm, D_MODEL), lambda s, i: (rev(i), 0)),
                  pl.BlockSpec((2, tm, FF_SHARD), lambda s, i: (0, rev(i), s)),
                  pl.BlockSpec((2, tm, FF_SHARD), lambda s, i: (0, rev(i), s)),
                  pl.BlockSpec((FF_SHARD, D_MODEL), lambda s, i: (s, 0)),
                  pl.BlockSpec((1, D_MODEL, 2 * FF_SHARD), lambda s, i: (s, 0, 0)),
                  fwspec(0), fwspec(2)],
        out_specs=[pl.BlockSpec((1, tm, D_MODEL), lambda s, i: (s, rev(i), 0)),
                   pl.BlockSpec((2, tm, FF_SHARD), lambda s, i: (0, rev(i), s)),
                   dwspec, dwspec],
        out_shape=[jax.ShapeDtypeStruct((2, T, D_MODEL), BF16),
                   jax.ShapeDtypeStruct((2, T, D_FF), BF16),
                   jax.ShapeDtypeStruct((2, FF_HALO, FF_SHARD), F32),
                   jax.ShapeDtypeStruct((2, FF_HALO, FF_SHARD), F32)],
        scratch_shapes=[pltpu.VMEM((SUBLANES, FF_SHARD), F32), pltpu.VMEM((SUBLANES, FF_SHARD), F32)],
        compiler_params=_params(56, 2),
    )(df, hf, pre, w_down, w_cat, fw, fw)[0]


def _bwd_mid(du2p, dy, h1, mixed, g3, g2, w_out, rider=None):
    T = dy.shape[0]
    tm = 512

    def body(du_ref, dy_ref, h1_ref, mx_ref, g3_ref, g2_ref, w_ref,
             dh1_ref, dmx_ref, dco_ref, dao_ref, dg3_ref, dg2_ref):
        i = pl.program_id(0)
        dres, dg3_rows = _rms_bwd(du_ref[0].astype(F32) + du_ref[1].astype(F32), h1_ref[...], g3_ref[...])
        dh1 = dy_ref[...] + dres
        dh1_ref[...] = dh1
        dmx, dg2_rows = _rms_bwd(dh1, mx_ref[...].astype(F32), g2_ref[...])
        dmx = dmx.astype(BF16)
        dmx_ref[...] = dmx
        dcat = _dot_nt(dmx, w_ref[...])
        dco_ref[...] = dcat[:, :CONV_W]
        dao_ref[...] = dcat[:, CONV_W:].astype(BF16)
        dg3 = jnp.sum(dg3_rows, axis=0, keepdims=True)
        dg2 = jnp.sum(dg2_rows, axis=0, keepdims=True)

        @pl.when(i == 0)
        def _():
            dg3_ref[...] = dg3
            dg2_ref[...] = dg2

        @pl.when(i > 0)
        def _():
            dg3_ref[...] += dg3
            dg2_ref[...] += dg2

    row = lambda w: pl.BlockSpec((tm, w), lambda i: (i, 0))
    vec = pl.BlockSpec((1, D_MODEL), lambda i: (0, 0))
    return _call(
        body, rider, name="bwd_mid", grid=(T // tm,),
        in_specs=[pl.BlockSpec((2, tm, D_MODEL), lambda i: (0, i, 0)), row(D_MODEL), row(D_MODEL),
                  row(D_MODEL), vec, vec, pl.BlockSpec((D_MODEL, D_MODEL), lambda i: (0, 0))],
        out_specs=[row(D_MODEL), row(D_MODEL), row(CONV_W), row(ATTN_W), vec, vec],
        out_shape=[jax.ShapeDtypeStruct((T, D_MODEL), F32),
                   jax.ShapeDtypeStruct((T, D_MODEL), BF16),
                   jax.ShapeDtypeStruct((T, CONV_W), F32),
                   jax.ShapeDtypeStruct((T, ATTN_W), BF16),
                   jax.ShapeDtypeStruct((1, D_MODEL), F32),
                   jax.ShapeDtypeStruct((1, D_MODEL), F32)],
        compiler_params=_params(48, 1),
    )(du2p, dy, h1, mixed, g3, g2, w_out)


def _bwd_attn(qkv, ao, dao, lse, bias, rider=None):
    T = qkv.shape[0]
    nb = T // ATT_BLK
    scale = HEAD_DIM ** -0.5

    def body(k_ref, v_ref, q0, q1, q2, do0, do1, do2, o0, o1, o2, l0, l1, l2, b_ref,
             dp_ref, ds_ref, acc1, acc2):
        j = pl.program_id(0)
        q_refs, do_refs, o_refs, l_refs = (q0, q1, q2), (do0, do1, do2), (o0, o1, o2), (l0, l1, l2)

        @pl.when(j == 0)
        def _():
            ds_ref[...] = jnp.zeros(ds_ref.shape, F32)
            acc1[...] = jnp.zeros(acc1.shape, F32)
            acc2[...] = jnp.zeros(acc2.shape, F32)

        dq_new = [[], [], []]
        dk_cols, dv_cols = [], []
        for g in range(N_HEADS // 2):
            cols = slice(g * LANES, (g + 1) * LANES)
            kg = k_ref[:, cols]
            vg = v_ref[:, cols]
            dkg = jnp.zeros((ATT_BLK, LANES), F32)
            dvg = jnp.zeros((ATT_BLK, LANES), F32)
            dqg = [jnp.zeros((ATT_BLK, LANES), F32) for _ in range(N_ATT_TILES)]
            for d in range(N_ATT_TILES):
                qg = q_refs[d][:, cols] * scale
                dog = do_refs[d][:, cols]
                if d > 0:
                    dog = jnp.where(j + d < nb, dog, jnp.zeros_like(dog))
                prod = dog.astype(F32) * o_refs[d][:, cols].astype(F32)
                for h in (2 * g, 2 * g + 1):
                    hm = _head_mask(h)
                    qh = jnp.where(hm, qg, jnp.zeros_like(qg))
                    doh = jnp.where(hm, dog, jnp.zeros_like(dog))
                    kh = jnp.where(hm, kg, jnp.zeros_like(kg))
                    delta = jnp.sum(jnp.where(hm, prod, 0.0), axis=-1, keepdims=True)
                    s = _dot_nt(qh, kg) + b_ref[d * N_HEADS + h]
                    p = jnp.exp(s - l_refs[d][:, h:h + 1])
                    dvg = dvg + _dot_tn(p.astype(BF16), doh)
                    dpm = _dot_nt(doh, vg)
                    dsc = p * (dpm - delta)
                    ds_ref[d * N_HEADS + h] += dsc
                    dsb = dsc.astype(BF16)
                    dqg[d] = dqg[d] + _dot(dsb, kh)
                    dkg = dkg + _dot_tn(dsb, qh)
            for d in range(N_ATT_TILES):
                dq_new[d].append(dqg[d])
            dk_cols.append(dkg)
            dv_cols.append(dvg)
        x0, x1, x2 = (jnp.concatenate(c, axis=1) * scale for c in dq_new)
        dp_ref[:, 0:1024] = jnp.zeros((ATT_BLK, 1024), BF16)
        dp_ref[:, 1024:1536] = (acc1[...] + x0).astype(BF16)
        dp_ref[:, 1536:2048] = jnp.concatenate(dk_cols, axis=1).astype(BF16)
        dp_ref[:, 2048:2560] = jnp.concatenate(dv_cols, axis=1).astype(BF16)
        acc1[...] = acc2[...] + x1
        acc2[...] = x2

    def fwd_spec(d, width, col):
        return pl.BlockSpec((ATT_BLK, width), lambda j: (jnp.minimum(j + d, nb - 1), col))

    return _call(
        body, rider, name="bwd_attn", grid=(nb,),
        in_specs=[pl.BlockSpec((ATT_BLK, ATTN_W), lambda j: (j, 1)),
                  pl.BlockSpec((ATT_BLK, ATTN_W), lambda j: (j, 2)),
                  fwd_spec(0, ATTN_W, 0), fwd_spec(1, ATTN_W, 0), fwd_spec(2, ATTN_W, 0),
                  fwd_spec(0, ATTN_W, 0), fwd_spec(1, ATTN_W, 0), fwd_spec(2, ATTN_W, 0),
                  fwd_spec(0, ATTN_W, 0), fwd_spec(1, ATTN_W, 0), fwd_spec(2, ATTN_W, 0),
                  fwd_spec(0, LANES, 0), fwd_spec(1, LANES, 0), fwd_spec(2, LANES, 0),
                  pl.BlockSpec((N_ATT_TILES * N_HEADS, ATT_BLK, ATT_BLK), lambda j: (0, 0, 0))],
        out_specs=[pl.BlockSpec((ATT_BLK, IN_COLS), lambda j: (j, 0)),
                   pl.BlockSpec((N_ATT_TILES * N_HEADS, ATT_BLK, ATT_BLK), lambda j: (0, 0, 0))],
        out_shape=[jax.ShapeDtypeStruct((T, IN_COLS), BF16),
                   jax.ShapeDtypeStruct((N_ATT_TILES * N_HEADS, ATT_BLK, ATT_BLK), F32)],
        scratch_shapes=[pltpu.VMEM((ATT_BLK, ATTN_W), F32), pltpu.VMEM((ATT_BLK, ATTN_W), F32)],
        compiler_params=_params(56, 1),
    )(qkv, qkv, qkv, qkv, qkv, dao, dao, dao, ao, ao, ao, lse, lse, lse, bias)


def _bwd_conv(dproj, a, dco, hc, cw, lg, lb, rider=None):
    T = a.shape[0]
    tm = 512
    rc = 32
    ni = T // tm
    hb = tm // CONV_HALO

    def body(dp_in, a_ref, ap_ref, dco_ref, dcon_ref, hc_ref, hcn_ref, w_ref, lg_ref, lb_ref,
             dp_ref, dw_ref, db_ref, dlg_ref, dlb_ref, hext, dext, hsh, dsh, dwacc):
        del dp_in
        i = pl.program_id(0)

        def ln_bwd(dco_v, hc_v):
            mu = jnp.mean(hc_v, axis=-1, keepdims=True)
            xc = hc_v - mu
            rstd = lax.rsqrt(jnp.mean(xc * xc, axis=-1, keepdims=True) + EPS)
            xh = xc * rstd
            z = xh * lg_ref[...] + lb_ref[...]
            sg = _sigmoid(z)
            dz = dco_v * (sg * (1.0 + z * (1.0 - sg)))
            dxh = dz * lg_ref[...]
            dhc = rstd * (dxh - jnp.mean(dxh, axis=-1, keepdims=True)
                          - xh * jnp.mean(dxh * xh, axis=-1, keepdims=True))
            return dhc, dz * xh, dz

        hext[0:CONV_HALO, :] = jnp.where(i > 0, ap_ref[:, :CONV_W] * _sigmoid(ap_ref[:, CONV_W:]), 0.0)
        hext[CONV_HALO:CONV_HALO + tm, :] = a_ref[:, :CONV_W] * _sigmoid(a_ref[:, CONV_W:])
        dhc, dlg_rows, dlb_rows = ln_bwd(dco_ref[...], hc_ref[...])
        dext[0:tm, :] = dhc
        dhc_next, _, _ = ln_bwd(dcon_ref[...], hcn_ref[...])
        dext[tm:tm + CONV_HALO, :] = jnp.where(i < ni - 1, dhc_next, 0.0)

        @pl.when(i == 0)
        def _():
            dw_ref[...] = jnp.zeros(dw_ref.shape, F32)
            db_ref[...] = jnp.zeros(db_ref.shape, F32)
            dlg_ref[...] = jnp.zeros(dlg_ref.shape, F32)
            dlb_ref[...] = jnp.zeros(dlb_ref.shape, F32)

            dwacc[...] = jnp.zeros(dwacc.shape, F32)

        db_ref[...] += jnp.sum(dhc, axis=0, keepdims=True)
        dlg_ref[...] += jnp.sum(dlg_rows, axis=0, keepdims=True)
        dlb_ref[...] += jnp.sum(dlb_rows, axis=0, keepdims=True)
        _fill_shifted(hext, hsh, tm)
        _fill_shifted(dext, dsh, tm)
        for c in range(tm // rc):
            r0 = c * rc
            dh = jnp.zeros((rc, CONV_W), F32)
            dhc_c = dext[r0:r0 + rc, :]
            for k in range(CONV_K):
                dh = dh + w_ref[k:k + 1, :] * _shifted_rows(dext, dsh, r0 + 30 - k, rc)
                prod = dhc_c * _shifted_rows(hext, hsh, r0 + 2 + k, rc)
                dwacc[k] += jnp.sum(prod.reshape(rc // SUBLANES, SUBLANES, CONV_W), axis=0)
            av = a_ref[r0:r0 + rc, :CONV_W]
            sg = _sigmoid(a_ref[r0:r0 + rc, CONV_W:])
            dp_ref[r0:r0 + rc, 0:CONV_W] = (dh * sg).astype(BF16)
            dp_ref[r0:r0 + rc, CONV_W:] = (dh * av * sg * (1.0 - sg)).astype(BF16)

        @pl.when(i == ni - 1)
        def _():
            dw_ref[...] = jnp.sum(dwacc[...], axis=1)

    row = lambda w: pl.BlockSpec((tm, w), lambda i: (i, 0))
    prev = lambda w: pl.BlockSpec((CONV_HALO, w), lambda i: (jnp.maximum(i * hb - 1, 0), 0))
    nxt = lambda w: pl.BlockSpec((CONV_HALO, w), lambda i: (jnp.minimum((i + 1) * hb, ni * hb - 1), 0))
    vec = pl.BlockSpec((1, CONV_W), lambda i: (0, 0))
    return _call(
        body, rider, name="bwd_conv", grid=(ni,),
        in_specs=[ANY, row(1024), prev(1024), row(CONV_W), nxt(CONV_W), row(CONV_W), nxt(CONV_W),
                  pl.BlockSpec((CONV_HALO, CONV_W), lambda i: (0, 0)), vec, vec],
        out_specs=[pl.BlockSpec((tm, 1024), lambda i: (i, 0)),
                   pl.BlockSpec((CONV_HALO, CONV_W), lambda i: (0, 0)), vec, vec, vec],
        out_shape=[jax.ShapeDtypeStruct((T, IN_COLS), BF16),
                   jax.ShapeDtypeStruct((CONV_HALO, CONV_W), F32),
                   jax.ShapeDtypeStruct((1, CONV_W), F32),
                   jax.ShapeDtypeStruct((1, CONV_W), F32),
                   jax.ShapeDtypeStruct((1, CONV_W), F32)],
        scratch_shapes=[pltpu.VMEM((tm + CONV_HALO, CONV_W), F32), pltpu.VMEM((tm + CONV_HALO, CONV_W), F32),
                        pltpu.VMEM((SUBLANES - 1, tm + CONV_HALO - SUBLANES, CONV_W), F32),
                        pltpu.VMEM((SUBLANES - 1, tm + CONV_HALO - SUBLANES, CONV_W), F32),
                        pltpu.VMEM((CONV_HALO, SUBLANES, CONV_W), F32)],
        input_output_aliases={0: 0},
        compiler_params=_params(56, 1),
    )(dproj, a, a, dco, dco, hc, hc, cw, lg, lb)


def _bwd_in_proj(dproj, w_in, x, dh1, g1, rider=None):
    T = x.shape[0]
    tm = 512

    def body(dp_ref, w_ref, x_ref, dh_ref, g_ref, gx_ref, dg_ref):
        i = pl.program_id(0)
        du = _dot_nt(dp_ref[...], w_ref[...])
        dx, dg_rows = _rms_bwd(du, x_ref[...], g_ref[...])
        gx_ref[...] = dh_ref[...] + dx
        dg = jnp.sum(dg_rows, axis=0, keepdims=True)

        @pl.when(i == 0)
        def _():
            dg_ref[...] = dg

        @pl.when(i > 0)
        def _():
            dg_ref[...] += dg

    row = lambda w: pl.BlockSpec((tm, w), lambda i: (i, 0))
    vec = pl.BlockSpec((1, D_MODEL), lambda i: (0, 0))
    return _call(
        body, rider, name="bwd_in_proj", grid=(T // tm,),
        in_specs=[row(IN_COLS), pl.BlockSpec((D_MODEL, IN_COLS), lambda i: (0, 0)),
                  row(D_MODEL), row(D_MODEL), vec],
        out_specs=[row(D_MODEL), vec],
        out_shape=[jax.ShapeDtypeStruct((T, D_MODEL), F32), jax.ShapeDtypeStruct((1, D_MODEL), F32)],
        compiler_params=_params(40, 1),
    )(dproj, w_in, x, dh1, g1)


def _wgrad(name, a_list, a_spec, b, b_spec, out_spec, out_shape, n_outer, T, tk, select=None, rider=None):
    def body(*refs):
        a_refs, b_ref, o_ref = refs[:len(a_list)], refs[len(a_list)], refs[len(a_list) + 1]
        kt = pl.program_id(1)

        @pl.when(kt == 0)
        def _():
            o_ref[...] = jnp.zeros(o_ref.shape, F32)

        bv = b_ref[...].reshape(b_ref.shape[-2:])
        if select is None:
            o_ref[...] += _dot_tn(a_refs[0][...].reshape(a_refs[0].shape[-2:]), bv).reshape(o_ref.shape)
        else:
            for n, a_ref in enumerate(a_refs):
                @pl.when(select(pl.program_id(0)) == n)
                def _():
                    o_ref[...] += _dot_tn(a_ref[...], bv).reshape(o_ref.shape)

    (res,), got = _call(
        body, rider, name=name, grid=(n_outer, T // tk),
        in_specs=[a_spec] * len(a_list) + [b_spec],
        out_specs=[out_spec], out_shape=[out_shape],
        compiler_params=_params(56, 2),
    )(*a_list, b)
    return (res, got) if rider is not None else res


def _mesh_pos():
    return lax.axis_index("x"), lax.axis_index("y"), lax.axis_index("c")


def _other_chips(x, y):
    return [((1 - x, y), 2 * (1 - x) + y), ((x, 1 - y), 2 * x + (1 - y)), ((1 - x, 1 - y), 2 * (1 - x) + (1 - y))]


def _exchange_rider(operands, out_shape, aliases, sem_shape, pairs):
    def start(ins, outs, sems):
        for send, _ in pairs(ins, outs, *sems):
            send.start()

    def finish(ins, outs, sems):
        for send, recv in pairs(ins, outs, *sems):
            send.wait_send()
            recv.wait_recv()

    sems = [pltpu.SemaphoreType.DMA(sem_shape), pltpu.SemaphoreType.DMA(sem_shape)]
    return _Rider(list(operands), list(out_shape), aliases, sems, start, finish)


def _remote(src, dst, send_sem, recv_sem, device):
    return pltpu.make_async_remote_copy(src_ref=src, dst_ref=dst, send_sem=send_sem, recv_sem=recv_sem,
                                        device_id=device, device_id_type=MESH)


def _fetch_rider(bufs):
    def pairs(ins, outs, send_sems, recv_sems):
        x, y, c = _mesh_pos()
        res = []
        for t, buf in enumerate(bufs):
            rows = pl.ds(c * (buf.shape[1] // 2), buf.shape[1] // 2)
            mine = outs[t].at[2 * x + y, rows]
            for k, (chip, s) in enumerate(_other_chips(x, y)):
                landed = outs[t].at[s, rows]
                res.append((_remote(mine, mine, send_sems.at[t, k], recv_sems.at[t, k], (*chip, c)),
                            _remote(landed, landed, send_sems.at[t, k], recv_sems.at[t, k], (*chip, c))))
        return res

    shapes = [jax.ShapeDtypeStruct(b.shape, b.dtype) for b in bufs]
    return _exchange_rider(bufs, shapes, {t: t for t in range(len(bufs))}, (len(bufs), 3), pairs)


def _forward_rider(bufs):
    def pairs(ins, outs, send_sems, recv_sems):
        x, y, c = _mesh_pos()
        res = []
        for t, buf in enumerate(bufs):
            half = buf.shape[1] // 2
            for k, (_, s) in enumerate(_other_chips(x, y)):
                landed = outs[t].at[s, pl.ds(c * half, half)]
                theirs = outs[t].at[s, pl.ds((1 - c) * half, half)]
                res.append((_remote(landed, landed, send_sems.at[t, k], recv_sems.at[t, k], (x, y, 1 - c)),
                            _remote(theirs, theirs, send_sems.at[t, k], recv_sems.at[t, k], (x, y, 1 - c))))
        return res

    shapes = [jax.ShapeDtypeStruct(b.shape, b.dtype) for b in bufs]
    return _exchange_rider(bufs, shapes, {t: t for t in range(len(bufs))}, (len(bufs), 3), pairs)


def _pair_exchange_rider(grads):
    def pairs(ins, outs, send_sems, recv_sems):
        x, y, c = _mesh_pos()
        res = []
        for t, g in enumerate(grads):
            half = g.shape[1] // 2
            cp = _remote(ins[t].at[:, pl.ds((1 - c) * half, half), :], outs[t], send_sems.at[t], recv_sems.at[t],
                         (x, y, 1 - c))
            res.append((cp, cp))
        return res

    shapes = [jax.ShapeDtypeStruct((N_CHIPS, g.shape[1] // 2, g.shape[2]), F32) for g in grads]
    return _exchange_rider(grads, shapes, {}, (len(grads),), pairs)


def _chip_exchange_rider(sums):
    def pairs(ins, outs, send_sems, recv_sems):
        x, y, c = _mesh_pos()
        res = []
        for t in range(len(sums)):
            for k, (chip, s) in enumerate(_other_chips(x, y)):
                cp = _remote(ins[t].at[s], outs[t].at[k], send_sems.at[t, k], recv_sems.at[t, k], (*chip, c))
                res.append((cp, cp))
        return res

    shapes = [jax.ShapeDtypeStruct((3,) + p.shape[1:], p.dtype) for p in sums]
    return _exchange_rider(sums, shapes, {}, (len(sums), 3), pairs)


def _pair_gather_rider(fulls):
    def pairs(ins, outs, send_sems, recv_sems):
        x, y, c = _mesh_pos()
        res = []
        for t, f in enumerate(fulls):
            half = f.shape[0] // 2
            mine = outs[t].at[pl.ds(c * half, half)]
            theirs = outs[t].at[pl.ds((1 - c) * half, half)]
            res.append((_remote(mine, mine, send_sems.at[t], recv_sems.at[t], (x, y, 1 - c)),
                        _remote(theirs, theirs, send_sems.at[t], recv_sems.at[t], (x, y, 1 - c))))
        return res

    shapes = [jax.ShapeDtypeStruct(f.shape, F32) for f in fulls]
    return _exchange_rider(fulls, shapes, {t: t for t in range(len(fulls))}, (len(fulls),), pairs)


def _alone(name, rider):
    return _call(lambda: None, rider, name=name)()[1]


def _all_reduce_small(pack, rider=None):
    rows = pack.shape[0]

    def body(p_ref, o_ref, buf, send_sems, recv_sems):
        x, y, c = _mesh_pos()
        me = 4 * x + 2 * y + c
        buf[0] = p_ref[...]
        copies = []
        for k in range(1, 8):
            peer = (x ^ (k >> 2), y ^ ((k >> 1) & 1), c ^ (k & 1))
            cp = pltpu.make_async_remote_copy(
                src_ref=p_ref, dst_ref=buf.at[k], send_sem=send_sems.at[k - 1], recv_sem=recv_sems.at[k - 1],
                device_id=peer, device_id_type=MESH)
            cp.start()
            copies.append(cp)
        for cp in copies:
            cp.wait()
        total = buf[me]
        for dev in range(1, 8):
            total = total + buf[me ^ dev]
        o_ref[...] = total

    return _call(
        body, rider, name="all_reduce_small",
        in_specs=[VMEM_FULL], out_specs=[VMEM_FULL],
        out_shape=[jax.ShapeDtypeStruct(pack.shape, F32)],
        scratch_shapes=[pltpu.VMEM((8, rows, LANES), F32),
                        pltpu.SemaphoreType.DMA((7,)), pltpu.SemaphoreType.DMA((7,))],
    )(pack)


def _row_block(rows):
    if rows <= 512:
        return rows
    for rb in (256, 352):
        if rows % rb == 0:
            return rb
    raise ValueError(f"no row block for {rows} rows")


def _place(name, w, pos, dtype):
    R, C = w.shape
    rb = _row_block(R)

    def body(pos_ref, w_ref, o_ref):
        del pos_ref
        o_ref[0] = w_ref[...].astype(dtype)

    return pl.pallas_call(
        body, name=name,
        grid_spec=pltpu.PrefetchScalarGridSpec(
            num_scalar_prefetch=1, grid=(R // rb,),
            in_specs=[pl.BlockSpec((rb, C), lambda r, p: (r, 0))],
            out_specs=pl.BlockSpec((1, rb, C), lambda r, p: (p[0], r, 0))),
        out_shape=(pltpu.HBM if N_CHIPS * R * C * jnp.dtype(dtype).itemsize >= PIN_BYTES
                   else jax.ShapeDtypeStruct)((N_CHIPS, R, C), dtype),
        compiler_params=_params(32, 1),
    )(pos, w)


def _pair_sum(name, g, got, pos):
    S, R, C = g.shape
    half = R // 2
    rb = _row_block(half)
    nh = half // rb

    def body(pos_ref, a_ref, b_ref, o_ref):
        del pos_ref
        o_ref[...] = (a_ref[...] + b_ref[...]).astype(BF16)

    spec = pl.BlockSpec((1, rb, C), lambda s, r, p: (s, r, 0))
    return pl.pallas_call(
        body, name=name,
        grid_spec=pltpu.PrefetchScalarGridSpec(
            num_scalar_prefetch=1, grid=(S, nh),
            in_specs=[pl.BlockSpec((1, rb, C), lambda s, r, p: (s, p[1] * nh + r, 0)), spec],
            out_specs=spec),
        out_shape=jax.ShapeDtypeStruct((S, half, C), BF16), compiler_params=_params(32, 2),
    )(pos, g, got)


def _chip_sum(name, pairs, got, pos):
    _, half, C = pairs.shape
    rb = _row_block(half)
    nh = half // rb

    def body(pos_ref, a_ref, g_ref, o_ref):
        del pos_ref
        o_ref[...] = ((a_ref[0].astype(F32) + g_ref[0].astype(F32)) + g_ref[1].astype(F32)) + g_ref[2].astype(F32)

    return pl.pallas_call(
        body, name=name,
        grid_spec=pltpu.PrefetchScalarGridSpec(
            num_scalar_prefetch=1, grid=(nh,),
            in_specs=[pl.BlockSpec((1, rb, C), lambda r, p: (p[0], r, 0)),
                      pl.BlockSpec((3, rb, C), lambda r, p: (0, r, 0))],
            out_specs=pl.BlockSpec((rb, C), lambda r, p: (p[1] * nh + r, 0))),
        out_shape=jax.ShapeDtypeStruct((2 * half, C), F32), compiler_params=_params(32, 1),
    )(pos, pairs, got)


def _adamw(name, w, g, m, v):
    R, C = w.shape
    rb = _row_block(R)
    c1 = 1.0 - ADAM_B1 ** ADAM_STEP
    c2 = 1.0 - ADAM_B2 ** ADAM_STEP

    def body(w_ref, g_ref, m_ref, v_ref, d_ref, nm_ref, nv_ref):
        gv = g_ref[...]
        nm = ADAM_B1 * m_ref[...] + (1.0 - ADAM_B1) * gv
        nv = ADAM_B2 * v_ref[...] + (1.0 - ADAM_B2) * (gv * gv)
        nm_ref[...] = nm
        nv_ref[...] = nv
        d_ref[...] = -ADAM_LR * ((nm / c1) / (jnp.sqrt(nv / c2) + ADAM_EPS) + ADAM_WD * w_ref[...])

    spec = pl.BlockSpec((rb, C), lambda r: (r, 0))
    sds = jax.ShapeDtypeStruct(w.shape, F32)
    return pl.pallas_call(
        body, name=name, grid=(R // rb,), in_specs=[spec] * 4, out_specs=[spec] * 3,
        out_shape=[sds, sds, sds], compiler_params=_params(40, 1),
    )(w, g, m, v)


def _rel_index():
    m = np.arange(2 * ATT_BLK)
    off = np.where(m < ATT_BLK, m, m - 2 * ATT_BLK)
    rel = np.stack([ATT_BLK * d - off for d in range(N_ATT_TILES)])
    return np.clip(rel, -MAX_REL, MAX_REL) + MAX_REL


def _local_step(x, tgt, g1, w_in, cw, cb, lg, lb, bias, w_out, g2, g3, w_up, fw, fb, w_down, g4, pos=None):
    T = x.shape[0]
    dist = pos is not None
    idx = _rel_index()

    (u, a, qkv), got = _fwd_in_proj(x, g1, w_in, _fetch_rider([w_out, w_down]) if dist else None)
    if dist:
        w_out, w_down = got
    (co, hc), got = _fwd_conv(a, cw, cb, lg, lb, _merge_riders(_forward_rider([w_out, w_down]),
                                                               _fetch_rider([w_up])) if dist else None)
    if dist:
        w_out, w_down, w_up = got
    (ao, lse), got = _fwd_attn(qkv, bias, _forward_rider([w_up]) if dist else None)
    if dist:
        (w_up,) = got
        w_out, w_down = w_out.reshape(D_MODEL, D_MODEL), w_down.reshape(D_FF, D_MODEL)
    mixed, h1, u2 = _fwd_out_proj(co, ao, w_out, x, g2, g3)
    w_cat = _pair_up_weights(w_up)
    hf, pre, act, fp = _fwd_ffn(u2, w_cat, fw, fb, w_down)
    loss, dy, df, dg4 = _fwd_loss(fp, h1, tgt, g4)

    tk = min(2048, T)
    du2p, dhf, dfw_g, dfw_v = _bwd_ffn(df, hf, pre, w_cat, fw, w_down)
    gw_up = _wgrad(
        "wgrad_up", [u2], pl.BlockSpec((tk, D_MODEL), lambda s, k: (k, 0)),
        dhf, pl.BlockSpec((1, tk, FF_SHARD), lambda s, k: (s // 2, k, s % 2)),
        pl.BlockSpec((1, D_MODEL, FF_SHARD), lambda s, k: (s, 0, 0)),
        jax.ShapeDtypeStruct((N_CHIPS, D_MODEL, FF_SHARD), F32), N_CHIPS, T, tk)
    gw_down = _wgrad(
        "wgrad_down", [act], pl.BlockSpec((tk, FF_SHARD), lambda s, k: (k, s)),
        df, pl.BlockSpec((tk, D_MODEL), lambda s, k: (k, 0)),
        pl.BlockSpec((FF_SHARD, D_MODEL), lambda s, k: (s, 0)),
        jax.ShapeDtypeStruct((D_FF, D_MODEL), F32), 2, T, tk).reshape(N_CHIPS, D_FF // N_CHIPS, D_MODEL)
    (dh1, dmx, dco, dao, dg3, dg2), _ = _bwd_mid(du2p, dy, h1, mixed, g3, g2, w_out)
    gw_out = _wgrad(
        "wgrad_out", [co, ao], pl.BlockSpec((tk, CONV_W), lambda s, k: (k, 0)),
        dmx, pl.BlockSpec((tk, D_MODEL), lambda s, k: (k, 0)),
        pl.BlockSpec((CONV_W, D_MODEL), lambda s, k: (s, 0)),
        jax.ShapeDtypeStruct((D_MODEL, D_MODEL), F32), 2, T, tk,
        select=lambda s: s, rider=_pair_exchange_rider([gw_up, gw_down]) if dist else None)
    if dist:
        gw_out, got = gw_out
        p_up = _pair_sum("pair_sum_w_up", gw_up, got[0], pos)
        p_down = _pair_sum("pair_sum_w_down", gw_down, got[1], pos)
    gw_out = gw_out.reshape(N_CHIPS, D_MODEL // N_CHIPS, D_MODEL)
    (dproj, dsacc), got = _bwd_attn(
        qkv, ao, dao, lse, bias,
        _merge_riders(_chip_exchange_rider([p_up, p_down]), _pair_exchange_rider([gw_out])) if dist else None)
    if dist:
        gw_up = _chip_sum("chip_sum_w_up", p_up, got[0], pos)
        gw_down = _chip_sum("chip_sum_w_down", p_down, got[1], pos)
        p_out = _pair_sum("pair_sum_w_out", gw_out, got[2], pos)
    (dproj, dcw, dcb, dlg, dlb), got = _bwd_conv(
        dproj, a, dco, hc, cw, lg, lb,
        _merge_riders(_pair_gather_rider([gw_up, gw_down]), _chip_exchange_rider([p_out])) if dist else None)
    if dist:
        gw_up, gw_down = got[:2]
        gw_out = _chip_sum("chip_sum_w_out", p_out, got[2], pos)
    gw_in = _wgrad(
        "wgrad_in", [u], pl.BlockSpec((tk, D_MODEL), lambda s, k: (k, 0)),
        dproj, pl.BlockSpec((tk, IN_SHARD), lambda s, k: (k, s)),
        pl.BlockSpec((1, D_MODEL, IN_SHARD), lambda s, k: (s, 0, 0)),
        jax.ShapeDtypeStruct((N_CHIPS, D_MODEL, IN_SHARD), F32), N_CHIPS, T, tk)
    if dist:
        got = _alone("pair_exchange_w_in", _merge_riders(_pair_exchange_rider([gw_in]), _pair_gather_rider([gw_out])))
        p_in, gw_out = _pair_sum("pair_sum_w_in", gw_in, got[0], pos), got[1]
    (gx, dg1), _ = _bwd_in_proj(dproj, w_in, x, dh1, g1)
    (diag,), got = _diag_sums(dsacc, _chip_exchange_rider([p_in]) if dist else None)
    if dist:
        gw_in = _chip_sum("chip_sum_w_in", p_in, got[0], pos)

    diag = diag.reshape(N_ATT_TILES, N_HEADS, 2 * ATT_BLK)
    onehot = np.zeros((N_ATT_TILES, 2 * ATT_BLK, 2 * MAX_REL + 1), np.float32)
    for d in range(N_ATT_TILES):
        onehot[d, np.arange(2 * ATT_BLK), idx[d]] = 1.0
    drel = jnp.einsum("dhm,dmr->hr", diag, jnp.asarray(onehot), precision=lax.Precision.HIGHEST)

    small = dict(norm_mix_pre=dg1, conv_dw_w=dcw[:CONV_K], conv_dw_b=dcb, conv_ln_g=dlg, conv_ln_b=dlb,
                 rel_bias=drel, norm_mix_post=dg2, norm_ffn_pre=dg3,
                 ffn_dw_w=jnp.concatenate([dfw_g[0, :3], dfw_g[1, :3], dfw_v[0, :3], dfw_v[1, :3]], axis=1),
                 ffn_dw_b=jnp.concatenate([dfw_g[0, 3:4], dfw_g[1, 3:4], dfw_v[0, 3:4], dfw_v[1, 3:4]], axis=1),
                 norm_ffn_post=dg4)
    return loss, gx, small, dict(w_in=gw_in, w_out=gw_out, w_up=gw_up, w_down=gw_down)


SMALL_ORDER = ["norm_mix_pre", "conv_dw_b", "conv_ln_g", "conv_ln_b", "rel_bias", "norm_mix_post",
               "norm_ffn_pre", "ffn_dw_b", "norm_ffn_post", "conv_dw_w", "ffn_dw_w"]


def _pack(parts):
    rows = []
    for p in parts:
        width = -(-p.shape[1] // LANES) * LANES
        rows.append(jnp.pad(p, ((0, 0), (0, width - p.shape[1]))).reshape(-1, LANES))
    packed = jnp.concatenate(rows, axis=0)
    pad = -packed.shape[0] % 8
    return jnp.pad(packed, ((0, pad), (0, 0)))


def _unpack(packed, shapes):
    out, r = [], 0
    for shp in shapes:
        width = -(-shp[1] // LANES) * LANES
        n = shp[0] * width // LANES
        out.append(packed[r:r + n].reshape(shp[0], width)[:, :shp[1]])
        r += n
    return out


WEIGHTS = ["norm_mix_pre", "w_in", "conv_dw_w", "conv_dw_b", "conv_ln_g", "conv_ln_b", "rel_bias", "w_out",
           "norm_mix_post", "norm_ffn_pre", "w_up", "ffn_dw_w", "ffn_dw_b", "w_down", "norm_ffn_post"]
BIG = ["w_in", "w_out", "w_up", "w_down"]


def kernel(x, norm_mix_pre, w_in, conv_dw_w, conv_dw_b, conv_ln_g, conv_ln_b, rel_bias, w_out, norm_mix_post, norm_ffn_pre, w_up, ffn_dw_w, ffn_dw_b, w_down, norm_ffn_post, loss_target, m_norm_mix_pre, m_w_in, m_conv_dw_w, m_conv_dw_b, m_conv_ln_g, m_conv_ln_b, m_rel_bias, m_w_out, m_norm_mix_post, m_norm_ffn_pre, m_w_up, m_ffn_dw_w, m_ffn_dw_b, m_w_down, m_norm_ffn_post, v_norm_mix_pre, v_w_in, v_conv_dw_w, v_conv_dw_b, v_conv_ln_g, v_conv_ln_b, v_rel_bias, v_w_out, v_norm_mix_post, v_norm_ffn_pre, v_w_up, v_ffn_dw_w, v_ffn_dw_b, v_w_down, v_norm_ffn_post):
    args = locals()
    w = {n: args[n][0] for n in WEIGHTS}
    m = {n: args["m_" + n][0] for n in WEIGHTS}
    v = {n: args["v_" + n][0] for n in WEIGHTS}
    for d in (w, m, v):
        d["rel_bias"] = d["rel_bias"].reshape(N_HEADS, 2 * MAX_REL + 1)
        for n in ("norm_mix_pre", "conv_dw_b", "conv_ln_g", "conv_ln_b", "norm_mix_post", "norm_ffn_pre",
                  "ffn_dw_b", "norm_ffn_post"):
            d[n] = d[n].reshape(1, -1)
    shard = 2 * lax.axis_index("x") + lax.axis_index("y")

    cw_sh = jnp.pad(w["conv_dw_w"], ((0, CONV_HALO - CONV_K), (0, 0)))
    fw_sh = jnp.pad(w["ffn_dw_w"], ((0, FF_HALO - 3), (0, 0)))
    pos = jnp.stack([shard, lax.axis_index("c")]).astype(jnp.int32)
    bufs = {n: _place("place_" + n, w[n], pos, BF16) for n in BIG}
    first = [bufs["w_in"], _place("place_conv_dw_w", cw_sh, pos, F32), _place("place_ffn_dw_w", fw_sh, pos, F32)]
    (bias,), first = _bias_tiles(w["rel_bias"], _fetch_rider(first))
    w_in_f, cw_f, fw_f = _alone("all_gather_forward", _forward_rider(list(first)))
    cw_full = jnp.transpose(cw_f, (1, 0, 2)).reshape(CONV_HALO, CONV_W)

    loss, gx, small, big = _local_step(
        x[0], loss_target[0], w["norm_mix_pre"], _join_columns(w_in_f), cw_full, w["conv_dw_b"], w["conv_ln_g"],
        w["conv_ln_b"], bias, bufs["w_out"], w["norm_mix_post"],
        w["norm_ffn_pre"], bufs["w_up"], fw_f, w["ffn_dw_b"].reshape(N_CHIPS, 1, FF_SHARD),
        bufs["w_down"], w["norm_ffn_post"], pos)
    (gsum,), (big["w_in"],) = _all_reduce_small(_pack([small[n] for n in SMALL_ORDER] + [loss]),
                                                _pair_gather_rider([big["w_in"]]))

    grads, deltas, new_m, new_v = {}, {}, {}, {}
    for n in BIG:
        grads[n] = big[n]
        deltas[n], new_m[n], new_v[n] = _adamw("adamw_" + n, w[n], big[n], m[n], v[n])
    shapes = [small[n].shape for n in SMALL_ORDER]
    *reduced, total = _unpack(gsum, shapes + [loss.shape])
    gs = dict(zip(SMALL_ORDER, reduced))
    gs["conv_dw_w"] = lax.dynamic_slice_in_dim(gs["conv_dw_w"], shard * LANES, LANES, axis=1)
    gs["ffn_dw_w"] = lax.dynamic_slice_in_dim(gs["ffn_dw_w"], shard * FF_SHARD, FF_SHARD, axis=1)
    shapes = [gs[n].shape for n in SMALL_ORDER]
    d_p, m_p, v_p = _adamw("adamw_small", _pack([w[n] for n in SMALL_ORDER]), _pack([gs[n] for n in SMALL_ORDER]),
                           _pack([m[n] for n in SMALL_ORDER]), _pack([v[n] for n in SMALL_ORDER]))
    for dst, packed in ((deltas, d_p), (new_m, m_p), (new_v, v_p)):
        dst.update(zip(SMALL_ORDER, _unpack(packed, shapes)))
    grads.update(gs)

    outs = [total[0, 0], gx[None]]
    for group in (grads, deltas, new_m, new_v):
        outs += [group[n].reshape(args[n].shape) for n in WEIGHTS]
    return tuple(outs)
```

```python
import functools
import math
from typing import Callable, NamedTuple

import numpy as np
import jax
import jax.numpy as jnp
from jax import lax
from jax.experimental import pallas as pl
from jax.experimental.pallas import tpu as pltpu

F32 = jnp.float32
BF16 = jnp.bfloat16

D_MODEL = 1024
CONV_W = 512
ATTN_W = 512
N_HEADS = 8
HEAD_DIM = 64
CHUNK = 64
N_LEFT = 8
MAX_REL = 128
CONV_K = 31
CONV_HALO = 32
D_FF = 2816
FF_SHARD = 1408
IN_COLS = 2560
IN_SHARD = 640
EPS = 1e-6
NEG_INF = -1e30
ATT_BLK = 256
N_ATT_TILES = 3
LANES = 128
SUBLANES = 8
N_CHIPS = 4

ADAM_LR = 0.001
ADAM_B1 = 0.9
ADAM_B2 = 0.999
ADAM_EPS = 1e-08
ADAM_WD = 0.01
ADAM_STEP = 10

MESH = pl.DeviceIdType.MESH
ANY = pl.BlockSpec(memory_space=pl.ANY)
VMEM_FULL = pl.BlockSpec(memory_space=pltpu.VMEM)


def _params(vmem_mb, n_grid=0):
    sem = ("arbitrary",) * n_grid if n_grid else None
    return pltpu.CompilerParams(dimension_semantics=sem, vmem_limit_bytes=vmem_mb << 20)


class _Rider(NamedTuple):
    operands: list
    out_shape: list
    aliases: dict
    sems: list
    start: Callable
    finish: Callable


def _merge_riders(a, b):
    ia, oa, sa = len(a.operands), len(a.out_shape), len(a.sems)

    def start(ins, outs, sems):
        a.start(ins[:ia], outs[:oa], sems[:sa])
        b.start(ins[ia:], outs[oa:], sems[sa:])

    def finish(ins, outs, sems):
        a.finish(ins[:ia], outs[:oa], sems[:sa])
        b.finish(ins[ia:], outs[oa:], sems[sa:])

    aliases = {**a.aliases, **{k + ia: v + oa for k, v in b.aliases.items()}}
    return _Rider(a.operands + b.operands, a.out_shape + b.out_shape, aliases, a.sems + b.sems, start, finish)


PIN_BYTES = 1 << 20


def _big(a):
    return math.prod(a.shape) * jnp.dtype(a.dtype).itemsize >= PIN_BYTES


def _pin_args(args):
    return [pltpu.with_memory_space_constraint(a, pltpu.HBM) if _big(a) else a for a in args]


def _call(body, rider, *, grid=(), in_specs=(), out_specs=(), out_shape=(), scratch_shapes=(),
          input_output_aliases=None, **kwargs):
    in_specs, out_specs = list(in_specs), list(out_specs)
    pin_out = lambda shapes: [pltpu.HBM(s.shape, s.dtype) if _big(s) else s for s in shapes]
    out_shape = pin_out(out_shape)
    scratch, aliases = list(scratch_shapes), dict(input_output_aliases or {})
    if rider is None:
        plain = pl.pallas_call(body, grid=grid, in_specs=in_specs, out_specs=out_specs, out_shape=out_shape,
                               scratch_shapes=scratch, input_output_aliases=aliases, **kwargs)
        return lambda *args: (plain(*_pin_args(args)), [])
    n_in, n_out, n_scr = len(in_specs), len(out_specs), len(scratch)
    r_in, r_out = len(rider.operands), len(rider.out_shape)

    def carried(*refs):
        ins, r_ins, refs = refs[:n_in], refs[n_in:n_in + r_in], refs[n_in + r_in:]
        outs, r_outs, refs = refs[:n_out], refs[n_out:n_out + r_out], refs[n_out + r_out:]
        scr, r_sems = refs[:n_scr], refs[n_scr:]
        if not grid:
            rider.start(r_ins, r_outs, r_sems)
            body(*ins, *outs, *scr)
            rider.finish(r_ins, r_outs, r_sems)
            return
        at = [pl.program_id(d) for d in range(len(grid))]
        first = functools.reduce(jnp.logical_and, [p == 0 for p in at])
        last = functools.reduce(jnp.logical_and, [p == n - 1 for p, n in zip(at, grid)])

        @pl.when(first)
        def _():
            rider.start(r_ins, r_outs, r_sems)

        body(*ins, *outs, *scr)

        @pl.when(last)
        def _():
            rider.finish(r_ins, r_outs, r_sems)

    aliases.update({n_in + k: n_out + v for k, v in rider.aliases.items()})
    both = pl.pallas_call(carried, grid=grid, in_specs=in_specs + [ANY] * r_in, out_specs=out_specs + [ANY] * r_out,
                          out_shape=out_shape + pin_out(rider.out_shape), scratch_shapes=scratch + rider.sems,
                          input_output_aliases=aliases, **kwargs)

    def run(*args):
        res = both(*_pin_args(args), *rider.operands)
        return res[:n_out], res[n_out:]

    return run


def _sigmoid(v):
    return 1.0 / (1.0 + jnp.exp(-v))


def _dot(a, b):
    return jnp.dot(a, b, preferred_element_type=F32)


def _dot_nt(a, b):
    return lax.dot_general(a, b, (((1,), (1,)), ((), ())), preferred_element_type=F32)


def _dot_tn(a, b):
    return lax.dot_general(a, b, (((0,), (0,)), ((), ())), preferred_element_type=F32)


def _rms_fwd(v, g):
    r = lax.rsqrt(jnp.mean(v * v, axis=-1, keepdims=True) + EPS)
    return v * r * g, r


def _rms_bwd(dy, v, g):
    r = lax.rsqrt(jnp.mean(v * v, axis=-1, keepdims=True) + EPS)
    vh = v * r
    dvh = dy * g
    dv = r * (dvh - vh * jnp.mean(dvh * vh, axis=-1, keepdims=True))
    return dv, dy * vh


def _join_columns(w):
    S, R, C = w.shape
    rb = 256

    def body(w_ref, o_ref):
        for s in range(S):
            o_ref[:, s * C:(s + 1) * C] = w_ref[s]

    return pl.pallas_call(
        body, name="join_columns", grid=(R // rb,),
        in_specs=[pl.BlockSpec((S, rb, C), lambda r: (0, r, 0))],
        out_specs=pl.BlockSpec((rb, S * C), lambda r: (r, 0)),
        out_shape=jax.ShapeDtypeStruct((R, S * C), w.dtype),
        compiler_params=_params(32, 1),
    )(w)


GLU_COLS = 2 * CONV_W


def _fwd_in_proj(x, g1, w_in, rider=None):
    T = x.shape[0]
    tm = 512

    def body(x_ref, g_ref, w_ref, u_ref, a_ref, qkv_ref):
        u, _ = _rms_fwd(x_ref[...], g_ref[...])
        u = u.astype(BF16)
        u_ref[...] = u
        a_ref[...] = _dot(u, w_ref[:, :GLU_COLS])
        qkv_ref[...] = _dot(u, w_ref[:, GLU_COLS:]).astype(BF16)

    return _call(
        body, rider, name="fwd_in_proj", grid=(T // tm,),
        in_specs=[pl.BlockSpec((tm, D_MODEL), lambda i: (i, 0)),
                  pl.BlockSpec((1, D_MODEL), lambda i: (0, 0)),
                  pl.BlockSpec((D_MODEL, IN_COLS), lambda i: (0, 0))],
        out_specs=[pl.BlockSpec((tm, D_MODEL), lambda i: (i, 0)),
                   pl.BlockSpec((tm, 1024), lambda i: (i, 0)),
                   pl.BlockSpec((tm, 1536), lambda i: (i, 0))],
        out_shape=[jax.ShapeDtypeStruct((T, D_MODEL), BF16),
                   jax.ShapeDtypeStruct((T, 1024), F32),
                   jax.ShapeDtypeStruct((T, 1536), BF16)],
        compiler_params=_params(40, 1),
    )(x, g1, w_in)


def _fill_shifted(ext, shifted, tm):
    n = tm + CONV_HALO - SUBLANES
    for j in range(1, SUBLANES):
        shifted[j - 1] = ext[j:j + n, :]


def _shifted_rows(ext, shifted, start, rows):
    j = start % SUBLANES
    if j == 0:
        return ext[start:start + rows, :]
    return shifted[j - 1, start - j:start - j + rows, :]


def _fwd_conv(a, cw, cb, lg, lb, rider=None):
    T = a.shape[0]
    tm = 512
    rc = 64

    def body(a_ref, w_ref, b_ref, lg_ref, lb_ref, co_ref, hc_ref, hext, hsh):
        i = pl.program_id(0)

        @pl.when(i == 0)
        def _():
            hext[0:CONV_HALO, :] = jnp.zeros((CONV_HALO, CONV_W), F32)

        @pl.when(i > 0)
        def _():
            hext[0:CONV_HALO, :] = hext[tm:tm + CONV_HALO, :]

        hext[CONV_HALO:CONV_HALO + tm, :] = a_ref[:, :CONV_W] * _sigmoid(a_ref[:, CONV_W:])
        _fill_shifted(hext, hsh, tm)
        for c in range(tm // rc):
            acc = jnp.zeros((rc, CONV_W), F32)
            for k in range(CONV_K):
                acc = acc + w_ref[k:k + 1, :] * _shifted_rows(hext, hsh, c * rc + 2 + k, rc)
            hc = acc + b_ref[...]
            hc_ref[c * rc:(c + 1) * rc, :] = hc
            mu = jnp.mean(hc, axis=-1, keepdims=True)
            xc = hc - mu
            var = jnp.mean(xc * xc, axis=-1, keepdims=True)
            z = xc * lax.rsqrt(var + EPS) * lg_ref[...] + lb_ref[...]
            co_ref[c * rc:(c + 1) * rc, :] = (z * _sigmoid(z)).astype(BF16)

    return _call(
        body, rider, name="fwd_conv", grid=(T // tm,),
        in_specs=[pl.BlockSpec((tm, 1024), lambda i: (i, 0)),
                  pl.BlockSpec((CONV_HALO, CONV_W), lambda i: (0, 0)),
                  pl.BlockSpec((1, CONV_W), lambda i: (0, 0)),
                  pl.BlockSpec((1, CONV_W), lambda i: (0, 0)),
                  pl.BlockSpec((1, CONV_W), lambda i: (0, 0))],
        out_specs=[pl.BlockSpec((tm, CONV_W), lambda i: (i, 0)),
                   pl.BlockSpec((tm, CONV_W), lambda i: (i, 0))],
        out_shape=[jax.ShapeDtypeStruct((T, CONV_W), BF16),
                   jax.ShapeDtypeStruct((T, CONV_W), F32)],
        scratch_shapes=[pltpu.VMEM((tm + CONV_HALO, CONV_W), F32),
                        pltpu.VMEM((SUBLANES - 1, tm + CONV_HALO - SUBLANES, CONV_W), F32)],
        compiler_params=_params(40, 1),
    )(a, cw, cb, lg, lb)


def _row_skew(v, sign):
    rows, width = v.shape
    row = lax.broadcasted_iota(jnp.int32, (rows, 1), 0)
    for b in range(int(math.log2(rows))):
        shift = (1 << b) if sign > 0 else width - (1 << b)
        v = jnp.where(((row >> b) & 1) == 1, pltpu.roll(v, shift, 1), v)
    return v


def _att_visible(d):
    rq = lax.broadcasted_iota(jnp.int32, (ATT_BLK, ATT_BLK), 0) // CHUNK
    ck = lax.broadcasted_iota(jnp.int32, (ATT_BLK, ATT_BLK), 1) // CHUNK
    slack = ATT_BLK
    above = jnp.where(d == 0, 0, slack)
    below = jnp.where(d == 2, 0, slack)
    return (ck <= rq + above) & (ck >= rq - below)


def _bias_tiles(rel, rider=None):
    vec = jnp.transpose(rel[:, _rel_index()], (1, 0, 2)).reshape(N_ATT_TILES * N_HEADS, 1, 2 * ATT_BLK)

    def body(v_ref, o_ref):
        visible = _att_visible(pl.program_id(0))
        for h in range(N_HEADS):
            full = _row_skew(jnp.broadcast_to(v_ref[h], (ATT_BLK, 2 * ATT_BLK)), 1)
            o_ref[h] = jnp.where(visible, full[:, :ATT_BLK], NEG_INF)

    return _call(
        body, rider, name="bias_tiles", grid=(N_ATT_TILES,),
        in_specs=[pl.BlockSpec((N_HEADS, 1, 2 * ATT_BLK), lambda d: (d, 0, 0))],
        out_specs=[pl.BlockSpec((N_HEADS, ATT_BLK, ATT_BLK), lambda d: (d, 0, 0))],
        out_shape=[jax.ShapeDtypeStruct((N_ATT_TILES * N_HEADS, ATT_BLK, ATT_BLK), F32)],
        compiler_params=_params(32, 1),
    )(vec)


def _diag_sums(ds, rider=None):
    def body(d_ref, o_ref):
        wide = jnp.concatenate([d_ref[0], jnp.zeros((ATT_BLK, ATT_BLK), F32)], axis=1)
        o_ref[0] = jnp.sum(_row_skew(wide, -1), axis=0, keepdims=True)

    return _call(
        body, rider, name="diag_sums", grid=(N_ATT_TILES * N_HEADS,),
        in_specs=[pl.BlockSpec((1, ATT_BLK, ATT_BLK), lambda n: (n, 0, 0))],
        out_specs=[pl.BlockSpec((1, 1, 2 * ATT_BLK), lambda n: (n, 0, 0))],
        out_shape=[jax.ShapeDtypeStruct((N_ATT_TILES * N_HEADS, 1, 2 * ATT_BLK), F32)],
        compiler_params=_params(16, 1),
    )(ds)


def _head_mask(h):
    lane = lax.broadcasted_iota(jnp.int32, (1, LANES), 1)
    return (lane // HEAD_DIM) == (h % 2)


def _fwd_attn(qkv, bias, rider=None):
    T = qkv.shape[0]
    nb = T // ATT_BLK
    scale = HEAD_DIM ** -0.5

    def body(q_ref, k0_ref, k1_ref, k2_ref, v0_ref, v1_ref, v2_ref, b_ref, o_ref, lse_ref):
        i = pl.program_id(0)

        @pl.when(i >= N_ATT_TILES - 1)
        def _():
            block(i, False, q_ref, k0_ref, k1_ref, k2_ref, v0_ref, v1_ref, v2_ref, b_ref, o_ref, lse_ref)

        @pl.when(i < N_ATT_TILES - 1)
        def _():
            block(i, True, q_ref, k0_ref, k1_ref, k2_ref, v0_ref, v1_ref, v2_ref, b_ref, o_ref, lse_ref)

    def block(i, hide_absent, q_ref, k0_ref, k1_ref, k2_ref, v0_ref, v1_ref, v2_ref, b_ref, o_ref, lse_ref):
        k_refs = (k0_ref, k1_ref, k2_ref)
        v_refs = (v0_ref, v1_ref, v2_ref)
        lane = lax.broadcasted_iota(jnp.int32, (1, LANES), 1)
        lse_tile = jnp.zeros((ATT_BLK, LANES), F32)
        for g in range(N_HEADS // 2):
            cols = slice(g * LANES, (g + 1) * LANES)
            qg = q_ref[:, cols] * scale
            og = jnp.zeros((ATT_BLK, LANES), F32)
            for h in (2 * g, 2 * g + 1):
                hm = _head_mask(h)
                qh = jnp.where(hm, qg, jnp.zeros_like(qg))
                s = []
                for d in range(N_ATT_TILES):
                    sd = _dot_nt(qh, k_refs[d][:, cols]) + b_ref[d * N_HEADS + h]
                    if d > 0 and hide_absent:
                        sd = jnp.where(i >= d, sd, NEG_INF)
                    s.append(sd)
                m = jnp.maximum(jnp.maximum(jnp.max(s[0], axis=-1, keepdims=True),
                                            jnp.max(s[1], axis=-1, keepdims=True)),
                                jnp.max(s[2], axis=-1, keepdims=True))
                p = [jnp.exp(sd - m) for sd in s]
                l = (jnp.sum(p[0], axis=-1, keepdims=True) + jnp.sum(p[1], axis=-1, keepdims=True)
                     + jnp.sum(p[2], axis=-1, keepdims=True))
                oh = jnp.zeros((ATT_BLK, LANES), F32)
                for d in range(N_ATT_TILES):
                    vg = v_refs[d][:, cols]
                    oh = oh + _dot(p[d].astype(BF16), jnp.where(hm, vg, jnp.zeros_like(vg)))
                og = og + oh / l
                lse_tile = jnp.where(lane == h, m + jnp.log(l), lse_tile)
            o_ref[:, cols] = og.astype(BF16)
        lse_ref[...] = lse_tile

    def kv_spec(d, col):
        return pl.BlockSpec((ATT_BLK, ATTN_W), lambda i: (jnp.maximum(i - d, 0), col))

    return _call(
        body, rider, name="fwd_attn", grid=(nb,),
        in_specs=[pl.BlockSpec((ATT_BLK, ATTN_W), lambda i: (i, 0)),
                  kv_spec(0, 1), kv_spec(1, 1), kv_spec(2, 1),
                  kv_spec(0, 2), kv_spec(1, 2), kv_spec(2, 2),
                  pl.BlockSpec((N_ATT_TILES * N_HEADS, ATT_BLK, ATT_BLK), lambda i: (0, 0, 0))],
        out_specs=[pl.BlockSpec((ATT_BLK, ATTN_W), lambda i: (i, 0)),
                   pl.BlockSpec((ATT_BLK, LANES), lambda i: (i, 0))],
        out_shape=[jax.ShapeDtypeStruct((T, ATTN_W), BF16),
                   jax.ShapeDtypeStruct((T, LANES), F32)],
        compiler_params=_params(40, 1),
    )(qkv, qkv, qkv, qkv, qkv, qkv, qkv, bias)


def _fwd_out_proj(co, ao, w_out, x, g2, g3):
    T = x.shape[0]
    tm = 512

    def body(co_ref, ao_ref, w_ref, x_ref, g2_ref, g3_ref, mixed_ref, h1_ref, u2_ref):
        mixed = _dot(co_ref[...], w_ref[0:CONV_W, :]) + _dot(ao_ref[...], w_ref[CONV_W:, :])
        mixed_ref[...] = mixed.astype(BF16)
        y, _ = _rms_fwd(mixed, g2_ref[...])
        h1 = x_ref[...] + y
        h1_ref[...] = h1
        u2, _ = _rms_fwd(h1, g3_ref[...])
        u2_ref[...] = u2.astype(BF16)

    row = lambda w: pl.BlockSpec((tm, w), lambda i: (i, 0))
    vec = pl.BlockSpec((1, D_MODEL), lambda i: (0, 0))
    return _call(
        body, None, name="fwd_out_proj", grid=(T // tm,),
        in_specs=[row(CONV_W), row(ATTN_W), pl.BlockSpec((D_MODEL, D_MODEL), lambda i: (0, 0)),
                  row(D_MODEL), vec, vec],
        out_specs=[row(D_MODEL), row(D_MODEL), row(D_MODEL)],
        out_shape=[jax.ShapeDtypeStruct((T, D_MODEL), BF16),
                   jax.ShapeDtypeStruct((T, D_MODEL), F32),
                   jax.ShapeDtypeStruct((T, D_MODEL), BF16)],
        compiler_params=_params(40, 1),
    )(co, ao, w_out, x, g2, g3)[0]


GELU_C = math.sqrt(2.0 / math.pi)
GELU_A = 0.044715


def _gelu_and_grad(v):
    sq = v * v
    th = jnp.tanh(v * (GELU_C + (GELU_C * GELU_A) * sq))
    half = 0.5 + 0.5 * th
    gl = v * half
    dgl = half + (v * (half * (1.0 - th))) * (GELU_C + (3.0 * GELU_C * GELU_A) * sq)
    return gl, dgl


FF_TM = 256
FF_HALO = 16
FF_CHUNKS = [(lo, min(lo + 256, FF_SHARD)) for lo in range(0, FF_SHARD, 256)]


def _rows_before(prev, cur):
    ext = jnp.concatenate([prev, cur], axis=0)
    return pltpu.roll(ext, 1, 0)[SUBLANES:], pltpu.roll(ext, 2, 0)[SUBLANES:]


def _rows_after(cur, nxt):
    ext = jnp.concatenate([cur, nxt], axis=0)
    n = ext.shape[0]
    return pltpu.roll(ext, n - 1, 0)[:cur.shape[0]], pltpu.roll(ext, n - 2, 0)[:cur.shape[0]]


def _pair_up_weights(w_up):
    rb = 256

    def body(g_ref, v_ref, o_ref):
        for lo, hi in FF_CHUNKS:
            o_ref[0, :, 2 * lo:lo + hi] = g_ref[0, :, lo:hi]
            o_ref[0, :, lo + hi:2 * hi] = v_ref[0, :, lo:hi]

    return pl.pallas_call(
        body, name="pair_up_weights", grid=(2, D_MODEL // rb),
        in_specs=[pl.BlockSpec((1, rb, FF_SHARD), lambda s, r: (s, r, 0)),
                  pl.BlockSpec((1, rb, FF_SHARD), lambda s, r: (s + 2, r, 0))],
        out_specs=pl.BlockSpec((1, rb, 2 * FF_SHARD), lambda s, r: (s, r, 0)),
        out_shape=jax.ShapeDtypeStruct((2, D_MODEL, 2 * FF_SHARD), w_up.dtype),
        compiler_params=_params(32, 2),
    )(w_up, w_up)


def _fwd_ffn_loss(u2, w_cat, fw, fb, w_down, h1, tgt, g4):
    T = u2.shape[0]
    tm = FF_TM

    def body(u_ref, w_ref, fw_ref, fb_ref, wd_ref, h1_ref, t_ref, g_ref,
             hf_ref, pre_ref, act_ref, loss_ref, dy_ref, df_ref, dg_ref, carg, carv):
        i = pl.program_id(0)

        @pl.when(i == 0)
        def _():
            carg[...] = jnp.zeros(carg.shape, F32)
            carv[...] = jnp.zeros(carv.shape, F32)

        u = u_ref[...]
        f = None
        chunks = [(s, lo, hi) for s in range(2) for lo, hi in FF_CHUNKS]
        up = lambda s, lo, hi: _dot(u, w_ref[s, :, 2 * lo:2 * hi])
        ahead = up(*chunks[0])
        for c, (s, lo, hi) in enumerate(chunks):
            conv = []
            hs = (ahead[:, :hi - lo], ahead[:, hi - lo:])
            if c + 1 < len(chunks):
                ahead = up(*chunks[c + 1])
            at = slice(s * FF_SHARD + lo, s * FF_SHARD + hi)
            for n, car in enumerate((carg, carv)):
                h0 = hs[n]
                hf_ref[n, :, at] = h0.astype(BF16)
                h1v, h2v = _rows_before(car[:, at], h0)
                car[:, at] = h0[tm - SUBLANES:, :]
                conv.append(fw_ref[2 * n + s, 0:1, lo:hi] * h2v + fw_ref[2 * n + s, 1:2, lo:hi] * h1v
                            + fw_ref[2 * n + s, 2:3, lo:hi] * h0 + fb_ref[2 * n + s, :, lo:hi])
            pre_ref[0, :, at] = conv[0].astype(BF16)
            pre_ref[1, :, at] = conv[1].astype(BF16)
            gl, _ = _gelu_and_grad(conv[0])
            act = (gl * conv[1]).astype(BF16)
            act_ref[:, at] = act
            term = _dot(act, wd_ref[at, :])
            f = term if f is None else f + term

        r, _ = _rms_fwd(f, g_ref[...])
        e = (h1_ref[...] + r) - t_ref[...]
        dy = e * (1.0 / D_MODEL)
        dy_ref[...] = dy
        df, dg_rows = _rms_bwd(dy, f, g_ref[...])
        df_ref[...] = df.astype(BF16)
        part = 0.5 * jnp.sum(jnp.mean(e * e, axis=-1, keepdims=True), axis=0, keepdims=True)
        dg = jnp.sum(dg_rows, axis=0, keepdims=True)

        @pl.when(i == 0)
        def _():
            loss_ref[...] = part
            dg_ref[...] = dg

        @pl.when(i > 0)
        def _():
            loss_ref[...] += part
            dg_ref[...] += dg

    row = lambda w: pl.BlockSpec((tm, w), lambda i: (i, 0))
    vec = pl.BlockSpec((1, D_MODEL), lambda i: (0, 0))
    once = pl.Buffered(1)
    return _call(
        body, None, name="fwd_ffn_loss", grid=(T // tm,),
        in_specs=[row(D_MODEL),
                  pl.BlockSpec((2, D_MODEL, 2 * FF_SHARD), lambda i: (0, 0, 0), pipeline_mode=once),
                  pl.BlockSpec((N_CHIPS, FF_HALO, FF_SHARD), lambda i: (0, 0, 0)),
                  pl.BlockSpec((N_CHIPS, 1, FF_SHARD), lambda i: (0, 0, 0)),
                  pl.BlockSpec((D_FF, D_MODEL), lambda i: (0, 0), pipeline_mode=once),
                  row(D_MODEL), row(D_MODEL), vec],
        out_specs=[pl.BlockSpec((2, tm, D_FF), lambda i: (0, i, 0)),
                   pl.BlockSpec((2, tm, D_FF), lambda i: (0, i, 0)),
                   row(D_FF), pl.BlockSpec((1, 1), lambda i: (0, 0)), row(D_MODEL), row(D_MODEL), vec],
        out_shape=[jax.ShapeDtypeStruct((2, T, D_FF), BF16),
                   jax.ShapeDtypeStruct((2, T, D_FF), BF16),
                   jax.ShapeDtypeStruct((T, D_FF), BF16),
                   jax.ShapeDtypeStruct((1, 1), F32),
                   jax.ShapeDtypeStruct((T, D_MODEL), F32),
                   jax.ShapeDtypeStruct((T, D_MODEL), BF16),
                   jax.ShapeDtypeStruct((1, D_MODEL), F32)],
        scratch_shapes=[pltpu.VMEM((SUBLANES, D_FF), F32), pltpu.VMEM((SUBLANES, D_FF), F32)],
        compiler_params=_params(60, 1),
    )(u2, w_cat, fw, fb, w_down, h1, tgt, g4)[0]


def _bwd_ffn(df, hf, pre, w_cat, fw, w_down):
    T = df.shape[0]
    tm = FF_TM
    ni = T // tm

    def body(df_ref, hf_ref, pre_ref, wd_ref, w_ref, fwg_ref, fwv_ref,
             du_ref, dhf_ref, dwg_ref, dwv_ref, carg, carv):
        i = pl.program_id(1)

        @pl.when(i == 0)
        def _():
            dwg_ref[...] = jnp.zeros(dwg_ref.shape, F32)
            dwv_ref[...] = jnp.zeros(dwv_ref.shape, F32)
            carg[...] = jnp.zeros(carg.shape, F32)
            carv[...] = jnp.zeros(carv.shape, F32)

        df = df_ref[...]
        du = None
        down = lambda lo, hi: _dot_nt(df, wd_ref[lo:hi, :])
        ahead = down(*FF_CHUNKS[0])
        for c, (lo, hi) in enumerate(FF_CHUNKS):
            dact = ahead
            if c + 1 < len(FF_CHUNKS):
                ahead = down(*FF_CHUNKS[c + 1])
            pre_g = pre_ref[0, :, lo:hi].astype(F32)
            pre_v = pre_ref[1, :, lo:hi].astype(F32)
            gl, dgl = _gelu_and_grad(pre_g)
            dpre = (dact * pre_v * dgl, dact * gl)
            dhs = []
            for n, (car, fw_ref, dw_ref) in enumerate(((carg, fwg_ref, dwg_ref), (carv, fwv_ref, dwv_ref))):
                dp = dpre[n]
                h0 = hf_ref[n, :, lo:hi].astype(F32)
                up1, up2 = _rows_after(dp, car[:, lo:hi])
                car[:, lo:hi] = dp[0:SUBLANES, :]
                for k, shifted in enumerate((up2, up1, dp)):
                    dw_ref[0, k:k + 1, lo:hi] += jnp.sum(shifted * h0, axis=0, keepdims=True)
                dw_ref[0, 3:4, lo:hi] += jnp.sum(dp, axis=0, keepdims=True)
                dh = (fw_ref[0, 2:3, lo:hi] * dp + fw_ref[0, 1:2, lo:hi] * up1
                      + fw_ref[0, 0:1, lo:hi] * up2).astype(BF16)
                dhf_ref[n, :, lo:hi] = dh
                dhs.append(dh)
            term = _dot_nt(jnp.concatenate(dhs, axis=1), w_ref[0, :, 2 * lo:2 * hi])
            du = term if du is None else du + term
        du_ref[0] = du.astype(BF16)

    rev = lambda i: ni - 1 - i
    fwspec = lambda off: pl.BlockSpec((1, FF_HALO, FF_SHARD), lambda s, i: (s + off, 0, 0))
    dwspec = pl.BlockSpec((1, FF_HALO, FF_SHARD), lambda s, i: (s, 0, 0))
    return _call(
        body, None, name="bwd_ffn", grid=(2, ni),
        in_specs=[pl.BlockSpec((tm, D_MODEL), lambda s, i: (rev(i), 0)),
                  pl.BlockSpec((2, tm, FF_SHARD), lambda s, i: (0, rev(i), s)),
                  pl.BlockSpec((2, tm, FF_SHARD), lambda s, i: (0, rev(i), s)),
                  pl.BlockSpec((FF_SHARD, D_MODEL), lambda s, i: (s, 0)),
                  pl.BlockSpec((1, D_MODEL, 2 * FF_SHARD), lambda s, i: (s, 0, 0)),
                  fwspec(0), fwspec(2)],
        out_specs=[pl.BlockSpec((1, tm, D_MODEL), lambda s, i: (s, rev(i), 0)),
                   pl.BlockSpec((2, tm, FF_SHARD), lambda s, i: (0, rev(i), s)),
                   dwspec, dwspec],
        out_shape=[jax.ShapeDtypeStruct((2, T, D_MODEL), BF16),
                   jax.ShapeDtypeStruct((2, T, D_FF), BF16),
                   jax.ShapeDtypeStruct((2, FF_HALO, FF_SHARD), F32),
                   jax.ShapeDtypeStruct((2, FF_HALO, FF_SHARD), F32)],
        scratch_shapes=[pltpu.VMEM((SUBLANES, FF_SHARD), F32), pltpu.VMEM((SUBLANES, FF_SHARD), F32)],
        compiler_params=_params(56, 2),
    )(df, hf, pre, w_down, w_cat, fw, fw)[0]


def _bwd_mid(du2p, dy, h1, mixed, g3, g2, w_out, rider=None):
    T = dy.shape[0]
    tm = 512

    def body(du_ref, dy_ref, h1_ref, mx_ref, g3_ref, g2_ref, w_ref,
             dh1_ref, dmx_ref, dco_ref, dao_ref, dg3_ref, dg2_ref):
        i = pl.program_id(0)
        dres, dg3_rows = _rms_bwd(du_ref[0].astype(F32) + du_ref[1].astype(F32), h1_ref[...], g3_ref[...])
        dh1 = dy_ref[...] + dres
        dh1_ref[...] = dh1
        dmx, dg2_rows = _rms_bwd(dh1, mx_ref[...].astype(F32), g2_ref[...])
        dmx = dmx.astype(BF16)
        dmx_ref[...] = dmx
        dcat = _dot_nt(dmx, w_ref[...])
        dco_ref[...] = dcat[:, :CONV_W]
        dao_ref[...] = dcat[:, CONV_W:].astype(BF16)
        dg3 = jnp.sum(dg3_rows, axis=0, keepdims=True)
        dg2 = jnp.sum(dg2_rows, axis=0, keepdims=True)

        @pl.when(i == 0)
        def _():
            dg3_ref[...] = dg3
            dg2_ref[...] = dg2

        @pl.when(i > 0)
        def _():
            dg3_ref[...] += dg3
            dg2_ref[...] += dg2

    row = lambda w: pl.BlockSpec((tm, w), lambda i: (i, 0))
    vec = pl.BlockSpec((1, D_MODEL), lambda i: (0, 0))
    return _call(
        body, rider, name="bwd_mid", grid=(T // tm,),
        in_specs=[pl.BlockSpec((2, tm, D_MODEL), lambda i: (0, i, 0)), row(D_MODEL), row(D_MODEL),
                  row(D_MODEL), vec, vec, pl.BlockSpec((D_MODEL, D_MODEL), lambda i: (0, 0))],
        out_specs=[row(D_MODEL), row(D_MODEL), row(CONV_W), row(ATTN_W), vec, vec],
        out_shape=[jax.ShapeDtypeStruct((T, D_MODEL), F32),
                   jax.ShapeDtypeStruct((T, D_MODEL), BF16),
                   jax.ShapeDtypeStruct((T, CONV_W), F32),
                   jax.ShapeDtypeStruct((T, ATTN_W), BF16),
                   jax.ShapeDtypeStruct((1, D_MODEL), F32),
                   jax.ShapeDtypeStruct((1, D_MODEL), F32)],
        compiler_params=_params(48, 1),
    )(du2p, dy, h1, mixed, g3, g2, w_out)


def _bwd_attn(qkv, ao, dao, lse, bias, rider=None):
    T = qkv.shape[0]
    nb = T // ATT_BLK
    scale = HEAD_DIM ** -0.5

    def body(k_ref, v_ref, q0, q1, q2, do0, do1, do2, o0, o1, o2, l0, l1, l2, b_ref,
             dp_ref, ds_ref, acc1, acc2):
        j = pl.program_id(0)
        q_refs, do_refs, o_refs, l_refs = (q0, q1, q2), (do0, do1, do2), (o0, o1, o2), (l0, l1, l2)

        @pl.when(j == 0)
        def _():
            ds_ref[...] = jnp.zeros(ds_ref.shape, F32)
            acc1[...] = jnp.zeros(acc1.shape, F32)
            acc2[...] = jnp.zeros(acc2.shape, F32)

        dq_new = [[], [], []]
        dk_cols, dv_cols = [], []
        for g in range(N_HEADS // 2):
            cols = slice(g * LANES, (g + 1) * LANES)
            kg = k_ref[:, cols]
            vg = v_ref[:, cols]
            dkg = jnp.zeros((ATT_BLK, LANES), F32)
            dvg = jnp.zeros((ATT_BLK, LANES), F32)
            dqg = [jnp.zeros((ATT_BLK, LANES), F32) for _ in range(N_ATT_TILES)]
            for d in range(N_ATT_TILES):
                qg = q_refs[d][:, cols] * scale
                dog = do_refs[d][:, cols]
                if d > 0:
                    dog = jnp.where(j + d < nb, dog, jnp.zeros_like(dog))
                prod = dog.astype(F32) * o_refs[d][:, cols].astype(F32)
                for h in (2 * g, 2 * g + 1):
                    hm = _head_mask(h)
                    qh = jnp.where(hm, qg, jnp.zeros_like(qg))
                    doh = jnp.where(hm, dog, jnp.zeros_like(dog))
                    kh = jnp.where(hm, kg, jnp.zeros_like(kg))
                    delta = jnp.sum(jnp.where(hm, prod, 0.0), axis=-1, keepdims=True)
                    s = _dot_nt(qh, kg) + b_ref[d * N_HEADS + h]
                    p = jnp.exp(s - l_refs[d][:, h:h + 1])
                    dvg = dvg + _dot_tn(p.astype(BF16), doh)
                    dpm = _dot_nt(doh, vg)
                    dsc = p * (dpm - delta)
                    ds_ref[d * N_HEADS + h] += dsc
                    dsb = dsc.astype(BF16)
                    dqg[d] = dqg[d] + _dot(dsb, kh)
                    dkg = dkg + _dot_tn(dsb, qh)
            for d in range(N_ATT_TILES):
                dq_new[d].append(dqg[d])
            dk_cols.append(dkg)
            dv_cols.append(dvg)
        x0, x1, x2 = (jnp.concatenate(c, axis=1) * scale for c in dq_new)
        dp_ref[:, 0:1024] = jnp.zeros((ATT_BLK, 1024), BF16)
        dp_ref[:, 1024:1536] = (acc1[...] + x0).astype(BF16)
        dp_ref[:, 1536:2048] = jnp.concatenate(dk_cols, axis=1).astype(BF16)
        dp_ref[:, 2048:2560] = jnp.concatenate(dv_cols, axis=1).astype(BF16)
        acc1[...] = acc2[...] + x1
        acc2[...] = x2

    def fwd_spec(d, width, col):
        return pl.BlockSpec((ATT_BLK, width), lambda j: (jnp.minimum(j + d, nb - 1), col))

    return _call(
        body, rider, name="bwd_attn", grid=(nb,),
        in_specs=[pl.BlockSpec((ATT_BLK, ATTN_W), lambda j: (j, 1)),
                  pl.BlockSpec((ATT_BLK, ATTN_W), lambda j: (j, 2)),
                  fwd_spec(0, ATTN_W, 0), fwd_spec(1, ATTN_W, 0), fwd_spec(2, ATTN_W, 0),
                  fwd_spec(0, ATTN_W, 0), fwd_spec(1, ATTN_W, 0), fwd_spec(2, ATTN_W, 0),
                  fwd_spec(0, ATTN_W, 0), fwd_spec(1, ATTN_W, 0), fwd_spec(2, ATTN_W, 0),
                  fwd_spec(0, LANES, 0), fwd_spec(1, LANES, 0), fwd_spec(2, LANES, 0),
                  pl.BlockSpec((N_ATT_TILES * N_HEADS, ATT_BLK, ATT_BLK), lambda j: (0, 0, 0))],
        out_specs=[pl.BlockSpec((ATT_BLK, IN_COLS), lambda j: (j, 0)),
                   pl.BlockSpec((N_ATT_TILES * N_HEADS, ATT_BLK, ATT_BLK), lambda j: (0, 0, 0))],
        out_shape=[jax.ShapeDtypeStruct((T, IN_COLS), BF16),
                   jax.ShapeDtypeStruct((N_ATT_TILES * N_HEADS, ATT_BLK, ATT_BLK), F32)],
        scratch_shapes=[pltpu.VMEM((ATT_BLK, ATTN_W), F32), pltpu.VMEM((ATT_BLK, ATTN_W), F32)],
        compiler_params=_params(56, 1),
    )(qkv, qkv, qkv, qkv, qkv, dao, dao, dao, ao, ao, ao, lse, lse, lse, bias)


def _bwd_conv(dproj, a, dco, hc, cw, lg, lb, rider=None):
    T = a.shape[0]
    tm = 512
    rc = 32
    ni = T // tm
    hb = tm // CONV_HALO

    def body(dp_in, a_ref, ap_ref, dco_ref, dcon_ref, hc_ref, hcn_ref, w_ref, lg_ref, lb_ref,
             dp_ref, dw_ref, db_ref, dlg_ref, dlb_ref, hext, dext, hsh, dsh, dwacc):
        del dp_in
        i = pl.program_id(0)

        def ln_bwd(dco_v, hc_v):
            mu = jnp.mean(hc_v, axis=-1, keepdims=True)
            xc = hc_v - mu
            rstd = lax.rsqrt(jnp.mean(xc * xc, axis=-1, keepdims=True) + EPS)
            xh = xc * rstd
            z = xh * lg_ref[...] + lb_ref[...]
            sg = _sigmoid(z)
            dz = dco_v * (sg * (1.0 + z * (1.0 - sg)))
            dxh = dz * lg_ref[...]
            dhc = rstd * (dxh - jnp.mean(dxh, axis=-1, keepdims=True)
                          - xh * jnp.mean(dxh * xh, axis=-1, keepdims=True))
            return dhc, dz * xh, dz

        hext[0:CONV_HALO, :] = jnp.where(i > 0, ap_ref[:, :CONV_W] * _sigmoid(ap_ref[:, CONV_W:]), 0.0)
        hext[CONV_HALO:CONV_HALO + tm, :] = a_ref[:, :CONV_W] * _sigmoid(a_ref[:, CONV_W:])
        dhc, dlg_rows, dlb_rows = ln_bwd(dco_ref[...], hc_ref[...])
        dext[0:tm, :] = dhc
        dhc_next, _, _ = ln_bwd(dcon_ref[...], hcn_ref[...])
        dext[tm:tm + CONV_HALO, :] = jnp.where(i < ni - 1, dhc_next, 0.0)

        @pl.when(i == 0)
        def _():
            dw_ref[...] = jnp.zeros(dw_ref.shape, F32)
            db_ref[...] = jnp.zeros(db_ref.shape, F32)
            dlg_ref[...] = jnp.zeros(dlg_ref.shape, F32)
            dlb_ref[...] = jnp.zeros(dlb_ref.shape, F32)

            dwacc[...] = jnp.zeros(dwacc.shape, F32)

        db_ref[...] += jnp.sum(dhc, axis=0, keepdims=True)
        dlg_ref[...] += jnp.sum(dlg_rows, axis=0, keepdims=True)
        dlb_ref[...] += jnp.sum(dlb_rows, axis=0, keepdims=True)
        _fill_shifted(hext, hsh, tm)
        _fill_shifted(dext, dsh, tm)
        for c in range(tm // rc):
            r0 = c * rc
            dh = jnp.zeros((rc, CONV_W), F32)
            dhc_c = dext[r0:r0 + rc, :]
            for k in range(CONV_K):
                dh = dh + w_ref[k:k + 1, :] * _shifted_rows(dext, dsh, r0 + 30 - k, rc)
                prod = dhc_c * _shifted_rows(hext, hsh, r0 + 2 + k, rc)
                dwacc[k] += jnp.sum(prod.reshape(rc // SUBLANES, SUBLANES, CONV_W), axis=0)
            av = a_ref[r0:r0 + rc, :CONV_W]
            sg = _sigmoid(a_ref[r0:r0 + rc, CONV_W:])
            dp_ref[r0:r0 + rc, 0:CONV_W] = (dh * sg).astype(BF16)
            dp_ref[r0:r0 + rc, CONV_W:] = (dh * av * sg * (1.0 - sg)).astype(BF16)

        @pl.when(i == ni - 1)
        def _():
            dw_ref[...] = jnp.sum(dwacc[...], axis=1)

    row = lambda w: pl.BlockSpec((tm, w), lambda i: (i, 0))
    prev = lambda w: pl.BlockSpec((CONV_HALO, w), lambda i: (jnp.maximum(i * hb - 1, 0), 0))
    nxt = lambda w: pl.BlockSpec((CONV_HALO, w), lambda i: (jnp.minimum((i + 1) * hb, ni * hb - 1), 0))
    vec = pl.BlockSpec((1, CONV_W), lambda i: (0, 0))
    return _call(
        body, rider, name="bwd_conv", grid=(ni,),
        in_specs=[ANY, row(1024), prev(1024), row(CONV_W), nxt(CONV_W), row(CONV_W), nxt(CONV_W),
                  pl.BlockSpec((CONV_HALO, CONV_W), lambda i: (0, 0)), vec, vec],
        out_specs=[pl.BlockSpec((tm, 1024), lambda i: (i, 0)),
                   pl.BlockSpec((CONV_HALO, CONV_W), lambda i: (0, 0)), vec, vec, vec],
        out_shape=[jax.ShapeDtypeStruct((T, IN_COLS), BF16),
                   jax.ShapeDtypeStruct((CONV_HALO, CONV_W), F32),
                   jax.ShapeDtypeStruct((1, CONV_W), F32),
                   jax.ShapeDtypeStruct((1, CONV_W), F32),
                   jax.ShapeDtypeStruct((1, CONV_W), F32)],
        scratch_shapes=[pltpu.VMEM((tm + CONV_HALO, CONV_W), F32), pltpu.VMEM((tm + CONV_HALO, CONV_W), F32),
                        pltpu.VMEM((SUBLANES - 1, tm + CONV_HALO - SUBLANES, CONV_W), F32),
                        pltpu.VMEM((SUBLANES - 1, tm + CONV_HALO - SUBLANES, CONV_W), F32),
                        pltpu.VMEM((CONV_HALO, SUBLANES, CONV_W), F32)],
        input_output_aliases={0: 0},
        compiler_params=_params(56, 1),
    )(dproj, a, a, dco, dco, hc, hc, cw, lg, lb)


def _bwd_in_proj(dproj, w_in, x, dh1, g1, rider=None):
    T = x.shape[0]
    tm = 512

    def body(dp_ref, w_ref, x_ref, dh_ref, g_ref, gx_ref, dg_ref):
        i = pl.program_id(0)
        du = _dot_nt(dp_ref[...], w_ref[...])
        dx, dg_rows = _rms_bwd(du, x_ref[...], g_ref[...])
        gx_ref[...] = dh_ref[...] + dx
        dg = jnp.sum(dg_rows, axis=0, keepdims=True)

        @pl.when(i == 0)
        def _():
            dg_ref[...] = dg

        @pl.when(i > 0)
        def _():
            dg_ref[...] += dg

    row = lambda w: pl.BlockSpec((tm, w), lambda i: (i, 0))
    vec = pl.BlockSpec((1, D_MODEL), lambda i: (0, 0))
    return _call(
        body, rider, name="bwd_in_proj", grid=(T // tm,),
        in_specs=[row(IN_COLS), pl.BlockSpec((D_MODEL, IN_COLS), lambda i: (0, 0)),
                  row(D_MODEL), row(D_MODEL), vec],
        out_specs=[row(D_MODEL), vec],
        out_shape=[jax.ShapeDtypeStruct((T, D_MODEL), F32), jax.ShapeDtypeStruct((1, D_MODEL), F32)],
        compiler_params=_params(40, 1),
    )(dproj, w_in, x, dh1, g1)


def _wgrad(name, a_list, a_spec, b, b_spec, out_spec, out_shape, n_outer, T, tk, select=None, rider=None):
    def body(*refs):
        a_refs, b_ref, o_ref = refs[:len(a_list)], refs[len(a_list)], refs[len(a_list) + 1]
        kt = pl.program_id(1)

        @pl.when(kt == 0)
        def _():
            o_ref[...] = jnp.zeros(o_ref.shape, F32)

        bv = b_ref[...].reshape(b_ref.shape[-2:])
        if select is None:
            o_ref[...] += _dot_tn(a_refs[0][...].reshape(a_refs[0].shape[-2:]), bv).reshape(o_ref.shape)
        else:
            for n, a_ref in enumerate(a_refs):
                @pl.when(select(pl.program_id(0)) == n)
                def _():
                    o_ref[...] += _dot_tn(a_ref[...], bv).reshape(o_ref.shape)

    (res,), got = _call(
        body, rider, name=name, grid=(n_outer, T // tk),
        in_specs=[a_spec] * len(a_list) + [b_spec],
        out_specs=[out_spec], out_shape=[out_shape],
        compiler_params=_params(56, 2),
    )(*a_list, b)
    return (res, got) if rider is not None else res


def _mesh_pos():
    return lax.axis_index("x"), lax.axis_index("y"), lax.axis_index("c")


def _other_chips(x, y):
    return [((1 - x, y), 2 * (1 - x) + y), ((x, 1 - y), 2 * x + (1 - y)), ((1 - x, 1 - y), 2 * (1 - x) + (1 - y))]


def _exchange_rider(operands, out_shape, aliases, sem_shape, pairs):
    def start(ins, outs, sems):
        for send, _ in pairs(ins, outs, *sems):
            send.start()

    def finish(ins, outs, sems):
        for send, recv in pairs(ins, outs, *sems):
            send.wait_send()
            recv.wait_recv()

    sems = [pltpu.SemaphoreType.DMA(sem_shape), pltpu.SemaphoreType.DMA(sem_shape)]
    return _Rider(list(operands), list(out_shape), aliases, sems, start, finish)


def _remote(src, dst, send_sem, recv_sem, device):
    return pltpu.make_async_remote_copy(src_ref=src, dst_ref=dst, send_sem=send_sem, recv_sem=recv_sem,
                                        device_id=device, device_id_type=MESH)


def _fetch_rider(bufs):
    def pairs(ins, outs, send_sems, recv_sems):
        x, y, c = _mesh_pos()
        res = []
        for t, buf in enumerate(bufs):
            rows = pl.ds(c * (buf.shape[1] // 2), buf.shape[1] // 2)
            mine = outs[t].at[2 * x + y, rows]
            for k, (chip, s) in enumerate(_other_chips(x, y)):
                landed = outs[t].at[s, rows]
                res.append((_remote(mine, mine, send_sems.at[t, k], recv_sems.at[t, k], (*chip, c)),
                            _remote(landed, landed, send_sems.at[t, k], recv_sems.at[t, k], (*chip, c))))
        return res

    shapes = [jax.ShapeDtypeStruct(b.shape, b.dtype) for b in bufs]
    return _exchange_rider(bufs, shapes, {t: t for t in range(len(bufs))}, (len(bufs), 3), pairs)


def _forward_rider(bufs):
    def pairs(ins, outs, send_sems, recv_sems):
        x, y, c = _mesh_pos()
        res = []
        for t, buf in enumerate(bufs):
            half = buf.shape[1] // 2
            for k, (_, s) in enumerate(_other_chips(x, y)):
                landed = outs[t].at[s, pl.ds(c * half, half)]
                theirs = outs[t].at[s, pl.ds((1 - c) * half, half)]
                res.append((_remote(landed, landed, send_sems.at[t, k], recv_sems.at[t, k], (x, y, 1 - c)),
                            _remote(theirs, theirs, send_sems.at[t, k], recv_sems.at[t, k], (x, y, 1 - c))))
        return res

    shapes = [jax.ShapeDtypeStruct(b.shape, b.dtype) for b in bufs]
    return _exchange_rider(bufs, shapes, {t: t for t in range(len(bufs))}, (len(bufs), 3), pairs)


def _pair_exchange_rider(grads):
    def pairs(ins, outs, send_sems, recv_sems):
        x, y, c = _mesh_pos()
        res = []
        for t, g in enumerate(grads):
            half = g.shape[1] // 2
            cp = _remote(ins[t].at[:, pl.ds((1 - c) * half, half), :], outs[t], send_sems.at[t], recv_sems.at[t],
                         (x, y, 1 - c))
            res.append((cp, cp))
        return res

    shapes = [jax.ShapeDtypeStruct((N_CHIPS, g.shape[1] // 2, g.shape[2]), F32) for g in grads]
    return _exchange_rider(grads, shapes, {}, (len(grads),), pairs)


def _chip_exchange_rider(sums):
    def pairs(ins, outs, send_sems, recv_sems):
        x, y, c = _mesh_pos()
        res = []
        for t in range(len(sums)):
            for k, (chip, s) in enumerate(_other_chips(x, y)):
                cp = _remote(ins[t].at[s], outs[t].at[k], send_sems.at[t, k], recv_sems.at[t, k], (*chip, c))
                res.append((cp, cp))
        return res

    shapes = [jax.ShapeDtypeStruct((3,) + p.shape[1:], p.dtype) for p in sums]
    return _exchange_rider(sums, shapes, {}, (len(sums), 3), pairs)


def _pair_gather_rider(fulls):
    def pairs(ins, outs, send_sems, recv_sems):
        x, y, c = _mesh_pos()
        res = []
        for t, f in enumerate(fulls):
            half = f.shape[0] // 2
            mine = outs[t].at[pl.ds(c * half, half)]
            theirs = outs[t].at[pl.ds((1 - c) * half, half)]
            res.append((_remote(mine, mine, send_sems.at[t], recv_sems.at[t], (x, y, 1 - c)),
                        _remote(theirs, theirs, send_sems.at[t], recv_sems.at[t], (x, y, 1 - c))))
        return res

    shapes = [jax.ShapeDtypeStruct(f.shape, F32) for f in fulls]
    return _exchange_rider(fulls, shapes, {t: t for t in range(len(fulls))}, (len(fulls),), pairs)


def _alone(name, rider):
    return _call(lambda: None, rider, name=name)()[1]


def _all_reduce_small(pack, rider=None):
    rows = pack.shape[0]

    def body(p_ref, o_ref, buf, send_sems, recv_sems):
        x, y, c = _mesh_pos()
        me = 4 * x + 2 * y + c
        buf[0] = p_ref[...]
        copies = []
        for k in range(1, 8):
            peer = (x ^ (k >> 2), y ^ ((k >> 1) & 1), c ^ (k & 1))
            cp = pltpu.make_async_remote_copy(
                src_ref=p_ref, dst_ref=buf.at[k], send_sem=send_sems.at[k - 1], recv_sem=recv_sems.at[k - 1],
                device_id=peer, device_id_type=MESH)
            cp.start()
            copies.append(cp)
        for cp in copies:
            cp.wait()
        total = buf[me]
        for dev in range(1, 8):
            total = total + buf[me ^ dev]
        o_ref[...] = total

    return _call(
        body, rider, name="all_reduce_small",
        in_specs=[VMEM_FULL], out_specs=[VMEM_FULL],
        out_shape=[jax.ShapeDtypeStruct(pack.shape, F32)],
        scratch_shapes=[pltpu.VMEM((8, rows, LANES), F32),
                        pltpu.SemaphoreType.DMA((7,)), pltpu.SemaphoreType.DMA((7,))],
    )(pack)


def _row_block(rows):
    if rows <= 512:
        return rows
    for rb in (256, 352):
        if rows % rb == 0:
            return rb
    raise ValueError(f"no row block for {rows} rows")


def _place(name, w, pos, dtype):
    R, C = w.shape
    rb = _row_block(R)

    def body(pos_ref, w_ref, o_ref):
        del pos_ref
        o_ref[0] = w_ref[...].astype(dtype)

    return pl.pallas_call(
        body, name=name,
        grid_spec=pltpu.PrefetchScalarGridSpec(
            num_scalar_prefetch=1, grid=(R // rb,),
            in_specs=[pl.BlockSpec((rb, C), lambda r, p: (r, 0))],
            out_specs=pl.BlockSpec((1, rb, C), lambda r, p: (p[0], r, 0))),
        out_shape=(pltpu.HBM if N_CHIPS * R * C * jnp.dtype(dtype).itemsize >= PIN_BYTES
                   else jax.ShapeDtypeStruct)((N_CHIPS, R, C), dtype),
        compiler_params=_params(32, 1),
    )(pos, w)


def _pair_sum(name, g, got, pos):
    S, R, C = g.shape
    half = R // 2
    rb = _row_block(half)
    nh = half // rb

    def body(pos_ref, a_ref, b_ref, o_ref):
        del pos_ref
        o_ref[...] = (a_ref[...] + b_ref[...]).astype(BF16)

    spec = pl.BlockSpec((1, rb, C), lambda s, r, p: (s, r, 0))
    return pl.pallas_call(
        body, name=name,
        grid_spec=pltpu.PrefetchScalarGridSpec(
            num_scalar_prefetch=1, grid=(S, nh),
            in_specs=[pl.BlockSpec((1, rb, C), lambda s, r, p: (s, p[1] * nh + r, 0)), spec],
            out_specs=spec),
        out_shape=jax.ShapeDtypeStruct((S, half, C), BF16), compiler_params=_params(32, 2),
    )(pos, g, got)


def _chip_sum(name, pairs, got, pos):
    _, half, C = pairs.shape
    rb = _row_block(half)
    nh = half // rb

    def body(pos_ref, a_ref, g_ref, o_ref):
        del pos_ref
        o_ref[...] = ((a_ref[0].astype(F32) + g_ref[0].astype(F32)) + g_ref[1].astype(F32)) + g_ref[2].astype(F32)

    return pl.pallas_call(
        body, name=name,
        grid_spec=pltpu.PrefetchScalarGridSpec(
            num_scalar_prefetch=1, grid=(nh,),
            in_specs=[pl.BlockSpec((1, rb, C), lambda r, p: (p[0], r, 0)),
                      pl.BlockSpec((3, rb, C), lambda r, p: (0, r, 0))],
            out_specs=pl.BlockSpec((rb, C), lambda r, p: (p[1] * nh + r, 0))),
        out_shape=jax.ShapeDtypeStruct((2 * half, C), F32), compiler_params=_params(32, 1),
    )(pos, pairs, got)


def _adamw(name, w, g, m, v):
    R, C = w.shape
    rb = _row_block(R)
    c1 = 1.0 - ADAM_B1 ** ADAM_STEP
    c2 = 1.0 - ADAM_B2 ** ADAM_STEP

    def body(w_ref, g_ref, m_ref, v_ref, d_ref, nm_ref, nv_ref):
        gv = g_ref[...]
        nm = ADAM_B1 * m_ref[...] + (1.0 - ADAM_B1) * gv
        nv = ADAM_B2 * v_ref[...] + (1.0 - ADAM_B2) * (gv * gv)
        nm_ref[...] = nm
        nv_ref[...] = nv
        d_ref[...] = -ADAM_LR * ((nm / c1) / (jnp.sqrt(nv / c2) + ADAM_EPS) + ADAM_WD * w_ref[...])

    spec = pl.BlockSpec((rb, C), lambda r: (r, 0))
    sds = jax.ShapeDtypeStruct(w.shape, F32)
    return pl.pallas_call(
        body, name=name, grid=(R // rb,), in_specs=[spec] * 4, out_specs=[spec] * 3,
        out_shape=[sds, sds, sds], compiler_params=_params(40, 1),
    )(w, g, m, v)


def _rel_index():
    m = np.arange(2 * ATT_BLK)
    off = np.where(m < ATT_BLK, m, m - 2 * ATT_BLK)
    rel = np.stack([ATT_BLK * d - off for d in range(N_ATT_TILES)])
    return np.clip(rel, -MAX_REL, MAX_REL) + MAX_REL


def _local_step(x, tgt, g1, w_in, cw, cb, lg, lb, bias, w_out, g2, g3, w_up, fw, fb, w_down, g4, pos=None):
    T = x.shape[0]
    dist = pos is not None
    idx = _rel_index()

    (u, a, qkv), got = _fwd_in_proj(x, g1, w_in, _fetch_rider([w_out, w_down]) if dist else None)
    if dist:
        w_out, w_down = got
    (co, hc), got = _fwd_conv(a, cw, cb, lg, lb, _merge_riders(_forward_rider([w_out, w_down]),
                                                               _fetch_rider([w_up])) if dist else None)
    if dist:
        w_out, w_down, w_up = got
    (ao, lse), got = _fwd_attn(qkv, bias, _forward_rider([w_up]) if dist else None)
    if dist:
        (w_up,) = got
        w_out, w_down = w_out.reshape(D_MODEL, D_MODEL), w_down.reshape(D_FF, D_MODEL)
    mixed, h1, u2 = _fwd_out_proj(co, ao, w_out, x, g2, g3)
    w_cat = _pair_up_weights(w_up)
    hf, pre, act, loss, dy, df, dg4 = _fwd_ffn_loss(u2, w_cat, fw, fb, w_down, h1, tgt, g4)

    tk = min(2048, T)
    du2p, dhf, dfw_g, dfw_v = _bwd_ffn(df, hf, pre, w_cat, fw, w_down)
    gw_up = _wgrad(
        "wgrad_up", [u2], pl.BlockSpec((tk, D_MODEL), lambda s, k: (k, 0)),
        dhf, pl.BlockSpec((1, tk, FF_SHARD), lambda s, k: (s // 2, k, s % 2)),
        pl.BlockSpec((1, D_MODEL, FF_SHARD), lambda s, k: (s, 0, 0)),
        jax.ShapeDtypeStruct((N_CHIPS, D_MODEL, FF_SHARD), F32), N_CHIPS, T, tk)
    gw_down = _wgrad(
        "wgrad_down", [act], pl.BlockSpec((tk, FF_SHARD), lambda s, k: (k, s)),
        df, pl.BlockSpec((tk, D_MODEL), lambda s, k: (k, 0)),
        pl.BlockSpec((FF_SHARD, D_MODEL), lambda s, k: (s, 0)),
        jax.ShapeDtypeStruct((D_FF, D_MODEL), F32), 2, T, tk).reshape(N_CHIPS, D_FF // N_CHIPS, D_MODEL)
    (dh1, dmx, dco, dao, dg3, dg2), _ = _bwd_mid(du2p, dy, h1, mixed, g3, g2, w_out)
    gw_out = _wgrad(
        "wgrad_out", [co, ao], pl.BlockSpec((tk, CONV_W), lambda s, k: (k, 0)),
        dmx, pl.BlockSpec((tk, D_MODEL), lambda s, k: (k, 0)),
        pl.BlockSpec((CONV_W, D_MODEL), lambda s, k: (s, 0)),
        jax.ShapeDtypeStruct((D_MODEL, D_MODEL), F32), 2, T, tk,
        select=lambda s: s, rider=_pair_exchange_rider([gw_up, gw_down]) if dist else None)
    if dist:
        gw_out, got = gw_out
        p_up = _pair_sum("pair_sum_w_up", gw_up, got[0], pos)
        p_down = _pair_sum("pair_sum_w_down", gw_down, got[1], pos)
    gw_out = gw_out.reshape(N_CHIPS, D_MODEL // N_CHIPS, D_MODEL)
    (dproj, dsacc), got = _bwd_attn(
        qkv, ao, dao, lse, bias,
        _merge_riders(_chip_exchange_rider([p_up, p_down]), _pair_exchange_rider([gw_out])) if dist else None)
    if dist:
        gw_up = _chip_sum("chip_sum_w_up", p_up, got[0], pos)
        gw_down = _chip_sum("chip_sum_w_down", p_down, got[1], pos)
        p_out = _pair_sum("pair_sum_w_out", gw_out, got[2], pos)
    (dproj, dcw, dcb, dlg, dlb), got = _bwd_conv(
        dproj, a, dco, hc, cw, lg, lb,
        _merge_riders(_pair_gather_rider([gw_up, gw_down]), _chip_exchange_rider([p_out])) if dist else None)
    if dist:
        gw_up, gw_down = got[:2]
        gw_out = _chip_sum("chip_sum_w_out", p_out, got[2], pos)
    gw_in = _wgrad(
        "wgrad_in", [u], pl.BlockSpec((tk, D_MODEL), lambda s, k: (k, 0)),
        dproj, pl.BlockSpec((tk, IN_SHARD), lambda s, k: (k, s)),
        pl.BlockSpec((1, D_MODEL, IN_SHARD), lambda s, k: (s, 0, 0)),
        jax.ShapeDtypeStruct((N_CHIPS, D_MODEL, IN_SHARD), F32), N_CHIPS, T, tk)
    if dist:
        got = _alone("pair_exchange_w_in", _merge_riders(_pair_exchange_rider([gw_in]), _pair_gather_rider([gw_out])))
        p_in, gw_out = _pair_sum("pair_sum_w_in", gw_in, got[0], pos), got[1]
    (gx, dg1), _ = _bwd_in_proj(dproj, w_in, x, dh1, g1)
    (diag,), got = _diag_sums(dsacc, _chip_exchange_rider([p_in]) if dist else None)
    if dist:
        gw_in = _chip_sum("chip_sum_w_in", p_in, got[0], pos)

    diag = diag.reshape(N_ATT_TILES, N_HEADS, 2 * ATT_BLK)
    onehot = np.zeros((N_ATT_TILES, 2 * ATT_BLK, 2 * MAX_REL + 1), np.float32)
    for d in range(N_ATT_TILES):
        onehot[d, np.arange(2 * ATT_BLK), idx[d]] = 1.0
    drel = jnp.einsum("dhm,dmr->hr", diag, jnp.asarray(onehot), precision=lax.Precision.HIGHEST)

    small = dict(norm_mix_pre=dg1, conv_dw_w=dcw[:CONV_K], conv_dw_b=dcb, conv_ln_g=dlg, conv_ln_b=dlb,
                 rel_bias=drel, norm_mix_post=dg2, norm_ffn_pre=dg3,
                 ffn_dw_w=jnp.concatenate([dfw_g[0, :3], dfw_g[1, :3], dfw_v[0, :3], dfw_v[1, :3]], axis=1),
                 ffn_dw_b=jnp.concatenate([dfw_g[0, 3:4], dfw_g[1, 3:4], dfw_v[0, 3:4], dfw_v[1, 3:4]], axis=1),
                 norm_ffn_post=dg4)
    return loss, gx, small, dict(w_in=gw_in, w_out=gw_out, w_up=gw_up, w_down=gw_down)


SMALL_ORDER = ["norm_mix_pre", "conv_dw_b", "conv_ln_g", "conv_ln_b", "rel_bias", "norm_mix_post",
               "norm_ffn_pre", "ffn_dw_b", "norm_ffn_post", "conv_dw_w", "ffn_dw_w"]


def _pack(parts):
    rows = []
    for p in parts:
        width = -(-p.shape[1] // LANES) * LANES
        rows.append(jnp.pad(p, ((0, 0), (0, width - p.shape[1]))).reshape(-1, LANES))
    packed = jnp.concatenate(rows, axis=0)
    pad = -packed.shape[0] % 8
    return jnp.pad(packed, ((0, pad), (0, 0)))


def _unpack(packed, shapes):
    out, r = [], 0
    for shp in shapes:
        width = -(-shp[1] // LANES) * LANES
        n = shp[0] * width // LANES
        out.append(packed[r:r + n].reshape(shp[0], width)[:, :shp[1]])
        r += n
    return out


WEIGHTS = ["norm_mix_pre", "w_in", "conv_dw_w", "conv_dw_b", "conv_ln_g", "conv_ln_b", "rel_bias", "w_out",
           "norm_mix_post", "norm_ffn_pre", "w_up", "ffn_dw_w", "ffn_dw_b", "w_down", "norm_ffn_post"]
BIG = ["w_in", "w_out", "w_up", "w_down"]


def kernel(x, norm_mix_pre, w_in, conv_dw_w, conv_dw_b, conv_ln_g, conv_ln_b, rel_bias, w_out, norm_mix_post, norm_ffn_pre, w_up, ffn_dw_w, ffn_dw_b, w_down, norm_ffn_post, loss_target, m_norm_mix_pre, m_w_in, m_conv_dw_w, m_conv_dw_b, m_conv_ln_g, m_conv_ln_b, m_rel_bias, m_w_out, m_norm_mix_post, m_norm_ffn_pre, m_w_up, m_ffn_dw_w, m_ffn_dw_b, m_w_down, m_norm_ffn_post, v_norm_mix_pre, v_w_in, v_conv_dw_w, v_conv_dw_b, v_conv_ln_g, v_conv_ln_b, v_rel_bias, v_w_out, v_norm_mix_post, v_norm_ffn_pre, v_w_up, v_ffn_dw_w, v_ffn_dw_b, v_w_down, v_norm_ffn_post):
    args = locals()
    w = {n: args[n][0] for n in WEIGHTS}
    m = {n: args["m_" + n][0] for n in WEIGHTS}
    v = {n: args["v_" + n][0] for n in WEIGHTS}
    for d in (w, m, v):
        d["rel_bias"] = d["rel_bias"].reshape(N_HEADS, 2 * MAX_REL + 1)
        for n in ("norm_mix_pre", "conv_dw_b", "conv_ln_g", "conv_ln_b", "norm_mix_post", "norm_ffn_pre",
                  "ffn_dw_b", "norm_ffn_post"):
            d[n] = d[n].reshape(1, -1)
    shard = 2 * lax.axis_index("x") + lax.axis_index("y")

    cw_sh = jnp.pad(w["conv_dw_w"], ((0, CONV_HALO - CONV_K), (0, 0)))
    fw_sh = jnp.pad(w["ffn_dw_w"], ((0, FF_HALO - 3), (0, 0)))
    pos = jnp.stack([shard, lax.axis_index("c")]).astype(jnp.int32)
    bufs = {n: _place("place_" + n, w[n], pos, BF16) for n in BIG}
    first = [bufs["w_in"], _place("place_conv_dw_w", cw_sh, pos, F32), _place("place_ffn_dw_w", fw_sh, pos, F32)]
    (bias,), first = _bias_tiles(w["rel_bias"], _fetch_rider(first))
    w_in_f, cw_f, fw_f = _alone("all_gather_forward", _forward_rider(list(first)))
    cw_full = jnp.transpose(cw_f, (1, 0, 2)).reshape(CONV_HALO, CONV_W)

    loss, gx, small, big = _local_step(
        x[0], loss_target[0], w["norm_mix_pre"], _join_columns(w_in_f), cw_full, w["conv_dw_b"], w["conv_ln_g"],
        w["conv_ln_b"], bias, bufs["w_out"], w["norm_mix_post"],
        w["norm_ffn_pre"], bufs["w_up"], fw_f, w["ffn_dw_b"].reshape(N_CHIPS, 1, FF_SHARD),
        bufs["w_down"], w["norm_ffn_post"], pos)
    (gsum,), (big["w_in"],) = _all_reduce_small(_pack([small[n] for n in SMALL_ORDER] + [loss]),
                                                _pair_gather_rider([big["w_in"]]))

    grads, deltas, new_m, new_v = {}, {}, {}, {}
    for n in BIG:
        grads[n] = big[n]
        deltas[n], new_m[n], new_v[n] = _adamw("adamw_" + n, w[n], big[n], m[n], v[n])
    shapes = [small[n].shape for n in SMALL_ORDER]
    *reduced, total = _unpack(gsum, shapes + [loss.shape])
    gs = dict(zip(SMALL_ORDER, reduced))
    gs["conv_dw_w"] = lax.dynamic_slice_in_dim(gs["conv_dw_w"], shard * LANES, LANES, axis=1)
    gs["ffn_dw_w"] = lax.dynamic_slice_in_dim(gs["ffn_dw_w"], shard * FF_SHARD, FF_SHARD, axis=1)
    shapes = [gs[n].shape for n in SMALL_ORDER]
    d_p, m_p, v_p = _adamw("adamw_small", _pack([w[n] for n in SMALL_ORDER]), _pack([gs[n] for n in SMALL_ORDER]),
                           _pack([m[n] for n in SMALL_ORDER]), _pack([v[n] for n in SMALL_ORDER]))
    for dst, packed in ((deltas, d_p), (new_m, m_p), (new_v, v_p)):
        dst.update(zip(SMALL_ORDER, _unpack(packed, shapes)))
    grads.update(gs)

    outs = [total[0, 0], gx[None]]
    for group in (grads, deltas, new_m, new_v):
        outs += [group[n].reshape(args[n].shape) for n in WEIGHTS]
    return tuple(outs)
```

```python
import functools
import math
from typing import Callable, NamedTuple

import numpy as np
import jax
import jax.numpy as jnp
from jax import lax
from jax.experimental import pallas as pl
from jax.experimental.pallas import tpu as pltpu

F32 = jnp.float32
BF16 = jnp.bfloat16

D_MODEL = 1024
CONV_W = 512
ATTN_W = 512
N_HEADS = 8
HEAD_DIM = 64
CHUNK = 64
N_LEFT = 8
MAX_REL = 128
CONV_K = 31
CONV_HALO = 32
D_FF = 2816
FF_SHARD = 1408
IN_COLS = 2560
IN_SHARD = 640
EPS = 1e-6
NEG_INF = -1e30
ATT_BLK = 256
N_ATT_TILES = 3
LANES = 128
SUBLANES = 8
N_CHIPS = 4

ADAM_LR = 0.001
ADAM_B1 = 0.9
ADAM_B2 = 0.999
ADAM_EPS = 1e-08
ADAM_WD = 0.01
ADAM_STEP = 10

MESH = pl.DeviceIdType.MESH
ANY = pl.BlockSpec(memory_space=pl.ANY)
VMEM_FULL = pl.BlockSpec(memory_space=pltpu.VMEM)


def _params(vmem_mb, n_grid=0):
    sem = ("arbitrary",) * n_grid if n_grid else None
    return pltpu.CompilerParams(dimension_semantics=sem, vmem_limit_bytes=vmem_mb << 20)


class _Rider(NamedTuple):
    operands: list
    out_shape: list
    aliases: dict
    sems: list
    start: Callable
    finish: Callable


def _merge_riders(a, b):
    ia, oa, sa = len(a.operands), len(a.out_shape), len(a.sems)

    def start(ins, outs, sems):
        a.start(ins[:ia], outs[:oa], sems[:sa])
        b.start(ins[ia:], outs[oa:], sems[sa:])

    def finish(ins, outs, sems):
        a.finish(ins[:ia], outs[:oa], sems[:sa])
        b.finish(ins[ia:], outs[oa:], sems[sa:])

    aliases = {**a.aliases, **{k + ia: v + oa for k, v in b.aliases.items()}}
    return _Rider(a.operands + b.operands, a.out_shape + b.out_shape, aliases, a.sems + b.sems, start, finish)


PIN_BYTES = 1 << 20


def _big(a):
    return math.prod(a.shape) * jnp.dtype(a.dtype).itemsize >= PIN_BYTES


def _pin_args(args):
    return [pltpu.with_memory_space_constraint(a, pltpu.HBM) if _big(a) else a for a in args]


def _call(body, rider, *, grid=(), in_specs=(), out_specs=(), out_shape=(), scratch_shapes=(),
          input_output_aliases=None, **kwargs):
    in_specs, out_specs = list(in_specs), list(out_specs)
    pin_out = lambda shapes: [pltpu.HBM(s.shape, s.dtype) if _big(s) else s for s in shapes]
    out_shape = pin_out(out_shape)
    scratch, aliases = list(scratch_shapes), dict(input_output_aliases or {})
    if rider is None:
        plain = pl.pallas_call(body, grid=grid, in_specs=in_specs, out_specs=out_specs, out_shape=out_shape,
                               scratch_shapes=scratch, input_output_aliases=aliases, **kwargs)
        return lambda *args: (plain(*_pin_args(args)), [])
    n_in, n_out, n_scr = len(in_specs), len(out_specs), len(scratch)
    r_in, r_out = len(rider.operands), len(rider.out_shape)

    def carried(*refs):
        ins, r_ins, refs = refs[:n_in], refs[n_in:n_in + r_in], refs[n_in + r_in:]
        outs, r_outs, refs = refs[:n_out], refs[n_out:n_out + r_out], refs[n_out + r_out:]
        scr, r_sems = refs[:n_scr], refs[n_scr:]
        if not grid:
            rider.start(r_ins, r_outs, r_sems)
            body(*ins, *outs, *scr)
            rider.finish(r_ins, r_outs, r_sems)
            return
        at = [pl.program_id(d) for d in range(len(grid))]
        first = functools.reduce(jnp.logical_and, [p == 0 for p in at])
        last = functools.reduce(jnp.logical_and, [p == n - 1 for p, n in zip(at, grid)])

        @pl.when(first)
        def _():
            rider.start(r_ins, r_outs, r_sems)

        body(*ins, *outs, *scr)

        @pl.when(last)
        def _():
            rider.finish(r_ins, r_outs, r_sems)

    aliases.update({n_in + k: n_out + v for k, v in rider.aliases.items()})
    both = pl.pallas_call(carried, grid=grid, in_specs=in_specs + [ANY] * r_in, out_specs=out_specs + [ANY] * r_out,
                          out_shape=out_shape + pin_out(rider.out_shape), scratch_shapes=scratch + rider.sems,
                          input_output_aliases=aliases, **kwargs)

    def run(*args):
        res = both(*_pin_args(args), *rider.operands)
        return res[:n_out], res[n_out:]

    return run


def _sigmoid(v):
    return 1.0 / (1.0 + jnp.exp(-v))


def _dot(a, b):
    return jnp.dot(a, b, preferred_element_type=F32)


def _dot_nt(a, b):
    return lax.dot_general(a, b, (((1,), (1,)), ((), ())), preferred_element_type=F32)


def _dot_tn(a, b):
    return lax.dot_general(a, b, (((0,), (0,)), ((), ())), preferred_element_type=F32)


def _rms_fwd(v, g):
    r = lax.rsqrt(jnp.mean(v * v, axis=-1, keepdims=True) + EPS)
    return v * r * g, r


def _rms_bwd(dy, v, g):
    r = lax.rsqrt(jnp.mean(v * v, axis=-1, keepdims=True) + EPS)
    vh = v * r
    dvh = dy * g
    dv = r * (dvh - vh * jnp.mean(dvh * vh, axis=-1, keepdims=True))
    return dv, dy * vh


def _join_columns(w):
    S, R, C = w.shape
    rb = 256

    def body(w_ref, o_ref):
        for s in range(S):
            o_ref[:, s * C:(s + 1) * C] = w_ref[s]

    return pl.pallas_call(
        body, name="join_columns", grid=(R // rb,),
        in_specs=[pl.BlockSpec((S, rb, C), lambda r: (0, r, 0))],
        out_specs=pl.BlockSpec((rb, S * C), lambda r: (r, 0)),
        out_shape=jax.ShapeDtypeStruct((R, S * C), w.dtype),
        compiler_params=_params(32, 1),
    )(w)


GLU_COLS = 2 * CONV_W


def _fwd_in_proj(x, g1, w_in, rider=None):
    T = x.shape[0]
    tm = 1024

    def body(x_ref, g_ref, w_ref, u_ref, a_ref, qkv_ref):
        u, _ = _rms_fwd(x_ref[...], g_ref[...])
        u = u.astype(BF16)
        u_ref[...] = u
        a_ref[...] = _dot(u, w_ref[:, :GLU_COLS])
        qkv_ref[...] = _dot(u, w_ref[:, GLU_COLS:]).astype(BF16)

    return _call(
        body, rider, name="fwd_in_proj", grid=(T // tm,),
        in_specs=[pl.BlockSpec((tm, D_MODEL), lambda i: (i, 0)),
                  pl.BlockSpec((1, D_MODEL), lambda i: (0, 0)),
                  pl.BlockSpec((D_MODEL, IN_COLS), lambda i: (0, 0))],
        out_specs=[pl.BlockSpec((tm, D_MODEL), lambda i: (i, 0)),
                   pl.BlockSpec((tm, 1024), lambda i: (i, 0)),
                   pl.BlockSpec((tm, 1536), lambda i: (i, 0))],
        out_shape=[jax.ShapeDtypeStruct((T, D_MODEL), BF16),
                   jax.ShapeDtypeStruct((T, 1024), F32),
                   jax.ShapeDtypeStruct((T, 1536), BF16)],
        compiler_params=_params(56, 1),
    )(x, g1, w_in)


def _fill_shifted(ext, shifted, tm):
    n = tm + CONV_HALO - SUBLANES
    for j in range(1, SUBLANES):
        shifted[j - 1] = ext[j:j + n, :]


def _shifted_rows(ext, shifted, start, rows):
    j = start % SUBLANES
    if j == 0:
        return ext[start:start + rows, :]
    return shifted[j - 1, start - j:start - j + rows, :]


def _fwd_conv(a, cw, cb, lg, lb, rider=None):
    T = a.shape[0]
    tm = 512
    rc = 64

    def body(a_ref, w_ref, b_ref, lg_ref, lb_ref, co_ref, hc_ref, hext, hsh):
        i = pl.program_id(0)

        @pl.when(i == 0)
        def _():
            hext[0:CONV_HALO, :] = jnp.zeros((CONV_HALO, CONV_W), F32)

        @pl.when(i > 0)
        def _():
            hext[0:CONV_HALO, :] = hext[tm:tm + CONV_HALO, :]

        hext[CONV_HALO:CONV_HALO + tm, :] = a_ref[:, :CONV_W] * _sigmoid(a_ref[:, CONV_W:])
        _fill_shifted(hext, hsh, tm)
        for c in range(tm // rc):
            acc = jnp.zeros((rc, CONV_W), F32)
            for k in range(CONV_K):
                acc = acc + w_ref[k:k + 1, :] * _shifted_rows(hext, hsh, c * rc + 2 + k, rc)
            hc = acc + b_ref[...]
            hc_ref[c * rc:(c + 1) * rc, :] = hc
            mu = jnp.mean(hc, axis=-1, keepdims=True)
            xc = hc - mu
            var = jnp.mean(xc * xc, axis=-1, keepdims=True)
            z = xc * lax.rsqrt(var + EPS) * lg_ref[...] + lb_ref[...]
            co_ref[c * rc:(c + 1) * rc, :] = (z * _sigmoid(z)).astype(BF16)

    return _call(
        body, rider, name="fwd_conv", grid=(T // tm,),
        in_specs=[pl.BlockSpec((tm, 1024), lambda i: (i, 0)),
                  pl.BlockSpec((CONV_HALO, CONV_W), lambda i: (0, 0)),
                  pl.BlockSpec((1, CONV_W), lambda i: (0, 0)),
                  pl.BlockSpec((1, CONV_W), lambda i: (0, 0)),
                  pl.BlockSpec((1, CONV_W), lambda i: (0, 0))],
        out_specs=[pl.BlockSpec((tm, CONV_W), lambda i: (i, 0)),
                   pl.BlockSpec((tm, CONV_W), lambda i: (i, 0))],
        out_shape=[jax.ShapeDtypeStruct((T, CONV_W), BF16),
                   jax.ShapeDtypeStruct((T, CONV_W), F32)],
        scratch_shapes=[pltpu.VMEM((tm + CONV_HALO, CONV_W), F32),
                        pltpu.VMEM((SUBLANES - 1, tm + CONV_HALO - SUBLANES, CONV_W), F32)],
        compiler_params=_params(40, 1),
    )(a, cw, cb, lg, lb)


def _row_skew(v, sign):
    rows, width = v.shape
    row = lax.broadcasted_iota(jnp.int32, (rows, 1), 0)
    for b in range(int(math.log2(rows))):
        shift = (1 << b) if sign > 0 else width - (1 << b)
        v = jnp.where(((row >> b) & 1) == 1, pltpu.roll(v, shift, 1), v)
    return v


def _att_visible(d):
    rq = lax.broadcasted_iota(jnp.int32, (ATT_BLK, ATT_BLK), 0) // CHUNK
    ck = lax.broadcasted_iota(jnp.int32, (ATT_BLK, ATT_BLK), 1) // CHUNK
    slack = ATT_BLK
    above = jnp.where(d == 0, 0, slack)
    below = jnp.where(d == 2, 0, slack)
    return (ck <= rq + above) & (ck >= rq - below)


def _bias_tiles(rel, rider=None):
    vec = jnp.transpose(rel[:, _rel_index()], (1, 0, 2)).reshape(N_ATT_TILES * N_HEADS, 1, 2 * ATT_BLK)

    def body(v_ref, o_ref):
        visible = _att_visible(pl.program_id(0))
        for h in range(N_HEADS):
            full = _row_skew(jnp.broadcast_to(v_ref[h], (ATT_BLK, 2 * ATT_BLK)), 1)
            o_ref[h] = jnp.where(visible, full[:, :ATT_BLK], NEG_INF)

    return _call(
        body, rider, name="bias_tiles", grid=(N_ATT_TILES,),
        in_specs=[pl.BlockSpec((N_HEADS, 1, 2 * ATT_BLK), lambda d: (d, 0, 0))],
        out_specs=[pl.BlockSpec((N_HEADS, ATT_BLK, ATT_BLK), lambda d: (d, 0, 0))],
        out_shape=[jax.ShapeDtypeStruct((N_ATT_TILES * N_HEADS, ATT_BLK, ATT_BLK), F32)],
        compiler_params=_params(32, 1),
    )(vec)


def _diag_sums(ds, rider=None):
    def body(d_ref, o_ref):
        wide = jnp.concatenate([d_ref[0], jnp.zeros((ATT_BLK, ATT_BLK), F32)], axis=1)
        o_ref[0] = jnp.sum(_row_skew(wide, -1), axis=0, keepdims=True)

    return _call(
        body, rider, name="diag_sums", grid=(N_ATT_TILES * N_HEADS,),
        in_specs=[pl.BlockSpec((1, ATT_BLK, ATT_BLK), lambda n: (n, 0, 0))],
        out_specs=[pl.BlockSpec((1, 1, 2 * ATT_BLK), lambda n: (n, 0, 0))],
        out_shape=[jax.ShapeDtypeStruct((N_ATT_TILES * N_HEADS, 1, 2 * ATT_BLK), F32)],
        compiler_params=_params(16, 1),
    )(ds)


def _head_mask(h):
    lane = lax.broadcasted_iota(jnp.int32, (1, LANES), 1)
    return (lane // HEAD_DIM) == (h % 2)


def _fwd_attn(qkv, bias, rider=None):
    T = qkv.shape[0]
    nb = T // ATT_BLK
    scale = HEAD_DIM ** -0.5

    def body(q_ref, k0_ref, k1_ref, k2_ref, v0_ref, v1_ref, v2_ref, b_ref, o_ref, lse_ref):
        i = pl.program_id(0)

        @pl.when(i >= N_ATT_TILES - 1)
        def _():
            block(i, False, q_ref, k0_ref, k1_ref, k2_ref, v0_ref, v1_ref, v2_ref, b_ref, o_ref, lse_ref)

        @pl.when(i < N_ATT_TILES - 1)
        def _():
            block(i, True, q_ref, k0_ref, k1_ref, k2_ref, v0_ref, v1_ref, v2_ref, b_ref, o_ref, lse_ref)

    def block(i, hide_absent, q_ref, k0_ref, k1_ref, k2_ref, v0_ref, v1_ref, v2_ref, b_ref, o_ref, lse_ref):
        k_refs = (k0_ref, k1_ref, k2_ref)
        v_refs = (v0_ref, v1_ref, v2_ref)
        lane = lax.broadcasted_iota(jnp.int32, (1, LANES), 1)
        lse_tile = jnp.zeros((ATT_BLK, LANES), F32)
        for g in range(N_HEADS // 2):
            cols = slice(g * LANES, (g + 1) * LANES)
            qg = q_ref[:, cols] * scale
            og = jnp.zeros((ATT_BLK, LANES), F32)
            for h in (2 * g, 2 * g + 1):
                hm = _head_mask(h)
                qh = jnp.where(hm, qg, jnp.zeros_like(qg))
                s = []
                for d in range(N_ATT_TILES):
                    sd = _dot_nt(qh, k_refs[d][:, cols]) + b_ref[d * N_HEADS + h]
                    if d > 0 and hide_absent:
                        sd = jnp.where(i >= d, sd, NEG_INF)
                    s.append(sd)
                m = jnp.maximum(jnp.maximum(jnp.max(s[0], axis=-1, keepdims=True),
                                            jnp.max(s[1], axis=-1, keepdims=True)),
                                jnp.max(s[2], axis=-1, keepdims=True))
                p = [jnp.exp(sd - m) for sd in s]
                l = (jnp.sum(p[0], axis=-1, keepdims=True) + jnp.sum(p[1], axis=-1, keepdims=True)
                     + jnp.sum(p[2], axis=-1, keepdims=True))
                oh = jnp.zeros((ATT_BLK, LANES), F32)
                for d in range(N_ATT_TILES):
                    vg = v_refs[d][:, cols]
                    oh = oh + _dot(p[d].astype(BF16), jnp.where(hm, vg, jnp.zeros_like(vg)))
                og = og + oh / l
                lse_tile = jnp.where(lane == h, m + jnp.log(l), lse_tile)
            o_ref[:, cols] = og.astype(BF16)
        lse_ref[...] = lse_tile

    def kv_spec(d, col):
        return pl.BlockSpec((ATT_BLK, ATTN_W), lambda i: (jnp.maximum(i - d, 0), col))

    return _call(
        body, rider, name="fwd_attn", grid=(nb,),
        in_specs=[pl.BlockSpec((ATT_BLK, ATTN_W), lambda i: (i, 0)),
                  kv_spec(0, 1), kv_spec(1, 1), kv_spec(2, 1),
                  kv_spec(0, 2), kv_spec(1, 2), kv_spec(2, 2),
                  pl.BlockSpec((N_ATT_TILES * N_HEADS, ATT_BLK, ATT_BLK), lambda i: (0, 0, 0))],
        out_specs=[pl.BlockSpec((ATT_BLK, ATTN_W), lambda i: (i, 0)),
                   pl.BlockSpec((ATT_BLK, LANES), lambda i: (i, 0))],
        out_shape=[jax.ShapeDtypeStruct((T, ATTN_W), BF16),
                   jax.ShapeDtypeStruct((T, LANES), F32)],
        compiler_params=_params(40, 1),
    )(qkv, qkv, qkv, qkv, qkv, qkv, qkv, bias)


def _fwd_out_proj(co, ao, w_out, x, g2, g3):
    T = x.shape[0]
    tm = 1024

    def body(co_ref, ao_ref, w_ref, x_ref, g2_ref, g3_ref, mixed_ref, h1_ref, u2_ref):
        mixed = _dot(co_ref[...], w_ref[0:CONV_W, :]) + _dot(ao_ref[...], w_ref[CONV_W:, :])
        mixed_ref[...] = mixed.astype(BF16)
        y, _ = _rms_fwd(mixed, g2_ref[...])
        h1 = x_ref[...] + y
        h1_ref[...] = h1
        u2, _ = _rms_fwd(h1, g3_ref[...])
        u2_ref[...] = u2.astype(BF16)

    row = lambda w: pl.BlockSpec((tm, w), lambda i: (i, 0))
    vec = pl.BlockSpec((1, D_MODEL), lambda i: (0, 0))
    return _call(
        body, None, name="fwd_out_proj", grid=(T // tm,),
        in_specs=[row(CONV_W), row(ATTN_W), pl.BlockSpec((D_MODEL, D_MODEL), lambda i: (0, 0)),
                  row(D_MODEL), vec, vec],
        out_specs=[row(D_MODEL), row(D_MODEL), row(D_MODEL)],
        out_shape=[jax.ShapeDtypeStruct((T, D_MODEL), BF16),
                   jax.ShapeDtypeStruct((T, D_MODEL), F32),
                   jax.ShapeDtypeStruct((T, D_MODEL), BF16)],
        compiler_params=_params(56, 1),
    )(co, ao, w_out, x, g2, g3)[0]


GELU_C = math.sqrt(2.0 / math.pi)
GELU_A = 0.044715


def _gelu_and_grad(v):
    sq = v * v
    th = jnp.tanh(v * (GELU_C + (GELU_C * GELU_A) * sq))
    half = 0.5 + 0.5 * th
    gl = v * half
    dgl = half + (v * (half * (1.0 - th))) * (GELU_C + (3.0 * GELU_C * GELU_A) * sq)
    return gl, dgl


FF_TM = 256
FF_HALO = 16
FF_CHUNKS = [(lo, min(lo + 256, FF_SHARD)) for lo in range(0, FF_SHARD, 256)]


def _rows_before(prev, cur):
    ext = jnp.concatenate([prev, cur], axis=0)
    return pltpu.roll(ext, 1, 0)[SUBLANES:], pltpu.roll(ext, 2, 0)[SUBLANES:]


def _rows_after(cur, nxt):
    ext = jnp.concatenate([cur, nxt], axis=0)
    n = ext.shape[0]
    return pltpu.roll(ext, n - 1, 0)[:cur.shape[0]], pltpu.roll(ext, n - 2, 0)[:cur.shape[0]]


def _pair_up_weights(w_up):
    rb = 256

    def body(g_ref, v_ref, o_ref):
        for lo, hi in FF_CHUNKS:
            o_ref[0, :, 2 * lo:lo + hi] = g_ref[0, :, lo:hi]
            o_ref[0, :, lo + hi:2 * hi] = v_ref[0, :, lo:hi]

    return pl.pallas_call(
        body, name="pair_up_weights", grid=(2, D_MODEL // rb),
        in_specs=[pl.BlockSpec((1, rb, FF_SHARD), lambda s, r: (s, r, 0)),
                  pl.BlockSpec((1, rb, FF_SHARD), lambda s, r: (s + 2, r, 0))],
        out_specs=pl.BlockSpec((1, rb, 2 * FF_SHARD), lambda s, r: (s, r, 0)),
        out_shape=jax.ShapeDtypeStruct((2, D_MODEL, 2 * FF_SHARD), w_up.dtype),
        compiler_params=_params(32, 2),
    )(w_up, w_up)


def _fwd_ffn_loss(u2, w_cat, fw, fb, w_down, h1, tgt, g4):
    T = u2.shape[0]
    tm = FF_TM

    def body(u_ref, w_ref, fw_ref, fb_ref, wd_ref, h1_ref, t_ref, g_ref,
             hf_ref, pre_ref, act_ref, loss_ref, dy_ref, df_ref, dg_ref, carg, carv):
        i = pl.program_id(0)

        @pl.when(i == 0)
        def _():
            carg[...] = jnp.zeros(carg.shape, F32)
            carv[...] = jnp.zeros(carv.shape, F32)

        u = u_ref[...]
        f = None
        chunks = [(s, lo, hi) for s in range(2) for lo, hi in FF_CHUNKS]
        up = lambda s, lo, hi: _dot(u, w_ref[s, :, 2 * lo:2 * hi])
        ahead = up(*chunks[0])
        for c, (s, lo, hi) in enumerate(chunks):
            conv = []
            hs = (ahead[:, :hi - lo], ahead[:, hi - lo:])
            if c + 1 < len(chunks):
                ahead = up(*chunks[c + 1])
            at = slice(s * FF_SHARD + lo, s * FF_SHARD + hi)
            for n, car in enumerate((carg, carv)):
                h0 = hs[n]
                hf_ref[n, :, at] = h0.astype(BF16)
                h1v, h2v = _rows_before(car[:, at], h0)
                car[:, at] = h0[tm - SUBLANES:, :]
                conv.append(fw_ref[2 * n + s, 0:1, lo:hi] * h2v + fw_ref[2 * n + s, 1:2, lo:hi] * h1v
                            + fw_ref[2 * n + s, 2:3, lo:hi] * h0 + fb_ref[2 * n + s, :, lo:hi])
            pre_ref[0, :, at] = conv[0].astype(BF16)
            pre_ref[1, :, at] = conv[1].astype(BF16)
            gl, _ = _gelu_and_grad(conv[0])
            act = (gl * conv[1]).astype(BF16)
            act_ref[:, at] = act
            term = _dot(act, wd_ref[at, :])
            f = term if f is None else f + term

        r, _ = _rms_fwd(f, g_ref[...])
        e = (h1_ref[...] + r) - t_ref[...]
        dy = e * (1.0 / D_MODEL)
        dy_ref[...] = dy
        df, dg_rows = _rms_bwd(dy, f, g_ref[...])
        df_ref[...] = df.astype(BF16)
        part = 0.5 * jnp.sum(jnp.mean(e * e, axis=-1, keepdims=True), axis=0, keepdims=True)
        dg = jnp.sum(dg_rows, axis=0, keepdims=True)

        @pl.when(i == 0)
        def _():
            loss_ref[...] = part
            dg_ref[...] = dg

        @pl.when(i > 0)
        def _():
            loss_ref[...] += part
            dg_ref[...] += dg

    row = lambda w: pl.BlockSpec((tm, w), lambda i: (i, 0))
    vec = pl.BlockSpec((1, D_MODEL), lambda i: (0, 0))
    once = pl.Buffered(1)
    return _call(
        body, None, name="fwd_ffn_loss", grid=(T // tm,),
        in_specs=[row(D_MODEL),
                  pl.BlockSpec((2, D_MODEL, 2 * FF_SHARD), lambda i: (0, 0, 0), pipeline_mode=once),
                  pl.BlockSpec((N_CHIPS, FF_HALO, FF_SHARD), lambda i: (0, 0, 0)),
                  pl.BlockSpec((N_CHIPS, 1, FF_SHARD), lambda i: (0, 0, 0)),
                  pl.BlockSpec((D_FF, D_MODEL), lambda i: (0, 0), pipeline_mode=once),
                  row(D_MODEL), row(D_MODEL), vec],
        out_specs=[pl.BlockSpec((2, tm, D_FF), lambda i: (0, i, 0)),
                   pl.BlockSpec((2, tm, D_FF), lambda i: (0, i, 0)),
                   row(D_FF), pl.BlockSpec((1, 1), lambda i: (0, 0)), row(D_MODEL), row(D_MODEL), vec],
        out_shape=[jax.ShapeDtypeStruct((2, T, D_FF), BF16),
                   jax.ShapeDtypeStruct((2, T, D_FF), BF16),
                   jax.ShapeDtypeStruct((T, D_FF), BF16),
                   jax.ShapeDtypeStruct((1, 1), F32),
                   jax.ShapeDtypeStruct((T, D_MODEL), F32),
                   jax.ShapeDtypeStruct((T, D_MODEL), BF16),
                   jax.ShapeDtypeStruct((1, D_MODEL), F32)],
        scratch_shapes=[pltpu.VMEM((SUBLANES, D_FF), F32), pltpu.VMEM((SUBLANES, D_FF), F32)],
        compiler_params=_params(60, 1),
    )(u2, w_cat, fw, fb, w_down, h1, tgt, g4)[0]


def _bwd_ffn(df, hf, pre, w_cat, fw, w_down):
    T = df.shape[0]
    tm = FF_TM
    ni = T // tm

    def body(df_ref, hf_ref, pre_ref, wd_ref, w_ref, fw_ref,
             du_ref, dhf_ref, dwg_ref, dwv_ref, carg, carv):
        i = pl.program_id(0)

        @pl.when(i == 0)
        def _():
            dwg_ref[...] = jnp.zeros(dwg_ref.shape, F32)
            dwv_ref[...] = jnp.zeros(dwv_ref.shape, F32)
            carg[...] = jnp.zeros(carg.shape, F32)
            carv[...] = jnp.zeros(carv.shape, F32)

        df = df_ref[...]
        du = None
        chunks = [(s, lo, hi) for s in range(2) for lo, hi in FF_CHUNKS]
        cols = lambda s, lo, hi: slice(s * FF_SHARD + lo, s * FF_SHARD + hi)
        down = lambda s, lo, hi: _dot_nt(df, wd_ref[cols(s, lo, hi), :])
        ahead = down(*chunks[0])
        for c, (s, lo, hi) in enumerate(chunks):
            dact = ahead
            if c + 1 < len(chunks):
                ahead = down(*chunks[c + 1])
            at = cols(s, lo, hi)
            pre_g = pre_ref[0, :, at].astype(F32)
            pre_v = pre_ref[1, :, at].astype(F32)
            gl, dgl = _gelu_and_grad(pre_g)
            dpre = (dact * pre_v * dgl, dact * gl)
            dhs = []
            for n, (car, dw_ref) in enumerate(((carg, dwg_ref), (carv, dwv_ref))):
                dp = dpre[n]
                h0 = hf_ref[n, :, at].astype(F32)
                up1, up2 = _rows_after(dp, car[:, at])
                car[:, at] = dp[0:SUBLANES, :]
                for k, shifted in enumerate((up2, up1, dp)):
                    dw_ref[s, k:k + 1, lo:hi] += jnp.sum(shifted * h0, axis=0, keepdims=True)
                dw_ref[s, 3:4, lo:hi] += jnp.sum(dp, axis=0, keepdims=True)
                taps = 2 * n + s
                dh = (fw_ref[taps, 2:3, lo:hi] * dp + fw_ref[taps, 1:2, lo:hi] * up1
                      + fw_ref[taps, 0:1, lo:hi] * up2).astype(BF16)
                dhf_ref[n, :, at] = dh
                dhs.append(dh)
            term = _dot_nt(jnp.concatenate(dhs, axis=1), w_ref[s, :, 2 * lo:2 * hi])
            du = term if du is None else du + term
        du_ref[...] = du.astype(BF16)

    rev = lambda i: ni - 1 - i
    once = pl.Buffered(1)
    dwspec = pl.BlockSpec((2, FF_HALO, FF_SHARD), lambda i: (0, 0, 0))
    return _call(
        body, None, name="bwd_ffn", grid=(ni,),
        in_specs=[pl.BlockSpec((tm, D_MODEL), lambda i: (rev(i), 0)),
                  pl.BlockSpec((2, tm, D_FF), lambda i: (0, rev(i), 0)),
                  pl.BlockSpec((2, tm, D_FF), lambda i: (0, rev(i), 0)),
                  pl.BlockSpec((D_FF, D_MODEL), lambda i: (0, 0), pipeline_mode=once),
                  pl.BlockSpec((2, D_MODEL, 2 * FF_SHARD), lambda i: (0, 0, 0), pipeline_mode=once),
                  pl.BlockSpec((N_CHIPS, FF_HALO, FF_SHARD), lambda i: (0, 0, 0))],
        out_specs=[pl.BlockSpec((tm, D_MODEL), lambda i: (rev(i), 0)),
                   pl.BlockSpec((2, tm, D_FF), lambda i: (0, rev(i), 0)),
                   dwspec, dwspec],
        out_shape=[jax.ShapeDtypeStruct((T, D_MODEL), BF16),
                   jax.ShapeDtypeStruct((2, T, D_FF), BF16),
                   jax.ShapeDtypeStruct((2, FF_HALO, FF_SHARD), F32),
                   jax.ShapeDtypeStruct((2, FF_HALO, FF_SHARD), F32)],
        scratch_shapes=[pltpu.VMEM((SUBLANES, D_FF), F32), pltpu.VMEM((SUBLANES, D_FF), F32)],
        compiler_params=_params(60, 1),
    )(df, hf, pre, w_down, w_cat, fw)[0]


def _bwd_mid(du2p, dy, h1, mixed, g3, g2, w_out, rider=None):
    T = dy.shape[0]
    tm = 512

    def body(du_ref, dy_ref, h1_ref, mx_ref, g3_ref, g2_ref, w_ref,
             dh1_ref, dmx_ref, dco_ref, dao_ref, dg3_ref, dg2_ref):
        i = pl.program_id(0)
        dres, dg3_rows = _rms_bwd(du_ref[...].astype(F32), h1_ref[...], g3_ref[...])
        dh1 = dy_ref[...] + dres
        dh1_ref[...] = dh1
        dmx, dg2_rows = _rms_bwd(dh1, mx_ref[...].astype(F32), g2_ref[...])
        dmx = dmx.astype(BF16)
        dmx_ref[...] = dmx
        dcat = _dot_nt(dmx, w_ref[...])
        dco_ref[...] = dcat[:, :CONV_W]
        dao_ref[...] = dcat[:, CONV_W:].astype(BF16)
        dg3 = jnp.sum(dg3_rows, axis=0, keepdims=True)
        dg2 = jnp.sum(dg2_rows, axis=0, keepdims=True)

        @pl.when(i == 0)
        def _():
            dg3_ref[...] = dg3
            dg2_ref[...] = dg2

        @pl.when(i > 0)
        def _():
            dg3_ref[...] += dg3
            dg2_ref[...] += dg2

    row = lambda w: pl.BlockSpec((tm, w), lambda i: (i, 0))
    vec = pl.BlockSpec((1, D_MODEL), lambda i: (0, 0))
    return _call(
        body, rider, name="bwd_mid", grid=(T // tm,),
        in_specs=[row(D_MODEL), row(D_MODEL), row(D_MODEL),
                  row(D_MODEL), vec, vec, pl.BlockSpec((D_MODEL, D_MODEL), lambda i: (0, 0))],
        out_specs=[row(D_MODEL), row(D_MODEL), row(CONV_W), row(ATTN_W), vec, vec],
        out_shape=[jax.ShapeDtypeStruct((T, D_MODEL), F32),
                   jax.ShapeDtypeStruct((T, D_MODEL), BF16),
                   jax.ShapeDtypeStruct((T, CONV_W), F32),
                   jax.ShapeDtypeStruct((T, ATTN_W), BF16),
                   jax.ShapeDtypeStruct((1, D_MODEL), F32),
                   jax.ShapeDtypeStruct((1, D_MODEL), F32)],
        compiler_params=_params(48, 1),
    )(du2p, dy, h1, mixed, g3, g2, w_out)


def _bwd_attn(qkv, ao, dao, lse, bias, rider=None):
    T = qkv.shape[0]
    nb = T // ATT_BLK
    scale = HEAD_DIM ** -0.5

    def body(k_ref, v_ref, q0, q1, q2, do0, do1, do2, o0, o1, o2, l0, l1, l2, b_ref,
             dp_ref, ds_ref, acc1, acc2):
        j = pl.program_id(0)
        q_refs, do_refs, o_refs, l_refs = (q0, q1, q2), (do0, do1, do2), (o0, o1, o2), (l0, l1, l2)

        @pl.when(j == 0)
        def _():
            ds_ref[...] = jnp.zeros(ds_ref.shape, F32)
            acc1[...] = jnp.zeros(acc1.shape, F32)
            acc2[...] = jnp.zeros(acc2.shape, F32)

        dq_new = [[], [], []]
        dk_cols, dv_cols = [], []
        for g in range(N_HEADS // 2):
            cols = slice(g * LANES, (g + 1) * LANES)
            kg = k_ref[:, cols]
            vg = v_ref[:, cols]
            dkg = jnp.zeros((ATT_BLK, LANES), F32)
            dvg = jnp.zeros((ATT_BLK, LANES), F32)
            dqg = [jnp.zeros((ATT_BLK, LANES), F32) for _ in range(N_ATT_TILES)]
            for d in range(N_ATT_TILES):
                qg = q_refs[d][:, cols] * scale
                dog = do_refs[d][:, cols]
                if d > 0:
                    dog = jnp.where(j + d < nb, dog, jnp.zeros_like(dog))
                prod = dog.astype(F32) * o_refs[d][:, cols].astype(F32)
                for h in (2 * g, 2 * g + 1):
                    hm = _head_mask(h)
                    qh = jnp.where(hm, qg, jnp.zeros_like(qg))
                    doh = jnp.where(hm, dog, jnp.zeros_like(dog))
                    kh = jnp.where(hm, kg, jnp.zeros_like(kg))
                    delta = jnp.sum(jnp.where(hm, prod, 0.0), axis=-1, keepdims=True)
                    s = _dot_nt(qh, kg) + b_ref[d * N_HEADS + h]
                    p = jnp.exp(s - l_refs[d][:, h:h + 1])
                    dvg = dvg + _dot_tn(p.astype(BF16), doh)
                    dpm = _dot_nt(doh, vg)
                    dsc = p * (dpm - delta)
                    ds_ref[d * N_HEADS + h] += dsc
                    dsb = dsc.astype(BF16)
                    dqg[d] = dqg[d] + _dot(dsb, kh)
                    dkg = dkg + _dot_tn(dsb, qh)
            for d in range(N_ATT_TILES):
                dq_new[d].append(dqg[d])
            dk_cols.append(dkg)
            dv_cols.append(dvg)
        x0, x1, x2 = (jnp.concatenate(c, axis=1) * scale for c in dq_new)
        dp_ref[:, 0:1024] = jnp.zeros((ATT_BLK, 1024), BF16)
        dp_ref[:, 1024:1536] = (acc1[...] + x0).astype(BF16)
        dp_ref[:, 1536:2048] = jnp.concatenate(dk_cols, axis=1).astype(BF16)
        dp_ref[:, 2048:2560] = jnp.concatenate(dv_cols, axis=1).astype(BF16)
        acc1[...] = acc2[...] + x1
        acc2[...] = x2

    def fwd_spec(d, width, col):
        return pl.BlockSpec((ATT_BLK, width), lambda j: (jnp.minimum(j + d, nb - 1), col))

    return _call(
        body, rider, name="bwd_attn", grid=(nb,),
        in_specs=[pl.BlockSpec((ATT_BLK, ATTN_W), lambda j: (j, 1)),
                  pl.BlockSpec((ATT_BLK, ATTN_W), lambda j: (j, 2)),
                  fwd_spec(0, ATTN_W, 0), fwd_spec(1, ATTN_W, 0), fwd_spec(2, ATTN_W, 0),
                  fwd_spec(0, ATTN_W, 0), fwd_spec(1, ATTN_W, 0), fwd_spec(2, ATTN_W, 0),
                  fwd_spec(0, ATTN_W, 0), fwd_spec(1, ATTN_W, 0), fwd_spec(2, ATTN_W, 0),
                  fwd_spec(0, LANES, 0), fwd_spec(1, LANES, 0), fwd_spec(2, LANES, 0),
                  pl.BlockSpec((N_ATT_TILES * N_HEADS, ATT_BLK, ATT_BLK), lambda j: (0, 0, 0))],
        out_specs=[pl.BlockSpec((ATT_BLK, IN_COLS), lambda j: (j, 0)),
                   pl.BlockSpec((N_ATT_TILES * N_HEADS, ATT_BLK, ATT_BLK), lambda j: (0, 0, 0))],
        out_shape=[jax.ShapeDtypeStruct((T, IN_COLS), BF16),
                   jax.ShapeDtypeStruct((N_ATT_TILES * N_HEADS, ATT_BLK, ATT_BLK), F32)],
        scratch_shapes=[pltpu.VMEM((ATT_BLK, ATTN_W), F32), pltpu.VMEM((ATT_BLK, ATTN_W), F32)],
        compiler_params=_params(56, 1),
    )(qkv, qkv, qkv, qkv, qkv, dao, dao, dao, ao, ao, ao, lse, lse, lse, bias)


def _bwd_conv(dproj, a, dco, hc, cw, lg, lb, rider=None):
    T = a.shape[0]
    tm = 512
    rc = 32
    ni = T // tm
    hb = tm // CONV_HALO

    def body(dp_in, a_ref, ap_ref, dco_ref, dcon_ref, hc_ref, hcn_ref, w_ref, lg_ref, lb_ref,
             dp_ref, dw_ref, db_ref, dlg_ref, dlb_ref, hext, dext, hsh, dsh, dwacc):
        del dp_in
        i = pl.program_id(0)

        def ln_bwd(dco_v, hc_v):
            mu = jnp.mean(hc_v, axis=-1, keepdims=True)
            xc = hc_v - mu
            rstd = lax.rsqrt(jnp.mean(xc * xc, axis=-1, keepdims=True) + EPS)
            xh = xc * rstd
            z = xh * lg_ref[...] + lb_ref[...]
            sg = _sigmoid(z)
            dz = dco_v * (sg * (1.0 + z * (1.0 - sg)))
            dxh = dz * lg_ref[...]
            dhc = rstd * (dxh - jnp.mean(dxh, axis=-1, keepdims=True)
                          - xh * jnp.mean(dxh * xh, axis=-1, keepdims=True))
            return dhc, dz * xh, dz

        hext[0:CONV_HALO, :] = jnp.where(i > 0, ap_ref[:, :CONV_W] * _sigmoid(ap_ref[:, CONV_W:]), 0.0)
        hext[CONV_HALO:CONV_HALO + tm, :] = a_ref[:, :CONV_W] * _sigmoid(a_ref[:, CONV_W:])
        dhc, dlg_rows, dlb_rows = ln_bwd(dco_ref[...], hc_ref[...])
        dext[0:tm, :] = dhc
        dhc_next, _, _ = ln_bwd(dcon_ref[...], hcn_ref[...])
        dext[tm:tm + CONV_HALO, :] = jnp.where(i < ni - 1, dhc_next, 0.0)

        @pl.when(i == 0)
        def _():
            dw_ref[...] = jnp.zeros(dw_ref.shape, F32)
            db_ref[...] = jnp.zeros(db_ref.shape, F32)
            dlg_ref[...] = jnp.zeros(dlg_ref.shape, F32)
            dlb_ref[...] = jnp.zeros(dlb_ref.shape, F32)

            dwacc[...] = jnp.zeros(dwacc.shape, F32)

        db_ref[...] += jnp.sum(dhc, axis=0, keepdims=True)
        dlg_ref[...] += jnp.sum(dlg_rows, axis=0, keepdims=True)
        dlb_ref[...] += jnp.sum(dlb_rows, axis=0, keepdims=True)
        _fill_shifted(hext, hsh, tm)
        _fill_shifted(dext, dsh, tm)
        for c in range(tm // rc):
            r0 = c * rc
            dh = jnp.zeros((rc, CONV_W), F32)
            dhc_c = dext[r0:r0 + rc, :]
            for k in range(CONV_K):
                dh = dh + w_ref[k:k + 1, :] * _shifted_rows(dext, dsh, r0 + 30 - k, rc)
                prod = dhc_c * _shifted_rows(hext, hsh, r0 + 2 + k, rc)
                dwacc[k] += jnp.sum(prod.reshape(rc // SUBLANES, SUBLANES, CONV_W), axis=0)
            av = a_ref[r0:r0 + rc, :CONV_W]
            sg = _sigmoid(a_ref[r0:r0 + rc, CONV_W:])
            dp_ref[r0:r0 + rc, 0:CONV_W] = (dh * sg).astype(BF16)
            dp_ref[r0:r0 + rc, CONV_W:] = (dh * av * sg * (1.0 - sg)).astype(BF16)

        @pl.when(i == ni - 1)
        def _():
            dw_ref[...] = jnp.sum(dwacc[...], axis=1)

    row = lambda w: pl.BlockSpec((tm, w), lambda i: (i, 0))
    prev = lambda w: pl.BlockSpec((CONV_HALO, w), lambda i: (jnp.maximum(i * hb - 1, 0), 0))
    nxt = lambda w: pl.BlockSpec((CONV_HALO, w), lambda i: (jnp.minimum((i + 1) * hb, ni * hb - 1), 0))
    vec = pl.BlockSpec((1, CONV_W), lambda i: (0, 0))
    return _call(
        body, rider, name="bwd_conv", grid=(ni,),
        in_specs=[ANY, row(1024), prev(1024), row(CONV_W), nxt(CONV_W), row(CONV_W), nxt(CONV_W),
                  pl.BlockSpec((CONV_HALO, CONV_W), lambda i: (0, 0)), vec, vec],
        out_specs=[pl.BlockSpec((tm, 1024), lambda i: (i, 0)),
                   pl.BlockSpec((CONV_HALO, CONV_W), lambda i: (0, 0)), vec, vec, vec],
        out_shape=[jax.ShapeDtypeStruct((T, IN_COLS), BF16),
                   jax.ShapeDtypeStruct((CONV_HALO, CONV_W), F32),
                   jax.ShapeDtypeStruct((1, CONV_W), F32),
                   jax.ShapeDtypeStruct((1, CONV_W), F32),
                   jax.ShapeDtypeStruct((1, CONV_W), F32)],
        scratch_shapes=[pltpu.VMEM((tm + CONV_HALO, CONV_W), F32), pltpu.VMEM((tm + CONV_HALO, CONV_W), F32),
                        pltpu.VMEM((SUBLANES - 1, tm + CONV_HALO - SUBLANES, CONV_W), F32),
                        pltpu.VMEM((SUBLANES - 1, tm + CONV_HALO - SUBLANES, CONV_W), F32),
                        pltpu.VMEM((CONV_HALO, SUBLANES, CONV_W), F32)],
        input_output_aliases={0: 0},
        compiler_params=_params(56, 1),
    )(dproj, a, a, dco, dco, hc, hc, cw, lg, lb)


def _bwd_in_proj(dproj, w_in, x, dh1, g1, rider=None):
    T = x.shape[0]
    tm = 1024

    def body(dp_ref, w_ref, x_ref, dh_ref, g_ref, gx_ref, dg_ref):
        i = pl.program_id(0)
        du = _dot_nt(dp_ref[...], w_ref[...])
        dx, dg_rows = _rms_bwd(du, x_ref[...], g_ref[...])
        gx_ref[...] = dh_ref[...] + dx
        dg = jnp.sum(dg_rows, axis=0, keepdims=True)

        @pl.when(i == 0)
        def _():
            dg_ref[...] = dg

        @pl.when(i > 0)
        def _():
            dg_ref[...] += dg

    row = lambda w: pl.BlockSpec((tm, w), lambda i: (i, 0))
    vec = pl.BlockSpec((1, D_MODEL), lambda i: (0, 0))
    return _call(
        body, rider, name="bwd_in_proj", grid=(T // tm,),
        in_specs=[row(IN_COLS), pl.BlockSpec((D_MODEL, IN_COLS), lambda i: (0, 0)),
                  row(D_MODEL), row(D_MODEL), vec],
        out_specs=[row(D_MODEL), vec],
        out_shape=[jax.ShapeDtypeStruct((T, D_MODEL), F32), jax.ShapeDtypeStruct((1, D_MODEL), F32)],
        compiler_params=_params(56, 1),
    )(dproj, w_in, x, dh1, g1)


def _wgrad(name, a_list, a_spec, b, b_spec, out_spec, out_shape, n_outer, T, tk, select=None, rider=None):
    def body(*refs):
        a_refs, b_ref, o_ref = refs[:len(a_list)], refs[len(a_list)], refs[len(a_list) + 1]
        kt = pl.program_id(1)

        @pl.when(kt == 0)
        def _():
            o_ref[...] = jnp.zeros(o_ref.shape, F32)

        bv = b_ref[...].reshape(b_ref.shape[-2:])
        if select is None:
            o_ref[...] += _dot_tn(a_refs[0][...].reshape(a_refs[0].shape[-2:]), bv).reshape(o_ref.shape)
        else:
            for n, a_ref in enumerate(a_refs):
                @pl.when(select(pl.program_id(0)) == n)
                def _():
                    o_ref[...] += _dot_tn(a_ref[...], bv).reshape(o_ref.shape)

    (res,), got = _call(
        body, rider, name=name, grid=(n_outer, T // tk),
        in_specs=[a_spec] * len(a_list) + [b_spec],
        out_specs=[out_spec], out_shape=[out_shape],
        compiler_params=_params(56, 2),
    )(*a_list, b)
    return (res, got) if rider is not None else res


def _mesh_pos():
    return lax.axis_index("x"), lax.axis_index("y"), lax.axis_index("c")


def _other_chips(x, y):
    return [((1 - x, y), 2 * (1 - x) + y), ((x, 1 - y), 2 * x + (1 - y)), ((1 - x, 1 - y), 2 * (1 - x) + (1 - y))]


def _exchange_rider(operands, out_shape, aliases, sem_shape, pairs):
    def start(ins, outs, sems):
        for send, _ in pairs(ins, outs, *sems):
            send.start()

    def finish(ins, outs, sems):
        for send, recv in pairs(ins, outs, *sems):
            send.wait_send()
            recv.wait_recv()

    sems = [pltpu.SemaphoreType.DMA(sem_shape), pltpu.SemaphoreType.DMA(sem_shape)]
    return _Rider(list(operands), list(out_shape), aliases, sems, start, finish)


def _remote(src, dst, send_sem, recv_sem, device):
    return pltpu.make_async_remote_copy(src_ref=src, dst_ref=dst, send_sem=send_sem, recv_sem=recv_sem,
                                        device_id=device, device_id_type=MESH)


def _fetch_rider(bufs):
    def pairs(ins, outs, send_sems, recv_sems):
        x, y, c = _mesh_pos()
        res = []
        for t, buf in enumerate(bufs):
            rows = pl.ds(c * (buf.shape[1] // 2), buf.shape[1] // 2)
            mine = outs[t].at[2 * x + y, rows]
            for k, (chip, s) in enumerate(_other_chips(x, y)):
                landed = outs[t].at[s, rows]
                res.append((_remote(mine, mine, send_sems.at[t, k], recv_sems.at[t, k], (*chip, c)),
                            _remote(landed, landed, send_sems.at[t, k], recv_sems.at[t, k], (*chip, c))))
        return res

    shapes = [jax.ShapeDtypeStruct(b.shape, b.dtype) for b in bufs]
    return _exchange_rider(bufs, shapes, {t: t for t in range(len(bufs))}, (len(bufs), 3), pairs)


def _forward_rider(bufs):
    def pairs(ins, outs, send_sems, recv_sems):
        x, y, c = _mesh_pos()
        res = []
        for t, buf in enumerate(bufs):
            half = buf.shape[1] // 2
            for k, (_, s) in enumerate(_other_chips(x, y)):
                landed = outs[t].at[s, pl.ds(c * half, half)]
                theirs = outs[t].at[s, pl.ds((1 - c) * half, half)]
                res.append((_remote(landed, landed, send_sems.at[t, k], recv_sems.at[t, k], (x, y, 1 - c)),
                            _remote(theirs, theirs, send_sems.at[t, k], recv_sems.at[t, k], (x, y, 1 - c))))
        return res

    shapes = [jax.ShapeDtypeStruct(b.shape, b.dtype) for b in bufs]
    return _exchange_rider(bufs, shapes, {t: t for t in range(len(bufs))}, (len(bufs), 3), pairs)


def _pair_exchange_rider(grads):
    def pairs(ins, outs, send_sems, recv_sems):
        x, y, c = _mesh_pos()
        res = []
        for t, g in enumerate(grads):
            half = g.shape[1] // 2
            cp = _remote(ins[t].at[:, pl.ds((1 - c) * half, half), :], outs[t], send_sems.at[t], recv_sems.at[t],
                         (x, y, 1 - c))
            res.append((cp, cp))
        return res

    shapes = [jax.ShapeDtypeStruct((N_CHIPS, g.shape[1] // 2, g.shape[2]), F32) for g in grads]
    return _exchange_rider(grads, shapes, {}, (len(grads),), pairs)


def _chip_exchange_rider(sums):
    def pairs(ins, outs, send_sems, recv_sems):
        x, y, c = _mesh_pos()
        res = []
        for t in range(len(sums)):
            for k, (chip, s) in enumerate(_other_chips(x, y)):
                cp = _remote(ins[t].at[s], outs[t].at[k], send_sems.at[t, k], recv_sems.at[t, k], (*chip, c))
                res.append((cp, cp))
        return res

    shapes = [jax.ShapeDtypeStruct((3,) + p.shape[1:], p.dtype) for p in sums]
    return _exchange_rider(sums, shapes, {}, (len(sums), 3), pairs)


def _pair_gather_rider(fulls):
    def pairs(ins, outs, send_sems, recv_sems):
        x, y, c = _mesh_pos()
        res = []
        for t, f in enumerate(fulls):
            half = f.shape[0] // 2
            mine = outs[t].at[pl.ds(c * half, half)]
            theirs = outs[t].at[pl.ds((1 - c) * half, half)]
            res.append((_remote(mine, mine, send_sems.at[t], recv_sems.at[t], (x, y, 1 - c)),
                        _remote(theirs, theirs, send_sems.at[t], recv_sems.at[t], (x, y, 1 - c))))
        return res

    shapes = [jax.ShapeDtypeStruct(f.shape, F32) for f in fulls]
    return _exchange_rider(fulls, shapes, {t: t for t in range(len(fulls))}, (len(fulls),), pairs)


def _alone(name, rider):
    return _call(lambda: None, rider, name=name)()[1]


def _all_reduce_small(pack, rider=None):
    rows = pack.shape[0]

    def body(p_ref, o_ref, buf, send_sems, recv_sems):
        x, y, c = _mesh_pos()
        me = 4 * x + 2 * y + c
        buf[0] = p_ref[...]
        copies = []
        for k in range(1, 8):
            peer = (x ^ (k >> 2), y ^ ((k >> 1) & 1), c ^ (k & 1))
            cp = pltpu.make_async_remote_copy(
                src_ref=p_ref, dst_ref=buf.at[k], send_sem=send_sems.at[k - 1], recv_sem=recv_sems.at[k - 1],
                device_id=peer, device_id_type=MESH)
            cp.start()
            copies.append(cp)
        for cp in copies:
            cp.wait()
        total = buf[me]
        for dev in range(1, 8):
            total = total + buf[me ^ dev]
        o_ref[...] = total

    return _call(
        body, rider, name="all_reduce_small",
        in_specs=[VMEM_FULL], out_specs=[VMEM_FULL],
        out_shape=[jax.ShapeDtypeStruct(pack.shape, F32)],
        scratch_shapes=[pltpu.VMEM((8, rows, LANES), F32),
                        pltpu.SemaphoreType.DMA((7,)), pltpu.SemaphoreType.DMA((7,))],
    )(pack)


def _row_block(rows):
    if rows <= 512:
        return rows
    for rb in (256, 352):
        if rows % rb == 0:
            return rb
    raise ValueError(f"no row block for {rows} rows")


def _place(name, w, pos, dtype):
    R, C = w.shape
    rb = _row_block(R)

    def body(pos_ref, w_ref, o_ref):
        del pos_ref
        o_ref[0] = w_ref[...].astype(dtype)

    return pl.pallas_call(
        body, name=name,
        grid_spec=pltpu.PrefetchScalarGridSpec(
            num_scalar_prefetch=1, grid=(R // rb,),
            in_specs=[pl.BlockSpec((rb, C), lambda r, p: (r, 0))],
            out_specs=pl.BlockSpec((1, rb, C), lambda r, p: (p[0], r, 0))),
        out_shape=(pltpu.HBM if N_CHIPS * R * C * jnp.dtype(dtype).itemsize >= PIN_BYTES
                   else jax.ShapeDtypeStruct)((N_CHIPS, R, C), dtype),
        compiler_params=_params(32, 1),
    )(pos, w)


def _pair_sum(name, g, got, pos):
    S, R, C = g.shape
    half = R // 2
    rb = _row_block(half)
    nh = half // rb

    def body(pos_ref, a_ref, b_ref, o_ref):
        del pos_ref
        o_ref[...] = (a_ref[...] + b_ref[...]).astype(BF16)

    spec = pl.BlockSpec((1, rb, C), lambda s, r, p: (s, r, 0))
    return pl.pallas_call(
        body, name=name,
        grid_spec=pltpu.PrefetchScalarGridSpec(
            num_scalar_prefetch=1, grid=(S, nh),
            in_specs=[pl.BlockSpec((1, rb, C), lambda s, r, p: (s, p[1] * nh + r, 0)), spec],
            out_specs=spec),
        out_shape=jax.ShapeDtypeStruct((S, half, C), BF16), compiler_params=_params(32, 2),
    )(pos, g, got)


def _chip_sum(name, pairs, got, pos):
    _, half, C = pairs.shape
    rb = _row_block(half)
    nh = half // rb

    def body(pos_ref, a_ref, g_ref, o_ref):
        del pos_ref
        o_ref[...] = ((a_ref[0].astype(F32) + g_ref[0].astype(F32)) + g_ref[1].astype(F32)) + g_ref[2].astype(F32)

    return pl.pallas_call(
        body, name=name,
        grid_spec=pltpu.PrefetchScalarGridSpec(
            num_scalar_prefetch=1, grid=(nh,),
            in_specs=[pl.BlockSpec((1, rb, C), lambda r, p: (p[0], r, 0)),
                      pl.BlockSpec((3, rb, C), lambda r, p: (0, r, 0))],
            out_specs=pl.BlockSpec((rb, C), lambda r, p: (p[1] * nh + r, 0))),
        out_shape=jax.ShapeDtypeStruct((2 * half, C), F32), compiler_params=_params(32, 1),
    )(pos, pairs, got)


def _adamw(name, w, g, m, v):
    R, C = w.shape
    rb = _row_block(R)
    c1 = 1.0 - ADAM_B1 ** ADAM_STEP
    c2 = 1.0 - ADAM_B2 ** ADAM_STEP

    def body(w_ref, g_ref, m_ref, v_ref, d_ref, nm_ref, nv_ref):
        gv = g_ref[...]
        nm = ADAM_B1 * m_ref[...] + (1.0 - ADAM_B1) * gv
        nv = ADAM_B2 * v_ref[...] + (1.0 - ADAM_B2) * (gv * gv)
        nm_ref[...] = nm
        nv_ref[...] = nv
        d_ref[...] = -ADAM_LR * ((nm / c1) / (jnp.sqrt(nv / c2) + ADAM_EPS) + ADAM_WD * w_ref[...])

    spec = pl.BlockSpec((rb, C), lambda r: (r, 0))
    sds = jax.ShapeDtypeStruct(w.shape, F32)
    return pl.pallas_call(
        body, name=name, grid=(R // rb,), in_specs=[spec] * 4, out_specs=[spec] * 3,
        out_shape=[sds, sds, sds], compiler_params=_params(40, 1),
    )(w, g, m, v)


def _rel_index():
    m = np.arange(2 * ATT_BLK)
    off = np.where(m < ATT_BLK, m, m - 2 * ATT_BLK)
    rel = np.stack([ATT_BLK * d - off for d in range(N_ATT_TILES)])
    return np.clip(rel, -MAX_REL, MAX_REL) + MAX_REL


def _local_step(x, tgt, g1, w_in, cw, cb, lg, lb, bias, w_out, g2, g3, w_up, fw, fb, w_down, g4, pos=None):
    T = x.shape[0]
    dist = pos is not None
    idx = _rel_index()

    (u, a, qkv), got = _fwd_in_proj(x, g1, w_in, _fetch_rider([w_out, w_down]) if dist else None)
    if dist:
        w_out, w_down = got
    (co, hc), got = _fwd_conv(a, cw, cb, lg, lb, _merge_riders(_forward_rider([w_out, w_down]),
                                                               _fetch_rider([w_up])) if dist else None)
    if dist:
        w_out, w_down, w_up = got
    (ao, lse), got = _fwd_attn(qkv, bias, _forward_rider([w_up]) if dist else None)
    if dist:
        (w_up,) = got
        w_out, w_down = w_out.reshape(D_MODEL, D_MODEL), w_down.reshape(D_FF, D_MODEL)
    mixed, h1, u2 = _fwd_out_proj(co, ao, w_out, x, g2, g3)
    w_cat = _pair_up_weights(w_up)
    hf, pre, act, loss, dy, df, dg4 = _fwd_ffn_loss(u2, w_cat, fw, fb, w_down, h1, tgt, g4)

    tk = min(2048, T)
    du2p, dhf, dfw_g, dfw_v = _bwd_ffn(df, hf, pre, w_cat, fw, w_down)
    gw_up = _wgrad(
        "wgrad_up", [u2], pl.BlockSpec((tk, D_MODEL), lambda s, k: (k, 0)),
        dhf, pl.BlockSpec((1, tk, FF_SHARD), lambda s, k: (s // 2, k, s % 2)),
        pl.BlockSpec((1, D_MODEL, FF_SHARD), lambda s, k: (s, 0, 0)),
        jax.ShapeDtypeStruct((N_CHIPS, D_MODEL, FF_SHARD), F32), N_CHIPS, T, tk)
    gw_down = _wgrad(
        "wgrad_down", [act], pl.BlockSpec((tk, FF_SHARD), lambda s, k: (k, s)),
        df, pl.BlockSpec((tk, D_MODEL), lambda s, k: (k, 0)),
        pl.BlockSpec((FF_SHARD, D_MODEL), lambda s, k: (s, 0)),
        jax.ShapeDtypeStruct((D_FF, D_MODEL), F32), 2, T, tk).reshape(N_CHIPS, D_FF // N_CHIPS, D_MODEL)
    (dh1, dmx, dco, dao, dg3, dg2), _ = _bwd_mid(du2p, dy, h1, mixed, g3, g2, w_out)
    gw_out = _wgrad(
        "wgrad_out", [co, ao], pl.BlockSpec((tk, CONV_W), lambda s, k: (k, 0)),
        dmx, pl.BlockSpec((tk, D_MODEL), lambda s, k: (k, 0)),
        pl.BlockSpec((CONV_W, D_MODEL), lambda s, k: (s, 0)),
        jax.ShapeDtypeStruct((D_MODEL, D_MODEL), F32), 2, T, tk,
        select=lambda s: s, rider=_pair_exchange_rider([gw_up, gw_down]) if dist else None)
    if dist:
        gw_out, got = gw_out
        p_up = _pair_sum("pair_sum_w_up", gw_up, got[0], pos)
        p_down = _pair_sum("pair_sum_w_down", gw_down, got[1], pos)
    gw_out = gw_out.reshape(N_CHIPS, D_MODEL // N_CHIPS, D_MODEL)
    (dproj, dsacc), got = _bwd_attn(
        qkv, ao, dao, lse, bias,
        _merge_riders(_chip_exchange_rider([p_up, p_down]), _pair_exchange_rider([gw_out])) if dist else None)
    if dist:
        gw_up = _chip_sum("chip_sum_w_up", p_up, got[0], pos)
        gw_down = _chip_sum("chip_sum_w_down", p_down, got[1], pos)
        p_out = _pair_sum("pair_sum_w_out", gw_out, got[2], pos)
    (dproj, dcw, dcb, dlg, dlb), got = _bwd_conv(
        dproj, a, dco, hc, cw, lg, lb,
        _merge_riders(_pair_gather_rider([gw_up, gw_down]), _chip_exchange_rider([p_out])) if dist else None)
    if dist:
        gw_up, gw_down = got[:2]
        gw_out = _chip_sum("chip_sum_w_out", p_out, got[2], pos)
    gw_in = _wgrad(
        "wgrad_in", [u], pl.BlockSpec((tk, D_MODEL), lambda s, k: (k, 0)),
        dproj, pl.BlockSpec((tk, IN_SHARD), lambda s, k: (k, s)),
        pl.BlockSpec((1, D_MODEL, IN_SHARD), lambda s, k: (s, 0, 0)),
        jax.ShapeDtypeStruct((N_CHIPS, D_MODEL, IN_SHARD), F32), N_CHIPS, T, tk)
    if dist:
        got = _alone("pair_exchange_w_in", _merge_riders(_pair_exchange_rider([gw_in]), _pair_gather_rider([gw_out])))
        p_in, gw_out = _pair_sum("pair_sum_w_in", gw_in, got[0], pos), got[1]
    (gx, dg1), _ = _bwd_in_proj(dproj, w_in, x, dh1, g1)
    (diag,), got = _diag_sums(dsacc, _chip_exchange_rider([p_in]) if dist else None)
    if dist:
        gw_in = _chip_sum("chip_sum_w_in", p_in, got[0], pos)

    diag = diag.reshape(N_ATT_TILES, N_HEADS, 2 * ATT_BLK)
    onehot = np.zeros((N_ATT_TILES, 2 * ATT_BLK, 2 * MAX_REL + 1), np.float32)
    for d in range(N_ATT_TILES):
        onehot[d, np.arange(2 * ATT_BLK), idx[d]] = 1.0
    drel = jnp.einsum("dhm,dmr->hr", diag, jnp.asarray(onehot), precision=lax.Precision.HIGHEST)

    small = dict(norm_mix_pre=dg1, conv_dw_w=dcw[:CONV_K], conv_dw_b=dcb, conv_ln_g=dlg, conv_ln_b=dlb,
                 rel_bias=drel, norm_mix_post=dg2, norm_ffn_pre=dg3,
                 ffn_dw_w=jnp.concatenate([dfw_g[0, :3], dfw_g[1, :3], dfw_v[0, :3], dfw_v[1, :3]], axis=1),
                 ffn_dw_b=jnp.concatenate([dfw_g[0, 3:4], dfw_g[1, 3:4], dfw_v[0, 3:4], dfw_v[1, 3:4]], axis=1),
                 norm_ffn_post=dg4)
    return loss, gx, small, dict(w_in=gw_in, w_out=gw_out, w_up=gw_up, w_down=gw_down)


SMALL_ORDER = ["norm_mix_pre", "conv_dw_b", "conv_ln_g", "conv_ln_b", "rel_bias", "norm_mix_post",
               "norm_ffn_pre", "ffn_dw_b", "norm_ffn_post", "conv_dw_w", "ffn_dw_w"]


def _pack(parts):
    rows = []
    for p in parts:
        width = -(-p.shape[1] // LANES) * LANES
        rows.append(jnp.pad(p, ((0, 0), (0, width - p.shape[1]))).reshape(-1, LANES))
    packed = jnp.concatenate(rows, axis=0)
    pad = -packed.shape[0] % 8
    return jnp.pad(packed, ((0, pad), (0, 0)))


def _unpack(packed, shapes):
    out, r = [], 0
    for shp in shapes:
        width = -(-shp[1] // LANES) * LANES
        n = shp[0] * width // LANES
        out.append(packed[r:r + n].reshape(shp[0], width)[:, :shp[1]])
        r += n
    return out


WEIGHTS = ["norm_mix_pre", "w_in", "conv_dw_w", "conv_dw_b", "conv_ln_g", "conv_ln_b", "rel_bias", "w_out",
           "norm_mix_post", "norm_ffn_pre", "w_up", "ffn_dw_w", "ffn_dw_b", "w_down", "norm_ffn_post"]
BIG = ["w_in", "w_out", "w_up", "w_down"]


def kernel(x, norm_mix_pre, w_in, conv_dw_w, conv_dw_b, conv_ln_g, conv_ln_b, rel_bias, w_out, norm_mix_post, norm_ffn_pre, w_up, ffn_dw_w, ffn_dw_b, w_down, norm_ffn_post, loss_target, m_norm_mix_pre, m_w_in, m_conv_dw_w, m_conv_dw_b, m_conv_ln_g, m_conv_ln_b, m_rel_bias, m_w_out, m_norm_mix_post, m_norm_ffn_pre, m_w_up, m_ffn_dw_w, m_ffn_dw_b, m_w_down, m_norm_ffn_post, v_norm_mix_pre, v_w_in, v_conv_dw_w, v_conv_dw_b, v_conv_ln_g, v_conv_ln_b, v_rel_bias, v_w_out, v_norm_mix_post, v_norm_ffn_pre, v_w_up, v_ffn_dw_w, v_ffn_dw_b, v_w_down, v_norm_ffn_post):
    args = locals()
    w = {n: args[n][0] for n in WEIGHTS}
    m = {n: args["m_" + n][0] for n in WEIGHTS}
    v = {n: args["v_" + n][0] for n in WEIGHTS}
    for d in (w, m, v):
        d["rel_bias"] = d["rel_bias"].reshape(N_HEADS, 2 * MAX_REL + 1)
        for n in ("norm_mix_pre", "conv_dw_b", "conv_ln_g", "conv_ln_b", "norm_mix_post", "norm_ffn_pre",
                  "ffn_dw_b", "norm_ffn_post"):
            d[n] = d[n].reshape(1, -1)
    shard = 2 * lax.axis_index("x") + lax.axis_index("y")

    cw_sh = jnp.pad(w["conv_dw_w"], ((0, CONV_HALO - CONV_K), (0, 0)))
    fw_sh = jnp.pad(w["ffn_dw_w"], ((0, FF_HALO - 3), (0, 0)))
    pos = jnp.stack([shard, lax.axis_index("c")]).astype(jnp.int32)
    bufs = {n: _place("place_" + n, w[n], pos, BF16) for n in BIG}
    first = [bufs["w_in"], _place("place_conv_dw_w", cw_sh, pos, F32), _place("place_ffn_dw_w", fw_sh, pos, F32)]
    (bias,), first = _bias_tiles(w["rel_bias"], _fetch_rider(first))
    w_in_f, cw_f, fw_f = _alone("all_gather_forward", _forward_rider(list(first)))
    cw_full = jnp.transpose(cw_f, (1, 0, 2)).reshape(CONV_HALO, CONV_W)

    loss, gx, small, big = _local_step(
        x[0], loss_target[0], w["norm_mix_pre"], _join_columns(w_in_f), cw_full, w["conv_dw_b"], w["conv_ln_g"],
        w["conv_ln_b"], bias, bufs["w_out"], w["norm_mix_post"],
        w["norm_ffn_pre"], bufs["w_up"], fw_f, w["ffn_dw_b"].reshape(N_CHIPS, 1, FF_SHARD),
        bufs["w_down"], w["norm_ffn_post"], pos)
    (gsum,), (big["w_in"],) = _all_reduce_small(_pack([small[n] for n in SMALL_ORDER] + [loss]),
                                                _pair_gather_rider([big["w_in"]]))

    grads, deltas, new_m, new_v = {}, {}, {}, {}
    for n in BIG:
        grads[n] = big[n]
        deltas[n], new_m[n], new_v[n] = _adamw("adamw_" + n, w[n], big[n], m[n], v[n])
    shapes = [small[n].shape for n in SMALL_ORDER]
    *reduced, total = _unpack(gsum, shapes + [loss.shape])
    gs = dict(zip(SMALL_ORDER, reduced))
    gs["conv_dw_w"] = lax.dynamic_slice_in_dim(gs["conv_dw_w"], shard * LANES, LANES, axis=1)
    gs["ffn_dw_w"] = lax.dynamic_slice_in_dim(gs["ffn_dw_w"], shard * FF_SHARD, FF_SHARD, axis=1)
    shapes = [gs[n].shape for n in SMALL_ORDER]
    d_p, m_p, v_p = _adamw("adamw_small", _pack([w[n] for n in SMALL_ORDER]), _pack([gs[n] for n in SMALL_ORDER]),
                           _pack([m[n] for n in SMALL_ORDER]), _pack([v[n] for n in SMALL_ORDER]))
    for dst, packed in ((deltas, d_p), (new_m, m_p), (new_v, v_p)):
        dst.update(zip(SMALL_ORDER, _unpack(packed, shapes)))
    grads.update(gs)

    outs = [total[0, 0], gx[None]]
    for group in (grads, deltas, new_m, new_v):
        outs += [group[n].reshape(args[n].shape) for n in WEIGHTS]
    return tuple(outs)
```

```python
import functools
import math
from typing import Callable, NamedTuple

import numpy as np
import jax
import jax.numpy as jnp
from jax import lax
from jax.experimental import pallas as pl
from jax.experimental.pallas import tpu as pltpu

F32 = jnp.float32
BF16 = jnp.bfloat16

D_MODEL = 1024
CONV_W = 512
ATTN_W = 512
N_HEADS = 8
HEAD_DIM = 64
CHUNK = 64
N_LEFT = 8
MAX_REL = 128
CONV_K = 31
CONV_HALO = 32
D_FF = 2816
FF_SHARD = 1408
IN_COLS = 2560
IN_SHARD = 640
EPS = 1e-6
NEG_INF = -1e30
ATT_BLK = 256
N_ATT_TILES = 3
LANES = 128
SUBLANES = 8
N_CHIPS = 4

ADAM_LR = 0.001
ADAM_B1 = 0.9
ADAM_B2 = 0.999
ADAM_EPS = 1e-08
ADAM_WD = 0.01
ADAM_STEP = 10

MESH = pl.DeviceIdType.MESH
ANY = pl.BlockSpec(memory_space=pl.ANY)
VMEM_FULL = pl.BlockSpec(memory_space=pltpu.VMEM)


def _params(vmem_mb, n_grid=0):
    sem = ("arbitrary",) * n_grid if n_grid else None
    return pltpu.CompilerParams(dimension_semantics=sem, vmem_limit_bytes=vmem_mb << 20)


class _Rider(NamedTuple):
    operands: list
    out_shape: list
    aliases: dict
    sems: list
    start: Callable
    finish: Callable


def _merge_riders(a, b):
    ia, oa, sa = len(a.operands), len(a.out_shape), len(a.sems)

    def start(ins, outs, sems):
        a.start(ins[:ia], outs[:oa], sems[:sa])
        b.start(ins[ia:], outs[oa:], sems[sa:])

    def finish(ins, outs, sems):
        a.finish(ins[:ia], outs[:oa], sems[:sa])
        b.finish(ins[ia:], outs[oa:], sems[sa:])

    aliases = {**a.aliases, **{k + ia: v + oa for k, v in b.aliases.items()}}
    return _Rider(a.operands + b.operands, a.out_shape + b.out_shape, aliases, a.sems + b.sems, start, finish)


PIN_BYTES = 1 << 20


def _big(a):
    return math.prod(a.shape) * jnp.dtype(a.dtype).itemsize >= PIN_BYTES


def _pin_args(args):
    return [pltpu.with_memory_space_constraint(a, pltpu.HBM) if _big(a) else a for a in args]


def _call(body, rider, *, grid=(), in_specs=(), out_specs=(), out_shape=(), scratch_shapes=(),
          input_output_aliases=None, **kwargs):
    in_specs, out_specs = list(in_specs), list(out_specs)
    pin_out = lambda shapes: [pltpu.HBM(s.shape, s.dtype) if _big(s) else s for s in shapes]
    out_shape = pin_out(out_shape)
    scratch, aliases = list(scratch_shapes), dict(input_output_aliases or {})
    if rider is None:
        plain = pl.pallas_call(body, grid=grid, in_specs=in_specs, out_specs=out_specs, out_shape=out_shape,
                               scratch_shapes=scratch, input_output_aliases=aliases, **kwargs)
        return lambda *args: (plain(*_pin_args(args)), [])
    n_in, n_out, n_scr = len(in_specs), len(out_specs), len(scratch)
    r_in, r_out = len(rider.operands), len(rider.out_shape)

    def carried(*refs):
        ins, r_ins, refs = refs[:n_in], refs[n_in:n_in + r_in], refs[n_in + r_in:]
        outs, r_outs, refs = refs[:n_out], refs[n_out:n_out + r_out], refs[n_out + r_out:]
        scr, r_sems = refs[:n_scr], refs[n_scr:]
        if not grid:
            rider.start(r_ins, r_outs, r_sems)
            body(*ins, *outs, *scr)
            rider.finish(r_ins, r_outs, r_sems)
            return
        at = [pl.program_id(d) for d in range(len(grid))]
        first = functools.reduce(jnp.logical_and, [p == 0 for p in at])
        last = functools.reduce(jnp.logical_and, [p == n - 1 for p, n in zip(at, grid)])

        @pl.when(first)
        def _():
            rider.start(r_ins, r_outs, r_sems)

        body(*ins, *outs, *scr)

        @pl.when(last)
        def _():
            rider.finish(r_ins, r_outs, r_sems)

    aliases.update({n_in + k: n_out + v for k, v in rider.aliases.items()})
    both = pl.pallas_call(carried, grid=grid, in_specs=in_specs + [ANY] * r_in, out_specs=out_specs + [ANY] * r_out,
                          out_shape=out_shape + pin_out(rider.out_shape), scratch_shapes=scratch + rider.sems,
                          input_output_aliases=aliases, **kwargs)

    def run(*args):
        res = both(*_pin_args(args), *rider.operands)
        return res[:n_out], res[n_out:]

    return run


def _sigmoid(v):
    return 1.0 / (1.0 + jnp.exp(-v))


def _dot(a, b):
    return jnp.dot(a, b, preferred_element_type=F32)


def _dot_nt(a, b):
    return lax.dot_general(a, b, (((1,), (1,)), ((), ())), preferred_element_type=F32)


def _dot_tn(a, b):
    return lax.dot_general(a, b, (((0,), (0,)), ((), ())), preferred_element_type=F32)


def _rms_fwd(v, g):
    r = lax.rsqrt(jnp.mean(v * v, axis=-1, keepdims=True) + EPS)
    return v * r * g, r


def _rms_bwd(dy, v, g):
    r = lax.rsqrt(jnp.mean(v * v, axis=-1, keepdims=True) + EPS)
    vh = v * r
    dvh = dy * g
    dv = r * (dvh - vh * jnp.mean(dvh * vh, axis=-1, keepdims=True))
    return dv, dy * vh


def _join_columns(w):
    S, R, C = w.shape
    rb = 256

    def body(w_ref, o_ref):
        for s in range(S):
            o_ref[:, s * C:(s + 1) * C] = w_ref[s]

    return pl.pallas_call(
        body, name="join_columns", grid=(R // rb,),
        in_specs=[pl.BlockSpec((S, rb, C), lambda r: (0, r, 0))],
        out_specs=pl.BlockSpec((rb, S * C), lambda r: (r, 0)),
        out_shape=jax.ShapeDtypeStruct((R, S * C), w.dtype),
        compiler_params=_params(32, 1),
    )(w)


GLU_COLS = 2 * CONV_W


def _fwd_in_proj(x, g1, w_in, rider=None):
    T = x.shape[0]
    tm = 1024

    def body(x_ref, g_ref, w_ref, u_ref, a_ref, qkv_ref):
        u, _ = _rms_fwd(x_ref[...], g_ref[...])
        u = u.astype(BF16)
        u_ref[...] = u
        a_ref[...] = _dot(u, w_ref[:, :GLU_COLS])
        qkv_ref[...] = _dot(u, w_ref[:, GLU_COLS:]).astype(BF16)

    return _call(
        body, rider, name="fwd_in_proj", grid=(T // tm,),
        in_specs=[pl.BlockSpec((tm, D_MODEL), lambda i: (i, 0)),
                  pl.BlockSpec((1, D_MODEL), lambda i: (0, 0)),
                  pl.BlockSpec((D_MODEL, IN_COLS), lambda i: (0, 0))],
        out_specs=[pl.BlockSpec((tm, D_MODEL), lambda i: (i, 0)),
                   pl.BlockSpec((tm, 1024), lambda i: (i, 0)),
                   pl.BlockSpec((tm, 1536), lambda i: (i, 0))],
        out_shape=[jax.ShapeDtypeStruct((T, D_MODEL), BF16),
                   jax.ShapeDtypeStruct((T, 1024), F32),
                   jax.ShapeDtypeStruct((T, 1536), BF16)],
        compiler_params=_params(56, 1),
    )(x, g1, w_in)


def _fill_shifted(ext, shifted, tm):
    n = tm + CONV_HALO - SUBLANES
    for j in range(1, SUBLANES):
        shifted[j - 1] = ext[j:j + n, :]


def _shifted_rows(ext, shifted, start, rows):
    j = start % SUBLANES
    if j == 0:
        return ext[start:start + rows, :]
    return shifted[j - 1, start - j:start - j + rows, :]


def _fwd_conv(a, cw, cb, lg, lb, rider=None):
    T = a.shape[0]
    tm = 512
    rc = 64

    def body(a_ref, w_ref, b_ref, lg_ref, lb_ref, co_ref, hc_ref, hext, hsh):
        i = pl.program_id(0)

        @pl.when(i == 0)
        def _():
            hext[0:CONV_HALO, :] = jnp.zeros((CONV_HALO, CONV_W), F32)

        @pl.when(i > 0)
        def _():
            hext[0:CONV_HALO, :] = hext[tm:tm + CONV_HALO, :]

        hext[CONV_HALO:CONV_HALO + tm, :] = a_ref[:, :CONV_W] * _sigmoid(a_ref[:, CONV_W:])
        _fill_shifted(hext, hsh, tm)
        for c in range(tm // rc):
            acc = jnp.zeros((rc, CONV_W), F32)
            for k in range(CONV_K):
                acc = acc + w_ref[k:k + 1, :] * _shifted_rows(hext, hsh, c * rc + 2 + k, rc)
            hc = acc + b_ref[...]
            hc_ref[c * rc:(c + 1) * rc, :] = hc
            mu = jnp.mean(hc, axis=-1, keepdims=True)
            xc = hc - mu
            var = jnp.mean(xc * xc, axis=-1, keepdims=True)
            z = xc * lax.rsqrt(var + EPS) * lg_ref[...] + lb_ref[...]
            co_ref[c * rc:(c + 1) * rc, :] = (z * _sigmoid(z)).astype(BF16)

    return _call(
        body, rider, name="fwd_conv", grid=(T // tm,),
        in_specs=[pl.BlockSpec((tm, 1024), lambda i: (i, 0)),
                  pl.BlockSpec((CONV_HALO, CONV_W), lambda i: (0, 0)),
                  pl.BlockSpec((1, CONV_W), lambda i: (0, 0)),
                  pl.BlockSpec((1, CONV_W), lambda i: (0, 0)),
                  pl.BlockSpec((1, CONV_W), lambda i: (0, 0))],
        out_specs=[pl.BlockSpec((tm, CONV_W), lambda i: (i, 0)),
                   pl.BlockSpec((tm, CONV_W), lambda i: (i, 0))],
        out_shape=[jax.ShapeDtypeStruct((T, CONV_W), BF16),
                   jax.ShapeDtypeStruct((T, CONV_W), F32)],
        scratch_shapes=[pltpu.VMEM((tm + CONV_HALO, CONV_W), F32),
                        pltpu.VMEM((SUBLANES - 1, tm + CONV_HALO - SUBLANES, CONV_W), F32)],
        compiler_params=_params(40, 1),
    )(a, cw, cb, lg, lb)


def _row_skew(v, sign):
    rows, width = v.shape
    row = lax.broadcasted_iota(jnp.int32, (rows, 1), 0)
    for b in range(int(math.log2(rows))):
        shift = (1 << b) if sign > 0 else width - (1 << b)
        v = jnp.where(((row >> b) & 1) == 1, pltpu.roll(v, shift, 1), v)
    return v


def _att_visible(d):
    rq = lax.broadcasted_iota(jnp.int32, (ATT_BLK, ATT_BLK), 0) // CHUNK
    ck = lax.broadcasted_iota(jnp.int32, (ATT_BLK, ATT_BLK), 1) // CHUNK
    slack = ATT_BLK
    above = jnp.where(d == 0, 0, slack)
    below = jnp.where(d == 2, 0, slack)
    return (ck <= rq + above) & (ck >= rq - below)


def _bias_tiles(rel, rider=None):
    vec = jnp.transpose(rel[:, _rel_index()], (1, 0, 2)).reshape(N_ATT_TILES * N_HEADS, 1, 2 * ATT_BLK)

    def body(v_ref, o_ref):
        visible = _att_visible(pl.program_id(0))
        for h in range(N_HEADS):
            full = _row_skew(jnp.broadcast_to(v_ref[h], (ATT_BLK, 2 * ATT_BLK)), 1)
            o_ref[h] = jnp.where(visible, full[:, :ATT_BLK], NEG_INF)

    return _call(
        body, rider, name="bias_tiles", grid=(N_ATT_TILES,),
        in_specs=[pl.BlockSpec((N_HEADS, 1, 2 * ATT_BLK), lambda d: (d, 0, 0))],
        out_specs=[pl.BlockSpec((N_HEADS, ATT_BLK, ATT_BLK), lambda d: (d, 0, 0))],
        out_shape=[jax.ShapeDtypeStruct((N_ATT_TILES * N_HEADS, ATT_BLK, ATT_BLK), F32)],
        compiler_params=_params(32, 1),
    )(vec)


def _diag_sums(ds, rider=None):
    def body(d_ref, o_ref):
        wide = jnp.concatenate([d_ref[0], jnp.zeros((ATT_BLK, ATT_BLK), F32)], axis=1)
        o_ref[0] = jnp.sum(_row_skew(wide, -1), axis=0, keepdims=True)

    return _call(
        body, rider, name="diag_sums", grid=(N_ATT_TILES * N_HEADS,),
        in_specs=[pl.BlockSpec((1, ATT_BLK, ATT_BLK), lambda n: (n, 0, 0))],
        out_specs=[pl.BlockSpec((1, 1, 2 * ATT_BLK), lambda n: (n, 0, 0))],
        out_shape=[jax.ShapeDtypeStruct((N_ATT_TILES * N_HEADS, 1, 2 * ATT_BLK), F32)],
        compiler_params=_params(16, 1),
    )(ds)


def _head_mask(h):
    lane = lax.broadcasted_iota(jnp.int32, (1, LANES), 1)
    return (lane // HEAD_DIM) == (h % 2)


def _fwd_attn(qkv, bias, rider=None):
    T = qkv.shape[0]
    nb = T // ATT_BLK
    scale = HEAD_DIM ** -0.5

    def body(q_ref, k0_ref, k1_ref, k2_ref, v0_ref, v1_ref, v2_ref, b_ref, o_ref, lse_ref):
        i = pl.program_id(0)

        @pl.when(i >= N_ATT_TILES - 1)
        def _():
            block(i, False, q_ref, k0_ref, k1_ref, k2_ref, v0_ref, v1_ref, v2_ref, b_ref, o_ref, lse_ref)

        @pl.when(i < N_ATT_TILES - 1)
        def _():
            block(i, True, q_ref, k0_ref, k1_ref, k2_ref, v0_ref, v1_ref, v2_ref, b_ref, o_ref, lse_ref)

    def block(i, hide_absent, q_ref, k0_ref, k1_ref, k2_ref, v0_ref, v1_ref, v2_ref, b_ref, o_ref, lse_ref):
        k_refs = (k0_ref, k1_ref, k2_ref)
        v_refs = (v0_ref, v1_ref, v2_ref)
        lane = lax.broadcasted_iota(jnp.int32, (1, LANES), 1)
        lse_tile = jnp.zeros((ATT_BLK, LANES), F32)
        for g in range(N_HEADS // 2):
            cols = slice(g * LANES, (g + 1) * LANES)
            qg = q_ref[:, cols] * scale
            og = jnp.zeros((ATT_BLK, LANES), F32)
            for h in (2 * g, 2 * g + 1):
                hm = _head_mask(h)
                qh = jnp.where(hm, qg, jnp.zeros_like(qg))
                s = []
                for d in range(N_ATT_TILES):
                    sd = _dot_nt(qh, k_refs[d][:, cols]) + b_ref[d * N_HEADS + h]
                    if d > 0 and hide_absent:
                        sd = jnp.where(i >= d, sd, NEG_INF)
                    s.append(sd)
                m = jnp.maximum(jnp.maximum(jnp.max(s[0], axis=-1, keepdims=True),
                                            jnp.max(s[1], axis=-1, keepdims=True)),
                                jnp.max(s[2], axis=-1, keepdims=True))
                p = [jnp.exp(sd - m) for sd in s]
                l = (jnp.sum(p[0], axis=-1, keepdims=True) + jnp.sum(p[1], axis=-1, keepdims=True)
                     + jnp.sum(p[2], axis=-1, keepdims=True))
                oh = jnp.zeros((ATT_BLK, LANES), F32)
                for d in range(N_ATT_TILES):
                    vg = v_refs[d][:, cols]
                    oh = oh + _dot(p[d].astype(BF16), jnp.where(hm, vg, jnp.zeros_like(vg)))
                og = og + oh / l
                lse_tile = jnp.where(lane == h, m + jnp.log(l), lse_tile)
            o_ref[:, cols] = og.astype(BF16)
        lse_ref[...] = lse_tile

    def kv_spec(d, col):
        return pl.BlockSpec((ATT_BLK, ATTN_W), lambda i: (jnp.maximum(i - d, 0), col))

    return _call(
        body, rider, name="fwd_attn", grid=(nb,),
        in_specs=[pl.BlockSpec((ATT_BLK, ATTN_W), lambda i: (i, 0)),
                  kv_spec(0, 1), kv_spec(1, 1), kv_spec(2, 1),
                  kv_spec(0, 2), kv_spec(1, 2), kv_spec(2, 2),
                  pl.BlockSpec((N_ATT_TILES * N_HEADS, ATT_BLK, ATT_BLK), lambda i: (0, 0, 0))],
        out_specs=[pl.BlockSpec((ATT_BLK, ATTN_W), lambda i: (i, 0)),
                   pl.BlockSpec((ATT_BLK, LANES), lambda i: (i, 0))],
        out_shape=[jax.ShapeDtypeStruct((T, ATTN_W), BF16),
                   jax.ShapeDtypeStruct((T, LANES), F32)],
        compiler_params=_params(40, 1),
    )(qkv, qkv, qkv, qkv, qkv, qkv, qkv, bias)


def _fwd_out_proj(co, ao, w_out, x, g2, g3):
    T = x.shape[0]
    tm = 1024

    def body(co_ref, ao_ref, w_ref, x_ref, g2_ref, g3_ref, mixed_ref, h1_ref, u2_ref):
        mixed = _dot(co_ref[...], w_ref[0:CONV_W, :]) + _dot(ao_ref[...], w_ref[CONV_W:, :])
        mixed_ref[...] = mixed.astype(BF16)
        y, _ = _rms_fwd(mixed, g2_ref[...])
        h1 = x_ref[...] + y
        h1_ref[...] = h1
        u2, _ = _rms_fwd(h1, g3_ref[...])
        u2_ref[...] = u2.astype(BF16)

    row = lambda w: pl.BlockSpec((tm, w), lambda i: (i, 0))
    vec = pl.BlockSpec((1, D_MODEL), lambda i: (0, 0))
    return _call(
        body, None, name="fwd_out_proj", grid=(T // tm,),
        in_specs=[row(CONV_W), row(ATTN_W), pl.BlockSpec((D_MODEL, D_MODEL), lambda i: (0, 0)),
                  row(D_MODEL), vec, vec],
        out_specs=[row(D_MODEL), row(D_MODEL), row(D_MODEL)],
        out_shape=[jax.ShapeDtypeStruct((T, D_MODEL), BF16),
                   jax.ShapeDtypeStruct((T, D_MODEL), F32),
                   jax.ShapeDtypeStruct((T, D_MODEL), BF16)],
        compiler_params=_params(56, 1),
    )(co, ao, w_out, x, g2, g3)[0]


GELU_C = math.sqrt(2.0 / math.pi)
GELU_A = 0.044715


def _gelu_and_grad(v):
    sq = v * v
    th = jnp.tanh(v * (GELU_C + (GELU_C * GELU_A) * sq))
    half = 0.5 + 0.5 * th
    gl = v * half
    dgl = half + (v * (half * (1.0 - th))) * (GELU_C + (3.0 * GELU_C * GELU_A) * sq)
    return gl, dgl


FF_TM = 256
FF_HALO = 16
FF_CHUNKS = [(lo, min(lo + 256, FF_SHARD)) for lo in range(0, FF_SHARD, 256)]


def _rows_before(prev, cur):
    ext = jnp.concatenate([prev, cur], axis=0)
    return pltpu.roll(ext, 1, 0)[SUBLANES:], pltpu.roll(ext, 2, 0)[SUBLANES:]


def _rows_after(cur, nxt):
    ext = jnp.concatenate([cur, nxt], axis=0)
    n = ext.shape[0]
    return pltpu.roll(ext, n - 1, 0)[:cur.shape[0]], pltpu.roll(ext, n - 2, 0)[:cur.shape[0]]


def _pair_up_weights(w_up):
    rb = 256

    def body(g_ref, v_ref, o_ref):
        for lo, hi in FF_CHUNKS:
            o_ref[0, :, 2 * lo:lo + hi] = g_ref[0, :, lo:hi]
            o_ref[0, :, lo + hi:2 * hi] = v_ref[0, :, lo:hi]

    return pl.pallas_call(
        body, name="pair_up_weights", grid=(2, D_MODEL // rb),
        in_specs=[pl.BlockSpec((1, rb, FF_SHARD), lambda s, r: (s, r, 0)),
                  pl.BlockSpec((1, rb, FF_SHARD), lambda s, r: (s + 2, r, 0))],
        out_specs=pl.BlockSpec((1, rb, 2 * FF_SHARD), lambda s, r: (s, r, 0)),
        out_shape=jax.ShapeDtypeStruct((2, D_MODEL, 2 * FF_SHARD), w_up.dtype),
        compiler_params=_params(32, 2),
    )(w_up, w_up)


def _fwd_ffn_loss(u2, w_cat, fw, fb, w_down, h1, tgt, g4):
    T = u2.shape[0]
    tm = FF_TM

    def body(u_ref, w_ref, fw_ref, fb_ref, wd_ref, h1_ref, t_ref, g_ref,
             hf_ref, pre_ref, act_ref, loss_ref, dy_ref, df_ref, dg_ref, carg, carv):
        i = pl.program_id(0)

        @pl.when(i == 0)
        def _():
            carg[...] = jnp.zeros(carg.shape, F32)
            carv[...] = jnp.zeros(carv.shape, F32)

        u = u_ref[...]
        f = None
        chunks = [(s, lo, hi) for s in range(2) for lo, hi in FF_CHUNKS]
        up = lambda s, lo, hi: _dot(u, w_ref[s, :, 2 * lo:2 * hi])
        ahead = up(*chunks[0])
        for c, (s, lo, hi) in enumerate(chunks):
            conv = []
            hs = (ahead[:, :hi - lo], ahead[:, hi - lo:])
            if c + 1 < len(chunks):
                ahead = up(*chunks[c + 1])
            at = slice(s * FF_SHARD + lo, s * FF_SHARD + hi)
            for n, car in enumerate((carg, carv)):
                h0 = hs[n]
                hf_ref[n, :, at] = h0.astype(BF16)
                h1v, h2v = _rows_before(car[:, at], h0)
                car[:, at] = h0[tm - SUBLANES:, :]
                conv.append(fw_ref[2 * n + s, 0:1, lo:hi] * h2v + fw_ref[2 * n + s, 1:2, lo:hi] * h1v
                            + fw_ref[2 * n + s, 2:3, lo:hi] * h0 + fb_ref[2 * n + s, :, lo:hi])
            pre_ref[0, :, at] = conv[0].astype(BF16)
            pre_ref[1, :, at] = conv[1].astype(BF16)
            gl, _ = _gelu_and_grad(conv[0])
            act = (gl * conv[1]).astype(BF16)
            act_ref[:, at] = act
            term = _dot(act, wd_ref[at, :])
            f = term if f is None else f + term

        r, _ = _rms_fwd(f, g_ref[...])
        e = (h1_ref[...] + r) - t_ref[...]
        dy = e * (1.0 / D_MODEL)
        dy_ref[...] = dy
        df, dg_rows = _rms_bwd(dy, f, g_ref[...])
        df_ref[...] = df.astype(BF16)
        part = 0.5 * jnp.sum(jnp.mean(e * e, axis=-1, keepdims=True), axis=0, keepdims=True)
        dg = jnp.sum(dg_rows, axis=0, keepdims=True)

        @pl.when(i == 0)
        def _():
            loss_ref[...] = part
            dg_ref[...] = dg

        @pl.when(i > 0)
        def _():
            loss_ref[...] += part
            dg_ref[...] += dg

    row = lambda w: pl.BlockSpec((tm, w), lambda i: (i, 0))
    vec = pl.BlockSpec((1, D_MODEL), lambda i: (0, 0))
    once = pl.Buffered(1)
    return _call(
        body, None, name="fwd_ffn_loss", grid=(T // tm,),
        in_specs=[row(D_MODEL),
                  pl.BlockSpec((2, D_MODEL, 2 * FF_SHARD), lambda i: (0, 0, 0), pipeline_mode=once),
                  pl.BlockSpec((N_CHIPS, FF_HALO, FF_SHARD), lambda i: (0, 0, 0)),
                  pl.BlockSpec((N_CHIPS, 1, FF_SHARD), lambda i: (0, 0, 0)),
                  pl.BlockSpec((D_FF, D_MODEL), lambda i: (0, 0), pipeline_mode=once),
                  row(D_MODEL), row(D_MODEL), vec],
        out_specs=[pl.BlockSpec((2, tm, D_FF), lambda i: (0, i, 0)),
                   pl.BlockSpec((2, tm, D_FF), lambda i: (0, i, 0)),
                   row(D_FF), pl.BlockSpec((1, 1), lambda i: (0, 0)), row(D_MODEL), row(D_MODEL), vec],
        out_shape=[jax.ShapeDtypeStruct((2, T, D_FF), BF16),
                   jax.ShapeDtypeStruct((2, T, D_FF), BF16),
                   jax.ShapeDtypeStruct((T, D_FF), BF16),
                   jax.ShapeDtypeStruct((1, 1), F32),
                   jax.ShapeDtypeStruct((T, D_MODEL), F32),
                   jax.ShapeDtypeStruct((T, D_MODEL), BF16),
                   jax.ShapeDtypeStruct((1, D_MODEL), F32)],
        scratch_shapes=[pltpu.VMEM((SUBLANES, D_FF), F32), pltpu.VMEM((SUBLANES, D_FF), F32)],
        compiler_params=_params(60, 1),
    )(u2, w_cat, fw, fb, w_down, h1, tgt, g4)[0]


def _bwd_ffn(df, hf, pre, w_cat, fw, w_down):
    T = df.shape[0]
    tm = FF_TM
    ni = T // tm

    def body(df_ref, hf_ref, pre_ref, wd_ref, w_ref, fw_ref,
             du_ref, dhf_ref, dwg_ref, dwv_ref, carg, carv):
        i = pl.program_id(0)

        @pl.when(i == 0)
        def _():
            dwg_ref[...] = jnp.zeros(dwg_ref.shape, F32)
            dwv_ref[...] = jnp.zeros(dwv_ref.shape, F32)
            carg[...] = jnp.zeros(carg.shape, F32)
            carv[...] = jnp.zeros(carv.shape, F32)

        df = df_ref[...]
        du = None
        chunks = [(s, lo, hi) for s in range(2) for lo, hi in FF_CHUNKS]
        cols = lambda s, lo, hi: slice(s * FF_SHARD + lo, s * FF_SHARD + hi)
        down = lambda s, lo, hi: _dot_nt(df, wd_ref[cols(s, lo, hi), :])
        ahead = down(*chunks[0])
        for c, (s, lo, hi) in enumerate(chunks):
            dact = ahead
            if c + 1 < len(chunks):
                ahead = down(*chunks[c + 1])
            at = cols(s, lo, hi)
            pre_g = pre_ref[0, :, at].astype(F32)
            pre_v = pre_ref[1, :, at].astype(F32)
            gl, dgl = _gelu_and_grad(pre_g)
            dpre = (dact * pre_v * dgl, dact * gl)
            dhs = []
            for n, (car, dw_ref) in enumerate(((carg, dwg_ref), (carv, dwv_ref))):
                dp = dpre[n]
                h0 = hf_ref[n, :, at].astype(F32)
                up1, up2 = _rows_after(dp, car[:, at])
                car[:, at] = dp[0:SUBLANES, :]
                for k, shifted in enumerate((up2, up1, dp)):
                    dw_ref[s, k:k + 1, lo:hi] += jnp.sum(shifted * h0, axis=0, keepdims=True)
                dw_ref[s, 3:4, lo:hi] += jnp.sum(dp, axis=0, keepdims=True)
                taps = 2 * n + s
                dh = (fw_ref[taps, 2:3, lo:hi] * dp + fw_ref[taps, 1:2, lo:hi] * up1
                      + fw_ref[taps, 0:1, lo:hi] * up2).astype(BF16)
                dhf_ref[n, :, at] = dh
                dhs.append(dh)
            term = _dot_nt(jnp.concatenate(dhs, axis=1), w_ref[s, :, 2 * lo:2 * hi])
            du = term if du is None else du + term
        du_ref[...] = du.astype(BF16)

    rev = lambda i: ni - 1 - i
    once = pl.Buffered(1)
    dwspec = pl.BlockSpec((2, FF_HALO, FF_SHARD), lambda i: (0, 0, 0))
    return _call(
        body, None, name="bwd_ffn", grid=(ni,),
        in_specs=[pl.BlockSpec((tm, D_MODEL), lambda i: (rev(i), 0)),
                  pl.BlockSpec((2, tm, D_FF), lambda i: (0, rev(i), 0)),
                  pl.BlockSpec((2, tm, D_FF), lambda i: (0, rev(i), 0)),
                  pl.BlockSpec((D_FF, D_MODEL), lambda i: (0, 0), pipeline_mode=once),
                  pl.BlockSpec((2, D_MODEL, 2 * FF_SHARD), lambda i: (0, 0, 0), pipeline_mode=once),
                  pl.BlockSpec((N_CHIPS, FF_HALO, FF_SHARD), lambda i: (0, 0, 0))],
        out_specs=[pl.BlockSpec((tm, D_MODEL), lambda i: (rev(i), 0)),
                   pl.BlockSpec((2, tm, D_FF), lambda i: (0, rev(i), 0)),
                   dwspec, dwspec],
        out_shape=[jax.ShapeDtypeStruct((T, D_MODEL), BF16),
                   jax.ShapeDtypeStruct((2, T, D_FF), BF16),
                   jax.ShapeDtypeStruct((2, FF_HALO, FF_SHARD), F32),
                   jax.ShapeDtypeStruct((2, FF_HALO, FF_SHARD), F32)],
        scratch_shapes=[pltpu.VMEM((SUBLANES, D_FF), F32), pltpu.VMEM((SUBLANES, D_FF), F32)],
        compiler_params=_params(60, 1),
    )(df, hf, pre, w_down, w_cat, fw)[0]


def _bwd_mid(du2p, dy, h1, mixed, g3, g2, w_out, rider=None):
    T = dy.shape[0]
    tm = 512

    def body(du_ref, dy_ref, h1_ref, mx_ref, g3_ref, g2_ref, w_ref,
             dh1_ref, dmx_ref, dco_ref, dao_ref, dg3_ref, dg2_ref):
        i = pl.program_id(0)
        dres, dg3_rows = _rms_bwd(du_ref[...].astype(F32), h1_ref[...], g3_ref[...])
        dh1 = dy_ref[...] + dres
        dh1_ref[...] = dh1
        dmx, dg2_rows = _rms_bwd(dh1, mx_ref[...].astype(F32), g2_ref[...])
        dmx = dmx.astype(BF16)
        dmx_ref[...] = dmx
        dcat = _dot_nt(dmx, w_ref[...])
        dco_ref[...] = dcat[:, :CONV_W]
        dao_ref[...] = dcat[:, CONV_W:].astype(BF16)
        dg3 = jnp.sum(dg3_rows, axis=0, keepdims=True)
        dg2 = jnp.sum(dg2_rows, axis=0, keepdims=True)

        @pl.when(i == 0)
        def _():
            dg3_ref[...] = dg3
            dg2_ref[...] = dg2

        @pl.when(i > 0)
        def _():
            dg3_ref[...] += dg3
            dg2_ref[...] += dg2

    row = lambda w: pl.BlockSpec((tm, w), lambda i: (i, 0))
    vec = pl.BlockSpec((1, D_MODEL), lambda i: (0, 0))
    return _call(
        body, rider, name="bwd_mid", grid=(T // tm,),
        in_specs=[row(D_MODEL), row(D_MODEL), row(D_MODEL),
                  row(D_MODEL), vec, vec, pl.BlockSpec((D_MODEL, D_MODEL), lambda i: (0, 0))],
        out_specs=[row(D_MODEL), row(D_MODEL), row(CONV_W), row(ATTN_W), vec, vec],
        out_shape=[jax.ShapeDtypeStruct((T, D_MODEL), F32),
                   jax.ShapeDtypeStruct((T, D_MODEL), BF16),
                   jax.ShapeDtypeStruct((T, CONV_W), F32),
                   jax.ShapeDtypeStruct((T, ATTN_W), BF16),
                   jax.ShapeDtypeStruct((1, D_MODEL), F32),
                   jax.ShapeDtypeStruct((1, D_MODEL), F32)],
        compiler_params=_params(48, 1),
    )(du2p, dy, h1, mixed, g3, g2, w_out)


def _bwd_attn(qkv, ao, dao, lse, bias, rider=None):
    T = qkv.shape[0]
    nb = T // ATT_BLK
    scale = HEAD_DIM ** -0.5

    def body(k_ref, v_ref, q0, q1, q2, do0, do1, do2, o0, o1, o2, l0, l1, l2, b_ref,
             dp_ref, ds_ref, acc1, acc2):
        j = pl.program_id(0)
        q_refs, do_refs, o_refs, l_refs = (q0, q1, q2), (do0, do1, do2), (o0, o1, o2), (l0, l1, l2)

        @pl.when(j == 0)
        def _():
            ds_ref[...] = jnp.zeros(ds_ref.shape, F32)
            acc1[...] = jnp.zeros(acc1.shape, F32)
            acc2[...] = jnp.zeros(acc2.shape, F32)

        dq_new = [[], [], []]
        dk_cols, dv_cols = [], []
        for g in range(N_HEADS // 2):
            cols = slice(g * LANES, (g + 1) * LANES)
            kg = k_ref[:, cols]
            vg = v_ref[:, cols]
            dkt = jnp.zeros((LANES, ATT_BLK), F32)
            dvt = jnp.zeros((LANES, ATT_BLK), F32)
            dqg = [jnp.zeros((ATT_BLK, LANES), F32) for _ in range(N_ATT_TILES)]
            row_head = lax.broadcasted_iota(jnp.int32, (LANES, 1), 0) // HEAD_DIM
            for d in range(N_ATT_TILES):
                qg = q_refs[d][:, cols] * scale
                dog = do_refs[d][:, cols]
                if d > 0:
                    dog = jnp.where(j + d < nb, dog, jnp.zeros_like(dog))
                prod = dog.astype(F32) * o_refs[d][:, cols].astype(F32)
                qgt = qg.astype(F32).T
                dogt = dog.astype(F32).T
                for h in (2 * g, 2 * g + 1):
                    hm = _head_mask(h)
                    qh = jnp.where(hm, qg, jnp.zeros_like(qg))
                    doh = jnp.where(hm, dog, jnp.zeros_like(dog))
                    kh = jnp.where(hm, kg, jnp.zeros_like(kg))
                    mine = row_head == (h % 2)
                    qht = jnp.where(mine, qgt, 0.0).astype(BF16)
                    doht = jnp.where(mine, dogt, 0.0).astype(BF16)
                    delta = jnp.sum(jnp.where(hm, prod, 0.0), axis=-1, keepdims=True)
                    s = _dot_nt(qh, kg) + b_ref[d * N_HEADS + h]
                    p = jnp.exp(s - l_refs[d][:, h:h + 1])
                    dvt = dvt + _dot(doht, p.astype(BF16))
                    dpm = _dot_nt(doh, vg)
                    dsc = p * (dpm - delta)
                    ds_ref[d * N_HEADS + h] += dsc
                    dsb = dsc.astype(BF16)
                    dqg[d] = dqg[d] + _dot(dsb, kh)
                    dkt = dkt + _dot(qht, dsb)
            for d in range(N_ATT_TILES):
                dq_new[d].append(dqg[d])
            dk_cols.append(dkt.T)
            dv_cols.append(dvt.T)
        x0, x1, x2 = (jnp.concatenate(c, axis=1) * scale for c in dq_new)
        dp_ref[:, 0:1024] = jnp.zeros((ATT_BLK, 1024), BF16)
        dp_ref[:, 1024:1536] = (acc1[...] + x0).astype(BF16)
        dp_ref[:, 1536:2048] = jnp.concatenate(dk_cols, axis=1).astype(BF16)
        dp_ref[:, 2048:2560] = jnp.concatenate(dv_cols, axis=1).astype(BF16)
        acc1[...] = acc2[...] + x1
        acc2[...] = x2

    def fwd_spec(d, width, col):
        return pl.BlockSpec((ATT_BLK, width), lambda j: (jnp.minimum(j + d, nb - 1), col))

    return _call(
        body, rider, name="bwd_attn", grid=(nb,),
        in_specs=[pl.BlockSpec((ATT_BLK, ATTN_W), lambda j: (j, 1)),
                  pl.BlockSpec((ATT_BLK, ATTN_W), lambda j: (j, 2)),
                  fwd_spec(0, ATTN_W, 0), fwd_spec(1, ATTN_W, 0), fwd_spec(2, ATTN_W, 0),
                  fwd_spec(0, ATTN_W, 0), fwd_spec(1, ATTN_W, 0), fwd_spec(2, ATTN_W, 0),
                  fwd_spec(0, ATTN_W, 0), fwd_spec(1, ATTN_W, 0), fwd_spec(2, ATTN_W, 0),
                  fwd_spec(0, LANES, 0), fwd_spec(1, LANES, 0), fwd_spec(2, LANES, 0),
                  pl.BlockSpec((N_ATT_TILES * N_HEADS, ATT_BLK, ATT_BLK), lambda j: (0, 0, 0))],
        out_specs=[pl.BlockSpec((ATT_BLK, IN_COLS), lambda j: (j, 0)),
                   pl.BlockSpec((N_ATT_TILES * N_HEADS, ATT_BLK, ATT_BLK), lambda j: (0, 0, 0))],
        out_shape=[jax.ShapeDtypeStruct((T, IN_COLS), BF16),
                   jax.ShapeDtypeStruct((N_ATT_TILES * N_HEADS, ATT_BLK, ATT_BLK), F32)],
        scratch_shapes=[pltpu.VMEM((ATT_BLK, ATTN_W), F32), pltpu.VMEM((ATT_BLK, ATTN_W), F32)],
        compiler_params=_params(56, 1),
    )(qkv, qkv, qkv, qkv, qkv, dao, dao, dao, ao, ao, ao, lse, lse, lse, bias)


def _bwd_conv(dproj, a, dco, hc, cw, lg, lb, rider=None):
    T = a.shape[0]
    tm = 512
    rc = 32
    ni = T // tm
    hb = tm // CONV_HALO

    def body(dp_in, a_ref, ap_ref, dco_ref, dcon_ref, hc_ref, hcn_ref, w_ref, lg_ref, lb_ref,
             dp_ref, dw_ref, db_ref, dlg_ref, dlb_ref, hext, dext, hsh, dsh, dwacc):
        del dp_in
        i = pl.program_id(0)

        def ln_bwd(dco_v, hc_v):
            mu = jnp.mean(hc_v, axis=-1, keepdims=True)
            xc = hc_v - mu
            rstd = lax.rsqrt(jnp.mean(xc * xc, axis=-1, keepdims=True) + EPS)
            xh = xc * rstd
            z = xh * lg_ref[...] + lb_ref[...]
            sg = _sigmoid(z)
            dz = dco_v * (sg * (1.0 + z * (1.0 - sg)))
            dxh = dz * lg_ref[...]
            dhc = rstd * (dxh - jnp.mean(dxh, axis=-1, keepdims=True)
                          - xh * jnp.mean(dxh * xh, axis=-1, keepdims=True))
            return dhc, dz * xh, dz

        hext[0:CONV_HALO, :] = jnp.where(i > 0, ap_ref[:, :CONV_W] * _sigmoid(ap_ref[:, CONV_W:]), 0.0)
        hext[CONV_HALO:CONV_HALO + tm, :] = a_ref[:, :CONV_W] * _sigmoid(a_ref[:, CONV_W:])
        dhc, dlg_rows, dlb_rows = ln_bwd(dco_ref[...], hc_ref[...])
        dext[0:tm, :] = dhc
        dhc_next, _, _ = ln_bwd(dcon_ref[...], hcn_ref[...])
        dext[tm:tm + CONV_HALO, :] = jnp.where(i < ni - 1, dhc_next, 0.0)

        @pl.when(i == 0)
        def _():
            dw_ref[...] = jnp.zeros(dw_ref.shape, F32)
            db_ref[...] = jnp.zeros(db_ref.shape, F32)
            dlg_ref[...] = jnp.zeros(dlg_ref.shape, F32)
            dlb_ref[...] = jnp.zeros(dlb_ref.shape, F32)

            dwacc[...] = jnp.zeros(dwacc.shape, F32)

        db_ref[...] += jnp.sum(dhc, axis=0, keepdims=True)
        dlg_ref[...] += jnp.sum(dlg_rows, axis=0, keepdims=True)
        dlb_ref[...] += jnp.sum(dlb_rows, axis=0, keepdims=True)
        _fill_shifted(hext, hsh, tm)
        _fill_shifted(dext, dsh, tm)
        for c in range(tm // rc):
            r0 = c * rc
            dh = jnp.zeros((rc, CONV_W), F32)
            dhc_c = dext[r0:r0 + rc, :]
            for k in range(CONV_K):
                dh = dh + w_ref[k:k + 1, :] * _shifted_rows(dext, dsh, r0 + 30 - k, rc)
                prod = dhc_c * _shifted_rows(hext, hsh, r0 + 2 + k, rc)
                dwacc[k] += jnp.sum(prod.reshape(rc // SUBLANES, SUBLANES, CONV_W), axis=0)
            av = a_ref[r0:r0 + rc, :CONV_W]
            sg = _sigmoid(a_ref[r0:r0 + rc, CONV_W:])
            dp_ref[r0:r0 + rc, 0:CONV_W] = (dh * sg).astype(BF16)
            dp_ref[r0:r0 + rc, CONV_W:] = (dh * av * sg * (1.0 - sg)).astype(BF16)

        @pl.when(i == ni - 1)
        def _():
            dw_ref[...] = jnp.sum(dwacc[...], axis=1)

    row = lambda w: pl.BlockSpec((tm, w), lambda i: (i, 0))
    prev = lambda w: pl.BlockSpec((CONV_HALO, w), lambda i: (jnp.maximum(i * hb - 1, 0), 0))
    nxt = lambda w: pl.BlockSpec((CONV_HALO, w), lambda i: (jnp.minimum((i + 1) * hb, ni * hb - 1), 0))
    vec = pl.BlockSpec((1, CONV_W), lambda i: (0, 0))
    return _call(
        body, rider, name="bwd_conv", grid=(ni,),
        in_specs=[ANY, row(1024), prev(1024), row(CONV_W), nxt(CONV_W), row(CONV_W), nxt(CONV_W),
                  pl.BlockSpec((CONV_HALO, CONV_W), lambda i: (0, 0)), vec, vec],
        out_specs=[pl.BlockSpec((tm, 1024), lambda i: (i, 0)),
                   pl.BlockSpec((CONV_HALO, CONV_W), lambda i: (0, 0)), vec, vec, vec],
        out_shape=[jax.ShapeDtypeStruct((T, IN_COLS), BF16),
                   jax.ShapeDtypeStruct((CONV_HALO, CONV_W), F32),
                   jax.ShapeDtypeStruct((1, CONV_W), F32),
                   jax.ShapeDtypeStruct((1, CONV_W), F32),
                   jax.ShapeDtypeStruct((1, CONV_W), F32)],
        scratch_shapes=[pltpu.VMEM((tm + CONV_HALO, CONV_W), F32), pltpu.VMEM((tm + CONV_HALO, CONV_W), F32),
                        pltpu.VMEM((SUBLANES - 1, tm + CONV_HALO - SUBLANES, CONV_W), F32),
                        pltpu.VMEM((SUBLANES - 1, tm + CONV_HALO - SUBLANES, CONV_W), F32),
                        pltpu.VMEM((CONV_HALO, SUBLANES, CONV_W), F32)],
        input_output_aliases={0: 0},
        compiler_params=_params(56, 1),
    )(dproj, a, a, dco, dco, hc, hc, cw, lg, lb)


def _bwd_in_proj(dproj, w_in, x, dh1, g1, rider=None):
    T = x.shape[0]
    tm = 1024

    def body(dp_ref, w_ref, x_ref, dh_ref, g_ref, gx_ref, dg_ref):
        i = pl.program_id(0)
        du = _dot_nt(dp_ref[...], w_ref[...])
        dx, dg_rows = _rms_bwd(du, x_ref[...], g_ref[...])
        gx_ref[...] = dh_ref[...] + dx
        dg = jnp.sum(dg_rows, axis=0, keepdims=True)

        @pl.when(i == 0)
        def _():
            dg_ref[...] = dg

        @pl.when(i > 0)
        def _():
            dg_ref[...] += dg

    row = lambda w: pl.BlockSpec((tm, w), lambda i: (i, 0))
    vec = pl.BlockSpec((1, D_MODEL), lambda i: (0, 0))
    return _call(
        body, rider, name="bwd_in_proj", grid=(T // tm,),
        in_specs=[row(IN_COLS), pl.BlockSpec((D_MODEL, IN_COLS), lambda i: (0, 0)),
                  row(D_MODEL), row(D_MODEL), vec],
        out_specs=[row(D_MODEL), vec],
        out_shape=[jax.ShapeDtypeStruct((T, D_MODEL), F32), jax.ShapeDtypeStruct((1, D_MODEL), F32)],
        compiler_params=_params(56, 1),
    )(dproj, w_in, x, dh1, g1)


def _wgrad(name, a_list, a_spec, b, b_spec, out_spec, out_shape, n_outer, T, tk, select=None, rider=None):
    def body(*refs):
        a_refs, b_ref, o_ref = refs[:len(a_list)], refs[len(a_list)], refs[len(a_list) + 1]
        kt = pl.program_id(1)

        @pl.when(kt == 0)
        def _():
            o_ref[...] = jnp.zeros(o_ref.shape, F32)

        bv = b_ref[...].reshape(b_ref.shape[-2:])
        if select is None:
            o_ref[...] += _dot_tn(a_refs[0][...].reshape(a_refs[0].shape[-2:]), bv).reshape(o_ref.shape)
        else:
            for n, a_ref in enumerate(a_refs):
                @pl.when(select(pl.program_id(0)) == n)
                def _():
                    o_ref[...] += _dot_tn(a_ref[...], bv).reshape(o_ref.shape)

    (res,), got = _call(
        body, rider, name=name, grid=(n_outer, T // tk),
        in_specs=[a_spec] * len(a_list) + [b_spec],
        out_specs=[out_spec], out_shape=[out_shape],
        compiler_params=_params(56, 2),
    )(*a_list, b)
    return (res, got) if rider is not None else res


def _mesh_pos():
    return lax.axis_index("x"), lax.axis_index("y"), lax.axis_index("c")


def _other_chips(x, y):
    return [((1 - x, y), 2 * (1 - x) + y), ((x, 1 - y), 2 * x + (1 - y)), ((1 - x, 1 - y), 2 * (1 - x) + (1 - y))]


def _exchange_rider(operands, out_shape, aliases, sem_shape, pairs):
    def start(ins, outs, sems):
        for send, _ in pairs(ins, outs, *sems):
            send.start()

    def finish(ins, outs, sems):
        for send, recv in pairs(ins, outs, *sems):
            send.wait_send()
            recv.wait_recv()

    sems = [pltpu.SemaphoreType.DMA(sem_shape), pltpu.SemaphoreType.DMA(sem_shape)]
    return _Rider(list(operands), list(out_shape), aliases, sems, start, finish)


def _remote(src, dst, send_sem, recv_sem, device):
    return pltpu.make_async_remote_copy(src_ref=src, dst_ref=dst, send_sem=send_sem, recv_sem=recv_sem,
                                        device_id=device, device_id_type=MESH)


def _fetch_rider(bufs):
    def pairs(ins, outs, send_sems, recv_sems):
        x, y, c = _mesh_pos()
        res = []
        for t, buf in enumerate(bufs):
            rows = pl.ds(c * (buf.shape[1] // 2), buf.shape[1] // 2)
            mine = outs[t].at[2 * x + y, rows]
            for k, (chip, s) in enumerate(_other_chips(x, y)):
                landed = outs[t].at[s, rows]
                res.append((_remote(mine, mine, send_sems.at[t, k], recv_sems.at[t, k], (*chip, c)),
                            _remote(landed, landed, send_sems.at[t, k], recv_sems.at[t, k], (*chip, c))))
        return res

    shapes = [jax.ShapeDtypeStruct(b.shape, b.dtype) for b in bufs]
    return _exchange_rider(bufs, shapes, {t: t for t in range(len(bufs))}, (len(bufs), 3), pairs)


def _forward_rider(bufs):
    def pairs(ins, outs, send_sems, recv_sems):
        x, y, c = _mesh_pos()
        res = []
        for t, buf in enumerate(bufs):
            half = buf.shape[1] // 2
            for k, (_, s) in enumerate(_other_chips(x, y)):
                landed = outs[t].at[s, pl.ds(c * half, half)]
                theirs = outs[t].at[s, pl.ds((1 - c) * half, half)]
                res.append((_remote(landed, landed, send_sems.at[t, k], recv_sems.at[t, k], (x, y, 1 - c)),
                            _remote(theirs, theirs, send_sems.at[t, k], recv_sems.at[t, k], (x, y, 1 - c))))
        return res

    shapes = [jax.ShapeDtypeStruct(b.shape, b.dtype) for b in bufs]
    return _exchange_rider(bufs, shapes, {t: t for t in range(len(bufs))}, (len(bufs), 3), pairs)


def _pair_exchange_rider(grads):
    def pairs(ins, outs, send_sems, recv_sems):
        x, y, c = _mesh_pos()
        res = []
        for t, g in enumerate(grads):
            half = g.shape[1] // 2
            cp = _remote(ins[t].at[:, pl.ds((1 - c) * half, half), :], outs[t], send_sems.at[t], recv_sems.at[t],
                         (x, y, 1 - c))
            res.append((cp, cp))
        return res

    shapes = [jax.ShapeDtypeStruct((N_CHIPS, g.shape[1] // 2, g.shape[2]), F32) for g in grads]
    return _exchange_rider(grads, shapes, {}, (len(grads),), pairs)


def _chip_exchange_rider(sums):
    def pairs(ins, outs, send_sems, recv_sems):
        x, y, c = _mesh_pos()
        res = []
        for t in range(len(sums)):
            for k, (chip, s) in enumerate(_other_chips(x, y)):
                cp = _remote(ins[t].at[s], outs[t].at[k], send_sems.at[t, k], recv_sems.at[t, k], (*chip, c))
                res.append((cp, cp))
        return res

    shapes = [jax.ShapeDtypeStruct((3,) + p.shape[1:], p.dtype) for p in sums]
    return _exchange_rider(sums, shapes, {}, (len(sums), 3), pairs)


def _pair_gather_rider(fulls):
    def pairs(ins, outs, send_sems, recv_sems):
        x, y, c = _mesh_pos()
        res = []
        for t, f in enumerate(fulls):
            half = f.shape[0] // 2
            mine = outs[t].at[pl.ds(c * half, half)]
            theirs = outs[t].at[pl.ds((1 - c) * half, half)]
            res.append((_remote(mine, mine, send_sems.at[t], recv_sems.at[t], (x, y, 1 - c)),
                        _remote(theirs, theirs, send_sems.at[t], recv_sems.at[t], (x, y, 1 - c))))
        return res

    shapes = [jax.ShapeDtypeStruct(f.shape, F32) for f in fulls]
    return _exchange_rider(fulls, shapes, {t: t for t in range(len(fulls))}, (len(fulls),), pairs)


def _alone(name, rider):
    return _call(lambda: None, rider, name=name)()[1]


def _all_reduce_small(pack, rider=None):
    rows = pack.shape[0]

    def body(p_ref, o_ref, buf, send_sems, recv_sems):
        x, y, c = _mesh_pos()
        me = 4 * x + 2 * y + c
        buf[0] = p_ref[...]
        copies = []
        for k in range(1, 8):
            peer = (x ^ (k >> 2), y ^ ((k >> 1) & 1), c ^ (k & 1))
            cp = pltpu.make_async_remote_copy(
                src_ref=p_ref, dst_ref=buf.at[k], send_sem=send_sems.at[k - 1], recv_sem=recv_sems.at[k - 1],
                device_id=peer, device_id_type=MESH)
            cp.start()
            copies.append(cp)
        for cp in copies:
            cp.wait()
        total = buf[me]
        for dev in range(1, 8):
            total = total + buf[me ^ dev]
        o_ref[...] = total

    return _call(
        body, rider, name="all_reduce_small",
        in_specs=[VMEM_FULL], out_specs=[VMEM_FULL],
        out_shape=[jax.ShapeDtypeStruct(pack.shape, F32)],
        scratch_shapes=[pltpu.VMEM((8, rows, LANES), F32),
                        pltpu.SemaphoreType.DMA((7,)), pltpu.SemaphoreType.DMA((7,))],
    )(pack)


def _row_block(rows):
    if rows <= 512:
        return rows
    for rb in (256, 352):
        if rows % rb == 0:
            return rb
    raise ValueError(f"no row block for {rows} rows")


def _place(name, w, pos, dtype):
    R, C = w.shape
    rb = _row_block(R)

    def body(pos_ref, w_ref, o_ref):
        del pos_ref
        o_ref[0] = w_ref[...].astype(dtype)

    return pl.pallas_call(
        body, name=name,
        grid_spec=pltpu.PrefetchScalarGridSpec(
            num_scalar_prefetch=1, grid=(R // rb,),
            in_specs=[pl.BlockSpec((rb, C), lambda r, p: (r, 0))],
            out_specs=pl.BlockSpec((1, rb, C), lambda r, p: (p[0], r, 0))),
        out_shape=(pltpu.HBM if N_CHIPS * R * C * jnp.dtype(dtype).itemsize >= PIN_BYTES
                   else jax.ShapeDtypeStruct)((N_CHIPS, R, C), dtype),
        compiler_params=_params(32, 1),
    )(pos, w)


def _pair_sum(name, g, got, pos):
    S, R, C = g.shape
    half = R // 2
    rb = _row_block(half)
    nh = half // rb

    def body(pos_ref, a_ref, b_ref, o_ref):
        del pos_ref
        o_ref[...] = (a_ref[...] + b_ref[...]).astype(BF16)

    spec = pl.BlockSpec((1, rb, C), lambda s, r, p: (s, r, 0))
    return pl.pallas_call(
        body, name=name,
        grid_spec=pltpu.PrefetchScalarGridSpec(
            num_scalar_prefetch=1, grid=(S, nh),
            in_specs=[pl.BlockSpec((1, rb, C), lambda s, r, p: (s, p[1] * nh + r, 0)), spec],
            out_specs=spec),
        out_shape=jax.ShapeDtypeStruct((S, half, C), BF16), compiler_params=_params(32, 2),
    )(pos, g, got)


def _chip_sum(name, pairs, got, pos):
    _, half, C = pairs.shape
    rb = _row_block(half)
    nh = half // rb

    def body(pos_ref, a_ref, g_ref, o_ref):
        del pos_ref
        o_ref[...] = ((a_ref[0].astype(F32) + g_ref[0].astype(F32)) + g_ref[1].astype(F32)) + g_ref[2].astype(F32)

    return pl.pallas_call(
        body, name=name,
        grid_spec=pltpu.PrefetchScalarGridSpec(
            num_scalar_prefetch=1, grid=(nh,),
            in_specs=[pl.BlockSpec((1, rb, C), lambda r, p: (p[0], r, 0)),
                      pl.BlockSpec((3, rb, C), lambda r, p: (0, r, 0))],
            out_specs=pl.BlockSpec((rb, C), lambda r, p: (p[1] * nh + r, 0))),
        out_shape=jax.ShapeDtypeStruct((2 * half, C), F32), compiler_params=_params(32, 1),
    )(pos, pairs, got)


def _adamw(name, w, g, m, v):
    R, C = w.shape
    rb = _row_block(R)
    c1 = 1.0 - ADAM_B1 ** ADAM_STEP
    c2 = 1.0 - ADAM_B2 ** ADAM_STEP

    def body(w_ref, g_ref, m_ref, v_ref, d_ref, nm_ref, nv_ref):
        gv = g_ref[...]
        nm = ADAM_B1 * m_ref[...] + (1.0 - ADAM_B1) * gv
        nv = ADAM_B2 * v_ref[...] + (1.0 - ADAM_B2) * (gv * gv)
        nm_ref[...] = nm
        nv_ref[...] = nv
        d_ref[...] = -ADAM_LR * ((nm / c1) / (jnp.sqrt(nv / c2) + ADAM_EPS) + ADAM_WD * w_ref[...])

    spec = pl.BlockSpec((rb, C), lambda r: (r, 0))
    sds = jax.ShapeDtypeStruct(w.shape, F32)
    return pl.pallas_call(
        body, name=name, grid=(R // rb,), in_specs=[spec] * 4, out_specs=[spec] * 3,
        out_shape=[sds, sds, sds], compiler_params=_params(40, 1),
    )(w, g, m, v)


def _rel_index():
    m = np.arange(2 * ATT_BLK)
    off = np.where(m < ATT_BLK, m, m - 2 * ATT_BLK)
    rel = np.stack([ATT_BLK * d - off for d in range(N_ATT_TILES)])
    return np.clip(rel, -MAX_REL, MAX_REL) + MAX_REL


def _local_step(x, tgt, g1, w_in, cw, cb, lg, lb, bias, w_out, g2, g3, w_up, fw, fb, w_down, g4, pos=None):
    T = x.shape[0]
    dist = pos is not None
    idx = _rel_index()

    (u, a, qkv), got = _fwd_in_proj(x, g1, w_in, _fetch_rider([w_out, w_down]) if dist else None)
    if dist:
        w_out, w_down = got
    (co, hc), got = _fwd_conv(a, cw, cb, lg, lb, _merge_riders(_forward_rider([w_out, w_down]),
                                                               _fetch_rider([w_up])) if dist else None)
    if dist:
        w_out, w_down, w_up = got
    (ao, lse), got = _fwd_attn(qkv, bias, _forward_rider([w_up]) if dist else None)
    if dist:
        (w_up,) = got
        w_out, w_down = w_out.reshape(D_MODEL, D_MODEL), w_down.reshape(D_FF, D_MODEL)
    mixed, h1, u2 = _fwd_out_proj(co, ao, w_out, x, g2, g3)
    w_cat = _pair_up_weights(w_up)
    hf, pre, act, loss, dy, df, dg4 = _fwd_ffn_loss(u2, w_cat, fw, fb, w_down, h1, tgt, g4)

    tk = min(2048, T)
    du2p, dhf, dfw_g, dfw_v = _bwd_ffn(df, hf, pre, w_cat, fw, w_down)
    gw_up = _wgrad(
        "wgrad_up", [u2], pl.BlockSpec((tk, D_MODEL), lambda s, k: (k, 0)),
        dhf, pl.BlockSpec((1, tk, FF_SHARD), lambda s, k: (s // 2, k, s % 2)),
        pl.BlockSpec((1, D_MODEL, FF_SHARD), lambda s, k: (s, 0, 0)),
        jax.ShapeDtypeStruct((N_CHIPS, D_MODEL, FF_SHARD), F32), N_CHIPS, T, tk)
    gw_down = _wgrad(
        "wgrad_down", [act], pl.BlockSpec((tk, FF_SHARD), lambda s, k: (k, s)),
        df, pl.BlockSpec((tk, D_MODEL), lambda s, k: (k, 0)),
        pl.BlockSpec((FF_SHARD, D_MODEL), lambda s, k: (s, 0)),
        jax.ShapeDtypeStruct((D_FF, D_MODEL), F32), 2, T, tk).reshape(N_CHIPS, D_FF // N_CHIPS, D_MODEL)
    (dh1, dmx, dco, dao, dg3, dg2), _ = _bwd_mid(du2p, dy, h1, mixed, g3, g2, w_out)
    gw_out = _wgrad(
        "wgrad_out", [co, ao], pl.BlockSpec((tk, CONV_W), lambda s, k: (k, 0)),
        dmx, pl.BlockSpec((tk, D_MODEL), lambda s, k: (k, 0)),
        pl.BlockSpec((CONV_W, D_MODEL), lambda s, k: (s, 0)),
        jax.ShapeDtypeStruct((D_MODEL, D_MODEL), F32), 2, T, tk,
        select=lambda s: s, rider=_pair_exchange_rider([gw_up, gw_down]) if dist else None)
    if dist:
        gw_out, got = gw_out
        p_up = _pair_sum("pair_sum_w_up", gw_up, got[0], pos)
        p_down = _pair_sum("pair_sum_w_down", gw_down, got[1], pos)
    gw_out = gw_out.reshape(N_CHIPS, D_MODEL // N_CHIPS, D_MODEL)
    (dproj, dsacc), got = _bwd_attn(
        qkv, ao, dao, lse, bias,
        _merge_riders(_chip_exchange_rider([p_up, p_down]), _pair_exchange_rider([gw_out])) if dist else None)
    if dist:
        gw_up = _chip_sum("chip_sum_w_up", p_up, got[0], pos)
        gw_down = _chip_sum("chip_sum_w_down", p_down, got[1], pos)
        p_out = _pair_sum("pair_sum_w_out", gw_out, got[2], pos)
    (dproj, dcw, dcb, dlg, dlb), got = _bwd_conv(
        dproj, a, dco, hc, cw, lg, lb,
        _merge_riders(_pair_gather_rider([gw_up, gw_down]), _chip_exchange_rider([p_out])) if dist else None)
    if dist:
        gw_up, gw_down = got[:2]
        gw_out = _chip_sum("chip_sum_w_out", p_out, got[2], pos)
    gw_in = _wgrad(
        "wgrad_in", [u], pl.BlockSpec((tk, D_MODEL), lambda s, k: (k, 0)),
        dproj, pl.BlockSpec((tk, IN_SHARD), lambda s, k: (k, s)),
        pl.BlockSpec((1, D_MODEL, IN_SHARD), lambda s, k: (s, 0, 0)),
        jax.ShapeDtypeStruct((N_CHIPS, D_MODEL, IN_SHARD), F32), N_CHIPS, T, tk)
    if dist:
        got = _alone("pair_exchange_w_in", _merge_riders(_pair_exchange_rider([gw_in]), _pair_gather_rider([gw_out])))
        p_in, gw_out = _pair_sum("pair_sum_w_in", gw_in, got[0], pos), got[1]
    (gx, dg1), _ = _bwd_in_proj(dproj, w_in, x, dh1, g1)
    (diag,), got = _diag_sums(dsacc, _chip_exchange_rider([p_in]) if dist else None)
    if dist:
        gw_in = _chip_sum("chip_sum_w_in", p_in, got[0], pos)

    diag = diag.reshape(N_ATT_TILES, N_HEADS, 2 * ATT_BLK)
    onehot = np.zeros((N_ATT_TILES, 2 * ATT_BLK, 2 * MAX_REL + 1), np.float32)
    for d in range(N_ATT_TILES):
        onehot[d, np.arange(2 * ATT_BLK), idx[d]] = 1.0
    drel = jnp.einsum("dhm,dmr->hr", diag, jnp.asarray(onehot), precision=lax.Precision.HIGHEST)

    small = dict(norm_mix_pre=dg1, conv_dw_w=dcw[:CONV_K], conv_dw_b=dcb, conv_ln_g=dlg, conv_ln_b=dlb,
                 rel_bias=drel, norm_mix_post=dg2, norm_ffn_pre=dg3,
                 ffn_dw_w=jnp.concatenate([dfw_g[0, :3], dfw_g[1, :3], dfw_v[0, :3], dfw_v[1, :3]], axis=1),
                 ffn_dw_b=jnp.concatenate([dfw_g[0, 3:4], dfw_g[1, 3:4], dfw_v[0, 3:4], dfw_v[1, 3:4]], axis=1),
                 norm_ffn_post=dg4)
    return loss, gx, small, dict(w_in=gw_in, w_out=gw_out, w_up=gw_up, w_down=gw_down)


SMALL_ORDER = ["norm_mix_pre", "conv_dw_b", "conv_ln_g", "conv_ln_b", "rel_bias", "norm_mix_post",
               "norm_ffn_pre", "ffn_dw_b", "norm_ffn_post", "conv_dw_w", "ffn_dw_w"]


def _pack(parts):
    rows = []
    for p in parts:
        width = -(-p.shape[1] // LANES) * LANES
        rows.append(jnp.pad(p, ((0, 0), (0, width - p.shape[1]))).reshape(-1, LANES))
    packed = jnp.concatenate(rows, axis=0)
    pad = -packed.shape[0] % 8
    return jnp.pad(packed, ((0, pad), (0, 0)))


def _unpack(packed, shapes):
    out, r = [], 0
    for shp in shapes:
        width = -(-shp[1] // LANES) * LANES
        n = shp[0] * width // LANES
        out.append(packed[r:r + n].reshape(shp[0], width)[:, :shp[1]])
        r += n
    return out


WEIGHTS = ["norm_mix_pre", "w_in", "conv_dw_w", "conv_dw_b", "conv_ln_g", "conv_ln_b", "rel_bias", "w_out",
           "norm_mix_post", "norm_ffn_pre", "w_up", "ffn_dw_w", "ffn_dw_b", "w_down", "norm_ffn_post"]
BIG = ["w_in", "w_out", "w_up", "w_down"]


def kernel(x, norm_mix_pre, w_in, conv_dw_w, conv_dw_b, conv_ln_g, conv_ln_b, rel_bias, w_out, norm_mix_post, norm_ffn_pre, w_up, ffn_dw_w, ffn_dw_b, w_down, norm_ffn_post, loss_target, m_norm_mix_pre, m_w_in, m_conv_dw_w, m_conv_dw_b, m_conv_ln_g, m_conv_ln_b, m_rel_bias, m_w_out, m_norm_mix_post, m_norm_ffn_pre, m_w_up, m_ffn_dw_w, m_ffn_dw_b, m_w_down, m_norm_ffn_post, v_norm_mix_pre, v_w_in, v_conv_dw_w, v_conv_dw_b, v_conv_ln_g, v_conv_ln_b, v_rel_bias, v_w_out, v_norm_mix_post, v_norm_ffn_pre, v_w_up, v_ffn_dw_w, v_ffn_dw_b, v_w_down, v_norm_ffn_post):
    args = locals()
    w = {n: args[n][0] for n in WEIGHTS}
    m = {n: args["m_" + n][0] for n in WEIGHTS}
    v = {n: args["v_" + n][0] for n in WEIGHTS}
    for d in (w, m, v):
        d["rel_bias"] = d["rel_bias"].reshape(N_HEADS, 2 * MAX_REL + 1)
        for n in ("norm_mix_pre", "conv_dw_b", "conv_ln_g", "conv_ln_b", "norm_mix_post", "norm_ffn_pre",
                  "ffn_dw_b", "norm_ffn_post"):
            d[n] = d[n].reshape(1, -1)
    shard = 2 * lax.axis_index("x") + lax.axis_index("y")

    cw_sh = jnp.pad(w["conv_dw_w"], ((0, CONV_HALO - CONV_K), (0, 0)))
    fw_sh = jnp.pad(w["ffn_dw_w"], ((0, FF_HALO - 3), (0, 0)))
    pos = jnp.stack([shard, lax.axis_index("c")]).astype(jnp.int32)
    bufs = {n: _place("place_" + n, w[n], pos, BF16) for n in BIG}
    first = [bufs["w_in"], _place("place_conv_dw_w", cw_sh, pos, F32), _place("place_ffn_dw_w", fw_sh, pos, F32)]
    (bias,), first = _bias_tiles(w["rel_bias"], _fetch_rider(first))
    w_in_f, cw_f, fw_f = _alone("all_gather_forward", _forward_rider(list(first)))
    cw_full = jnp.transpose(cw_f, (1, 0, 2)).reshape(CONV_HALO, CONV_W)

    loss, gx, small, big = _local_step(
        x[0], loss_target[0], w["norm_mix_pre"], _join_columns(w_in_f), cw_full, w["conv_dw_b"], w["conv_ln_g"],
        w["conv_ln_b"], bias, bufs["w_out"], w["norm_mix_post"],
        w["norm_ffn_pre"], bufs["w_up"], fw_f, w["ffn_dw_b"].reshape(N_CHIPS, 1, FF_SHARD),
        bufs["w_down"], w["norm_ffn_post"], pos)
    (gsum,), (big["w_in"],) = _all_reduce_small(_pack([small[n] for n in SMALL_ORDER] + [loss]),
                                                _pair_gather_rider([big["w_in"]]))

    grads, deltas, new_m, new_v = {}, {}, {}, {}
    for n in BIG:
        grads[n] = big[n]
        deltas[n], new_m[n], new_v[n] = _adamw("adamw_" + n, w[n], big[n], m[n], v[n])
    shapes = [small[n].shape for n in SMALL_ORDER]
    *reduced, total = _unpack(gsum, shapes + [loss.shape])
    gs = dict(zip(SMALL_ORDER, reduced))
    gs["conv_dw_w"] = lax.dynamic_slice_in_dim(gs["conv_dw_w"], shard * LANES, LANES, axis=1)
    gs["ffn_dw_w"] = lax.dynamic_slice_in_dim(gs["ffn_dw_w"], shard * FF_SHARD, FF_SHARD, axis=1)
    shapes = [gs[n].shape for n in SMALL_ORDER]
    d_p, m_p, v_p = _adamw("adamw_small", _pack([w[n] for n in SMALL_ORDER]), _pack([gs[n] for n in SMALL_ORDER]),
                           _pack([m[n] for n in SMALL_ORDER]), _pack([v[n] for n in SMALL_ORDER]))
    for dst, packed in ((deltas, d_p), (new_m, m_p), (new_v, v_p)):
        dst.update(zip(SMALL_ORDER, _unpack(packed, shapes)))
    grads.update(gs)

    outs = [total[0, 0], gx[None]]
    for group in (grads, deltas, new_m, new_v):
        outs += [group[n].reshape(args[n].shape) for n in WEIGHTS]
    return tuple(outs)
```

```python
import functools
import math
from typing import Callable, NamedTuple

import numpy as np
import jax
import jax.numpy as jnp
from jax import lax
from jax.experimental import pallas as pl
from jax.experimental.pallas import tpu as pltpu

F32 = jnp.float32
BF16 = jnp.bfloat16

D_MODEL = 1024
CONV_W = 512
ATTN_W = 512
N_HEADS = 8
HEAD_DIM = 64
CHUNK = 64
N_LEFT = 8
MAX_REL = 128
CONV_K = 31
CONV_HALO = 32
D_FF = 2816
FF_SHARD = 1408
IN_COLS = 2560
IN_SHARD = 640
EPS = 1e-6
NEG_INF = -1e30
ATT_BLK = 256
N_ATT_TILES = 3
LANES = 128
SUBLANES = 8
N_CHIPS = 4

ADAM_LR = 0.001
ADAM_B1 = 0.9
ADAM_B2 = 0.999
ADAM_EPS = 1e-08
ADAM_WD = 0.01
ADAM_STEP = 10

MESH = pl.DeviceIdType.MESH
ANY = pl.BlockSpec(memory_space=pl.ANY)
VMEM_FULL = pl.BlockSpec(memory_space=pltpu.VMEM)


def _params(vmem_mb, n_grid=0):
    sem = ("arbitrary",) * n_grid if n_grid else None
    return pltpu.CompilerParams(dimension_semantics=sem, vmem_limit_bytes=vmem_mb << 20)


class _Rider(NamedTuple):
    operands: list
    out_shape: list
    aliases: dict
    sems: list
    start: Callable
    finish: Callable


def _merge_riders(a, b):
    ia, oa, sa = len(a.operands), len(a.out_shape), len(a.sems)

    def start(ins, outs, sems):
        a.start(ins[:ia], outs[:oa], sems[:sa])
        b.start(ins[ia:], outs[oa:], sems[sa:])

    def finish(ins, outs, sems):
        a.finish(ins[:ia], outs[:oa], sems[:sa])
        b.finish(ins[ia:], outs[oa:], sems[sa:])

    aliases = {**a.aliases, **{k + ia: v + oa for k, v in b.aliases.items()}}
    return _Rider(a.operands + b.operands, a.out_shape + b.out_shape, aliases, a.sems + b.sems, start, finish)


PIN_BYTES = 1 << 20


def _big(a):
    return math.prod(a.shape) * jnp.dtype(a.dtype).itemsize >= PIN_BYTES


def _pin_args(args):
    return [pltpu.with_memory_space_constraint(a, pltpu.HBM) if _big(a) else a for a in args]


def _call(body, rider, *, grid=(), in_specs=(), out_specs=(), out_shape=(), scratch_shapes=(),
          input_output_aliases=None, **kwargs):
    in_specs, out_specs = list(in_specs), list(out_specs)
    pin_out = lambda shapes: [pltpu.HBM(s.shape, s.dtype) if _big(s) else s for s in shapes]
    out_shape = pin_out(out_shape)
    scratch, aliases = list(scratch_shapes), dict(input_output_aliases or {})
    if rider is None:
        plain = pl.pallas_call(body, grid=grid, in_specs=in_specs, out_specs=out_specs, out_shape=out_shape,
                               scratch_shapes=scratch, input_output_aliases=aliases, **kwargs)
        return lambda *args: (plain(*_pin_args(args)), [])
    n_in, n_out, n_scr = len(in_specs), len(out_specs), len(scratch)
    r_in, r_out = len(rider.operands), len(rider.out_shape)

    def carried(*refs):
        ins, r_ins, refs = refs[:n_in], refs[n_in:n_in + r_in], refs[n_in + r_in:]
        outs, r_outs, refs = refs[:n_out], refs[n_out:n_out + r_out], refs[n_out + r_out:]
        scr, r_sems = refs[:n_scr], refs[n_scr:]
        if not grid:
            rider.start(r_ins, r_outs, r_sems)
            body(*ins, *outs, *scr)
            rider.finish(r_ins, r_outs, r_sems)
            return
        at = [pl.program_id(d) for d in range(len(grid))]
        first = functools.reduce(jnp.logical_and, [p == 0 for p in at])
        last = functools.reduce(jnp.logical_and, [p == n - 1 for p, n in zip(at, grid)])

        @pl.when(first)
        def _():
            rider.start(r_ins, r_outs, r_sems)

        body(*ins, *outs, *scr)

        @pl.when(last)
        def _():
            rider.finish(r_ins, r_outs, r_sems)

    aliases.update({n_in + k: n_out + v for k, v in rider.aliases.items()})
    both = pl.pallas_call(carried, grid=grid, in_specs=in_specs + [ANY] * r_in, out_specs=out_specs + [ANY] * r_out,
                          out_shape=out_shape + pin_out(rider.out_shape), scratch_shapes=scratch + rider.sems,
                          input_output_aliases=aliases, **kwargs)

    def run(*args):
        res = both(*_pin_args(args), *rider.operands)
        return res[:n_out], res[n_out:]

    return run


def _sigmoid(v):
    return 1.0 / (1.0 + jnp.exp(-v))


def _dot(a, b):
    return jnp.dot(a, b, preferred_element_type=F32)


def _dot_nt(a, b):
    return lax.dot_general(a, b, (((1,), (1,)), ((), ())), preferred_element_type=F32)


def _dot_tn(a, b):
    return lax.dot_general(a, b, (((0,), (0,)), ((), ())), preferred_element_type=F32)


def _rms_fwd(v, g):
    r = lax.rsqrt(jnp.mean(v * v, axis=-1, keepdims=True) + EPS)
    return v * r * g, r


def _rms_bwd(dy, v, g):
    r = lax.rsqrt(jnp.mean(v * v, axis=-1, keepdims=True) + EPS)
    vh = v * r
    dvh = dy * g
    dv = r * (dvh - vh * jnp.mean(dvh * vh, axis=-1, keepdims=True))
    return dv, dy * vh


def _join_columns(w):
    S, R, C = w.shape
    rb = 256

    def body(w_ref, o_ref):
        for s in range(S):
            o_ref[:, s * C:(s + 1) * C] = w_ref[s]

    return pl.pallas_call(
        body, name="join_columns", grid=(R // rb,),
        in_specs=[pl.BlockSpec((S, rb, C), lambda r: (0, r, 0))],
        out_specs=pl.BlockSpec((rb, S * C), lambda r: (r, 0)),
        out_shape=jax.ShapeDtypeStruct((R, S * C), w.dtype),
        compiler_params=_params(32, 1),
    )(w)


GLU_COLS = 2 * CONV_W


def _fwd_in_proj(x, g1, w_in, rider=None):
    T = x.shape[0]
    tm = 1024

    def body(x_ref, g_ref, w_ref, u_ref, a_ref, qkv_ref):
        u, _ = _rms_fwd(x_ref[...], g_ref[...])
        u = u.astype(BF16)
        u_ref[...] = u
        a_ref[...] = _dot(u, w_ref[:, :GLU_COLS])
        qkv_ref[...] = _dot(u, w_ref[:, GLU_COLS:]).astype(BF16)

    return _call(
        body, rider, name="fwd_in_proj", grid=(T // tm,),
        in_specs=[pl.BlockSpec((tm, D_MODEL), lambda i: (i, 0)),
                  pl.BlockSpec((1, D_MODEL), lambda i: (0, 0)),
                  pl.BlockSpec((D_MODEL, IN_COLS), lambda i: (0, 0))],
        out_specs=[pl.BlockSpec((tm, D_MODEL), lambda i: (i, 0)),
                   pl.BlockSpec((tm, 1024), lambda i: (i, 0)),
                   pl.BlockSpec((tm, 1536), lambda i: (i, 0))],
        out_shape=[jax.ShapeDtypeStruct((T, D_MODEL), BF16),
                   jax.ShapeDtypeStruct((T, 1024), F32),
                   jax.ShapeDtypeStruct((T, 1536), BF16)],
        compiler_params=_params(56, 1),
    )(x, g1, w_in)


def _fill_shifted(ext, shifted, tm):
    n = tm + CONV_HALO - SUBLANES
    for j in range(1, SUBLANES):
        shifted[j - 1] = ext[j:j + n, :]


def _shifted_rows(ext, shifted, start, rows):
    j = start % SUBLANES
    if j == 0:
        return ext[start:start + rows, :]
    return shifted[j - 1, start - j:start - j + rows, :]


def _fwd_conv(a, cw, cb, lg, lb, rider=None):
    T = a.shape[0]
    tm = 512
    rc = 64

    def body(a_ref, w_ref, b_ref, lg_ref, lb_ref, co_ref, hc_ref, hext, hsh):
        i = pl.program_id(0)

        @pl.when(i == 0)
        def _():
            hext[0:CONV_HALO, :] = jnp.zeros((CONV_HALO, CONV_W), F32)

        @pl.when(i > 0)
        def _():
            hext[0:CONV_HALO, :] = hext[tm:tm + CONV_HALO, :]

        hext[CONV_HALO:CONV_HALO + tm, :] = a_ref[:, :CONV_W] * _sigmoid(a_ref[:, CONV_W:])
        _fill_shifted(hext, hsh, tm)
        for c in range(tm // rc):
            acc = jnp.zeros((rc, CONV_W), F32)
            for k in range(CONV_K):
                acc = acc + w_ref[k:k + 1, :] * _shifted_rows(hext, hsh, c * rc + 2 + k, rc)
            hc = acc + b_ref[...]
            hc_ref[c * rc:(c + 1) * rc, :] = hc
            mu = jnp.mean(hc, axis=-1, keepdims=True)
            xc = hc - mu
            var = jnp.mean(xc * xc, axis=-1, keepdims=True)
            z = xc * lax.rsqrt(var + EPS) * lg_ref[...] + lb_ref[...]
            co_ref[c * rc:(c + 1) * rc, :] = (z * _sigmoid(z)).astype(BF16)

    return _call(
        body, rider, name="fwd_conv", grid=(T // tm,),
        in_specs=[pl.BlockSpec((tm, 1024), lambda i: (i, 0)),
                  pl.BlockSpec((CONV_HALO, CONV_W), lambda i: (0, 0)),
                  pl.BlockSpec((1, CONV_W), lambda i: (0, 0)),
                  pl.BlockSpec((1, CONV_W), lambda i: (0, 0)),
                  pl.BlockSpec((1, CONV_W), lambda i: (0, 0))],
        out_specs=[pl.BlockSpec((tm, CONV_W), lambda i: (i, 0)),
                   pl.BlockSpec((tm, CONV_W), lambda i: (i, 0))],
        out_shape=[jax.ShapeDtypeStruct((T, CONV_W), BF16),
                   jax.ShapeDtypeStruct((T, CONV_W), F32)],
        scratch_shapes=[pltpu.VMEM((tm + CONV_HALO, CONV_W), F32),
                        pltpu.VMEM((SUBLANES - 1, tm + CONV_HALO - SUBLANES, CONV_W), F32)],
        compiler_params=_params(40, 1),
    )(a, cw, cb, lg, lb)


def _row_skew(v, sign):
    rows, width = v.shape
    row = lax.broadcasted_iota(jnp.int32, (rows, 1), 0)
    for b in range(int(math.log2(rows))):
        shift = (1 << b) if sign > 0 else width - (1 << b)
        v = jnp.where(((row >> b) & 1) == 1, pltpu.roll(v, shift, 1), v)
    return v


def _att_visible(d):
    rq = lax.broadcasted_iota(jnp.int32, (ATT_BLK, ATT_BLK), 0) // CHUNK
    ck = lax.broadcasted_iota(jnp.int32, (ATT_BLK, ATT_BLK), 1) // CHUNK
    slack = ATT_BLK
    above = jnp.where(d == 0, 0, slack)
    below = jnp.where(d == 2, 0, slack)
    return (ck <= rq + above) & (ck >= rq - below)


def _bias_tiles(rel, rider=None):
    vec = jnp.transpose(rel[:, _rel_index()], (1, 0, 2)).reshape(N_ATT_TILES * N_HEADS, 1, 2 * ATT_BLK)

    def body(v_ref, o_ref):
        visible = _att_visible(pl.program_id(0))
        for h in range(N_HEADS):
            full = _row_skew(jnp.broadcast_to(v_ref[h], (ATT_BLK, 2 * ATT_BLK)), 1)
            o_ref[h] = jnp.where(visible, full[:, :ATT_BLK], NEG_INF)

    return _call(
        body, rider, name="bias_tiles", grid=(N_ATT_TILES,),
        in_specs=[pl.BlockSpec((N_HEADS, 1, 2 * ATT_BLK), lambda d: (d, 0, 0))],
        out_specs=[pl.BlockSpec((N_HEADS, ATT_BLK, ATT_BLK), lambda d: (d, 0, 0))],
        out_shape=[jax.ShapeDtypeStruct((N_ATT_TILES * N_HEADS, ATT_BLK, ATT_BLK), F32)],
        compiler_params=_params(32, 1),
    )(vec)


def _diag_sums(ds, rider=None):
    def body(d_ref, o_ref):
        wide = jnp.concatenate([d_ref[0], jnp.zeros((ATT_BLK, ATT_BLK), F32)], axis=1)
        o_ref[0] = jnp.sum(_row_skew(wide, -1), axis=0, keepdims=True)

    return _call(
        body, rider, name="diag_sums", grid=(N_ATT_TILES * N_HEADS,),
        in_specs=[pl.BlockSpec((1, ATT_BLK, ATT_BLK), lambda n: (n, 0, 0))],
        out_specs=[pl.BlockSpec((1, 1, 2 * ATT_BLK), lambda n: (n, 0, 0))],
        out_shape=[jax.ShapeDtypeStruct((N_ATT_TILES * N_HEADS, 1, 2 * ATT_BLK), F32)],
        compiler_params=_params(16, 1),
    )(ds)


def _head_mask(h):
    lane = lax.broadcasted_iota(jnp.int32, (1, LANES), 1)
    return (lane // HEAD_DIM) == (h % 2)


def _fwd_attn(qkv, bias, rider=None):
    T = qkv.shape[0]
    nb = T // ATT_BLK
    scale = HEAD_DIM ** -0.5

    def body(q_ref, k0_ref, k1_ref, k2_ref, v0_ref, v1_ref, v2_ref, b_ref, o_ref, lse_ref):
        i = pl.program_id(0)

        @pl.when(i >= N_ATT_TILES - 1)
        def _():
            block(i, False, q_ref, k0_ref, k1_ref, k2_ref, v0_ref, v1_ref, v2_ref, b_ref, o_ref, lse_ref)

        @pl.when(i < N_ATT_TILES - 1)
        def _():
            block(i, True, q_ref, k0_ref, k1_ref, k2_ref, v0_ref, v1_ref, v2_ref, b_ref, o_ref, lse_ref)

    def block(i, hide_absent, q_ref, k0_ref, k1_ref, k2_ref, v0_ref, v1_ref, v2_ref, b_ref, o_ref, lse_ref):
        k_refs = (k0_ref, k1_ref, k2_ref)
        v_refs = (v0_ref, v1_ref, v2_ref)
        lane = lax.broadcasted_iota(jnp.int32, (1, LANES), 1)
        lse_tile = jnp.zeros((ATT_BLK, LANES), F32)
        for g in range(N_HEADS // 2):
            cols = slice(g * LANES, (g + 1) * LANES)
            qg = q_ref[:, cols] * scale
            og = jnp.zeros((ATT_BLK, LANES), F32)
            both = jnp.concatenate([jnp.where(_head_mask(h), qg, jnp.zeros_like(qg)) for h in (2 * g, 2 * g + 1)],
                                   axis=0)
            raw = [_dot_nt(both, k_refs[d][:, cols]) for d in range(N_ATT_TILES)]
            for h in (2 * g, 2 * g + 1):
                hm = _head_mask(h)
                mine = slice((h % 2) * ATT_BLK, (h % 2 + 1) * ATT_BLK)
                s = []
                for d in range(N_ATT_TILES):
                    sd = raw[d][mine] + b_ref[d * N_HEADS + h]
                    if d > 0 and hide_absent:
                        sd = jnp.where(i >= d, sd, NEG_INF)
                    s.append(sd)
                m = jnp.maximum(jnp.maximum(jnp.max(s[0], axis=-1, keepdims=True),
                                            jnp.max(s[1], axis=-1, keepdims=True)),
                                jnp.max(s[2], axis=-1, keepdims=True))
                p = [jnp.exp(sd - m) for sd in s]
                l = (jnp.sum(p[0], axis=-1, keepdims=True) + jnp.sum(p[1], axis=-1, keepdims=True)
                     + jnp.sum(p[2], axis=-1, keepdims=True))
                oh = jnp.zeros((ATT_BLK, LANES), F32)
                for d in range(N_ATT_TILES):
                    vg = v_refs[d][:, cols]
                    oh = oh + _dot(p[d].astype(BF16), jnp.where(hm, vg, jnp.zeros_like(vg)))
                og = og + oh / l
                lse_tile = jnp.where(lane == h, m + jnp.log(l), lse_tile)
            o_ref[:, cols] = og.astype(BF16)
        lse_ref[...] = lse_tile

    def kv_spec(d, col):
        return pl.BlockSpec((ATT_BLK, ATTN_W), lambda i: (jnp.maximum(i - d, 0), col))

    return _call(
        body, rider, name="fwd_attn", grid=(nb,),
        in_specs=[pl.BlockSpec((ATT_BLK, ATTN_W), lambda i: (i, 0)),
                  kv_spec(0, 1), kv_spec(1, 1), kv_spec(2, 1),
                  kv_spec(0, 2), kv_spec(1, 2), kv_spec(2, 2),
                  pl.BlockSpec((N_ATT_TILES * N_HEADS, ATT_BLK, ATT_BLK), lambda i: (0, 0, 0))],
        out_specs=[pl.BlockSpec((ATT_BLK, ATTN_W), lambda i: (i, 0)),
                   pl.BlockSpec((ATT_BLK, LANES), lambda i: (i, 0))],
        out_shape=[jax.ShapeDtypeStruct((T, ATTN_W), BF16),
                   jax.ShapeDtypeStruct((T, LANES), F32)],
        compiler_params=_params(40, 1),
    )(qkv, qkv, qkv, qkv, qkv, qkv, qkv, bias)


def _fwd_out_proj(co, ao, w_out, x, g2, g3):
    T = x.shape[0]
    tm = 1024

    def body(co_ref, ao_ref, w_ref, x_ref, g2_ref, g3_ref, mixed_ref, h1_ref, u2_ref):
        mixed = _dot(co_ref[...], w_ref[0:CONV_W, :]) + _dot(ao_ref[...], w_ref[CONV_W:, :])
        mixed_ref[...] = mixed.astype(BF16)
        y, _ = _rms_fwd(mixed, g2_ref[...])
        h1 = x_ref[...] + y
        h1_ref[...] = h1
        u2, _ = _rms_fwd(h1, g3_ref[...])
        u2_ref[...] = u2.astype(BF16)

    row = lambda w: pl.BlockSpec((tm, w), lambda i: (i, 0))
    vec = pl.BlockSpec((1, D_MODEL), lambda i: (0, 0))
    return _call(
        body, None, name="fwd_out_proj", grid=(T // tm,),
        in_specs=[row(CONV_W), row(ATTN_W), pl.BlockSpec((D_MODEL, D_MODEL), lambda i: (0, 0)),
                  row(D_MODEL), vec, vec],
        out_specs=[row(D_MODEL), row(D_MODEL), row(D_MODEL)],
        out_shape=[jax.ShapeDtypeStruct((T, D_MODEL), BF16),
                   jax.ShapeDtypeStruct((T, D_MODEL), F32),
                   jax.ShapeDtypeStruct((T, D_MODEL), BF16)],
        compiler_params=_params(56, 1),
    )(co, ao, w_out, x, g2, g3)[0]


GELU_C = math.sqrt(2.0 / math.pi)
GELU_A = 0.044715


def _gelu_and_grad(v):
    sq = v * v
    th = jnp.tanh(v * (GELU_C + (GELU_C * GELU_A) * sq))
    half = 0.5 + 0.5 * th
    gl = v * half
    dgl = half + (v * (half * (1.0 - th))) * (GELU_C + (3.0 * GELU_C * GELU_A) * sq)
    return gl, dgl


FF_TM = 256
FF_HALO = 16
FF_CHUNKS = [(lo, min(lo + 256, FF_SHARD)) for lo in range(0, FF_SHARD, 256)]


def _rows_before(prev, cur):
    ext = jnp.concatenate([prev, cur], axis=0)
    return pltpu.roll(ext, 1, 0)[SUBLANES:], pltpu.roll(ext, 2, 0)[SUBLANES:]


def _rows_after(cur, nxt):
    ext = jnp.concatenate([cur, nxt], axis=0)
    n = ext.shape[0]
    return pltpu.roll(ext, n - 1, 0)[:cur.shape[0]], pltpu.roll(ext, n - 2, 0)[:cur.shape[0]]


def _pair_up_weights(w_up):
    rb = 256

    def body(g_ref, v_ref, o_ref):
        for lo, hi in FF_CHUNKS:
            o_ref[0, :, 2 * lo:lo + hi] = g_ref[0, :, lo:hi]
            o_ref[0, :, lo + hi:2 * hi] = v_ref[0, :, lo:hi]

    return pl.pallas_call(
        body, name="pair_up_weights", grid=(2, D_MODEL // rb),
        in_specs=[pl.BlockSpec((1, rb, FF_SHARD), lambda s, r: (s, r, 0)),
                  pl.BlockSpec((1, rb, FF_SHARD), lambda s, r: (s + 2, r, 0))],
        out_specs=pl.BlockSpec((1, rb, 2 * FF_SHARD), lambda s, r: (s, r, 0)),
        out_shape=jax.ShapeDtypeStruct((2, D_MODEL, 2 * FF_SHARD), w_up.dtype),
        compiler_params=_params(32, 2),
    )(w_up, w_up)


def _fwd_ffn_loss(u2, w_cat, fw, fb, w_down, h1, tgt, g4):
    T = u2.shape[0]
    tm = FF_TM

    def body(u_ref, w_ref, fw_ref, fb_ref, wd_ref, h1_ref, t_ref, g_ref,
             hf_ref, pre_ref, act_ref, loss_ref, dy_ref, df_ref, dg_ref, carg, carv):
        i = pl.program_id(0)

        @pl.when(i == 0)
        def _():
            carg[...] = jnp.zeros(carg.shape, F32)
            carv[...] = jnp.zeros(carv.shape, F32)

        u = u_ref[...]
        f = None
        chunks = [(s, lo, hi) for s in range(2) for lo, hi in FF_CHUNKS]
        up = lambda s, lo, hi: _dot(u, w_ref[s, :, 2 * lo:2 * hi])
        ahead = up(*chunks[0])
        for c, (s, lo, hi) in enumerate(chunks):
            conv = []
            hs = (ahead[:, :hi - lo], ahead[:, hi - lo:])
            if c + 1 < len(chunks):
                ahead = up(*chunks[c + 1])
            at = slice(s * FF_SHARD + lo, s * FF_SHARD + hi)
            for n, car in enumerate((carg, carv)):
                h0 = hs[n]
                hf_ref[n, :, at] = h0.astype(BF16)
                h1v, h2v = _rows_before(car[:, at], h0)
                car[:, at] = h0[tm - SUBLANES:, :]
                conv.append(fw_ref[2 * n + s, 0:1, lo:hi] * h2v + fw_ref[2 * n + s, 1:2, lo:hi] * h1v
                            + fw_ref[2 * n + s, 2:3, lo:hi] * h0 + fb_ref[2 * n + s, :, lo:hi])
            pre_ref[0, :, at] = conv[0].astype(BF16)
            pre_ref[1, :, at] = conv[1].astype(BF16)
            gl, _ = _gelu_and_grad(conv[0])
            act = (gl * conv[1]).astype(BF16)
            act_ref[:, at] = act
            term = _dot(act, wd_ref[at, :])
            f = term if f is None else f + term

        r, _ = _rms_fwd(f, g_ref[...])
        e = (h1_ref[...] + r) - t_ref[...]
        dy = e * (1.0 / D_MODEL)
        dy_ref[...] = dy
        df, dg_rows = _rms_bwd(dy, f, g_ref[...])
        df_ref[...] = df.astype(BF16)
        part = 0.5 * jnp.sum(jnp.mean(e * e, axis=-1, keepdims=True), axis=0, keepdims=True)
        dg = jnp.sum(dg_rows, axis=0, keepdims=True)

        @pl.when(i == 0)
        def _():
            loss_ref[...] = part
            dg_ref[...] = dg

        @pl.when(i > 0)
        def _():
            loss_ref[...] += part
            dg_ref[...] += dg

    row = lambda w: pl.BlockSpec((tm, w), lambda i: (i, 0))
    vec = pl.BlockSpec((1, D_MODEL), lambda i: (0, 0))
    once = pl.Buffered(1)
    return _call(
        body, None, name="fwd_ffn_loss", grid=(T // tm,),
        in_specs=[row(D_MODEL),
                  pl.BlockSpec((2, D_MODEL, 2 * FF_SHARD), lambda i: (0, 0, 0), pipeline_mode=once),
                  pl.BlockSpec((N_CHIPS, FF_HALO, FF_SHARD), lambda i: (0, 0, 0)),
                  pl.BlockSpec((N_CHIPS, 1, FF_SHARD), lambda i: (0, 0, 0)),
                  pl.BlockSpec((D_FF, D_MODEL), lambda i: (0, 0), pipeline_mode=once),
                  row(D_MODEL), row(D_MODEL), vec],
        out_specs=[pl.BlockSpec((2, tm, D_FF), lambda i: (0, i, 0)),
                   pl.BlockSpec((2, tm, D_FF), lambda i: (0, i, 0)),
                   row(D_FF), pl.BlockSpec((1, 1), lambda i: (0, 0)), row(D_MODEL), row(D_MODEL), vec],
        out_shape=[jax.ShapeDtypeStruct((2, T, D_FF), BF16),
                   jax.ShapeDtypeStruct((2, T, D_FF), BF16),
                   jax.ShapeDtypeStruct((T, D_FF), BF16),
                   jax.ShapeDtypeStruct((1, 1), F32),
                   jax.ShapeDtypeStruct((T, D_MODEL), F32),
                   jax.ShapeDtypeStruct((T, D_MODEL), BF16),
                   jax.ShapeDtypeStruct((1, D_MODEL), F32)],
        scratch_shapes=[pltpu.VMEM((SUBLANES, D_FF), F32), pltpu.VMEM((SUBLANES, D_FF), F32)],
        compiler_params=_params(60, 1),
    )(u2, w_cat, fw, fb, w_down, h1, tgt, g4)[0]


def _bwd_ffn(df, hf, pre, w_cat, fw, w_down):
    T = df.shape[0]
    tm = FF_TM
    ni = T // tm

    def body(df_ref, hf_ref, pre_ref, wd_ref, w_ref, fw_ref,
             du_ref, dhf_ref, dwg_ref, dwv_ref, carg, carv):
        i = pl.program_id(0)

        @pl.when(i == 0)
        def _():
            dwg_ref[...] = jnp.zeros(dwg_ref.shape, F32)
            dwv_ref[...] = jnp.zeros(dwv_ref.shape, F32)
            carg[...] = jnp.zeros(carg.shape, F32)
            carv[...] = jnp.zeros(carv.shape, F32)

        df = df_ref[...]
        du = None
        chunks = [(s, lo, hi) for s in range(2) for lo, hi in FF_CHUNKS]
        cols = lambda s, lo, hi: slice(s * FF_SHARD + lo, s * FF_SHARD + hi)
        down = lambda s, lo, hi: _dot_nt(df, wd_ref[cols(s, lo, hi), :])
        ahead = down(*chunks[0])
        for c, (s, lo, hi) in enumerate(chunks):
            dact = ahead
            if c + 1 < len(chunks):
                ahead = down(*chunks[c + 1])
            at = cols(s, lo, hi)
            pre_g = pre_ref[0, :, at].astype(F32)
            pre_v = pre_ref[1, :, at].astype(F32)
            gl, dgl = _gelu_and_grad(pre_g)
            dpre = (dact * pre_v * dgl, dact * gl)
            dhs = []
            for n, (car, dw_ref) in enumerate(((carg, dwg_ref), (carv, dwv_ref))):
                dp = dpre[n]
                h0 = hf_ref[n, :, at].astype(F32)
                up1, up2 = _rows_after(dp, car[:, at])
                car[:, at] = dp[0:SUBLANES, :]
                for k, shifted in enumerate((up2, up1, dp)):
                    dw_ref[s, k:k + 1, lo:hi] += jnp.sum(shifted * h0, axis=0, keepdims=True)
                dw_ref[s, 3:4, lo:hi] += jnp.sum(dp, axis=0, keepdims=True)
                taps = 2 * n + s
                dh = (fw_ref[taps, 2:3, lo:hi] * dp + fw_ref[taps, 1:2, lo:hi] * up1
                      + fw_ref[taps, 0:1, lo:hi] * up2).astype(BF16)
                dhf_ref[n, :, at] = dh
                dhs.append(dh)
            term = _dot_nt(jnp.concatenate(dhs, axis=1), w_ref[s, :, 2 * lo:2 * hi])
            du = term if du is None else du + term
        du_ref[...] = du.astype(BF16)

    rev = lambda i: ni - 1 - i
    once = pl.Buffered(1)
    dwspec = pl.BlockSpec((2, FF_HALO, FF_SHARD), lambda i: (0, 0, 0))
    return _call(
        body, None, name="bwd_ffn", grid=(ni,),
        in_specs=[pl.BlockSpec((tm, D_MODEL), lambda i: (rev(i), 0)),
                  pl.BlockSpec((2, tm, D_FF), lambda i: (0, rev(i), 0)),
                  pl.BlockSpec((2, tm, D_FF), lambda i: (0, rev(i), 0)),
                  pl.BlockSpec((D_FF, D_MODEL), lambda i: (0, 0), pipeline_mode=once),
                  pl.BlockSpec((2, D_MODEL, 2 * FF_SHARD), lambda i: (0, 0, 0), pipeline_mode=once),
                  pl.BlockSpec((N_CHIPS, FF_HALO, FF_SHARD), lambda i: (0, 0, 0))],
        out_specs=[pl.BlockSpec((tm, D_MODEL), lambda i: (rev(i), 0)),
                   pl.BlockSpec((2, tm, D_FF), lambda i: (0, rev(i), 0)),
                   dwspec, dwspec],
        out_shape=[jax.ShapeDtypeStruct((T, D_MODEL), BF16),
                   jax.ShapeDtypeStruct((2, T, D_FF), BF16),
                   jax.ShapeDtypeStruct((2, FF_HALO, FF_SHARD), F32),
                   jax.ShapeDtypeStruct((2, FF_HALO, FF_SHARD), F32)],
        scratch_shapes=[pltpu.VMEM((SUBLANES, D_FF), F32), pltpu.VMEM((SUBLANES, D_FF), F32)],
        compiler_params=_params(60, 1),
    )(df, hf, pre, w_down, w_cat, fw)[0]


def _bwd_mid(du2p, dy, h1, mixed, g3, g2, w_out, rider=None):
    T = dy.shape[0]
    tm = 512

    def body(du_ref, dy_ref, h1_ref, mx_ref, g3_ref, g2_ref, w_ref,
             dh1_ref, dmx_ref, dco_ref, dao_ref, dg3_ref, dg2_ref):
        i = pl.program_id(0)
        dres, dg3_rows = _rms_bwd(du_ref[...].astype(F32), h1_ref[...], g3_ref[...])
        dh1 = dy_ref[...] + dres
        dh1_ref[...] = dh1
        dmx, dg2_rows = _rms_bwd(dh1, mx_ref[...].astype(F32), g2_ref[...])
        dmx = dmx.astype(BF16)
        dmx_ref[...] = dmx
        dcat = _dot_nt(dmx, w_ref[...])
        dco_ref[...] = dcat[:, :CONV_W]
        dao_ref[...] = dcat[:, CONV_W:].astype(BF16)
        dg3 = jnp.sum(dg3_rows, axis=0, keepdims=True)
        dg2 = jnp.sum(dg2_rows, axis=0, keepdims=True)

        @pl.when(i == 0)
        def _():
            dg3_ref[...] = dg3
            dg2_ref[...] = dg2

        @pl.when(i > 0)
        def _():
            dg3_ref[...] += dg3
            dg2_ref[...] += dg2

    row = lambda w: pl.BlockSpec((tm, w), lambda i: (i, 0))
    vec = pl.BlockSpec((1, D_MODEL), lambda i: (0, 0))
    return _call(
        body, rider, name="bwd_mid", grid=(T // tm,),
        in_specs=[row(D_MODEL), row(D_MODEL), row(D_MODEL),
                  row(D_MODEL), vec, vec, pl.BlockSpec((D_MODEL, D_MODEL), lambda i: (0, 0))],
        out_specs=[row(D_MODEL), row(D_MODEL), row(CONV_W), row(ATTN_W), vec, vec],
        out_shape=[jax.ShapeDtypeStruct((T, D_MODEL), F32),
                   jax.ShapeDtypeStruct((T, D_MODEL), BF16),
                   jax.ShapeDtypeStruct((T, CONV_W), F32),
                   jax.ShapeDtypeStruct((T, ATTN_W), BF16),
                   jax.ShapeDtypeStruct((1, D_MODEL), F32),
                   jax.ShapeDtypeStruct((1, D_MODEL), F32)],
        compiler_params=_params(48, 1),
    )(du2p, dy, h1, mixed, g3, g2, w_out)


def _bwd_attn(qkv, ao, dao, lse, bias, rider=None):
    T = qkv.shape[0]
    nb = T // ATT_BLK
    scale = HEAD_DIM ** -0.5

    def body(k_ref, v_ref, q0, q1, q2, do0, do1, do2, o0, o1, o2, l0, l1, l2, b_ref,
             dp_ref, ds_ref, acc1, acc2):
        j = pl.program_id(0)
        q_refs, do_refs, o_refs, l_refs = (q0, q1, q2), (do0, do1, do2), (o0, o1, o2), (l0, l1, l2)

        @pl.when(j == 0)
        def _():
            ds_ref[...] = jnp.zeros(ds_ref.shape, F32)
            acc1[...] = jnp.zeros(acc1.shape, F32)
            acc2[...] = jnp.zeros(acc2.shape, F32)

        dq_new = [[], [], []]
        dk_cols, dv_cols = [], []
        for g in range(N_HEADS // 2):
            cols = slice(g * LANES, (g + 1) * LANES)
            kg = k_ref[:, cols]
            vg = v_ref[:, cols]
            dkt = jnp.zeros((LANES, ATT_BLK), F32)
            dvt = jnp.zeros((LANES, ATT_BLK), F32)
            dqg = [jnp.zeros((ATT_BLK, LANES), F32) for _ in range(N_ATT_TILES)]
            row_head = lax.broadcasted_iota(jnp.int32, (LANES, 1), 0) // HEAD_DIM
            for d in range(N_ATT_TILES):
                qg = q_refs[d][:, cols] * scale
                dog = do_refs[d][:, cols]
                if d > 0:
                    dog = jnp.where(j + d < nb, dog, jnp.zeros_like(dog))
                prod = dog.astype(F32) * o_refs[d][:, cols].astype(F32)
                qgt = qg.astype(F32).T
                dogt = dog.astype(F32).T
                heads = (2 * g, 2 * g + 1)
                s_both = _dot_nt(jnp.concatenate([jnp.where(_head_mask(h), qg, jnp.zeros_like(qg)) for h in heads],
                                                 axis=0), kg)
                dp_both = _dot_nt(jnp.concatenate([jnp.where(_head_mask(h), dog, jnp.zeros_like(dog)) for h in heads],
                                                  axis=0), vg)
                for h in heads:
                    hm = _head_mask(h)
                    kh = jnp.where(hm, kg, jnp.zeros_like(kg))
                    mine = row_head == (h % 2)
                    rows = slice((h % 2) * ATT_BLK, (h % 2 + 1) * ATT_BLK)
                    qht = jnp.where(mine, qgt, 0.0).astype(BF16)
                    doht = jnp.where(mine, dogt, 0.0).astype(BF16)
                    delta = jnp.sum(jnp.where(hm, prod, 0.0), axis=-1, keepdims=True)
                    s = s_both[rows] + b_ref[d * N_HEADS + h]
                    p = jnp.exp(s - l_refs[d][:, h:h + 1])
                    dvt = dvt + _dot(doht, p.astype(BF16))
                    dpm = dp_both[rows]
                    dsc = p * (dpm - delta)
                    ds_ref[d * N_HEADS + h] += dsc
                    dsb = dsc.astype(BF16)
                    dqg[d] = dqg[d] + _dot(dsb, kh)
                    dkt = dkt + _dot(qht, dsb)
            for d in range(N_ATT_TILES):
                dq_new[d].append(dqg[d])
            dk_cols.append(dkt.T)
            dv_cols.append(dvt.T)
        x0, x1, x2 = (jnp.concatenate(c, axis=1) * scale for c in dq_new)
        dp_ref[:, 0:1024] = jnp.zeros((ATT_BLK, 1024), BF16)
        dp_ref[:, 1024:1536] = (acc1[...] + x0).astype(BF16)
        dp_ref[:, 1536:2048] = jnp.concatenate(dk_cols, axis=1).astype(BF16)
        dp_ref[:, 2048:2560] = jnp.concatenate(dv_cols, axis=1).astype(BF16)
        acc1[...] = acc2[...] + x1
        acc2[...] = x2

    def fwd_spec(d, width, col):
        return pl.BlockSpec((ATT_BLK, width), lambda j: (jnp.minimum(j + d, nb - 1), col))

    return _call(
        body, rider, name="bwd_attn", grid=(nb,),
        in_specs=[pl.BlockSpec((ATT_BLK, ATTN_W), lambda j: (j, 1)),
                  pl.BlockSpec((ATT_BLK, ATTN_W), lambda j: (j, 2)),
                  fwd_spec(0, ATTN_W, 0), fwd_spec(1, ATTN_W, 0), fwd_spec(2, ATTN_W, 0),
                  fwd_spec(0, ATTN_W, 0), fwd_spec(1, ATTN_W, 0), fwd_spec(2, ATTN_W, 0),
                  fwd_spec(0, ATTN_W, 0), fwd_spec(1, ATTN_W, 0), fwd_spec(2, ATTN_W, 0),
                  fwd_spec(0, LANES, 0), fwd_spec(1, LANES, 0), fwd_spec(2, LANES, 0),
                  pl.BlockSpec((N_ATT_TILES * N_HEADS, ATT_BLK, ATT_BLK), lambda j: (0, 0, 0))],
        out_specs=[pl.BlockSpec((ATT_BLK, IN_COLS), lambda j: (j, 0)),
                   pl.BlockSpec((N_ATT_TILES * N_HEADS, ATT_BLK, ATT_BLK), lambda j: (0, 0, 0))],
        out_shape=[jax.ShapeDtypeStruct((T, IN_COLS), BF16),
                   jax.ShapeDtypeStruct((N_ATT_TILES * N_HEADS, ATT_BLK, ATT_BLK), F32)],
        scratch_shapes=[pltpu.VMEM((ATT_BLK, ATTN_W), F32), pltpu.VMEM((ATT_BLK, ATTN_W), F32)],
        compiler_params=_params(56, 1),
    )(qkv, qkv, qkv, qkv, qkv, dao, dao, dao, ao, ao, ao, lse, lse, lse, bias)


def _bwd_conv(dproj, a, dco, hc, cw, lg, lb, rider=None):
    T = a.shape[0]
    tm = 512
    rc = 32
    ni = T // tm
    hb = tm // CONV_HALO

    def body(dp_in, a_ref, ap_ref, dco_ref, dcon_ref, hc_ref, hcn_ref, w_ref, lg_ref, lb_ref,
             dp_ref, dw_ref, db_ref, dlg_ref, dlb_ref, hext, dext, hsh, dsh, dwacc):
        del dp_in
        i = pl.program_id(0)

        def ln_bwd(dco_v, hc_v):
            mu = jnp.mean(hc_v, axis=-1, keepdims=True)
            xc = hc_v - mu
            rstd = lax.rsqrt(jnp.mean(xc * xc, axis=-1, keepdims=True) + EPS)
            xh = xc * rstd
            z = xh * lg_ref[...] + lb_ref[...]
            sg = _sigmoid(z)
            dz = dco_v * (sg * (1.0 + z * (1.0 - sg)))
            dxh = dz * lg_ref[...]
            dhc = rstd * (dxh - jnp.mean(dxh, axis=-1, keepdims=True)
                          - xh * jnp.mean(dxh * xh, axis=-1, keepdims=True))
            return dhc, dz * xh, dz

        hext[0:CONV_HALO, :] = jnp.where(i > 0, ap_ref[:, :CONV_W] * _sigmoid(ap_ref[:, CONV_W:]), 0.0)
        hext[CONV_HALO:CONV_HALO + tm, :] = a_ref[:, :CONV_W] * _sigmoid(a_ref[:, CONV_W:])
        dhc, dlg_rows, dlb_rows = ln_bwd(dco_ref[...], hc_ref[...])
        dext[0:tm, :] = dhc
        dhc_next, _, _ = ln_bwd(dcon_ref[...], hcn_ref[...])
        dext[tm:tm + CONV_HALO, :] = jnp.where(i < ni - 1, dhc_next, 0.0)

        @pl.when(i == 0)
        def _():
            dw_ref[...] = jnp.zeros(dw_ref.shape, F32)
            db_ref[...] = jnp.zeros(db_ref.shape, F32)
            dlg_ref[...] = jnp.zeros(dlg_ref.shape, F32)
            dlb_ref[...] = jnp.zeros(dlb_ref.shape, F32)

            dwacc[...] = jnp.zeros(dwacc.shape, F32)

        db_ref[...] += jnp.sum(dhc, axis=0, keepdims=True)
        dlg_ref[...] += jnp.sum(dlg_rows, axis=0, keepdims=True)
        dlb_ref[...] += jnp.sum(dlb_rows, axis=0, keepdims=True)
        _fill_shifted(hext, hsh, tm)
        _fill_shifted(dext, dsh, tm)
        for c in range(tm // rc):
            r0 = c * rc
            dh = jnp.zeros((rc, CONV_W), F32)
            dhc_c = dext[r0:r0 + rc, :]
            for k in range(CONV_K):
                dh = dh + w_ref[k:k + 1, :] * _shifted_rows(dext, dsh, r0 + 30 - k, rc)
                prod = dhc_c * _shifted_rows(hext, hsh, r0 + 2 + k, rc)
                dwacc[k] += jnp.sum(prod.reshape(rc // SUBLANES, SUBLANES, CONV_W), axis=0)
            av = a_ref[r0:r0 + rc, :CONV_W]
            sg = _sigmoid(a_ref[r0:r0 + rc, CONV_W:])
            dp_ref[r0:r0 + rc, 0:CONV_W] = (dh * sg).astype(BF16)
            dp_ref[r0:r0 + rc, CONV_W:] = (dh * av * sg * (1.0 - sg)).astype(BF16)

        @pl.when(i == ni - 1)
        def _():
            dw_ref[...] = jnp.sum(dwacc[...], axis=1)

    row = lambda w: pl.BlockSpec((tm, w), lambda i: (i, 0))
    prev = lambda w: pl.BlockSpec((CONV_HALO, w), lambda i: (jnp.maximum(i * hb - 1, 0), 0))
    nxt = lambda w: pl.BlockSpec((CONV_HALO, w), lambda i: (jnp.minimum((i + 1) * hb, ni * hb - 1), 0))
    vec = pl.BlockSpec((1, CONV_W), lambda i: (0, 0))
    return _call(
        body, rider, name="bwd_conv", grid=(ni,),
        in_specs=[ANY, row(1024), prev(1024), row(CONV_W), nxt(CONV_W), row(CONV_W), nxt(CONV_W),
                  pl.BlockSpec((CONV_HALO, CONV_W), lambda i: (0, 0)), vec, vec],
        out_specs=[pl.BlockSpec((tm, 1024), lambda i: (i, 0)),
                   pl.BlockSpec((CONV_HALO, CONV_W), lambda i: (0, 0)), vec, vec, vec],
        out_shape=[jax.ShapeDtypeStruct((T, IN_COLS), BF16),
                   jax.ShapeDtypeStruct((CONV_HALO, CONV_W), F32),
                   jax.ShapeDtypeStruct((1, CONV_W), F32),
                   jax.ShapeDtypeStruct((1, CONV_W), F32),
                   jax.ShapeDtypeStruct((1, CONV_W), F32)],
        scratch_shapes=[pltpu.VMEM((tm + CONV_HALO, CONV_W), F32), pltpu.VMEM((tm + CONV_HALO, CONV_W), F32),
                        pltpu.VMEM((SUBLANES - 1, tm + CONV_HALO - SUBLANES, CONV_W), F32),
                        pltpu.VMEM((SUBLANES - 1, tm + CONV_HALO - SUBLANES, CONV_W), F32),
                        pltpu.VMEM((CONV_HALO, SUBLANES, CONV_W), F32)],
        input_output_aliases={0: 0},
        compiler_params=_params(56, 1),
    )(dproj, a, a, dco, dco, hc, hc, cw, lg, lb)


def _bwd_in_proj(dproj, w_in, x, dh1, g1, rider=None):
    T = x.shape[0]
    tm = 1024

    def body(dp_ref, w_ref, x_ref, dh_ref, g_ref, gx_ref, dg_ref):
        i = pl.program_id(0)
        du = _dot_nt(dp_ref[...], w_ref[...])
        dx, dg_rows = _rms_bwd(du, x_ref[...], g_ref[...])
        gx_ref[...] = dh_ref[...] + dx
        dg = jnp.sum(dg_rows, axis=0, keepdims=True)

        @pl.when(i == 0)
        def _():
            dg_ref[...] = dg

        @pl.when(i > 0)
        def _():
            dg_ref[...] += dg

    row = lambda w: pl.BlockSpec((tm, w), lambda i: (i, 0))
    vec = pl.BlockSpec((1, D_MODEL), lambda i: (0, 0))
    return _call(
        body, rider, name="bwd_in_proj", grid=(T // tm,),
        in_specs=[row(IN_COLS), pl.BlockSpec((D_MODEL, IN_COLS), lambda i: (0, 0)),
                  row(D_MODEL), row(D_MODEL), vec],
        out_specs=[row(D_MODEL), vec],
        out_shape=[jax.ShapeDtypeStruct((T, D_MODEL), F32), jax.ShapeDtypeStruct((1, D_MODEL), F32)],
        compiler_params=_params(56, 1),
    )(dproj, w_in, x, dh1, g1)


def _wgrad(name, a_list, a_spec, b, b_spec, out_spec, out_shape, n_outer, T, tk, select=None, rider=None):
    def body(*refs):
        a_refs, b_ref, o_ref = refs[:len(a_list)], refs[len(a_list)], refs[len(a_list) + 1]
        kt = pl.program_id(1)

        @pl.when(kt == 0)
        def _():
            o_ref[...] = jnp.zeros(o_ref.shape, F32)

        bv = b_ref[...].reshape(b_ref.shape[-2:])
        if select is None:
            o_ref[...] += _dot_tn(a_refs[0][...].reshape(a_refs[0].shape[-2:]), bv).reshape(o_ref.shape)
        else:
            for n, a_ref in enumerate(a_refs):
                @pl.when(select(pl.program_id(0)) == n)
                def _():
                    o_ref[...] += _dot_tn(a_ref[...], bv).reshape(o_ref.shape)

    (res,), got = _call(
        body, rider, name=name, grid=(n_outer, T // tk),
        in_specs=[a_spec] * len(a_list) + [b_spec],
        out_specs=[out_spec], out_shape=[out_shape],
        compiler_params=_params(56, 2),
    )(*a_list, b)
    return (res, got) if rider is not None else res


def _mesh_pos():
    return lax.axis_index("x"), lax.axis_index("y"), lax.axis_index("c")


def _other_chips(x, y):
    return [((1 - x, y), 2 * (1 - x) + y), ((x, 1 - y), 2 * x + (1 - y)), ((1 - x, 1 - y), 2 * (1 - x) + (1 - y))]


def _exchange_rider(operands, out_shape, aliases, sem_shape, pairs):
    def start(ins, outs, sems):
        for send, _ in pairs(ins, outs, *sems):
            send.start()

    def finish(ins, outs, sems):
        for send, recv in pairs(ins, outs, *sems):
            send.wait_send()
            recv.wait_recv()

    sems = [pltpu.SemaphoreType.DMA(sem_shape), pltpu.SemaphoreType.DMA(sem_shape)]
    return _Rider(list(operands), list(out_shape), aliases, sems, start, finish)


def _remote(src, dst, send_sem, recv_sem, device):
    return pltpu.make_async_remote_copy(src_ref=src, dst_ref=dst, send_sem=send_sem, recv_sem=recv_sem,
                                        device_id=device, device_id_type=MESH)


def _fetch_rider(bufs):
    def pairs(ins, outs, send_sems, recv_sems):
        x, y, c = _mesh_pos()
        res = []
        for t, buf in enumerate(bufs):
            rows = pl.ds(c * (buf.shape[1] // 2), buf.shape[1] // 2)
            mine = outs[t].at[2 * x + y, rows]
            for k, (chip, s) in enumerate(_other_chips(x, y)):
                landed = outs[t].at[s, rows]
                res.append((_remote(mine, mine, send_sems.at[t, k], recv_sems.at[t, k], (*chip, c)),
                            _remote(landed, landed, send_sems.at[t, k], recv_sems.at[t, k], (*chip, c))))
        return res

    shapes = [jax.ShapeDtypeStruct(b.shape, b.dtype) for b in bufs]
    return _exchange_rider(bufs, shapes, {t: t for t in range(len(bufs))}, (len(bufs), 3), pairs)


def _forward_rider(bufs):
    def pairs(ins, outs, send_sems, recv_sems):
        x, y, c = _mesh_pos()
        res = []
        for t, buf in enumerate(bufs):
            half = buf.shape[1] // 2
            for k, (_, s) in enumerate(_other_chips(x, y)):
                landed = outs[t].at[s, pl.ds(c * half, half)]
                theirs = outs[t].at[s, pl.ds((1 - c) * half, half)]
                res.append((_remote(landed, landed, send_sems.at[t, k], recv_sems.at[t, k], (x, y, 1 - c)),
                            _remote(theirs, theirs, send_sems.at[t, k], recv_sems.at[t, k], (x, y, 1 - c))))
        return res

    shapes = [jax.ShapeDtypeStruct(b.shape, b.dtype) for b in bufs]
    return _exchange_rider(bufs, shapes, {t: t for t in range(len(bufs))}, (len(bufs), 3), pairs)


def _pair_exchange_rider(grads):
    def pairs(ins, outs, send_sems, recv_sems):
        x, y, c = _mesh_pos()
        res = []
        for t, g in enumerate(grads):
            half = g.shape[1] // 2
            cp = _remote(ins[t].at[:, pl.ds((1 - c) * half, half), :], outs[t], send_sems.at[t], recv_sems.at[t],
                         (x, y, 1 - c))
            res.append((cp, cp))
        return res

    shapes = [jax.ShapeDtypeStruct((N_CHIPS, g.shape[1] // 2, g.shape[2]), F32) for g in grads]
    return _exchange_rider(grads, shapes, {}, (len(grads),), pairs)


def _chip_exchange_rider(sums):
    def pairs(ins, outs, send_sems, recv_sems):
        x, y, c = _mesh_pos()
        res = []
        for t in range(len(sums)):
            for k, (chip, s) in enumerate(_other_chips(x, y)):
                cp = _remote(ins[t].at[s], outs[t].at[k], send_sems.at[t, k], recv_sems.at[t, k], (*chip, c))
                res.append((cp, cp))
        return res

    shapes = [jax.ShapeDtypeStruct((3,) + p.shape[1:], p.dtype) for p in sums]
    return _exchange_rider(sums, shapes, {}, (len(sums), 3), pairs)


def _pair_gather_rider(fulls):
    def pairs(ins, outs, send_sems, recv_sems):
        x, y, c = _mesh_pos()
        res = []
        for t, f in enumerate(fulls):
            half = f.shape[0] // 2
            mine = outs[t].at[pl.ds(c * half, half)]
            theirs = outs[t].at[pl.ds((1 - c) * half, half)]
            res.append((_remote(mine, mine, send_sems.at[t], recv_sems.at[t], (x, y, 1 - c)),
                        _remote(theirs, theirs, send_sems.at[t], recv_sems.at[t], (x, y, 1 - c))))
        return res

    shapes = [jax.ShapeDtypeStruct(f.shape, F32) for f in fulls]
    return _exchange_rider(fulls, shapes, {t: t for t in range(len(fulls))}, (len(fulls),), pairs)


def _alone(name, rider):
    return _call(lambda: None, rider, name=name)()[1]


def _all_reduce_small(pack, rider=None):
    rows = pack.shape[0]

    def body(p_ref, o_ref, buf, send_sems, recv_sems):
        x, y, c = _mesh_pos()
        me = 4 * x + 2 * y + c
        buf[0] = p_ref[...]
        copies = []
        for k in range(1, 8):
            peer = (x ^ (k >> 2), y ^ ((k >> 1) & 1), c ^ (k & 1))
            cp = pltpu.make_async_remote_copy(
                src_ref=p_ref, dst_ref=buf.at[k], send_sem=send_sems.at[k - 1], recv_sem=recv_sems.at[k - 1],
                device_id=peer, device_id_type=MESH)
            cp.start()
            copies.append(cp)
        for cp in copies:
            cp.wait()
        total = buf[me]
        for dev in range(1, 8):
            total = total + buf[me ^ dev]
        o_ref[...] = total

    return _call(
        body, rider, name="all_reduce_small",
        in_specs=[VMEM_FULL], out_specs=[VMEM_FULL],
        out_shape=[jax.ShapeDtypeStruct(pack.shape, F32)],
        scratch_shapes=[pltpu.VMEM((8, rows, LANES), F32),
                        pltpu.SemaphoreType.DMA((7,)), pltpu.SemaphoreType.DMA((7,))],
    )(pack)


def _row_block(rows):
    if rows <= 512:
        return rows
    for rb in (256, 352):
        if rows % rb == 0:
            return rb
    raise ValueError(f"no row block for {rows} rows")


def _place(name, w, pos, dtype):
    R, C = w.shape
    rb = _row_block(R)

    def body(pos_ref, w_ref, o_ref):
        del pos_ref
        o_ref[0] = w_ref[...].astype(dtype)

    return pl.pallas_call(
        body, name=name,
        grid_spec=pltpu.PrefetchScalarGridSpec(
            num_scalar_prefetch=1, grid=(R // rb,),
            in_specs=[pl.BlockSpec((rb, C), lambda r, p: (r, 0))],
            out_specs=pl.BlockSpec((1, rb, C), lambda r, p: (p[0], r, 0))),
        out_shape=(pltpu.HBM if N_CHIPS * R * C * jnp.dtype(dtype).itemsize >= PIN_BYTES
                   else jax.ShapeDtypeStruct)((N_CHIPS, R, C), dtype),
        compiler_params=_params(32, 1),
    )(pos, w)


def _pair_sum(name, g, got, pos):
    S, R, C = g.shape
    half = R // 2
    rb = _row_block(half)
    nh = half // rb

    def body(pos_ref, a_ref, b_ref, o_ref):
        del pos_ref
        o_ref[...] = (a_ref[...] + b_ref[...]).astype(BF16)

    spec = pl.BlockSpec((1, rb, C), lambda s, r, p: (s, r, 0))
    return pl.pallas_call(
        body, name=name,
        grid_spec=pltpu.PrefetchScalarGridSpec(
            num_scalar_prefetch=1, grid=(S, nh),
            in_specs=[pl.BlockSpec((1, rb, C), lambda s, r, p: (s, p[1] * nh + r, 0)), spec],
            out_specs=spec),
        out_shape=jax.ShapeDtypeStruct((S, half, C), BF16), compiler_params=_params(32, 2),
    )(pos, g, got)


def _chip_sum(name, pairs, got, pos):
    _, half, C = pairs.shape
    rb = _row_block(half)
    nh = half // rb

    def body(pos_ref, a_ref, g_ref, o_ref):
        del pos_ref
        o_ref[...] = ((a_ref[0].astype(F32) + g_ref[0].astype(F32)) + g_ref[1].astype(F32)) + g_ref[2].astype(F32)

    return pl.pallas_call(
        body, name=name,
        grid_spec=pltpu.PrefetchScalarGridSpec(
            num_scalar_prefetch=1, grid=(nh,),
            in_specs=[pl.BlockSpec((1, rb, C), lambda r, p: (p[0], r, 0)),
                      pl.BlockSpec((3, rb, C), lambda r, p: (0, r, 0))],
            out_specs=pl.BlockSpec((rb, C), lambda r, p: (p[1] * nh + r, 0))),
        out_shape=jax.ShapeDtypeStruct((2 * half, C), F32), compiler_params=_params(32, 1),
    )(pos, pairs, got)


def _adamw(name, w, g, m, v):
    R, C = w.shape
    rb = _row_block(R)
    c1 = 1.0 - ADAM_B1 ** ADAM_STEP
    c2 = 1.0 - ADAM_B2 ** ADAM_STEP

    def body(w_ref, g_ref, m_ref, v_ref, d_ref, nm_ref, nv_ref):
        gv = g_ref[...]
        nm = ADAM_B1 * m_ref[...] + (1.0 - ADAM_B1) * gv
        nv = ADAM_B2 * v_ref[...] + (1.0 - ADAM_B2) * (gv * gv)
        nm_ref[...] = nm
        nv_ref[...] = nv
        d_ref[...] = -ADAM_LR * ((nm / c1) / (jnp.sqrt(nv / c2) + ADAM_EPS) + ADAM_WD * w_ref[...])

    spec = pl.BlockSpec((rb, C), lambda r: (r, 0))
    sds = jax.ShapeDtypeStruct(w.shape, F32)
    return pl.pallas_call(
        body, name=name, grid=(R // rb,), in_specs=[spec] * 4, out_specs=[spec] * 3,
        out_shape=[sds, sds, sds], compiler_params=_params(40, 1),
    )(w, g, m, v)


def _rel_index():
    m = np.arange(2 * ATT_BLK)
    off = np.where(m < ATT_BLK, m, m - 2 * ATT_BLK)
    rel = np.stack([ATT_BLK * d - off for d in range(N_ATT_TILES)])
    return np.clip(rel, -MAX_REL, MAX_REL) + MAX_REL


def _local_step(x, tgt, g1, w_in, cw, cb, lg, lb, bias, w_out, g2, g3, w_up, fw, fb, w_down, g4, pos=None):
    T = x.shape[0]
    dist = pos is not None
    idx = _rel_index()

    (u, a, qkv), got = _fwd_in_proj(x, g1, w_in, _fetch_rider([w_out, w_down]) if dist else None)
    if dist:
        w_out, w_down = got
    (co, hc), got = _fwd_conv(a, cw, cb, lg, lb, _merge_riders(_forward_rider([w_out, w_down]),
                                                               _fetch_rider([w_up])) if dist else None)
    if dist:
        w_out, w_down, w_up = got
    (ao, lse), got = _fwd_attn(qkv, bias, _forward_rider([w_up]) if dist else None)
    if dist:
        (w_up,) = got
        w_out, w_down = w_out.reshape(D_MODEL, D_MODEL), w_down.reshape(D_FF, D_MODEL)
    mixed, h1, u2 = _fwd_out_proj(co, ao, w_out, x, g2, g3)
    w_cat = _pair_up_weights(w_up)
    hf, pre, act, loss, dy, df, dg4 = _fwd_ffn_loss(u2, w_cat, fw, fb, w_down, h1, tgt, g4)

    tk = min(2048, T)
    du2p, dhf, dfw_g, dfw_v = _bwd_ffn(df, hf, pre, w_cat, fw, w_down)
    gw_up = _wgrad(
        "wgrad_up", [u2], pl.BlockSpec((tk, D_MODEL), lambda s, k: (k, 0)),
        dhf, pl.BlockSpec((1, tk, FF_SHARD), lambda s, k: (s // 2, k, s % 2)),
        pl.BlockSpec((1, D_MODEL, FF_SHARD), lambda s, k: (s, 0, 0)),
        jax.ShapeDtypeStruct((N_CHIPS, D_MODEL, FF_SHARD), F32), N_CHIPS, T, tk)
    gw_down = _wgrad(
        "wgrad_down", [act], pl.BlockSpec((tk, FF_SHARD), lambda s, k: (k, s)),
        df, pl.BlockSpec((tk, D_MODEL), lambda s, k: (k, 0)),
        pl.BlockSpec((FF_SHARD, D_MODEL), lambda s, k: (s, 0)),
        jax.ShapeDtypeStruct((D_FF, D_MODEL), F32), 2, T, tk).reshape(N_CHIPS, D_FF // N_CHIPS, D_MODEL)
    (dh1, dmx, dco, dao, dg3, dg2), _ = _bwd_mid(du2p, dy, h1, mixed, g3, g2, w_out)
    gw_out = _wgrad(
        "wgrad_out", [co, ao], pl.BlockSpec((tk, CONV_W), lambda s, k: (k, 0)),
        dmx, pl.BlockSpec((tk, D_MODEL), lambda s, k: (k, 0)),
        pl.BlockSpec((CONV_W, D_MODEL), lambda s, k: (s, 0)),
        jax.ShapeDtypeStruct((D_MODEL, D_MODEL), F32), 2, T, tk,
        select=lambda s: s, rider=_pair_exchange_rider([gw_up, gw_down]) if dist else None)
    if dist:
        gw_out, got = gw_out
        p_up = _pair_sum("pair_sum_w_up", gw_up, got[0], pos)
        p_down = _pair_sum("pair_sum_w_down", gw_down, got[1], pos)
    gw_out = gw_out.reshape(N_CHIPS, D_MODEL // N_CHIPS, D_MODEL)
    (dproj, dsacc), got = _bwd_attn(
        qkv, ao, dao, lse, bias,
        _merge_riders(_chip_exchange_rider([p_up, p_down]), _pair_exchange_rider([gw_out])) if dist else None)
    if dist:
        gw_up = _chip_sum("chip_sum_w_up", p_up, got[0], pos)
        gw_down = _chip_sum("chip_sum_w_down", p_down, got[1], pos)
        p_out = _pair_sum("pair_sum_w_out", gw_out, got[2], pos)
    (dproj, dcw, dcb, dlg, dlb), got = _bwd_conv(
        dproj, a, dco, hc, cw, lg, lb,
        _merge_riders(_pair_gather_rider([gw_up, gw_down]), _chip_exchange_rider([p_out])) if dist else None)
    if dist:
        gw_up, gw_down = got[:2]
        gw_out = _chip_sum("chip_sum_w_out", p_out, got[2], pos)
    gw_in = _wgrad(
        "wgrad_in", [u], pl.BlockSpec((tk, D_MODEL), lambda s, k: (k, 0)),
        dproj, pl.BlockSpec((tk, IN_SHARD), lambda s, k: (k, s)),
        pl.BlockSpec((1, D_MODEL, IN_SHARD), lambda s, k: (s, 0, 0)),
        jax.ShapeDtypeStruct((N_CHIPS, D_MODEL, IN_SHARD), F32), N_CHIPS, T, tk)
    if dist:
        got = _alone("pair_exchange_w_in", _merge_riders(_pair_exchange_rider([gw_in]), _pair_gather_rider([gw_out])))
        p_in, gw_out = _pair_sum("pair_sum_w_in", gw_in, got[0], pos), got[1]
    (gx, dg1), _ = _bwd_in_proj(dproj, w_in, x, dh1, g1)
    (diag,), got = _diag_sums(dsacc, _chip_exchange_rider([p_in]) if dist else None)
    if dist:
        gw_in = _chip_sum("chip_sum_w_in", p_in, got[0], pos)

    diag = diag.reshape(N_ATT_TILES, N_HEADS, 2 * ATT_BLK)
    onehot = np.zeros((N_ATT_TILES, 2 * ATT_BLK, 2 * MAX_REL + 1), np.float32)
    for d in range(N_ATT_TILES):
        onehot[d, np.arange(2 * ATT_BLK), idx[d]] = 1.0
    drel = jnp.einsum("dhm,dmr->hr", diag, jnp.asarray(onehot), precision=lax.Precision.HIGHEST)

    small = dict(norm_mix_pre=dg1, conv_dw_w=dcw[:CONV_K], conv_dw_b=dcb, conv_ln_g=dlg, conv_ln_b=dlb,
                 rel_bias=drel, norm_mix_post=dg2, norm_ffn_pre=dg3,
                 ffn_dw_w=jnp.concatenate([dfw_g[0, :3], dfw_g[1, :3], dfw_v[0, :3], dfw_v[1, :3]], axis=1),
                 ffn_dw_b=jnp.concatenate([dfw_g[0, 3:4], dfw_g[1, 3:4], dfw_v[0, 3:4], dfw_v[1, 3:4]], axis=1),
                 norm_ffn_post=dg4)
    return loss, gx, small, dict(w_in=gw_in, w_out=gw_out, w_up=gw_up, w_down=gw_down)


SMALL_ORDER = ["norm_mix_pre", "conv_dw_b", "conv_ln_g", "conv_ln_b", "rel_bias", "norm_mix_post",
               "norm_ffn_pre", "ffn_dw_b", "norm_ffn_post", "conv_dw_w", "ffn_dw_w"]


def _pack(parts):
    rows = []
    for p in parts:
        width = -(-p.shape[1] // LANES) * LANES
        rows.append(jnp.pad(p, ((0, 0), (0, width - p.shape[1]))).reshape(-1, LANES))
    packed = jnp.concatenate(rows, axis=0)
    pad = -packed.shape[0] % 8
    return jnp.pad(packed, ((0, pad), (0, 0)))


def _unpack(packed, shapes):
    out, r = [], 0
    for shp in shapes:
        width = -(-shp[1] // LANES) * LANES
        n = shp[0] * width // LANES
        out.append(packed[r:r + n].reshape(shp[0], width)[:, :shp[1]])
        r += n
    return out


WEIGHTS = ["norm_mix_pre", "w_in", "conv_dw_w", "conv_dw_b", "conv_ln_g", "conv_ln_b", "rel_bias", "w_out",
           "norm_mix_post", "norm_ffn_pre", "w_up", "ffn_dw_w", "ffn_dw_b", "w_down", "norm_ffn_post"]
BIG = ["w_in", "w_out", "w_up", "w_down"]


def kernel(x, norm_mix_pre, w_in, conv_dw_w, conv_dw_b, conv_ln_g, conv_ln_b, rel_bias, w_out, norm_mix_post, norm_ffn_pre, w_up, ffn_dw_w, ffn_dw_b, w_down, norm_ffn_post, loss_target, m_norm_mix_pre, m_w_in, m_conv_dw_w, m_conv_dw_b, m_conv_ln_g, m_conv_ln_b, m_rel_bias, m_w_out, m_norm_mix_post, m_norm_ffn_pre, m_w_up, m_ffn_dw_w, m_ffn_dw_b, m_w_down, m_norm_ffn_post, v_norm_mix_pre, v_w_in, v_conv_dw_w, v_conv_dw_b, v_conv_ln_g, v_conv_ln_b, v_rel_bias, v_w_out, v_norm_mix_post, v_norm_ffn_pre, v_w_up, v_ffn_dw_w, v_ffn_dw_b, v_w_down, v_norm_ffn_post):
    args = locals()
    w = {n: args[n][0] for n in WEIGHTS}
    m = {n: args["m_" + n][0] for n in WEIGHTS}
    v = {n: args["v_" + n][0] for n in WEIGHTS}
    for d in (w, m, v):
        d["rel_bias"] = d["rel_bias"].reshape(N_HEADS, 2 * MAX_REL + 1)
        for n in ("norm_mix_pre", "conv_dw_b", "conv_ln_g", "conv_ln_b", "norm_mix_post", "norm_ffn_pre",
                  "ffn_dw_b", "norm_ffn_post"):
            d[n] = d[n].reshape(1, -1)
    shard = 2 * lax.axis_index("x") + lax.axis_index("y")

    cw_sh = jnp.pad(w["conv_dw_w"], ((0, CONV_HALO - CONV_K), (0, 0)))
    fw_sh = jnp.pad(w["ffn_dw_w"], ((0, FF_HALO - 3), (0, 0)))
    pos = jnp.stack([shard, lax.axis_index("c")]).astype(jnp.int32)
    bufs = {n: _place("place_" + n, w[n], pos, BF16) for n in BIG}
    first = [bufs["w_in"], _place("place_conv_dw_w", cw_sh, pos, F32), _place("place_ffn_dw_w", fw_sh, pos, F32)]
    (bias,), first = _bias_tiles(w["rel_bias"], _fetch_rider(first))
    w_in_f, cw_f, fw_f = _alone("all_gather_forward", _forward_rider(list(first)))
    cw_full = jnp.transpose(cw_f, (1, 0, 2)).reshape(CONV_HALO, CONV_W)

    loss, gx, small, big = _local_step(
        x[0], loss_target[0], w["norm_mix_pre"], _join_columns(w_in_f), cw_full, w["conv_dw_b"], w["conv_ln_g"],
        w["conv_ln_b"], bias, bufs["w_out"], w["norm_mix_post"],
        w["norm_ffn_pre"], bufs["w_up"], fw_f, w["ffn_dw_b"].reshape(N_CHIPS, 1, FF_SHARD),
        bufs["w_down"], w["norm_ffn_post"], pos)
    (gsum,), (big["w_in"],) = _all_reduce_small(_pack([small[n] for n in SMALL_ORDER] + [loss]),
                                                _pair_gather_rider([big["w_in"]]))

    grads, deltas, new_m, new_v = {}, {}, {}, {}
    for n in BIG:
        grads[n] = big[n]
        deltas[n], new_m[n], new_v[n] = _adamw("adamw_" + n, w[n], big[n], m[n], v[n])
    shapes = [small[n].shape for n in SMALL_ORDER]
    *reduced, total = _unpack(gsum, shapes + [loss.shape])
    gs = dict(zip(SMALL_ORDER, reduced))
    gs["conv_dw_w"] = lax.dynamic_slice_in_dim(gs["conv_dw_w"], shard * LANES, LANES, axis=1)
    gs["ffn_dw_w"] = lax.dynamic_slice_in_dim(gs["ffn_dw_w"], shard * FF_SHARD, FF_SHARD, axis=1)
    shapes = [gs[n].shape for n in SMALL_ORDER]
    d_p, m_p, v_p = _adamw("adamw_small", _pack([w[n] for n in SMALL_ORDER]), _pack([gs[n] for n in SMALL_ORDER]),
                           _pack([m[n] for n in SMALL_ORDER]), _pack([v[n] for n in SMALL_ORDER]))
    for dst, packed in ((deltas, d_p), (new_m, m_p), (new_v, v_p)):
        dst.update(zip(SMALL_ORDER, _unpack(packed, shapes)))
    grads.update(gs)

    outs = [total[0, 0], gx[None]]
    for group in (grads, deltas, new_m, new_v):
        outs += [group[n].reshape(args[n].shape) for n in WEIGHTS]
    return tuple(outs)
```

```python
import functools
import math
from typing import Callable, NamedTuple

import numpy as np
import jax
import jax.numpy as jnp
from jax import lax
from jax.experimental import pallas as pl
from jax.experimental.pallas import tpu as pltpu

F32 = jnp.float32
BF16 = jnp.bfloat16

D_MODEL = 1024
CONV_W = 512
ATTN_W = 512
N_HEADS = 8
HEAD_DIM = 64
CHUNK = 64
N_LEFT = 8
MAX_REL = 128
CONV_K = 31
CONV_HALO = 32
D_FF = 2816
FF_SHARD = 1408
IN_COLS = 2560
IN_SHARD = 640
EPS = 1e-6
NEG_INF = -1e30
ATT_BLK = 256
N_ATT_TILES = 3
LANES = 128
SUBLANES = 8
N_CHIPS = 4

ADAM_LR = 0.001
ADAM_B1 = 0.9
ADAM_B2 = 0.999
ADAM_EPS = 1e-08
ADAM_WD = 0.01
ADAM_STEP = 10

MESH = pl.DeviceIdType.MESH
ANY = pl.BlockSpec(memory_space=pl.ANY)
VMEM_FULL = pl.BlockSpec(memory_space=pltpu.VMEM)


def _params(vmem_mb, n_grid=0):
    sem = ("arbitrary",) * n_grid if n_grid else None
    return pltpu.CompilerParams(dimension_semantics=sem, vmem_limit_bytes=vmem_mb << 20)


class _Rider(NamedTuple):
    operands: list
    out_shape: list
    aliases: dict
    sems: list
    start: Callable
    finish: Callable


def _merge_riders(a, b):
    ia, oa, sa = len(a.operands), len(a.out_shape), len(a.sems)

    def start(ins, outs, sems):
        a.start(ins[:ia], outs[:oa], sems[:sa])
        b.start(ins[ia:], outs[oa:], sems[sa:])

    def finish(ins, outs, sems):
        a.finish(ins[:ia], outs[:oa], sems[:sa])
        b.finish(ins[ia:], outs[oa:], sems[sa:])

    aliases = {**a.aliases, **{k + ia: v + oa for k, v in b.aliases.items()}}
    return _Rider(a.operands + b.operands, a.out_shape + b.out_shape, aliases, a.sems + b.sems, start, finish)


PIN_BYTES = 1 << 20


def _big(a):
    return math.prod(a.shape) * jnp.dtype(a.dtype).itemsize >= PIN_BYTES


def _pin_args(args):
    return [pltpu.with_memory_space_constraint(a, pltpu.HBM) if _big(a) else a for a in args]


def _call(body, rider, *, grid=(), in_specs=(), out_specs=(), out_shape=(), scratch_shapes=(),
          input_output_aliases=None, **kwargs):
    in_specs, out_specs = list(in_specs), list(out_specs)
    pin_out = lambda shapes: [pltpu.HBM(s.shape, s.dtype) if _big(s) else s for s in shapes]
    out_shape = pin_out(out_shape)
    scratch, aliases = list(scratch_shapes), dict(input_output_aliases or {})
    if rider is None:
        plain = pl.pallas_call(body, grid=grid, in_specs=in_specs, out_specs=out_specs, out_shape=out_shape,
                               scratch_shapes=scratch, input_output_aliases=aliases, **kwargs)
        return lambda *args: (plain(*_pin_args(args)), [])
    n_in, n_out, n_scr = len(in_specs), len(out_specs), len(scratch)
    r_in, r_out = len(rider.operands), len(rider.out_shape)

    def carried(*refs):
        ins, r_ins, refs = refs[:n_in], refs[n_in:n_in + r_in], refs[n_in + r_in:]
        outs, r_outs, refs = refs[:n_out], refs[n_out:n_out + r_out], refs[n_out + r_out:]
        scr, r_sems = refs[:n_scr], refs[n_scr:]
        if not grid:
            rider.start(r_ins, r_outs, r_sems)
            body(*ins, *outs, *scr)
            rider.finish(r_ins, r_outs, r_sems)
            return
        at = [pl.program_id(d) for d in range(len(grid))]
        first = functools.reduce(jnp.logical_and, [p == 0 for p in at])
        last = functools.reduce(jnp.logical_and, [p == n - 1 for p, n in zip(at, grid)])

        @pl.when(first)
        def _():
            rider.start(r_ins, r_outs, r_sems)

        body(*ins, *outs, *scr)

        @pl.when(last)
        def _():
            rider.finish(r_ins, r_outs, r_sems)

    aliases.update({n_in + k: n_out + v for k, v in rider.aliases.items()})
    both = pl.pallas_call(carried, grid=grid, in_specs=in_specs + [ANY] * r_in, out_specs=out_specs + [ANY] * r_out,
                          out_shape=out_shape + pin_out(rider.out_shape), scratch_shapes=scratch + rider.sems,
                          input_output_aliases=aliases, **kwargs)

    def run(*args):
        res = both(*_pin_args(args), *rider.operands)
        return res[:n_out], res[n_out:]

    return run


def _sigmoid(v):
    return 1.0 / (1.0 + jnp.exp(-v))


def _dot(a, b):
    return jnp.dot(a, b, preferred_element_type=F32)


def _dot_nt(a, b):
    return lax.dot_general(a, b, (((1,), (1,)), ((), ())), preferred_element_type=F32)


def _dot_tn(a, b):
    return lax.dot_general(a, b, (((0,), (0,)), ((), ())), preferred_element_type=F32)


def _rms_fwd(v, g):
    r = lax.rsqrt(jnp.mean(v * v, axis=-1, keepdims=True) + EPS)
    return v * r * g, r


def _rms_bwd(dy, v, g):
    r = lax.rsqrt(jnp.mean(v * v, axis=-1, keepdims=True) + EPS)
    vh = v * r
    dvh = dy * g
    dv = r * (dvh - vh * jnp.mean(dvh * vh, axis=-1, keepdims=True))
    return dv, dy * vh


def _join_columns(w):
    S, R, C = w.shape
    rb = 256

    def body(w_ref, o_ref):
        for s in range(S):
            o_ref[:, s * C:(s + 1) * C] = w_ref[s]

    return pl.pallas_call(
        body, name="join_columns", grid=(R // rb,),
        in_specs=[pl.BlockSpec((S, rb, C), lambda r: (0, r, 0))],
        out_specs=pl.BlockSpec((rb, S * C), lambda r: (r, 0)),
        out_shape=jax.ShapeDtypeStruct((R, S * C), w.dtype),
        compiler_params=_params(32, 1),
    )(w)


GLU_COLS = 2 * CONV_W


def _fwd_in_proj(x, g1, w_in, rider=None):
    T = x.shape[0]
    tm = 1024

    def body(x_ref, g_ref, w_ref, u_ref, a_ref, qkv_ref):
        u, _ = _rms_fwd(x_ref[...], g_ref[...])
        u = u.astype(BF16)
        u_ref[...] = u
        a_ref[...] = _dot(u, w_ref[:, :GLU_COLS])
        qkv_ref[...] = _dot(u, w_ref[:, GLU_COLS:]).astype(BF16)

    return _call(
        body, rider, name="fwd_in_proj", grid=(T // tm,),
        in_specs=[pl.BlockSpec((tm, D_MODEL), lambda i: (i, 0)),
                  pl.BlockSpec((1, D_MODEL), lambda i: (0, 0)),
                  pl.BlockSpec((D_MODEL, IN_COLS), lambda i: (0, 0))],
        out_specs=[pl.BlockSpec((tm, D_MODEL), lambda i: (i, 0)),
                   pl.BlockSpec((tm, 1024), lambda i: (i, 0)),
                   pl.BlockSpec((tm, 1536), lambda i: (i, 0))],
        out_shape=[jax.ShapeDtypeStruct((T, D_MODEL), BF16),
                   jax.ShapeDtypeStruct((T, 1024), F32),
                   jax.ShapeDtypeStruct((T, 1536), BF16)],
        compiler_params=_params(56, 1),
    )(x, g1, w_in)


def _fill_shifted(ext, shifted, tm):
    n = tm + CONV_HALO - SUBLANES
    for j in range(1, SUBLANES):
        shifted[j - 1] = ext[j:j + n, :]


def _shifted_rows(ext, shifted, start, rows):
    j = start % SUBLANES
    if j == 0:
        return ext[start:start + rows, :]
    return shifted[j - 1, start - j:start - j + rows, :]


def _fwd_conv(a, cw, cb, lg, lb, rider=None):
    T = a.shape[0]
    tm = 512
    rc = 64

    def body(a_ref, w_ref, b_ref, lg_ref, lb_ref, co_ref, hc_ref, hext, hsh):
        i = pl.program_id(0)

        @pl.when(i == 0)
        def _():
            hext[0:CONV_HALO, :] = jnp.zeros((CONV_HALO, CONV_W), F32)

        @pl.when(i > 0)
        def _():
            hext[0:CONV_HALO, :] = hext[tm:tm + CONV_HALO, :]

        hext[CONV_HALO:CONV_HALO + tm, :] = a_ref[:, :CONV_W] * _sigmoid(a_ref[:, CONV_W:])
        _fill_shifted(hext, hsh, tm)
        for c in range(tm // rc):
            acc = jnp.zeros((rc, CONV_W), F32)
            for k in range(CONV_K):
                acc = acc + w_ref[k:k + 1, :] * _shifted_rows(hext, hsh, c * rc + 2 + k, rc)
            hc = acc + b_ref[...]
            hc_ref[c * rc:(c + 1) * rc, :] = hc
            mu = jnp.mean(hc, axis=-1, keepdims=True)
            xc = hc - mu
            var = jnp.mean(xc * xc, axis=-1, keepdims=True)
            z = xc * lax.rsqrt(var + EPS) * lg_ref[...] + lb_ref[...]
            co_ref[c * rc:(c + 1) * rc, :] = (z * _sigmoid(z)).astype(BF16)

    return _call(
        body, rider, name="fwd_conv", grid=(T // tm,),
        in_specs=[pl.BlockSpec((tm, 1024), lambda i: (i, 0)),
                  pl.BlockSpec((CONV_HALO, CONV_W), lambda i: (0, 0)),
                  pl.BlockSpec((1, CONV_W), lambda i: (0, 0)),
                  pl.BlockSpec((1, CONV_W), lambda i: (0, 0)),
                  pl.BlockSpec((1, CONV_W), lambda i: (0, 0))],
        out_specs=[pl.BlockSpec((tm, CONV_W), lambda i: (i, 0)),
                   pl.BlockSpec((tm, CONV_W), lambda i: (i, 0))],
        out_shape=[jax.ShapeDtypeStruct((T, CONV_W), BF16),
                   jax.ShapeDtypeStruct((T, CONV_W), F32)],
        scratch_shapes=[pltpu.VMEM((tm + CONV_HALO, CONV_W), F32),
                        pltpu.VMEM((SUBLANES - 1, tm + CONV_HALO - SUBLANES, CONV_W), F32)],
        compiler_params=_params(40, 1),
    )(a, cw, cb, lg, lb)


def _row_skew(v, sign):
    rows, width = v.shape
    row = lax.broadcasted_iota(jnp.int32, (rows, 1), 0)
    for b in range(int(math.log2(rows))):
        shift = (1 << b) if sign > 0 else width - (1 << b)
        v = jnp.where(((row >> b) & 1) == 1, pltpu.roll(v, shift, 1), v)
    return v


def _att_visible(d):
    rq = lax.broadcasted_iota(jnp.int32, (ATT_BLK, ATT_BLK), 0) // CHUNK
    ck = lax.broadcasted_iota(jnp.int32, (ATT_BLK, ATT_BLK), 1) // CHUNK
    slack = ATT_BLK
    above = jnp.where(d == 0, 0, slack)
    below = jnp.where(d == 2, 0, slack)
    return (ck <= rq + above) & (ck >= rq - below)


def _bias_tiles(rel, rider=None):
    vec = jnp.transpose(rel[:, _rel_index()], (1, 0, 2)).reshape(N_ATT_TILES * N_HEADS, 1, 2 * ATT_BLK)

    def body(v_ref, o_ref):
        visible = _att_visible(pl.program_id(0))
        for h in range(N_HEADS):
            full = _row_skew(jnp.broadcast_to(v_ref[h], (ATT_BLK, 2 * ATT_BLK)), 1)
            o_ref[h] = jnp.where(visible, full[:, :ATT_BLK], NEG_INF)

    return _call(
        body, rider, name="bias_tiles", grid=(N_ATT_TILES,),
        in_specs=[pl.BlockSpec((N_HEADS, 1, 2 * ATT_BLK), lambda d: (d, 0, 0))],
        out_specs=[pl.BlockSpec((N_HEADS, ATT_BLK, ATT_BLK), lambda d: (d, 0, 0))],
        out_shape=[jax.ShapeDtypeStruct((N_ATT_TILES * N_HEADS, ATT_BLK, ATT_BLK), F32)],
        compiler_params=_params(32, 1),
    )(vec)


def _diag_sums(ds, rider=None):
    def body(d_ref, o_ref):
        wide = jnp.concatenate([d_ref[0], jnp.zeros((ATT_BLK, ATT_BLK), F32)], axis=1)
        o_ref[0] = jnp.sum(_row_skew(wide, -1), axis=0, keepdims=True)

    return _call(
        body, rider, name="diag_sums", grid=(N_ATT_TILES * N_HEADS,),
        in_specs=[pl.BlockSpec((1, ATT_BLK, ATT_BLK), lambda n: (n, 0, 0))],
        out_specs=[pl.BlockSpec((1, 1, 2 * ATT_BLK), lambda n: (n, 0, 0))],
        out_shape=[jax.ShapeDtypeStruct((N_ATT_TILES * N_HEADS, 1, 2 * ATT_BLK), F32)],
        compiler_params=_params(16, 1),
    )(ds)


def _head_mask(h):
    lane = lax.broadcasted_iota(jnp.int32, (1, LANES), 1)
    return (lane // HEAD_DIM) == (h % 2)


def _fwd_attn(qkv, bias, rider=None):
    T = qkv.shape[0]
    nb = T // ATT_BLK
    scale = HEAD_DIM ** -0.5

    def body(q_ref, k0_ref, k1_ref, k2_ref, v0_ref, v1_ref, v2_ref, b_ref, o_ref, lse_ref):
        i = pl.program_id(0)

        @pl.when(i >= N_ATT_TILES - 1)
        def _():
            block(i, False, q_ref, k0_ref, k1_ref, k2_ref, v0_ref, v1_ref, v2_ref, b_ref, o_ref, lse_ref)

        @pl.when(i < N_ATT_TILES - 1)
        def _():
            block(i, True, q_ref, k0_ref, k1_ref, k2_ref, v0_ref, v1_ref, v2_ref, b_ref, o_ref, lse_ref)

    def block(i, hide_absent, q_ref, k0_ref, k1_ref, k2_ref, v0_ref, v1_ref, v2_ref, b_ref, o_ref, lse_ref):
        k_refs = (k0_ref, k1_ref, k2_ref)
        v_refs = (v0_ref, v1_ref, v2_ref)
        lane = lax.broadcasted_iota(jnp.int32, (1, LANES), 1)
        lse_tile = jnp.zeros((ATT_BLK, LANES), F32)
        for g in range(N_HEADS // 2):
            cols = slice(g * LANES, (g + 1) * LANES)
            qg = q_ref[:, cols] * scale
            both = jnp.concatenate([jnp.where(_head_mask(h), qg, jnp.zeros_like(qg)) for h in (2 * g, 2 * g + 1)],
                                   axis=0)
            raw = [_dot_nt(both, k_refs[d][:, cols]) for d in range(N_ATT_TILES)]
            probs, sums = [], []
            for h in (2 * g, 2 * g + 1):
                mine = slice((h % 2) * ATT_BLK, (h % 2 + 1) * ATT_BLK)
                s = []
                for d in range(N_ATT_TILES):
                    sd = raw[d][mine] + b_ref[d * N_HEADS + h]
                    if d > 0 and hide_absent:
                        sd = jnp.where(i >= d, sd, NEG_INF)
                    s.append(sd)
                m = jnp.maximum(jnp.maximum(jnp.max(s[0], axis=-1, keepdims=True),
                                            jnp.max(s[1], axis=-1, keepdims=True)),
                                jnp.max(s[2], axis=-1, keepdims=True))
                p = [jnp.exp(sd - m) for sd in s]
                l = (jnp.sum(p[0], axis=-1, keepdims=True) + jnp.sum(p[1], axis=-1, keepdims=True)
                     + jnp.sum(p[2], axis=-1, keepdims=True))
                probs.append(p)
                sums.append(l)
                lse_tile = jnp.where(lane == h, m + jnp.log(l), lse_tile)
            out = None
            for d in range(N_ATT_TILES):
                term = _dot(jnp.concatenate([probs[0][d], probs[1][d]], axis=0).astype(BF16), v_refs[d][:, cols])
                out = term if out is None else out + term
            og = jnp.where(_head_mask(2 * g), out[:ATT_BLK] / sums[0], out[ATT_BLK:] / sums[1])
            o_ref[:, cols] = og.astype(BF16)
        lse_ref[...] = lse_tile

    def kv_spec(d, col):
        return pl.BlockSpec((ATT_BLK, ATTN_W), lambda i: (jnp.maximum(i - d, 0), col))

    return _call(
        body, rider, name="fwd_attn", grid=(nb,),
        in_specs=[pl.BlockSpec((ATT_BLK, ATTN_W), lambda i: (i, 0)),
                  kv_spec(0, 1), kv_spec(1, 1), kv_spec(2, 1),
                  kv_spec(0, 2), kv_spec(1, 2), kv_spec(2, 2),
                  pl.BlockSpec((N_ATT_TILES * N_HEADS, ATT_BLK, ATT_BLK), lambda i: (0, 0, 0))],
        out_specs=[pl.BlockSpec((ATT_BLK, ATTN_W), lambda i: (i, 0)),
                   pl.BlockSpec((ATT_BLK, LANES), lambda i: (i, 0))],
        out_shape=[jax.ShapeDtypeStruct((T, ATTN_W), BF16),
                   jax.ShapeDtypeStruct((T, LANES), F32)],
        compiler_params=_params(40, 1),
    )(qkv, qkv, qkv, qkv, qkv, qkv, qkv, bias)


def _fwd_out_proj(co, ao, w_out, x, g2, g3):
    T = x.shape[0]
    tm = 1024

    def body(co_ref, ao_ref, w_ref, x_ref, g2_ref, g3_ref, mixed_ref, h1_ref, u2_ref):
        mixed = _dot(co_ref[...], w_ref[0:CONV_W, :]) + _dot(ao_ref[...], w_ref[CONV_W:, :])
        mixed_ref[...] = mixed.astype(BF16)
        y, _ = _rms_fwd(mixed, g2_ref[...])
        h1 = x_ref[...] + y
        h1_ref[...] = h1
        u2, _ = _rms_fwd(h1, g3_ref[...])
        u2_ref[...] = u2.astype(BF16)

    row = lambda w: pl.BlockSpec((tm, w), lambda i: (i, 0))
    vec = pl.BlockSpec((1, D_MODEL), lambda i: (0, 0))
    return _call(
        body, None, name="fwd_out_proj", grid=(T // tm,),
        in_specs=[row(CONV_W), row(ATTN_W), pl.BlockSpec((D_MODEL, D_MODEL), lambda i: (0, 0)),
                  row(D_MODEL), vec, vec],
        out_specs=[row(D_MODEL), row(D_MODEL), row(D_MODEL)],
        out_shape=[jax.ShapeDtypeStruct((T, D_MODEL), BF16),
                   jax.ShapeDtypeStruct((T, D_MODEL), F32),
                   jax.ShapeDtypeStruct((T, D_MODEL), BF16)],
        compiler_params=_params(56, 1),
    )(co, ao, w_out, x, g2, g3)[0]


GELU_C = math.sqrt(2.0 / math.pi)
GELU_A = 0.044715


def _gelu_and_grad(v):
    sq = v * v
    th = jnp.tanh(v * (GELU_C + (GELU_C * GELU_A) * sq))
    half = 0.5 + 0.5 * th
    gl = v * half
    dgl = half + (v * (half * (1.0 - th))) * (GELU_C + (3.0 * GELU_C * GELU_A) * sq)
    return gl, dgl


FF_TM = 256
FF_HALO = 16
FF_CHUNKS = [(lo, min(lo + 256, FF_SHARD)) for lo in range(0, FF_SHARD, 256)]


def _rows_before(prev, cur):
    ext = jnp.concatenate([prev, cur], axis=0)
    return pltpu.roll(ext, 1, 0)[SUBLANES:], pltpu.roll(ext, 2, 0)[SUBLANES:]


def _rows_after(cur, nxt):
    ext = jnp.concatenate([cur, nxt], axis=0)
    n = ext.shape[0]
    return pltpu.roll(ext, n - 1, 0)[:cur.shape[0]], pltpu.roll(ext, n - 2, 0)[:cur.shape[0]]


def _pair_up_weights(w_up):
    rb = 256

    def body(g_ref, v_ref, o_ref):
        for lo, hi in FF_CHUNKS:
            o_ref[0, :, 2 * lo:lo + hi] = g_ref[0, :, lo:hi]
            o_ref[0, :, lo + hi:2 * hi] = v_ref[0, :, lo:hi]

    return pl.pallas_call(
        body, name="pair_up_weights", grid=(2, D_MODEL // rb),
        in_specs=[pl.BlockSpec((1, rb, FF_SHARD), lambda s, r: (s, r, 0)),
                  pl.BlockSpec((1, rb, FF_SHARD), lambda s, r: (s + 2, r, 0))],
        out_specs=pl.BlockSpec((1, rb, 2 * FF_SHARD), lambda s, r: (s, r, 0)),
        out_shape=jax.ShapeDtypeStruct((2, D_MODEL, 2 * FF_SHARD), w_up.dtype),
        compiler_params=_params(32, 2),
    )(w_up, w_up)


def _fwd_ffn_loss(u2, w_cat, fw, fb, w_down, h1, tgt, g4):
    T = u2.shape[0]
    tm = FF_TM

    def body(u_ref, w_ref, fw_ref, fb_ref, wd_ref, h1_ref, t_ref, g_ref,
             hf_ref, pre_ref, act_ref, loss_ref, dy_ref, df_ref, dg_ref, carg, carv):
        i = pl.program_id(0)

        @pl.when(i == 0)
        def _():
            carg[...] = jnp.zeros(carg.shape, F32)
            carv[...] = jnp.zeros(carv.shape, F32)

        u = u_ref[...]
        f = None
        chunks = [(s, lo, hi) for s in range(2) for lo, hi in FF_CHUNKS]
        up = lambda s, lo, hi: _dot(u, w_ref[s, :, 2 * lo:2 * hi])
        ahead = up(*chunks[0])
        for c, (s, lo, hi) in enumerate(chunks):
            conv = []
            hs = (ahead[:, :hi - lo], ahead[:, hi - lo:])
            if c + 1 < len(chunks):
                ahead = up(*chunks[c + 1])
            at = slice(s * FF_SHARD + lo, s * FF_SHARD + hi)
            for n, car in enumerate((carg, carv)):
                h0 = hs[n]
                hf_ref[n, :, at] = h0.astype(BF16)
                h1v, h2v = _rows_before(car[:, at], h0)
                car[:, at] = h0[tm - SUBLANES:, :]
                conv.append(fw_ref[2 * n + s, 0:1, lo:hi] * h2v + fw_ref[2 * n + s, 1:2, lo:hi] * h1v
                            + fw_ref[2 * n + s, 2:3, lo:hi] * h0 + fb_ref[2 * n + s, :, lo:hi])
            pre_ref[0, :, at] = conv[0].astype(BF16)
            pre_ref[1, :, at] = conv[1].astype(BF16)
            gl, _ = _gelu_and_grad(conv[0])
            act = (gl * conv[1]).astype(BF16)
            act_ref[:, at] = act
            term = _dot(act, wd_ref[at, :])
            f = term if f is None else f + term

        r, _ = _rms_fwd(f, g_ref[...])
        e = (h1_ref[...] + r) - t_ref[...]
        dy = e * (1.0 / D_MODEL)
        dy_ref[...] = dy
        df, dg_rows = _rms_bwd(dy, f, g_ref[...])
        df_ref[...] = df.astype(BF16)
        part = 0.5 * jnp.sum(jnp.mean(e * e, axis=-1, keepdims=True), axis=0, keepdims=True)
        dg = jnp.sum(dg_rows, axis=0, keepdims=True)

        @pl.when(i == 0)
        def _():
            loss_ref[...] = part
            dg_ref[...] = dg

        @pl.when(i > 0)
        def _():
            loss_ref[...] += part
            dg_ref[...] += dg

    row = lambda w: pl.BlockSpec((tm, w), lambda i: (i, 0))
    vec = pl.BlockSpec((1, D_MODEL), lambda i: (0, 0))
    once = pl.Buffered(1)
    return _call(
        body, None, name="fwd_ffn_loss", grid=(T // tm,),
        in_specs=[row(D_MODEL),
                  pl.BlockSpec((2, D_MODEL, 2 * FF_SHARD), lambda i: (0, 0, 0), pipeline_mode=once),
                  pl.BlockSpec((N_CHIPS, FF_HALO, FF_SHARD), lambda i: (0, 0, 0)),
                  pl.BlockSpec((N_CHIPS, 1, FF_SHARD), lambda i: (0, 0, 0)),
                  pl.BlockSpec((D_FF, D_MODEL), lambda i: (0, 0), pipeline_mode=once),
                  row(D_MODEL), row(D_MODEL), vec],
        out_specs=[pl.BlockSpec((2, tm, D_FF), lambda i: (0, i, 0)),
                   pl.BlockSpec((2, tm, D_FF), lambda i: (0, i, 0)),
                   row(D_FF), pl.BlockSpec((1, 1), lambda i: (0, 0)), row(D_MODEL), row(D_MODEL), vec],
        out_shape=[jax.ShapeDtypeStruct((2, T, D_FF), BF16),
                   jax.ShapeDtypeStruct((2, T, D_FF), BF16),
                   jax.ShapeDtypeStruct((T, D_FF), BF16),
                   jax.ShapeDtypeStruct((1, 1), F32),
                   jax.ShapeDtypeStruct((T, D_MODEL), F32),
                   jax.ShapeDtypeStruct((T, D_MODEL), BF16),
                   jax.ShapeDtypeStruct((1, D_MODEL), F32)],
        scratch_shapes=[pltpu.VMEM((SUBLANES, D_FF), F32), pltpu.VMEM((SUBLANES, D_FF), F32)],
        compiler_params=_params(60, 1),
    )(u2, w_cat, fw, fb, w_down, h1, tgt, g4)[0]


def _bwd_ffn(df, hf, pre, w_cat, fw, w_down):
    T = df.shape[0]
    tm = FF_TM
    ni = T // tm

    def body(df_ref, hf_ref, pre_ref, wd_ref, w_ref, fw_ref,
             du_ref, dhf_ref, dwg_ref, dwv_ref, carg, carv):
        i = pl.program_id(0)

        @pl.when(i == 0)
        def _():
            dwg_ref[...] = jnp.zeros(dwg_ref.shape, F32)
            dwv_ref[...] = jnp.zeros(dwv_ref.shape, F32)
            carg[...] = jnp.zeros(carg.shape, F32)
            carv[...] = jnp.zeros(carv.shape, F32)

        df = df_ref[...]
        du = None
        chunks = [(s, lo, hi) for s in range(2) for lo, hi in FF_CHUNKS]
        cols = lambda s, lo, hi: slice(s * FF_SHARD + lo, s * FF_SHARD + hi)
        down = lambda s, lo, hi: _dot_nt(df, wd_ref[cols(s, lo, hi), :])
        ahead = down(*chunks[0])
        for c, (s, lo, hi) in enumerate(chunks):
            dact = ahead
            if c + 1 < len(chunks):
                ahead = down(*chunks[c + 1])
            at = cols(s, lo, hi)
            pre_g = pre_ref[0, :, at].astype(F32)
            pre_v = pre_ref[1, :, at].astype(F32)
            gl, dgl = _gelu_and_grad(pre_g)
            dpre = (dact * pre_v * dgl, dact * gl)
            dhs = []
            for n, (car, dw_ref) in enumerate(((carg, dwg_ref), (carv, dwv_ref))):
                dp = dpre[n]
                h0 = hf_ref[n, :, at].astype(F32)
                up1, up2 = _rows_after(dp, car[:, at])
                car[:, at] = dp[0:SUBLANES, :]
                for k, shifted in enumerate((up2, up1, dp)):
                    dw_ref[s, k:k + 1, lo:hi] += jnp.sum(shifted * h0, axis=0, keepdims=True)
                dw_ref[s, 3:4, lo:hi] += jnp.sum(dp, axis=0, keepdims=True)
                taps = 2 * n + s
                dh = (fw_ref[taps, 2:3, lo:hi] * dp + fw_ref[taps, 1:2, lo:hi] * up1
                      + fw_ref[taps, 0:1, lo:hi] * up2).astype(BF16)
                dhf_ref[n, :, at] = dh
                dhs.append(dh)
            term = _dot_nt(jnp.concatenate(dhs, axis=1), w_ref[s, :, 2 * lo:2 * hi])
            du = term if du is None else du + term
        du_ref[...] = du.astype(BF16)

    rev = lambda i: ni - 1 - i
    once = pl.Buffered(1)
    dwspec = pl.BlockSpec((2, FF_HALO, FF_SHARD), lambda i: (0, 0, 0))
    return _call(
        body, None, name="bwd_ffn", grid=(ni,),
        in_specs=[pl.BlockSpec((tm, D_MODEL), lambda i: (rev(i), 0)),
                  pl.BlockSpec((2, tm, D_FF), lambda i: (0, rev(i), 0)),
                  pl.BlockSpec((2, tm, D_FF), lambda i: (0, rev(i), 0)),
                  pl.BlockSpec((D_FF, D_MODEL), lambda i: (0, 0), pipeline_mode=once),
                  pl.BlockSpec((2, D_MODEL, 2 * FF_SHARD), lambda i: (0, 0, 0), pipeline_mode=once),
                  pl.BlockSpec((N_CHIPS, FF_HALO, FF_SHARD), lambda i: (0, 0, 0))],
        out_specs=[pl.BlockSpec((tm, D_MODEL), lambda i: (rev(i), 0)),
                   pl.BlockSpec((2, tm, D_FF), lambda i: (0, rev(i), 0)),
                   dwspec, dwspec],
        out_shape=[jax.ShapeDtypeStruct((T, D_MODEL), BF16),
                   jax.ShapeDtypeStruct((2, T, D_FF), BF16),
                   jax.ShapeDtypeStruct((2, FF_HALO, FF_SHARD), F32),
                   jax.ShapeDtypeStruct((2, FF_HALO, FF_SHARD), F32)],
        scratch_shapes=[pltpu.VMEM((SUBLANES, D_FF), F32), pltpu.VMEM((SUBLANES, D_FF), F32)],
        compiler_params=_params(60, 1),
    )(df, hf, pre, w_down, w_cat, fw)[0]


def _bwd_mid(du2p, dy, h1, mixed, g3, g2, w_out, rider=None):
    T = dy.shape[0]
    tm = 512

    def body(du_ref, dy_ref, h1_ref, mx_ref, g3_ref, g2_ref, w_ref,
             dh1_ref, dmx_ref, dco_ref, dao_ref, dg3_ref, dg2_ref):
        i = pl.program_id(0)
        dres, dg3_rows = _rms_bwd(du_ref[...].astype(F32), h1_ref[...], g3_ref[...])
        dh1 = dy_ref[...] + dres
        dh1_ref[...] = dh1
        dmx, dg2_rows = _rms_bwd(dh1, mx_ref[...].astype(F32), g2_ref[...])
        dmx = dmx.astype(BF16)
        dmx_ref[...] = dmx
        dcat = _dot_nt(dmx, w_ref[...])
        dco_ref[...] = dcat[:, :CONV_W]
        dao_ref[...] = dcat[:, CONV_W:].astype(BF16)
        dg3 = jnp.sum(dg3_rows, axis=0, keepdims=True)
        dg2 = jnp.sum(dg2_rows, axis=0, keepdims=True)

        @pl.when(i == 0)
        def _():
            dg3_ref[...] = dg3
            dg2_ref[...] = dg2

        @pl.when(i > 0)
        def _():
            dg3_ref[...] += dg3
            dg2_ref[...] += dg2

    row = lambda w: pl.BlockSpec((tm, w), lambda i: (i, 0))
    vec = pl.BlockSpec((1, D_MODEL), lambda i: (0, 0))
    return _call(
        body, rider, name="bwd_mid", grid=(T // tm,),
        in_specs=[row(D_MODEL), row(D_MODEL), row(D_MODEL),
                  row(D_MODEL), vec, vec, pl.BlockSpec((D_MODEL, D_MODEL), lambda i: (0, 0))],
        out_specs=[row(D_MODEL), row(D_MODEL), row(CONV_W), row(ATTN_W), vec, vec],
        out_shape=[jax.ShapeDtypeStruct((T, D_MODEL), F32),
                   jax.ShapeDtypeStruct((T, D_MODEL), BF16),
                   jax.ShapeDtypeStruct((T, CONV_W), F32),
                   jax.ShapeDtypeStruct((T, ATTN_W), BF16),
                   jax.ShapeDtypeStruct((1, D_MODEL), F32),
                   jax.ShapeDtypeStruct((1, D_MODEL), F32)],
        compiler_params=_params(48, 1),
    )(du2p, dy, h1, mixed, g3, g2, w_out)


def _bwd_attn(qkv, ao, dao, lse, bias, rider=None):
    T = qkv.shape[0]
    nb = T // ATT_BLK
    scale = HEAD_DIM ** -0.5

    def body(k_ref, v_ref, q0, q1, q2, do0, do1, do2, o0, o1, o2, l0, l1, l2, b_ref,
             dp_ref, ds_ref, acc1, acc2):
        j = pl.program_id(0)
        q_refs, do_refs, o_refs, l_refs = (q0, q1, q2), (do0, do1, do2), (o0, o1, o2), (l0, l1, l2)

        @pl.when(j == 0)
        def _():
            ds_ref[...] = jnp.zeros(ds_ref.shape, F32)
            acc1[...] = jnp.zeros(acc1.shape, F32)
            acc2[...] = jnp.zeros(acc2.shape, F32)

        dq_new = [[], [], []]
        dk_cols, dv_cols = [], []
        for g in range(N_HEADS // 2):
            cols = slice(g * LANES, (g + 1) * LANES)
            kg = k_ref[:, cols]
            vg = v_ref[:, cols]
            dkt = jnp.zeros((LANES, ATT_BLK), F32)
            dvt = jnp.zeros((LANES, ATT_BLK), F32)
            dqg = [jnp.zeros((ATT_BLK, LANES), F32) for _ in range(N_ATT_TILES)]
            row_head = lax.broadcasted_iota(jnp.int32, (LANES, 1), 0) // HEAD_DIM
            for d in range(N_ATT_TILES):
                qg = q_refs[d][:, cols] * scale
                dog = do_refs[d][:, cols]
                if d > 0:
                    dog = jnp.where(j + d < nb, dog, jnp.zeros_like(dog))
                prod = dog.astype(F32) * o_refs[d][:, cols].astype(F32)
                qgt = qg.astype(F32).T
                dogt = dog.astype(F32).T
                heads = (2 * g, 2 * g + 1)
                s_both = _dot_nt(jnp.concatenate([jnp.where(_head_mask(h), qg, jnp.zeros_like(qg)) for h in heads],
                                                 axis=0), kg)
                dp_both = _dot_nt(jnp.concatenate([jnp.where(_head_mask(h), dog, jnp.zeros_like(dog)) for h in heads],
                                                  axis=0), vg)
                for h in heads:
                    hm = _head_mask(h)
                    kh = jnp.where(hm, kg, jnp.zeros_like(kg))
                    mine = row_head == (h % 2)
                    rows = slice((h % 2) * ATT_BLK, (h % 2 + 1) * ATT_BLK)
                    qht = jnp.where(mine, qgt, 0.0).astype(BF16)
                    doht = jnp.where(mine, dogt, 0.0).astype(BF16)
                    delta = jnp.sum(jnp.where(hm, prod, 0.0), axis=-1, keepdims=True)
                    s = s_both[rows] + b_ref[d * N_HEADS + h]
                    p = jnp.exp(s - l_refs[d][:, h:h + 1])
                    dvt = dvt + _dot(doht, p.astype(BF16))
                    dpm = dp_both[rows]
                    dsc = p * (dpm - delta)
                    ds_ref[d * N_HEADS + h] += dsc
                    dsb = dsc.astype(BF16)
                    dqg[d] = dqg[d] + _dot(dsb, kh)
                    dkt = dkt + _dot(qht, dsb)
            for d in range(N_ATT_TILES):
                dq_new[d].append(dqg[d])
            dk_cols.append(dkt.T)
            dv_cols.append(dvt.T)
        x0, x1, x2 = (jnp.concatenate(c, axis=1) * scale for c in dq_new)
        dp_ref[:, 0:1024] = jnp.zeros((ATT_BLK, 1024), BF16)
        dp_ref[:, 1024:1536] = (acc1[...] + x0).astype(BF16)
        dp_ref[:, 1536:2048] = jnp.concatenate(dk_cols, axis=1).astype(BF16)
        dp_ref[:, 2048:2560] = jnp.concatenate(dv_cols, axis=1).astype(BF16)
        acc1[...] = acc2[...] + x1
        acc2[...] = x2

    def fwd_spec(d, width, col):
        return pl.BlockSpec((ATT_BLK, width), lambda j: (jnp.minimum(j + d, nb - 1), col))

    return _call(
        body, rider, name="bwd_attn", grid=(nb,),
        in_specs=[pl.BlockSpec((ATT_BLK, ATTN_W), lambda j: (j, 1)),
                  pl.BlockSpec((ATT_BLK, ATTN_W), lambda j: (j, 2)),
                  fwd_spec(0, ATTN_W, 0), fwd_spec(1, ATTN_W, 0), fwd_spec(2, ATTN_W, 0),
                  fwd_spec(0, ATTN_W, 0), fwd_spec(1, ATTN_W, 0), fwd_spec(2, ATTN_W, 0),
                  fwd_spec(0, ATTN_W, 0), fwd_spec(1, ATTN_W, 0), fwd_spec(2, ATTN_W, 0),
                  fwd_spec(0, LANES, 0), fwd_spec(1, LANES, 0), fwd_spec(2, LANES, 0),
                  pl.BlockSpec((N_ATT_TILES * N_HEADS, ATT_BLK, ATT_BLK), lambda j: (0, 0, 0))],
        out_specs=[pl.BlockSpec((ATT_BLK, IN_COLS), lambda j: (j, 0)),
                   pl.BlockSpec((N_ATT_TILES * N_HEADS, ATT_BLK, ATT_BLK), lambda j: (0, 0, 0))],
        out_shape=[jax.ShapeDtypeStruct((T, IN_COLS), BF16),
                   jax.ShapeDtypeStruct((N_ATT_TILES * N_HEADS, ATT_BLK, ATT_BLK), F32)],
        scratch_shapes=[pltpu.VMEM((ATT_BLK, ATTN_W), F32), pltpu.VMEM((ATT_BLK, ATTN_W), F32)],
        compiler_params=_params(56, 1),
    )(qkv, qkv, qkv, qkv, qkv, dao, dao, dao, ao, ao, ao, lse, lse, lse, bias)


def _bwd_conv(dproj, a, dco, hc, cw, lg, lb, rider=None):
    T = a.shape[0]
    tm = 512
    rc = 32
    ni = T // tm
    hb = tm // CONV_HALO

    def body(dp_in, a_ref, ap_ref, dco_ref, dcon_ref, hc_ref, hcn_ref, w_ref, lg_ref, lb_ref,
             dp_ref, dw_ref, db_ref, dlg_ref, dlb_ref, hext, dext, hsh, dsh, dwacc):
        del dp_in
        i = pl.program_id(0)

        def ln_bwd(dco_v, hc_v):
            mu = jnp.mean(hc_v, axis=-1, keepdims=True)
            xc = hc_v - mu
            rstd = lax.rsqrt(jnp.mean(xc * xc, axis=-1, keepdims=True) + EPS)
            xh = xc * rstd
            z = xh * lg_ref[...] + lb_ref[...]
            sg = _sigmoid(z)
            dz = dco_v * (sg * (1.0 + z * (1.0 - sg)))
            dxh = dz * lg_ref[...]
            dhc = rstd * (dxh - jnp.mean(dxh, axis=-1, keepdims=True)
                          - xh * jnp.mean(dxh * xh, axis=-1, keepdims=True))
            return dhc, dz * xh, dz

        hext[0:CONV_HALO, :] = jnp.where(i > 0, ap_ref[:, :CONV_W] * _sigmoid(ap_ref[:, CONV_W:]), 0.0)
        hext[CONV_HALO:CONV_HALO + tm, :] = a_ref[:, :CONV_W] * _sigmoid(a_ref[:, CONV_W:])
        dhc, dlg_rows, dlb_rows = ln_bwd(dco_ref[...], hc_ref[...])
        dext[0:tm, :] = dhc
        dhc_next, _, _ = ln_bwd(dcon_ref[...], hcn_ref[...])
        dext[tm:tm + CONV_HALO, :] = jnp.where(i < ni - 1, dhc_next, 0.0)

        @pl.when(i == 0)
        def _():
            dw_ref[...] = jnp.zeros(dw_ref.shape, F32)
            db_ref[...] = jnp.zeros(db_ref.shape, F32)
            dlg_ref[...] = jnp.zeros(dlg_ref.shape, F32)
            dlb_ref[...] = jnp.zeros(dlb_ref.shape, F32)

            dwacc[...] = jnp.zeros(dwacc.shape, F32)

        db_ref[...] += jnp.sum(dhc, axis=0, keepdims=True)
        dlg_ref[...] += jnp.sum(dlg_rows, axis=0, keepdims=True)
        dlb_ref[...] += jnp.sum(dlb_rows, axis=0, keepdims=True)
        _fill_shifted(hext, hsh, tm)
        _fill_shifted(dext, dsh, tm)
        for c in range(tm // rc):
            r0 = c * rc
            dh = jnp.zeros((rc, CONV_W), F32)
            dhc_c = dext[r0:r0 + rc, :]
            for k in range(CONV_K):
                dh = dh + w_ref[k:k + 1, :] * _shifted_rows(dext, dsh, r0 + 30 - k, rc)
                prod = dhc_c * _shifted_rows(hext, hsh, r0 + 2 + k, rc)
                dwacc[k] += jnp.sum(prod.reshape(rc // SUBLANES, SUBLANES, CONV_W), axis=0)
            av = a_ref[r0:r0 + rc, :CONV_W]
            sg = _sigmoid(a_ref[r0:r0 + rc, CONV_W:])
            dp_ref[r0:r0 + rc, 0:CONV_W] = (dh * sg).astype(BF16)
            dp_ref[r0:r0 + rc, CONV_W:] = (dh * av * sg * (1.0 - sg)).astype(BF16)

        @pl.when(i == ni - 1)
        def _():
            dw_ref[...] = jnp.sum(dwacc[...], axis=1)

    row = lambda w: pl.BlockSpec((tm, w), lambda i: (i, 0))
    prev = lambda w: pl.BlockSpec((CONV_HALO, w), lambda i: (jnp.maximum(i * hb - 1, 0), 0))
    nxt = lambda w: pl.BlockSpec((CONV_HALO, w), lambda i: (jnp.minimum((i + 1) * hb, ni * hb - 1), 0))
    vec = pl.BlockSpec((1, CONV_W), lambda i: (0, 0))
    return _call(
        body, rider, name="bwd_conv", grid=(ni,),
        in_specs=[ANY, row(1024), prev(1024), row(CONV_W), nxt(CONV_W), row(CONV_W), nxt(CONV_W),
                  pl.BlockSpec((CONV_HALO, CONV_W), lambda i: (0, 0)), vec, vec],
        out_specs=[pl.BlockSpec((tm, 1024), lambda i: (i, 0)),
                   pl.BlockSpec((CONV_HALO, CONV_W), lambda i: (0, 0)), vec, vec, vec],
        out_shape=[jax.ShapeDtypeStruct((T, IN_COLS), BF16),
                   jax.ShapeDtypeStruct((CONV_HALO, CONV_W), F32),
                   jax.ShapeDtypeStruct((1, CONV_W), F32),
                   jax.ShapeDtypeStruct((1, CONV_W), F32),
                   jax.ShapeDtypeStruct((1, CONV_W), F32)],
        scratch_shapes=[pltpu.VMEM((tm + CONV_HALO, CONV_W), F32), pltpu.VMEM((tm + CONV_HALO, CONV_W), F32),
                        pltpu.VMEM((SUBLANES - 1, tm + CONV_HALO - SUBLANES, CONV_W), F32),
                        pltpu.VMEM((SUBLANES - 1, tm + CONV_HALO - SUBLANES, CONV_W), F32),
                        pltpu.VMEM((CONV_HALO, SUBLANES, CONV_W), F32)],
        input_output_aliases={0: 0},
        compiler_params=_params(56, 1),
    )(dproj, a, a, dco, dco, hc, hc, cw, lg, lb)


def _bwd_in_proj(dproj, w_in, x, dh1, g1, rider=None):
    T = x.shape[0]
    tm = 1024

    def body(dp_ref, w_ref, x_ref, dh_ref, g_ref, gx_ref, dg_ref):
        i = pl.program_id(0)
        du = _dot_nt(dp_ref[...], w_ref[...])
        dx, dg_rows = _rms_bwd(du, x_ref[...], g_ref[...])
        gx_ref[...] = dh_ref[...] + dx
        dg = jnp.sum(dg_rows, axis=0, keepdims=True)

        @pl.when(i == 0)
        def _():
            dg_ref[...] = dg

        @pl.when(i > 0)
        def _():
            dg_ref[...] += dg

    row = lambda w: pl.BlockSpec((tm, w), lambda i: (i, 0))
    vec = pl.BlockSpec((1, D_MODEL), lambda i: (0, 0))
    return _call(
        body, rider, name="bwd_in_proj", grid=(T // tm,),
        in_specs=[row(IN_COLS), pl.BlockSpec((D_MODEL, IN_COLS), lambda i: (0, 0)),
                  row(D_MODEL), row(D_MODEL), vec],
        out_specs=[row(D_MODEL), vec],
        out_shape=[jax.ShapeDtypeStruct((T, D_MODEL), F32), jax.ShapeDtypeStruct((1, D_MODEL), F32)],
        compiler_params=_params(56, 1),
    )(dproj, w_in, x, dh1, g1)


def _wgrad(name, a_list, a_spec, b, b_spec, out_spec, out_shape, n_outer, T, tk, select=None, rider=None):
    def body(*refs):
        a_refs, b_ref, o_ref = refs[:len(a_list)], refs[len(a_list)], refs[len(a_list) + 1]
        kt = pl.program_id(1)

        @pl.when(kt == 0)
        def _():
            o_ref[...] = jnp.zeros(o_ref.shape, F32)

        bv = b_ref[...].reshape(b_ref.shape[-2:])
        if select is None:
            o_ref[...] += _dot_tn(a_refs[0][...].reshape(a_refs[0].shape[-2:]), bv).reshape(o_ref.shape)
        else:
            for n, a_ref in enumerate(a_refs):
                @pl.when(select(pl.program_id(0)) == n)
                def _():
                    o_ref[...] += _dot_tn(a_ref[...], bv).reshape(o_ref.shape)

    (res,), got = _call(
        body, rider, name=name, grid=(n_outer, T // tk),
        in_specs=[a_spec] * len(a_list) + [b_spec],
        out_specs=[out_spec], out_shape=[out_shape],
        compiler_params=_params(56, 2),
    )(*a_list, b)
    return (res, got) if rider is not None else res


def _mesh_pos():
    return lax.axis_index("x"), lax.axis_index("y"), lax.axis_index("c")


def _other_chips(x, y):
    return [((1 - x, y), 2 * (1 - x) + y), ((x, 1 - y), 2 * x + (1 - y)), ((1 - x, 1 - y), 2 * (1 - x) + (1 - y))]


def _exchange_rider(operands, out_shape, aliases, sem_shape, pairs):
    def start(ins, outs, sems):
        for send, _ in pairs(ins, outs, *sems):
            send.start()

    def finish(ins, outs, sems):
        for send, recv in pairs(ins, outs, *sems):
            send.wait_send()
            recv.wait_recv()

    sems = [pltpu.SemaphoreType.DMA(sem_shape), pltpu.SemaphoreType.DMA(sem_shape)]
    return _Rider(list(operands), list(out_shape), aliases, sems, start, finish)


def _remote(src, dst, send_sem, recv_sem, device):
    return pltpu.make_async_remote_copy(src_ref=src, dst_ref=dst, send_sem=send_sem, recv_sem=recv_sem,
                                        device_id=device, device_id_type=MESH)


def _fetch_rider(bufs):
    def pairs(ins, outs, send_sems, recv_sems):
        x, y, c = _mesh_pos()
        res = []
        for t, buf in enumerate(bufs):
            rows = pl.ds(c * (buf.shape[1] // 2), buf.shape[1] // 2)
            mine = outs[t].at[2 * x + y, rows]
            for k, (chip, s) in enumerate(_other_chips(x, y)):
                landed = outs[t].at[s, rows]
                res.append((_remote(mine, mine, send_sems.at[t, k], recv_sems.at[t, k], (*chip, c)),
                            _remote(landed, landed, send_sems.at[t, k], recv_sems.at[t, k], (*chip, c))))
        return res

    shapes = [jax.ShapeDtypeStruct(b.shape, b.dtype) for b in bufs]
    return _exchange_rider(bufs, shapes, {t: t for t in range(len(bufs))}, (len(bufs), 3), pairs)


def _forward_rider(bufs):
    def pairs(ins, outs, send_sems, recv_sems):
        x, y, c = _mesh_pos()
        res = []
        for t, buf in enumerate(bufs):
            half = buf.shape[1] // 2
            for k, (_, s) in enumerate(_other_chips(x, y)):
                landed = outs[t].at[s, pl.ds(c * half, half)]
                theirs = outs[t].at[s, pl.ds((1 - c) * half, half)]
                res.append((_remote(landed, landed, send_sems.at[t, k], recv_sems.at[t, k], (x, y, 1 - c)),
                            _remote(theirs, theirs, send_sems.at[t, k], recv_sems.at[t, k], (x, y, 1 - c))))
        return res

    shapes = [jax.ShapeDtypeStruct(b.shape, b.dtype) for b in bufs]
    return _exchange_rider(bufs, shapes, {t: t for t in range(len(bufs))}, (len(bufs), 3), pairs)


def _pair_exchange_rider(grads):
    def pairs(ins, outs, send_sems, recv_sems):
        x, y, c = _mesh_pos()
        res = []
        for t, g in enumerate(grads):
            half = g.shape[1] // 2
            cp = _remote(ins[t].at[:, pl.ds((1 - c) * half, half), :], outs[t], send_sems.at[t], recv_sems.at[t],
                         (x, y, 1 - c))
            res.append((cp, cp))
        return res

    shapes = [jax.ShapeDtypeStruct((N_CHIPS, g.shape[1] // 2, g.shape[2]), F32) for g in grads]
    return _exchange_rider(grads, shapes, {}, (len(grads),), pairs)


def _chip_exchange_rider(sums):
    def pairs(ins, outs, send_sems, recv_sems):
        x, y, c = _mesh_pos()
        res = []
        for t in range(len(sums)):
            for k, (chip, s) in enumerate(_other_chips(x, y)):
                cp = _remote(ins[t].at[s], outs[t].at[k], send_sems.at[t, k], recv_sems.at[t, k], (*chip, c))
                res.append((cp, cp))
        return res

    shapes = [jax.ShapeDtypeStruct((3,) + p.shape[1:], p.dtype) for p in sums]
    return _exchange_rider(sums, shapes, {}, (len(sums), 3), pairs)


def _pair_gather_rider(fulls):
    def pairs(ins, outs, send_sems, recv_sems):
        x, y, c = _mesh_pos()
        res = []
        for t, f in enumerate(fulls):
            half = f.shape[0] // 2
            mine = outs[t].at[pl.ds(c * half, half)]
            theirs = outs[t].at[pl.ds((1 - c) * half, half)]
            res.append((_remote(mine, mine, send_sems.at[t], recv_sems.at[t], (x, y, 1 - c)),
                        _remote(theirs, theirs, send_sems.at[t], recv_sems.at[t], (x, y, 1 - c))))
        return res

    shapes = [jax.ShapeDtypeStruct(f.shape, F32) for f in fulls]
    return _exchange_rider(fulls, shapes, {t: t for t in range(len(fulls))}, (len(fulls),), pairs)


def _alone(name, rider):
    return _call(lambda: None, rider, name=name)()[1]


def _all_reduce_small(pack, rider=None):
    rows = pack.shape[0]

    def body(p_ref, o_ref, buf, send_sems, recv_sems):
        x, y, c = _mesh_pos()
        me = 4 * x + 2 * y + c
        buf[0] = p_ref[...]
        copies = []
        for k in range(1, 8):
            peer = (x ^ (k >> 2), y ^ ((k >> 1) & 1), c ^ (k & 1))
            cp = pltpu.make_async_remote_copy(
                src_ref=p_ref, dst_ref=buf.at[k], send_sem=send_sems.at[k - 1], recv_sem=recv_sems.at[k - 1],
                device_id=peer, device_id_type=MESH)
            cp.start()
            copies.append(cp)
        for cp in copies:
            cp.wait()
        total = buf[me]
        for dev in range(1, 8):
            total = total + buf[me ^ dev]
        o_ref[...] = total

    return _call(
        body, rider, name="all_reduce_small",
        in_specs=[VMEM_FULL], out_specs=[VMEM_FULL],
        out_shape=[jax.ShapeDtypeStruct(pack.shape, F32)],
        scratch_shapes=[pltpu.VMEM((8, rows, LANES), F32),
                        pltpu.SemaphoreType.DMA((7,)), pltpu.SemaphoreType.DMA((7,))],
    )(pack)


def _row_block(rows):
    if rows <= 512:
        return rows
    for rb in (256, 352):
        if rows % rb == 0:
            return rb
    raise ValueError(f"no row block for {rows} rows")


def _place(name, w, pos, dtype):
    R, C = w.shape
    rb = _row_block(R)

    def body(pos_ref, w_ref, o_ref):
        del pos_ref
        o_ref[0] = w_ref[...].astype(dtype)

    return pl.pallas_call(
        body, name=name,
        grid_spec=pltpu.PrefetchScalarGridSpec(
            num_scalar_prefetch=1, grid=(R // rb,),
            in_specs=[pl.BlockSpec((rb, C), lambda r, p: (r, 0))],
            out_specs=pl.BlockSpec((1, rb, C), lambda r, p: (p[0], r, 0))),
        out_shape=(pltpu.HBM if N_CHIPS * R * C * jnp.dtype(dtype).itemsize >= PIN_BYTES
                   else jax.ShapeDtypeStruct)((N_CHIPS, R, C), dtype),
        compiler_params=_params(32, 1),
    )(pos, w)


def _pair_sum(name, g, got, pos):
    S, R, C = g.shape
    half = R // 2
    rb = _row_block(half)
    nh = half // rb

    def body(pos_ref, a_ref, b_ref, o_ref):
        del pos_ref
        o_ref[...] = (a_ref[...] + b_ref[...]).astype(BF16)

    spec = pl.BlockSpec((1, rb, C), lambda s, r, p: (s, r, 0))
    return pl.pallas_call(
        body, name=name,
        grid_spec=pltpu.PrefetchScalarGridSpec(
            num_scalar_prefetch=1, grid=(S, nh),
            in_specs=[pl.BlockSpec((1, rb, C), lambda s, r, p: (s, p[1] * nh + r, 0)), spec],
            out_specs=spec),
        out_shape=jax.ShapeDtypeStruct((S, half, C), BF16), compiler_params=_params(32, 2),
    )(pos, g, got)


def _chip_sum(name, pairs, got, pos):
    _, half, C = pairs.shape
    rb = _row_block(half)
    nh = half // rb

    def body(pos_ref, a_ref, g_ref, o_ref):
        del pos_ref
        o_ref[...] = ((a_ref[0].astype(F32) + g_ref[0].astype(F32)) + g_ref[1].astype(F32)) + g_ref[2].astype(F32)

    return pl.pallas_call(
        body, name=name,
        grid_spec=pltpu.PrefetchScalarGridSpec(
            num_scalar_prefetch=1, grid=(nh,),
            in_specs=[pl.BlockSpec((1, rb, C), lambda r, p: (p[0], r, 0)),
                      pl.BlockSpec((3, rb, C), lambda r, p: (0, r, 0))],
            out_specs=pl.BlockSpec((rb, C), lambda r, p: (p[1] * nh + r, 0))),
        out_shape=jax.ShapeDtypeStruct((2 * half, C), F32), compiler_params=_params(32, 1),
    )(pos, pairs, got)


def _adamw(name, w, g, m, v):
    R, C = w.shape
    rb = _row_block(R)
    c1 = 1.0 - ADAM_B1 ** ADAM_STEP
    c2 = 1.0 - ADAM_B2 ** ADAM_STEP

    def body(w_ref, g_ref, m_ref, v_ref, d_ref, nm_ref, nv_ref):
        gv = g_ref[...]
        nm = ADAM_B1 * m_ref[...] + (1.0 - ADAM_B1) * gv
        nv = ADAM_B2 * v_ref[...] + (1.0 - ADAM_B2) * (gv * gv)
        nm_ref[...] = nm
        nv_ref[...] = nv
        d_ref[...] = -ADAM_LR * ((nm / c1) / (jnp.sqrt(nv / c2) + ADAM_EPS) + ADAM_WD * w_ref[...])

    spec = pl.BlockSpec((rb, C), lambda r: (r, 0))
    sds = jax.ShapeDtypeStruct(w.shape, F32)
    return pl.pallas_call(
        body, name=name, grid=(R // rb,), in_specs=[spec] * 4, out_specs=[spec] * 3,
        out_shape=[sds, sds, sds], compiler_params=_params(40, 1),
    )(w, g, m, v)


def _rel_index():
    m = np.arange(2 * ATT_BLK)
    off = np.where(m < ATT_BLK, m, m - 2 * ATT_BLK)
    rel = np.stack([ATT_BLK * d - off for d in range(N_ATT_TILES)])
    return np.clip(rel, -MAX_REL, MAX_REL) + MAX_REL


def _local_step(x, tgt, g1, w_in, cw, cb, lg, lb, bias, w_out, g2, g3, w_up, fw, fb, w_down, g4, pos=None):
    T = x.shape[0]
    dist = pos is not None
    idx = _rel_index()

    (u, a, qkv), got = _fwd_in_proj(x, g1, w_in, _fetch_rider([w_out, w_down]) if dist else None)
    if dist:
        w_out, w_down = got
    (co, hc), got = _fwd_conv(a, cw, cb, lg, lb, _merge_riders(_forward_rider([w_out, w_down]),
                                                               _fetch_rider([w_up])) if dist else None)
    if dist:
        w_out, w_down, w_up = got
    (ao, lse), got = _fwd_attn(qkv, bias, _forward_rider([w_up]) if dist else None)
    if dist:
        (w_up,) = got
        w_out, w_down = w_out.reshape(D_MODEL, D_MODEL), w_down.reshape(D_FF, D_MODEL)
    mixed, h1, u2 = _fwd_out_proj(co, ao, w_out, x, g2, g3)
    w_cat = _pair_up_weights(w_up)
    hf, pre, act, loss, dy, df, dg4 = _fwd_ffn_loss(u2, w_cat, fw, fb, w_down, h1, tgt, g4)

    tk = min(2048, T)
    du2p, dhf, dfw_g, dfw_v = _bwd_ffn(df, hf, pre, w_cat, fw, w_down)
    gw_up = _wgrad(
        "wgrad_up", [u2], pl.BlockSpec((tk, D_MODEL), lambda s, k: (k, 0)),
        dhf, pl.BlockSpec((1, tk, FF_SHARD), lambda s, k: (s // 2, k, s % 2)),
        pl.BlockSpec((1, D_MODEL, FF_SHARD), lambda s, k: (s, 0, 0)),
        jax.ShapeDtypeStruct((N_CHIPS, D_MODEL, FF_SHARD), F32), N_CHIPS, T, tk)
    gw_down = _wgrad(
        "wgrad_down", [act], pl.BlockSpec((tk, FF_SHARD), lambda s, k: (k, s)),
        df, pl.BlockSpec((tk, D_MODEL), lambda s, k: (k, 0)),
        pl.BlockSpec((FF_SHARD, D_MODEL), lambda s, k: (s, 0)),
        jax.ShapeDtypeStruct((D_FF, D_MODEL), F32), 2, T, tk).reshape(N_CHIPS, D_FF // N_CHIPS, D_MODEL)
    (dh1, dmx, dco, dao, dg3, dg2), _ = _bwd_mid(du2p, dy, h1, mixed, g3, g2, w_out)
    gw_out = _wgrad(
        "wgrad_out", [co, ao], pl.BlockSpec((tk, CONV_W), lambda s, k: (k, 0)),
        dmx, pl.BlockSpec((tk, D_MODEL), lambda s, k: (k, 0)),
        pl.BlockSpec((CONV_W, D_MODEL), lambda s, k: (s, 0)),
        jax.ShapeDtypeStruct((D_MODEL, D_MODEL), F32), 2, T, tk,
        select=lambda s: s, rider=_pair_exchange_rider([gw_up, gw_down]) if dist else None)
    if dist:
        gw_out, got = gw_out
        p_up = _pair_sum("pair_sum_w_up", gw_up, got[0], pos)
        p_down = _pair_sum("pair_sum_w_down", gw_down, got[1], pos)
    gw_out = gw_out.reshape(N_CHIPS, D_MODEL // N_CHIPS, D_MODEL)
    (dproj, dsacc), got = _bwd_attn(
        qkv, ao, dao, lse, bias,
        _merge_riders(_chip_exchange_rider([p_up, p_down]), _pair_exchange_rider([gw_out])) if dist else None)
    if dist:
        gw_up = _chip_sum("chip_sum_w_up", p_up, got[0], pos)
        gw_down = _chip_sum("chip_sum_w_down", p_down, got[1], pos)
        p_out = _pair_sum("pair_sum_w_out", gw_out, got[2], pos)
    (dproj, dcw, dcb, dlg, dlb), got = _bwd_conv(
        dproj, a, dco, hc, cw, lg, lb,
        _merge_riders(_pair_gather_rider([gw_up, gw_down]), _chip_exchange_rider([p_out])) if dist else None)
    if dist:
        gw_up, gw_down = got[:2]
        gw_out = _chip_sum("chip_sum_w_out", p_out, got[2], pos)
    gw_in = _wgrad(
        "wgrad_in", [u], pl.BlockSpec((tk, D_MODEL), lambda s, k: (k, 0)),
        dproj, pl.BlockSpec((tk, IN_SHARD), lambda s, k: (k, s)),
        pl.BlockSpec((1, D_MODEL, IN_SHARD), lambda s, k: (s, 0, 0)),
        jax.ShapeDtypeStruct((N_CHIPS, D_MODEL, IN_SHARD), F32), N_CHIPS, T, tk)
    if dist:
        got = _alone("pair_exchange_w_in", _merge_riders(_pair_exchange_rider([gw_in]), _pair_gather_rider([gw_out])))
        p_in, gw_out = _pair_sum("pair_sum_w_in", gw_in, got[0], pos), got[1]
    (gx, dg1), _ = _bwd_in_proj(dproj, w_in, x, dh1, g1)
    (diag,), got = _diag_sums(dsacc, _chip_exchange_rider([p_in]) if dist else None)
    if dist:
        gw_in = _chip_sum("chip_sum_w_in", p_in, got[0], pos)

    diag = diag.reshape(N_ATT_TILES, N_HEADS, 2 * ATT_BLK)
    onehot = np.zeros((N_ATT_TILES, 2 * ATT_BLK, 2 * MAX_REL + 1), np.float32)
    for d in range(N_ATT_TILES):
        onehot[d, np.arange(2 * ATT_BLK), idx[d]] = 1.0
    drel = jnp.einsum("dhm,dmr->hr", diag, jnp.asarray(onehot), precision=lax.Precision.HIGHEST)

    small = dict(norm_mix_pre=dg1, conv_dw_w=dcw[:CONV_K], conv_dw_b=dcb, conv_ln_g=dlg, conv_ln_b=dlb,
                 rel_bias=drel, norm_mix_post=dg2, norm_ffn_pre=dg3,
                 ffn_dw_w=jnp.concatenate([dfw_g[0, :3], dfw_g[1, :3], dfw_v[0, :3], dfw_v[1, :3]], axis=1),
                 ffn_dw_b=jnp.concatenate([dfw_g[0, 3:4], dfw_g[1, 3:4], dfw_v[0, 3:4], dfw_v[1, 3:4]], axis=1),
                 norm_ffn_post=dg4)
    return loss, gx, small, dict(w_in=gw_in, w_out=gw_out, w_up=gw_up, w_down=gw_down)


SMALL_ORDER = ["norm_mix_pre", "conv_dw_b", "conv_ln_g", "conv_ln_b", "rel_bias", "norm_mix_post",
               "norm_ffn_pre", "ffn_dw_b", "norm_ffn_post", "conv_dw_w", "ffn_dw_w"]


def _pack(parts):
    rows = []
    for p in parts:
        width = -(-p.shape[1] // LANES) * LANES
        rows.append(jnp.pad(p, ((0, 0), (0, width - p.shape[1]))).reshape(-1, LANES))
    packed = jnp.concatenate(rows, axis=0)
    pad = -packed.shape[0] % 8
    return jnp.pad(packed, ((0, pad), (0, 0)))


def _unpack(packed, shapes):
    out, r = [], 0
    for shp in shapes:
        width = -(-shp[1] // LANES) * LANES
        n = shp[0] * width // LANES
        out.append(packed[r:r + n].reshape(shp[0], width)[:, :shp[1]])
        r += n
    return out


WEIGHTS = ["norm_mix_pre", "w_in", "conv_dw_w", "conv_dw_b", "conv_ln_g", "conv_ln_b", "rel_bias", "w_out",
           "norm_mix_post", "norm_ffn_pre", "w_up", "ffn_dw_w", "ffn_dw_b", "w_down", "norm_ffn_post"]
BIG = ["w_in", "w_out", "w_up", "w_down"]


def kernel(x, norm_mix_pre, w_in, conv_dw_w, conv_dw_b, conv_ln_g, conv_ln_b, rel_bias, w_out, norm_mix_post, norm_ffn_pre, w_up, ffn_dw_w, ffn_dw_b, w_down, norm_ffn_post, loss_target, m_norm_mix_pre, m_w_in, m_conv_dw_w, m_conv_dw_b, m_conv_ln_g, m_conv_ln_b, m_rel_bias, m_w_out, m_norm_mix_post, m_norm_ffn_pre, m_w_up, m_ffn_dw_w, m_ffn_dw_b, m_w_down, m_norm_ffn_post, v_norm_mix_pre, v_w_in, v_conv_dw_w, v_conv_dw_b, v_conv_ln_g, v_conv_ln_b, v_rel_bias, v_w_out, v_norm_mix_post, v_norm_ffn_pre, v_w_up, v_ffn_dw_w, v_ffn_dw_b, v_w_down, v_norm_ffn_post):
    args = locals()
    w = {n: args[n][0] for n in WEIGHTS}
    m = {n: args["m_" + n][0] for n in WEIGHTS}
    v = {n: args["v_" + n][0] for n in WEIGHTS}
    for d in (w, m, v):
        d["rel_bias"] = d["rel_bias"].reshape(N_HEADS, 2 * MAX_REL + 1)
        for n in ("norm_mix_pre", "conv_dw_b", "conv_ln_g", "conv_ln_b", "norm_mix_post", "norm_ffn_pre",
                  "ffn_dw_b", "norm_ffn_post"):
            d[n] = d[n].reshape(1, -1)
    shard = 2 * lax.axis_index("x") + lax.axis_index("y")

    cw_sh = jnp.pad(w["conv_dw_w"], ((0, CONV_HALO - CONV_K), (0, 0)))
    fw_sh = jnp.pad(w["ffn_dw_w"], ((0, FF_HALO - 3), (0, 0)))
    pos = jnp.stack([shard, lax.axis_index("c")]).astype(jnp.int32)
    bufs = {n: _place("place_" + n, w[n], pos, BF16) for n in BIG}
    first = [bufs["w_in"], _place("place_conv_dw_w", cw_sh, pos, F32), _place("place_ffn_dw_w", fw_sh, pos, F32)]
    (bias,), first = _bias_tiles(w["rel_bias"], _fetch_rider(first))
    w_in_f, cw_f, fw_f = _alone("all_gather_forward", _forward_rider(list(first)))
    cw_full = jnp.transpose(cw_f, (1, 0, 2)).reshape(CONV_HALO, CONV_W)

    loss, gx, small, big = _local_step(
        x[0], loss_target[0], w["norm_mix_pre"], _join_columns(w_in_f), cw_full, w["conv_dw_b"], w["conv_ln_g"],
        w["conv_ln_b"], bias, bufs["w_out"], w["norm_mix_post"],
        w["norm_ffn_pre"], bufs["w_up"], fw_f, w["ffn_dw_b"].reshape(N_CHIPS, 1, FF_SHARD),
        bufs["w_down"], w["norm_ffn_post"], pos)
    (gsum,), (big["w_in"],) = _all_reduce_small(_pack([small[n] for n in SMALL_ORDER] + [loss]),
                                                _pair_gather_rider([big["w_in"]]))

    grads, deltas, new_m, new_v = {}, {}, {}, {}
    for n in BIG:
        grads[n] = big[n]
        deltas[n], new_m[n], new_v[n] = _adamw("adamw_" + n, w[n], big[n], m[n], v[n])
    shapes = [small[n].shape for n in SMALL_ORDER]
    *reduced, total = _unpack(gsum, shapes + [loss.shape])
    gs = dict(zip(SMALL_ORDER, reduced))
    gs["conv_dw_w"] = lax.dynamic_slice_in_dim(gs["conv_dw_w"], shard * LANES, LANES, axis=1)
    gs["ffn_dw_w"] = lax.dynamic_slice_in_dim(gs["ffn_dw_w"], shard * FF_SHARD, FF_SHARD, axis=1)
    shapes = [gs[n].shape for n in SMALL_ORDER]
    d_p, m_p, v_p = _adamw("adamw_small", _pack([w[n] for n in SMALL_ORDER]), _pack([gs[n] for n in SMALL_ORDER]),
                           _pack([m[n] for n in SMALL_ORDER]), _pack([v[n] for n in SMALL_ORDER]))
    for dst, packed in ((deltas, d_p), (new_m, m_p), (new_v, v_p)):
        dst.update(zip(SMALL_ORDER, _unpack(packed, shapes)))
    grads.update(gs)

    outs = [total[0, 0], gx[None]]
    for group in (grads, deltas, new_m, new_v):
        outs += [group[n].reshape(args[n].shape) for n in WEIGHTS]
    return tuple(outs)
```

```python
import functools
import math
from typing import Callable, NamedTuple

import numpy as np
import jax
import jax.numpy as jnp
from jax import lax
from jax.experimental import pallas as pl
from jax.experimental.pallas import tpu as pltpu

F32 = jnp.float32
BF16 = jnp.bfloat16

D_MODEL = 1024
CONV_W = 512
ATTN_W = 512
N_HEADS = 8
HEAD_DIM = 64
CHUNK = 64
N_LEFT = 8
MAX_REL = 128
CONV_K = 31
CONV_HALO = 32
D_FF = 2816
FF_SHARD = 1408
IN_COLS = 2560
IN_SHARD = 640
EPS = 1e-6
NEG_INF = -1e30
ATT_BLK = 256
N_ATT_TILES = 3
LANES = 128
SUBLANES = 8
N_CHIPS = 4

ADAM_LR = 0.001
ADAM_B1 = 0.9
ADAM_B2 = 0.999
ADAM_EPS = 1e-08
ADAM_WD = 0.01
ADAM_STEP = 10

MESH = pl.DeviceIdType.MESH
ANY = pl.BlockSpec(memory_space=pl.ANY)
VMEM_FULL = pl.BlockSpec(memory_space=pltpu.VMEM)


def _params(vmem_mb, n_grid=0):
    sem = ("arbitrary",) * n_grid if n_grid else None
    return pltpu.CompilerParams(dimension_semantics=sem, vmem_limit_bytes=vmem_mb << 20)


class _Rider(NamedTuple):
    operands: list
    out_shape: list
    aliases: dict
    sems: list
    start: Callable
    finish: Callable


def _merge_riders(a, b):
    ia, oa, sa = len(a.operands), len(a.out_shape), len(a.sems)

    def start(ins, outs, sems):
        a.start(ins[:ia], outs[:oa], sems[:sa])
        b.start(ins[ia:], outs[oa:], sems[sa:])

    def finish(ins, outs, sems):
        a.finish(ins[:ia], outs[:oa], sems[:sa])
        b.finish(ins[ia:], outs[oa:], sems[sa:])

    aliases = {**a.aliases, **{k + ia: v + oa for k, v in b.aliases.items()}}
    return _Rider(a.operands + b.operands, a.out_shape + b.out_shape, aliases, a.sems + b.sems, start, finish)


PIN_BYTES = 1 << 20


def _big(a):
    return math.prod(a.shape) * jnp.dtype(a.dtype).itemsize >= PIN_BYTES


def _pin_args(args):
    return [pltpu.with_memory_space_constraint(a, pltpu.HBM) if _big(a) else a for a in args]


def _call(body, rider, *, grid=(), in_specs=(), out_specs=(), out_shape=(), scratch_shapes=(),
          input_output_aliases=None, **kwargs):
    in_specs, out_specs = list(in_specs), list(out_specs)
    pin_out = lambda shapes: [pltpu.HBM(s.shape, s.dtype) if _big(s) else s for s in shapes]
    out_shape = pin_out(out_shape)
    scratch, aliases = list(scratch_shapes), dict(input_output_aliases or {})
    if rider is None:
        plain = pl.pallas_call(body, grid=grid, in_specs=in_specs, out_specs=out_specs, out_shape=out_shape,
                               scratch_shapes=scratch, input_output_aliases=aliases, **kwargs)
        return lambda *args: (plain(*_pin_args(args)), [])
    n_in, n_out, n_scr = len(in_specs), len(out_specs), len(scratch)
    r_in, r_out = len(rider.operands), len(rider.out_shape)

    def carried(*refs):
        ins, r_ins, refs = refs[:n_in], refs[n_in:n_in + r_in], refs[n_in + r_in:]
        outs, r_outs, refs = refs[:n_out], refs[n_out:n_out + r_out], refs[n_out + r_out:]
        scr, r_sems = refs[:n_scr], refs[n_scr:]
        if not grid:
            rider.start(r_ins, r_outs, r_sems)
            body(*ins, *outs, *scr)
            rider.finish(r_ins, r_outs, r_sems)
            return
        at = [pl.program_id(d) for d in range(len(grid))]
        first = functools.reduce(jnp.logical_and, [p == 0 for p in at])
        last = functools.reduce(jnp.logical_and, [p == n - 1 for p, n in zip(at, grid)])

        @pl.when(first)
        def _():
            rider.start(r_ins, r_outs, r_sems)

        body(*ins, *outs, *scr)

        @pl.when(last)
        def _():
            rider.finish(r_ins, r_outs, r_sems)

    aliases.update({n_in + k: n_out + v for k, v in rider.aliases.items()})
    both = pl.pallas_call(carried, grid=grid, in_specs=in_specs + [ANY] * r_in, out_specs=out_specs + [ANY] * r_out,
                          out_shape=out_shape + pin_out(rider.out_shape), scratch_shapes=scratch + rider.sems,
                          input_output_aliases=aliases, **kwargs)

    def run(*args):
        res = both(*_pin_args(args), *rider.operands)
        return res[:n_out], res[n_out:]

    return run


def _sigmoid(v):
    return 1.0 / (1.0 + jnp.exp(-v))


def _dot(a, b):
    return jnp.dot(a, b, preferred_element_type=F32)


def _dot_nt(a, b):
    return lax.dot_general(a, b, (((1,), (1,)), ((), ())), preferred_element_type=F32)


def _dot_tn(a, b):
    return lax.dot_general(a, b, (((0,), (0,)), ((), ())), preferred_element_type=F32)


def _rms_fwd(v, g):
    r = lax.rsqrt(jnp.mean(v * v, axis=-1, keepdims=True) + EPS)
    return v * r * g, r


def _rms_bwd(dy, v, g):
    r = lax.rsqrt(jnp.mean(v * v, axis=-1, keepdims=True) + EPS)
    vh = v * r
    dvh = dy * g
    dv = r * (dvh - vh * jnp.mean(dvh * vh, axis=-1, keepdims=True))
    return dv, dy * vh


def _join_columns(w):
    S, R, C = w.shape
    rb = 256

    def body(w_ref, o_ref):
        for s in range(S):
            o_ref[:, s * C:(s + 1) * C] = w_ref[s]

    return pl.pallas_call(
        body, name="join_columns", grid=(R // rb,),
        in_specs=[pl.BlockSpec((S, rb, C), lambda r: (0, r, 0))],
        out_specs=pl.BlockSpec((rb, S * C), lambda r: (r, 0)),
        out_shape=jax.ShapeDtypeStruct((R, S * C), w.dtype),
        compiler_params=_params(32, 1),
    )(w)


GLU_COLS = 2 * CONV_W


def _fwd_in_proj(x, g1, w_in, rider=None):
    T = x.shape[0]
    tm = 1024

    def body(x_ref, g_ref, w_ref, u_ref, a_ref, qkv_ref):
        u, _ = _rms_fwd(x_ref[...], g_ref[...])
        u = u.astype(BF16)
        u_ref[...] = u
        a_ref[...] = _dot(u, w_ref[:, :GLU_COLS])
        qkv_ref[...] = _dot(u, w_ref[:, GLU_COLS:]).astype(BF16)

    return _call(
        body, rider, name="fwd_in_proj", grid=(T // tm,),
        in_specs=[pl.BlockSpec((tm, D_MODEL), lambda i: (i, 0)),
                  pl.BlockSpec((1, D_MODEL), lambda i: (0, 0)),
                  pl.BlockSpec((D_MODEL, IN_COLS), lambda i: (0, 0))],
        out_specs=[pl.BlockSpec((tm, D_MODEL), lambda i: (i, 0)),
                   pl.BlockSpec((tm, 1024), lambda i: (i, 0)),
                   pl.BlockSpec((tm, 1536), lambda i: (i, 0))],
        out_shape=[jax.ShapeDtypeStruct((T, D_MODEL), BF16),
                   jax.ShapeDtypeStruct((T, 1024), F32),
                   jax.ShapeDtypeStruct((T, 1536), BF16)],
        compiler_params=_params(56, 1),
    )(x, g1, w_in)


def _fill_shifted(ext, shifted, tm):
    n = tm + CONV_HALO - SUBLANES
    for j in range(1, SUBLANES):
        shifted[j - 1] = ext[j:j + n, :]


def _shifted_rows(ext, shifted, start, rows):
    j = start % SUBLANES
    if j == 0:
        return ext[start:start + rows, :]
    return shifted[j - 1, start - j:start - j + rows, :]


def _fwd_conv(a, cw, cb, lg, lb, rider=None):
    T = a.shape[0]
    tm = 512
    rc = 64

    def body(a_ref, w_ref, b_ref, lg_ref, lb_ref, co_ref, hc_ref, hext, hsh):
        i = pl.program_id(0)

        @pl.when(i == 0)
        def _():
            hext[0:CONV_HALO, :] = jnp.zeros((CONV_HALO, CONV_W), F32)

        @pl.when(i > 0)
        def _():
            hext[0:CONV_HALO, :] = hext[tm:tm + CONV_HALO, :]

        hext[CONV_HALO:CONV_HALO + tm, :] = a_ref[:, :CONV_W] * _sigmoid(a_ref[:, CONV_W:])
        _fill_shifted(hext, hsh, tm)
        for c in range(tm // rc):
            acc = jnp.zeros((rc, CONV_W), F32)
            for k in range(CONV_K):
                acc = acc + w_ref[k:k + 1, :] * _shifted_rows(hext, hsh, c * rc + 2 + k, rc)
            hc = acc + b_ref[...]
            hc_ref[c * rc:(c + 1) * rc, :] = hc
            mu = jnp.mean(hc, axis=-1, keepdims=True)
            xc = hc - mu
            var = jnp.mean(xc * xc, axis=-1, keepdims=True)
            z = xc * lax.rsqrt(var + EPS) * lg_ref[...] + lb_ref[...]
            co_ref[c * rc:(c + 1) * rc, :] = (z * _sigmoid(z)).astype(BF16)

    return _call(
        body, rider, name="fwd_conv", grid=(T // tm,),
        in_specs=[pl.BlockSpec((tm, 1024), lambda i: (i, 0)),
                  pl.BlockSpec((CONV_HALO, CONV_W), lambda i: (0, 0)),
                  pl.BlockSpec((1, CONV_W), lambda i: (0, 0)),
                  pl.BlockSpec((1, CONV_W), lambda i: (0, 0)),
                  pl.BlockSpec((1, CONV_W), lambda i: (0, 0))],
        out_specs=[pl.BlockSpec((tm, CONV_W), lambda i: (i, 0)),
                   pl.BlockSpec((tm, CONV_W), lambda i: (i, 0))],
        out_shape=[jax.ShapeDtypeStruct((T, CONV_W), BF16),
                   jax.ShapeDtypeStruct((T, CONV_W), F32)],
        scratch_shapes=[pltpu.VMEM((tm + CONV_HALO, CONV_W), F32),
                        pltpu.VMEM((SUBLANES - 1, tm + CONV_HALO - SUBLANES, CONV_W), F32)],
        compiler_params=_params(40, 1),
    )(a, cw, cb, lg, lb)


def _row_skew(v, sign):
    rows, width = v.shape
    row = lax.broadcasted_iota(jnp.int32, (rows, 1), 0)
    for b in range(int(math.log2(rows))):
        shift = (1 << b) if sign > 0 else width - (1 << b)
        v = jnp.where(((row >> b) & 1) == 1, pltpu.roll(v, shift, 1), v)
    return v


def _att_visible(d):
    rq = lax.broadcasted_iota(jnp.int32, (ATT_BLK, ATT_BLK), 0) // CHUNK
    ck = lax.broadcasted_iota(jnp.int32, (ATT_BLK, ATT_BLK), 1) // CHUNK
    slack = ATT_BLK
    above = jnp.where(d == 0, 0, slack)
    below = jnp.where(d == 2, 0, slack)
    return (ck <= rq + above) & (ck >= rq - below)


def _bias_tiles(rel, rider=None):
    vec = jnp.transpose(rel[:, _rel_index()], (1, 0, 2)).reshape(N_ATT_TILES * N_HEADS, 1, 2 * ATT_BLK)

    def body(v_ref, o_ref):
        visible = _att_visible(pl.program_id(0))
        for h in range(N_HEADS):
            full = _row_skew(jnp.broadcast_to(v_ref[h], (ATT_BLK, 2 * ATT_BLK)), 1)
            o_ref[h] = jnp.where(visible, full[:, :ATT_BLK], NEG_INF)

    return _call(
        body, rider, name="bias_tiles", grid=(N_ATT_TILES,),
        in_specs=[pl.BlockSpec((N_HEADS, 1, 2 * ATT_BLK), lambda d: (d, 0, 0))],
        out_specs=[pl.BlockSpec((N_HEADS, ATT_BLK, ATT_BLK), lambda d: (d, 0, 0))],
        out_shape=[jax.ShapeDtypeStruct((N_ATT_TILES * N_HEADS, ATT_BLK, ATT_BLK), F32)],
        compiler_params=_params(32, 1),
    )(vec)


def _diag_sums(ds, rider=None):
    def body(d_ref, o_ref):
        wide = jnp.concatenate([d_ref[0], jnp.zeros((ATT_BLK, ATT_BLK), F32)], axis=1)
        o_ref[0] = jnp.sum(_row_skew(wide, -1), axis=0, keepdims=True)

    return _call(
        body, rider, name="diag_sums", grid=(N_ATT_TILES * N_HEADS,),
        in_specs=[pl.BlockSpec((1, ATT_BLK, ATT_BLK), lambda n: (n, 0, 0))],
        out_specs=[pl.BlockSpec((1, 1, 2 * ATT_BLK), lambda n: (n, 0, 0))],
        out_shape=[jax.ShapeDtypeStruct((N_ATT_TILES * N_HEADS, 1, 2 * ATT_BLK), F32)],
        compiler_params=_params(16, 1),
    )(ds)


def _head_mask(h):
    lane = lax.broadcasted_iota(jnp.int32, (1, LANES), 1)
    return (lane // HEAD_DIM) == (h % 2)


def _fwd_attn(qkv, bias, rider=None):
    T = qkv.shape[0]
    nb = T // ATT_BLK
    scale = HEAD_DIM ** -0.5

    def body(q_ref, k0_ref, k1_ref, k2_ref, v0_ref, v1_ref, v2_ref, b_ref, o_ref, lse_ref):
        i = pl.program_id(0)

        @pl.when(i >= N_ATT_TILES - 1)
        def _():
            block(i, False, q_ref, k0_ref, k1_ref, k2_ref, v0_ref, v1_ref, v2_ref, b_ref, o_ref, lse_ref)

        @pl.when(i < N_ATT_TILES - 1)
        def _():
            block(i, True, q_ref, k0_ref, k1_ref, k2_ref, v0_ref, v1_ref, v2_ref, b_ref, o_ref, lse_ref)

    def block(i, hide_absent, q_ref, k0_ref, k1_ref, k2_ref, v0_ref, v1_ref, v2_ref, b_ref, o_ref, lse_ref):
        k_refs = (k0_ref, k1_ref, k2_ref)
        v_refs = (v0_ref, v1_ref, v2_ref)
        lane = lax.broadcasted_iota(jnp.int32, (1, LANES), 1)
        lse_tile = jnp.zeros((ATT_BLK, LANES), F32)
        for g in range(N_HEADS // 2):
            cols = slice(g * LANES, (g + 1) * LANES)
            qg = q_ref[:, cols] * scale
            both = jnp.concatenate([jnp.where(_head_mask(h), qg, jnp.zeros_like(qg)) for h in (2 * g, 2 * g + 1)],
                                   axis=0)
            raw = [_dot_nt(both, k_refs[d][:, cols]) for d in range(N_ATT_TILES)]
            probs, sums = [], []
            for h in (2 * g, 2 * g + 1):
                mine = slice((h % 2) * ATT_BLK, (h % 2 + 1) * ATT_BLK)
                s = []
                for d in range(N_ATT_TILES):
                    sd = raw[d][mine] + b_ref[d * N_HEADS + h]
                    if d > 0 and hide_absent:
                        sd = jnp.where(i >= d, sd, NEG_INF)
                    s.append(sd)
                m = jnp.maximum(jnp.maximum(jnp.max(s[0], axis=-1, keepdims=True),
                                            jnp.max(s[1], axis=-1, keepdims=True)),
                                jnp.max(s[2], axis=-1, keepdims=True))
                p = [jnp.exp(sd - m) for sd in s]
                l = (jnp.sum(p[0], axis=-1, keepdims=True) + jnp.sum(p[1], axis=-1, keepdims=True)
                     + jnp.sum(p[2], axis=-1, keepdims=True))
                probs.append(p)
                sums.append(l)
                lse_tile = jnp.where(lane == h, m + jnp.log(l), lse_tile)
            out = None
            for d in range(N_ATT_TILES):
                term = _dot(jnp.concatenate([probs[0][d], probs[1][d]], axis=0).astype(BF16), v_refs[d][:, cols])
                out = term if out is None else out + term
            og = jnp.where(_head_mask(2 * g), out[:ATT_BLK] / sums[0], out[ATT_BLK:] / sums[1])
            o_ref[:, cols] = og.astype(BF16)
        lse_ref[...] = lse_tile

    def kv_spec(d, col):
        return pl.BlockSpec((ATT_BLK, ATTN_W), lambda i: (jnp.maximum(i - d, 0), col))

    return _call(
        body, rider, name="fwd_attn", grid=(nb,),
        in_specs=[pl.BlockSpec((ATT_BLK, ATTN_W), lambda i: (i, 0)),
                  kv_spec(0, 1), kv_spec(1, 1), kv_spec(2, 1),
                  kv_spec(0, 2), kv_spec(1, 2), kv_spec(2, 2),
                  pl.BlockSpec((N_ATT_TILES * N_HEADS, ATT_BLK, ATT_BLK), lambda i: (0, 0, 0))],
        out_specs=[pl.BlockSpec((ATT_BLK, ATTN_W), lambda i: (i, 0)),
                   pl.BlockSpec((ATT_BLK, LANES), lambda i: (i, 0))],
        out_shape=[jax.ShapeDtypeStruct((T, ATTN_W), BF16),
                   jax.ShapeDtypeStruct((T, LANES), F32)],
        compiler_params=_params(40, 1),
    )(qkv, qkv, qkv, qkv, qkv, qkv, qkv, bias)


def _fwd_out_proj(co, ao, w_out, x, g2, g3):
    T = x.shape[0]
    tm = 1024

    def body(co_ref, ao_ref, w_ref, x_ref, g2_ref, g3_ref, mixed_ref, h1_ref, u2_ref):
        mixed = _dot(co_ref[...], w_ref[0:CONV_W, :]) + _dot(ao_ref[...], w_ref[CONV_W:, :])
        mixed_ref[...] = mixed.astype(BF16)
        y, _ = _rms_fwd(mixed, g2_ref[...])
        h1 = x_ref[...] + y
        h1_ref[...] = h1
        u2, _ = _rms_fwd(h1, g3_ref[...])
        u2_ref[...] = u2.astype(BF16)

    row = lambda w: pl.BlockSpec((tm, w), lambda i: (i, 0))
    vec = pl.BlockSpec((1, D_MODEL), lambda i: (0, 0))
    return _call(
        body, None, name="fwd_out_proj", grid=(T // tm,),
        in_specs=[row(CONV_W), row(ATTN_W), pl.BlockSpec((D_MODEL, D_MODEL), lambda i: (0, 0)),
                  row(D_MODEL), vec, vec],
        out_specs=[row(D_MODEL), row(D_MODEL), row(D_MODEL)],
        out_shape=[jax.ShapeDtypeStruct((T, D_MODEL), BF16),
                   jax.ShapeDtypeStruct((T, D_MODEL), F32),
                   jax.ShapeDtypeStruct((T, D_MODEL), BF16)],
        compiler_params=_params(56, 1),
    )(co, ao, w_out, x, g2, g3)[0]


GELU_C = math.sqrt(2.0 / math.pi)
GELU_A = 0.044715


def _gelu_and_grad(v):
    sq = v * v
    th = jnp.tanh(v * (GELU_C + (GELU_C * GELU_A) * sq))
    half = 0.5 + 0.5 * th
    gl = v * half
    dgl = half + (v * (half * (1.0 - th))) * (GELU_C + (3.0 * GELU_C * GELU_A) * sq)
    return gl, dgl


FF_TM = 256
FF_HALO = 16
FF_CHUNKS = [(lo, min(lo + 256, FF_SHARD)) for lo in range(0, FF_SHARD, 256)]


def _rows_before(prev, cur):
    ext = jnp.concatenate([prev, cur], axis=0)
    return pltpu.roll(ext, 1, 0)[SUBLANES:], pltpu.roll(ext, 2, 0)[SUBLANES:]


def _rows_after(cur, nxt):
    ext = jnp.concatenate([cur, nxt], axis=0)
    n = ext.shape[0]
    return pltpu.roll(ext, n - 1, 0)[:cur.shape[0]], pltpu.roll(ext, n - 2, 0)[:cur.shape[0]]


def _pair_up_weights(w_up):
    rb = 256

    def body(g_ref, v_ref, o_ref):
        for lo, hi in FF_CHUNKS:
            o_ref[0, :, 2 * lo:lo + hi] = g_ref[0, :, lo:hi]
            o_ref[0, :, lo + hi:2 * hi] = v_ref[0, :, lo:hi]

    return pl.pallas_call(
        body, name="pair_up_weights", grid=(2, D_MODEL // rb),
        in_specs=[pl.BlockSpec((1, rb, FF_SHARD), lambda s, r: (s, r, 0)),
                  pl.BlockSpec((1, rb, FF_SHARD), lambda s, r: (s + 2, r, 0))],
        out_specs=pl.BlockSpec((1, rb, 2 * FF_SHARD), lambda s, r: (s, r, 0)),
        out_shape=jax.ShapeDtypeStruct((2, D_MODEL, 2 * FF_SHARD), w_up.dtype),
        compiler_params=_params(32, 2),
    )(w_up, w_up)


def _fwd_ffn_loss(u2, w_cat, fw, fb, w_down, h1, tgt, g4):
    T = u2.shape[0]
    tm = FF_TM

    def body(u_ref, w_ref, fw_ref, fb_ref, wd_ref, h1_ref, t_ref, g_ref,
             hf_ref, pre_ref, act_ref, loss_ref, dy_ref, df_ref, dg_ref, carg, carv):
        i = pl.program_id(0)

        @pl.when(i == 0)
        def _():
            carg[...] = jnp.zeros(carg.shape, F32)
            carv[...] = jnp.zeros(carv.shape, F32)

        u = u_ref[...]
        f = None
        chunks = [(s, lo, hi) for s in range(2) for lo, hi in FF_CHUNKS]
        up = lambda s, lo, hi: _dot(u, w_ref[s, :, 2 * lo:2 * hi])
        ahead = up(*chunks[0])
        for c, (s, lo, hi) in enumerate(chunks):
            conv = []
            hs = (ahead[:, :hi - lo], ahead[:, hi - lo:])
            if c + 1 < len(chunks):
                ahead = up(*chunks[c + 1])
            at = slice(s * FF_SHARD + lo, s * FF_SHARD + hi)
            for n, car in enumerate((carg, carv)):
                h0 = hs[n]
                hf_ref[n, :, at] = h0.astype(BF16)
                h1v, h2v = _rows_before(car[:, at], h0)
                car[:, at] = h0[tm - SUBLANES:, :]
                conv.append(fw_ref[2 * n + s, 0:1, lo:hi] * h2v + fw_ref[2 * n + s, 1:2, lo:hi] * h1v
                            + fw_ref[2 * n + s, 2:3, lo:hi] * h0 + fb_ref[2 * n + s, :, lo:hi])
            pre_ref[0, :, at] = conv[0].astype(BF16)
            pre_ref[1, :, at] = conv[1].astype(BF16)
            gl, _ = _gelu_and_grad(conv[0])
            act = (gl * conv[1]).astype(BF16)
            act_ref[:, at] = act
            term = _dot(act, wd_ref[at, :])
            f = term if f is None else f + term

        r, _ = _rms_fwd(f, g_ref[...])
        e = (h1_ref[...] + r) - t_ref[...]
        dy = e * (1.0 / D_MODEL)
        dy_ref[...] = dy
        df, dg_rows = _rms_bwd(dy, f, g_ref[...])
        df_ref[...] = df.astype(BF16)
        part = 0.5 * jnp.sum(jnp.mean(e * e, axis=-1, keepdims=True), axis=0, keepdims=True)
        dg = jnp.sum(dg_rows, axis=0, keepdims=True)

        @pl.when(i == 0)
        def _():
            loss_ref[...] = part
            dg_ref[...] = dg

        @pl.when(i > 0)
        def _():
            loss_ref[...] += part
            dg_ref[...] += dg

    row = lambda w: pl.BlockSpec((tm, w), lambda i: (i, 0))
    vec = pl.BlockSpec((1, D_MODEL), lambda i: (0, 0))
    once = pl.Buffered(1)
    return _call(
        body, None, name="fwd_ffn_loss", grid=(T // tm,),
        in_specs=[row(D_MODEL),
                  pl.BlockSpec((2, D_MODEL, 2 * FF_SHARD), lambda i: (0, 0, 0), pipeline_mode=once),
                  pl.BlockSpec((N_CHIPS, FF_HALO, FF_SHARD), lambda i: (0, 0, 0)),
                  pl.BlockSpec((N_CHIPS, 1, FF_SHARD), lambda i: (0, 0, 0)),
                  pl.BlockSpec((D_FF, D_MODEL), lambda i: (0, 0), pipeline_mode=once),
                  row(D_MODEL), row(D_MODEL), vec],
        out_specs=[pl.BlockSpec((2, tm, D_FF), lambda i: (0, i, 0)),
                   pl.BlockSpec((2, tm, D_FF), lambda i: (0, i, 0)),
                   row(D_FF), pl.BlockSpec((1, 1), lambda i: (0, 0)), row(D_MODEL), row(D_MODEL), vec],
        out_shape=[jax.ShapeDtypeStruct((2, T, D_FF), BF16),
                   jax.ShapeDtypeStruct((2, T, D_FF), BF16),
                   jax.ShapeDtypeStruct((T, D_FF), BF16),
                   jax.ShapeDtypeStruct((1, 1), F32),
                   jax.ShapeDtypeStruct((T, D_MODEL), F32),
                   jax.ShapeDtypeStruct((T, D_MODEL), BF16),
                   jax.ShapeDtypeStruct((1, D_MODEL), F32)],
        scratch_shapes=[pltpu.VMEM((SUBLANES, D_FF), F32), pltpu.VMEM((SUBLANES, D_FF), F32)],
        compiler_params=_params(60, 1),
    )(u2, w_cat, fw, fb, w_down, h1, tgt, g4)[0]


def _bwd_ffn(df, hf, pre, w_cat, fw, w_down):
    T = df.shape[0]
    tm = FF_TM
    ni = T // tm

    def body(df_ref, hf_ref, pre_ref, wd_ref, w_ref, fw_ref,
             du_ref, dhf_ref, dwg_ref, dwv_ref, carg, carv):
        i = pl.program_id(0)

        @pl.when(i == 0)
        def _():
            dwg_ref[...] = jnp.zeros(dwg_ref.shape, F32)
            dwv_ref[...] = jnp.zeros(dwv_ref.shape, F32)
            carg[...] = jnp.zeros(carg.shape, F32)
            carv[...] = jnp.zeros(carv.shape, F32)

        df = df_ref[...]
        du = None
        chunks = [(s, lo, hi) for s in range(2) for lo, hi in FF_CHUNKS]
        cols = lambda s, lo, hi: slice(s * FF_SHARD + lo, s * FF_SHARD + hi)
        down = lambda s, lo, hi: _dot_nt(df, wd_ref[cols(s, lo, hi), :])
        ahead = down(*chunks[0])
        for c, (s, lo, hi) in enumerate(chunks):
            dact = ahead
            if c + 1 < len(chunks):
                ahead = down(*chunks[c + 1])
            at = cols(s, lo, hi)
            pre_g = pre_ref[0, :, at].astype(F32)
            pre_v = pre_ref[1, :, at].astype(F32)
            gl, dgl = _gelu_and_grad(pre_g)
            dpre = (dact * pre_v * dgl, dact * gl)
            dhs = []
            for n, (car, dw_ref) in enumerate(((carg, dwg_ref), (carv, dwv_ref))):
                dp = dpre[n]
                h0 = hf_ref[n, :, at].astype(F32)
                up1, up2 = _rows_after(dp, car[:, at])
                car[:, at] = dp[0:SUBLANES, :]
                for k, shifted in enumerate((up2, up1, dp)):
                    dw_ref[s, k:k + 1, lo:hi] += jnp.sum(shifted * h0, axis=0, keepdims=True)
                dw_ref[s, 3:4, lo:hi] += jnp.sum(dp, axis=0, keepdims=True)
                taps = 2 * n + s
                dh = (fw_ref[taps, 2:3, lo:hi] * dp + fw_ref[taps, 1:2, lo:hi] * up1
                      + fw_ref[taps, 0:1, lo:hi] * up2).astype(BF16)
                dhf_ref[n, :, at] = dh
                dhs.append(dh)
            term = _dot_nt(jnp.concatenate(dhs, axis=1), w_ref[s, :, 2 * lo:2 * hi])
            du = term if du is None else du + term
        du_ref[...] = du.astype(BF16)

    rev = lambda i: ni - 1 - i
    once = pl.Buffered(1)
    dwspec = pl.BlockSpec((2, FF_HALO, FF_SHARD), lambda i: (0, 0, 0))
    return _call(
        body, None, name="bwd_ffn", grid=(ni,),
        in_specs=[pl.BlockSpec((tm, D_MODEL), lambda i: (rev(i), 0)),
                  pl.BlockSpec((2, tm, D_FF), lambda i: (0, rev(i), 0)),
                  pl.BlockSpec((2, tm, D_FF), lambda i: (0, rev(i), 0)),
                  pl.BlockSpec((D_FF, D_MODEL), lambda i: (0, 0), pipeline_mode=once),
                  pl.BlockSpec((2, D_MODEL, 2 * FF_SHARD), lambda i: (0, 0, 0), pipeline_mode=once),
                  pl.BlockSpec((N_CHIPS, FF_HALO, FF_SHARD), lambda i: (0, 0, 0))],
        out_specs=[pl.BlockSpec((tm, D_MODEL), lambda i: (rev(i), 0)),
                   pl.BlockSpec((2, tm, D_FF), lambda i: (0, rev(i), 0)),
                   dwspec, dwspec],
        out_shape=[jax.ShapeDtypeStruct((T, D_MODEL), BF16),
                   jax.ShapeDtypeStruct((2, T, D_FF), BF16),
                   jax.ShapeDtypeStruct((2, FF_HALO, FF_SHARD), F32),
                   jax.ShapeDtypeStruct((2, FF_HALO, FF_SHARD), F32)],
        scratch_shapes=[pltpu.VMEM((SUBLANES, D_FF), F32), pltpu.VMEM((SUBLANES, D_FF), F32)],
        compiler_params=_params(60, 1),
    )(df, hf, pre, w_down, w_cat, fw)[0]


def _bwd_mid(du2p, dy, h1, mixed, g3, g2, w_out, rider=None):
    T = dy.shape[0]
    tm = 512

    def body(du_ref, dy_ref, h1_ref, mx_ref, g3_ref, g2_ref, w_ref,
             dh1_ref, dmx_ref, dco_ref, dao_ref, dg3_ref, dg2_ref):
        i = pl.program_id(0)
        dres, dg3_rows = _rms_bwd(du_ref[...].astype(F32), h1_ref[...], g3_ref[...])
        dh1 = dy_ref[...] + dres
        dh1_ref[...] = dh1
        dmx, dg2_rows = _rms_bwd(dh1, mx_ref[...].astype(F32), g2_ref[...])
        dmx = dmx.astype(BF16)
        dmx_ref[...] = dmx
        dcat = _dot_nt(dmx, w_ref[...])
        dco_ref[...] = dcat[:, :CONV_W]
        dao_ref[...] = dcat[:, CONV_W:].astype(BF16)
        dg3 = jnp.sum(dg3_rows, axis=0, keepdims=True)
        dg2 = jnp.sum(dg2_rows, axis=0, keepdims=True)

        @pl.when(i == 0)
        def _():
            dg3_ref[...] = dg3
            dg2_ref[...] = dg2

        @pl.when(i > 0)
        def _():
            dg3_ref[...] += dg3
            dg2_ref[...] += dg2

    row = lambda w: pl.BlockSpec((tm, w), lambda i: (i, 0))
    vec = pl.BlockSpec((1, D_MODEL), lambda i: (0, 0))
    return _call(
        body, rider, name="bwd_mid", grid=(T // tm,),
        in_specs=[row(D_MODEL), row(D_MODEL), row(D_MODEL),
                  row(D_MODEL), vec, vec, pl.BlockSpec((D_MODEL, D_MODEL), lambda i: (0, 0))],
        out_specs=[row(D_MODEL), row(D_MODEL), row(CONV_W), row(ATTN_W), vec, vec],
        out_shape=[jax.ShapeDtypeStruct((T, D_MODEL), F32),
                   jax.ShapeDtypeStruct((T, D_MODEL), BF16),
                   jax.ShapeDtypeStruct((T, CONV_W), F32),
                   jax.ShapeDtypeStruct((T, ATTN_W), BF16),
                   jax.ShapeDtypeStruct((1, D_MODEL), F32),
                   jax.ShapeDtypeStruct((1, D_MODEL), F32)],
        compiler_params=_params(48, 1),
    )(du2p, dy, h1, mixed, g3, g2, w_out)


def _bwd_attn(qkv, ao, dao, lse, bias, rider=None):
    T = qkv.shape[0]
    nb = T // ATT_BLK
    scale = HEAD_DIM ** -0.5

    def body(k_ref, v_ref, q0, q1, q2, do0, do1, do2, o0, o1, o2, l0, l1, l2, b_ref,
             dp_ref, ds_ref, acc1, acc2):
        j = pl.program_id(0)
        q_refs, do_refs, o_refs, l_refs = (q0, q1, q2), (do0, do1, do2), (o0, o1, o2), (l0, l1, l2)

        @pl.when(j == 0)
        def _():
            ds_ref[...] = jnp.zeros(ds_ref.shape, F32)
            acc1[...] = jnp.zeros(acc1.shape, F32)
            acc2[...] = jnp.zeros(acc2.shape, F32)

        dq_new = [[], [], []]
        dk_cols, dv_cols = [], []
        for g in range(N_HEADS // 2):
            cols = slice(g * LANES, (g + 1) * LANES)
            kg = k_ref[:, cols]
            vg = v_ref[:, cols]
            dkt = jnp.zeros((LANES, ATT_BLK), F32)
            dvt = jnp.zeros((LANES, ATT_BLK), F32)
            dqg = [jnp.zeros((ATT_BLK, LANES), F32) for _ in range(N_ATT_TILES)]
            row_head = lax.broadcasted_iota(jnp.int32, (LANES, 1), 0) // HEAD_DIM
            for d in range(N_ATT_TILES):
                qg = q_refs[d][:, cols] * scale
                dog = do_refs[d][:, cols]
                if d > 0:
                    dog = jnp.where(j + d < nb, dog, jnp.zeros_like(dog))
                prod = dog.astype(F32) * o_refs[d][:, cols].astype(F32)
                qgt = qg.astype(F32).T
                dogt = dog.astype(F32).T
                heads = (2 * g, 2 * g + 1)
                s_both = _dot_nt(jnp.concatenate([jnp.where(_head_mask(h), qg, jnp.zeros_like(qg)) for h in heads],
                                                 axis=0), kg)
                dp_both = _dot_nt(jnp.concatenate([jnp.where(_head_mask(h), dog, jnp.zeros_like(dog)) for h in heads],
                                                  axis=0), vg)
                held = []
                for h in heads:
                    hm = _head_mask(h)
                    mine = row_head == (h % 2)
                    rows = slice((h % 2) * ATT_BLK, (h % 2 + 1) * ATT_BLK)
                    qht = jnp.where(mine, qgt, 0.0).astype(BF16)
                    doht = jnp.where(mine, dogt, 0.0).astype(BF16)
                    delta = jnp.sum(jnp.where(hm, prod, 0.0), axis=-1, keepdims=True)
                    s = s_both[rows] + b_ref[d * N_HEADS + h]
                    p = jnp.exp(s - l_refs[d][:, h:h + 1])
                    dvt = dvt + _dot(doht, p.astype(BF16))
                    dpm = dp_both[rows]
                    dsc = p * (dpm - delta)
                    ds_ref[d * N_HEADS + h] += dsc
                    dsb = dsc.astype(BF16)
                    held.append(dsb)
                    dkt = dkt + _dot(qht, dsb)
                both = _dot(jnp.concatenate(held, axis=0), kg)
                dqg[d] = jnp.where(_head_mask(2 * g), both[:ATT_BLK], both[ATT_BLK:])
            for d in range(N_ATT_TILES):
                dq_new[d].append(dqg[d])
            dk_cols.append(dkt.T)
            dv_cols.append(dvt.T)
        x0, x1, x2 = (jnp.concatenate(c, axis=1) * scale for c in dq_new)
        dp_ref[:, 0:1024] = jnp.zeros((ATT_BLK, 1024), BF16)
        dp_ref[:, 1024:1536] = (acc1[...] + x0).astype(BF16)
        dp_ref[:, 1536:2048] = jnp.concatenate(dk_cols, axis=1).astype(BF16)
        dp_ref[:, 2048:2560] = jnp.concatenate(dv_cols, axis=1).astype(BF16)
        acc1[...] = acc2[...] + x1
        acc2[...] = x2

    def fwd_spec(d, width, col):
        return pl.BlockSpec((ATT_BLK, width), lambda j: (jnp.minimum(j + d, nb - 1), col))

    return _call(
        body, rider, name="bwd_attn", grid=(nb,),
        in_specs=[pl.BlockSpec((ATT_BLK, ATTN_W), lambda j: (j, 1)),
                  pl.BlockSpec((ATT_BLK, ATTN_W), lambda j: (j, 2)),
                  fwd_spec(0, ATTN_W, 0), fwd_spec(1, ATTN_W, 0), fwd_spec(2, ATTN_W, 0),
                  fwd_spec(0, ATTN_W, 0), fwd_spec(1, ATTN_W, 0), fwd_spec(2, ATTN_W, 0),
                  fwd_spec(0, ATTN_W, 0), fwd_spec(1, ATTN_W, 0), fwd_spec(2, ATTN_W, 0),
                  fwd_spec(0, LANES, 0), fwd_spec(1, LANES, 0), fwd_spec(2, LANES, 0),
                  pl.BlockSpec((N_ATT_TILES * N_HEADS, ATT_BLK, ATT_BLK), lambda j: (0, 0, 0))],
        out_specs=[pl.BlockSpec((ATT_BLK, IN_COLS), lambda j: (j, 0)),
                   pl.BlockSpec((N_ATT_TILES * N_HEADS, ATT_BLK, ATT_BLK), lambda j: (0, 0, 0))],
        out_shape=[jax.ShapeDtypeStruct((T, IN_COLS), BF16),
                   jax.ShapeDtypeStruct((N_ATT_TILES * N_HEADS, ATT_BLK, ATT_BLK), F32)],
        scratch_shapes=[pltpu.VMEM((ATT_BLK, ATTN_W), F32), pltpu.VMEM((ATT_BLK, ATTN_W), F32)],
        compiler_params=_params(56, 1),
    )(qkv, qkv, qkv, qkv, qkv, dao, dao, dao, ao, ao, ao, lse, lse, lse, bias)


def _bwd_conv(dproj, a, dco, hc, cw, lg, lb, rider=None):
    T = a.shape[0]
    tm = 512
    rc = 32
    ni = T // tm
    hb = tm // CONV_HALO

    def body(dp_in, a_ref, ap_ref, dco_ref, dcon_ref, hc_ref, hcn_ref, w_ref, lg_ref, lb_ref,
             dp_ref, dw_ref, db_ref, dlg_ref, dlb_ref, hext, dext, hsh, dsh, dwacc):
        del dp_in
        i = pl.program_id(0)

        def ln_bwd(dco_v, hc_v):
            mu = jnp.mean(hc_v, axis=-1, keepdims=True)
            xc = hc_v - mu
            rstd = lax.rsqrt(jnp.mean(xc * xc, axis=-1, keepdims=True) + EPS)
            xh = xc * rstd
            z = xh * lg_ref[...] + lb_ref[...]
            sg = _sigmoid(z)
            dz = dco_v * (sg * (1.0 + z * (1.0 - sg)))
            dxh = dz * lg_ref[...]
            dhc = rstd * (dxh - jnp.mean(dxh, axis=-1, keepdims=True)
                          - xh * jnp.mean(dxh * xh, axis=-1, keepdims=True))
            return dhc, dz * xh, dz

        hext[0:CONV_HALO, :] = jnp.where(i > 0, ap_ref[:, :CONV_W] * _sigmoid(ap_ref[:, CONV_W:]), 0.0)
        hext[CONV_HALO:CONV_HALO + tm, :] = a_ref[:, :CONV_W] * _sigmoid(a_ref[:, CONV_W:])
        dhc, dlg_rows, dlb_rows = ln_bwd(dco_ref[...], hc_ref[...])
        dext[0:tm, :] = dhc
        dhc_next, _, _ = ln_bwd(dcon_ref[...], hcn_ref[...])
        dext[tm:tm + CONV_HALO, :] = jnp.where(i < ni - 1, dhc_next, 0.0)

        @pl.when(i == 0)
        def _():
            dw_ref[...] = jnp.zeros(dw_ref.shape, F32)
            db_ref[...] = jnp.zeros(db_ref.shape, F32)
            dlg_ref[...] = jnp.zeros(dlg_ref.shape, F32)
            dlb_ref[...] = jnp.zeros(dlb_ref.shape, F32)

            dwacc[...] = jnp.zeros(dwacc.shape, F32)

        db_ref[...] += jnp.sum(dhc, axis=0, keepdims=True)
        dlg_ref[...] += jnp.sum(dlg_rows, axis=0, keepdims=True)
        dlb_ref[...] += jnp.sum(dlb_rows, axis=0, keepdims=True)
        _fill_shifted(hext, hsh, tm)
        _fill_shifted(dext, dsh, tm)
        for c in range(tm // rc):
            r0 = c * rc
            dh = jnp.zeros((rc, CONV_W), F32)
            dhc_c = dext[r0:r0 + rc, :]
            for k in range(CONV_K):
                dh = dh + w_ref[k:k + 1, :] * _shifted_rows(dext, dsh, r0 + 30 - k, rc)
                prod = dhc_c * _shifted_rows(hext, hsh, r0 + 2 + k, rc)
                dwacc[k] += jnp.sum(prod.reshape(rc // SUBLANES, SUBLANES, CONV_W), axis=0)
            av = a_ref[r0:r0 + rc, :CONV_W]
            sg = _sigmoid(a_ref[r0:r0 + rc, CONV_W:])
            dp_ref[r0:r0 + rc, 0:CONV_W] = (dh * sg).astype(BF16)
            dp_ref[r0:r0 + rc, CONV_W:] = (dh * av * sg * (1.0 - sg)).astype(BF16)

        @pl.when(i == ni - 1)
        def _():
            dw_ref[...] = jnp.sum(dwacc[...], axis=1)

    row = lambda w: pl.BlockSpec((tm, w), lambda i: (i, 0))
    prev = lambda w: pl.BlockSpec((CONV_HALO, w), lambda i: (jnp.maximum(i * hb - 1, 0), 0))
    nxt = lambda w: pl.BlockSpec((CONV_HALO, w), lambda i: (jnp.minimum((i + 1) * hb, ni * hb - 1), 0))
    vec = pl.BlockSpec((1, CONV_W), lambda i: (0, 0))
    return _call(
        body, rider, name="bwd_conv", grid=(ni,),
        in_specs=[ANY, row(1024), prev(1024), row(CONV_W), nxt(CONV_W), row(CONV_W), nxt(CONV_W),
                  pl.BlockSpec((CONV_HALO, CONV_W), lambda i: (0, 0)), vec, vec],
        out_specs=[pl.BlockSpec((tm, 1024), lambda i: (i, 0)),
                   pl.BlockSpec((CONV_HALO, CONV_W), lambda i: (0, 0)), vec, vec, vec],
        out_shape=[jax.ShapeDtypeStruct((T, IN_COLS), BF16),
                   jax.ShapeDtypeStruct((CONV_HALO, CONV_W), F32),
                   jax.ShapeDtypeStruct((1, CONV_W), F32),
                   jax.ShapeDtypeStruct((1, CONV_W), F32),
                   jax.ShapeDtypeStruct((1, CONV_W), F32)],
        scratch_shapes=[pltpu.VMEM((tm + CONV_HALO, CONV_W), F32), pltpu.VMEM((tm + CONV_HALO, CONV_W), F32),
                        pltpu.VMEM((SUBLANES - 1, tm + CONV_HALO - SUBLANES, CONV_W), F32),
                        pltpu.VMEM((SUBLANES - 1, tm + CONV_HALO - SUBLANES, CONV_W), F32),
                        pltpu.VMEM((CONV_HALO, SUBLANES, CONV_W), F32)],
        input_output_aliases={0: 0},
        compiler_params=_params(56, 1),
    )(dproj, a, a, dco, dco, hc, hc, cw, lg, lb)


def _bwd_in_proj(dproj, w_in, x, dh1, g1, rider=None):
    T = x.shape[0]
    tm = 1024

    def body(dp_ref, w_ref, x_ref, dh_ref, g_ref, gx_ref, dg_ref):
        i = pl.program_id(0)
        du = _dot_nt(dp_ref[...], w_ref[...])
        dx, dg_rows = _rms_bwd(du, x_ref[...], g_ref[...])
        gx_ref[...] = dh_ref[...] + dx
        dg = jnp.sum(dg_rows, axis=0, keepdims=True)

        @pl.when(i == 0)
        def _():
            dg_ref[...] = dg

        @pl.when(i > 0)
        def _():
            dg_ref[...] += dg

    row = lambda w: pl.BlockSpec((tm, w), lambda i: (i, 0))
    vec = pl.BlockSpec((1, D_MODEL), lambda i: (0, 0))
    return _call(
        body, rider, name="bwd_in_proj", grid=(T // tm,),
        in_specs=[row(IN_COLS), pl.BlockSpec((D_MODEL, IN_COLS), lambda i: (0, 0)),
                  row(D_MODEL), row(D_MODEL), vec],
        out_specs=[row(D_MODEL), vec],
        out_shape=[jax.ShapeDtypeStruct((T, D_MODEL), F32), jax.ShapeDtypeStruct((1, D_MODEL), F32)],
        compiler_params=_params(56, 1),
    )(dproj, w_in, x, dh1, g1)


def _wgrad(name, a_list, a_spec, b, b_spec, out_spec, out_shape, n_outer, T, tk, select=None, rider=None):
    def body(*refs):
        a_refs, b_ref, o_ref = refs[:len(a_list)], refs[len(a_list)], refs[len(a_list) + 1]
        kt = pl.program_id(1)

        @pl.when(kt == 0)
        def _():
            o_ref[...] = jnp.zeros(o_ref.shape, F32)

        bv = b_ref[...].reshape(b_ref.shape[-2:])
        if select is None:
            o_ref[...] += _dot_tn(a_refs[0][...].reshape(a_refs[0].shape[-2:]), bv).reshape(o_ref.shape)
        else:
            for n, a_ref in enumerate(a_refs):
                @pl.when(select(pl.program_id(0)) == n)
                def _():
                    o_ref[...] += _dot_tn(a_ref[...], bv).reshape(o_ref.shape)

    (res,), got = _call(
        body, rider, name=name, grid=(n_outer, T // tk),
        in_specs=[a_spec] * len(a_list) + [b_spec],
        out_specs=[out_spec], out_shape=[out_shape],
        compiler_params=_params(56, 2),
    )(*a_list, b)
    return (res, got) if rider is not None else res


def _mesh_pos():
    return lax.axis_index("x"), lax.axis_index("y"), lax.axis_index("c")


def _other_chips(x, y):
    return [((1 - x, y), 2 * (1 - x) + y), ((x, 1 - y), 2 * x + (1 - y)), ((1 - x, 1 - y), 2 * (1 - x) + (1 - y))]


def _exchange_rider(operands, out_shape, aliases, sem_shape, pairs):
    def start(ins, outs, sems):
        for send, _ in pairs(ins, outs, *sems):
            send.start()

    def finish(ins, outs, sems):
        for send, recv in pairs(ins, outs, *sems):
            send.wait_send()
            recv.wait_recv()

    sems = [pltpu.SemaphoreType.DMA(sem_shape), pltpu.SemaphoreType.DMA(sem_shape)]
    return _Rider(list(operands), list(out_shape), aliases, sems, start, finish)


def _remote(src, dst, send_sem, recv_sem, device):
    return pltpu.make_async_remote_copy(src_ref=src, dst_ref=dst, send_sem=send_sem, recv_sem=recv_sem,
                                        device_id=device, device_id_type=MESH)


def _fetch_rider(bufs):
    def pairs(ins, outs, send_sems, recv_sems):
        x, y, c = _mesh_pos()
        res = []
        for t, buf in enumerate(bufs):
            rows = pl.ds(c * (buf.shape[1] // 2), buf.shape[1] // 2)
            mine = outs[t].at[2 * x + y, rows]
            for k, (chip, s) in enumerate(_other_chips(x, y)):
                landed = outs[t].at[s, rows]
                res.append((_remote(mine, mine, send_sems.at[t, k], recv_sems.at[t, k], (*chip, c)),
                            _remote(landed, landed, send_sems.at[t, k], recv_sems.at[t, k], (*chip, c))))
        return res

    shapes = [jax.ShapeDtypeStruct(b.shape, b.dtype) for b in bufs]
    return _exchange_rider(bufs, shapes, {t: t for t in range(len(bufs))}, (len(bufs), 3), pairs)


def _forward_rider(bufs):
    def pairs(ins, outs, send_sems, recv_sems):
        x, y, c = _mesh_pos()
        res = []
        for t, buf in enumerate(bufs):
            half = buf.shape[1] // 2
            for k, (_, s) in enumerate(_other_chips(x, y)):
                landed = outs[t].at[s, pl.ds(c * half, half)]
                theirs = outs[t].at[s, pl.ds((1 - c) * half, half)]
                res.append((_remote(landed, landed, send_sems.at[t, k], recv_sems.at[t, k], (x, y, 1 - c)),
                            _remote(theirs, theirs, send_sems.at[t, k], recv_sems.at[t, k], (x, y, 1 - c))))
        return res

    shapes = [jax.ShapeDtypeStruct(b.shape, b.dtype) for b in bufs]
    return _exchange_rider(bufs, shapes, {t: t for t in range(len(bufs))}, (len(bufs), 3), pairs)


def _pair_exchange_rider(grads):
    def pairs(ins, outs, send_sems, recv_sems):
        x, y, c = _mesh_pos()
        res = []
        for t, g in enumerate(grads):
            half = g.shape[1] // 2
            cp = _remote(ins[t].at[:, pl.ds((1 - c) * half, half), :], outs[t], send_sems.at[t], recv_sems.at[t],
                         (x, y, 1 - c))
            res.append((cp, cp))
        return res

    shapes = [jax.ShapeDtypeStruct((N_CHIPS, g.shape[1] // 2, g.shape[2]), F32) for g in grads]
    return _exchange_rider(grads, shapes, {}, (len(grads),), pairs)


def _chip_exchange_rider(sums):
    def pairs(ins, outs, send_sems, recv_sems):
        x, y, c = _mesh_pos()
        res = []
        for t in range(len(sums)):
            for k, (chip, s) in enumerate(_other_chips(x, y)):
                cp = _remote(ins[t].at[s], outs[t].at[k], send_sems.at[t, k], recv_sems.at[t, k], (*chip, c))
                res.append((cp, cp))
        return res

    shapes = [jax.ShapeDtypeStruct((3,) + p.shape[1:], p.dtype) for p in sums]
    return _exchange_rider(sums, shapes, {}, (len(sums), 3), pairs)


def _pair_gather_rider(fulls):
    def pairs(ins, outs, send_sems, recv_sems):
        x, y, c = _mesh_pos()
        res = []
        for t, f in enumerate(fulls):
            half = f.shape[0] // 2
            mine = outs[t].at[pl.ds(c * half, half)]
            theirs = outs[t].at[pl.ds((1 - c) * half, half)]
            res.append((_remote(mine, mine, send_sems.at[t], recv_sems.at[t], (x, y, 1 - c)),
                        _remote(theirs, theirs, send_sems.at[t], recv_sems.at[t], (x, y, 1 - c))))
        return res

    shapes = [jax.ShapeDtypeStruct(f.shape, F32) for f in fulls]
    return _exchange_rider(fulls, shapes, {t: t for t in range(len(fulls))}, (len(fulls),), pairs)


def _alone(name, rider):
    return _call(lambda: None, rider, name=name)()[1]


def _all_reduce_small(pack, rider=None):
    rows = pack.shape[0]

    def body(p_ref, o_ref, buf, send_sems, recv_sems):
        x, y, c = _mesh_pos()
        me = 4 * x + 2 * y + c
        buf[0] = p_ref[...]
        copies = []
        for k in range(1, 8):
            peer = (x ^ (k >> 2), y ^ ((k >> 1) & 1), c ^ (k & 1))
            cp = pltpu.make_async_remote_copy(
                src_ref=p_ref, dst_ref=buf.at[k], send_sem=send_sems.at[k - 1], recv_sem=recv_sems.at[k - 1],
                device_id=peer, device_id_type=MESH)
            cp.start()
            copies.append(cp)
        for cp in copies:
            cp.wait()
        total = buf[me]
        for dev in range(1, 8):
            total = total + buf[me ^ dev]
        o_ref[...] = total

    return _call(
        body, rider, name="all_reduce_small",
        in_specs=[VMEM_FULL], out_specs=[VMEM_FULL],
        out_shape=[jax.ShapeDtypeStruct(pack.shape, F32)],
        scratch_shapes=[pltpu.VMEM((8, rows, LANES), F32),
                        pltpu.SemaphoreType.DMA((7,)), pltpu.SemaphoreType.DMA((7,))],
    )(pack)


def _row_block(rows):
    if rows <= 512:
        return rows
    for rb in (256, 352):
        if rows % rb == 0:
            return rb
    raise ValueError(f"no row block for {rows} rows")


def _place(name, w, pos, dtype):
    R, C = w.shape
    rb = _row_block(R)

    def body(pos_ref, w_ref, o_ref):
        del pos_ref
        o_ref[0] = w_ref[...].astype(dtype)

    return pl.pallas_call(
        body, name=name,
        grid_spec=pltpu.PrefetchScalarGridSpec(
            num_scalar_prefetch=1, grid=(R // rb,),
            in_specs=[pl.BlockSpec((rb, C), lambda r, p: (r, 0))],
            out_specs=pl.BlockSpec((1, rb, C), lambda r, p: (p[0], r, 0))),
        out_shape=(pltpu.HBM if N_CHIPS * R * C * jnp.dtype(dtype).itemsize >= PIN_BYTES
                   else jax.ShapeDtypeStruct)((N_CHIPS, R, C), dtype),
        compiler_params=_params(32, 1),
    )(pos, w)


def _pair_sum(name, g, got, pos):
    S, R, C = g.shape
    half = R // 2
    rb = _row_block(half)
    nh = half // rb

    def body(pos_ref, a_ref, b_ref, o_ref):
        del pos_ref
        o_ref[...] = (a_ref[...] + b_ref[...]).astype(BF16)

    spec = pl.BlockSpec((1, rb, C), lambda s, r, p: (s, r, 0))
    return pl.pallas_call(
        body, name=name,
        grid_spec=pltpu.PrefetchScalarGridSpec(
            num_scalar_prefetch=1, grid=(S, nh),
            in_specs=[pl.BlockSpec((1, rb, C), lambda s, r, p: (s, p[1] * nh + r, 0)), spec],
            out_specs=spec),
        out_shape=jax.ShapeDtypeStruct((S, half, C), BF16), compiler_params=_params(32, 2),
    )(pos, g, got)


def _chip_sum(name, pairs, got, pos):
    _, half, C = pairs.shape
    rb = _row_block(half)
    nh = half // rb

    def body(pos_ref, a_ref, g_ref, o_ref):
        del pos_ref
        o_ref[...] = ((a_ref[0].astype(F32) + g_ref[0].astype(F32)) + g_ref[1].astype(F32)) + g_ref[2].astype(F32)

    return pl.pallas_call(
        body, name=name,
        grid_spec=pltpu.PrefetchScalarGridSpec(
            num_scalar_prefetch=1, grid=(nh,),
            in_specs=[pl.BlockSpec((1, rb, C), lambda r, p: (p[0], r, 0)),
                      pl.BlockSpec((3, rb, C), lambda r, p: (0, r, 0))],
            out_specs=pl.BlockSpec((rb, C), lambda r, p: (p[1] * nh + r, 0))),
        out_shape=jax.ShapeDtypeStruct((2 * half, C), F32), compiler_params=_params(32, 1),
    )(pos, pairs, got)


def _adamw(name, w, g, m, v):
    R, C = w.shape
    rb = _row_block(R)
    c1 = 1.0 - ADAM_B1 ** ADAM_STEP
    c2 = 1.0 - ADAM_B2 ** ADAM_STEP

    def body(w_ref, g_ref, m_ref, v_ref, d_ref, nm_ref, nv_ref):
        gv = g_ref[...]
        nm = ADAM_B1 * m_ref[...] + (1.0 - ADAM_B1) * gv
        nv = ADAM_B2 * v_ref[...] + (1.0 - ADAM_B2) * (gv * gv)
        nm_ref[...] = nm
        nv_ref[...] = nv
        d_ref[...] = -ADAM_LR * ((nm / c1) / (jnp.sqrt(nv / c2) + ADAM_EPS) + ADAM_WD * w_ref[...])

    spec = pl.BlockSpec((rb, C), lambda r: (r, 0))
    sds = jax.ShapeDtypeStruct(w.shape, F32)
    return pl.pallas_call(
        body, name=name, grid=(R // rb,), in_specs=[spec] * 4, out_specs=[spec] * 3,
        out_shape=[sds, sds, sds], compiler_params=_params(40, 1),
    )(w, g, m, v)


def _rel_index():
    m = np.arange(2 * ATT_BLK)
    off = np.where(m < ATT_BLK, m, m - 2 * ATT_BLK)
    rel = np.stack([ATT_BLK * d - off for d in range(N_ATT_TILES)])
    return np.clip(rel, -MAX_REL, MAX_REL) + MAX_REL


def _local_step(x, tgt, g1, w_in, cw, cb, lg, lb, bias, w_out, g2, g3, w_up, fw, fb, w_down, g4, pos=None):
    T = x.shape[0]
    dist = pos is not None
    idx = _rel_index()

    (u, a, qkv), got = _fwd_in_proj(x, g1, w_in, _fetch_rider([w_out, w_down]) if dist else None)
    if dist:
        w_out, w_down = got
    (co, hc), got = _fwd_conv(a, cw, cb, lg, lb, _merge_riders(_forward_rider([w_out, w_down]),
                                                               _fetch_rider([w_up])) if dist else None)
    if dist:
        w_out, w_down, w_up = got
    (ao, lse), got = _fwd_attn(qkv, bias, _forward_rider([w_up]) if dist else None)
    if dist:
        (w_up,) = got
        w_out, w_down = w_out.reshape(D_MODEL, D_MODEL), w_down.reshape(D_FF, D_MODEL)
    mixed, h1, u2 = _fwd_out_proj(co, ao, w_out, x, g2, g3)
    w_cat = _pair_up_weights(w_up)
    hf, pre, act, loss, dy, df, dg4 = _fwd_ffn_loss(u2, w_cat, fw, fb, w_down, h1, tgt, g4)

    tk = min(2048, T)
    du2p, dhf, dfw_g, dfw_v = _bwd_ffn(df, hf, pre, w_cat, fw, w_down)
    gw_up = _wgrad(
        "wgrad_up", [u2], pl.BlockSpec((tk, D_MODEL), lambda s, k: (k, 0)),
        dhf, pl.BlockSpec((1, tk, FF_SHARD), lambda s, k: (s // 2, k, s % 2)),
        pl.BlockSpec((1, D_MODEL, FF_SHARD), lambda s, k: (s, 0, 0)),
        jax.ShapeDtypeStruct((N_CHIPS, D_MODEL, FF_SHARD), F32), N_CHIPS, T, tk)
    gw_down = _wgrad(
        "wgrad_down", [act], pl.BlockSpec((tk, FF_SHARD), lambda s, k: (k, s)),
        df, pl.BlockSpec((tk, D_MODEL), lambda s, k: (k, 0)),
        pl.BlockSpec((FF_SHARD, D_MODEL), lambda s, k: (s, 0)),
        jax.ShapeDtypeStruct((D_FF, D_MODEL), F32), 2, T, tk).reshape(N_CHIPS, D_FF // N_CHIPS, D_MODEL)
    (dh1, dmx, dco, dao, dg3, dg2), _ = _bwd_mid(du2p, dy, h1, mixed, g3, g2, w_out)
    gw_out = _wgrad(
        "wgrad_out", [co, ao], pl.BlockSpec((tk, CONV_W), lambda s, k: (k, 0)),
        dmx, pl.BlockSpec((tk, D_MODEL), lambda s, k: (k, 0)),
        pl.BlockSpec((CONV_W, D_MODEL), lambda s, k: (s, 0)),
        jax.ShapeDtypeStruct((D_MODEL, D_MODEL), F32), 2, T, tk,
        select=lambda s: s, rider=_pair_exchange_rider([gw_up, gw_down]) if dist else None)
    if dist:
        gw_out, got = gw_out
        p_up = _pair_sum("pair_sum_w_up", gw_up, got[0], pos)
        p_down = _pair_sum("pair_sum_w_down", gw_down, got[1], pos)
    gw_out = gw_out.reshape(N_CHIPS, D_MODEL // N_CHIPS, D_MODEL)
    (dproj, dsacc), got = _bwd_attn(
        qkv, ao, dao, lse, bias,
        _merge_riders(_chip_exchange_rider([p_up, p_down]), _pair_exchange_rider([gw_out])) if dist else None)
    if dist:
        gw_up = _chip_sum("chip_sum_w_up", p_up, got[0], pos)
        gw_down = _chip_sum("chip_sum_w_down", p_down, got[1], pos)
        p_out = _pair_sum("pair_sum_w_out", gw_out, got[2], pos)
    (dproj, dcw, dcb, dlg, dlb), got = _bwd_conv(
        dproj, a, dco, hc, cw, lg, lb,
        _merge_riders(_pair_gather_rider([gw_up, gw_down]), _chip_exchange_rider([p_out])) if dist else None)
    if dist:
        gw_up, gw_down = got[:2]
        gw_out = _chip_sum("chip_sum_w_out", p_out, got[2], pos)
    gw_in = _wgrad(
        "wgrad_in", [u], pl.BlockSpec((tk, D_MODEL), lambda s, k: (k, 0)),
        dproj, pl.BlockSpec((tk, IN_SHARD), lambda s, k: (k, s)),
        pl.BlockSpec((1, D_MODEL, IN_SHARD), lambda s, k: (s, 0, 0)),
        jax.ShapeDtypeStruct((N_CHIPS, D_MODEL, IN_SHARD), F32), N_CHIPS, T, tk)
    if dist:
        got = _alone("pair_exchange_w_in", _merge_riders(_pair_exchange_rider([gw_in]), _pair_gather_rider([gw_out])))
        p_in, gw_out = _pair_sum("pair_sum_w_in", gw_in, got[0], pos), got[1]
    (gx, dg1), _ = _bwd_in_proj(dproj, w_in, x, dh1, g1)
    (diag,), got = _diag_sums(dsacc, _chip_exchange_rider([p_in]) if dist else None)
    if dist:
        gw_in = _chip_sum("chip_sum_w_in", p_in, got[0], pos)

    diag = diag.reshape(N_ATT_TILES, N_HEADS, 2 * ATT_BLK)
    onehot = np.zeros((N_ATT_TILES, 2 * ATT_BLK, 2 * MAX_REL + 1), np.float32)
    for d in range(N_ATT_TILES):
        onehot[d, np.arange(2 * ATT_BLK), idx[d]] = 1.0
    drel = jnp.einsum("dhm,dmr->hr", diag, jnp.asarray(onehot), precision=lax.Precision.HIGHEST)

    small = dict(norm_mix_pre=dg1, conv_dw_w=dcw[:CONV_K], conv_dw_b=dcb, conv_ln_g=dlg, conv_ln_b=dlb,
                 rel_bias=drel, norm_mix_post=dg2, norm_ffn_pre=dg3,
                 ffn_dw_w=jnp.concatenate([dfw_g[0, :3], dfw_g[1, :3], dfw_v[0, :3], dfw_v[1, :3]], axis=1),
                 ffn_dw_b=jnp.concatenate([dfw_g[0, 3:4], dfw_g[1, 3:4], dfw_v[0, 3:4], dfw_v[1, 3:4]], axis=1),
                 norm_ffn_post=dg4)
    return loss, gx, small, dict(w_in=gw_in, w_out=gw_out, w_up=gw_up, w_down=gw_down)


SMALL_ORDER = ["norm_mix_pre", "conv_dw_b", "conv_ln_g", "conv_ln_b", "rel_bias", "norm_mix_post",
               "norm_ffn_pre", "ffn_dw_b", "norm_ffn_post", "conv_dw_w", "ffn_dw_w"]


def _pack(parts):
    rows = []
    for p in parts:
        width = -(-p.shape[1] // LANES) * LANES
        rows.append(jnp.pad(p, ((0, 0), (0, width - p.shape[1]))).reshape(-1, LANES))
    packed = jnp.concatenate(rows, axis=0)
    pad = -packed.shape[0] % 8
    return jnp.pad(packed, ((0, pad), (0, 0)))


def _unpack(packed, shapes):
    out, r = [], 0
    for shp in shapes:
        width = -(-shp[1] // LANES) * LANES
        n = shp[0] * width // LANES
        out.append(packed[r:r + n].reshape(shp[0], width)[:, :shp[1]])
        r += n
    return out


WEIGHTS = ["norm_mix_pre", "w_in", "conv_dw_w", "conv_dw_b", "conv_ln_g", "conv_ln_b", "rel_bias", "w_out",
           "norm_mix_post", "norm_ffn_pre", "w_up", "ffn_dw_w", "ffn_dw_b", "w_down", "norm_ffn_post"]
BIG = ["w_in", "w_out", "w_up", "w_down"]


def kernel(x, norm_mix_pre, w_in, conv_dw_w, conv_dw_b, conv_ln_g, conv_ln_b, rel_bias, w_out, norm_mix_post, norm_ffn_pre, w_up, ffn_dw_w, ffn_dw_b, w_down, norm_ffn_post, loss_target, m_norm_mix_pre, m_w_in, m_conv_dw_w, m_conv_dw_b, m_conv_ln_g, m_conv_ln_b, m_rel_bias, m_w_out, m_norm_mix_post, m_norm_ffn_pre, m_w_up, m_ffn_dw_w, m_ffn_dw_b, m_w_down, m_norm_ffn_post, v_norm_mix_pre, v_w_in, v_conv_dw_w, v_conv_dw_b, v_conv_ln_g, v_conv_ln_b, v_rel_bias, v_w_out, v_norm_mix_post, v_norm_ffn_pre, v_w_up, v_ffn_dw_w, v_ffn_dw_b, v_w_down, v_norm_ffn_post):
    args = locals()
    w = {n: args[n][0] for n in WEIGHTS}
    m = {n: args["m_" + n][0] for n in WEIGHTS}
    v = {n: args["v_" + n][0] for n in WEIGHTS}
    for d in (w, m, v):
        d["rel_bias"] = d["rel_bias"].reshape(N_HEADS, 2 * MAX_REL + 1)
        for n in ("norm_mix_pre", "conv_dw_b", "conv_ln_g", "conv_ln_b", "norm_mix_post", "norm_ffn_pre",
                  "ffn_dw_b", "norm_ffn_post"):
            d[n] = d[n].reshape(1, -1)
    shard = 2 * lax.axis_index("x") + lax.axis_index("y")

    cw_sh = jnp.pad(w["conv_dw_w"], ((0, CONV_HALO - CONV_K), (0, 0)))
    fw_sh = jnp.pad(w["ffn_dw_w"], ((0, FF_HALO - 3), (0, 0)))
    pos = jnp.stack([shard, lax.axis_index("c")]).astype(jnp.int32)
    bufs = {n: _place("place_" + n, w[n], pos, BF16) for n in BIG}
    first = [bufs["w_in"], _place("place_conv_dw_w", cw_sh, pos, F32), _place("place_ffn_dw_w", fw_sh, pos, F32)]
    (bias,), first = _bias_tiles(w["rel_bias"], _fetch_rider(first))
    w_in_f, cw_f, fw_f = _alone("all_gather_forward", _forward_rider(list(first)))
    cw_full = jnp.transpose(cw_f, (1, 0, 2)).reshape(CONV_HALO, CONV_W)

    loss, gx, small, big = _local_step(
        x[0], loss_target[0], w["norm_mix_pre"], _join_columns(w_in_f), cw_full, w["conv_dw_b"], w["conv_ln_g"],
        w["conv_ln_b"], bias, bufs["w_out"], w["norm_mix_post"],
        w["norm_ffn_pre"], bufs["w_up"], fw_f, w["ffn_dw_b"].reshape(N_CHIPS, 1, FF_SHARD),
        bufs["w_down"], w["norm_ffn_post"], pos)
    (gsum,), (big["w_in"],) = _all_reduce_small(_pack([small[n] for n in SMALL_ORDER] + [loss]),
                                                _pair_gather_rider([big["w_in"]]))

    grads, deltas, new_m, new_v = {}, {}, {}, {}
    for n in BIG:
        grads[n] = big[n]
        deltas[n], new_m[n], new_v[n] = _adamw("adamw_" + n, w[n], big[n], m[n], v[n])
    shapes = [small[n].shape for n in SMALL_ORDER]
    *reduced, total = _unpack(gsum, shapes + [loss.shape])
    gs = dict(zip(SMALL_ORDER, reduced))
    gs["conv_dw_w"] = lax.dynamic_slice_in_dim(gs["conv_dw_w"], shard * LANES, LANES, axis=1)
    gs["ffn_dw_w"] = lax.dynamic_slice_in_dim(gs["ffn_dw_w"], shard * FF_SHARD, FF_SHARD, axis=1)
    shapes = [gs[n].shape for n in SMALL_ORDER]
    d_p, m_p, v_p = _adamw("adamw_small", _pack([w[n] for n in SMALL_ORDER]), _pack([gs[n] for n in SMALL_ORDER]),
                           _pack([m[n] for n in SMALL_ORDER]), _pack([v[n] for n in SMALL_ORDER]))
    for dst, packed in ((deltas, d_p), (new_m, m_p), (new_v, v_p)):
        dst.update(zip(SMALL_ORDER, _unpack(packed, shapes)))
    grads.update(gs)

    outs = [total[0, 0], gx[None]]
    for group in (grads, deltas, new_m, new_v):
        outs += [group[n].reshape(args[n].shape) for n in WEIGHTS]
    return tuple(outs)
```
